```python
import math
import jax, jax.numpy as jnp
from jax import lax
import numpy as np

D_MODEL = 2048
BATCH = 8
SEQ = 8192
DEPTH = 1

POOL_WINDOWS = (2, 4, 8, 16)
POOL_GROUPS = len(POOL_WINDOWS)
POOL_GROUP_WIDTH = D_MODEL // 8
POOL_WIDTH = POOL_GROUPS * POOL_GROUP_WIDTH

ATTN_GROUPS = ((128, 1), (512, 4), (2048, 16))
HEADS_PER_GROUP = 4
N_ATTN_HEADS = HEADS_PER_GROUP * len(ATTN_GROUPS)
HEAD_DIM = 128
ATTN_WIDTH = N_ATTN_HEADS * HEAD_DIM
ATTN_OUT_WIDTH = HEADS_PER_GROUP * HEAD_DIM

N_BRANCHES = 2
IN_WIDTH = POOL_WIDTH + 3 * ATTN_WIDTH + N_BRANCHES * D_MODEL

D_FF = 5632
CONV_WIDTH = 3

RMS_EPS = 1e-6

kernel_name = "hybrid_pool_dilated_alibi_convffn_block"


def alibi_slopes(n_heads):
    return np.array([2.0 ** (-8.0 * (h + 1) / n_heads) for h in range(n_heads)], dtype=np.float32)


def rms_norm(x, g):
    xf = x.astype(jnp.float32)
    y = xf * lax.rsqrt(jnp.mean(xf * xf, axis=-1, keepdims=True) + RMS_EPS) * g.astype(jnp.float32)
    return y.astype(x.dtype)


def pool_mixer(u, w_lin, scale):
    B, S, _ = u.shape
    uf = u.astype(jnp.float32).reshape(B, S, POOL_GROUPS, POOL_GROUP_WIDTH)
    t = jnp.arange(S)
    outs = []
    for gi, w in enumerate(POOL_WINDOWS):
        ug = uf[:, :, gi]
        cs = jnp.cumsum(ug, axis=1)
        lag = jnp.pad(cs, ((0, 0), (w, 0), (0, 0)))[:, :S]
        cnt = jnp.minimum(t + 1, w).astype(jnp.float32)[None, :, None]
        outs.append((cs - lag) / cnt - ug)
    pooled = jnp.stack(outs, axis=2)
    y = jnp.einsum('bsgc,gce->bsge', pooled, w_lin.astype(jnp.float32))
    y = y.reshape(B, S, POOL_WIDTH) * scale.astype(jnp.float32)
    return y.astype(u.dtype)


def dilated_group_attention(q, k, v, slopes, window, dilation):
    B, S, H, Dh = q.shape
    span = window // dilation
    L = S // dilation
    nb = -(-L // span)
    Lp = nb * span
    N = B * dilation

    def to_sub(a):
        a = a.reshape(B, L, dilation, H, Dh).transpose(0, 2, 1, 3, 4).reshape(N, L, H, Dh)
        a = jnp.pad(a, ((0, 0), (0, Lp - L), (0, 0), (0, 0)))
        return a.reshape(N, nb, span, H, Dh)

    def with_prev(a):
        prev = jnp.pad(a, ((0, 0), (1, 0), (0, 0), (0, 0), (0, 0)))[:, :nb]
        return jnp.concatenate([prev, a], axis=2)

    qb = to_sub(q)
    kk = with_prev(to_sub(k))
    vv = with_prev(to_sub(v))

    s = jnp.einsum('nbqhd,nbkhd->nbhqk', qb, kk, preferred_element_type=jnp.float32) * (Dh ** -0.5)
    qi = jnp.arange(span)[:, None] + span
    ki = jnp.arange(2 * span)[None, :]
    j = qi - ki
    key_abs = (jnp.arange(nb) * span)[:, None] - span + jnp.arange(2 * span)[None, :]
    valid = ((j >= 0) & (j <= span))[None] & (key_abs >= 0)[:, None, :]
    bias = -slopes[:, None, None] * (j * dilation).astype(jnp.float32)[None]
    s = jnp.where(valid[None, :, None], s + bias[None, None], -jnp.inf)
    m = jnp.max(s, axis=-1, keepdims=True)
    p = jnp.exp(s - m)
    l = jnp.sum(p, axis=-1, keepdims=True)
    o = jnp.einsum('nbhqk,nbkhd->nbqhd', p, vv.astype(jnp.float32)) / jnp.swapaxes(l, 2, 3)
    lse = jnp.swapaxes((m + jnp.log(l))[..., 0], 2, 3)

    o = o.reshape(N, Lp, H, Dh)[:, :L].reshape(B, dilation, L, H, Dh).transpose(0, 2, 1, 3, 4).reshape(B, S, H, Dh)
    lse = lse.reshape(N, Lp, H)[:, :L].reshape(B, dilation, L, H).transpose(0, 2, 1, 3).reshape(B, S, H)
    return o, lse


def dilated_attention_mixer(q, k, v):
    B, S = q.shape[:2]
    slopes = jnp.asarray(alibi_slopes(N_ATTN_HEADS))
    outs, lses = [], []
    for gi, (window, dilation) in enumerate(ATTN_GROUPS):
        hs = slice(gi * HEADS_PER_GROUP, (gi + 1) * HEADS_PER_GROUP)
        o, lse = dilated_group_attention(q[:, :, hs], k[:, :, hs], v[:, :, hs], slopes[hs], window, dilation)
        outs.append(o)
        lses.append(lse)
    wts = jax.nn.softmax(jnp.stack(lses, axis=0), axis=0)
    y = jnp.sum(wts[..., None] * jnp.stack(outs, axis=0), axis=0)
    return y.reshape(B, S, ATTN_OUT_WIDTH).astype(q.dtype)


def causal_dwconv(u, w, b):
    S = u.shape[1]
    up = jnp.pad(u, ((0, 0), (CONV_WIDTH - 1, 0), (0, 0)))
    y = b
    for i in range(CONV_WIDTH):
        y = y + w[i] * up[:, i:i + S]
    return y


def _fwd_setup_inputs(seed: int = 0) -> dict:
    key = jax.random.key(seed)
    ks = jax.random.split(key, 16)
    f32 = jnp.float32
    nrm = lambda k, shape, fan: jax.random.normal(k, shape, f32) * (fan ** -0.5)
    return {
        "x": jax.random.normal(ks[0], (BATCH, SEQ, D_MODEL), f32),
        "g_mix": 1.0 + 0.02 * jax.random.normal(ks[1], (DEPTH, D_MODEL), f32),
        "w_in": nrm(ks[2], (DEPTH, D_MODEL, IN_WIDTH), D_MODEL),
        "b_gate": 0.1 * jax.random.normal(ks[3], (DEPTH, N_BRANCHES * D_MODEL), f32),
        "w_pool_lin": nrm(ks[4], (DEPTH, POOL_GROUPS, POOL_GROUP_WIDTH, POOL_GROUP_WIDTH), POOL_GROUP_WIDTH),
        "pool_scale": 1.0 + 0.02 * jax.random.normal(ks[5], (DEPTH, POOL_WIDTH), f32),
        "w_pool_out": nrm(ks[6], (DEPTH, POOL_WIDTH, D_MODEL), POOL_WIDTH),
        "w_attn_out": nrm(ks[7], (DEPTH, ATTN_OUT_WIDTH, D_MODEL), ATTN_OUT_WIDTH),
        "w_out": nrm(ks[8], (DEPTH, D_MODEL, D_MODEL), D_MODEL),
        "g_ffn": 1.0 + 0.02 * jax.random.normal(ks[9], (DEPTH, D_MODEL), f32),
        "w_up": nrm(ks[10], (DEPTH, D_MODEL, 2 * D_FF), D_MODEL),
        "conv_w": nrm(ks[11], (DEPTH, CONV_WIDTH, 2 * D_FF), CONV_WIDTH),
        "conv_b": 0.02 * jax.random.normal(ks[12], (DEPTH, 2 * D_FF), f32),
        "w_down": nrm(ks[13], (DEPTH, D_FF, D_MODEL), D_FF),
        "g_final": 1.0 + 0.02 * jax.random.normal(ks[14], (D_MODEL,), f32),
    }


def _fwd_reference(x, g_mix, w_in, b_gate, w_pool_lin, pool_scale, w_pool_out, w_attn_out, w_out,
              g_ffn, w_up, conv_w, conv_b, w_down, g_final):
    B, S, _ = x.shape
    o_q = POOL_WIDTH
    o_k = o_q + ATTN_WIDTH
    o_v = o_k + ATTN_WIDTH
    o_g = o_v + ATTN_WIDTH
    for l in range(DEPTH):
        h = rms_norm(x, g_mix[l])
        proj = h @ w_in[l]
        u = proj[..., :o_q]
        q = proj[..., o_q:o_k].reshape(B, S, N_ATTN_HEADS, HEAD_DIM)
        k = proj[..., o_k:o_v].reshape(B, S, N_ATTN_HEADS, HEAD_DIM)
        v = proj[..., o_v:o_g].reshape(B, S, N_ATTN_HEADS, HEAD_DIM)
        gates = jax.nn.sigmoid(proj[..., o_g:] + b_gate[l]).reshape(B, S, N_BRANCHES, D_MODEL)

        y_pool = pool_mixer(u, w_pool_lin[l], pool_scale[l]) @ w_pool_out[l]
        y_attn = dilated_attention_mixer(q, k, v) @ w_attn_out[l]
        mixed = gates[:, :, 0] * y_pool + gates[:, :, 1] * y_attn
        x = x + mixed @ w_out[l]

        h = rms_norm(x, g_ffn[l])
        up = causal_dwconv(h @ w_up[l], conv_w[l], conv_b[l])
        a, b = up[..., :D_FF], up[..., D_FF:]
        x = x + (jax.nn.gelu(a, approximate=False) * b) @ w_down[l]
    return rms_norm(x, g_final)


import jax as _jax
import jax.numpy as _jnp

TWIN_FORMAT = 'train_step'
FWD_PARAMS = ['x', 'g_mix', 'w_in', 'b_gate', 'w_pool_lin', 'pool_scale', 'w_pool_out', 'w_attn_out', 'w_out', 'g_ffn', 'w_up', 'conv_w', 'conv_b', 'w_down', 'g_final']
TWIN_WEIGHTS = ['g_mix', 'w_in', 'b_gate', 'w_pool_lin', 'pool_scale', 'w_pool_out', 'w_attn_out', 'w_out', 'g_ffn', 'w_up', 'conv_w', 'conv_b', 'w_down', 'g_final']
TWIN_DIFF_INPUT = 'x'
TWIN_INPUTS = ['x', 'g_mix', 'w_in', 'b_gate', 'w_pool_lin', 'pool_scale', 'w_pool_out', 'w_attn_out', 'w_out', 'g_ffn', 'w_up', 'conv_w', 'conv_b', 'w_down', 'g_final', 'loss_target', 'm_g_mix', 'm_w_in', 'm_b_gate', 'm_w_pool_lin', 'm_pool_scale', 'm_w_pool_out', 'm_w_attn_out', 'm_w_out', 'm_g_ffn', 'm_w_up', 'm_conv_w', 'm_conv_b', 'm_w_down', 'm_g_final', 'v_g_mix', 'v_w_in', 'v_b_gate', 'v_w_pool_lin', 'v_pool_scale', 'v_w_pool_out', 'v_w_attn_out', 'v_w_out', 'v_g_ffn', 'v_w_up', 'v_conv_w', 'v_conv_b', 'v_w_down', 'v_g_final']
TWIN_OUTPUTS = ['loss', 'grad_x', 'grad_g_mix', 'grad_w_in', 'grad_b_gate', 'grad_w_pool_lin', 'grad_pool_scale', 'grad_w_pool_out', 'grad_w_attn_out', 'grad_w_out', 'grad_g_ffn', 'grad_w_up', 'grad_conv_w', 'grad_conv_b', 'grad_w_down', 'grad_g_final', 'delta_g_mix', 'delta_w_in', 'delta_b_gate', 'delta_w_pool_lin', 'delta_pool_scale', 'delta_w_pool_out', 'delta_w_attn_out', 'delta_w_out', 'delta_g_ffn', 'delta_w_up', 'delta_conv_w', 'delta_conv_b', 'delta_w_down', 'delta_g_final', 'new_m_g_mix', 'new_m_w_in', 'new_m_b_gate', 'new_m_w_pool_lin', 'new_m_pool_scale', 'new_m_w_pool_out', 'new_m_w_attn_out', 'new_m_w_out', 'new_m_g_ffn', 'new_m_w_up', 'new_m_conv_w', 'new_m_conv_b', 'new_m_w_down', 'new_m_g_final', 'new_v_g_mix', 'new_v_w_in', 'new_v_b_gate', 'new_v_w_pool_lin', 'new_v_pool_scale', 'new_v_w_pool_out', 'new_v_w_attn_out', 'new_v_w_out', 'new_v_g_ffn', 'new_v_w_up', 'new_v_conv_w', 'new_v_conv_b', 'new_v_w_down', 'new_v_g_final']
TWIN_LEAF_KINDS = {'loss': 'loss', 'grad_x': 'grad_x', 'grad_g_mix': 'grad_w', 'grad_w_in': 'grad_w', 'grad_b_gate': 'grad_w', 'grad_w_pool_lin': 'grad_w', 'grad_pool_scale': 'grad_w', 'grad_w_pool_out': 'grad_w', 'grad_w_attn_out': 'grad_w', 'grad_w_out': 'grad_w', 'grad_g_ffn': 'grad_w', 'grad_w_up': 'grad_w', 'grad_conv_w': 'grad_w', 'grad_conv_b': 'grad_w', 'grad_w_down': 'grad_w', 'grad_g_final': 'grad_w', 'delta_g_mix': 'delta_w', 'delta_w_in': 'delta_w', 'delta_b_gate': 'delta_w', 'delta_w_pool_lin': 'delta_w', 'delta_pool_scale': 'delta_w', 'delta_w_pool_out': 'delta_w', 'delta_w_attn_out': 'delta_w', 'delta_w_out': 'delta_w', 'delta_g_ffn': 'delta_w', 'delta_w_up': 'delta_w', 'delta_conv_w': 'delta_w', 'delta_conv_b': 'delta_w', 'delta_w_down': 'delta_w', 'delta_g_final': 'delta_w', 'new_m_g_mix': 'new_m', 'new_m_w_in': 'new_m', 'new_m_b_gate': 'new_m', 'new_m_w_pool_lin': 'new_m', 'new_m_pool_scale': 'new_m', 'new_m_w_pool_out': 'new_m', 'new_m_w_attn_out': 'new_m', 'new_m_w_out': 'new_m', 'new_m_g_ffn': 'new_m', 'new_m_w_up': 'new_m', 'new_m_conv_w': 'new_m', 'new_m_conv_b': 'new_m', 'new_m_w_down': 'new_m', 'new_m_g_final': 'new_m', 'new_v_g_mix': 'new_v', 'new_v_w_in': 'new_v', 'new_v_b_gate': 'new_v', 'new_v_w_pool_lin': 'new_v', 'new_v_pool_scale': 'new_v', 'new_v_w_pool_out': 'new_v', 'new_v_w_attn_out': 'new_v', 'new_v_w_out': 'new_v', 'new_v_g_ffn': 'new_v', 'new_v_w_up': 'new_v', 'new_v_conv_w': 'new_v', 'new_v_conv_b': 'new_v', 'new_v_w_down': 'new_v', 'new_v_g_final': 'new_v'}


def _forward(args):
    return _fwd_reference(*[args[k] for k in FWD_PARAMS])


def _output_shape():
    def fwd():
        inp = _fwd_setup_inputs(0)
        return _fwd_reference(*[inp[k] for k in FWD_PARAMS])
    out = _jax.eval_shape(fwd)
    return out.shape, out.dtype

N_MICROBATCH = 1
ADAM_LR = 0.001
ADAM_B1 = 0.9
ADAM_B2 = 0.999
ADAM_EPS = 1e-08
ADAM_WD = 0.01
ADAM_STEP = 10
PER_EXAMPLE_BATCH_AXIS = {'x': 0, 'loss_target': 0}
SHARED_INPUTS = []
_WEIGHT_DTYPES = {'g_mix': _jnp.float32, 'w_in': _jnp.float32, 'b_gate': _jnp.float32, 'w_pool_lin': _jnp.float32, 'pool_scale': _jnp.float32, 'w_pool_out': _jnp.float32, 'w_attn_out': _jnp.float32, 'w_out': _jnp.float32, 'g_ffn': _jnp.float32, 'w_up': _jnp.float32, 'conv_w': _jnp.float32, 'conv_b': _jnp.float32, 'w_down': _jnp.float32, 'g_final': _jnp.float32}
MOMENT_SCALE = {'g_mix': 7.220909e-02, 'w_in': 3.352007e-02, 'b_gate': 1.761402e-02, 'w_pool_lin': 8.571459e-02, 'pool_scale': 8.659339e-02, 'w_pool_out': 6.047193e-02, 'w_attn_out': 2.234437e-02, 'w_out': 6.366478e-02, 'g_ffn': 9.628240e-02, 'w_up': 3.971690e-02, 'conv_w': 3.943356e-02, 'conv_b': 3.862026e-02, 'w_down': 6.490204e-02, 'g_final': 3.195700e+01}


def _to_microbatches(a, axis):
    t = _jnp.moveaxis(a, axis, 0)
    t = t.reshape((N_MICROBATCH, t.shape[0] // N_MICROBATCH) + t.shape[1:])
    return _jnp.moveaxis(t, 1, axis + 1)


def setup_inputs(seed: int = 0) -> dict:
    inp = _fwd_setup_inputs(seed)
    key = _jax.random.fold_in(_jax.random.key(seed), 7919)
    shape, _ = _output_shape()
    out = dict(inp)
    out["loss_target"] = _jax.random.normal(_jax.random.fold_in(key, 0), shape, _jnp.float32)
    for i, name in enumerate(TWIN_WEIGHTS):
        w = inp[name].astype(_jnp.float32)
        if MOMENT_SCALE is None:
            s = _jnp.sqrt(_jnp.mean(_jnp.square(w)) + 1e-30)
        else:
            s = MOMENT_SCALE[name]
        km, kv = _jax.random.split(_jax.random.fold_in(key, i + 1))
        out[name] = w
        out["m_" + name] = s * _jax.random.normal(km, w.shape, _jnp.float32)
        out["v_" + name] = (s * s) * _jax.random.uniform(kv, w.shape, _jnp.float32, 0.5, 1.5)
    if N_MICROBATCH > 1:
        for name, axis in PER_EXAMPLE_BATCH_AXIS.items():
            out[name] = _to_microbatches(out[name], axis)
    return {'x': out['x'], 'g_mix': out['g_mix'], 'w_in': out['w_in'], 'b_gate': out['b_gate'], 'w_pool_lin': out['w_pool_lin'], 'pool_scale': out['pool_scale'], 'w_pool_out': out['w_pool_out'], 'w_attn_out': out['w_attn_out'], 'w_out': out['w_out'], 'g_ffn': out['g_ffn'], 'w_up': out['w_up'], 'conv_w': out['conv_w'], 'conv_b': out['conv_b'], 'w_down': out['w_down'], 'g_final': out['g_final'], 'loss_target': out['loss_target'], 'm_g_mix': out['m_g_mix'], 'm_w_in': out['m_w_in'], 'm_b_gate': out['m_b_gate'], 'm_w_pool_lin': out['m_w_pool_lin'], 'm_pool_scale': out['m_pool_scale'], 'm_w_pool_out': out['m_w_pool_out'], 'm_w_attn_out': out['m_w_attn_out'], 'm_w_out': out['m_w_out'], 'm_g_ffn': out['m_g_ffn'], 'm_w_up': out['m_w_up'], 'm_conv_w': out['m_conv_w'], 'm_conv_b': out['m_conv_b'], 'm_w_down': out['m_w_down'], 'm_g_final': out['m_g_final'], 'v_g_mix': out['v_g_mix'], 'v_w_in': out['v_w_in'], 'v_b_gate': out['v_b_gate'], 'v_w_pool_lin': out['v_w_pool_lin'], 'v_pool_scale': out['v_pool_scale'], 'v_w_pool_out': out['v_w_pool_out'], 'v_w_attn_out': out['v_w_attn_out'], 'v_w_out': out['v_w_out'], 'v_g_ffn': out['v_g_ffn'], 'v_w_up': out['v_w_up'], 'v_conv_w': out['v_conv_w'], 'v_conv_b': out['v_conv_b'], 'v_w_down': out['v_w_down'], 'v_g_final': out['v_g_final']}


def _loss(weights, diff, rest, loss_target):
    with _jax.named_scope("forward"):
        args = {**rest, TWIN_DIFF_INPUT: diff, **{k: w.astype(_WEIGHT_DTYPES[k]) for k, w in weights.items()}}
        y = _forward(args)
    with _jax.named_scope("loss_head"):
        err = _jnp.square(y.astype(_jnp.float32) - loss_target)
        return 0.5 * _jnp.sum(_jnp.mean(err, axis=-1)) if err.ndim else 0.5 * err


def _adamw(w, g, m, v):
    m = ADAM_B1 * m + (1.0 - ADAM_B1) * g
    v = ADAM_B2 * v + (1.0 - ADAM_B2) * _jnp.square(g)
    m_hat = m / (1.0 - ADAM_B1 ** ADAM_STEP)
    v_hat = v / (1.0 - ADAM_B2 ** ADAM_STEP)
    delta = -ADAM_LR * (m_hat / (_jnp.sqrt(v_hat) + ADAM_EPS) + ADAM_WD * w)
    return delta, m, v


def reference(x, g_mix, w_in, b_gate, w_pool_lin, pool_scale, w_pool_out, w_attn_out, w_out, g_ffn, w_up, conv_w, conv_b, w_down, g_final, loss_target, m_g_mix, m_w_in, m_b_gate, m_w_pool_lin, m_pool_scale, m_w_pool_out, m_w_attn_out, m_w_out, m_g_ffn, m_w_up, m_conv_w, m_conv_b, m_w_down, m_g_final, v_g_mix, v_w_in, v_b_gate, v_w_pool_lin, v_pool_scale, v_w_pool_out, v_w_attn_out, v_w_out, v_g_ffn, v_w_up, v_conv_w, v_conv_b, v_w_down, v_g_final):
    given = dict(x=x, g_mix=g_mix, w_in=w_in, b_gate=b_gate, w_pool_lin=w_pool_lin, pool_scale=pool_scale, w_pool_out=w_pool_out, w_attn_out=w_attn_out, w_out=w_out, g_ffn=g_ffn, w_up=w_up, conv_w=conv_w, conv_b=conv_b, w_down=w_down, g_final=g_final, loss_target=loss_target, m_g_mix=m_g_mix, m_w_in=m_w_in, m_b_gate=m_b_gate, m_w_pool_lin=m_w_pool_lin, m_pool_scale=m_pool_scale, m_w_pool_out=m_w_pool_out, m_w_attn_out=m_w_attn_out, m_w_out=m_w_out, m_g_ffn=m_g_ffn, m_w_up=m_w_up, m_conv_w=m_conv_w, m_conv_b=m_conv_b, m_w_down=m_w_down, m_g_final=m_g_final, v_g_mix=v_g_mix, v_w_in=v_w_in, v_b_gate=v_b_gate, v_w_pool_lin=v_w_pool_lin, v_pool_scale=v_pool_scale, v_w_pool_out=v_w_pool_out, v_w_attn_out=v_w_attn_out, v_w_out=v_w_out, v_g_ffn=v_g_ffn, v_w_up=v_w_up, v_conv_w=v_conv_w, v_conv_b=v_conv_b, v_w_down=v_w_down, v_g_final=v_g_final)
    weights = {n: given[n] for n in TWIN_WEIGHTS}
    shared = {n: given[n] for n in SHARED_INPUTS}
    per_example = {n: given[n] for n in ['x']}
    grad_fn = _jax.value_and_grad(_loss, argnums=(0, 1))

    def one_microbatch(ex, loss_target):
        ex = dict(ex)
        diff = ex.pop(TWIN_DIFF_INPUT)
        return grad_fn(weights, diff, {**shared, **ex}, loss_target)

    if N_MICROBATCH == 1:
        loss, (grad_w, grad_x) = one_microbatch(per_example, given["loss_target"])
    else:
        def body(carry, xs):
            loss_sum, grad_sum = carry
            l_k, (gw_k, gx_k) = one_microbatch(xs[0], xs[1])
            with _jax.named_scope("update"):
                return (loss_sum + l_k, _jax.tree.map(_jnp.add, grad_sum, gw_k)), gx_k

        init = (_jnp.zeros((), _jnp.float32), _jax.tree.map(_jnp.zeros_like, weights))
        (loss, grad_w), grad_x = _jax.lax.scan(body, init, (per_example, given["loss_target"]))
    with _jax.named_scope("update"):
        delta_w, new_m, new_v = {}, {}, {}
        for n in TWIN_WEIGHTS:
            delta_w[n], new_m[n], new_v[n] = _adamw(weights[n], grad_w[n], given["m_" + n], given["v_" + n])
    return (loss, grad_x, *[grad_w[n] for n in TWIN_WEIGHTS], *[delta_w[n] for n in TWIN_WEIGHTS],
            *[new_m[n] for n in TWIN_WEIGHTS], *[new_v[n] for n in TWIN_WEIGHTS])
```

```python
import functools
import math

import jax
import jax.numpy as jnp
from jax import lax
from jax.experimental import pallas as pl
from jax.experimental.pallas import tpu as pltpu

F32 = jnp.float32
BF16 = jnp.bfloat16

RMS_EPS = 1e-6
POOL_WINDOWS = (2, 4, 8, 16)
ATTN_GROUPS = ((128, 1), (512, 4), (2048, 16))
HEADS_PER_GROUP = 4
HEAD_DIM = 128
N_ATTN_HEADS = HEADS_PER_GROUP * len(ATTN_GROUPS)
SPAN = 128
GROUP_WIDTH = HEADS_PER_GROUP * HEAD_DIM
ATTN_WIDTH = N_ATTN_HEADS * HEAD_DIM
ATTN_SCALE = HEAD_DIM ** -0.5
NEG_BIG = -1e30
ALIBI_SLOPES = tuple(2.0 ** (-8.0 * (h + 1) / N_ATTN_HEADS) for h in range(N_ATTN_HEADS))

ADAM_LR = 0.001
ADAM_B1 = 0.9
ADAM_B2 = 0.999
ADAM_EPS = 1e-08
ADAM_WD = 0.01
ADAM_STEP = 10

INV_SQRT2 = 1.0 / math.sqrt(2.0)
INV_SQRT_2PI = 1.0 / math.sqrt(2.0 * math.pi)

HALO = 16
VMEM_LIMIT = 56 * 1024 * 1024
N_CHIPS = 4
N_DEV = 8
MESH = pl.DeviceIdType.MESH
ANY = pl.BlockSpec(memory_space=pl.ANY)


def _cparams(*sem):
    return pltpu.CompilerParams(dimension_semantics=sem, vmem_limit_bytes=VMEM_LIMIT)


def _tile(n, pref, mult=128):
    t = (min(pref, n) // mult) * mult
    while t >= mult:
        if n % t == 0:
            return t
        t -= mult
    return n


def _dot(a, b, contract):
    return lax.dot_general(a, b, (contract, ((), ())), preferred_element_type=F32)


def _dot_nn(a, b):
    return _dot(a, b, ((1,), (0,)))


def _dot_nt(a, b):
    return _dot(a, b, ((1,), (1,)))


def _mm(a, b, *, mode, dims, name, tiles=None, out_dtypes=(BF16,), epilogue=None, extras=(), a_off=(0, 0), b_off=(0, 0)):
    M, N, K = dims
    if tiles is None:
        tiles = (_tile(M, 1024), _tile(N, 512), _tile(K, 2048))
    tm, tn, tk = tiles
    assert M % tm == 0 and N % tn == 0 and K % tk == 0, (name, dims, tiles)
    nk = K // tk
    if mode == "nn":
        ab, bb, contract = (tm, tk), (tk, tn), ((1,), (0,))
        amap = lambda i, j, k: (i + a_off[0] // tm, k + a_off[1] // tk)
        bmap = lambda i, j, k: (k + b_off[0] // tk, j + b_off[1] // tn)
    elif mode == "nt":
        ab, bb, contract = (tm, tk), (tn, tk), ((1,), (1,))
        amap = lambda i, j, k: (i + a_off[0] // tm, k + a_off[1] // tk)
        bmap = lambda i, j, k: (j + b_off[0] // tn, k + b_off[1] // tk)
    else:
        ab, bb, contract = (tk, tm), (tk, tn), ((0,), (0,))
        amap = lambda i, j, k: (k + a_off[0] // tk, i + a_off[1] // tm)
        bmap = lambda i, j, k: (k + b_off[0] // tk, j + b_off[1] // tn)
    assert a_off[0] % ab[0] == 0 and a_off[1] % ab[1] == 0 and b_off[0] % bb[0] == 0 and b_off[1] % bb[1] == 0, name
    in_specs = [pl.BlockSpec(ab, amap), pl.BlockSpec(bb, bmap)]
    ex_arrays = []
    for arr, kind, off in extras:
        if kind == "mn":
            assert off[0] % tm == 0 and off[1] % tn == 0, name
            in_specs.append(pl.BlockSpec((tm, tn), lambda i, j, k, off=off: (i + off[0] // tm, j + off[1] // tn)))
        else:
            assert off[1] % tn == 0, name
            in_specs.append(pl.BlockSpec((1, tn), lambda i, j, k, off=off: (0, j + off[1] // tn)))
        ex_arrays.append(arr)
    ne, no = len(ex_arrays), len(out_dtypes)
    if epilogue is None:
        def epilogue(acc, ex, outs):
            outs[0][...] = acc.astype(outs[0].dtype)

    def body(*refs):
        a_ref, b_ref = refs[0], refs[1]
        ex, outs, acc = refs[2:2 + ne], refs[2 + ne:2 + ne + no], refs[-1]
        k = pl.program_id(2)

        @pl.when(k == 0)
        def _():
            acc[...] = jnp.zeros_like(acc)

        acc[...] += _dot(a_ref[...], b_ref[...], contract)

        @pl.when(k == nk - 1)
        def _():
            epilogue(acc[...], ex, outs)

    res = pl.pallas_call(
        body,
        grid=(M // tm, N // tn, nk),
        in_specs=in_specs,
        out_specs=[pl.BlockSpec((tm, tn), lambda i, j, k: (i, j)) for _ in out_dtypes],
        out_shape=[jax.ShapeDtypeStruct((M, N), dt) for dt in out_dtypes],
        scratch_shapes=[pltpu.VMEM((tm, tn), F32)],
        compiler_params=_cparams("parallel", "parallel", "arbitrary"),
        name=name,
    )(a, b, *ex_arrays)
    return res[0] if no == 1 else res


def _rms_fwd(x, g, name):
    S, D = x.shape
    tm = _tile(S, 256)

    def body(x_ref, g_ref, h_ref):
        xv = x_ref[...]
        r = lax.rsqrt(jnp.mean(xv * xv, axis=-1, keepdims=True) + RMS_EPS)
        h_ref[...] = (xv * r * g_ref[...]).astype(h_ref.dtype)

    return pl.pallas_call(
        body,
        grid=(S // tm,),
        in_specs=[pl.BlockSpec((tm, D), lambda i: (i, 0)), pl.BlockSpec((1, D), lambda i: (0, 0))],
        out_specs=pl.BlockSpec((tm, D), lambda i: (i, 0)),
        out_shape=jax.ShapeDtypeStruct((S, D), BF16),
        compiler_params=_cparams("parallel"),
        name=name,
    )(x, g)


def _rms_bwd(dh, x, g, dres, name, with_bf16):
    S, D = x.shape
    tm = _tile(S, 256)

    def body(dh_ref, x_ref, g_ref, dres_ref, *outs):
        dx_ref, dg_ref = outs[0], outs[-1]
        xv = x_ref[...]
        r = lax.rsqrt(jnp.mean(xv * xv, axis=-1, keepdims=True) + RMS_EPS)
        xr = xv * r
        dhv = dh_ref[...].astype(F32)

        @pl.when(pl.program_id(0) == 0)
        def _():
            dg_ref[...] = jnp.zeros_like(dg_ref)

        dg_ref[...] += jnp.sum(dhv * xr, axis=0, keepdims=True)
        u = dhv * g_ref[...]
        c = jnp.mean(u * xr, axis=-1, keepdims=True)
        dx = dres_ref[...] + r * (u - xr * c)
        dx_ref[...] = dx
        if with_bf16:
            outs[1][...] = dx.astype(BF16)

    row = pl.BlockSpec((tm, D), lambda i: (i, 0))
    vec = pl.BlockSpec((1, D), lambda i: (0, 0))
    out_specs = [row] + ([row] if with_bf16 else []) + [vec]
    out_shape = [jax.ShapeDtypeStruct((S, D), F32)] + ([jax.ShapeDtypeStruct((S, D), BF16)] if with_bf16 else []) + [jax.ShapeDtypeStruct((1, D), F32)]
    return pl.pallas_call(
        body,
        grid=(S // tm,),
        in_specs=[row, row, vec, row],
        out_specs=out_specs,
        out_shape=out_shape,
        compiler_params=_cparams("arbitrary"),
        name=name,
    )(dh, x, g, dres)


def _loss_head(x3, tgt, g, name):
    S, D = x3.shape
    tm = _tile(S, 256)

    def body(x_ref, t_ref, g_ref, dx_ref, dxb_ref, dg_ref, loss_ref):
        xv = x_ref[...]
        gv = g_ref[...]
        r = lax.rsqrt(jnp.mean(xv * xv, axis=-1, keepdims=True) + RMS_EPS)
        xr = xv * r
        e = xr * gv - t_ref[...]

        @pl.when(pl.program_id(0) == 0)
        def _():
            dg_ref[...] = jnp.zeros_like(dg_ref)
            loss_ref[...] = jnp.zeros_like(loss_ref)

        loss_ref[...] += jnp.sum(e * e, axis=0, keepdims=True) * (0.5 / D)
        dy = e * (1.0 / D)
        dg_ref[...] += jnp.sum(dy * xr, axis=0, keepdims=True)
        u = dy * gv
        c = jnp.mean(u * xr, axis=-1, keepdims=True)
        dx = r * (u - xr * c)
        dx_ref[...] = dx
        dxb_ref[...] = dx.astype(BF16)

    row = pl.BlockSpec((tm, D), lambda i: (i, 0))
    vec = pl.BlockSpec((1, D), lambda i: (0, 0))
    return pl.pallas_call(
        body,
        grid=(S // tm,),
        in_specs=[row, row, vec],
        out_specs=[row, row, vec, vec],
        out_shape=[jax.ShapeDtypeStruct((S, D), F32), jax.ShapeDtypeStruct((S, D), BF16), jax.ShapeDtypeStruct((1, D), F32), jax.ShapeDtypeStruct((1, D), F32)],
        compiler_params=_cparams("arbitrary"),
        name=name,
    )(x3, tgt, g)


def _conv_taps(cur_ref, halo_ref, w_ref, b_ref, first):
    cur = cur_ref[...].astype(F32)
    halo = jnp.where(first, 0.0, halo_ref[...].astype(F32))
    xx = jnp.concatenate([halo, cur], axis=0)
    p1 = pltpu.roll(xx, 1, 0)[HALO:]
    p2 = pltpu.roll(xx, 2, 0)[HALO:]
    w = w_ref[...]
    y = b_ref[...] + w[0:1] * p2 + w[1:2] * p1 + w[2:3] * cur
    return y, (cur, p1, p2)


def _convglu_specs(S, F, tm, tn, rows_axis):
    nj = F // tn
    if rows_axis == 0:
        ij = lambda f: (lambda i, j: f(i, j))
    else:
        ij = lambda f: (lambda j, i: f(i, j))
    hb = tm // HALO
    return [
        pl.BlockSpec((tm, tn), ij(lambda i, j: (i, j))),
        pl.BlockSpec((tm, tn), ij(lambda i, j: (i, j + nj))),
        pl.BlockSpec((HALO, tn), ij(lambda i, j: (jnp.maximum(i * hb - 1, 0), j))),
        pl.BlockSpec((HALO, tn), ij(lambda i, j: (jnp.maximum(i * hb - 1, 0), j + nj))),
        pl.BlockSpec((3, tn), ij(lambda i, j: (0, j))),
        pl.BlockSpec((3, tn), ij(lambda i, j: (0, j + nj))),
        pl.BlockSpec((1, tn), ij(lambda i, j: (0, j))),
        pl.BlockSpec((1, tn), ij(lambda i, j: (0, j + nj))),
    ]


def _convglu_fwd(up, cw, cb, name):
    S, F2 = up.shape
    F = F2 // 2
    tm, tn = _tile(S, 512), _tile(F, 512)

    def body(ua, ub, ha, hb, wa, wb, ba, bb, f_ref):
        first = pl.program_id(0) == 0
        a, _ = _conv_taps(ua, ha, wa, ba, first)
        b, _ = _conv_taps(ub, hb, wb, bb, first)
        f_ref[...] = (0.5 * a * (1.0 + lax.erf(a * INV_SQRT2)) * b).astype(f_ref.dtype)

    return pl.pallas_call(
        body,
        grid=(S // tm, F // tn),
        in_specs=_convglu_specs(S, F, tm, tn, 0),
        out_specs=pl.BlockSpec((tm, tn), lambda i, j: (i, j)),
        out_shape=jax.ShapeDtypeStruct((S, F), BF16),
        compiler_params=_cparams("parallel", "parallel"),
        name=name,
    )(up, up, up, up, cw, cw, cb, cb)


def _convglu_bwd(df, up, cw, cb, name):
    S, F2 = up.shape
    F = F2 // 2
    tm, tn = _tile(S, 512), _tile(F, 512)

    def body(df_ref, ua, ub, ha, hb, wa, wb, ba, bb, da_ref, db_ref, dba_ref, dbb_ref, dwa_ref, dwb_ref):
        first = pl.program_id(1) == 0
        a, pa = _conv_taps(ua, ha, wa, ba, first)
        b, pb = _conv_taps(ub, hb, wb, bb, first)
        dfv = df_ref[...].astype(F32)
        cdf = 0.5 * (1.0 + lax.erf(a * INV_SQRT2))
        pdf = jnp.exp(-0.5 * a * a) * INV_SQRT_2PI
        da = dfv * b * (cdf + a * pdf)
        db = dfv * (a * cdf)
        da_ref[...] = da.astype(BF16)
        db_ref[...] = db.astype(BF16)

        @pl.when(first)
        def _():
            for r in (dba_ref, dbb_ref, dwa_ref, dwb_ref):
                r[...] = jnp.zeros_like(r)

        for d, taps, dbias, dw in ((da, pa, dba_ref, dwa_ref), (db, pb, dbb_ref, dwb_ref)):
            dbias[...] += jnp.sum(d, axis=0, keepdims=True)
            dw[0:1, :] += jnp.sum(d * taps[2], axis=0, keepdims=True)
            dw[1:2, :] += jnp.sum(d * taps[1], axis=0, keepdims=True)
            dw[2:3, :] += jnp.sum(d * taps[0], axis=0, keepdims=True)

    tile = pl.BlockSpec((tm, tn), lambda j, i: (i, j))
    b1 = pl.BlockSpec((1, tn), lambda j, i: (0, j))
    b3 = pl.BlockSpec((3, tn), lambda j, i: (0, j))
    return pl.pallas_call(
        body,
        grid=(F // tn, S // tm),
        in_specs=[tile] + _convglu_specs(S, F, tm, tn, 1),
        out_specs=[tile, tile, b1, b1, b3, b3],
        out_shape=[jax.ShapeDtypeStruct((S, F), BF16)] * 2 + [jax.ShapeDtypeStruct((1, F), F32)] * 2 + [jax.ShapeDtypeStruct((3, F), F32)] * 2,
        compiler_params=_cparams("parallel", "arbitrary"),
        name=name,
    )(df, up, up, up, up, cw, cw, cb, cb)


def _conv_transpose(da, db, cw, name):
    S, F = da.shape
    tm, tn = _tile(S, 512), _tile(F, 512)
    nj, ni, hb = F // tn, S // tm, tm // HALO
    n = tm + HALO

    def body(a_ref, b_ref, an_ref, bn_ref, w_ref, o_ref):
        j, i = pl.program_id(0), pl.program_id(1)

        def run(c_ref, h_ref):
            cur = c_ref[...].astype(F32)
            halo = jnp.where(i == ni - 1, 0.0, h_ref[...].astype(F32))
            xx = jnp.concatenate([cur, halo], axis=0)
            n1 = pltpu.roll(xx, n - 1, 0)[:tm]
            n2 = pltpu.roll(xx, n - 2, 0)[:tm]
            w = w_ref[...]
            o_ref[...] = (w[2:3] * cur + w[1:2] * n1 + w[0:1] * n2).astype(o_ref.dtype)

        @pl.when(j < nj)
        def _():
            run(a_ref, an_ref)

        @pl.when(j >= nj)
        def _():
            run(b_ref, bn_ref)

    ja = lambda j: jnp.minimum(j, nj - 1)
    jb = lambda j: jnp.maximum(j - nj, 0)
    nxt = lambda i: jnp.minimum((i + 1) * hb, S // HALO - 1)
    return pl.pallas_call(
        body,
        grid=(2 * nj, ni),
        in_specs=[
            pl.BlockSpec((tm, tn), lambda j, i: (i, ja(j))),
            pl.BlockSpec((tm, tn), lambda j, i: (i, jb(j))),
            pl.BlockSpec((HALO, tn), lambda j, i: (nxt(i), ja(j))),
            pl.BlockSpec((HALO, tn), lambda j, i: (nxt(i), jb(j))),
            pl.BlockSpec((3, tn), lambda j, i: (0, j)),
        ],
        out_specs=pl.BlockSpec((tm, tn), lambda j, i: (i, j)),
        out_shape=jax.ShapeDtypeStruct((S, 2 * F), BF16),
        compiler_params=_cparams("parallel", "parallel"),
        name=name,
    )(da, db, da, db, cw)


def _gate_bwd(dmixed, gates, y_pool, y_attn, name):
    S, D = dmixed.shape
    tm, tn = _tile(S, 512), _tile(D, 512)
    nj = D // tn

    def body(dm_ref, g_ref, yp_ref, ya_ref, dy_ref, dpre_ref, db_ref):
        j = pl.program_id(0)

        @pl.when(pl.program_id(1) == 0)
        def _():
            db_ref[...] = jnp.zeros_like(db_ref)

        def run(y_ref):
            dm = dm_ref[...].astype(F32)
            gv = g_ref[...]
            dy_ref[...] = (dm * gv).astype(BF16)
            dpre = dm * y_ref[...].astype(F32) * gv * (1.0 - gv)
            dpre_ref[...] = dpre.astype(BF16)
            db_ref[...] += jnp.sum(dpre, axis=0, keepdims=True)

        @pl.when(j < nj)
        def _():
            run(yp_ref)

        @pl.when(j >= nj)
        def _():
            run(ya_ref)

    tile2 = pl.BlockSpec((tm, tn), lambda j, i: (i, j))
    return pl.pallas_call(
        body,
        grid=(2 * nj, S // tm),
        in_specs=[
            pl.BlockSpec((tm, tn), lambda j, i: (i, lax.rem(j, nj))),
            tile2,
            pl.BlockSpec((tm, tn), lambda j, i: (i, jnp.minimum(j, nj - 1))),
            pl.BlockSpec((tm, tn), lambda j, i: (i, jnp.maximum(j - nj, 0))),
        ],
        out_specs=[tile2, tile2, pl.BlockSpec((1, tn), lambda j, i: (0, j))],
        out_shape=[jax.ShapeDtypeStruct((S, 2 * D), BF16), jax.ShapeDtypeStruct((S, 2 * D), BF16), jax.ShapeDtypeStruct((1, 2 * D), F32)],
        compiler_params=_cparams("parallel", "arbitrary"),
        name=name,
    )(dmixed, gates, y_pool, y_attn)


def _pool_counts(i, tm, rows, w):
    t = i * tm + lax.broadcasted_iota(jnp.int32, (rows, 1), 0)
    return jnp.minimum(t + 1, w).astype(F32)


def _pooled_groups(u_ref, uh_ref, i, tm, C):
    cur = u_ref[...]
    halo = jnp.where(i == 0, 0.0, uh_ref[...])
    xx = jnp.concatenate([halo, cur], axis=0)
    out = []
    s = xx
    for gi, w in enumerate(POOL_WINDOWS):
        s = s + pltpu.roll(s, w // 2, 0)
        tot = s[HALO:, 0:C]
        out.append(tot / _pool_counts(i, tm, tm, w) - cur[:, gi * C:(gi + 1) * C])
        s = s[:, C:] if gi + 1 < len(POOL_WINDOWS) else s
    return out


def _pool_fwd(u, wl, scale, name):
    S, PW = u.shape
    C = PW // len(POOL_WINDOWS)
    tm = _tile(S, 512)
    hb = tm // HALO

    def body(u_ref, uh_ref, wl_ref, sc_ref, o_ref):
        i = pl.program_id(0)
        pooled = _pooled_groups(u_ref, uh_ref, i, tm, C)
        for gi in range(len(POOL_WINDOWS)):
            y = _dot_nn(pooled[gi].astype(BF16), wl_ref[gi])
            o_ref[:, gi * C:(gi + 1) * C] = (y * sc_ref[:, gi * C:(gi + 1) * C]).astype(o_ref.dtype)

    return pl.pallas_call(
        body,
        grid=(S // tm,),
        in_specs=[
            pl.BlockSpec((tm, PW), lambda i: (i, 0)),
            pl.BlockSpec((HALO, PW), lambda i: (jnp.maximum(i * hb - 1, 0), 0)),
            pl.BlockSpec((len(POOL_WINDOWS), C, C), lambda i: (0, 0, 0)),
            pl.BlockSpec((1, PW), lambda i: (0, 0)),
        ],
        out_specs=pl.BlockSpec((tm, PW), lambda i: (i, 0)),
        out_shape=jax.ShapeDtypeStruct((S, PW), BF16),
        compiler_params=_cparams("parallel"),
        name=name,
    )(u, u, wl, scale)


def _pool_bwd(u, dp, wl, scale, name):
    S, PW = u.shape
    G = len(POOL_WINDOWS)
    C = PW // G
    tm = _tile(S, 512)
    hb, ni = tm // HALO, S // tm
    n = tm + HALO

    def body(u_ref, uh_ref, dp_ref, dpn_ref, wl_ref, sc_ref, du_ref, dwl_ref, dsc_ref):
        i = pl.program_id(0)

        @pl.when(i == 0)
        def _():
            dwl_ref[...] = jnp.zeros_like(dwl_ref)
            dsc_ref[...] = jnp.zeros_like(dsc_ref)

        pooled = _pooled_groups(u_ref, uh_ref, i, tm, C)
        dpc = dp_ref[...].astype(F32)
        dpn = jnp.where(i == ni - 1, 0.0, dpn_ref[...].astype(F32))
        sc = sc_ref[...]
        dyl = jnp.concatenate([dpc, dpn], axis=0) * sc
        for gi, w in enumerate(POOL_WINDOWS):
            cols = slice(gi * C, (gi + 1) * C)
            pb = pooled[gi].astype(BF16)
            ylin = _dot_nn(pb, wl_ref[gi])
            dsc_ref[:, cols] += jnp.sum(dpc[:, cols] * ylin, axis=0, keepdims=True)
            dylg = dyl[:, cols].astype(BF16)
            dwl_ref[gi] += _dot(pb, dylg[:tm], ((0,), (0,)))
            dpool = _dot_nt(dylg, wl_ref[gi])
            e = dpool / _pool_counts(i, tm, n, w)
            k = 1
            while k < w:
                e = e + pltpu.roll(e, n - k, 0)
                k *= 2
            du_ref[:, cols] = (e[:tm] - dpool[:tm]).astype(du_ref.dtype)

    return pl.pallas_call(
        body,
        grid=(ni,),
        in_specs=[
            pl.BlockSpec((tm, PW), lambda i: (i, 0)),
            pl.BlockSpec((HALO, PW), lambda i: (jnp.maximum(i * hb - 1, 0), 0)),
            pl.BlockSpec((tm, PW), lambda i: (i, 0)),
            pl.BlockSpec((HALO, PW), lambda i: (jnp.minimum((i + 1) * hb, S // HALO - 1), 0)),
            pl.BlockSpec((G, C, C), lambda i: (0, 0, 0)),
            pl.BlockSpec((1, PW), lambda i: (0, 0)),
        ],
        out_specs=[pl.BlockSpec((tm, PW), lambda i: (i, 0)), pl.BlockSpec((G, C, C), lambda i: (0, 0, 0)), pl.BlockSpec((1, PW), lambda i: (0, 0))],
        out_shape=[jax.ShapeDtypeStruct((S, PW), BF16), jax.ShapeDtypeStruct((G, C, C), F32), jax.ShapeDtypeStruct((1, PW), F32)],
        compiler_params=_cparams("arbitrary"),
        name=name,
    )(u, u, dp, dp, wl, scale)


def _band_masks():
    ii = lax.broadcasted_iota(jnp.int32, (SPAN, SPAN), 0)
    kk = lax.broadcasted_iota(jnp.int32, (SPAN, SPAN), 1)
    return ((ii + SPAN - kk).astype(F32), kk >= ii), ((ii - kk).astype(F32), kk <= ii)


def _attn_chunk(L):
    return min(4 * SPAN, L)


def _attn_fwd(qv, d, cols, g, name):
    L = qv.shape[0]
    CQ = _attn_chunk(L)
    nb = CQ // SPAN
    cq, ck, cv = cols

    def body(q_ref, k_ref, v_ref, kp_ref, vp_ref, o_ref, lse_ref):
        c = pl.program_id(1)
        (jp, mp), (jc, mc) = _band_masks()
        for h in range(HEADS_PER_GROUP):
            hs = slice(h * HEAD_DIM, (h + 1) * HEAD_DIM)
            slope = ALIBI_SLOPES[g * HEADS_PER_GROUP + h] * d
            for b in range(nb):
                rows = slice(b * SPAN, (b + 1) * SPAN)
                q = q_ref[rows, hs]
                kc, vc = k_ref[rows, hs], v_ref[rows, hs]
                if b == 0:
                    kp, vp, okp = kp_ref[:, hs], vp_ref[:, hs], jnp.logical_and(mp, c > 0)
                else:
                    prev = slice((b - 1) * SPAN, b * SPAN)
                    kp, vp, okp = k_ref[prev, hs], v_ref[prev, hs], mp
                sc = jnp.where(mc, _dot_nt(q, kc) * ATTN_SCALE - slope * jc, NEG_BIG)
                sp = jnp.where(okp, _dot_nt(q, kp) * ATTN_SCALE - slope * jp, NEG_BIG)
                m = jnp.maximum(jnp.max(sc, axis=-1, keepdims=True), jnp.max(sp, axis=-1, keepdims=True))
                pc, pp = jnp.exp(sc - m), jnp.exp(sp - m)
                l = jnp.sum(pc, axis=-1, keepdims=True) + jnp.sum(pp, axis=-1, keepdims=True)
                o = (_dot_nn(pc.astype(BF16), vc) + _dot_nn(pp.astype(BF16), vp)) / l
                o_ref[rows, hs] = o
                lse_ref[rows, hs] = jnp.broadcast_to(m + jnp.log(l), (SPAN, HEAD_DIM))

    W = GROUP_WIDTH
    pb = CQ // SPAN
    cur = lambda f: pl.BlockSpec((CQ, W), lambda r, c: (c, f(r)))
    prv = lambda f: pl.BlockSpec((SPAN, W), lambda r, c: (jnp.maximum(c * pb - 1, 0), f(r)))
    out = pl.BlockSpec((CQ, W), lambda r, c: (c, r))
    return pl.pallas_call(
        body,
        grid=(d, L // CQ),
        in_specs=[cur(cq), cur(ck), cur(cv), prv(ck), prv(cv)],
        out_specs=[out, out],
        out_shape=[jax.ShapeDtypeStruct((L, d * W), F32)] * 2,
        compiler_params=_cparams("parallel", "parallel"),
        name=name,
    )(qv, qv, qv, qv, qv)


def _attn_merge(os_, lses, name):
    S, W = os_[0].shape
    tm = _tile(S, 512)

    def body(o0, o1, o2, l0, l1, l2, y_ref, lse_ref):
        ls = [l0[...], l1[...], l2[...]]
        m = jnp.maximum(jnp.maximum(ls[0], ls[1]), ls[2])
        es = [jnp.exp(v - m) for v in ls]
        tot = es[0] + es[1] + es[2]
        y = (es[0] * o0[...] + es[1] * o1[...] + es[2] * o2[...]) / tot
        y_ref[...] = y.astype(y_ref.dtype)
        lse_ref[...] = m + jnp.log(tot)

    row = pl.BlockSpec((tm, W), lambda i: (i, 0))
    return pl.pallas_call(
        body,
        grid=(S // tm,),
        in_specs=[row] * 6,
        out_specs=[row, row],
        out_shape=[jax.ShapeDtypeStruct((S, W), BF16), jax.ShapeDtypeStruct((S, W), F32)],
        compiler_params=_cparams("parallel"),
        name=name,
    )(*os_, *lses)


def _attn_bwd(qv, dav, yv, lsev, d, cols, g, name):
    L = qv.shape[0]
    CQ = _attn_chunk(L)
    nb = CQ // SPAN
    nchunk = L // CQ
    cq, ck, cv = cols

    def body(q_ref, k_ref, v_ref, kp_ref, vp_ref, qn_ref, da_ref, dan_ref, y_ref, yn_ref, lse_ref, lsen_ref, dq_ref, dk_ref, dv_ref):
        c = pl.program_id(1)
        (jp, mp), (jc, mc) = _band_masks()
        for h in range(HEADS_PER_GROUP):
            hs = slice(h * HEAD_DIM, (h + 1) * HEAD_DIM)
            slope = ALIBI_SLOPES[g * HEADS_PER_GROUP + h] * d
            dq = [None] * nb
            dk = [None] * nb
            dv = [None] * nb

            def add(lst, idx, val):
                lst[idx] = val if lst[idx] is None else lst[idx] + val

            for qb in range(nb + 1):
                if qb < nb:
                    rows = slice(qb * SPAN, (qb + 1) * SPAN)
                    q, da, yy, lse = q_ref[rows, hs], da_ref[rows, hs], y_ref[rows, hs], lse_ref[rows, hs]
                else:
                    q, da, yy, lse = qn_ref[:, hs], dan_ref[:, hs], yn_ref[:, hs], lsen_ref[:, hs]
                dd = jnp.sum(da.astype(F32) * yy.astype(F32), axis=-1, keepdims=True)
                lse_col = lse[:, 0:1]
                for kb in (qb - 1, qb):
                    if kb >= nb:
                        continue
                    if kb < 0:
                        kk, vv, ok = kp_ref[:, hs], vp_ref[:, hs], jnp.logical_and(mp, c > 0)
                    else:
                        krows = slice(kb * SPAN, (kb + 1) * SPAN)
                        kk, vv = k_ref[krows, hs], v_ref[krows, hs]
                        ok = mc if kb == qb else (mp if qb < nb else jnp.logical_and(mp, c < nchunk - 1))
                    jj = jc if kb == qb else jp
                    s = jnp.where(ok, _dot_nt(q, kk) * ATTN_SCALE - slope * jj, NEG_BIG)
                    p = jnp.exp(s - lse_col)
                    ds = p * (_dot_nt(da, vv) - dd)
                    if qb < nb:
                        add(dq, qb, _dot_nn(ds.astype(BF16), kk))
                    if kb >= 0:
                        add(dv, kb, _dot_nn(p.T.astype(BF16), da))
                        add(dk, kb, _dot_nn(ds.T.astype(BF16), q))
            for b in range(nb):
                rows = slice(b * SPAN, (b + 1) * SPAN)
                dq_ref[rows, hs] = (dq[b] * ATTN_SCALE).astype(dq_ref.dtype)
                dk_ref[rows, hs] = (dk[b] * ATTN_SCALE).astype(dk_ref.dtype)
                dv_ref[rows, hs] = dv[b].astype(dv_ref.dtype)

    W = GROUP_WIDTH
    pb = CQ // SPAN
    nblocks = L // SPAN
    cur = lambda f: pl.BlockSpec((CQ, W), lambda r, c: (c, f(r)))
    prv = lambda f: pl.BlockSpec((SPAN, W), lambda r, c: (jnp.maximum(c * pb - 1, 0), f(r)))
    nxt = lambda f: pl.BlockSpec((SPAN, W), lambda r, c: (jnp.minimum((c + 1) * pb, nblocks - 1), f(r)))
    ident = lambda r: r
    out = pl.BlockSpec((CQ, W), lambda r, c: (c, r))
    return pl.pallas_call(
        body,
        grid=(d, nchunk),
        in_specs=[cur(cq), cur(ck), cur(cv), prv(ck), prv(cv), nxt(cq), cur(ident), nxt(ident), cur(ident), nxt(ident), cur(ident), nxt(ident)],
        out_specs=[out, out, out],
        out_shape=[jax.ShapeDtypeStruct((L, d * W), BF16)] * 3,
        compiler_params=_cparams("parallel", "parallel"),
        name=name,
    )(qv, qv, qv, qv, qv, qv, dav, dav, yv, yv, lsev, lsev)


def _row_block(R, C, bytes_per_row_elem=4, budget=1 << 20):
    if R % 8:
        return R
    best = 8
    t = 8
    while t <= R:
        if R % t == 0 and t * C * bytes_per_row_elem <= budget:
            best = t
        t += 8
    return best


def _adamw(w, g, m, v, name):
    R, C = w.shape
    tr = _row_block(R, C)
    c1 = 1.0 - ADAM_B1 ** ADAM_STEP
    c2 = 1.0 - ADAM_B2 ** ADAM_STEP

    def body(w_ref, g_ref, m_ref, v_ref, d_ref, nm_ref, nv_ref):
        gv = g_ref[...]
        nm = ADAM_B1 * m_ref[...] + (1.0 - ADAM_B1) * gv
        nv = ADAM_B2 * v_ref[...] + (1.0 - ADAM_B2) * (gv * gv)
        d_ref[...] = -ADAM_LR * ((nm / c1) / (jnp.sqrt(nv / c2) + ADAM_EPS) + ADAM_WD * w_ref[...])
        nm_ref[...] = nm
        nv_ref[...] = nv

    blk = pl.BlockSpec((tr, C), lambda i: (i, 0))
    return pl.pallas_call(
        body,
        grid=(R // tr,),
        in_specs=[blk] * 4,
        out_specs=[blk] * 3,
        out_shape=[jax.ShapeDtypeStruct((R, C), F32)] * 3,
        compiler_params=_cparams("parallel"),
        name=name,
    )(w, g, m, v)


def _sum_slots(buf, name):
    n, R, C = buf.shape
    tr = _row_block(R, C, bytes_per_row_elem=n * buf.dtype.itemsize, budget=4 << 20)

    def body(b_ref, o_ref):
        acc = b_ref[0].astype(F32)
        for s in range(1, n):
            acc = acc + b_ref[s].astype(F32)
        o_ref[...] = acc

    return pl.pallas_call(
        body,
        grid=(R // tr,),
        in_specs=[pl.BlockSpec((n, tr, C), lambda i: (0, i, 0))],
        out_specs=pl.BlockSpec((tr, C), lambda i: (i, 0)),
        out_shape=jax.ShapeDtypeStruct((R, C), F32),
        compiler_params=_cparams("parallel"),
        name=name,
    )(buf)


def _position():
    return lax.axis_index("x"), lax.axis_index("y"), lax.axis_index("c")


def _shard_slice(ref, axis, idx, size):
    start = idx * size
    if axis == ref.ndim - 1:
        start = pl.multiple_of(start, 128)
    ix = [slice(None)] * ref.ndim
    ix[axis] = pl.ds(start, size)
    return ref.at[tuple(ix)]


def _gather_chips(shards, axes, name):
    n = len(shards)
    fulls = []
    for s, ax in zip(shards, axes):
        shp = list(s.shape)
        shp[ax] *= N_CHIPS
        fulls.append(jax.ShapeDtypeStruct(tuple(shp), s.dtype))

    def body(*refs):
        ins, outs = refs[:n], refs[n:2 * n]
        send_sems, recv_sems, loc_sems = refs[2 * n:]
        x, y, c = _position()
        me = 2 * x + y
        peers = [(1 - x, y), (x, 1 - y), (1 - x, 1 - y)]
        started = []
        for t in range(n):
            size = ins[t].shape[axes[t]]
            mine = _shard_slice(outs[t], axes[t], me, size)
            loc = pltpu.make_async_copy(ins[t], mine, loc_sems.at[t])
            loc.start()
            started.append(loc)
            for p, (px, py) in enumerate(peers):
                cp = pltpu.make_async_remote_copy(src_ref=ins[t], dst_ref=mine, send_sem=send_sems.at[3 * t + p], recv_sem=recv_sems.at[3 * t + p],
                                                  device_id=(px, py, c), device_id_type=MESH)
                cp.start()
        for t in range(n):
            size = ins[t].shape[axes[t]]
            started[t].wait()
            for p, (px, py) in enumerate(peers):
                theirs = _shard_slice(outs[t], axes[t], 2 * px + py, size)
                cp = pltpu.make_async_remote_copy(src_ref=ins[t], dst_ref=theirs, send_sem=send_sems.at[3 * t + p], recv_sem=recv_sems.at[3 * t + p],
                                                  device_id=(px, py, c), device_id_type=MESH)
                cp.wait_send()
                cp.wait_recv()

    return pl.pallas_call(
        body,
        in_specs=[ANY] * n,
        out_specs=[ANY] * n,
        out_shape=fulls,
        scratch_shapes=[pltpu.SemaphoreType.DMA((3 * n,)), pltpu.SemaphoreType.DMA((3 * n,)), pltpu.SemaphoreType.DMA((n,))],
        name=name,
    )(*shards)


def _scatter_pieces(grads, axes, small, name):
    n = len(grads)
    pieces = []
    for gr, ax in zip(grads, axes):
        R, C = gr.shape
        pieces.append((R // (2 * N_CHIPS), C) if ax == 0 else (R // 2, C // N_CHIPS))
    outs_shape = [jax.ShapeDtypeStruct((N_DEV,) + p, gr.dtype) for p, gr in zip(pieces, grads)]
    outs_shape.append(jax.ShapeDtypeStruct((N_DEV,) + small.shape, small.dtype))
    nt = n + 1

    def body(*refs):
        ins, outs = refs[:nt], refs[nt:2 * nt]
        send_sems, recv_sems, loc_sems = refs[2 * nt:]
        x, y, c = _position()
        me = 4 * x + 2 * y + c

        def piece_of(t, chip, core):
            if t == n:
                return ins[t]
            pr, pc = pieces[t]
            if axes[t] == 0:
                return ins[t].at[pl.ds((chip * 2 + core) * pr, pr), :]
            return ins[t].at[pl.ds(core * pr, pr), pl.ds(pl.multiple_of(chip * pc, 128), pc)]

        def peer(k):
            fx, fy, fc = (k >> 2) & 1, (k >> 1) & 1, k & 1
            px = (1 - x) if fx else x
            py = (1 - y) if fy else y
            pc = (1 - c) if fc else c
            return px, py, pc

        local = []
        for t in range(nt):
            loc = pltpu.make_async_copy(piece_of(t, 2 * x + y, c), outs[t].at[me], loc_sems.at[t])
            loc.start()
            local.append(loc)
            for k in range(1, N_DEV):
                px, py, pc = peer(k)
                cp = pltpu.make_async_remote_copy(src_ref=piece_of(t, 2 * px + py, pc), dst_ref=outs[t].at[me], send_sem=send_sems.at[7 * t + k - 1],
                                                  recv_sem=recv_sems.at[7 * t + k - 1], device_id=(px, py, pc), device_id_type=MESH)
                cp.start()
        for t in range(nt):
            local[t].wait()
            for k in range(1, N_DEV):
                px, py, pc = peer(k)
                cp = pltpu.make_async_remote_copy(src_ref=piece_of(t, 2 * px + py, pc), dst_ref=outs[t].at[4 * px + 2 * py + pc], send_sem=send_sems.at[7 * t + k - 1],
                                                  recv_sem=recv_sems.at[7 * t + k - 1], device_id=(px, py, pc), device_id_type=MESH)
                cp.wait_send()
                cp.wait_recv()

    return pl.pallas_call(
        body,
        in_specs=[ANY] * nt,
        out_specs=[ANY] * nt,
        out_shape=outs_shape,
        scratch_shapes=[pltpu.SemaphoreType.DMA((7 * nt,)), pltpu.SemaphoreType.DMA((7 * nt,)), pltpu.SemaphoreType.DMA((nt,))],
        name=name,
    )(*grads, small)


def _exchange_halves(halves, axes, name):
    n = len(halves)
    outs_shape = [jax.ShapeDtypeStruct((2 * h.shape[0], h.shape[1]), h.dtype) for h in halves]

    def body(*refs):
        ins, outs = refs[:n], refs[n:2 * n]
        send_sems, recv_sems, loc_sems = refs[2 * n:]
        x, y, c = _position()
        local, remote = [], []
        for t in range(n):
            pr = ins[t].shape[0]
            mine = outs[t].at[pl.ds(c * pr, pr), :]
            loc = pltpu.make_async_copy(ins[t], mine, loc_sems.at[t])
            loc.start()
            local.append(loc)
            cp = pltpu.make_async_remote_copy(src_ref=ins[t], dst_ref=mine, send_sem=send_sems.at[t], recv_sem=recv_sems.at[t],
                                              device_id=(x, y, 1 - c), device_id_type=MESH)
            cp.start()
        for t in range(n):
            pr = ins[t].shape[0]
            local[t].wait()
            theirs = outs[t].at[pl.ds((1 - c) * pr, pr), :]
            cp = pltpu.make_async_remote_copy(src_ref=ins[t], dst_ref=theirs, send_sem=send_sems.at[t], recv_sem=recv_sems.at[t],
                                              device_id=(x, y, 1 - c), device_id_type=MESH)
            cp.wait_send()
            cp.wait_recv()

    return pl.pallas_call(
        body,
        in_specs=[ANY] * n,
        out_specs=[ANY] * n,
        out_shape=outs_shape,
        scratch_shapes=[pltpu.SemaphoreType.DMA((n,)), pltpu.SemaphoreType.DMA((n,)), pltpu.SemaphoreType.DMA((n,))],
        name=name,
    )(*halves)


def _dilated_view(a, d):
    S, W = a.shape
    return a.reshape(S // d, d * W)


def _natural_view(a, d):
    L, W = a.shape
    return a.reshape(L * d, W // d)


def _local_step(x, tgt, w):
    S, D = x.shape
    PW = w["pool_scale"].shape[1]
    F = w["w_down"].shape[0]
    o_q = PW
    o_g = PW + 3 * ATTN_WIDTH
    QKV = 3 * ATTN_WIDTH

    h1 = _rms_fwd(x, w["g_mix"], "rms1")
    u = _mm(h1, w["w_in"], mode="nn", dims=(S, PW, D), out_dtypes=(F32,), name="proj_u")
    qkv = _mm(h1, w["w_in"], mode="nn", dims=(S, QKV, D), b_off=(0, o_q), name="proj_qkv")

    def gate_epilogue(acc, ex, outs):
        outs[0][...] = 1.0 / (1.0 + jnp.exp(-(acc + ex[0][...])))

    gates = _mm(h1, w["w_in"], mode="nn", dims=(S, 2 * D, D), b_off=(0, o_g), out_dtypes=(F32,), epilogue=gate_epilogue,
                extras=[(w["b_gate"], "n", (0, 0))], name="proj_gates")

    pool_out = _pool_fwd(u, w["w_pool_lin"], w["pool_scale"], "pool_fwd")

    ident3 = (lambda r: 0, lambda r: 3, lambda r: 6)
    trip = (lambda r: 3 * r, lambda r: 3 * r + 1, lambda r: 3 * r + 2)
    qviews, os_, lses = [], [], []
    for gi, (_, d) in enumerate(ATTN_GROUPS):
        if d == 1:
            assert gi == 0
            qv, cols = qkv, ident3
        else:
            qv = _dilated_view(qkv.reshape(S, 9, GROUP_WIDTH)[:, gi::3, :].reshape(S, 3 * GROUP_WIDTH), d)
            cols = trip
        qviews.append((qv, cols))
        o, lse = _attn_fwd(qv, d, cols, gi, f"attn_fwd{gi}")
        os_.append(_natural_view(o, d))
        lses.append(_natural_view(lse, d))
    attn, lse_tot = _attn_merge(os_, lses, "attn_merge")

    y_pool = _mm(pool_out, w["w_pool_out"], mode="nn", dims=(S, D, PW), name="y_pool")

    def mix_epilogue(acc, ex, outs):
        outs[0][...] = acc.astype(BF16)
        outs[1][...] = (ex[0][...] * ex[2][...].astype(F32) + ex[1][...] * acc).astype(BF16)

    y_attn, mixed = _mm(attn, w["w_attn_out"], mode="nn", dims=(S, D, GROUP_WIDTH), out_dtypes=(BF16, BF16), epilogue=mix_epilogue,
                        extras=[(gates, "mn", (0, 0)), (gates, "mn", (0, D)), (y_pool, "mn", (0, 0))], name="y_attn_mix")

    def residual_epilogue(acc, ex, outs):
        outs[0][...] = ex[0][...] + acc

    x2 = _mm(mixed, w["w_out"], mode="nn", dims=(S, D, D), out_dtypes=(F32,), epilogue=residual_epilogue, extras=[(x, "mn", (0, 0))], name="out_proj")

    h2 = _rms_fwd(x2, w["g_ffn"], "rms2")
    up = _mm(h2, w["w_up"], mode="nn", dims=(S, 2 * F, D), name="up_proj")
    f = _convglu_fwd(up, w["conv_w"], w["conv_b"], "convglu_fwd")
    x3 = _mm(f, w["w_down"], mode="nn", dims=(S, D, F), tiles=(_tile(S, 1024), _tile(D, 512), _tile(F, 512)), out_dtypes=(F32,),
             epilogue=residual_epilogue, extras=[(x2, "mn", (0, 0))], name="down_proj")

    g = {}
    dx3, dx3b, g["g_final"], loss_cols = _loss_head(x3, tgt, w["g_final"], "loss_head")

    tn_tiles = lambda M, N: (_tile(M, 512), _tile(N, 1024), _tile(S, 512))
    g["w_down"] = _mm(f, dx3b, mode="tn", dims=(F, D, S), tiles=tn_tiles(F, D), name="dw_down")
    df = _mm(dx3b, w["w_down"], mode="nt", dims=(S, F, D), name="d_f")
    da, db, dcb_a, dcb_b, dcw_a, dcw_b = _convglu_bwd(df, up, w["conv_w"], w["conv_b"], "convglu_bwd")
    g["conv_b"] = jnp.concatenate([dcb_a, dcb_b], axis=1)
    g["conv_w"] = jnp.concatenate([dcw_a, dcw_b], axis=1)
    dup = _conv_transpose(da, db, w["conv_w"], "conv_transpose")
    g["w_up"] = _mm(h2, dup, mode="tn", dims=(D, 2 * F, S), tiles=tn_tiles(D, 2 * F), name="dw_up")
    dh2 = _mm(dup, w["w_up"], mode="nt", dims=(S, D, 2 * F), tiles=(_tile(S, 1024), _tile(D, 512), _tile(2 * F, 1024)), name="d_h2")
    dx2, dx2b, g["g_ffn"] = _rms_bwd(dh2, x2, w["g_ffn"], dx3, "rms2_bwd", True)

    g["w_out"] = _mm(mixed, dx2b, mode="tn", dims=(D, D, S), tiles=tn_tiles(D, D), name="dw_out")
    dmixed = _mm(dx2b, w["w_out"], mode="nt", dims=(S, D, D), name="d_mixed")
    dy_both, dpre, g["b_gate"] = _gate_bwd(dmixed, gates, y_pool, y_attn, "gate_bwd")

    g["w_pool_out"] = _mm(pool_out, dy_both, mode="tn", dims=(PW, D, S), tiles=tn_tiles(PW, D), name="dw_pool_out")
    g["w_attn_out"] = _mm(attn, dy_both, mode="tn", dims=(GROUP_WIDTH, D, S), b_off=(0, D), tiles=tn_tiles(GROUP_WIDTH, D), name="dw_attn_out")
    dpool = _mm(dy_both, w["w_pool_out"], mode="nt", dims=(S, PW, D), name="d_pool")
    dattn = _mm(dy_both, w["w_attn_out"], mode="nt", dims=(S, GROUP_WIDTH, D), a_off=(0, D), name="d_attn")

    du, g["w_pool_lin"], g["pool_scale"] = _pool_bwd(u, dpool, w["w_pool_lin"], w["pool_scale"], "pool_bwd")

    dqs, dks, dvs = [], [], []
    for gi, (_, d) in enumerate(ATTN_GROUPS):
        qv, cols = qviews[gi]
        dq, dk, dv = _attn_bwd(qv, _dilated_view(dattn, d), _dilated_view(attn, d), _dilated_view(lse_tot, d), d, cols, gi, f"attn_bwd{gi}")
        dqs.append(_natural_view(dq, d))
        dks.append(_natural_view(dk, d))
        dvs.append(_natural_view(dv, d))
    dproj = jnp.concatenate([du] + dqs + dks + dvs + [dpre], axis=1)
    IN = dproj.shape[1]

    g["w_in"] = _mm(h1, dproj, mode="tn", dims=(D, IN, S), tiles=tn_tiles(D, IN), name="dw_in")
    dh1 = _mm(dproj, w["w_in"], mode="nt", dims=(S, D, IN), tiles=(_tile(S, 1024), _tile(D, 512), _tile(IN, 512)), name="d_h1")
    (grad_x, g["g_mix"]) = _rms_bwd(dh1, x, w["g_mix"], dx2, "rms1_bwd", False)
    return loss_cols, grad_x, g


BIG = ("w_in", "w_pool_out", "w_attn_out", "w_out", "w_up", "w_down")
BIG_AXIS = {"w_in": 1, "w_pool_out": 1, "w_attn_out": 1, "w_out": 0, "w_up": 1, "w_down": 0}
SMALL = ("g_mix", "b_gate", "w_pool_lin", "pool_scale", "g_ffn", "conv_w", "conv_b", "g_final")
SMALL_COLS = 1024
ORDER = ("g_mix", "w_in", "b_gate", "w_pool_lin", "pool_scale", "w_pool_out", "w_attn_out", "w_out", "g_ffn", "w_up", "conv_w", "conv_b", "w_down", "g_final")


def kernel(x, g_mix, w_in, b_gate, w_pool_lin, pool_scale, w_pool_out, w_attn_out, w_out, g_ffn, w_up, conv_w, conv_b, w_down, g_final, loss_target, m_g_mix, m_w_in, m_b_gate, m_w_pool_lin, m_pool_scale, m_w_pool_out, m_w_attn_out, m_w_out, m_g_ffn, m_w_up, m_conv_w, m_conv_b, m_w_down, m_g_final, v_g_mix, v_w_in, v_b_gate, v_w_pool_lin, v_pool_scale, v_w_pool_out, v_w_attn_out, v_w_out, v_g_ffn, v_w_up, v_conv_w, v_conv_b, v_w_down, v_g_final):
    shard = dict(g_mix=g_mix, w_in=w_in, b_gate=b_gate, w_pool_lin=w_pool_lin, pool_scale=pool_scale, w_pool_out=w_pool_out, w_attn_out=w_attn_out,
                 w_out=w_out, g_ffn=g_ffn, w_up=w_up, conv_w=conv_w, conv_b=conv_b, w_down=w_down, g_final=g_final)
    mom = dict(g_mix=m_g_mix, w_in=m_w_in, b_gate=m_b_gate, w_pool_lin=m_w_pool_lin, pool_scale=m_pool_scale, w_pool_out=m_w_pool_out, w_attn_out=m_w_attn_out,
               w_out=m_w_out, g_ffn=m_g_ffn, w_up=m_w_up, conv_w=m_conv_w, conv_b=m_conv_b, w_down=m_w_down, g_final=m_g_final)
    vel = dict(g_mix=v_g_mix, w_in=v_w_in, b_gate=v_b_gate, w_pool_lin=v_w_pool_lin, pool_scale=v_pool_scale, w_pool_out=v_w_pool_out, w_attn_out=v_w_attn_out,
               w_out=v_w_out, g_ffn=v_g_ffn, w_up=v_w_up, conv_w=v_conv_w, conv_b=v_conv_b, w_down=v_w_down, g_final=v_g_final)
    xi, yi = lax.axis_index("x"), lax.axis_index("y")
    chip = 2 * xi + yi
    S, D = x.shape[1], x.shape[2]

    names = list(BIG) + ["w_pool_lin", "conv_w"]
    locs = [shard[k][0].astype(BF16) for k in BIG] + [shard["w_pool_lin"][0].astype(BF16), shard["conv_w"][0]]
    axes = [BIG_AXIS[k] for k in BIG] + [1, 1]
    full = dict(zip(names, _gather_chips(locs, axes, "comm_gather_weights")))
    for k in ("g_mix", "b_gate", "pool_scale", "g_ffn", "conv_b"):
        full[k] = shard[k]
    full["g_final"] = shard["g_final"].reshape(1, D)

    loss_cols, grad_x, gr = _local_step(x[0], loss_target[0], full)

    parts = [loss_cols.reshape(-1)] + [gr[k].astype(F32).reshape(-1) for k in SMALL]
    sizes = [p.shape[0] for p in parts]
    flat = jnp.concatenate(parts)
    rows = -(-flat.shape[0] // (8 * SMALL_COLS)) * 8
    small = jnp.pad(flat, (0, rows * SMALL_COLS - flat.shape[0])).reshape(rows, SMALL_COLS)

    bufs = _scatter_pieces([gr[k] for k in BIG], [BIG_AXIS[k] for k in BIG], small, "comm_scatter_grads")
    halves = [_sum_slots(b, f"sum_{k}") for b, k in zip(bufs[:-1], BIG)]
    small_sum = _sum_slots(bufs[-1], "sum_small").reshape(-1)
    wholes = _exchange_halves(halves, [BIG_AXIS[k] for k in BIG], "comm_exchange_halves")

    grads = {}
    for k, whole in zip(BIG, wholes):
        grads[k] = whole.reshape(shard[k].shape)
    off = 0
    segs = []
    for sz in sizes:
        segs.append(small_sum[off:off + sz])
        off += sz
    loss = jnp.sum(segs[0])
    for k, seg in zip(SMALL, segs[1:]):
        fullg = seg.reshape(gr[k].shape)
        if k == "w_pool_lin":
            n = shard[k].shape[2]
            fullg = lax.dynamic_slice_in_dim(fullg, chip * n, n, axis=1)
        elif k == "conv_w":
            n = shard[k].shape[2]
            fullg = lax.dynamic_slice_in_dim(fullg, chip * n, n, axis=1)
        grads[k] = fullg.reshape(shard[k].shape)

    deltas, new_m, new_v = {}, {}, {}
    for k in ORDER:
        shp = shard[k].shape
        two_d = (-1, shp[-1])
        dl, nm, nv = _adamw(shard[k].reshape(two_d), grads[k].reshape(two_d), mom[k].reshape(two_d), vel[k].reshape(two_d), f"adamw_{k}")
        deltas[k], new_m[k], new_v[k] = dl.reshape(shp), nm.reshape(shp), nv.reshape(shp)

    return (loss, grad_x[None], *[grads[k] for k in ORDER], *[deltas[k] for k in ORDER], *[new_m[k] for k in ORDER], *[new_v[k] for k in ORDER])
```

```python
import functools
import math

import jax
import jax.numpy as jnp
from jax import lax
from jax.experimental import pallas as pl
from jax.experimental.pallas import tpu as pltpu

F32 = jnp.float32
BF16 = jnp.bfloat16

RMS_EPS = 1e-6
POOL_WINDOWS = (2, 4, 8, 16)
ATTN_GROUPS = ((128, 1), (512, 4), (2048, 16))
HEADS_PER_GROUP = 4
HEAD_DIM = 128
N_ATTN_HEADS = HEADS_PER_GROUP * len(ATTN_GROUPS)
SPAN = 128
GROUP_WIDTH = HEADS_PER_GROUP * HEAD_DIM
ATTN_WIDTH = N_ATTN_HEADS * HEAD_DIM
ATTN_SCALE = HEAD_DIM ** -0.5
NEG_BIG = -1e30
ALIBI_SLOPES = tuple(2.0 ** (-8.0 * (h + 1) / N_ATTN_HEADS) for h in range(N_ATTN_HEADS))

ADAM_LR = 0.001
ADAM_B1 = 0.9
ADAM_B2 = 0.999
ADAM_EPS = 1e-08
ADAM_WD = 0.01
ADAM_STEP = 10

INV_SQRT2 = 1.0 / math.sqrt(2.0)
INV_SQRT_2PI = 1.0 / math.sqrt(2.0 * math.pi)

HALO = 16
VMEM_LIMIT = 56 * 1024 * 1024
N_CHIPS = 4
N_DEV = 8
MESH = pl.DeviceIdType.MESH
ANY = pl.BlockSpec(memory_space=pl.ANY)


def _cparams(*sem):
    return pltpu.CompilerParams(dimension_semantics=sem, vmem_limit_bytes=VMEM_LIMIT)


def _tile(n, pref, mult=128):
    t = (min(pref, n) // mult) * mult
    while t >= mult:
        if n % t == 0:
            return t
        t -= mult
    return n


def _dot(a, b, contract):
    return lax.dot_general(a, b, (contract, ((), ())), preferred_element_type=F32)


def _dot_nn(a, b):
    return _dot(a, b, ((1,), (0,)))


def _dot_nt(a, b):
    return _dot(a, b, ((1,), (1,)))


def _mm(a, b, *, mode, dims, name, tiles=None, out_dtypes=(BF16,), epilogue=None, extras=(), a_off=(0, 0), b_off=(0, 0)):
    M, N, K = dims
    if tiles is None:
        tiles = (_tile(M, 1408), _tile(N, 2816), _tile(K, 512)) if mode == "tn" else (_tile(M, 1024), _tile(N, 1536), _tile(K, 1408))
    tm, tn, tk = tiles
    assert M % tm == 0 and N % tn == 0 and K % tk == 0, (name, dims, tiles)
    nk = K // tk
    if mode == "nn":
        ab, bb, contract = (tm, tk), (tk, tn), ((1,), (0,))
        amap = lambda i, j, k: (i + a_off[0] // tm, k + a_off[1] // tk)
        bmap = lambda i, j, k: (k + b_off[0] // tk, j + b_off[1] // tn)
    elif mode == "nt":
        ab, bb, contract = (tm, tk), (tn, tk), ((1,), (1,))
        amap = lambda i, j, k: (i + a_off[0] // tm, k + a_off[1] // tk)
        bmap = lambda i, j, k: (j + b_off[0] // tn, k + b_off[1] // tk)
    else:
        ab, bb, contract = (tk, tm), (tk, tn), ((0,), (0,))
        amap = lambda i, j, k: (k + a_off[0] // tk, i + a_off[1] // tm)
        bmap = lambda i, j, k: (k + b_off[0] // tk, j + b_off[1] // tn)
    assert a_off[0] % ab[0] == 0 and a_off[1] % ab[1] == 0 and b_off[0] % bb[0] == 0 and b_off[1] % bb[1] == 0, name
    in_specs = [pl.BlockSpec(ab, amap), pl.BlockSpec(bb, bmap)]
    ex_arrays = []
    for arr, kind, off in extras:
        if kind == "mn":
            assert off[0] % tm == 0 and off[1] % tn == 0, name
            in_specs.append(pl.BlockSpec((tm, tn), lambda i, j, k, off=off: (i + off[0] // tm, j + off[1] // tn)))
        else:
            assert off[1] % tn == 0, name
            in_specs.append(pl.BlockSpec((1, tn), lambda i, j, k, off=off: (0, j + off[1] // tn)))
        ex_arrays.append(arr)
    ne, no = len(ex_arrays), len(out_dtypes)
    if epilogue is None:
        def epilogue(acc, ex, outs):
            outs[0][...] = acc.astype(outs[0].dtype)

    def body(*refs):
        a_ref, b_ref = refs[0], refs[1]
        ex, outs, acc = refs[2:2 + ne], refs[2 + ne:2 + ne + no], refs[-1]
        k = pl.program_id(2)

        @pl.when(k == 0)
        def _():
            acc[...] = jnp.zeros_like(acc)

        acc[...] += _dot(a_ref[...], b_ref[...], contract)

        @pl.when(k == nk - 1)
        def _():
            epilogue(acc[...], ex, outs)

    res = pl.pallas_call(
        body,
        grid=(M // tm, N // tn, nk),
        in_specs=in_specs,
        out_specs=[pl.BlockSpec((tm, tn), lambda i, j, k: (i, j)) for _ in out_dtypes],
        out_shape=[jax.ShapeDtypeStruct((M, N), dt) for dt in out_dtypes],
        scratch_shapes=[pltpu.VMEM((tm, tn), F32)],
        compiler_params=_cparams("parallel", "parallel", "arbitrary"),
        name=name,
    )(a, b, *ex_arrays)
    return res[0] if no == 1 else res


def _rms_fwd(x, g, name):
    S, D = x.shape
    tm = _tile(S, 256)

    def body(x_ref, g_ref, h_ref):
        xv = x_ref[...]
        r = lax.rsqrt(jnp.mean(xv * xv, axis=-1, keepdims=True) + RMS_EPS)
        h_ref[...] = (xv * r * g_ref[...]).astype(h_ref.dtype)

    return pl.pallas_call(
        body,
        grid=(S // tm,),
        in_specs=[pl.BlockSpec((tm, D), lambda i: (i, 0)), pl.BlockSpec((1, D), lambda i: (0, 0))],
        out_specs=pl.BlockSpec((tm, D), lambda i: (i, 0)),
        out_shape=jax.ShapeDtypeStruct((S, D), BF16),
        compiler_params=_cparams("parallel"),
        name=name,
    )(x, g)


def _rms_bwd(dh, x, g, dres, name, with_bf16):
    S, D = x.shape
    tm = _tile(S, 256)

    def body(dh_ref, x_ref, g_ref, dres_ref, *outs):
        dx_ref, dg_ref = outs[0], outs[-1]
        xv = x_ref[...]
        r = lax.rsqrt(jnp.mean(xv * xv, axis=-1, keepdims=True) + RMS_EPS)
        xr = xv * r
        dhv = dh_ref[...].astype(F32)

        @pl.when(pl.program_id(0) == 0)
        def _():
            dg_ref[...] = jnp.zeros_like(dg_ref)

        dg_ref[...] += jnp.sum(dhv * xr, axis=0, keepdims=True)
        u = dhv * g_ref[...]
        c = jnp.mean(u * xr, axis=-1, keepdims=True)
        dx = dres_ref[...] + r * (u - xr * c)
        dx_ref[...] = dx
        if with_bf16:
            outs[1][...] = dx.astype(BF16)

    row = pl.BlockSpec((tm, D), lambda i: (i, 0))
    vec = pl.BlockSpec((1, D), lambda i: (0, 0))
    out_specs = [row] + ([row] if with_bf16 else []) + [vec]
    out_shape = [jax.ShapeDtypeStruct((S, D), F32)] + ([jax.ShapeDtypeStruct((S, D), BF16)] if with_bf16 else []) + [jax.ShapeDtypeStruct((1, D), F32)]
    return pl.pallas_call(
        body,
        grid=(S // tm,),
        in_specs=[row, row, vec, row],
        out_specs=out_specs,
        out_shape=out_shape,
        compiler_params=_cparams("arbitrary"),
        name=name,
    )(dh, x, g, dres)


def _loss_head(x3, tgt, g, name):
    S, D = x3.shape
    tm = _tile(S, 256)

    def body(x_ref, t_ref, g_ref, dx_ref, dxb_ref, dg_ref, loss_ref):
        xv = x_ref[...]
        gv = g_ref[...]
        r = lax.rsqrt(jnp.mean(xv * xv, axis=-1, keepdims=True) + RMS_EPS)
        xr = xv * r
        e = xr * gv - t_ref[...]

        @pl.when(pl.program_id(0) == 0)
        def _():
            dg_ref[...] = jnp.zeros_like(dg_ref)
            loss_ref[...] = jnp.zeros_like(loss_ref)

        loss_ref[...] += jnp.sum(e * e, axis=0, keepdims=True) * (0.5 / D)
        dy = e * (1.0 / D)
        dg_ref[...] += jnp.sum(dy * xr, axis=0, keepdims=True)
        u = dy * gv
        c = jnp.mean(u * xr, axis=-1, keepdims=True)
        dx = r * (u - xr * c)
        dx_ref[...] = dx
        dxb_ref[...] = dx.astype(BF16)

    row = pl.BlockSpec((tm, D), lambda i: (i, 0))
    vec = pl.BlockSpec((1, D), lambda i: (0, 0))
    return pl.pallas_call(
        body,
        grid=(S // tm,),
        in_specs=[row, row, vec],
        out_specs=[row, row, vec, vec],
        out_shape=[jax.ShapeDtypeStruct((S, D), F32), jax.ShapeDtypeStruct((S, D), BF16), jax.ShapeDtypeStruct((1, D), F32), jax.ShapeDtypeStruct((1, D), F32)],
        compiler_params=_cparams("arbitrary"),
        name=name,
    )(x3, tgt, g)


def _conv_taps(cur_ref, halo_ref, w_ref, b_ref, first):
    cur = cur_ref[...].astype(F32)
    halo = jnp.where(first, 0.0, halo_ref[...].astype(F32))
    xx = jnp.concatenate([halo, cur], axis=0)
    p1 = pltpu.roll(xx, 1, 0)[HALO:]
    p2 = pltpu.roll(xx, 2, 0)[HALO:]
    w = w_ref[...]
    y = b_ref[...] + w[0:1] * p2 + w[1:2] * p1 + w[2:3] * cur
    return y, (cur, p1, p2)


def _convglu_specs(S, F, tm, tn, rows_axis):
    nj = F // tn
    if rows_axis == 0:
        ij = lambda f: (lambda i, j: f(i, j))
    else:
        ij = lambda f: (lambda j, i: f(i, j))
    hb = tm // HALO
    return [
        pl.BlockSpec((tm, tn), ij(lambda i, j: (i, j))),
        pl.BlockSpec((tm, tn), ij(lambda i, j: (i, j + nj))),
        pl.BlockSpec((HALO, tn), ij(lambda i, j: (jnp.maximum(i * hb - 1, 0), j))),
        pl.BlockSpec((HALO, tn), ij(lambda i, j: (jnp.maximum(i * hb - 1, 0), j + nj))),
        pl.BlockSpec((3, tn), ij(lambda i, j: (0, j))),
        pl.BlockSpec((3, tn), ij(lambda i, j: (0, j + nj))),
        pl.BlockSpec((1, tn), ij(lambda i, j: (0, j))),
        pl.BlockSpec((1, tn), ij(lambda i, j: (0, j + nj))),
    ]


def _convglu_fwd(up, cw, cb, name):
    S, F2 = up.shape
    F = F2 // 2
    tm, tn = _tile(S, 512), _tile(F, 512)

    def body(ua, ub, ha, hb, wa, wb, ba, bb, f_ref):
        first = pl.program_id(0) == 0
        a, _ = _conv_taps(ua, ha, wa, ba, first)
        b, _ = _conv_taps(ub, hb, wb, bb, first)
        f_ref[...] = (0.5 * a * (1.0 + lax.erf(a * INV_SQRT2)) * b).astype(f_ref.dtype)

    return pl.pallas_call(
        body,
        grid=(S // tm, F // tn),
        in_specs=_convglu_specs(S, F, tm, tn, 0),
        out_specs=pl.BlockSpec((tm, tn), lambda i, j: (i, j)),
        out_shape=jax.ShapeDtypeStruct((S, F), BF16),
        compiler_params=_cparams("parallel", "parallel"),
        name=name,
    )(up, up, up, up, cw, cw, cb, cb)


def _convglu_bwd(df, up, cw, cb, name):
    S, F2 = up.shape
    F = F2 // 2
    tm, tn = _tile(S, 512), _tile(F, 512)

    def body(df_ref, ua, ub, ha, hb, wa, wb, ba, bb, da_ref, db_ref, dba_ref, dbb_ref, dwa_ref, dwb_ref):
        first = pl.program_id(1) == 0
        a, pa = _conv_taps(ua, ha, wa, ba, first)
        b, pb = _conv_taps(ub, hb, wb, bb, first)
        dfv = df_ref[...].astype(F32)
        cdf = 0.5 * (1.0 + lax.erf(a * INV_SQRT2))
        pdf = jnp.exp(-0.5 * a * a) * INV_SQRT_2PI
        da = dfv * b * (cdf + a * pdf)
        db = dfv * (a * cdf)
        da_ref[...] = da.astype(BF16)
        db_ref[...] = db.astype(BF16)

        @pl.when(first)
        def _():
            for r in (dba_ref, dbb_ref, dwa_ref, dwb_ref):
                r[...] = jnp.zeros_like(r)

        for d, taps, dbias, dw in ((da, pa, dba_ref, dwa_ref), (db, pb, dbb_ref, dwb_ref)):
            dbias[...] += jnp.sum(d, axis=0, keepdims=True)
            dw[0:1, :] += jnp.sum(d * taps[2], axis=0, keepdims=True)
            dw[1:2, :] += jnp.sum(d * taps[1], axis=0, keepdims=True)
            dw[2:3, :] += jnp.sum(d * taps[0], axis=0, keepdims=True)

    tile = pl.BlockSpec((tm, tn), lambda j, i: (i, j))
    b1 = pl.BlockSpec((1, tn), lambda j, i: (0, j))
    b3 = pl.BlockSpec((3, tn), lambda j, i: (0, j))
    return pl.pallas_call(
        body,
        grid=(F // tn, S // tm),
        in_specs=[tile] + _convglu_specs(S, F, tm, tn, 1),
        out_specs=[tile, tile, b1, b1, b3, b3],
        out_shape=[jax.ShapeDtypeStruct((S, F), BF16)] * 2 + [jax.ShapeDtypeStruct((1, F), F32)] * 2 + [jax.ShapeDtypeStruct((3, F), F32)] * 2,
        compiler_params=_cparams("parallel", "arbitrary"),
        name=name,
    )(df, up, up, up, up, cw, cw, cb, cb)


def _conv_transpose(da, db, cw, name):
    S, F = da.shape
    tm, tn = _tile(S, 512), _tile(F, 512)
    nj, ni, hb = F // tn, S // tm, tm // HALO
    n = tm + HALO

    def body(a_ref, b_ref, an_ref, bn_ref, w_ref, o_ref):
        j, i = pl.program_id(0), pl.program_id(1)

        def run(c_ref, h_ref):
            cur = c_ref[...].astype(F32)
            halo = jnp.where(i == ni - 1, 0.0, h_ref[...].astype(F32))
            xx = jnp.concatenate([cur, halo], axis=0)
            n1 = pltpu.roll(xx, n - 1, 0)[:tm]
            n2 = pltpu.roll(xx, n - 2, 0)[:tm]
            w = w_ref[...]
            o_ref[...] = (w[2:3] * cur + w[1:2] * n1 + w[0:1] * n2).astype(o_ref.dtype)

        @pl.when(j < nj)
        def _():
            run(a_ref, an_ref)

        @pl.when(j >= nj)
        def _():
            run(b_ref, bn_ref)

    ja = lambda j: jnp.minimum(j, nj - 1)
    jb = lambda j: jnp.maximum(j - nj, 0)
    nxt = lambda i: jnp.minimum((i + 1) * hb, S // HALO - 1)
    return pl.pallas_call(
        body,
        grid=(2 * nj, ni),
        in_specs=[
            pl.BlockSpec((tm, tn), lambda j, i: (i, ja(j))),
            pl.BlockSpec((tm, tn), lambda j, i: (i, jb(j))),
            pl.BlockSpec((HALO, tn), lambda j, i: (nxt(i), ja(j))),
            pl.BlockSpec((HALO, tn), lambda j, i: (nxt(i), jb(j))),
            pl.BlockSpec((3, tn), lambda j, i: (0, j)),
        ],
        out_specs=pl.BlockSpec((tm, tn), lambda j, i: (i, j)),
        out_shape=jax.ShapeDtypeStruct((S, 2 * F), BF16),
        compiler_params=_cparams("parallel", "parallel"),
        name=name,
    )(da, db, da, db, cw)


def _gate_bwd(dmixed, gates, y_pool, y_attn, name):
    S, D = dmixed.shape
    tm, tn = _tile(S, 512), _tile(D, 512)
    nj = D // tn

    def body(dm_ref, g_ref, yp_ref, ya_ref, dy_ref, dpre_ref, db_ref):
        j = pl.program_id(0)

        @pl.when(pl.program_id(1) == 0)
        def _():
            db_ref[...] = jnp.zeros_like(db_ref)

        def run(y_ref):
            dm = dm_ref[...].astype(F32)
            gv = g_ref[...]
            dy_ref[...] = (dm * gv).astype(BF16)
            dpre = dm * y_ref[...].astype(F32) * gv * (1.0 - gv)
            dpre_ref[...] = dpre.astype(BF16)
            db_ref[...] += jnp.sum(dpre, axis=0, keepdims=True)

        @pl.when(j < nj)
        def _():
            run(yp_ref)

        @pl.when(j >= nj)
        def _():
            run(ya_ref)

    tile2 = pl.BlockSpec((tm, tn), lambda j, i: (i, j))
    return pl.pallas_call(
        body,
        grid=(2 * nj, S // tm),
        in_specs=[
            pl.BlockSpec((tm, tn), lambda j, i: (i, lax.rem(j, nj))),
            tile2,
            pl.BlockSpec((tm, tn), lambda j, i: (i, jnp.minimum(j, nj - 1))),
            pl.BlockSpec((tm, tn), lambda j, i: (i, jnp.maximum(j - nj, 0))),
        ],
        out_specs=[tile2, tile2, pl.BlockSpec((1, tn), lambda j, i: (0, j))],
        out_shape=[jax.ShapeDtypeStruct((S, 2 * D), BF16), jax.ShapeDtypeStruct((S, 2 * D), BF16), jax.ShapeDtypeStruct((1, 2 * D), F32)],
        compiler_params=_cparams("parallel", "arbitrary"),
        name=name,
    )(dmixed, gates, y_pool, y_attn)


def _pool_counts(i, tm, rows, w):
    t = i * tm + lax.broadcasted_iota(jnp.int32, (rows, 1), 0)
    return jnp.minimum(t + 1, w).astype(F32)


def _pooled_groups(u_ref, uh_ref, i, tm, C):
    cur = u_ref[...]
    halo = jnp.where(i == 0, 0.0, uh_ref[...])
    xx = jnp.concatenate([halo, cur], axis=0)
    out = []
    s = xx
    for gi, w in enumerate(POOL_WINDOWS):
        s = s + pltpu.roll(s, w // 2, 0)
        tot = s[HALO:, 0:C]
        out.append(tot / _pool_counts(i, tm, tm, w) - cur[:, gi * C:(gi + 1) * C])
        s = s[:, C:] if gi + 1 < len(POOL_WINDOWS) else s
    return out


def _pool_fwd(u, wl, scale, name):
    S, PW = u.shape
    C = PW // len(POOL_WINDOWS)
    tm = _tile(S, 512)
    hb = tm // HALO

    def body(u_ref, uh_ref, wl_ref, sc_ref, o_ref):
        i = pl.program_id(0)
        pooled = _pooled_groups(u_ref, uh_ref, i, tm, C)
        for gi in range(len(POOL_WINDOWS)):
            y = _dot_nn(pooled[gi].astype(BF16), wl_ref[gi])
            o_ref[:, gi * C:(gi + 1) * C] = (y * sc_ref[:, gi * C:(gi + 1) * C]).astype(o_ref.dtype)

    return pl.pallas_call(
        body,
        grid=(S // tm,),
        in_specs=[
            pl.BlockSpec((tm, PW), lambda i: (i, 0)),
            pl.BlockSpec((HALO, PW), lambda i: (jnp.maximum(i * hb - 1, 0), 0)),
            pl.BlockSpec((len(POOL_WINDOWS), C, C), lambda i: (0, 0, 0)),
            pl.BlockSpec((1, PW), lambda i: (0, 0)),
        ],
        out_specs=pl.BlockSpec((tm, PW), lambda i: (i, 0)),
        out_shape=jax.ShapeDtypeStruct((S, PW), BF16),
        compiler_params=_cparams("parallel"),
        name=name,
    )(u, u, wl, scale)


def _pool_bwd(u, dp, wl, scale, name):
    S, PW = u.shape
    G = len(POOL_WINDOWS)
    C = PW // G
    tm = _tile(S, 512)
    hb, ni = tm // HALO, S // tm
    n = tm + HALO

    def body(u_ref, uh_ref, dp_ref, dpn_ref, wl_ref, sc_ref, du_ref, dwl_ref, dsc_ref):
        i = pl.program_id(0)

        @pl.when(i == 0)
        def _():
            dwl_ref[...] = jnp.zeros_like(dwl_ref)
            dsc_ref[...] = jnp.zeros_like(dsc_ref)

        pooled = _pooled_groups(u_ref, uh_ref, i, tm, C)
        dpc = dp_ref[...].astype(F32)
        dpn = jnp.where(i == ni - 1, 0.0, dpn_ref[...].astype(F32))
        sc = sc_ref[...]
        dyl = jnp.concatenate([dpc, dpn], axis=0) * sc
        for gi, w in enumerate(POOL_WINDOWS):
            cols = slice(gi * C, (gi + 1) * C)
            pb = pooled[gi].astype(BF16)
            ylin = _dot_nn(pb, wl_ref[gi])
            dsc_ref[:, cols] += jnp.sum(dpc[:, cols] * ylin, axis=0, keepdims=True)
            dylg = dyl[:, cols].astype(BF16)
            dwl_ref[gi] += _dot(pb, dylg[:tm], ((0,), (0,)))
            dpool = _dot_nt(dylg, wl_ref[gi])
            e = dpool / _pool_counts(i, tm, n, w)
            k = 1
            while k < w:
                e = e + pltpu.roll(e, n - k, 0)
                k *= 2
            du_ref[:, cols] = (e[:tm] - dpool[:tm]).astype(du_ref.dtype)

    return pl.pallas_call(
        body,
        grid=(ni,),
        in_specs=[
            pl.BlockSpec((tm, PW), lambda i: (i, 0)),
            pl.BlockSpec((HALO, PW), lambda i: (jnp.maximum(i * hb - 1, 0), 0)),
            pl.BlockSpec((tm, PW), lambda i: (i, 0)),
            pl.BlockSpec((HALO, PW), lambda i: (jnp.minimum((i + 1) * hb, S // HALO - 1), 0)),
            pl.BlockSpec((G, C, C), lambda i: (0, 0, 0)),
            pl.BlockSpec((1, PW), lambda i: (0, 0)),
        ],
        out_specs=[pl.BlockSpec((tm, PW), lambda i: (i, 0)), pl.BlockSpec((G, C, C), lambda i: (0, 0, 0)), pl.BlockSpec((1, PW), lambda i: (0, 0))],
        out_shape=[jax.ShapeDtypeStruct((S, PW), BF16), jax.ShapeDtypeStruct((G, C, C), F32), jax.ShapeDtypeStruct((1, PW), F32)],
        compiler_params=_cparams("arbitrary"),
        name=name,
    )(u, u, dp, dp, wl, scale)


def _band_masks():
    ii = lax.broadcasted_iota(jnp.int32, (SPAN, SPAN), 0)
    kk = lax.broadcasted_iota(jnp.int32, (SPAN, SPAN), 1)
    return ((ii + SPAN - kk).astype(F32), kk >= ii), ((ii - kk).astype(F32), kk <= ii)


def _attn_chunk(L):
    return min(4 * SPAN, L)


def _attn_fwd(qv, d, cols, g, name):
    L = qv.shape[0]
    CQ = _attn_chunk(L)
    nb = CQ // SPAN
    cq, ck, cv = cols

    def body(q_ref, k_ref, v_ref, kp_ref, vp_ref, o_ref, lse_ref):
        c = pl.program_id(1)
        (jp, mp), (jc, mc) = _band_masks()
        for h in range(HEADS_PER_GROUP):
            hs = slice(h * HEAD_DIM, (h + 1) * HEAD_DIM)
            slope = ALIBI_SLOPES[g * HEADS_PER_GROUP + h] * d
            for b in range(nb):
                rows = slice(b * SPAN, (b + 1) * SPAN)
                q = q_ref[rows, hs]
                kc, vc = k_ref[rows, hs], v_ref[rows, hs]
                if b == 0:
                    kp, vp, okp = kp_ref[:, hs], vp_ref[:, hs], jnp.logical_and(mp, c > 0)
                else:
                    prev = slice((b - 1) * SPAN, b * SPAN)
                    kp, vp, okp = k_ref[prev, hs], v_ref[prev, hs], mp
                sc = jnp.where(mc, _dot_nt(q, kc) * ATTN_SCALE - slope * jc, NEG_BIG)
                sp = jnp.where(okp, _dot_nt(q, kp) * ATTN_SCALE - slope * jp, NEG_BIG)
                m = jnp.maximum(jnp.max(sc, axis=-1, keepdims=True), jnp.max(sp, axis=-1, keepdims=True))
                pc, pp = jnp.exp(sc - m), jnp.exp(sp - m)
                l = jnp.sum(pc, axis=-1, keepdims=True) + jnp.sum(pp, axis=-1, keepdims=True)
                o = (_dot_nn(pc.astype(BF16), vc) + _dot_nn(pp.astype(BF16), vp)) / l
                o_ref[rows, hs] = o
                lse_ref[rows, hs] = jnp.broadcast_to(m + jnp.log(l), (SPAN, HEAD_DIM))

    W = GROUP_WIDTH
    pb = CQ // SPAN
    cur = lambda f: pl.BlockSpec((CQ, W), lambda r, c: (c, f(r)))
    prv = lambda f: pl.BlockSpec((SPAN, W), lambda r, c: (jnp.maximum(c * pb - 1, 0), f(r)))
    out = pl.BlockSpec((CQ, W), lambda r, c: (c, r))
    return pl.pallas_call(
        body,
        grid=(d, L // CQ),
        in_specs=[cur(cq), cur(ck), cur(cv), prv(ck), prv(cv)],
        out_specs=[out, out],
        out_shape=[jax.ShapeDtypeStruct((L, d * W), F32)] * 2,
        compiler_params=_cparams("parallel", "parallel"),
        name=name,
    )(qv, qv, qv, qv, qv)


def _attn_merge(os_, lses, name):
    S, W = os_[0].shape
    tm = _tile(S, 512)

    def body(o0, o1, o2, l0, l1, l2, y_ref, lse_ref):
        ls = [l0[...], l1[...], l2[...]]
        m = jnp.maximum(jnp.maximum(ls[0], ls[1]), ls[2])
        es = [jnp.exp(v - m) for v in ls]
        tot = es[0] + es[1] + es[2]
        y = (es[0] * o0[...] + es[1] * o1[...] + es[2] * o2[...]) / tot
        y_ref[...] = y.astype(y_ref.dtype)
        lse_ref[...] = m + jnp.log(tot)

    row = pl.BlockSpec((tm, W), lambda i: (i, 0))
    return pl.pallas_call(
        body,
        grid=(S // tm,),
        in_specs=[row] * 6,
        out_specs=[row, row],
        out_shape=[jax.ShapeDtypeStruct((S, W), BF16), jax.ShapeDtypeStruct((S, W), F32)],
        compiler_params=_cparams("parallel"),
        name=name,
    )(*os_, *lses)


def _attn_bwd(qv, dav, yv, lsev, d, cols, g, name):
    L = qv.shape[0]
    CQ = _attn_chunk(L)
    nb = CQ // SPAN
    nchunk = L // CQ
    cq, ck, cv = cols

    def body(q_ref, k_ref, v_ref, kp_ref, vp_ref, qn_ref, da_ref, dan_ref, y_ref, yn_ref, lse_ref, lsen_ref, dq_ref, dk_ref, dv_ref):
        c = pl.program_id(1)
        (jp, mp), (jc, mc) = _band_masks()
        for h in range(HEADS_PER_GROUP):
            hs = slice(h * HEAD_DIM, (h + 1) * HEAD_DIM)
            slope = ALIBI_SLOPES[g * HEADS_PER_GROUP + h] * d
            dq = [None] * nb
            dk = [None] * nb
            dv = [None] * nb

            def add(lst, idx, val):
                lst[idx] = val if lst[idx] is None else lst[idx] + val

            for qb in range(nb + 1):
                if qb < nb:
                    rows = slice(qb * SPAN, (qb + 1) * SPAN)
                    q, da, yy, lse = q_ref[rows, hs], da_ref[rows, hs], y_ref[rows, hs], lse_ref[rows, hs]
                else:
                    q, da, yy, lse = qn_ref[:, hs], dan_ref[:, hs], yn_ref[:, hs], lsen_ref[:, hs]
                dd = jnp.sum(da.astype(F32) * yy.astype(F32), axis=-1, keepdims=True)
                lse_col = lse[:, 0:1]
                for kb in (qb - 1, qb):
                    if kb >= nb:
                        continue
                    if kb < 0:
                        kk, vv, ok = kp_ref[:, hs], vp_ref[:, hs], jnp.logical_and(mp, c > 0)
                    else:
                        krows = slice(kb * SPAN, (kb + 1) * SPAN)
                        kk, vv = k_ref[krows, hs], v_ref[krows, hs]
                        ok = mc if kb == qb else (mp if qb < nb else jnp.logical_and(mp, c < nchunk - 1))
                    jj = jc if kb == qb else jp
                    s = jnp.where(ok, _dot_nt(q, kk) * ATTN_SCALE - slope * jj, NEG_BIG)
                    p = jnp.exp(s - lse_col)
                    ds = p * (_dot_nt(da, vv) - dd)
                    if qb < nb:
                        add(dq, qb, _dot_nn(ds.astype(BF16), kk))
                    if kb >= 0:
                        add(dv, kb, _dot_nn(p.T.astype(BF16), da))
                        add(dk, kb, _dot_nn(ds.T.astype(BF16), q))
            for b in range(nb):
                rows = slice(b * SPAN, (b + 1) * SPAN)
                dq_ref[rows, hs] = (dq[b] * ATTN_SCALE).astype(dq_ref.dtype)
                dk_ref[rows, hs] = (dk[b] * ATTN_SCALE).astype(dk_ref.dtype)
                dv_ref[rows, hs] = dv[b].astype(dv_ref.dtype)

    W = GROUP_WIDTH
    pb = CQ // SPAN
    nblocks = L // SPAN
    cur = lambda f: pl.BlockSpec((CQ, W), lambda r, c: (c, f(r)))
    prv = lambda f: pl.BlockSpec((SPAN, W), lambda r, c: (jnp.maximum(c * pb - 1, 0), f(r)))
    nxt = lambda f: pl.BlockSpec((SPAN, W), lambda r, c: (jnp.minimum((c + 1) * pb, nblocks - 1), f(r)))
    ident = lambda r: r
    out = pl.BlockSpec((CQ, W), lambda r, c: (c, r))
    return pl.pallas_call(
        body,
        grid=(d, nchunk),
        in_specs=[cur(cq), cur(ck), cur(cv), prv(ck), prv(cv), nxt(cq), cur(ident), nxt(ident), cur(ident), nxt(ident), cur(ident), nxt(ident)],
        out_specs=[out, out, out],
        out_shape=[jax.ShapeDtypeStruct((L, d * W), BF16)] * 3,
        compiler_params=_cparams("parallel", "parallel"),
        name=name,
    )(qv, qv, qv, qv, qv, qv, dav, dav, yv, yv, lsev, lsev)


def _row_block(R, C, bytes_per_row_elem=4, budget=1 << 20):
    if R % 8:
        return R
    best = 8
    t = 8
    while t <= R:
        if R % t == 0 and t * C * bytes_per_row_elem <= budget:
            best = t
        t += 8
    return best


def _adamw(w, g, m, v, name):
    R, C = w.shape
    tr = _row_block(R, C)
    c1 = 1.0 - ADAM_B1 ** ADAM_STEP
    c2 = 1.0 - ADAM_B2 ** ADAM_STEP

    def body(w_ref, g_ref, m_ref, v_ref, d_ref, nm_ref, nv_ref):
        gv = g_ref[...]
        nm = ADAM_B1 * m_ref[...] + (1.0 - ADAM_B1) * gv
        nv = ADAM_B2 * v_ref[...] + (1.0 - ADAM_B2) * (gv * gv)
        d_ref[...] = -ADAM_LR * ((nm / c1) / (jnp.sqrt(nv / c2) + ADAM_EPS) + ADAM_WD * w_ref[...])
        nm_ref[...] = nm
        nv_ref[...] = nv

    blk = pl.BlockSpec((tr, C), lambda i: (i, 0))
    return pl.pallas_call(
        body,
        grid=(R // tr,),
        in_specs=[blk] * 4,
        out_specs=[blk] * 3,
        out_shape=[jax.ShapeDtypeStruct((R, C), F32)] * 3,
        compiler_params=_cparams("parallel"),
        name=name,
    )(w, g, m, v)


def _sum_slots(buf, name, core=None):
    n, R, C = buf.shape
    tr = _row_block(R, C, bytes_per_row_elem=n * buf.dtype.itemsize, budget=4 << 20)
    nblk = R // tr

    def body(*refs):
        b_ref, o_ref = refs[-2], refs[-1]
        acc = b_ref[0].astype(F32)
        for s in range(1, n):
            acc = acc + b_ref[s].astype(F32)
        o_ref[...] = acc

    if core is None:
        return pl.pallas_call(
            body,
            grid=(nblk,),
            in_specs=[pl.BlockSpec((n, tr, C), lambda i: (0, i, 0))],
            out_specs=pl.BlockSpec((tr, C), lambda i: (i, 0)),
            out_shape=jax.ShapeDtypeStruct((R, C), F32),
            compiler_params=_cparams("parallel"),
            name=name,
        )(buf)
    return pl.pallas_call(
        body,
        grid_spec=pltpu.PrefetchScalarGridSpec(
            num_scalar_prefetch=1,
            grid=(nblk,),
            in_specs=[pl.BlockSpec((n, tr, C), lambda i, c: (0, i, 0))],
            out_specs=pl.BlockSpec((tr, C), lambda i, c: (i + c[0] * nblk, 0)),
        ),
        out_shape=jax.ShapeDtypeStruct((2 * R, C), F32),
        compiler_params=_cparams("parallel"),
        name=name,
    )(core, buf)


def _position():
    return lax.axis_index("x"), lax.axis_index("y"), lax.axis_index("c")


def _shard_slice(ref, axis, idx, size):
    start = idx * size
    if axis == ref.ndim - 1:
        start = pl.multiple_of(start, 128)
    ix = [slice(None)] * ref.ndim
    ix[axis] = pl.ds(start, size)
    return ref.at[tuple(ix)]


def _gather_chips(shards, axes, name):
    n = len(shards)
    fulls = []
    for s, ax in zip(shards, axes):
        shp = list(s.shape)
        shp[ax] *= N_CHIPS
        fulls.append(jax.ShapeDtypeStruct(tuple(shp), s.dtype))

    def body(*refs):
        ins, outs = refs[:n], refs[n:2 * n]
        send_sems, recv_sems, loc_sems = refs[2 * n:]
        x, y, c = _position()
        me = 2 * x + y
        peers = [(1 - x, y), (x, 1 - y), (1 - x, 1 - y)]
        started = []
        for t in range(n):
            size = ins[t].shape[axes[t]]
            mine = _shard_slice(outs[t], axes[t], me, size)
            loc = pltpu.make_async_copy(ins[t], mine, loc_sems.at[t])
            loc.start()
            started.append(loc)
            for p, (px, py) in enumerate(peers):
                cp = pltpu.make_async_remote_copy(src_ref=ins[t], dst_ref=mine, send_sem=send_sems.at[3 * t + p], recv_sem=recv_sems.at[3 * t + p],
                                                  device_id=(px, py, c), device_id_type=MESH)
                cp.start()
        for t in range(n):
            size = ins[t].shape[axes[t]]
            started[t].wait()
            for p, (px, py) in enumerate(peers):
                theirs = _shard_slice(outs[t], axes[t], 2 * px + py, size)
                cp = pltpu.make_async_remote_copy(src_ref=ins[t], dst_ref=theirs, send_sem=send_sems.at[3 * t + p], recv_sem=recv_sems.at[3 * t + p],
                                                  device_id=(px, py, c), device_id_type=MESH)
                cp.wait_send()
                cp.wait_recv()

    return pl.pallas_call(
        body,
        in_specs=[ANY] * n,
        out_specs=[ANY] * n,
        out_shape=fulls,
        scratch_shapes=[pltpu.SemaphoreType.DMA((3 * n,)), pltpu.SemaphoreType.DMA((3 * n,)), pltpu.SemaphoreType.DMA((n,))],
        name=name,
    )(*shards)


def _scatter_pieces(grads, axes, small, name):
    n = len(grads)
    pieces = []
    for gr, ax in zip(grads, axes):
        R, C = gr.shape
        pieces.append((R // (2 * N_CHIPS), C) if ax == 0 else (R // 2, C // N_CHIPS))
    outs_shape = [jax.ShapeDtypeStruct((N_DEV,) + p, gr.dtype) for p, gr in zip(pieces, grads)]
    outs_shape.append(jax.ShapeDtypeStruct((N_DEV,) + small.shape, small.dtype))
    nt = n + 1

    def body(*refs):
        ins, outs = refs[:nt], refs[nt:2 * nt]
        send_sems, recv_sems, loc_sems = refs[2 * nt:]
        x, y, c = _position()
        me = 4 * x + 2 * y + c

        def piece_of(t, chip, core):
            if t == n:
                return ins[t]
            pr, pc = pieces[t]
            if axes[t] == 0:
                return ins[t].at[pl.ds((chip * 2 + core) * pr, pr), :]
            return ins[t].at[pl.ds(core * pr, pr), pl.ds(pl.multiple_of(chip * pc, 128), pc)]

        def peer(k):
            fx, fy, fc = (k >> 2) & 1, (k >> 1) & 1, k & 1
            px = (1 - x) if fx else x
            py = (1 - y) if fy else y
            pc = (1 - c) if fc else c
            return px, py, pc

        local = []
        for t in range(nt):
            loc = pltpu.make_async_copy(piece_of(t, 2 * x + y, c), outs[t].at[me], loc_sems.at[t])
            loc.start()
            local.append(loc)
            for k in range(1, N_DEV):
                px, py, pc = peer(k)
                cp = pltpu.make_async_remote_copy(src_ref=piece_of(t, 2 * px + py, pc), dst_ref=outs[t].at[me], send_sem=send_sems.at[7 * t + k - 1],
                                                  recv_sem=recv_sems.at[7 * t + k - 1], device_id=(px, py, pc), device_id_type=MESH)
                cp.start()
        for t in range(nt):
            local[t].wait()
            for k in range(1, N_DEV):
                px, py, pc = peer(k)
                cp = pltpu.make_async_remote_copy(src_ref=piece_of(t, 2 * px + py, pc), dst_ref=outs[t].at[4 * px + 2 * py + pc], send_sem=send_sems.at[7 * t + k - 1],
                                                  recv_sem=recv_sems.at[7 * t + k - 1], device_id=(px, py, pc), device_id_type=MESH)
                cp.wait_send()
                cp.wait_recv()

    return pl.pallas_call(
        body,
        in_specs=[ANY] * nt,
        out_specs=[ANY] * nt,
        out_shape=outs_shape,
        scratch_shapes=[pltpu.SemaphoreType.DMA((7 * nt,)), pltpu.SemaphoreType.DMA((7 * nt,)), pltpu.SemaphoreType.DMA((nt,))],
        name=name,
    )(*grads, small)


EXCHANGE_CHUNKS = 2


def _exchange_halves(shards, name):
    n = len(shards)
    nc = EXCHANGE_CHUNKS

    def body(*refs):
        outs = refs[n:2 * n]
        send_sems, recv_sems = refs[2 * n:]
        x, y, c = _position()

        def chunk(t, core, q):
            rows = outs[t].shape[0] // (2 * nc)
            return outs[t].at[pl.ds((core * nc + q) * rows, rows), :]

        def copy(t, core, q):
            return pltpu.make_async_remote_copy(src_ref=chunk(t, core, q), dst_ref=chunk(t, core, q), send_sem=send_sems.at[nc * t + q],
                                                recv_sem=recv_sems.at[nc * t + q], device_id=(x, y, 1 - c), device_id_type=MESH)

        for t in range(n):
            for q in range(nc):
                copy(t, c, q).start()
        for t in range(n):
            for q in range(nc):
                copy(t, c, q).wait_send()
                copy(t, 1 - c, q).wait_recv()

    return pl.pallas_call(
        body,
        in_specs=[ANY] * n,
        out_specs=[ANY] * n,
        out_shape=[jax.ShapeDtypeStruct(s.shape, s.dtype) for s in shards],
        input_output_aliases={t: t for t in range(n)},
        scratch_shapes=[pltpu.SemaphoreType.DMA((nc * n,)), pltpu.SemaphoreType.DMA((nc * n,))],
        name=name,
    )(*shards)


def _dilated_view(a, d):
    S, W = a.shape
    return a.reshape(S // d, d * W)


def _natural_view(a, d):
    L, W = a.shape
    return a.reshape(L * d, W // d)


def _local_step(x, tgt, w):
    S, D = x.shape
    PW = w["pool_scale"].shape[1]
    F = w["w_down"].shape[0]
    o_q = PW
    o_g = PW + 3 * ATTN_WIDTH
    QKV = 3 * ATTN_WIDTH

    h1 = _rms_fwd(x, w["g_mix"], "rms1")
    proj_tiles = (_tile(S, 1024), 512, D)
    u = _mm(h1, w["w_in"], mode="nn", dims=(S, PW, D), tiles=proj_tiles, out_dtypes=(F32,), name="proj_u")
    qkv = _mm(h1, w["w_in"], mode="nn", dims=(S, QKV, D), tiles=proj_tiles, b_off=(0, o_q), name="proj_qkv")

    def gate_epilogue(acc, ex, outs):
        outs[0][...] = 1.0 / (1.0 + jnp.exp(-(acc + ex[0][...])))

    gates = _mm(h1, w["w_in"], mode="nn", dims=(S, 2 * D, D), tiles=proj_tiles, b_off=(0, o_g), out_dtypes=(F32,), epilogue=gate_epilogue,
                extras=[(w["b_gate"], "n", (0, 0))], name="proj_gates")

    pool_out = _pool_fwd(u, w["w_pool_lin"], w["pool_scale"], "pool_fwd")

    ident3 = (lambda r: 0, lambda r: 3, lambda r: 6)
    trip = (lambda r: 3 * r, lambda r: 3 * r + 1, lambda r: 3 * r + 2)
    qviews, os_, lses = [], [], []
    for gi, (_, d) in enumerate(ATTN_GROUPS):
        if d == 1:
            assert gi == 0
            qv, cols = qkv, ident3
        else:
            qv = _dilated_view(qkv.reshape(S, 9, GROUP_WIDTH)[:, gi::3, :].reshape(S, 3 * GROUP_WIDTH), d)
            cols = trip
        qviews.append((qv, cols))
        o, lse = _attn_fwd(qv, d, cols, gi, f"attn_fwd{gi}")
        os_.append(_natural_view(o, d))
        lses.append(_natural_view(lse, d))
    attn, lse_tot = _attn_merge(os_, lses, "attn_merge")

    y_pool = _mm(pool_out, w["w_pool_out"], mode="nn", dims=(S, D, PW), name="y_pool")

    def mix_epilogue(acc, ex, outs):
        outs[0][...] = acc.astype(BF16)
        outs[1][...] = (ex[0][...] * ex[2][...].astype(F32) + ex[1][...] * acc).astype(BF16)

    y_attn, mixed = _mm(attn, w["w_attn_out"], mode="nn", dims=(S, D, GROUP_WIDTH), out_dtypes=(BF16, BF16), epilogue=mix_epilogue,
                        extras=[(gates, "mn", (0, 0)), (gates, "mn", (0, D)), (y_pool, "mn", (0, 0))], name="y_attn_mix")

    def residual_epilogue(acc, ex, outs):
        outs[0][...] = ex[0][...] + acc

    x2 = _mm(mixed, w["w_out"], mode="nn", dims=(S, D, D), out_dtypes=(F32,), epilogue=residual_epilogue, extras=[(x, "mn", (0, 0))], name="out_proj")

    h2 = _rms_fwd(x2, w["g_ffn"], "rms2")
    up = _mm(h2, w["w_up"], mode="nn", dims=(S, 2 * F, D), name="up_proj")
    f = _convglu_fwd(up, w["conv_w"], w["conv_b"], "convglu_fwd")
    x3 = _mm(f, w["w_down"], mode="nn", dims=(S, D, F), out_dtypes=(F32,), epilogue=residual_epilogue, extras=[(x2, "mn", (0, 0))], name="down_proj")

    g = {}
    dx3, dx3b, g["g_final"], loss_cols = _loss_head(x3, tgt, w["g_final"], "loss_head")

    g["w_down"] = _mm(f, dx3b, mode="tn", dims=(F, D, S), name="dw_down")
    df = _mm(dx3b, w["w_down"], mode="nt", dims=(S, F, D), name="d_f")
    da, db, dcb_a, dcb_b, dcw_a, dcw_b = _convglu_bwd(df, up, w["conv_w"], w["conv_b"], "convglu_bwd")
    g["conv_b"] = jnp.concatenate([dcb_a, dcb_b], axis=1)
    g["conv_w"] = jnp.concatenate([dcw_a, dcw_b], axis=1)
    dup = _conv_transpose(da, db, w["conv_w"], "conv_transpose")
    g["w_up"] = _mm(h2, dup, mode="tn", dims=(D, 2 * F, S), name="dw_up")
    dh2 = _mm(dup, w["w_up"], mode="nt", dims=(S, D, 2 * F), name="d_h2")
    dx2, dx2b, g["g_ffn"] = _rms_bwd(dh2, x2, w["g_ffn"], dx3, "rms2_bwd", True)

    g["w_out"] = _mm(mixed, dx2b, mode="tn", dims=(D, D, S), name="dw_out")
    dmixed = _mm(dx2b, w["w_out"], mode="nt", dims=(S, D, D), name="d_mixed")
    dy_both, dpre, g["b_gate"] = _gate_bwd(dmixed, gates, y_pool, y_attn, "gate_bwd")

    g["w_pool_out"] = _mm(pool_out, dy_both, mode="tn", dims=(PW, D, S), name="dw_pool_out")
    g["w_attn_out"] = _mm(attn, dy_both, mode="tn", dims=(GROUP_WIDTH, D, S), b_off=(0, D), name="dw_attn_out")
    dpool = _mm(dy_both, w["w_pool_out"], mode="nt", dims=(S, PW, D), name="d_pool")
    dattn = _mm(dy_both, w["w_attn_out"], mode="nt", dims=(S, GROUP_WIDTH, D), a_off=(0, D), name="d_attn")

    du, g["w_pool_lin"], g["pool_scale"] = _pool_bwd(u, dpool, w["w_pool_lin"], w["pool_scale"], "pool_bwd")

    dqs, dks, dvs = [], [], []
    for gi, (_, d) in enumerate(ATTN_GROUPS):
        qv, cols = qviews[gi]
        dq, dk, dv = _attn_bwd(qv, _dilated_view(dattn, d), _dilated_view(attn, d), _dilated_view(lse_tot, d), d, cols, gi, f"attn_bwd{gi}")
        dqs.append(_natural_view(dq, d))
        dks.append(_natural_view(dk, d))
        dvs.append(_natural_view(dv, d))
    dproj = jnp.concatenate([du] + dqs + dks + dvs + [dpre], axis=1)
    IN = dproj.shape[1]

    g["w_in"] = _mm(h1, dproj, mode="tn", dims=(D, IN, S), name="dw_in")
    dh1 = _mm(dproj, w["w_in"], mode="nt", dims=(S, D, IN), tiles=(_tile(S, 1024), _tile(D, 2048), _tile(IN, 512)), name="d_h1")
    (grad_x, g["g_mix"]) = _rms_bwd(dh1, x, w["g_mix"], dx2, "rms1_bwd", False)
    return loss_cols, grad_x, g


BIG = ("w_in", "w_pool_out", "w_attn_out", "w_out", "w_up", "w_down")
BIG_AXIS = {"w_in": 1, "w_pool_out": 1, "w_attn_out": 1, "w_out": 0, "w_up": 1, "w_down": 0}
SMALL = ("g_mix", "b_gate", "w_pool_lin", "pool_scale", "g_ffn", "conv_w", "conv_b", "g_final")
SMALL_COLS = 1024
ORDER = ("g_mix", "w_in", "b_gate", "w_pool_lin", "pool_scale", "w_pool_out", "w_attn_out", "w_out", "g_ffn", "w_up", "conv_w", "conv_b", "w_down", "g_final")


def kernel(x, g_mix, w_in, b_gate, w_pool_lin, pool_scale, w_pool_out, w_attn_out, w_out, g_ffn, w_up, conv_w, conv_b, w_down, g_final, loss_target, m_g_mix, m_w_in, m_b_gate, m_w_pool_lin, m_pool_scale, m_w_pool_out, m_w_attn_out, m_w_out, m_g_ffn, m_w_up, m_conv_w, m_conv_b, m_w_down, m_g_final, v_g_mix, v_w_in, v_b_gate, v_w_pool_lin, v_pool_scale, v_w_pool_out, v_w_attn_out, v_w_out, v_g_ffn, v_w_up, v_conv_w, v_conv_b, v_w_down, v_g_final):
    shard = dict(g_mix=g_mix, w_in=w_in, b_gate=b_gate, w_pool_lin=w_pool_lin, pool_scale=pool_scale, w_pool_out=w_pool_out, w_attn_out=w_attn_out,
                 w_out=w_out, g_ffn=g_ffn, w_up=w_up, conv_w=conv_w, conv_b=conv_b, w_down=w_down, g_final=g_final)
    mom = dict(g_mix=m_g_mix, w_in=m_w_in, b_gate=m_b_gate, w_pool_lin=m_w_pool_lin, pool_scale=m_pool_scale, w_pool_out=m_w_pool_out, w_attn_out=m_w_attn_out,
               w_out=m_w_out, g_ffn=m_g_ffn, w_up=m_w_up, conv_w=m_conv_w, conv_b=m_conv_b, w_down=m_w_down, g_final=m_g_final)
    vel = dict(g_mix=v_g_mix, w_in=v_w_in, b_gate=v_b_gate, w_pool_lin=v_w_pool_lin, pool_scale=v_pool_scale, w_pool_out=v_w_pool_out, w_attn_out=v_w_attn_out,
               w_out=v_w_out, g_ffn=v_g_ffn, w_up=v_w_up, conv_w=v_conv_w, conv_b=v_conv_b, w_down=v_w_down, g_final=v_g_final)
    xi, yi = lax.axis_index("x"), lax.axis_index("y")
    chip = 2 * xi + yi
    S, D = x.shape[1], x.shape[2]

    names = list(BIG) + ["w_pool_lin", "conv_w"]
    locs = [shard[k][0].astype(BF16) for k in BIG] + [shard["w_pool_lin"][0].astype(BF16), shard["conv_w"][0]]
    axes = [BIG_AXIS[k] for k in BIG] + [1, 1]
    full = dict(zip(names, _gather_chips(locs, axes, "comm_gather_weights")))
    for k in ("g_mix", "b_gate", "pool_scale", "g_ffn", "conv_b"):
        full[k] = shard[k]
    full["g_final"] = shard["g_final"].reshape(1, D)

    loss_cols, grad_x, gr = _local_step(x[0], loss_target[0], full)

    parts = [loss_cols.reshape(-1)] + [gr[k].astype(F32).reshape(-1) for k in SMALL]
    sizes = [p.shape[0] for p in parts]
    flat = jnp.concatenate(parts)
    rows = -(-flat.shape[0] // (8 * SMALL_COLS)) * 8
    small = jnp.pad(flat, (0, rows * SMALL_COLS - flat.shape[0])).reshape(rows, SMALL_COLS)

    bufs = _scatter_pieces([gr[k] for k in BIG], [BIG_AXIS[k] for k in BIG], small, "comm_scatter_grads")
    core = lax.axis_index("c").astype(jnp.int32).reshape(1)
    halves = [_sum_slots(b, f"sum_{k}", core) for b, k in zip(bufs[:-1], BIG)]
    small_sum = _sum_slots(bufs[-1], "sum_small").reshape(-1)
    wholes = _exchange_halves(halves, "comm_exchange_halves")

    grads = {}
    for k, whole in zip(BIG, wholes):
        grads[k] = whole.reshape(shard[k].shape)
    off = 0
    segs = []
    for sz in sizes:
        segs.append(small_sum[off:off + sz])
        off += sz
    loss = jnp.sum(segs[0])
    for k, seg in zip(SMALL, segs[1:]):
        fullg = seg.reshape(gr[k].shape)
        if k == "w_pool_lin":
            n = shard[k].shape[2]
            fullg = lax.dynamic_slice_in_dim(fullg, chip * n, n, axis=1)
        elif k == "conv_w":
            n = shard[k].shape[2]
            fullg = lax.dynamic_slice_in_dim(fullg, chip * n, n, axis=1)
        grads[k] = fullg.reshape(shard[k].shape)

    deltas, new_m, new_v = {}, {}, {}
    for k in ORDER:
        shp = shard[k].shape
        two_d = (-1, shp[-1])
        dl, nm, nv = _adamw(shard[k].reshape(two_d), grads[k].reshape(two_d), mom[k].reshape(two_d), vel[k].reshape(two_d), f"adamw_{k}")
        deltas[k], new_m[k], new_v[k] = dl.reshape(shp), nm.reshape(shp), nv.reshape(shp)

    return (loss, grad_x[None], *[grads[k] for k in ORDER], *[deltas[k] for k in ORDER], *[new_m[k] for k in ORDER], *[new_v[k] for k in ORDER])
```

```python
import functools
import math

import jax
import jax.numpy as jnp
from jax import lax
from jax.experimental import pallas as pl
from jax.experimental.pallas import tpu as pltpu

F32 = jnp.float32
BF16 = jnp.bfloat16

RMS_EPS = 1e-6
POOL_WINDOWS = (2, 4, 8, 16)
ATTN_GROUPS = ((128, 1), (512, 4), (2048, 16))
HEADS_PER_GROUP = 4
HEAD_DIM = 128
N_ATTN_HEADS = HEADS_PER_GROUP * len(ATTN_GROUPS)
SPAN = 128
GROUP_WIDTH = HEADS_PER_GROUP * HEAD_DIM
ATTN_WIDTH = N_ATTN_HEADS * HEAD_DIM
ATTN_SCALE = HEAD_DIM ** -0.5
NEG_BIG = -1e30
ALIBI_SLOPES = tuple(2.0 ** (-8.0 * (h + 1) / N_ATTN_HEADS) for h in range(N_ATTN_HEADS))

ADAM_LR = 0.001
ADAM_B1 = 0.9
ADAM_B2 = 0.999
ADAM_EPS = 1e-08
ADAM_WD = 0.01
ADAM_STEP = 10

INV_SQRT2 = 1.0 / math.sqrt(2.0)
INV_SQRT_2PI = 1.0 / math.sqrt(2.0 * math.pi)

HALO = 16
VMEM_LIMIT = 56 * 1024 * 1024
N_CHIPS = 4
N_DEV = 8
MESH = pl.DeviceIdType.MESH
ANY = pl.BlockSpec(memory_space=pl.ANY)


def _cparams(*sem):
    return pltpu.CompilerParams(dimension_semantics=sem, vmem_limit_bytes=VMEM_LIMIT)


def _tile(n, pref, mult=128):
    t = (min(pref, n) // mult) * mult
    while t >= mult:
        if n % t == 0:
            return t
        t -= mult
    return n


def _dot(a, b, contract):
    return lax.dot_general(a, b, (contract, ((), ())), preferred_element_type=F32)


def _dot_nn(a, b):
    return _dot(a, b, ((1,), (0,)))


def _dot_nt(a, b):
    return _dot(a, b, ((1,), (1,)))


def _mm(a, b, *, mode, dims, name, tiles=None, out_dtypes=(BF16,), epilogue=None, extras=(), a_off=(0, 0), b_off=(0, 0)):
    M, N, K = dims
    if tiles is None:
        tiles = (_tile(M, 1408), _tile(N, 2816), _tile(K, 512)) if mode == "tn" else (_tile(M, 1024), _tile(N, 1536), _tile(K, 1408))
    tm, tn, tk = tiles
    assert M % tm == 0 and N % tn == 0 and K % tk == 0, (name, dims, tiles)
    nk = K // tk
    if mode == "nn":
        ab, bb, contract = (tm, tk), (tk, tn), ((1,), (0,))
        amap = lambda i, j, k: (i + a_off[0] // tm, k + a_off[1] // tk)
        bmap = lambda i, j, k: (k + b_off[0] // tk, j + b_off[1] // tn)
    elif mode == "nt":
        ab, bb, contract = (tm, tk), (tn, tk), ((1,), (1,))
        amap = lambda i, j, k: (i + a_off[0] // tm, k + a_off[1] // tk)
        bmap = lambda i, j, k: (j + b_off[0] // tn, k + b_off[1] // tk)
    else:
        ab, bb, contract = (tk, tm), (tk, tn), ((0,), (0,))
        amap = lambda i, j, k: (k + a_off[0] // tk, i + a_off[1] // tm)
        bmap = lambda i, j, k: (k + b_off[0] // tk, j + b_off[1] // tn)
    assert a_off[0] % ab[0] == 0 and a_off[1] % ab[1] == 0 and b_off[0] % bb[0] == 0 and b_off[1] % bb[1] == 0, name
    in_specs = [pl.BlockSpec(ab, amap), pl.BlockSpec(bb, bmap)]
    ex_arrays = []
    for arr, kind, off in extras:
        if kind == "mn":
            assert off[0] % tm == 0 and off[1] % tn == 0, name
            in_specs.append(pl.BlockSpec((tm, tn), lambda i, j, k, off=off: (i + off[0] // tm, j + off[1] // tn)))
        else:
            assert off[1] % tn == 0, name
            in_specs.append(pl.BlockSpec((1, tn), lambda i, j, k, off=off: (0, j + off[1] // tn)))
        ex_arrays.append(arr)
    ne, no = len(ex_arrays), len(out_dtypes)
    if epilogue is None:
        def epilogue(acc, ex, outs):
            outs[0][...] = acc.astype(outs[0].dtype)

    def body(*refs):
        a_ref, b_ref = refs[0], refs[1]
        ex, outs, acc = refs[2:2 + ne], refs[2 + ne:2 + ne + no], refs[-1]
        k = pl.program_id(2)

        @pl.when(k == 0)
        def _():
            acc[...] = jnp.zeros_like(acc)

        acc[...] += _dot(a_ref[...], b_ref[...], contract)

        @pl.when(k == nk - 1)
        def _():
            epilogue(acc[...], ex, outs)

    res = pl.pallas_call(
        body,
        grid=(M // tm, N // tn, nk),
        in_specs=in_specs,
        out_specs=[pl.BlockSpec((tm, tn), lambda i, j, k: (i, j)) for _ in out_dtypes],
        out_shape=[jax.ShapeDtypeStruct((M, N), dt) for dt in out_dtypes],
        scratch_shapes=[pltpu.VMEM((tm, tn), F32)],
        compiler_params=_cparams("parallel", "parallel", "arbitrary"),
        name=name,
    )(a, b, *ex_arrays)
    return res[0] if no == 1 else res


def _rms_fwd(x, g, name):
    S, D = x.shape
    tm = _tile(S, 256)

    def body(x_ref, g_ref, h_ref):
        xv = x_ref[...]
        r = lax.rsqrt(jnp.mean(xv * xv, axis=-1, keepdims=True) + RMS_EPS)
        h_ref[...] = (xv * r * g_ref[...]).astype(h_ref.dtype)

    return pl.pallas_call(
        body,
        grid=(S // tm,),
        in_specs=[pl.BlockSpec((tm, D), lambda i: (i, 0)), pl.BlockSpec((1, D), lambda i: (0, 0))],
        out_specs=pl.BlockSpec((tm, D), lambda i: (i, 0)),
        out_shape=jax.ShapeDtypeStruct((S, D), BF16),
        compiler_params=_cparams("parallel"),
        name=name,
    )(x, g)


def _rms_bwd(dh, x, g, dres, name, with_bf16):
    S, D = x.shape
    tm = _tile(S, 256)

    def body(dh_ref, x_ref, g_ref, dres_ref, *outs):
        dx_ref, dg_ref = outs[0], outs[-1]
        xv = x_ref[...]
        r = lax.rsqrt(jnp.mean(xv * xv, axis=-1, keepdims=True) + RMS_EPS)
        xr = xv * r
        dhv = dh_ref[...].astype(F32)

        @pl.when(pl.program_id(0) == 0)
        def _():
            dg_ref[...] = jnp.zeros_like(dg_ref)

        dg_ref[...] += jnp.sum(dhv * xr, axis=0, keepdims=True)
        u = dhv * g_ref[...]
        c = jnp.mean(u * xr, axis=-1, keepdims=True)
        dx = dres_ref[...] + r * (u - xr * c)
        dx_ref[...] = dx
        if with_bf16:
            outs[1][...] = dx.astype(BF16)

    row = pl.BlockSpec((tm, D), lambda i: (i, 0))
    vec = pl.BlockSpec((1, D), lambda i: (0, 0))
    out_specs = [row] + ([row] if with_bf16 else []) + [vec]
    out_shape = [jax.ShapeDtypeStruct((S, D), F32)] + ([jax.ShapeDtypeStruct((S, D), BF16)] if with_bf16 else []) + [jax.ShapeDtypeStruct((1, D), F32)]
    return pl.pallas_call(
        body,
        grid=(S // tm,),
        in_specs=[row, row, vec, row],
        out_specs=out_specs,
        out_shape=out_shape,
        compiler_params=_cparams("arbitrary"),
        name=name,
    )(dh, x, g, dres)


def _loss_head(x3, tgt, g, name):
    S, D = x3.shape
    tm = _tile(S, 256)

    def body(x_ref, t_ref, g_ref, dx_ref, dxb_ref, dg_ref, loss_ref):
        xv = x_ref[...]
        gv = g_ref[...]
        r = lax.rsqrt(jnp.mean(xv * xv, axis=-1, keepdims=True) + RMS_EPS)
        xr = xv * r
        e = xr * gv - t_ref[...]

        @pl.when(pl.program_id(0) == 0)
        def _():
            dg_ref[...] = jnp.zeros_like(dg_ref)
            loss_ref[...] = jnp.zeros_like(loss_ref)

        loss_ref[...] += jnp.sum(e * e, axis=0, keepdims=True) * (0.5 / D)
        dy = e * (1.0 / D)
        dg_ref[...] += jnp.sum(dy * xr, axis=0, keepdims=True)
        u = dy * gv
        c = jnp.mean(u * xr, axis=-1, keepdims=True)
        dx = r * (u - xr * c)
        dx_ref[...] = dx
        dxb_ref[...] = dx.astype(BF16)

    row = pl.BlockSpec((tm, D), lambda i: (i, 0))
    vec = pl.BlockSpec((1, D), lambda i: (0, 0))
    return pl.pallas_call(
        body,
        grid=(S // tm,),
        in_specs=[row, row, vec],
        out_specs=[row, row, vec, vec],
        out_shape=[jax.ShapeDtypeStruct((S, D), F32), jax.ShapeDtypeStruct((S, D), BF16), jax.ShapeDtypeStruct((1, D), F32), jax.ShapeDtypeStruct((1, D), F32)],
        compiler_params=_cparams("arbitrary"),
        name=name,
    )(x3, tgt, g)


def _conv_taps(cur_ref, halo_ref, w_ref, b_ref, first):
    cur = cur_ref[...].astype(F32)
    halo = jnp.where(first, 0.0, halo_ref[...].astype(F32))
    xx = jnp.concatenate([halo, cur], axis=0)
    p1 = pltpu.roll(xx, 1, 0)[HALO:]
    p2 = pltpu.roll(xx, 2, 0)[HALO:]
    w = w_ref[...]
    y = b_ref[...] + w[0:1] * p2 + w[1:2] * p1 + w[2:3] * cur
    return y, (cur, p1, p2)


def _convglu_specs(S, F, tm, tn, rows_axis):
    nj = F // tn
    if rows_axis == 0:
        ij = lambda f: (lambda i, j: f(i, j))
    else:
        ij = lambda f: (lambda j, i: f(i, j))
    hb = tm // HALO
    return [
        pl.BlockSpec((tm, tn), ij(lambda i, j: (i, j))),
        pl.BlockSpec((tm, tn), ij(lambda i, j: (i, j + nj))),
        pl.BlockSpec((HALO, tn), ij(lambda i, j: (jnp.maximum(i * hb - 1, 0), j))),
        pl.BlockSpec((HALO, tn), ij(lambda i, j: (jnp.maximum(i * hb - 1, 0), j + nj))),
        pl.BlockSpec((3, tn), ij(lambda i, j: (0, j))),
        pl.BlockSpec((3, tn), ij(lambda i, j: (0, j + nj))),
        pl.BlockSpec((1, tn), ij(lambda i, j: (0, j))),
        pl.BlockSpec((1, tn), ij(lambda i, j: (0, j + nj))),
    ]


def _convglu_fwd(up, cw, cb, name):
    S, F2 = up.shape
    F = F2 // 2
    tm, tn = _tile(S, 512), _tile(F, 512)

    def body(ua, ub, ha, hb, wa, wb, ba, bb, f_ref):
        first = pl.program_id(0) == 0
        a, _ = _conv_taps(ua, ha, wa, ba, first)
        b, _ = _conv_taps(ub, hb, wb, bb, first)
        f_ref[...] = (0.5 * a * (1.0 + lax.erf(a * INV_SQRT2)) * b).astype(f_ref.dtype)

    return pl.pallas_call(
        body,
        grid=(S // tm, F // tn),
        in_specs=_convglu_specs(S, F, tm, tn, 0),
        out_specs=pl.BlockSpec((tm, tn), lambda i, j: (i, j)),
        out_shape=jax.ShapeDtypeStruct((S, F), BF16),
        compiler_params=_cparams("parallel", "parallel"),
        name=name,
    )(up, up, up, up, cw, cw, cb, cb)


def _convglu_bwd(df, up, cw, cb, name):
    S, F2 = up.shape
    F = F2 // 2
    tm, tn = _tile(S, 512), _tile(F, 512)

    def body(df_ref, ua, ub, ha, hb, wa, wb, ba, bb, da_ref, db_ref, dba_ref, dbb_ref, dwa_ref, dwb_ref):
        first = pl.program_id(1) == 0
        a, pa = _conv_taps(ua, ha, wa, ba, first)
        b, pb = _conv_taps(ub, hb, wb, bb, first)
        dfv = df_ref[...].astype(F32)
        cdf = 0.5 * (1.0 + lax.erf(a * INV_SQRT2))
        pdf = jnp.exp(-0.5 * a * a) * INV_SQRT_2PI
        da = dfv * b * (cdf + a * pdf)
        db = dfv * (a * cdf)
        da_ref[...] = da.astype(BF16)
        db_ref[...] = db.astype(BF16)

        @pl.when(first)
        def _():
            for r in (dba_ref, dbb_ref, dwa_ref, dwb_ref):
                r[...] = jnp.zeros_like(r)

        for d, taps, dbias, dw in ((da, pa, dba_ref, dwa_ref), (db, pb, dbb_ref, dwb_ref)):
            dbias[...] += jnp.sum(d, axis=0, keepdims=True)
            dw[0:1, :] += jnp.sum(d * taps[2], axis=0, keepdims=True)
            dw[1:2, :] += jnp.sum(d * taps[1], axis=0, keepdims=True)
            dw[2:3, :] += jnp.sum(d * taps[0], axis=0, keepdims=True)

    tile = pl.BlockSpec((tm, tn), lambda j, i: (i, j))
    b1 = pl.BlockSpec((1, tn), lambda j, i: (0, j))
    b3 = pl.BlockSpec((3, tn), lambda j, i: (0, j))
    return pl.pallas_call(
        body,
        grid=(F // tn, S // tm),
        in_specs=[tile] + _convglu_specs(S, F, tm, tn, 1),
        out_specs=[tile, tile, b1, b1, b3, b3],
        out_shape=[jax.ShapeDtypeStruct((S, F), BF16)] * 2 + [jax.ShapeDtypeStruct((1, F), F32)] * 2 + [jax.ShapeDtypeStruct((3, F), F32)] * 2,
        compiler_params=_cparams("parallel", "arbitrary"),
        name=name,
    )(df, up, up, up, up, cw, cw, cb, cb)


def _conv_transpose(da, db, cw, name):
    S, F = da.shape
    tm, tn = _tile(S, 512), _tile(F, 512)
    nj, ni, hb = F // tn, S // tm, tm // HALO
    n = tm + HALO

    def body(a_ref, b_ref, an_ref, bn_ref, w_ref, o_ref):
        j, i = pl.program_id(0), pl.program_id(1)

        def run(c_ref, h_ref):
            cur = c_ref[...].astype(F32)
            halo = jnp.where(i == ni - 1, 0.0, h_ref[...].astype(F32))
            xx = jnp.concatenate([cur, halo], axis=0)
            n1 = pltpu.roll(xx, n - 1, 0)[:tm]
            n2 = pltpu.roll(xx, n - 2, 0)[:tm]
            w = w_ref[...]
            o_ref[...] = (w[2:3] * cur + w[1:2] * n1 + w[0:1] * n2).astype(o_ref.dtype)

        @pl.when(j < nj)
        def _():
            run(a_ref, an_ref)

        @pl.when(j >= nj)
        def _():
            run(b_ref, bn_ref)

    ja = lambda j: jnp.minimum(j, nj - 1)
    jb = lambda j: jnp.maximum(j - nj, 0)
    nxt = lambda i: jnp.minimum((i + 1) * hb, S // HALO - 1)
    return pl.pallas_call(
        body,
        grid=(2 * nj, ni),
        in_specs=[
            pl.BlockSpec((tm, tn), lambda j, i: (i, ja(j))),
            pl.BlockSpec((tm, tn), lambda j, i: (i, jb(j))),
            pl.BlockSpec((HALO, tn), lambda j, i: (nxt(i), ja(j))),
            pl.BlockSpec((HALO, tn), lambda j, i: (nxt(i), jb(j))),
            pl.BlockSpec((3, tn), lambda j, i: (0, j)),
        ],
        out_specs=pl.BlockSpec((tm, tn), lambda j, i: (i, j)),
        out_shape=jax.ShapeDtypeStruct((S, 2 * F), BF16),
        compiler_params=_cparams("parallel", "parallel"),
        name=name,
    )(da, db, da, db, cw)


def _gate_bwd(dmixed, gates, y_pool, y_attn, name):
    S, D = dmixed.shape
    tm, tn = _tile(S, 512), _tile(D, 512)
    nj = D // tn

    def body(dm_ref, g_ref, yp_ref, ya_ref, dy_ref, dpre_ref, db_ref):
        j = pl.program_id(0)

        @pl.when(pl.program_id(1) == 0)
        def _():
            db_ref[...] = jnp.zeros_like(db_ref)

        def run(y_ref):
            dm = dm_ref[...].astype(F32)
            gv = g_ref[...]
            dy_ref[...] = (dm * gv).astype(BF16)
            dpre = dm * y_ref[...].astype(F32) * gv * (1.0 - gv)
            dpre_ref[...] = dpre.astype(BF16)
            db_ref[...] += jnp.sum(dpre, axis=0, keepdims=True)

        @pl.when(j < nj)
        def _():
            run(yp_ref)

        @pl.when(j >= nj)
        def _():
            run(ya_ref)

    tile2 = pl.BlockSpec((tm, tn), lambda j, i: (i, j))
    return pl.pallas_call(
        body,
        grid=(2 * nj, S // tm),
        in_specs=[
            pl.BlockSpec((tm, tn), lambda j, i: (i, lax.rem(j, nj))),
            tile2,
            pl.BlockSpec((tm, tn), lambda j, i: (i, jnp.minimum(j, nj - 1))),
            pl.BlockSpec((tm, tn), lambda j, i: (i, jnp.maximum(j - nj, 0))),
        ],
        out_specs=[tile2, tile2, pl.BlockSpec((1, tn), lambda j, i: (0, j))],
        out_shape=[jax.ShapeDtypeStruct((S, 2 * D), BF16), jax.ShapeDtypeStruct((S, 2 * D), BF16), jax.ShapeDtypeStruct((1, 2 * D), F32)],
        compiler_params=_cparams("parallel", "arbitrary"),
        name=name,
    )(dmixed, gates, y_pool, y_attn)


def _pool_counts(i, tm, rows, w):
    t = i * tm + lax.broadcasted_iota(jnp.int32, (rows, 1), 0)
    return jnp.minimum(t + 1, w).astype(F32)


def _pooled_groups(u_ref, uh_ref, i, tm, C):
    cur = u_ref[...]
    halo = jnp.where(i == 0, 0.0, uh_ref[...])
    xx = jnp.concatenate([halo, cur], axis=0)
    out = []
    s = xx
    for gi, w in enumerate(POOL_WINDOWS):
        s = s + pltpu.roll(s, w // 2, 0)
        tot = s[HALO:, 0:C]
        out.append(tot / _pool_counts(i, tm, tm, w) - cur[:, gi * C:(gi + 1) * C])
        s = s[:, C:] if gi + 1 < len(POOL_WINDOWS) else s
    return out


def _pool_fwd(u, wl, scale, name):
    S, PW = u.shape
    C = PW // len(POOL_WINDOWS)
    tm = _tile(S, 512)
    hb = tm // HALO

    def body(u_ref, uh_ref, wl_ref, sc_ref, o_ref):
        i = pl.program_id(0)
        pooled = _pooled_groups(u_ref, uh_ref, i, tm, C)
        for gi in range(len(POOL_WINDOWS)):
            y = _dot_nn(pooled[gi].astype(BF16), wl_ref[gi])
            o_ref[:, gi * C:(gi + 1) * C] = (y * sc_ref[:, gi * C:(gi + 1) * C]).astype(o_ref.dtype)

    return pl.pallas_call(
        body,
        grid=(S // tm,),
        in_specs=[
            pl.BlockSpec((tm, PW), lambda i: (i, 0)),
            pl.BlockSpec((HALO, PW), lambda i: (jnp.maximum(i * hb - 1, 0), 0)),
            pl.BlockSpec((len(POOL_WINDOWS), C, C), lambda i: (0, 0, 0)),
            pl.BlockSpec((1, PW), lambda i: (0, 0)),
        ],
        out_specs=pl.BlockSpec((tm, PW), lambda i: (i, 0)),
        out_shape=jax.ShapeDtypeStruct((S, PW), BF16),
        compiler_params=_cparams("parallel"),
        name=name,
    )(u, u, wl, scale)


def _pool_bwd(u, dp, wl, scale, name):
    S, PW = u.shape
    G = len(POOL_WINDOWS)
    C = PW // G
    tm = _tile(S, 512)
    hb, ni = tm // HALO, S // tm
    n = tm + HALO

    def body(u_ref, uh_ref, dp_ref, dpn_ref, wl_ref, sc_ref, du_ref, dwl_ref, dsc_ref):
        i = pl.program_id(0)

        @pl.when(i == 0)
        def _():
            dwl_ref[...] = jnp.zeros_like(dwl_ref)
            dsc_ref[...] = jnp.zeros_like(dsc_ref)

        pooled = _pooled_groups(u_ref, uh_ref, i, tm, C)
        dpc = dp_ref[...].astype(F32)
        dpn = jnp.where(i == ni - 1, 0.0, dpn_ref[...].astype(F32))
        sc = sc_ref[...]
        dyl = jnp.concatenate([dpc, dpn], axis=0) * sc
        for gi, w in enumerate(POOL_WINDOWS):
            cols = slice(gi * C, (gi + 1) * C)
            pb = pooled[gi].astype(BF16)
            ylin = _dot_nn(pb, wl_ref[gi])
            dsc_ref[:, cols] += jnp.sum(dpc[:, cols] * ylin, axis=0, keepdims=True)
            dylg = dyl[:, cols].astype(BF16)
            dwl_ref[gi] += _dot(pb, dylg[:tm], ((0,), (0,)))
            dpool = _dot_nt(dylg, wl_ref[gi])
            e = dpool / _pool_counts(i, tm, n, w)
            k = 1
            while k < w:
                e = e + pltpu.roll(e, n - k, 0)
                k *= 2
            du_ref[:, cols] = (e[:tm] - dpool[:tm]).astype(du_ref.dtype)

    return pl.pallas_call(
        body,
        grid=(ni,),
        in_specs=[
            pl.BlockSpec((tm, PW), lambda i: (i, 0)),
            pl.BlockSpec((HALO, PW), lambda i: (jnp.maximum(i * hb - 1, 0), 0)),
            pl.BlockSpec((tm, PW), lambda i: (i, 0)),
            pl.BlockSpec((HALO, PW), lambda i: (jnp.minimum((i + 1) * hb, S // HALO - 1), 0)),
            pl.BlockSpec((G, C, C), lambda i: (0, 0, 0)),
            pl.BlockSpec((1, PW), lambda i: (0, 0)),
        ],
        out_specs=[pl.BlockSpec((tm, PW), lambda i: (i, 0)), pl.BlockSpec((G, C, C), lambda i: (0, 0, 0)), pl.BlockSpec((1, PW), lambda i: (0, 0))],
        out_shape=[jax.ShapeDtypeStruct((S, PW), BF16), jax.ShapeDtypeStruct((G, C, C), F32), jax.ShapeDtypeStruct((1, PW), F32)],
        compiler_params=_cparams("arbitrary"),
        name=name,
    )(u, u, dp, dp, wl, scale)


def _band_masks():
    ii = lax.broadcasted_iota(jnp.int32, (SPAN, SPAN), 0)
    kk = lax.broadcasted_iota(jnp.int32, (SPAN, SPAN), 1)
    return ((ii + SPAN - kk).astype(F32), kk >= ii), ((ii - kk).astype(F32), kk <= ii)


def _attn_chunk(L):
    return min(4 * SPAN, L)


def _attn_fwd(qv, d, cols, g, name):
    L = qv.shape[0]
    CQ = _attn_chunk(L)
    nb = CQ // SPAN
    cq, ck, cv = cols

    def body(q_ref, k_ref, v_ref, kp_ref, vp_ref, o_ref, lse_ref):
        c = pl.program_id(1)
        (jp, mp), (jc, mc) = _band_masks()
        for h in range(HEADS_PER_GROUP):
            hs = slice(h * HEAD_DIM, (h + 1) * HEAD_DIM)
            slope = ALIBI_SLOPES[g * HEADS_PER_GROUP + h] * d
            for b in range(nb):
                rows = slice(b * SPAN, (b + 1) * SPAN)
                q = q_ref[rows, hs]
                kc, vc = k_ref[rows, hs], v_ref[rows, hs]
                if b == 0:
                    kp, vp, okp = kp_ref[:, hs], vp_ref[:, hs], jnp.logical_and(mp, c > 0)
                else:
                    prev = slice((b - 1) * SPAN, b * SPAN)
                    kp, vp, okp = k_ref[prev, hs], v_ref[prev, hs], mp
                sc = jnp.where(mc, _dot_nt(q, kc) * ATTN_SCALE - slope * jc, NEG_BIG)
                sp = jnp.where(okp, _dot_nt(q, kp) * ATTN_SCALE - slope * jp, NEG_BIG)
                m = jnp.maximum(jnp.max(sc, axis=-1, keepdims=True), jnp.max(sp, axis=-1, keepdims=True))
                pc, pp = jnp.exp(sc - m), jnp.exp(sp - m)
                l = jnp.sum(pc, axis=-1, keepdims=True) + jnp.sum(pp, axis=-1, keepdims=True)
                o = (_dot_nn(pc.astype(BF16), vc) + _dot_nn(pp.astype(BF16), vp)) / l
                o_ref[rows, hs] = o
                lse_ref[rows, hs] = jnp.broadcast_to(m + jnp.log(l), (SPAN, HEAD_DIM))

    W = GROUP_WIDTH
    pb = CQ // SPAN
    cur = lambda f: pl.BlockSpec((CQ, W), lambda r, c: (c, f(r)))
    prv = lambda f: pl.BlockSpec((SPAN, W), lambda r, c: (jnp.maximum(c * pb - 1, 0), f(r)))
    out = pl.BlockSpec((CQ, W), lambda r, c: (c, r))
    return pl.pallas_call(
        body,
        grid=(d, L // CQ),
        in_specs=[cur(cq), cur(ck), cur(cv), prv(ck), prv(cv)],
        out_specs=[out, out],
        out_shape=[jax.ShapeDtypeStruct((L, d * W), F32)] * 2,
        compiler_params=_cparams("parallel", "parallel"),
        name=name,
    )(qv, qv, qv, qv, qv)


def _attn_merge(os_, lses, name):
    S, W = os_[0].shape
    tm = _tile(S, 512)

    def body(o0, o1, o2, l0, l1, l2, y_ref, lse_ref):
        ls = [l0[...], l1[...], l2[...]]
        m = jnp.maximum(jnp.maximum(ls[0], ls[1]), ls[2])
        es = [jnp.exp(v - m) for v in ls]
        tot = es[0] + es[1] + es[2]
        y = (es[0] * o0[...] + es[1] * o1[...] + es[2] * o2[...]) / tot
        y_ref[...] = y.astype(y_ref.dtype)
        lse_ref[...] = m + jnp.log(tot)

    row = pl.BlockSpec((tm, W), lambda i: (i, 0))
    return pl.pallas_call(
        body,
        grid=(S // tm,),
        in_specs=[row] * 6,
        out_specs=[row, row],
        out_shape=[jax.ShapeDtypeStruct((S, W), BF16), jax.ShapeDtypeStruct((S, W), F32)],
        compiler_params=_cparams("parallel"),
        name=name,
    )(*os_, *lses)


def _attn_bwd(qv, dav, yv, lsev, d, cols, g, name):
    L = qv.shape[0]
    CQ = _attn_chunk(L)
    nb = CQ // SPAN
    nchunk = L // CQ
    cq, ck, cv = cols

    def body(q_ref, k_ref, v_ref, kp_ref, vp_ref, qn_ref, da_ref, dan_ref, y_ref, yn_ref, lse_ref, lsen_ref, dq_ref, dk_ref, dv_ref):
        c = pl.program_id(1)
        (jp, mp), (jc, mc) = _band_masks()
        for h in range(HEADS_PER_GROUP):
            hs = slice(h * HEAD_DIM, (h + 1) * HEAD_DIM)
            slope = ALIBI_SLOPES[g * HEADS_PER_GROUP + h] * d
            dq = [None] * nb
            dk = [None] * nb
            dv = [None] * nb

            def add(lst, idx, val):
                lst[idx] = val if lst[idx] is None else lst[idx] + val

            for qb in range(nb + 1):
                if qb < nb:
                    rows = slice(qb * SPAN, (qb + 1) * SPAN)
                    q, da, yy, lse = q_ref[rows, hs], da_ref[rows, hs], y_ref[rows, hs], lse_ref[rows, hs]
                else:
                    q, da, yy, lse = qn_ref[:, hs], dan_ref[:, hs], yn_ref[:, hs], lsen_ref[:, hs]
                dd = jnp.sum(da.astype(F32) * yy.astype(F32), axis=-1, keepdims=True)
                lse_col = lse[:, 0:1]
                for kb in (qb - 1, qb):
                    if kb >= nb:
                        continue
                    if kb < 0:
                        kk, vv, ok = kp_ref[:, hs], vp_ref[:, hs], jnp.logical_and(mp, c > 0)
                    else:
                        krows = slice(kb * SPAN, (kb + 1) * SPAN)
                        kk, vv = k_ref[krows, hs], v_ref[krows, hs]
                        ok = mc if kb == qb else (mp if qb < nb else jnp.logical_and(mp, c < nchunk - 1))
                    jj = jc if kb == qb else jp
                    s = jnp.where(ok, _dot_nt(q, kk) * ATTN_SCALE - slope * jj, NEG_BIG)
                    p = jnp.exp(s - lse_col)
                    ds = p * (_dot_nt(da, vv) - dd)
                    if qb < nb:
                        add(dq, qb, _dot_nn(ds.astype(BF16), kk))
                    if kb >= 0:
                        add(dv, kb, _dot_nn(p.T.astype(BF16), da))
                        add(dk, kb, _dot_nn(ds.T.astype(BF16), q))
            for b in range(nb):
                rows = slice(b * SPAN, (b + 1) * SPAN)
                dq_ref[rows, hs] = (dq[b] * ATTN_SCALE).astype(dq_ref.dtype)
                dk_ref[rows, hs] = (dk[b] * ATTN_SCALE).astype(dk_ref.dtype)
                dv_ref[rows, hs] = dv[b].astype(dv_ref.dtype)

    W = GROUP_WIDTH
    pb = CQ // SPAN
    nblocks = L // SPAN
    cur = lambda f: pl.BlockSpec((CQ, W), lambda r, c: (c, f(r)))
    prv = lambda f: pl.BlockSpec((SPAN, W), lambda r, c: (jnp.maximum(c * pb - 1, 0), f(r)))
    nxt = lambda f: pl.BlockSpec((SPAN, W), lambda r, c: (jnp.minimum((c + 1) * pb, nblocks - 1), f(r)))
    ident = lambda r: r
    out = pl.BlockSpec((CQ, W), lambda r, c: (c, r))
    return pl.pallas_call(
        body,
        grid=(d, nchunk),
        in_specs=[cur(cq), cur(ck), cur(cv), prv(ck), prv(cv), nxt(cq), cur(ident), nxt(ident), cur(ident), nxt(ident), cur(ident), nxt(ident)],
        out_specs=[out, out, out],
        out_shape=[jax.ShapeDtypeStruct((L, d * W), BF16)] * 3,
        compiler_params=_cparams("parallel", "parallel"),
        name=name,
    )(qv, qv, qv, qv, qv, qv, dav, dav, yv, yv, lsev, lsev)


def _row_block(R, C, bytes_per_row_elem=4, budget=1 << 20):
    if R % 8:
        return R
    best = 8
    t = 8
    while t <= R:
        if R % t == 0 and t * C * bytes_per_row_elem <= budget:
            best = t
        t += 8
    return best


def _adamw(w, g, m, v, name):
    R, C = w.shape
    tr = _row_block(R, C)
    c1 = 1.0 - ADAM_B1 ** ADAM_STEP
    c2 = 1.0 - ADAM_B2 ** ADAM_STEP

    def body(w_ref, g_ref, m_ref, v_ref, d_ref, nm_ref, nv_ref):
        gv = g_ref[...]
        nm = ADAM_B1 * m_ref[...] + (1.0 - ADAM_B1) * gv
        nv = ADAM_B2 * v_ref[...] + (1.0 - ADAM_B2) * (gv * gv)
        d_ref[...] = -ADAM_LR * ((nm / c1) / (jnp.sqrt(nv / c2) + ADAM_EPS) + ADAM_WD * w_ref[...])
        nm_ref[...] = nm
        nv_ref[...] = nv

    blk = pl.BlockSpec((tr, C), lambda i: (i, 0))
    return pl.pallas_call(
        body,
        grid=(R // tr,),
        in_specs=[blk] * 4,
        out_specs=[blk] * 3,
        out_shape=[jax.ShapeDtypeStruct((R, C), F32)] * 3,
        compiler_params=_cparams("parallel"),
        name=name,
    )(w, g, m, v)


def _sum_pieces(grad, axis, recv, pos, name):
    n, pr, pc = recv.shape
    tr = _row_block(pr, pc, bytes_per_row_elem=(n + 1) * recv.dtype.itemsize, budget=4 << 20)
    nblk = pr // tr
    if axis == 1:
        own_map = lambda i, p: (p[1] * nblk + i, p[0])
    else:
        own_map = lambda i, p: ((2 * p[0] + p[1]) * nblk + i, 0)

    def body(p_ref, own_ref, r_ref, o_ref):
        acc = own_ref[...].astype(F32)
        for s in range(n):
            acc = acc + r_ref[s].astype(F32)
        o_ref[...] = acc

    return pl.pallas_call(
        body,
        grid_spec=pltpu.PrefetchScalarGridSpec(
            num_scalar_prefetch=1,
            grid=(nblk,),
            in_specs=[pl.BlockSpec((tr, pc), own_map), pl.BlockSpec((n, tr, pc), lambda i, p: (0, i, 0))],
            out_specs=pl.BlockSpec((tr, pc), lambda i, p: (p[1] * nblk + i, 0)),
        ),
        out_shape=jax.ShapeDtypeStruct((2 * pr, pc), F32),
        compiler_params=_cparams("parallel"),
        name=name,
    )(pos, grad, recv)


def _sum_small(own, recv, me, name):
    n, R, C = recv.shape
    tr = _row_block(R, C, bytes_per_row_elem=(n + 1) * 4, budget=4 << 20)

    def body(me_ref, own_ref, r_ref, o_ref):
        acc = None
        for dev in range(n + 1):
            k = jnp.bitwise_xor(me_ref[0], dev)
            term = jnp.where(k == 0, own_ref[...], r_ref[jnp.maximum(k - 1, 0)])
            acc = term if acc is None else acc + term
        o_ref[...] = acc

    return pl.pallas_call(
        body,
        grid_spec=pltpu.PrefetchScalarGridSpec(
            num_scalar_prefetch=1,
            grid=(R // tr,),
            in_specs=[pl.BlockSpec((tr, C), lambda i, m: (i, 0)), pl.BlockSpec((n, tr, C), lambda i, m: (0, i, 0))],
            out_specs=pl.BlockSpec((tr, C), lambda i, m: (i, 0)),
        ),
        out_shape=jax.ShapeDtypeStruct((R, C), F32),
        compiler_params=_cparams("parallel"),
        name=name,
    )(me, own, recv)


def _place(shard, axis, pos, dtype, name):
    shp = list(shard.shape)
    shp[axis] *= N_CHIPS
    if shard.ndim == 3:
        assert axis == 1
        in_spec = pl.BlockSpec(shard.shape, lambda i, p: (0, 0, 0))
        out_spec = pl.BlockSpec(shard.shape, lambda i, p: (0, p[0], 0))
        grid = (1,)
    else:
        R, C = shard.shape
        tr = _row_block(R, C, bytes_per_row_elem=4, budget=2 << 20)
        nblk = R // tr
        in_spec = pl.BlockSpec((tr, C), lambda i, p: (i, 0))
        out_spec = pl.BlockSpec((tr, C), (lambda i, p: (i, p[0])) if axis == 1 else (lambda i, p: (p[0] * nblk + i, 0)))
        grid = (nblk,)

    def body(p_ref, s_ref, o_ref):
        o_ref[...] = s_ref[...].astype(o_ref.dtype)

    return pl.pallas_call(
        body,
        grid_spec=pltpu.PrefetchScalarGridSpec(num_scalar_prefetch=1, grid=grid, in_specs=[in_spec], out_specs=out_spec),
        out_shape=jax.ShapeDtypeStruct(tuple(shp), dtype),
        compiler_params=_cparams("parallel"),
        name=name,
    )(pos, shard)


HBM = pl.BlockSpec(memory_space=pltpu.HBM)
SEM = pl.BlockSpec(memory_space=pltpu.SEMAPHORE)
DATAFLOW = pltpu.SideEffectType.DATAFLOW_SIDE_EFFECTING


def _position():
    return lax.axis_index("x"), lax.axis_index("y"), lax.axis_index("c")


def _peer(k):
    x, y, c = _position()
    return ((1 - x) if k & 4 else x, (1 - y) if k & 2 else y, (1 - c) if k & 1 else c)


def _shard_slice(ref, axis, idx, size):
    start = idx * size
    if axis == ref.ndim - 1:
        start = pl.multiple_of(start, 128)
    ix = [slice(None)] * ref.ndim
    ix[axis] = pl.ds(start, size)
    return ref.at[tuple(ix)]


def _gather_plan(axes):
    def plan(refs):
        x, y, c = _position()
        out = []
        for ref, ax in zip(refs, axes):
            mine = _shard_slice(ref, ax, 2 * x + y, ref.shape[ax] // N_CHIPS)
            for k in (4, 2, 6):
                px, py, _ = _peer(k)
                out.append((mine, mine, (px, py, c)))
        return out
    return plan


def _scatter_plan(axes):
    m = len(axes)

    def plan(refs):
        out = []
        for t in range(m):
            grad, recv = refs[t], refs[m + t]
            _, pr, pc = recv.shape
            for k in range(1, N_DEV):
                px, py, pcore = _peer(k)
                if axes[t] == 0:
                    piece = grad.at[pl.ds(((2 * px + py) * 2 + pcore) * pr, pr), :]
                else:
                    piece = grad.at[pl.ds(pcore * pr, pr), pl.ds(pl.multiple_of((2 * px + py) * pc, 128), pc)]
                out.append((piece, recv.at[k - 1], (px, py, pcore)))
        return out
    return plan


def _broadcast_plan(refs):
    small, recv = refs
    return [(small, recv.at[k - 1], _peer(k)) for k in range(1, N_DEV)]


def _start_all(plan, refs, send_sems, recv_sems):
    for q, (src, dst, dev) in enumerate(plan(refs)):
        pltpu.make_async_remote_copy(src_ref=src, dst_ref=dst, send_sem=send_sems.at[q], recv_sem=recv_sems.at[q], device_id=dev, device_id_type=MESH).start()


def _wait_all(plan, refs, send_sems, recv_sems):
    for q, (src, dst, dev) in enumerate(plan(refs)):
        cp = pltpu.make_async_remote_copy(src_ref=src, dst_ref=dst, send_sem=send_sems.at[q], recv_sem=recv_sems.at[q], device_id=dev, device_id_type=MESH)
        cp.wait_send()
        cp.wait_recv()


def _push(bufs, plan, ncopies, name):
    n = len(bufs)

    def body(*refs):
        outs = refs[n:2 * n]
        send_sems, recv_sems = refs[2 * n:]
        _start_all(plan, outs, send_sems, recv_sems)
        _wait_all(plan, outs, send_sems, recv_sems)

    return pl.pallas_call(
        body,
        in_specs=[ANY] * n,
        out_specs=[ANY] * n,
        out_shape=[jax.ShapeDtypeStruct(b.shape, b.dtype) for b in bufs],
        input_output_aliases={t: t for t in range(n)},
        scratch_shapes=[pltpu.SemaphoreType.DMA((ncopies,)), pltpu.SemaphoreType.DMA((ncopies,))],
        name=name,
    )(*bufs)


def _push_start(bufs, plan, ncopies, name):
    n = len(bufs)

    def body(*refs):
        ins = refs[:n]
        send_sems, recv_sems, token = refs[n], refs[n + 1], refs[-1]
        _start_all(plan, ins, send_sems, recv_sems)
        token[...] = jnp.zeros_like(token)

    res = pl.pallas_call(
        body,
        name=name,
        out_shape=(pltpu.SemaphoreType.DMA((ncopies,)), pltpu.SemaphoreType.DMA((ncopies,)), *[pltpu.HBM(b.shape, b.dtype) for b in bufs],
                   jax.ShapeDtypeStruct((8, 128), F32)),
        in_specs=[HBM] * n,
        out_specs=(SEM, SEM, *[HBM] * n, pl.BlockSpec(memory_space=pltpu.VMEM)),
        input_output_aliases={t: t + 2 for t in range(n)},
        compiler_params=pltpu.CompilerParams(has_side_effects=DATAFLOW),
    )(*[pltpu.with_memory_space_constraint(b, pltpu.HBM) for b in bufs])
    return res[0], res[1], list(res[2:2 + n]), res[-1]


def _push_wait(send_sems, recv_sems, bufs, plan, after, name):
    n = len(bufs)

    def body(*refs):
        ins = refs[:n]
        _wait_all(plan, ins, refs[n], refs[n + 1])

    return pl.pallas_call(
        body,
        name=name,
        out_shape=tuple(pltpu.HBM(b.shape, b.dtype) for b in bufs),
        in_specs=[HBM] * n + [SEM, SEM, ANY],
        out_specs=tuple([HBM] * n),
        input_output_aliases={t: t for t in range(n)},
        compiler_params=pltpu.CompilerParams(has_side_effects=DATAFLOW),
    )(*bufs, send_sems, recv_sems, after)


EXCHANGE_CHUNKS = 2


def _exchange_plan(refs):
    x, y, c = _position()
    out = []
    for ref in refs:
        rows = ref.shape[0] // (2 * EXCHANGE_CHUNKS)
        for q in range(EXCHANGE_CHUNKS):
            mine = ref.at[pl.ds((c * EXCHANGE_CHUNKS + q) * rows, rows), :]
            out.append((mine, mine, (x, y, 1 - c)))
    return out


def _dilated_view(a, d):
    S, W = a.shape
    return a.reshape(S // d, d * W)


def _natural_view(a, d):
    L, W = a.shape
    return a.reshape(L * d, W // d)


LATE_WEIGHTS = ("w_pool_lin", "w_pool_out", "w_attn_out", "w_out", "w_up", "conv_w", "w_down")


def _local_step(x, tgt, w, late_weights, send):
    S, D = x.shape
    PW = w["pool_scale"].shape[1]
    o_q = PW
    o_g = PW + 3 * ATTN_WIDTH
    QKV = 3 * ATTN_WIDTH

    h1 = _rms_fwd(x, w["g_mix"], "rms1")
    proj_tiles = (_tile(S, 1024), 512, D)
    u = _mm(h1, w["w_in"], mode="nn", dims=(S, PW, D), tiles=proj_tiles, out_dtypes=(F32,), name="proj_u")
    qkv = _mm(h1, w["w_in"], mode="nn", dims=(S, QKV, D), tiles=proj_tiles, b_off=(0, o_q), name="proj_qkv")

    def gate_epilogue(acc, ex, outs):
        outs[0][...] = 1.0 / (1.0 + jnp.exp(-(acc + ex[0][...])))

    gates = _mm(h1, w["w_in"], mode="nn", dims=(S, 2 * D, D), tiles=proj_tiles, b_off=(0, o_g), out_dtypes=(F32,), epilogue=gate_epilogue,
                extras=[(w["b_gate"], "n", (0, 0))], name="proj_gates")

    ident3 = (lambda r: 0, lambda r: 3, lambda r: 6)
    trip = (lambda r: 3 * r, lambda r: 3 * r + 1, lambda r: 3 * r + 2)
    qviews, os_, lses = [], [], []
    for gi, (_, d) in enumerate(ATTN_GROUPS):
        if d == 1:
            assert gi == 0
            qv, cols = qkv, ident3
        else:
            qv = _dilated_view(qkv.reshape(S, 9, GROUP_WIDTH)[:, gi::3, :].reshape(S, 3 * GROUP_WIDTH), d)
            cols = trip
        qviews.append((qv, cols))
        o, lse = _attn_fwd(qv, d, cols, gi, f"attn_fwd{gi}")
        os_.append(_natural_view(o, d))
        lses.append(_natural_view(lse, d))
    attn, lse_tot = _attn_merge(os_, lses, "attn_merge")

    w = dict(w, **late_weights(attn))
    F = w["w_down"].shape[0]
    pool_out = _pool_fwd(u, w["w_pool_lin"], w["pool_scale"], "pool_fwd")
    y_pool = _mm(pool_out, w["w_pool_out"], mode="nn", dims=(S, D, PW), name="y_pool")

    def mix_epilogue(acc, ex, outs):
        outs[0][...] = acc.astype(BF16)
        outs[1][...] = (ex[0][...] * ex[2][...].astype(F32) + ex[1][...] * acc).astype(BF16)

    y_attn, mixed = _mm(attn, w["w_attn_out"], mode="nn", dims=(S, D, GROUP_WIDTH), out_dtypes=(BF16, BF16), epilogue=mix_epilogue,
                        extras=[(gates, "mn", (0, 0)), (gates, "mn", (0, D)), (y_pool, "mn", (0, 0))], name="y_attn_mix")

    def residual_epilogue(acc, ex, outs):
        outs[0][...] = ex[0][...] + acc

    x2 = _mm(mixed, w["w_out"], mode="nn", dims=(S, D, D), out_dtypes=(F32,), epilogue=residual_epilogue, extras=[(x, "mn", (0, 0))], name="out_proj")

    h2 = _rms_fwd(x2, w["g_ffn"], "rms2")
    up = _mm(h2, w["w_up"], mode="nn", dims=(S, 2 * F, D), name="up_proj")
    f = _convglu_fwd(up, w["conv_w"], w["conv_b"], "convglu_fwd")
    x3 = _mm(f, w["w_down"], mode="nn", dims=(S, D, F), out_dtypes=(F32,), epilogue=residual_epilogue, extras=[(x2, "mn", (0, 0))], name="down_proj")

    g = {}
    dx3, dx3b, g["g_final"], loss_cols = _loss_head(x3, tgt, w["g_final"], "loss_head")

    g["w_down"] = _mm(f, dx3b, mode="tn", dims=(F, D, S), name="dw_down")
    sent = send(("w_down",), g)
    df = _mm(dx3b, w["w_down"], mode="nt", dims=(S, F, D), name="d_f")
    da, db, dcb_a, dcb_b, dcw_a, dcw_b = _convglu_bwd(df, up, w["conv_w"], w["conv_b"] + sent, "convglu_bwd")
    g["conv_b"] = jnp.concatenate([dcb_a, dcb_b], axis=1)
    g["conv_w"] = jnp.concatenate([dcw_a, dcw_b], axis=1)
    dup = _conv_transpose(da, db, w["conv_w"], "conv_transpose")
    g["w_up"] = _mm(h2, dup, mode="tn", dims=(D, 2 * F, S), name="dw_up")
    sent = send(("w_up",), g)
    dh2 = _mm(dup, w["w_up"], mode="nt", dims=(S, D, 2 * F), name="d_h2")
    dx2, dx2b, g["g_ffn"] = _rms_bwd(dh2, x2, w["g_ffn"] + sent, dx3, "rms2_bwd", True)

    g["w_out"] = _mm(mixed, dx2b, mode="tn", dims=(D, D, S), name="dw_out")
    dmixed = _mm(dx2b, w["w_out"], mode="nt", dims=(S, D, D), name="d_mixed")
    dy_both, dpre, g["b_gate"] = _gate_bwd(dmixed, gates, y_pool, y_attn, "gate_bwd")

    g["w_pool_out"] = _mm(pool_out, dy_both, mode="tn", dims=(PW, D, S), name="dw_pool_out")
    g["w_attn_out"] = _mm(attn, dy_both, mode="tn", dims=(GROUP_WIDTH, D, S), b_off=(0, D), name="dw_attn_out")
    sent = send(("w_out", "w_pool_out", "w_attn_out"), g)
    dpool = _mm(dy_both, w["w_pool_out"], mode="nt", dims=(S, PW, D), name="d_pool")
    dattn = _mm(dy_both, w["w_attn_out"], mode="nt", dims=(S, GROUP_WIDTH, D), a_off=(0, D), name="d_attn")

    du, g["w_pool_lin"], g["pool_scale"] = _pool_bwd(u, dpool, w["w_pool_lin"], w["pool_scale"] + sent, "pool_bwd")
    g["loss_cols"] = loss_cols
    sent = send("small", g)

    dqs, dks, dvs = [], [], []
    for gi, (_, d) in enumerate(ATTN_GROUPS):
        qv, cols = qviews[gi]
        dq, dk, dv = _attn_bwd(qv, _dilated_view(dattn, d), _dilated_view(attn, d), _dilated_view(lse_tot, d), d, cols, gi, f"attn_bwd{gi}")
        dqs.append(_natural_view(dq, d))
        dks.append(_natural_view(dk, d))
        dvs.append(_natural_view(dv, d))
    dproj = jnp.concatenate([du] + dqs + dks + dvs + [dpre], axis=1)
    IN = dproj.shape[1]

    g["w_in"] = _mm(h1, dproj, mode="tn", dims=(D, IN, S), name="dw_in")
    sent = sent + send(("w_in",), g)
    dh1 = _mm(dproj, w["w_in"], mode="nt", dims=(S, D, IN), tiles=(_tile(S, 1024), _tile(D, 2048), _tile(IN, 512)), name="d_h1")
    (grad_x, g["g_mix"]) = _rms_bwd(dh1, x, w["g_mix"] + sent, dx2, "rms1_bwd", False)
    return loss_cols, grad_x, g


BIG = ("w_in", "w_pool_out", "w_attn_out", "w_out", "w_up", "w_down")
BIG_AXIS = {"w_in": 1, "w_pool_out": 1, "w_attn_out": 1, "w_out": 0, "w_up": 1, "w_down": 0}
GATHER_AXIS = dict(BIG_AXIS, w_pool_lin=1, conv_w=1)
SMALL = ("loss_cols", "b_gate", "w_pool_lin", "pool_scale", "g_ffn", "conv_w", "conv_b", "g_final")
SMALL_COLS = 1024
ORDER = ("g_mix", "w_in", "b_gate", "w_pool_lin", "pool_scale", "w_pool_out", "w_attn_out", "w_out", "g_ffn", "w_up", "conv_w", "conv_b", "w_down", "g_final")


def _as_rows(parts):
    flat = jnp.concatenate([p.astype(F32).reshape(-1) for p in parts])
    rows = -(-flat.shape[0] // (8 * SMALL_COLS)) * 8
    return jnp.pad(flat, (0, rows * SMALL_COLS - flat.shape[0])).reshape(rows, SMALL_COLS)


def kernel(x, g_mix, w_in, b_gate, w_pool_lin, pool_scale, w_pool_out, w_attn_out, w_out, g_ffn, w_up, conv_w, conv_b, w_down, g_final, loss_target, m_g_mix, m_w_in, m_b_gate, m_w_pool_lin, m_pool_scale, m_w_pool_out, m_w_attn_out, m_w_out, m_g_ffn, m_w_up, m_conv_w, m_conv_b, m_w_down, m_g_final, v_g_mix, v_w_in, v_b_gate, v_w_pool_lin, v_pool_scale, v_w_pool_out, v_w_attn_out, v_w_out, v_g_ffn, v_w_up, v_conv_w, v_conv_b, v_w_down, v_g_final):
    shard = dict(g_mix=g_mix, w_in=w_in, b_gate=b_gate, w_pool_lin=w_pool_lin, pool_scale=pool_scale, w_pool_out=w_pool_out, w_attn_out=w_attn_out,
                 w_out=w_out, g_ffn=g_ffn, w_up=w_up, conv_w=conv_w, conv_b=conv_b, w_down=w_down, g_final=g_final)
    mom = dict(g_mix=m_g_mix, w_in=m_w_in, b_gate=m_b_gate, w_pool_lin=m_w_pool_lin, pool_scale=m_pool_scale, w_pool_out=m_w_pool_out, w_attn_out=m_w_attn_out,
               w_out=m_w_out, g_ffn=m_g_ffn, w_up=m_w_up, conv_w=m_conv_w, conv_b=m_conv_b, w_down=m_w_down, g_final=m_g_final)
    vel = dict(g_mix=v_g_mix, w_in=v_w_in, b_gate=v_b_gate, w_pool_lin=v_w_pool_lin, pool_scale=v_pool_scale, w_pool_out=v_w_pool_out, w_attn_out=v_w_attn_out,
               w_out=v_w_out, g_ffn=v_g_ffn, w_up=v_w_up, conv_w=v_conv_w, conv_b=v_conv_b, w_down=v_w_down, g_final=v_g_final)
    chip = 2 * lax.axis_index("x") + lax.axis_index("y")
    pos = jnp.stack([chip, lax.axis_index("c")]).astype(jnp.int32)
    me = (2 * chip + lax.axis_index("c")).astype(jnp.int32).reshape(1)
    D = x.shape[2]

    placed = {k: _place(shard[k][0], GATHER_AXIS[k], pos, F32 if k == "conv_w" else BF16, f"place_{k}") for k in GATHER_AXIS}
    w_in_full, = _push([placed["w_in"]], _gather_plan([GATHER_AXIS["w_in"]]), 3, "comm_gather_w_in")
    late_plan = _gather_plan([GATHER_AXIS[k] for k in LATE_WEIGHTS])
    late_send, late_recv, late_bufs, late_token = _push_start([placed[k] for k in LATE_WEIGHTS], late_plan, 3 * len(LATE_WEIGHTS), "comm_gather_late_start")

    def late_weights(after):
        return dict(zip(LATE_WEIGHTS, _push_wait(late_send, late_recv, late_bufs, late_plan, after, "comm_gather_late_wait")))

    pending = []

    def send(names, g):
        if names == "small":
            bufs = [_as_rows([g[k] for k in SMALL])]
            bufs.append(lax.empty((N_DEV - 1,) + bufs[0].shape, F32))
            plan, tag = _broadcast_plan, "small"
        else:
            bufs = [g[k] for k in names]
            for k in names:
                R, C = g[k].shape
                piece = (R // (2 * N_CHIPS), C) if BIG_AXIS[k] == 0 else (R // 2, C // N_CHIPS)
                bufs.append(lax.empty((N_DEV - 1,) + piece, BF16))
            plan, tag = _scatter_plan([BIG_AXIS[k] for k in names]), names[0]
        ncopies = (N_DEV - 1) * (len(bufs) // 2)
        send_sems, recv_sems, thru, token = _push_start(bufs, plan, ncopies, f"comm_scatter_start_{tag}")
        pending.append((names, send_sems, recv_sems, thru, plan, tag))
        return token[0, 0]

    w0 = dict(g_mix=shard["g_mix"] + late_token[0, 0], w_in=w_in_full, b_gate=shard["b_gate"], pool_scale=shard["pool_scale"], g_ffn=shard["g_ffn"],
              conv_b=shard["conv_b"], g_final=shard["g_final"].reshape(1, D))
    _, grad_x, gr = _local_step(x[0], loss_target[0], w0, late_weights, send)

    halves, small_parts = {}, None
    for names, send_sems, recv_sems, thru, plan, tag in pending:
        done = _push_wait(send_sems, recv_sems, thru, plan, grad_x, f"comm_scatter_wait_{tag}")
        if names == "small":
            small_parts = _sum_small(done[0], done[1], me, "sum_small").reshape(-1)
        else:
            m = len(names)
            for t, k in enumerate(names):
                halves[k] = _sum_pieces(done[t], BIG_AXIS[k], done[m + t], pos, f"sum_{k}")
    g_mix_own = _as_rows([gr["g_mix"]])
    _, g_mix_recv = _push([g_mix_own, lax.empty((N_DEV - 1,) + g_mix_own.shape, F32)], _broadcast_plan, N_DEV - 1, "comm_gather_g_mix")
    g_mix_sum = _sum_small(g_mix_own, g_mix_recv, me, "sum_g_mix").reshape(-1)[:D]
    wholes = _push([halves[k] for k in BIG], _exchange_plan, EXCHANGE_CHUNKS * len(BIG), "comm_exchange_halves")

    grads = {"g_mix": g_mix_sum.reshape(shard["g_mix"].shape)}
    for k, whole in zip(BIG, wholes):
        grads[k] = whole.reshape(shard[k].shape)
    off = 0
    loss = None
    for k in SMALL:
        sz = math.prod(gr[k].shape)
        fullg = small_parts[off:off + sz].reshape(gr[k].shape)
        off += sz
        if k == "loss_cols":
            loss = jnp.sum(fullg)
            continue
        if k in ("w_pool_lin", "conv_w"):
            n = shard[k].shape[2]
            fullg = lax.dynamic_slice_in_dim(fullg, chip * n, n, axis=1)
        grads[k] = fullg.reshape(shard[k].shape)

    deltas, new_m, new_v = {}, {}, {}
    for k in ORDER:
        shp = shard[k].shape
        two_d = (-1, shp[-1])
        dl, nm, nv = _adamw(shard[k].reshape(two_d), grads[k].reshape(two_d), mom[k].reshape(two_d), vel[k].reshape(two_d), f"adamw_{k}")
        deltas[k], new_m[k], new_v[k] = dl.reshape(shp), nm.reshape(shp), nv.reshape(shp)

    return (loss, grad_x[None], *[grads[k] for k in ORDER], *[deltas[k] for k in ORDER], *[new_m[k] for k in ORDER], *[new_v[k] for k in ORDER])
```

```python
import functools
import math

import jax
import jax.numpy as jnp
from jax import lax
from jax.experimental import pallas as pl
from jax.experimental.pallas import tpu as pltpu

F32 = jnp.float32
BF16 = jnp.bfloat16

RMS_EPS = 1e-6
POOL_WINDOWS = (2, 4, 8, 16)
ATTN_GROUPS = ((128, 1), (512, 4), (2048, 16))
HEADS_PER_GROUP = 4
HEAD_DIM = 128
N_ATTN_HEADS = HEADS_PER_GROUP * len(ATTN_GROUPS)
SPAN = 128
GROUP_WIDTH = HEADS_PER_GROUP * HEAD_DIM
ATTN_WIDTH = N_ATTN_HEADS * HEAD_DIM
ATTN_SCALE = HEAD_DIM ** -0.5
NEG_BIG = -1e30
ALIBI_SLOPES = tuple(2.0 ** (-8.0 * (h + 1) / N_ATTN_HEADS) for h in range(N_ATTN_HEADS))

ADAM_LR = 0.001
ADAM_B1 = 0.9
ADAM_B2 = 0.999
ADAM_EPS = 1e-08
ADAM_WD = 0.01
ADAM_STEP = 10

INV_SQRT2 = 1.0 / math.sqrt(2.0)
INV_SQRT_2PI = 1.0 / math.sqrt(2.0 * math.pi)

HALO = 16
VMEM_LIMIT = 56 * 1024 * 1024
N_CHIPS = 4
N_DEV = 8
MESH = pl.DeviceIdType.MESH
ANY = pl.BlockSpec(memory_space=pl.ANY)


def _cparams(*sem):
    return pltpu.CompilerParams(dimension_semantics=sem, vmem_limit_bytes=VMEM_LIMIT)


def _tile(n, pref, mult=128):
    t = (min(pref, n) // mult) * mult
    while t >= mult:
        if n % t == 0:
            return t
        t -= mult
    return n


def _dot(a, b, contract):
    return lax.dot_general(a, b, (contract, ((), ())), preferred_element_type=F32)


def _dot_nn(a, b):
    return _dot(a, b, ((1,), (0,)))


def _dot_nt(a, b):
    return _dot(a, b, ((1,), (1,)))


def _mm(a, b, *, mode, dims, name, tiles=None, out_dtypes=(BF16,), epilogue=None, extras=(), a_off=(0, 0), b_off=(0, 0)):
    M, N, K = dims
    if tiles is None:
        tiles = (_tile(M, 1408), _tile(N, 2816), _tile(K, 512)) if mode == "tn" else (_tile(M, 1024), _tile(N, 1536), _tile(K, 1408))
    tm, tn, tk = tiles
    assert M % tm == 0 and N % tn == 0 and K % tk == 0, (name, dims, tiles)
    nk = K // tk
    if mode == "nn":
        ab, bb, contract = (tm, tk), (tk, tn), ((1,), (0,))
        amap = lambda i, j, k: (i + a_off[0] // tm, k + a_off[1] // tk)
        bmap = lambda i, j, k: (k + b_off[0] // tk, j + b_off[1] // tn)
    elif mode == "nt":
        ab, bb, contract = (tm, tk), (tn, tk), ((1,), (1,))
        amap = lambda i, j, k: (i + a_off[0] // tm, k + a_off[1] // tk)
        bmap = lambda i, j, k: (j + b_off[0] // tn, k + b_off[1] // tk)
    else:
        ab, bb, contract = (tk, tm), (tk, tn), ((0,), (0,))
        amap = lambda i, j, k: (k + a_off[0] // tk, i + a_off[1] // tm)
        bmap = lambda i, j, k: (k + b_off[0] // tk, j + b_off[1] // tn)
    assert a_off[0] % ab[0] == 0 and a_off[1] % ab[1] == 0 and b_off[0] % bb[0] == 0 and b_off[1] % bb[1] == 0, name
    in_specs = [pl.BlockSpec(ab, amap), pl.BlockSpec(bb, bmap)]
    ex_arrays = []
    for arr, kind, off in extras:
        if kind == "mn":
            assert off[0] % tm == 0 and off[1] % tn == 0, name
            in_specs.append(pl.BlockSpec((tm, tn), lambda i, j, k, off=off: (i + off[0] // tm, j + off[1] // tn)))
        else:
            assert off[1] % tn == 0, name
            in_specs.append(pl.BlockSpec((1, tn), lambda i, j, k, off=off: (0, j + off[1] // tn)))
        ex_arrays.append(arr)
    ne, no = len(ex_arrays), len(out_dtypes)
    if epilogue is None:
        def epilogue(acc, ex, outs):
            outs[0][...] = acc.astype(outs[0].dtype)

    def body(*refs):
        a_ref, b_ref = refs[0], refs[1]
        ex, outs, acc = refs[2:2 + ne], refs[2 + ne:2 + ne + no], refs[-1]
        k = pl.program_id(2)

        @pl.when(k == 0)
        def _():
            acc[...] = jnp.zeros_like(acc)

        acc[...] += _dot(a_ref[...], b_ref[...], contract)

        @pl.when(k == nk - 1)
        def _():
            epilogue(acc[...], ex, outs)

    res = pl.pallas_call(
        body,
        grid=(M // tm, N // tn, nk),
        in_specs=in_specs,
        out_specs=[pl.BlockSpec((tm, tn), lambda i, j, k: (i, j)) for _ in out_dtypes],
        out_shape=[jax.ShapeDtypeStruct((M, N), dt) for dt in out_dtypes],
        scratch_shapes=[pltpu.VMEM((tm, tn), F32)],
        compiler_params=_cparams("parallel", "parallel", "arbitrary"),
        name=name,
    )(a, b, *ex_arrays)
    return res[0] if no == 1 else res


def _rms_fwd(x, g, name):
    S, D = x.shape
    tm = _tile(S, 256)

    def body(x_ref, g_ref, h_ref):
        xv = x_ref[...]
        r = lax.rsqrt(jnp.mean(xv * xv, axis=-1, keepdims=True) + RMS_EPS)
        h_ref[...] = (xv * r * g_ref[...]).astype(h_ref.dtype)

    return pl.pallas_call(
        body,
        grid=(S // tm,),
        in_specs=[pl.BlockSpec((tm, D), lambda i: (i, 0)), pl.BlockSpec((1, D), lambda i: (0, 0))],
        out_specs=pl.BlockSpec((tm, D), lambda i: (i, 0)),
        out_shape=jax.ShapeDtypeStruct((S, D), BF16),
        compiler_params=_cparams("parallel"),
        name=name,
    )(x, g)


def _rms_bwd(dh, x, g, dres, name, with_bf16):
    S, D = x.shape
    tm = _tile(S, 256)

    def body(dh_ref, x_ref, g_ref, dres_ref, *outs):
        dx_ref, dg_ref = outs[0], outs[-1]
        xv = x_ref[...]
        r = lax.rsqrt(jnp.mean(xv * xv, axis=-1, keepdims=True) + RMS_EPS)
        xr = xv * r
        dhv = dh_ref[...].astype(F32)

        @pl.when(pl.program_id(0) == 0)
        def _():
            dg_ref[...] = jnp.zeros_like(dg_ref)

        dg_ref[...] += jnp.sum(dhv * xr, axis=0, keepdims=True)
        u = dhv * g_ref[...]
        c = jnp.mean(u * xr, axis=-1, keepdims=True)
        dx = dres_ref[...] + r * (u - xr * c)
        dx_ref[...] = dx
        if with_bf16:
            outs[1][...] = dx.astype(BF16)

    row = pl.BlockSpec((tm, D), lambda i: (i, 0))
    vec = pl.BlockSpec((1, D), lambda i: (0, 0))
    out_specs = [row] + ([row] if with_bf16 else []) + [vec]
    out_shape = [jax.ShapeDtypeStruct((S, D), F32)] + ([jax.ShapeDtypeStruct((S, D), BF16)] if with_bf16 else []) + [jax.ShapeDtypeStruct((1, D), F32)]
    return pl.pallas_call(
        body,
        grid=(S // tm,),
        in_specs=[row, row, vec, row],
        out_specs=out_specs,
        out_shape=out_shape,
        compiler_params=_cparams("arbitrary"),
        name=name,
    )(dh, x, g, dres)


def _loss_head(x3, tgt, g, name):
    S, D = x3.shape
    tm = _tile(S, 256)

    def body(x_ref, t_ref, g_ref, dx_ref, dxb_ref, dg_ref, loss_ref):
        xv = x_ref[...]
        gv = g_ref[...]
        r = lax.rsqrt(jnp.mean(xv * xv, axis=-1, keepdims=True) + RMS_EPS)
        xr = xv * r
        e = xr * gv - t_ref[...]

        @pl.when(pl.program_id(0) == 0)
        def _():
            dg_ref[...] = jnp.zeros_like(dg_ref)
            loss_ref[...] = jnp.zeros_like(loss_ref)

        loss_ref[...] += jnp.sum(e * e, axis=0, keepdims=True) * (0.5 / D)
        dy = e * (1.0 / D)
        dg_ref[...] += jnp.sum(dy * xr, axis=0, keepdims=True)
        u = dy * gv
        c = jnp.mean(u * xr, axis=-1, keepdims=True)
        dx = r * (u - xr * c)
        dx_ref[...] = dx
        dxb_ref[...] = dx.astype(BF16)

    row = pl.BlockSpec((tm, D), lambda i: (i, 0))
    vec = pl.BlockSpec((1, D), lambda i: (0, 0))
    return pl.pallas_call(
        body,
        grid=(S // tm,),
        in_specs=[row, row, vec],
        out_specs=[row, row, vec, vec],
        out_shape=[jax.ShapeDtypeStruct((S, D), F32), jax.ShapeDtypeStruct((S, D), BF16), jax.ShapeDtypeStruct((1, D), F32), jax.ShapeDtypeStruct((1, D), F32)],
        compiler_params=_cparams("arbitrary"),
        name=name,
    )(x3, tgt, g)


def _conv_taps(cur_ref, halo_ref, w_ref, b_ref, first):
    cur = cur_ref[...].astype(F32)
    halo = jnp.where(first, 0.0, halo_ref[...].astype(F32))
    xx = jnp.concatenate([halo, cur], axis=0)
    p1 = pltpu.roll(xx, 1, 0)[HALO:]
    p2 = pltpu.roll(xx, 2, 0)[HALO:]
    w = w_ref[...]
    y = b_ref[...] + w[0:1] * p2 + w[1:2] * p1 + w[2:3] * cur
    return y, (cur, p1, p2)


def _convglu_specs(S, F, tm, tn, rows_axis):
    nj = F // tn
    if rows_axis == 0:
        ij = lambda f: (lambda i, j: f(i, j))
    else:
        ij = lambda f: (lambda j, i: f(i, j))
    hb = tm // HALO
    return [
        pl.BlockSpec((tm, tn), ij(lambda i, j: (i, j))),
        pl.BlockSpec((tm, tn), ij(lambda i, j: (i, j + nj))),
        pl.BlockSpec((HALO, tn), ij(lambda i, j: (jnp.maximum(i * hb - 1, 0), j))),
        pl.BlockSpec((HALO, tn), ij(lambda i, j: (jnp.maximum(i * hb - 1, 0), j + nj))),
        pl.BlockSpec((3, tn), ij(lambda i, j: (0, j))),
        pl.BlockSpec((3, tn), ij(lambda i, j: (0, j + nj))),
        pl.BlockSpec((1, tn), ij(lambda i, j: (0, j))),
        pl.BlockSpec((1, tn), ij(lambda i, j: (0, j + nj))),
    ]


def _convglu_fwd(up, cw, cb, name):
    S, F2 = up.shape
    F = F2 // 2
    tm, tn = _tile(S, 512), _tile(F, 512)

    def body(ua, ub, ha, hb, wa, wb, ba, bb, f_ref):
        first = pl.program_id(0) == 0
        a, _ = _conv_taps(ua, ha, wa, ba, first)
        b, _ = _conv_taps(ub, hb, wb, bb, first)
        f_ref[...] = (0.5 * a * (1.0 + lax.erf(a * INV_SQRT2)) * b).astype(f_ref.dtype)

    return pl.pallas_call(
        body,
        grid=(S // tm, F // tn),
        in_specs=_convglu_specs(S, F, tm, tn, 0),
        out_specs=pl.BlockSpec((tm, tn), lambda i, j: (i, j)),
        out_shape=jax.ShapeDtypeStruct((S, F), BF16),
        compiler_params=_cparams("parallel", "parallel"),
        name=name,
    )(up, up, up, up, cw, cw, cb, cb)


def _convglu_bwd(df, up, cw, cb, name):
    S, F2 = up.shape
    F = F2 // 2
    tm, tn = _tile(S, 512), _tile(F, 512)

    def body(df_ref, ua, ub, ha, hb, wa, wb, ba, bb, da_ref, db_ref, dba_ref, dbb_ref, dwa_ref, dwb_ref):
        first = pl.program_id(1) == 0
        a, pa = _conv_taps(ua, ha, wa, ba, first)
        b, pb = _conv_taps(ub, hb, wb, bb, first)
        dfv = df_ref[...].astype(F32)
        cdf = 0.5 * (1.0 + lax.erf(a * INV_SQRT2))
        pdf = jnp.exp(-0.5 * a * a) * INV_SQRT_2PI
        da = dfv * b * (cdf + a * pdf)
        db = dfv * (a * cdf)
        da_ref[...] = da.astype(BF16)
        db_ref[...] = db.astype(BF16)

        @pl.when(first)
        def _():
            for r in (dba_ref, dbb_ref, dwa_ref, dwb_ref):
                r[...] = jnp.zeros_like(r)

        for d, taps, dbias, dw in ((da, pa, dba_ref, dwa_ref), (db, pb, dbb_ref, dwb_ref)):
            dbias[...] += jnp.sum(d, axis=0, keepdims=True)
            dw[0:1, :] += jnp.sum(d * taps[2], axis=0, keepdims=True)
            dw[1:2, :] += jnp.sum(d * taps[1], axis=0, keepdims=True)
            dw[2:3, :] += jnp.sum(d * taps[0], axis=0, keepdims=True)

    tile = pl.BlockSpec((tm, tn), lambda j, i: (i, j))
    b1 = pl.BlockSpec((1, tn), lambda j, i: (0, j))
    b3 = pl.BlockSpec((3, tn), lambda j, i: (0, j))
    return pl.pallas_call(
        body,
        grid=(F // tn, S // tm),
        in_specs=[tile] + _convglu_specs(S, F, tm, tn, 1),
        out_specs=[tile, tile, b1, b1, b3, b3],
        out_shape=[jax.ShapeDtypeStruct((S, F), BF16)] * 2 + [jax.ShapeDtypeStruct((1, F), F32)] * 2 + [jax.ShapeDtypeStruct((3, F), F32)] * 2,
        compiler_params=_cparams("parallel", "arbitrary"),
        name=name,
    )(df, up, up, up, up, cw, cw, cb, cb)


def _conv_transpose(da, db, cw, name):
    S, F = da.shape
    tm, tn = _tile(S, 512), _tile(F, 512)
    nj, ni, hb = F // tn, S // tm, tm // HALO
    n = tm + HALO

    def body(a_ref, b_ref, an_ref, bn_ref, w_ref, o_ref):
        j, i = pl.program_id(0), pl.program_id(1)

        def run(c_ref, h_ref):
            cur = c_ref[...].astype(F32)
            halo = jnp.where(i == ni - 1, 0.0, h_ref[...].astype(F32))
            xx = jnp.concatenate([cur, halo], axis=0)
            n1 = pltpu.roll(xx, n - 1, 0)[:tm]
            n2 = pltpu.roll(xx, n - 2, 0)[:tm]
            w = w_ref[...]
            o_ref[...] = (w[2:3] * cur + w[1:2] * n1 + w[0:1] * n2).astype(o_ref.dtype)

        @pl.when(j < nj)
        def _():
            run(a_ref, an_ref)

        @pl.when(j >= nj)
        def _():
            run(b_ref, bn_ref)

    ja = lambda j: jnp.minimum(j, nj - 1)
    jb = lambda j: jnp.maximum(j - nj, 0)
    nxt = lambda i: jnp.minimum((i + 1) * hb, S // HALO - 1)
    return pl.pallas_call(
        body,
        grid=(2 * nj, ni),
        in_specs=[
            pl.BlockSpec((tm, tn), lambda j, i: (i, ja(j))),
            pl.BlockSpec((tm, tn), lambda j, i: (i, jb(j))),
            pl.BlockSpec((HALO, tn), lambda j, i: (nxt(i), ja(j))),
            pl.BlockSpec((HALO, tn), lambda j, i: (nxt(i), jb(j))),
            pl.BlockSpec((3, tn), lambda j, i: (0, j)),
        ],
        out_specs=pl.BlockSpec((tm, tn), lambda j, i: (i, j)),
        out_shape=jax.ShapeDtypeStruct((S, 2 * F), BF16),
        compiler_params=_cparams("parallel", "parallel"),
        name=name,
    )(da, db, da, db, cw)


def _gate_bwd(dmixed, gates, y_pool, y_attn, name):
    S, D = dmixed.shape
    tm, tn = _tile(S, 512), _tile(D, 512)
    nj = D // tn

    def body(dm_ref, g_ref, yp_ref, ya_ref, dy_ref, dpre_ref, db_ref):
        j = pl.program_id(0)

        @pl.when(pl.program_id(1) == 0)
        def _():
            db_ref[...] = jnp.zeros_like(db_ref)

        def run(y_ref):
            dm = dm_ref[...].astype(F32)
            gv = g_ref[...]
            dy_ref[...] = (dm * gv).astype(BF16)
            dpre = dm * y_ref[...].astype(F32) * gv * (1.0 - gv)
            dpre_ref[...] = dpre.astype(BF16)
            db_ref[...] += jnp.sum(dpre, axis=0, keepdims=True)

        @pl.when(j < nj)
        def _():
            run(yp_ref)

        @pl.when(j >= nj)
        def _():
            run(ya_ref)

    tile2 = pl.BlockSpec((tm, tn), lambda j, i: (i, j))
    return pl.pallas_call(
        body,
        grid=(2 * nj, S // tm),
        in_specs=[
            pl.BlockSpec((tm, tn), lambda j, i: (i, lax.rem(j, nj))),
            tile2,
            pl.BlockSpec((tm, tn), lambda j, i: (i, jnp.minimum(j, nj - 1))),
            pl.BlockSpec((tm, tn), lambda j, i: (i, jnp.maximum(j - nj, 0))),
        ],
        out_specs=[tile2, tile2, pl.BlockSpec((1, tn), lambda j, i: (0, j))],
        out_shape=[jax.ShapeDtypeStruct((S, 2 * D), BF16), jax.ShapeDtypeStruct((S, 2 * D), BF16), jax.ShapeDtypeStruct((1, 2 * D), F32)],
        compiler_params=_cparams("parallel", "arbitrary"),
        name=name,
    )(dmixed, gates, y_pool, y_attn)


def _pool_counts(i, tm, rows, w):
    t = i * tm + lax.broadcasted_iota(jnp.int32, (rows, 1), 0)
    return jnp.minimum(t + 1, w).astype(F32)


def _pooled_groups(u_ref, uh_ref, i, tm, C):
    cur = u_ref[...]
    halo = jnp.where(i == 0, 0.0, uh_ref[...])
    xx = jnp.concatenate([halo, cur], axis=0)
    out = []
    s = xx
    for gi, w in enumerate(POOL_WINDOWS):
        s = s + pltpu.roll(s, w // 2, 0)
        tot = s[HALO:, 0:C]
        out.append(tot / _pool_counts(i, tm, tm, w) - cur[:, gi * C:(gi + 1) * C])
        s = s[:, C:] if gi + 1 < len(POOL_WINDOWS) else s
    return out


def _pool_fwd(u, wl, scale, name):
    S, PW = u.shape
    C = PW // len(POOL_WINDOWS)
    tm = _tile(S, 512)
    hb = tm // HALO

    def body(u_ref, uh_ref, wl_ref, sc_ref, o_ref):
        i = pl.program_id(0)
        pooled = _pooled_groups(u_ref, uh_ref, i, tm, C)
        for gi in range(len(POOL_WINDOWS)):
            y = _dot_nn(pooled[gi].astype(BF16), wl_ref[gi])
            o_ref[:, gi * C:(gi + 1) * C] = (y * sc_ref[:, gi * C:(gi + 1) * C]).astype(o_ref.dtype)

    return pl.pallas_call(
        body,
        grid=(S // tm,),
        in_specs=[
            pl.BlockSpec((tm, PW), lambda i: (i, 0)),
            pl.BlockSpec((HALO, PW), lambda i: (jnp.maximum(i * hb - 1, 0), 0)),
            pl.BlockSpec((len(POOL_WINDOWS), C, C), lambda i: (0, 0, 0)),
            pl.BlockSpec((1, PW), lambda i: (0, 0)),
        ],
        out_specs=pl.BlockSpec((tm, PW), lambda i: (i, 0)),
        out_shape=jax.ShapeDtypeStruct((S, PW), BF16),
        compiler_params=_cparams("parallel"),
        name=name,
    )(u, u, wl, scale)


def _pool_bwd(u, dp, wl, scale, name):
    S, PW = u.shape
    G = len(POOL_WINDOWS)
    C = PW // G
    tm = _tile(S, 512)
    hb, ni = tm // HALO, S // tm
    n = tm + HALO

    def body(u_ref, uh_ref, dp_ref, dpn_ref, wl_ref, sc_ref, du_ref, dwl_ref, dsc_ref):
        i = pl.program_id(0)

        @pl.when(i == 0)
        def _():
            dwl_ref[...] = jnp.zeros_like(dwl_ref)
            dsc_ref[...] = jnp.zeros_like(dsc_ref)

        pooled = _pooled_groups(u_ref, uh_ref, i, tm, C)
        dpc = dp_ref[...].astype(F32)
        dpn = jnp.where(i == ni - 1, 0.0, dpn_ref[...].astype(F32))
        sc = sc_ref[...]
        dyl = jnp.concatenate([dpc, dpn], axis=0) * sc
        for gi, w in enumerate(POOL_WINDOWS):
            cols = slice(gi * C, (gi + 1) * C)
            pb = pooled[gi].astype(BF16)
            ylin = _dot_nn(pb, wl_ref[gi])
            dsc_ref[:, cols] += jnp.sum(dpc[:, cols] * ylin, axis=0, keepdims=True)
            dylg = dyl[:, cols].astype(BF16)
            dwl_ref[gi] += _dot(pb, dylg[:tm], ((0,), (0,)))
            dpool = _dot_nt(dylg, wl_ref[gi])
            e = dpool / _pool_counts(i, tm, n, w)
            k = 1
            while k < w:
                e = e + pltpu.roll(e, n - k, 0)
                k *= 2
            du_ref[:, cols] = (e[:tm] - dpool[:tm]).astype(du_ref.dtype)

    return pl.pallas_call(
        body,
        grid=(ni,),
        in_specs=[
            pl.BlockSpec((tm, PW), lambda i: (i, 0)),
            pl.BlockSpec((HALO, PW), lambda i: (jnp.maximum(i * hb - 1, 0), 0)),
            pl.BlockSpec((tm, PW), lambda i: (i, 0)),
            pl.BlockSpec((HALO, PW), lambda i: (jnp.minimum((i + 1) * hb, S // HALO - 1), 0)),
            pl.BlockSpec((G, C, C), lambda i: (0, 0, 0)),
            pl.BlockSpec((1, PW), lambda i: (0, 0)),
        ],
        out_specs=[pl.BlockSpec((tm, PW), lambda i: (i, 0)), pl.BlockSpec((G, C, C), lambda i: (0, 0, 0)), pl.BlockSpec((1, PW), lambda i: (0, 0))],
        out_shape=[jax.ShapeDtypeStruct((S, PW), BF16), jax.ShapeDtypeStruct((G, C, C), F32), jax.ShapeDtypeStruct((1, PW), F32)],
        compiler_params=_cparams("arbitrary"),
        name=name,
    )(u, u, dp, dp, wl, scale)


def _band_masks():
    ii = lax.broadcasted_iota(jnp.int32, (SPAN, SPAN), 0)
    kk = lax.broadcasted_iota(jnp.int32, (SPAN, SPAN), 1)
    return ((ii + SPAN - kk).astype(F32), kk >= ii), ((ii - kk).astype(F32), kk <= ii)


def _attn_chunk(L):
    return min(4 * SPAN, L)


def _attn_fwd(qv, d, cols, g, name):
    L = qv.shape[0]
    CQ = _attn_chunk(L)
    nb = CQ // SPAN
    cq, ck, cv = cols

    def body(q_ref, k_ref, v_ref, kp_ref, vp_ref, o_ref, lse_ref):
        c = pl.program_id(1)
        (jp, mp), (jc, mc) = _band_masks()
        for h in range(HEADS_PER_GROUP):
            hs = slice(h * HEAD_DIM, (h + 1) * HEAD_DIM)
            slope = ALIBI_SLOPES[g * HEADS_PER_GROUP + h] * d
            for b in range(nb):
                rows = slice(b * SPAN, (b + 1) * SPAN)
                q = q_ref[rows, hs]
                kc, vc = k_ref[rows, hs], v_ref[rows, hs]
                if b == 0:
                    kp, vp, okp = kp_ref[:, hs], vp_ref[:, hs], jnp.logical_and(mp, c > 0)
                else:
                    prev = slice((b - 1) * SPAN, b * SPAN)
                    kp, vp, okp = k_ref[prev, hs], v_ref[prev, hs], mp
                sc = jnp.where(mc, _dot_nt(q, kc) * ATTN_SCALE - slope * jc, NEG_BIG)
                sp = jnp.where(okp, _dot_nt(q, kp) * ATTN_SCALE - slope * jp, NEG_BIG)
                m = jnp.maximum(jnp.max(sc, axis=-1, keepdims=True), jnp.max(sp, axis=-1, keepdims=True))
                pc, pp = jnp.exp(sc - m), jnp.exp(sp - m)
                l = jnp.sum(pc, axis=-1, keepdims=True) + jnp.sum(pp, axis=-1, keepdims=True)
                o = (_dot_nn(pc.astype(BF16), vc) + _dot_nn(pp.astype(BF16), vp)) / l
                o_ref[rows, hs] = o
                lse_ref[rows, hs] = jnp.broadcast_to(m + jnp.log(l), (SPAN, HEAD_DIM))

    W = GROUP_WIDTH
    pb = CQ // SPAN
    cur = lambda f: pl.BlockSpec((CQ, W), lambda r, c: (c, f(r)))
    prv = lambda f: pl.BlockSpec((SPAN, W), lambda r, c: (jnp.maximum(c * pb - 1, 0), f(r)))
    out = pl.BlockSpec((CQ, W), lambda r, c: (c, r))
    return pl.pallas_call(
        body,
        grid=(d, L // CQ),
        in_specs=[cur(cq), cur(ck), cur(cv), prv(ck), prv(cv)],
        out_specs=[out, out],
        out_shape=[jax.ShapeDtypeStruct((L, d * W), F32)] * 2,
        compiler_params=_cparams("parallel", "parallel"),
        name=name,
    )(qv, qv, qv, qv, qv)


def _attn_merge(os_, lses, name):
    S, W = os_[0].shape
    tm = _tile(S, 512)

    def body(o0, o1, o2, l0, l1, l2, y_ref, lse_ref):
        ls = [l0[...], l1[...], l2[...]]
        m = jnp.maximum(jnp.maximum(ls[0], ls[1]), ls[2])
        es = [jnp.exp(v - m) for v in ls]
        tot = es[0] + es[1] + es[2]
        y = (es[0] * o0[...] + es[1] * o1[...] + es[2] * o2[...]) / tot
        y_ref[...] = y.astype(y_ref.dtype)
        lse_ref[...] = m + jnp.log(tot)

    row = pl.BlockSpec((tm, W), lambda i: (i, 0))
    return pl.pallas_call(
        body,
        grid=(S // tm,),
        in_specs=[row] * 6,
        out_specs=[row, row],
        out_shape=[jax.ShapeDtypeStruct((S, W), BF16), jax.ShapeDtypeStruct((S, W), F32)],
        compiler_params=_cparams("parallel"),
        name=name,
    )(*os_, *lses)


def _attn_bwd(qv, dav, yv, lsev, d, cols, g, name):
    L = qv.shape[0]
    CQ = _attn_chunk(L)
    nb = CQ // SPAN
    nchunk = L // CQ
    cq, ck, cv = cols

    def body(q_ref, k_ref, v_ref, kp_ref, vp_ref, qn_ref, da_ref, dan_ref, y_ref, yn_ref, lse_ref, lsen_ref, dq_ref, dk_ref, dv_ref):
        c = pl.program_id(1)
        (jp, mp), (jc, mc) = _band_masks()
        for h in range(HEADS_PER_GROUP):
            hs = slice(h * HEAD_DIM, (h + 1) * HEAD_DIM)
            slope = ALIBI_SLOPES[g * HEADS_PER_GROUP + h] * d
            dq = [None] * nb
            dk = [None] * nb
            dv = [None] * nb

            def add(lst, idx, val):
                lst[idx] = val if lst[idx] is None else lst[idx] + val

            for qb in range(nb + 1):
                if qb < nb:
                    rows = slice(qb * SPAN, (qb + 1) * SPAN)
                    q, da, yy, lse = q_ref[rows, hs], da_ref[rows, hs], y_ref[rows, hs], lse_ref[rows, hs]
                else:
                    q, da, yy, lse = qn_ref[:, hs], dan_ref[:, hs], yn_ref[:, hs], lsen_ref[:, hs]
                dd = jnp.sum(da.astype(F32) * yy.astype(F32), axis=-1, keepdims=True)
                lse_col = lse[:, 0:1]
                for kb in (qb - 1, qb):
                    if kb >= nb:
                        continue
                    if kb < 0:
                        kk, vv, ok = kp_ref[:, hs], vp_ref[:, hs], jnp.logical_and(mp, c > 0)
                    else:
                        krows = slice(kb * SPAN, (kb + 1) * SPAN)
                        kk, vv = k_ref[krows, hs], v_ref[krows, hs]
                        ok = mc if kb == qb else (mp if qb < nb else jnp.logical_and(mp, c < nchunk - 1))
                    jj = jc if kb == qb else jp
                    s = jnp.where(ok, _dot_nt(q, kk) * ATTN_SCALE - slope * jj, NEG_BIG)
                    p = jnp.exp(s - lse_col)
                    ds = p * (_dot_nt(da, vv) - dd)
                    if qb < nb:
                        add(dq, qb, _dot_nn(ds.astype(BF16), kk))
                    if kb >= 0:
                        add(dv, kb, _dot_nn(p.T.astype(BF16), da))
                        add(dk, kb, _dot_nn(ds.T.astype(BF16), q))
            for b in range(nb):
                rows = slice(b * SPAN, (b + 1) * SPAN)
                dq_ref[rows, hs] = (dq[b] * ATTN_SCALE).astype(dq_ref.dtype)
                dk_ref[rows, hs] = (dk[b] * ATTN_SCALE).astype(dk_ref.dtype)
                dv_ref[rows, hs] = dv[b].astype(dv_ref.dtype)

    W = GROUP_WIDTH
    pb = CQ // SPAN
    nblocks = L // SPAN
    cur = lambda f: pl.BlockSpec((CQ, W), lambda r, c: (c, f(r)))
    prv = lambda f: pl.BlockSpec((SPAN, W), lambda r, c: (jnp.maximum(c * pb - 1, 0), f(r)))
    nxt = lambda f: pl.BlockSpec((SPAN, W), lambda r, c: (jnp.minimum((c + 1) * pb, nblocks - 1), f(r)))
    ident = lambda r: r
    out = pl.BlockSpec((CQ, W), lambda r, c: (c, r))
    return pl.pallas_call(
        body,
        grid=(d, nchunk),
        in_specs=[cur(cq), cur(ck), cur(cv), prv(ck), prv(cv), nxt(cq), cur(ident), nxt(ident), cur(ident), nxt(ident), cur(ident), nxt(ident)],
        out_specs=[out, out, out],
        out_shape=[jax.ShapeDtypeStruct((L, d * W), BF16)] * 3,
        compiler_params=_cparams("parallel", "parallel"),
        name=name,
    )(qv, qv, qv, qv, qv, qv, dav, dav, yv, yv, lsev, lsev)


def _row_block(R, C, bytes_per_row_elem=4, budget=1 << 20):
    if R % 8:
        return R
    best = 8
    t = 8
    while t <= R:
        if R % t == 0 and t * C * bytes_per_row_elem <= budget:
            best = t
        t += 8
    return best


def _adamw(w, g, m, v, name):
    R, C = w.shape
    tr = _row_block(R, C)
    c1 = 1.0 - ADAM_B1 ** ADAM_STEP
    c2 = 1.0 - ADAM_B2 ** ADAM_STEP

    def body(w_ref, g_ref, m_ref, v_ref, d_ref, nm_ref, nv_ref):
        gv = g_ref[...]
        nm = ADAM_B1 * m_ref[...] + (1.0 - ADAM_B1) * gv
        nv = ADAM_B2 * v_ref[...] + (1.0 - ADAM_B2) * (gv * gv)
        d_ref[...] = -ADAM_LR * ((nm / c1) / (jnp.sqrt(nv / c2) + ADAM_EPS) + ADAM_WD * w_ref[...])
        nm_ref[...] = nm
        nv_ref[...] = nv

    blk = pl.BlockSpec((tr, C), lambda i: (i, 0))
    return pl.pallas_call(
        body,
        grid=(R // tr,),
        in_specs=[blk] * 4,
        out_specs=[blk] * 3,
        out_shape=[jax.ShapeDtypeStruct((R, C), F32)] * 3,
        compiler_params=_cparams("parallel"),
        name=name,
    )(w, g, m, v)


def _sum_pieces(grad, axis, recv, pos, name):
    n, pr, pc = recv.shape
    tr = _row_block(pr, pc, bytes_per_row_elem=(n + 1) * recv.dtype.itemsize, budget=4 << 20)
    nblk = pr // tr
    if axis == 1:
        own_map = lambda i, p: (p[1] * nblk + i, p[0])
    else:
        own_map = lambda i, p: ((2 * p[0] + p[1]) * nblk + i, 0)

    def body(p_ref, own_ref, r_ref, o_ref):
        acc = own_ref[...].astype(F32)
        for s in range(n):
            acc = acc + r_ref[s].astype(F32)
        o_ref[...] = acc

    return pl.pallas_call(
        body,
        grid_spec=pltpu.PrefetchScalarGridSpec(
            num_scalar_prefetch=1,
            grid=(nblk,),
            in_specs=[pl.BlockSpec((tr, pc), own_map), pl.BlockSpec((n, tr, pc), lambda i, p: (0, i, 0))],
            out_specs=pl.BlockSpec((tr, pc), lambda i, p: (p[1] * nblk + i, 0)),
        ),
        out_shape=jax.ShapeDtypeStruct((2 * pr, pc), F32),
        compiler_params=_cparams("parallel"),
        name=name,
    )(pos, grad, recv)


def _sum_small(own, recv, me, name):
    n, R, C = recv.shape
    tr = _row_block(R, C, bytes_per_row_elem=(n + 1) * 4, budget=4 << 20)

    def body(me_ref, own_ref, r_ref, o_ref):
        acc = None
        for dev in range(n + 1):
            k = jnp.bitwise_xor(me_ref[0], dev)
            term = jnp.where(k == 0, own_ref[...], r_ref[jnp.maximum(k - 1, 0)])
            acc = term if acc is None else acc + term
        o_ref[...] = acc

    return pl.pallas_call(
        body,
        grid_spec=pltpu.PrefetchScalarGridSpec(
            num_scalar_prefetch=1,
            grid=(R // tr,),
            in_specs=[pl.BlockSpec((tr, C), lambda i, m: (i, 0)), pl.BlockSpec((n, tr, C), lambda i, m: (0, i, 0))],
            out_specs=pl.BlockSpec((tr, C), lambda i, m: (i, 0)),
        ),
        out_shape=jax.ShapeDtypeStruct((R, C), F32),
        compiler_params=_cparams("parallel"),
        name=name,
    )(me, own, recv)


def _place(shard, axis, pos, dtype, name):
    shp = list(shard.shape)
    shp[axis] *= N_CHIPS
    if shard.ndim == 3:
        assert axis == 1
        in_spec = pl.BlockSpec(shard.shape, lambda i, p: (0, 0, 0))
        out_spec = pl.BlockSpec(shard.shape, lambda i, p: (0, p[0], 0))
        grid = (1,)
    else:
        R, C = shard.shape
        tr = _row_block(R, C, bytes_per_row_elem=4, budget=2 << 20)
        nblk = R // tr
        in_spec = pl.BlockSpec((tr, C), lambda i, p: (i, 0))
        out_spec = pl.BlockSpec((tr, C), (lambda i, p: (i, p[0])) if axis == 1 else (lambda i, p: (p[0] * nblk + i, 0)))
        grid = (nblk,)

    def body(p_ref, s_ref, o_ref):
        o_ref[...] = s_ref[...].astype(o_ref.dtype)

    return pl.pallas_call(
        body,
        grid_spec=pltpu.PrefetchScalarGridSpec(num_scalar_prefetch=1, grid=grid, in_specs=[in_spec], out_specs=out_spec),
        out_shape=jax.ShapeDtypeStruct(tuple(shp), dtype),
        compiler_params=_cparams("parallel"),
        name=name,
    )(pos, shard)


HBM = pl.BlockSpec(memory_space=pltpu.HBM)
SEM = pl.BlockSpec(memory_space=pltpu.SEMAPHORE)
DATAFLOW = pltpu.SideEffectType.DATAFLOW_SIDE_EFFECTING


def _position():
    return lax.axis_index("x"), lax.axis_index("y"), lax.axis_index("c")


def _peer(k):
    x, y, c = _position()
    return ((1 - x) if k & 4 else x, (1 - y) if k & 2 else y, (1 - c) if k & 1 else c)


def _shard_slice(ref, axis, idx, size):
    start = idx * size
    if axis == ref.ndim - 1:
        start = pl.multiple_of(start, 128)
    ix = [slice(None)] * ref.ndim
    ix[axis] = pl.ds(start, size)
    return ref.at[tuple(ix)]


def _gather_plan(axes):
    def plan(refs):
        x, y, c = _position()
        out = []
        for ref, ax in zip(refs, axes):
            mine = _shard_slice(ref, ax, 2 * x + y, ref.shape[ax] // N_CHIPS)
            for k in (4, 2, 6):
                px, py, _ = _peer(k)
                out.append((mine, mine, (px, py, c)))
        return out
    return plan


def _scatter_plan(axes):
    m = len(axes)

    def plan(refs):
        out = []
        for t in range(m):
            grad, recv = refs[t], refs[m + t]
            _, pr, pc = recv.shape
            for k in range(1, N_DEV):
                px, py, pcore = _peer(k)
                if axes[t] == 0:
                    piece = grad.at[pl.ds(((2 * px + py) * 2 + pcore) * pr, pr), :]
                else:
                    piece = grad.at[pl.ds(pcore * pr, pr), pl.ds(pl.multiple_of((2 * px + py) * pc, 128), pc)]
                out.append((piece, recv.at[k - 1], (px, py, pcore)))
        return out
    return plan


def _broadcast_plan(refs):
    small, recv = refs
    return [(small, recv.at[k - 1], _peer(k)) for k in range(1, N_DEV)]


def _start_all(plan, refs, send_sems, recv_sems):
    for q, (src, dst, dev) in enumerate(plan(refs)):
        pltpu.make_async_remote_copy(src_ref=src, dst_ref=dst, send_sem=send_sems.at[q], recv_sem=recv_sems.at[q], device_id=dev, device_id_type=MESH).start()


def _wait_all(plan, refs, send_sems, recv_sems):
    for q, (src, dst, dev) in enumerate(plan(refs)):
        cp = pltpu.make_async_remote_copy(src_ref=src, dst_ref=dst, send_sem=send_sems.at[q], recv_sem=recv_sems.at[q], device_id=dev, device_id_type=MESH)
        cp.wait_send()
        cp.wait_recv()


def _push(bufs, plan, ncopies, name):
    n = len(bufs)

    def body(*refs):
        outs = refs[n:2 * n]
        send_sems, recv_sems = refs[2 * n:]
        _start_all(plan, outs, send_sems, recv_sems)
        _wait_all(plan, outs, send_sems, recv_sems)

    return pl.pallas_call(
        body,
        in_specs=[ANY] * n,
        out_specs=[ANY] * n,
        out_shape=[jax.ShapeDtypeStruct(b.shape, b.dtype) for b in bufs],
        input_output_aliases={t: t for t in range(n)},
        scratch_shapes=[pltpu.SemaphoreType.DMA((ncopies,)), pltpu.SemaphoreType.DMA((ncopies,))],
        name=name,
    )(*bufs)


def _push_start(bufs, plan, ncopies, name, after=None):
    n = len(bufs)
    extra = [] if after is None else [after]

    def body(*refs):
        ins = refs[:n]
        first_out = n + len(extra)
        send_sems, recv_sems, token = refs[first_out], refs[first_out + 1], refs[-1]
        _start_all(plan, ins, send_sems, recv_sems)
        token[...] = jnp.zeros_like(token)

    res = pl.pallas_call(
        body,
        name=name,
        out_shape=(pltpu.SemaphoreType.DMA((ncopies,)), pltpu.SemaphoreType.DMA((ncopies,)), *[pltpu.HBM(b.shape, b.dtype) for b in bufs],
                   jax.ShapeDtypeStruct((8, 128), F32)),
        in_specs=[HBM] * n + [ANY] * len(extra),
        out_specs=(SEM, SEM, *[HBM] * n, pl.BlockSpec(memory_space=pltpu.VMEM)),
        input_output_aliases={t: t + 2 for t in range(n)},
        compiler_params=pltpu.CompilerParams(has_side_effects=DATAFLOW),
    )(*[pltpu.with_memory_space_constraint(b, pltpu.HBM) for b in bufs], *extra)
    return res[0], res[1], list(res[2:2 + n]), res[-1]


def _push_wait(send_sems, recv_sems, bufs, plan, after, name):
    n = len(bufs)

    def body(*refs):
        ins = refs[:n]
        _wait_all(plan, ins, refs[n], refs[n + 1])

    return pl.pallas_call(
        body,
        name=name,
        out_shape=tuple(pltpu.HBM(b.shape, b.dtype) for b in bufs),
        in_specs=[HBM] * n + [SEM, SEM, ANY],
        out_specs=tuple([HBM] * n),
        input_output_aliases={t: t for t in range(n)},
        compiler_params=pltpu.CompilerParams(has_side_effects=DATAFLOW),
    )(*bufs, send_sems, recv_sems, after)


EXCHANGE_CHUNKS = 2


def _exchange_plan(refs):
    x, y, c = _position()
    out = []
    for ref in refs:
        rows = ref.shape[0] // (2 * EXCHANGE_CHUNKS)
        for q in range(EXCHANGE_CHUNKS):
            mine = ref.at[pl.ds((c * EXCHANGE_CHUNKS + q) * rows, rows), :]
            out.append((mine, mine, (x, y, 1 - c)))
    return out


def _dilated_view(a, d):
    S, W = a.shape
    return a.reshape(S // d, d * W)


def _natural_view(a, d):
    L, W = a.shape
    return a.reshape(L * d, W // d)


LATE_WEIGHTS = ("w_pool_lin", "w_pool_out", "w_attn_out", "w_out", "w_up", "conv_w", "w_down")


def _local_step(x, tgt, w, late_weights, send):
    S, D = x.shape
    PW = w["pool_scale"].shape[1]
    o_q = PW
    o_g = PW + 3 * ATTN_WIDTH
    QKV = 3 * ATTN_WIDTH

    h1 = _rms_fwd(x, w["g_mix"], "rms1")
    proj_tiles = (_tile(S, 1024), 512, D)
    u = _mm(h1, w["w_in"], mode="nn", dims=(S, PW, D), tiles=proj_tiles, out_dtypes=(F32,), name="proj_u")
    qkv = _mm(h1, w["w_in"], mode="nn", dims=(S, QKV, D), tiles=proj_tiles, b_off=(0, o_q), name="proj_qkv")

    def gate_epilogue(acc, ex, outs):
        outs[0][...] = 1.0 / (1.0 + jnp.exp(-(acc + ex[0][...])))

    gates = _mm(h1, w["w_in"], mode="nn", dims=(S, 2 * D, D), tiles=proj_tiles, b_off=(0, o_g), out_dtypes=(F32,), epilogue=gate_epilogue,
                extras=[(w["b_gate"], "n", (0, 0))], name="proj_gates")

    ident3 = (lambda r: 0, lambda r: 3, lambda r: 6)
    trip = (lambda r: 3 * r, lambda r: 3 * r + 1, lambda r: 3 * r + 2)
    qviews, os_, lses = [], [], []
    for gi, (_, d) in enumerate(ATTN_GROUPS):
        if d == 1:
            assert gi == 0
            qv, cols = qkv, ident3
        else:
            qv = _dilated_view(qkv.reshape(S, 9, GROUP_WIDTH)[:, gi::3, :].reshape(S, 3 * GROUP_WIDTH), d)
            cols = trip
        qviews.append((qv, cols))
        o, lse = _attn_fwd(qv, d, cols, gi, f"attn_fwd{gi}")
        os_.append(_natural_view(o, d))
        lses.append(_natural_view(lse, d))
    attn, lse_tot = _attn_merge(os_, lses, "attn_merge")

    w = dict(w, **late_weights(attn))
    F = w["w_down"].shape[0]
    pool_out = _pool_fwd(u, w["w_pool_lin"], w["pool_scale"], "pool_fwd")
    y_pool = _mm(pool_out, w["w_pool_out"], mode="nn", dims=(S, D, PW), name="y_pool")

    def mix_epilogue(acc, ex, outs):
        outs[0][...] = acc.astype(BF16)
        outs[1][...] = (ex[0][...] * ex[2][...].astype(F32) + ex[1][...] * acc).astype(BF16)

    y_attn, mixed = _mm(attn, w["w_attn_out"], mode="nn", dims=(S, D, GROUP_WIDTH), out_dtypes=(BF16, BF16), epilogue=mix_epilogue,
                        extras=[(gates, "mn", (0, 0)), (gates, "mn", (0, D)), (y_pool, "mn", (0, 0))], name="y_attn_mix")

    def residual_epilogue(acc, ex, outs):
        outs[0][...] = ex[0][...] + acc

    x2 = _mm(mixed, w["w_out"], mode="nn", dims=(S, D, D), out_dtypes=(F32,), epilogue=residual_epilogue, extras=[(x, "mn", (0, 0))], name="out_proj")

    h2 = _rms_fwd(x2, w["g_ffn"], "rms2")
    up = _mm(h2, w["w_up"], mode="nn", dims=(S, 2 * F, D), name="up_proj")
    f = _convglu_fwd(up, w["conv_w"], w["conv_b"], "convglu_fwd")
    x3 = _mm(f, w["w_down"], mode="nn", dims=(S, D, F), out_dtypes=(F32,), epilogue=residual_epilogue, extras=[(x2, "mn", (0, 0))], name="down_proj")

    g = {}
    dx3, dx3b, g["g_final"], loss_cols = _loss_head(x3, tgt, w["g_final"], "loss_head")

    g["w_down"] = _mm(f, dx3b, mode="tn", dims=(F, D, S), name="dw_down")
    sent = send(("w_down",), g)
    df = _mm(dx3b, w["w_down"], mode="nt", dims=(S, F, D), name="d_f")
    da, db, dcb_a, dcb_b, dcw_a, dcw_b = _convglu_bwd(df, up, w["conv_w"], w["conv_b"] + sent, "convglu_bwd")
    g["conv_b"] = jnp.concatenate([dcb_a, dcb_b], axis=1)
    g["conv_w"] = jnp.concatenate([dcw_a, dcw_b], axis=1)
    dup = _conv_transpose(da, db, w["conv_w"], "conv_transpose")
    g["w_up"] = _mm(h2, dup, mode="tn", dims=(D, 2 * F, S), name="dw_up")
    sent = send(("w_up",), g)
    dh2 = _mm(dup, w["w_up"], mode="nt", dims=(S, D, 2 * F), name="d_h2")
    dx2, dx2b, g["g_ffn"] = _rms_bwd(dh2, x2, w["g_ffn"] + sent, dx3, "rms2_bwd", True)

    g["w_out"] = _mm(mixed, dx2b, mode="tn", dims=(D, D, S), name="dw_out")
    dmixed = _mm(dx2b, w["w_out"], mode="nt", dims=(S, D, D), name="d_mixed")
    dy_both, dpre, g["b_gate"] = _gate_bwd(dmixed, gates, y_pool, y_attn, "gate_bwd")

    g["w_pool_out"] = _mm(pool_out, dy_both, mode="tn", dims=(PW, D, S), name="dw_pool_out")
    g["w_attn_out"] = _mm(attn, dy_both, mode="tn", dims=(GROUP_WIDTH, D, S), b_off=(0, D), name="dw_attn_out")
    sent = send(("w_out", "w_pool_out", "w_attn_out"), g)
    dpool = _mm(dy_both, w["w_pool_out"], mode="nt", dims=(S, PW, D), name="d_pool")
    dattn = _mm(dy_both, w["w_attn_out"], mode="nt", dims=(S, GROUP_WIDTH, D), a_off=(0, D), name="d_attn")

    du, g["w_pool_lin"], g["pool_scale"] = _pool_bwd(u, dpool, w["w_pool_lin"], w["pool_scale"] + sent, "pool_bwd")
    g["loss_cols"] = loss_cols
    sent = send("small", g)

    dqs, dks, dvs = [], [], []
    for gi, (_, d) in enumerate(ATTN_GROUPS):
        qv, cols = qviews[gi]
        dq, dk, dv = _attn_bwd(qv, _dilated_view(dattn, d), _dilated_view(attn, d), _dilated_view(lse_tot, d), d, cols, gi, f"attn_bwd{gi}")
        dqs.append(_natural_view(dq, d))
        dks.append(_natural_view(dk, d))
        dvs.append(_natural_view(dv, d))
    dproj = jnp.concatenate([du] + dqs + dks + dvs + [dpre], axis=1)
    IN = dproj.shape[1]

    g["w_in"] = _mm(h1, dproj, mode="tn", dims=(D, IN, S), name="dw_in")
    sent = sent + send(("w_in",), g)
    dh1 = _mm(dproj, w["w_in"], mode="nt", dims=(S, D, IN), tiles=(_tile(S, 1024), _tile(D, 2048), _tile(IN, 512)), name="d_h1")
    (grad_x, g["g_mix"]) = _rms_bwd(dh1, x, w["g_mix"] + sent, dx2, "rms1_bwd", False)
    return loss_cols, grad_x, g


BIG = ("w_in", "w_pool_out", "w_attn_out", "w_out", "w_up", "w_down")
BIG_AXIS = {"w_in": 1, "w_pool_out": 1, "w_attn_out": 1, "w_out": 0, "w_up": 1, "w_down": 0}
GATHER_AXIS = dict(BIG_AXIS, w_pool_lin=1, conv_w=1)
SMALL = ("loss_cols", "b_gate", "w_pool_lin", "pool_scale", "g_ffn", "conv_w", "conv_b", "g_final")
SMALL_COLS = 1024
ORDER = ("g_mix", "w_in", "b_gate", "w_pool_lin", "pool_scale", "w_pool_out", "w_attn_out", "w_out", "g_ffn", "w_up", "conv_w", "conv_b", "w_down", "g_final")


def _as_rows(parts):
    flat = jnp.concatenate([p.astype(F32).reshape(-1) for p in parts])
    rows = -(-flat.shape[0] // (8 * SMALL_COLS)) * 8
    return jnp.pad(flat, (0, rows * SMALL_COLS - flat.shape[0])).reshape(rows, SMALL_COLS)


def kernel(x, g_mix, w_in, b_gate, w_pool_lin, pool_scale, w_pool_out, w_attn_out, w_out, g_ffn, w_up, conv_w, conv_b, w_down, g_final, loss_target, m_g_mix, m_w_in, m_b_gate, m_w_pool_lin, m_pool_scale, m_w_pool_out, m_w_attn_out, m_w_out, m_g_ffn, m_w_up, m_conv_w, m_conv_b, m_w_down, m_g_final, v_g_mix, v_w_in, v_b_gate, v_w_pool_lin, v_pool_scale, v_w_pool_out, v_w_attn_out, v_w_out, v_g_ffn, v_w_up, v_conv_w, v_conv_b, v_w_down, v_g_final):
    shard = dict(g_mix=g_mix, w_in=w_in, b_gate=b_gate, w_pool_lin=w_pool_lin, pool_scale=pool_scale, w_pool_out=w_pool_out, w_attn_out=w_attn_out,
                 w_out=w_out, g_ffn=g_ffn, w_up=w_up, conv_w=conv_w, conv_b=conv_b, w_down=w_down, g_final=g_final)
    mom = dict(g_mix=m_g_mix, w_in=m_w_in, b_gate=m_b_gate, w_pool_lin=m_w_pool_lin, pool_scale=m_pool_scale, w_pool_out=m_w_pool_out, w_attn_out=m_w_attn_out,
               w_out=m_w_out, g_ffn=m_g_ffn, w_up=m_w_up, conv_w=m_conv_w, conv_b=m_conv_b, w_down=m_w_down, g_final=m_g_final)
    vel = dict(g_mix=v_g_mix, w_in=v_w_in, b_gate=v_b_gate, w_pool_lin=v_w_pool_lin, pool_scale=v_pool_scale, w_pool_out=v_w_pool_out, w_attn_out=v_w_attn_out,
               w_out=v_w_out, g_ffn=v_g_ffn, w_up=v_w_up, conv_w=v_conv_w, conv_b=v_conv_b, w_down=v_w_down, g_final=v_g_final)
    chip = 2 * lax.axis_index("x") + lax.axis_index("y")
    pos = jnp.stack([chip, lax.axis_index("c")]).astype(jnp.int32)
    me = (2 * chip + lax.axis_index("c")).astype(jnp.int32).reshape(1)
    D = x.shape[2]

    placed = {k: _place(shard[k][0], GATHER_AXIS[k], pos, F32 if k == "conv_w" else BF16, f"place_{k}") for k in GATHER_AXIS}
    w_in_full, = _push([placed["w_in"]], _gather_plan([GATHER_AXIS["w_in"]]), 3, "comm_gather_w_in")
    late_plan = _gather_plan([GATHER_AXIS[k] for k in LATE_WEIGHTS])
    late_send, late_recv, late_bufs, late_token = _push_start([placed[k] for k in LATE_WEIGHTS], late_plan, 3 * len(LATE_WEIGHTS), "comm_gather_late_start",
                                                              after=w_in_full)

    def late_weights(after):
        return dict(zip(LATE_WEIGHTS, _push_wait(late_send, late_recv, late_bufs, late_plan, after, "comm_gather_late_wait")))

    pending = []

    def send(names, g):
        if names == "small":
            bufs = [_as_rows([g[k] for k in SMALL])]
            bufs.append(lax.empty((N_DEV - 1,) + bufs[0].shape, F32))
            plan, tag = _broadcast_plan, "small"
        else:
            bufs = [g[k] for k in names]
            for k in names:
                R, C = g[k].shape
                piece = (R // (2 * N_CHIPS), C) if BIG_AXIS[k] == 0 else (R // 2, C // N_CHIPS)
                bufs.append(lax.empty((N_DEV - 1,) + piece, BF16))
            plan, tag = _scatter_plan([BIG_AXIS[k] for k in names]), names[0]
        ncopies = (N_DEV - 1) * (len(bufs) // 2)
        send_sems, recv_sems, thru, token = _push_start(bufs, plan, ncopies, f"comm_scatter_start_{tag}")
        pending.append((names, send_sems, recv_sems, thru, plan, tag))
        return token[0, 0]

    w0 = dict(g_mix=shard["g_mix"] + late_token[0, 0], w_in=w_in_full, b_gate=shard["b_gate"], pool_scale=shard["pool_scale"], g_ffn=shard["g_ffn"],
              conv_b=shard["conv_b"], g_final=shard["g_final"].reshape(1, D))
    _, grad_x, gr = _local_step(x[0], loss_target[0], w0, late_weights, send)

    halves, small_parts = {}, None
    for names, send_sems, recv_sems, thru, plan, tag in pending:
        done = _push_wait(send_sems, recv_sems, thru, plan, grad_x, f"comm_scatter_wait_{tag}")
        if names == "small":
            small_parts = _sum_small(done[0], done[1], me, "sum_small").reshape(-1)
        else:
            m = len(names)
            for t, k in enumerate(names):
                halves[k] = _sum_pieces(done[t], BIG_AXIS[k], done[m + t], pos, f"sum_{k}")
    g_mix_own = _as_rows([gr["g_mix"]])
    _, g_mix_recv = _push([g_mix_own, lax.empty((N_DEV - 1,) + g_mix_own.shape, F32)], _broadcast_plan, N_DEV - 1, "comm_gather_g_mix")
    g_mix_sum = _sum_small(g_mix_own, g_mix_recv, me, "sum_g_mix").reshape(-1)[:D]
    wholes = _push([halves[k] for k in BIG], _exchange_plan, EXCHANGE_CHUNKS * len(BIG), "comm_exchange_halves")

    grads = {"g_mix": g_mix_sum.reshape(shard["g_mix"].shape)}
    for k, whole in zip(BIG, wholes):
        grads[k] = whole.reshape(shard[k].shape)
    off = 0
    loss = None
    for k in SMALL:
        sz = math.prod(gr[k].shape)
        fullg = small_parts[off:off + sz].reshape(gr[k].shape)
        off += sz
        if k == "loss_cols":
            loss = jnp.sum(fullg)
            continue
        if k in ("w_pool_lin", "conv_w"):
            n = shard[k].shape[2]
            fullg = lax.dynamic_slice_in_dim(fullg, chip * n, n, axis=1)
        grads[k] = fullg.reshape(shard[k].shape)

    deltas, new_m, new_v = {}, {}, {}
    for k in ORDER:
        shp = shard[k].shape
        two_d = (-1, shp[-1])
        dl, nm, nv = _adamw(shard[k].reshape(two_d), grads[k].reshape(two_d), mom[k].reshape(two_d), vel[k].reshape(two_d), f"adamw_{k}")
        deltas[k], new_m[k], new_v[k] = dl.reshape(shp), nm.reshape(shp), nv.reshape(shp)

    return (loss, grad_x[None], *[grads[k] for k in ORDER], *[deltas[k] for k in ORDER], *[new_m[k] for k in ORDER], *[new_v[k] for k in ORDER])
```

```python
import functools
import math

import jax
import jax.numpy as jnp
from jax import lax
from jax.experimental import pallas as pl
from jax.experimental.pallas import tpu as pltpu

F32 = jnp.float32
BF16 = jnp.bfloat16

RMS_EPS = 1e-6
POOL_WINDOWS = (2, 4, 8, 16)
ATTN_GROUPS = ((128, 1), (512, 4), (2048, 16))
HEADS_PER_GROUP = 4
HEAD_DIM = 128
N_ATTN_HEADS = HEADS_PER_GROUP * len(ATTN_GROUPS)
SPAN = 128
GROUP_WIDTH = HEADS_PER_GROUP * HEAD_DIM
ATTN_WIDTH = N_ATTN_HEADS * HEAD_DIM
ATTN_SCALE = HEAD_DIM ** -0.5
NEG_BIG = -1e30
ALIBI_SLOPES = tuple(2.0 ** (-8.0 * (h + 1) / N_ATTN_HEADS) for h in range(N_ATTN_HEADS))

ADAM_LR = 0.001
ADAM_B1 = 0.9
ADAM_B2 = 0.999
ADAM_EPS = 1e-08
ADAM_WD = 0.01
ADAM_STEP = 10

INV_SQRT2 = 1.0 / math.sqrt(2.0)
INV_SQRT_2PI = 1.0 / math.sqrt(2.0 * math.pi)

HALO = 16
VMEM_LIMIT = 56 * 1024 * 1024
N_CHIPS = 4
N_DEV = 8
MESH = pl.DeviceIdType.MESH
ANY = pl.BlockSpec(memory_space=pl.ANY)


def _cparams(*sem):
    return pltpu.CompilerParams(dimension_semantics=sem, vmem_limit_bytes=VMEM_LIMIT)


def _tile(n, pref, mult=128):
    t = (min(pref, n) // mult) * mult
    while t >= mult:
        if n % t == 0:
            return t
        t -= mult
    return n


def _dot(a, b, contract):
    return lax.dot_general(a, b, (contract, ((), ())), preferred_element_type=F32)


def _dot_nn(a, b):
    return _dot(a, b, ((1,), (0,)))


def _dot_nt(a, b):
    return _dot(a, b, ((1,), (1,)))


def _mm(a, b, *, mode, dims, name, tiles=None, out_dtypes=(BF16,), epilogue=None, extras=(), a_off=(0, 0), b_off=(0, 0)):
    M, N, K = dims
    if tiles is None:
        tiles = (_tile(M, 1408), _tile(N, 2816), _tile(K, 512)) if mode == "tn" else (_tile(M, 1024), _tile(N, 1536), _tile(K, 1408))
    tm, tn, tk = tiles
    assert M % tm == 0 and N % tn == 0 and K % tk == 0, (name, dims, tiles)
    nk = K // tk
    if mode == "nn":
        ab, bb, contract = (tm, tk), (tk, tn), ((1,), (0,))
        amap = lambda i, j, k: (i + a_off[0] // tm, k + a_off[1] // tk)
        bmap = lambda i, j, k: (k + b_off[0] // tk, j + b_off[1] // tn)
    elif mode == "nt":
        ab, bb, contract = (tm, tk), (tn, tk), ((1,), (1,))
        amap = lambda i, j, k: (i + a_off[0] // tm, k + a_off[1] // tk)
        bmap = lambda i, j, k: (j + b_off[0] // tn, k + b_off[1] // tk)
    else:
        ab, bb, contract = (tk, tm), (tk, tn), ((0,), (0,))
        amap = lambda i, j, k: (k + a_off[0] // tk, i + a_off[1] // tm)
        bmap = lambda i, j, k: (k + b_off[0] // tk, j + b_off[1] // tn)
    assert a_off[0] % ab[0] == 0 and a_off[1] % ab[1] == 0 and b_off[0] % bb[0] == 0 and b_off[1] % bb[1] == 0, name
    in_specs = [pl.BlockSpec(ab, amap), pl.BlockSpec(bb, bmap)]
    ex_arrays = []
    for arr, kind, off in extras:
        if kind == "mn":
            assert off[0] % tm == 0 and off[1] % tn == 0, name
            in_specs.append(pl.BlockSpec((tm, tn), lambda i, j, k, off=off: (i + off[0] // tm, j + off[1] // tn)))
        else:
            assert off[1] % tn == 0, name
            in_specs.append(pl.BlockSpec((1, tn), lambda i, j, k, off=off: (0, j + off[1] // tn)))
        ex_arrays.append(arr)
    ne, no = len(ex_arrays), len(out_dtypes)
    if epilogue is None:
        def epilogue(acc, ex, outs):
            outs[0][...] = acc.astype(outs[0].dtype)

    def body(*refs):
        a_ref, b_ref = refs[0], refs[1]
        ex, outs, acc = refs[2:2 + ne], refs[2 + ne:2 + ne + no], refs[-1]
        k = pl.program_id(2)

        @pl.when(k == 0)
        def _():
            acc[...] = jnp.zeros_like(acc)

        acc[...] += _dot(a_ref[...], b_ref[...], contract)

        @pl.when(k == nk - 1)
        def _():
            epilogue(acc[...], ex, outs)

    res = pl.pallas_call(
        body,
        grid=(M // tm, N // tn, nk),
        in_specs=in_specs,
        out_specs=[pl.BlockSpec((tm, tn), lambda i, j, k: (i, j)) for _ in out_dtypes],
        out_shape=[jax.ShapeDtypeStruct((M, N), dt) for dt in out_dtypes],
        scratch_shapes=[pltpu.VMEM((tm, tn), F32)],
        compiler_params=_cparams("parallel", "parallel", "arbitrary"),
        name=name,
    )(a, b, *ex_arrays)
    return res[0] if no == 1 else res


def _rms_fwd(x, g, name):
    S, D = x.shape
    tm = _tile(S, 256)

    def body(x_ref, g_ref, h_ref):
        xv = x_ref[...]
        r = lax.rsqrt(jnp.mean(xv * xv, axis=-1, keepdims=True) + RMS_EPS)
        h_ref[...] = (xv * r * g_ref[...]).astype(h_ref.dtype)

    return pl.pallas_call(
        body,
        grid=(S // tm,),
        in_specs=[pl.BlockSpec((tm, D), lambda i: (i, 0)), pl.BlockSpec((1, D), lambda i: (0, 0))],
        out_specs=pl.BlockSpec((tm, D), lambda i: (i, 0)),
        out_shape=jax.ShapeDtypeStruct((S, D), BF16),
        compiler_params=_cparams("parallel"),
        name=name,
    )(x, g)


def _rms_bwd(dh, x, g, dres, name, with_bf16):
    S, D = x.shape
    tm = _tile(S, 256)

    def body(dh_ref, x_ref, g_ref, dres_ref, *outs):
        dx_ref, dg_ref = outs[0], outs[-1]
        xv = x_ref[...]
        r = lax.rsqrt(jnp.mean(xv * xv, axis=-1, keepdims=True) + RMS_EPS)
        xr = xv * r
        dhv = dh_ref[...].astype(F32)

        @pl.when(pl.program_id(0) == 0)
        def _():
            dg_ref[...] = jnp.zeros_like(dg_ref)

        dg_ref[...] += jnp.sum(dhv * xr, axis=0, keepdims=True)
        u = dhv * g_ref[...]
        c = jnp.mean(u * xr, axis=-1, keepdims=True)
        dx = dres_ref[...] + r * (u - xr * c)
        dx_ref[...] = dx
        if with_bf16:
            outs[1][...] = dx.astype(BF16)

    row = pl.BlockSpec((tm, D), lambda i: (i, 0))
    vec = pl.BlockSpec((1, D), lambda i: (0, 0))
    out_specs = [row] + ([row] if with_bf16 else []) + [vec]
    out_shape = [jax.ShapeDtypeStruct((S, D), F32)] + ([jax.ShapeDtypeStruct((S, D), BF16)] if with_bf16 else []) + [jax.ShapeDtypeStruct((1, D), F32)]
    return pl.pallas_call(
        body,
        grid=(S // tm,),
        in_specs=[row, row, vec, row],
        out_specs=out_specs,
        out_shape=out_shape,
        compiler_params=_cparams("arbitrary"),
        name=name,
    )(dh, x, g, dres)


def _loss_head(x3, tgt, g, name):
    S, D = x3.shape
    tm = _tile(S, 256)

    def body(x_ref, t_ref, g_ref, dx_ref, dxb_ref, dg_ref, loss_ref):
        xv = x_ref[...]
        gv = g_ref[...]
        r = lax.rsqrt(jnp.mean(xv * xv, axis=-1, keepdims=True) + RMS_EPS)
        xr = xv * r
        e = xr * gv - t_ref[...]

        @pl.when(pl.program_id(0) == 0)
        def _():
            dg_ref[...] = jnp.zeros_like(dg_ref)
            loss_ref[...] = jnp.zeros_like(loss_ref)

        loss_ref[...] += jnp.sum(e * e, axis=0, keepdims=True) * (0.5 / D)
        dy = e * (1.0 / D)
        dg_ref[...] += jnp.sum(dy * xr, axis=0, keepdims=True)
        u = dy * gv
        c = jnp.mean(u * xr, axis=-1, keepdims=True)
        dx = r * (u - xr * c)
        dx_ref[...] = dx
        dxb_ref[...] = dx.astype(BF16)

    row = pl.BlockSpec((tm, D), lambda i: (i, 0))
    vec = pl.BlockSpec((1, D), lambda i: (0, 0))
    return pl.pallas_call(
        body,
        grid=(S // tm,),
        in_specs=[row, row, vec],
        out_specs=[row, row, vec, vec],
        out_shape=[jax.ShapeDtypeStruct((S, D), F32), jax.ShapeDtypeStruct((S, D), BF16), jax.ShapeDtypeStruct((1, D), F32), jax.ShapeDtypeStruct((1, D), F32)],
        compiler_params=_cparams("arbitrary"),
        name=name,
    )(x3, tgt, g)


def _conv_taps(cur_ref, halo_ref, w_ref, b_ref, first):
    cur = cur_ref[...].astype(F32)
    halo = jnp.where(first, 0.0, halo_ref[...].astype(F32))
    xx = jnp.concatenate([halo, cur], axis=0)
    p1 = pltpu.roll(xx, 1, 0)[HALO:]
    p2 = pltpu.roll(xx, 2, 0)[HALO:]
    w = w_ref[...]
    y = b_ref[...] + w[0:1] * p2 + w[1:2] * p1 + w[2:3] * cur
    return y, (cur, p1, p2)


def _convglu_specs(S, F, tm, tn, rows_axis):
    nj = F // tn
    if rows_axis == 0:
        ij = lambda f: (lambda i, j: f(i, j))
    else:
        ij = lambda f: (lambda j, i: f(i, j))
    hb = tm // HALO
    return [
        pl.BlockSpec((tm, tn), ij(lambda i, j: (i, j))),
        pl.BlockSpec((tm, tn), ij(lambda i, j: (i, j + nj))),
        pl.BlockSpec((HALO, tn), ij(lambda i, j: (jnp.maximum(i * hb - 1, 0), j))),
        pl.BlockSpec((HALO, tn), ij(lambda i, j: (jnp.maximum(i * hb - 1, 0), j + nj))),
        pl.BlockSpec((3, tn), ij(lambda i, j: (0, j))),
        pl.BlockSpec((3, tn), ij(lambda i, j: (0, j + nj))),
        pl.BlockSpec((1, tn), ij(lambda i, j: (0, j))),
        pl.BlockSpec((1, tn), ij(lambda i, j: (0, j + nj))),
    ]


def _convglu_fwd(up, cw, cb, name):
    S, F2 = up.shape
    F = F2 // 2
    tm, tn = _tile(S, 512), _tile(F, 512)

    def body(ua, ub, ha, hb, wa, wb, ba, bb, f_ref):
        first = pl.program_id(0) == 0
        a, _ = _conv_taps(ua, ha, wa, ba, first)
        b, _ = _conv_taps(ub, hb, wb, bb, first)
        f_ref[...] = (0.5 * a * (1.0 + lax.erf(a * INV_SQRT2)) * b).astype(f_ref.dtype)

    return pl.pallas_call(
        body,
        grid=(S // tm, F // tn),
        in_specs=_convglu_specs(S, F, tm, tn, 0),
        out_specs=pl.BlockSpec((tm, tn), lambda i, j: (i, j)),
        out_shape=jax.ShapeDtypeStruct((S, F), BF16),
        compiler_params=_cparams("parallel", "parallel"),
        name=name,
    )(up, up, up, up, cw, cw, cb, cb)


def _convglu_bwd(df, up, cw, cb, name):
    S, F2 = up.shape
    F = F2 // 2
    tm, tn = _tile(S, 512), _tile(F, 512)

    def body(df_ref, ua, ub, ha, hb, wa, wb, ba, bb, da_ref, db_ref, dba_ref, dbb_ref, dwa_ref, dwb_ref):
        first = pl.program_id(1) == 0
        a, pa = _conv_taps(ua, ha, wa, ba, first)
        b, pb = _conv_taps(ub, hb, wb, bb, first)
        dfv = df_ref[...].astype(F32)
        cdf = 0.5 * (1.0 + lax.erf(a * INV_SQRT2))
        pdf = jnp.exp(-0.5 * a * a) * INV_SQRT_2PI
        da = dfv * b * (cdf + a * pdf)
        db = dfv * (a * cdf)
        da_ref[...] = da.astype(BF16)
        db_ref[...] = db.astype(BF16)

        @pl.when(first)
        def _():
            for r in (dba_ref, dbb_ref, dwa_ref, dwb_ref):
                r[...] = jnp.zeros_like(r)

        for d, taps, dbias, dw in ((da, pa, dba_ref, dwa_ref), (db, pb, dbb_ref, dwb_ref)):
            dbias[...] += jnp.sum(d, axis=0, keepdims=True)
            dw[0:1, :] += jnp.sum(d * taps[2], axis=0, keepdims=True)
            dw[1:2, :] += jnp.sum(d * taps[1], axis=0, keepdims=True)
            dw[2:3, :] += jnp.sum(d * taps[0], axis=0, keepdims=True)

    tile = pl.BlockSpec((tm, tn), lambda j, i: (i, j))
    b1 = pl.BlockSpec((1, tn), lambda j, i: (0, j))
    b3 = pl.BlockSpec((3, tn), lambda j, i: (0, j))
    return pl.pallas_call(
        body,
        grid=(F // tn, S // tm),
        in_specs=[tile] + _convglu_specs(S, F, tm, tn, 1),
        out_specs=[tile, tile, b1, b1, b3, b3],
        out_shape=[jax.ShapeDtypeStruct((S, F), BF16)] * 2 + [jax.ShapeDtypeStruct((1, F), F32)] * 2 + [jax.ShapeDtypeStruct((3, F), F32)] * 2,
        compiler_params=_cparams("parallel", "arbitrary"),
        name=name,
    )(df, up, up, up, up, cw, cw, cb, cb)


def _conv_transpose(da, db, cw, name):
    S, F = da.shape
    tm, tn = _tile(S, 512), _tile(F, 512)
    nj, ni, hb = F // tn, S // tm, tm // HALO
    n = tm + HALO

    def body(a_ref, b_ref, an_ref, bn_ref, w_ref, o_ref):
        j, i = pl.program_id(0), pl.program_id(1)

        def run(c_ref, h_ref):
            cur = c_ref[...].astype(F32)
            halo = jnp.where(i == ni - 1, 0.0, h_ref[...].astype(F32))
            xx = jnp.concatenate([cur, halo], axis=0)
            n1 = pltpu.roll(xx, n - 1, 0)[:tm]
            n2 = pltpu.roll(xx, n - 2, 0)[:tm]
            w = w_ref[...]
            o_ref[...] = (w[2:3] * cur + w[1:2] * n1 + w[0:1] * n2).astype(o_ref.dtype)

        @pl.when(j < nj)
        def _():
            run(a_ref, an_ref)

        @pl.when(j >= nj)
        def _():
            run(b_ref, bn_ref)

    ja = lambda j: jnp.minimum(j, nj - 1)
    jb = lambda j: jnp.maximum(j - nj, 0)
    nxt = lambda i: jnp.minimum((i + 1) * hb, S // HALO - 1)
    return pl.pallas_call(
        body,
        grid=(2 * nj, ni),
        in_specs=[
            pl.BlockSpec((tm, tn), lambda j, i: (i, ja(j))),
            pl.BlockSpec((tm, tn), lambda j, i: (i, jb(j))),
            pl.BlockSpec((HALO, tn), lambda j, i: (nxt(i), ja(j))),
            pl.BlockSpec((HALO, tn), lambda j, i: (nxt(i), jb(j))),
            pl.BlockSpec((3, tn), lambda j, i: (0, j)),
        ],
        out_specs=pl.BlockSpec((tm, tn), lambda j, i: (i, j)),
        out_shape=jax.ShapeDtypeStruct((S, 2 * F), BF16),
        compiler_params=_cparams("parallel", "parallel"),
        name=name,
    )(da, db, da, db, cw)


def _gate_bwd(dmixed, gates, y_pool, y_attn, in_width, name):
    S, D = dmixed.shape
    tm, tn = _tile(S, 512), _tile(D, 512)
    nj = D // tn
    pre0 = (in_width - 2 * D) // tn
    assert pre0 * tn == in_width - 2 * D

    def body(dm_ref, g_ref, yp_ref, ya_ref, dy_ref, dpre_ref, db_ref):
        j = pl.program_id(0)

        @pl.when(pl.program_id(1) == 0)
        def _():
            db_ref[...] = jnp.zeros_like(db_ref)

        def run(y_ref):
            dm = dm_ref[...].astype(F32)
            gv = g_ref[...]
            dy_ref[...] = (dm * gv).astype(BF16)
            dpre = dm * y_ref[...].astype(F32) * gv * (1.0 - gv)
            dpre_ref[...] = dpre.astype(BF16)
            db_ref[...] += jnp.sum(dpre, axis=0, keepdims=True)

        @pl.when(j < nj)
        def _():
            run(yp_ref)

        @pl.when(j >= nj)
        def _():
            run(ya_ref)

    tile2 = pl.BlockSpec((tm, tn), lambda j, i: (i, j))
    return pl.pallas_call(
        body,
        grid=(2 * nj, S // tm),
        in_specs=[
            pl.BlockSpec((tm, tn), lambda j, i: (i, lax.rem(j, nj))),
            tile2,
            pl.BlockSpec((tm, tn), lambda j, i: (i, jnp.minimum(j, nj - 1))),
            pl.BlockSpec((tm, tn), lambda j, i: (i, jnp.maximum(j - nj, 0))),
        ],
        out_specs=[tile2, pl.BlockSpec((tm, tn), lambda j, i: (i, pre0 + j)), pl.BlockSpec((1, tn), lambda j, i: (0, j))],
        out_shape=[jax.ShapeDtypeStruct((S, 2 * D), BF16), jax.ShapeDtypeStruct((S, in_width), BF16), jax.ShapeDtypeStruct((1, 2 * D), F32)],
        compiler_params=_cparams("parallel", "arbitrary"),
        name=name,
    )(dmixed, gates, y_pool, y_attn)


def _pool_counts(i, tm, rows, w):
    t = i * tm + lax.broadcasted_iota(jnp.int32, (rows, 1), 0)
    return jnp.minimum(t + 1, w).astype(F32)


def _pooled_groups(u_ref, uh_ref, i, tm, C):
    cur = u_ref[...]
    halo = jnp.where(i == 0, 0.0, uh_ref[...])
    xx = jnp.concatenate([halo, cur], axis=0)
    out = []
    s = xx
    for gi, w in enumerate(POOL_WINDOWS):
        s = s + pltpu.roll(s, w // 2, 0)
        tot = s[HALO:, 0:C]
        out.append(tot / _pool_counts(i, tm, tm, w) - cur[:, gi * C:(gi + 1) * C])
        s = s[:, C:] if gi + 1 < len(POOL_WINDOWS) else s
    return out


def _pool_fwd(u, wl, scale, name):
    S, PW = u.shape
    C = PW // len(POOL_WINDOWS)
    tm = _tile(S, 512)
    hb = tm // HALO

    def body(u_ref, uh_ref, wl_ref, sc_ref, o_ref):
        i = pl.program_id(0)
        pooled = _pooled_groups(u_ref, uh_ref, i, tm, C)
        for gi in range(len(POOL_WINDOWS)):
            y = _dot_nn(pooled[gi].astype(BF16), wl_ref[gi])
            o_ref[:, gi * C:(gi + 1) * C] = (y * sc_ref[:, gi * C:(gi + 1) * C]).astype(o_ref.dtype)

    return pl.pallas_call(
        body,
        grid=(S // tm,),
        in_specs=[
            pl.BlockSpec((tm, PW), lambda i: (i, 0)),
            pl.BlockSpec((HALO, PW), lambda i: (jnp.maximum(i * hb - 1, 0), 0)),
            pl.BlockSpec((len(POOL_WINDOWS), C, C), lambda i: (0, 0, 0)),
            pl.BlockSpec((1, PW), lambda i: (0, 0)),
        ],
        out_specs=pl.BlockSpec((tm, PW), lambda i: (i, 0)),
        out_shape=jax.ShapeDtypeStruct((S, PW), BF16),
        compiler_params=_cparams("parallel"),
        name=name,
    )(u, u, wl, scale)


def _pool_bwd(u, dp, wl, scale, dproj, name):
    S, PW = u.shape
    G = len(POOL_WINDOWS)
    C = PW // G
    tm = _tile(S, 512)
    hb, ni = tm // HALO, S // tm
    n = tm + HALO

    def body(u_ref, uh_ref, dp_ref, dpn_ref, wl_ref, sc_ref, _, du_ref, dwl_ref, dsc_ref):
        i = pl.program_id(0)

        @pl.when(i == 0)
        def _():
            dwl_ref[...] = jnp.zeros_like(dwl_ref)
            dsc_ref[...] = jnp.zeros_like(dsc_ref)

        pooled = _pooled_groups(u_ref, uh_ref, i, tm, C)
        dpc = dp_ref[...].astype(F32)
        dpn = jnp.where(i == ni - 1, 0.0, dpn_ref[...].astype(F32))
        sc = sc_ref[...]
        dyl = jnp.concatenate([dpc, dpn], axis=0) * sc
        for gi, w in enumerate(POOL_WINDOWS):
            cols = slice(gi * C, (gi + 1) * C)
            pb = pooled[gi].astype(BF16)
            ylin = _dot_nn(pb, wl_ref[gi])
            dsc_ref[:, cols] += jnp.sum(dpc[:, cols] * ylin, axis=0, keepdims=True)
            dylg = dyl[:, cols].astype(BF16)
            dwl_ref[gi] += _dot(pb, dylg[:tm], ((0,), (0,)))
            dpool = _dot_nt(dylg, wl_ref[gi])
            e = dpool / _pool_counts(i, tm, n, w)
            k = 1
            while k < w:
                e = e + pltpu.roll(e, n - k, 0)
                k *= 2
            du_ref[:, cols] = (e[:tm] - dpool[:tm]).astype(du_ref.dtype)

    return pl.pallas_call(
        body,
        grid=(ni,),
        in_specs=[
            pl.BlockSpec((tm, PW), lambda i: (i, 0)),
            pl.BlockSpec((HALO, PW), lambda i: (jnp.maximum(i * hb - 1, 0), 0)),
            pl.BlockSpec((tm, PW), lambda i: (i, 0)),
            pl.BlockSpec((HALO, PW), lambda i: (jnp.minimum((i + 1) * hb, S // HALO - 1), 0)),
            pl.BlockSpec((G, C, C), lambda i: (0, 0, 0)),
            pl.BlockSpec((1, PW), lambda i: (0, 0)),
            ANY,
        ],
        out_specs=[pl.BlockSpec((tm, PW), lambda i: (i, 0)), pl.BlockSpec((G, C, C), lambda i: (0, 0, 0)), pl.BlockSpec((1, PW), lambda i: (0, 0))],
        out_shape=[jax.ShapeDtypeStruct(dproj.shape, dproj.dtype), jax.ShapeDtypeStruct((G, C, C), F32), jax.ShapeDtypeStruct((1, PW), F32)],
        input_output_aliases={6: 0},
        compiler_params=_cparams("arbitrary"),
        name=name,
    )(u, u, dp, dp, wl, scale, dproj)


def _band_masks():
    ii = lax.broadcasted_iota(jnp.int32, (SPAN, SPAN), 0)
    kk = lax.broadcasted_iota(jnp.int32, (SPAN, SPAN), 1)
    return ((ii + SPAN - kk).astype(F32), kk >= ii), ((ii - kk).astype(F32), kk <= ii)


ATTN_TILE = 16 * SPAN


def _unit_rows(r, b, d):
    return pl.ds(d * SPAN * b + r, SPAN, stride=d) if d > 1 else pl.ds(SPAN * b, SPAN)


def _f32_copies(refs, scratch, d):
    if d == 1:
        return list(refs)
    for ref, s in zip(refs, scratch):
        s[...] = ref[...].astype(F32)
    return list(scratch)


def _attn_fwd(qkv, d, g, name):
    S = qkv.shape[0]
    T = min(ATTN_TILE, S)
    P = SPAN * d
    nbk = T // P

    def body(q_ref, k_ref, v_ref, kp_ref, vp_ref, o_ref, lse_ref, *scratch):
        c = pl.program_id(0)
        (jp, mp), (jc, mc) = _band_masks()
        slopes = [ALIBI_SLOPES[g * HEADS_PER_GROUP + h] * d for h in range(HEADS_PER_GROUP)]
        slope = slopes[0]
        for h in range(1, HEADS_PER_GROUP):
            slope = jnp.where(pl.program_id(1) == h, slopes[h], slope)
        q_s, k_s, v_s, kp_s, vp_s = _f32_copies((q_ref, k_ref, v_ref, kp_ref, vp_ref), scratch[:5], d)
        o_s, l_s = (o_ref, lse_ref) if d == 1 else scratch[5:7]
        for r in range(d):
            for b in range(nbk):
                rows = _unit_rows(r, b, d)
                q = q_s[rows, :].astype(BF16)
                kc, vc = k_s[rows, :].astype(BF16), v_s[rows, :].astype(BF16)
                if b == 0:
                    prev = _unit_rows(r, 0, d)
                    kp, vp, okp = kp_s[prev, :].astype(BF16), vp_s[prev, :].astype(BF16), jnp.logical_and(mp, c > 0)
                else:
                    prev = _unit_rows(r, b - 1, d)
                    kp, vp, okp = k_s[prev, :].astype(BF16), v_s[prev, :].astype(BF16), mp
                sc = jnp.where(mc, _dot_nt(q, kc) * ATTN_SCALE - slope * jc, NEG_BIG)
                sp = jnp.where(okp, _dot_nt(q, kp) * ATTN_SCALE - slope * jp, NEG_BIG)
                m = jnp.maximum(jnp.max(sc, axis=-1, keepdims=True), jnp.max(sp, axis=-1, keepdims=True))
                pc, pp = jnp.exp(sc - m), jnp.exp(sp - m)
                l = jnp.sum(pc, axis=-1, keepdims=True) + jnp.sum(pp, axis=-1, keepdims=True)
                o_s[rows, :] = (_dot_nn(pc.astype(BF16), vc) + _dot_nn(pp.astype(BF16), vp)) / l
                l_s[rows, :] = jnp.broadcast_to(m + jnp.log(l), (SPAN, HEAD_DIM))
        if d > 1:
            o_ref[...] = o_s[...]
            lse_ref[...] = l_s[...]

    col = lambda kind: (lambda c, h: (c, kind * N_ATTN_HEADS + g * HEADS_PER_GROUP + h))
    pcol = lambda kind: (lambda c, h: (jnp.maximum(c * nbk - 1, 0), kind * N_ATTN_HEADS + g * HEADS_PER_GROUP + h))
    cur = lambda kind: pl.BlockSpec((T, HEAD_DIM), col(kind))
    prv = lambda kind: pl.BlockSpec((P, HEAD_DIM), pcol(kind))
    out = pl.BlockSpec((T, HEAD_DIM), lambda c, h: (c, h))
    scratch = [] if d == 1 else [pltpu.VMEM((T, HEAD_DIM), F32)] * 3 + [pltpu.VMEM((P, HEAD_DIM), F32)] * 2 + [pltpu.VMEM((T, HEAD_DIM), F32)] * 2
    return pl.pallas_call(
        body,
        grid=(S // T, HEADS_PER_GROUP),
        in_specs=[cur(0), cur(1), cur(2), prv(1), prv(2)],
        out_specs=[out, out],
        out_shape=[jax.ShapeDtypeStruct((S, GROUP_WIDTH), F32)] * 2,
        scratch_shapes=scratch,
        compiler_params=_cparams("parallel", "parallel"),
        name=name,
    )(qkv, qkv, qkv, qkv, qkv)


def _attn_merge(os_, lses, name):
    S, W = os_[0].shape
    tm = _tile(S, 512)

    def body(o0, o1, o2, l0, l1, l2, y_ref, lse_ref):
        ls = [l0[...], l1[...], l2[...]]
        m = jnp.maximum(jnp.maximum(ls[0], ls[1]), ls[2])
        es = [jnp.exp(v - m) for v in ls]
        tot = es[0] + es[1] + es[2]
        y = (es[0] * o0[...] + es[1] * o1[...] + es[2] * o2[...]) / tot
        y_ref[...] = y.astype(y_ref.dtype)
        lse_ref[...] = m + jnp.log(tot)

    row = pl.BlockSpec((tm, W), lambda i: (i, 0))
    return pl.pallas_call(
        body,
        grid=(S // tm,),
        in_specs=[row] * 6,
        out_specs=[row, row],
        out_shape=[jax.ShapeDtypeStruct((S, W), BF16), jax.ShapeDtypeStruct((S, W), F32)],
        compiler_params=_cparams("parallel"),
        name=name,
    )(*os_, *lses)


def _attn_bwd(qkv, dattn, y, lse, dproj, d, g, col0, name):
    S = qkv.shape[0]
    T = min(ATTN_TILE, S)
    P = SPAN * d
    nbk = T // P
    ntile = S // T

    def body(q_ref, k_ref, v_ref, kp_ref, vp_ref, qn_ref, da_ref, dan_ref, y_ref, yn_ref, lse_ref, lsen_ref, _, out_ref, dq_s, dk_s, dv_s, *scratch):
        c = pl.program_id(0)
        head_id = pl.program_id(1)
        kind = pl.program_id(2)

        @pl.when(kind == 0)
        def _():
            (jp, mp), (jc, mc) = _band_masks()
            slopes = [ALIBI_SLOPES[g * HEADS_PER_GROUP + h] * d for h in range(HEADS_PER_GROUP)]
            slope = slopes[0]
            for h in range(1, HEADS_PER_GROUP):
                slope = jnp.where(head_id == h, slopes[h], slope)
            q_s, k_s, v_s, da_s, y_s, kp_s, vp_s, qn_s, dan_s, yn_s = _f32_copies(
                (q_ref, k_ref, v_ref, da_ref, y_ref, kp_ref, vp_ref, qn_ref, dan_ref, yn_ref), scratch, d)
            for r in range(d):
                dq = [None] * nbk
                dk = [None] * nbk
                dv = [None] * nbk

                def add(lst, idx, val):
                    lst[idx] = val if lst[idx] is None else lst[idx] + val

                for qb in range(nbk + 1):
                    if qb < nbk:
                        rows = _unit_rows(r, qb, d)
                        q, da, yy, lse_blk = q_s[rows, :], da_s[rows, :], y_s[rows, :], lse_ref[rows, :]
                    else:
                        rows = _unit_rows(r, 0, d)
                        q, da, yy, lse_blk = qn_s[rows, :], dan_s[rows, :], yn_s[rows, :], lsen_ref[rows, :]
                    lse_col = lse_blk[:, 0:1]
                    dd = jnp.sum(da.astype(F32) * yy.astype(F32), axis=-1, keepdims=True)
                    q, da = q.astype(BF16), da.astype(BF16)
                    for kb in (qb - 1, qb):
                        if kb >= nbk:
                            continue
                        if kb < 0:
                            krows = _unit_rows(r, 0, d)
                            kk, vv, ok = kp_s[krows, :].astype(BF16), vp_s[krows, :].astype(BF16), jnp.logical_and(mp, c > 0)
                        else:
                            krows = _unit_rows(r, kb, d)
                            kk, vv = k_s[krows, :].astype(BF16), v_s[krows, :].astype(BF16)
                            ok = mc if kb == qb else (mp if qb < nbk else jnp.logical_and(mp, c < ntile - 1))
                        jj = jc if kb == qb else jp
                        s = jnp.where(ok, _dot_nt(q, kk) * ATTN_SCALE - slope * jj, NEG_BIG)
                        p = jnp.exp(s - lse_col)
                        ds = p * (_dot_nt(da, vv) - dd)
                        if qb < nbk:
                            add(dq, qb, _dot_nn(ds.astype(BF16), kk))
                        if kb >= 0:
                            add(dv, kb, _dot_nn(p.T.astype(BF16), da))
                            add(dk, kb, _dot_nn(ds.T.astype(BF16), q))
                for b in range(nbk):
                    rows = _unit_rows(r, b, d)
                    dq_s[rows, :] = dq[b] * ATTN_SCALE
                    dk_s[rows, :] = dk[b] * ATTN_SCALE
                    dv_s[rows, :] = dv[b]
            out_ref[...] = dq_s[...].astype(out_ref.dtype)

        @pl.when(kind == 1)
        def _():
            out_ref[...] = dk_s[...].astype(out_ref.dtype)

        @pl.when(kind == 2)
        def _():
            out_ref[...] = dv_s[...].astype(out_ref.dtype)

    head = lambda h: g * HEADS_PER_GROUP + h
    cur = lambda kind: pl.BlockSpec((T, HEAD_DIM), lambda c, h, kd: (c, kind * N_ATTN_HEADS + head(h)))
    prv = lambda kind: pl.BlockSpec((P, HEAD_DIM), lambda c, h, kd: (jnp.maximum(c * nbk - 1, 0), kind * N_ATTN_HEADS + head(h)))
    nxt_row = lambda c: jnp.minimum((c + 1) * nbk, S // P - 1)
    qnext = pl.BlockSpec((P, HEAD_DIM), lambda c, h, kd: (nxt_row(c), head(h)))
    hcur = pl.BlockSpec((T, HEAD_DIM), lambda c, h, kd: (c, h))
    hnext = pl.BlockSpec((P, HEAD_DIM), lambda c, h, kd: (nxt_row(c), h))
    out = pl.BlockSpec((T, HEAD_DIM), lambda c, h, kd: (c, col0 + kd * N_ATTN_HEADS + head(h)))
    stage = [pltpu.VMEM((T, HEAD_DIM), F32)] * 3
    copies = [] if d == 1 else [pltpu.VMEM((T, HEAD_DIM), F32)] * 5 + [pltpu.VMEM((P, HEAD_DIM), F32)] * 5
    return pl.pallas_call(
        body,
        grid=(ntile, HEADS_PER_GROUP, 3),
        in_specs=[cur(0), cur(1), cur(2), prv(1), prv(2), qnext, hcur, hnext, hcur, hnext, hcur, hnext, ANY],
        out_specs=out,
        out_shape=jax.ShapeDtypeStruct(dproj.shape, dproj.dtype),
        input_output_aliases={12: 0},
        scratch_shapes=stage + copies,
        compiler_params=_cparams("parallel", "parallel", "arbitrary"),
        name=name,
    )(qkv, qkv, qkv, qkv, qkv, qkv, dattn, dattn, y, y, lse, lse, dproj)


def _row_block(R, C, bytes_per_row_elem=4, budget=1 << 20):
    if R % 8:
        return R
    best = 8
    t = 8
    while t <= R:
        if R % t == 0 and t * C * bytes_per_row_elem <= budget:
            best = t
        t += 8
    return best


def _adamw(w, g, m, v, name):
    R, C = w.shape
    tr = _row_block(R, C)
    c1 = 1.0 - ADAM_B1 ** ADAM_STEP
    c2 = 1.0 - ADAM_B2 ** ADAM_STEP

    def body(w_ref, g_ref, m_ref, v_ref, d_ref, nm_ref, nv_ref):
        gv = g_ref[...]
        nm = ADAM_B1 * m_ref[...] + (1.0 - ADAM_B1) * gv
        nv = ADAM_B2 * v_ref[...] + (1.0 - ADAM_B2) * (gv * gv)
        d_ref[...] = -ADAM_LR * ((nm / c1) / (jnp.sqrt(nv / c2) + ADAM_EPS) + ADAM_WD * w_ref[...])
        nm_ref[...] = nm
        nv_ref[...] = nv

    blk = pl.BlockSpec((tr, C), lambda i: (i, 0))
    return pl.pallas_call(
        body,
        grid=(R // tr,),
        in_specs=[blk] * 4,
        out_specs=[blk] * 3,
        out_shape=[jax.ShapeDtypeStruct((R, C), F32)] * 3,
        compiler_params=_cparams("parallel"),
        name=name,
    )(w, g, m, v)


def _sum_pieces(grad, axis, recv, pos, name):
    n, pr, pc = recv.shape
    tr = _row_block(pr, pc, bytes_per_row_elem=(n + 1) * recv.dtype.itemsize, budget=4 << 20)
    nblk = pr // tr
    if axis == 1:
        own_map = lambda i, p: (p[1] * nblk + i, p[0])
    else:
        own_map = lambda i, p: ((2 * p[0] + p[1]) * nblk + i, 0)

    def body(p_ref, own_ref, r_ref, o_ref):
        acc = own_ref[...].astype(F32)
        for s in range(n):
            acc = acc + r_ref[s].astype(F32)
        o_ref[...] = acc

    return pl.pallas_call(
        body,
        grid_spec=pltpu.PrefetchScalarGridSpec(
            num_scalar_prefetch=1,
            grid=(nblk,),
            in_specs=[pl.BlockSpec((tr, pc), own_map), pl.BlockSpec((n, tr, pc), lambda i, p: (0, i, 0))],
            out_specs=pl.BlockSpec((tr, pc), lambda i, p: (p[1] * nblk + i, 0)),
        ),
        out_shape=jax.ShapeDtypeStruct((2 * pr, pc), F32),
        compiler_params=_cparams("parallel"),
        name=name,
    )(pos, grad, recv)


def _sum_small(own, recv, me, name):
    n, R, C = recv.shape
    tr = _row_block(R, C, bytes_per_row_elem=(n + 1) * 4, budget=4 << 20)

    def body(me_ref, own_ref, r_ref, o_ref):
        acc = None
        for dev in range(n + 1):
            k = jnp.bitwise_xor(me_ref[0], dev)
            term = jnp.where(k == 0, own_ref[...], r_ref[jnp.maximum(k - 1, 0)])
            acc = term if acc is None else acc + term
        o_ref[...] = acc

    return pl.pallas_call(
        body,
        grid_spec=pltpu.PrefetchScalarGridSpec(
            num_scalar_prefetch=1,
            grid=(R // tr,),
            in_specs=[pl.BlockSpec((tr, C), lambda i, m: (i, 0)), pl.BlockSpec((n, tr, C), lambda i, m: (0, i, 0))],
            out_specs=pl.BlockSpec((tr, C), lambda i, m: (i, 0)),
        ),
        out_shape=jax.ShapeDtypeStruct((R, C), F32),
        compiler_params=_cparams("parallel"),
        name=name,
    )(me, own, recv)


def _place(shard, axis, pos, dtype, name):
    shp = list(shard.shape)
    shp[axis] *= N_CHIPS
    if shard.ndim == 3:
        assert axis == 1
        in_spec = pl.BlockSpec(shard.shape, lambda i, p: (0, 0, 0))
        out_spec = pl.BlockSpec(shard.shape, lambda i, p: (0, p[0], 0))
        grid = (1,)
    else:
        R, C = shard.shape
        tr = _row_block(R, C, bytes_per_row_elem=4, budget=2 << 20)
        nblk = R // tr
        in_spec = pl.BlockSpec((tr, C), lambda i, p: (i, 0))
        out_spec = pl.BlockSpec((tr, C), (lambda i, p: (i, p[0])) if axis == 1 else (lambda i, p: (p[0] * nblk + i, 0)))
        grid = (nblk,)

    def body(p_ref, s_ref, o_ref):
        o_ref[...] = s_ref[...].astype(o_ref.dtype)

    return pl.pallas_call(
        body,
        grid_spec=pltpu.PrefetchScalarGridSpec(num_scalar_prefetch=1, grid=grid, in_specs=[in_spec], out_specs=out_spec),
        out_shape=jax.ShapeDtypeStruct(tuple(shp), dtype),
        compiler_params=_cparams("parallel"),
        name=name,
    )(pos, shard)


HBM = pl.BlockSpec(memory_space=pltpu.HBM)
SEM = pl.BlockSpec(memory_space=pltpu.SEMAPHORE)
DATAFLOW = pltpu.SideEffectType.DATAFLOW_SIDE_EFFECTING


def _position():
    return lax.axis_index("x"), lax.axis_index("y"), lax.axis_index("c")


def _peer(k):
    x, y, c = _position()
    return ((1 - x) if k & 4 else x, (1 - y) if k & 2 else y, (1 - c) if k & 1 else c)


def _shard_slice(ref, axis, idx, size):
    start = idx * size
    if axis == ref.ndim - 1:
        start = pl.multiple_of(start, 128)
    ix = [slice(None)] * ref.ndim
    ix[axis] = pl.ds(start, size)
    return ref.at[tuple(ix)]


def _gather_plan(axes):
    def plan(refs):
        x, y, c = _position()
        out = []
        for ref, ax in zip(refs, axes):
            mine = _shard_slice(ref, ax, 2 * x + y, ref.shape[ax] // N_CHIPS)
            for k in (4, 2, 6):
                px, py, _ = _peer(k)
                out.append((mine, mine, (px, py, c)))
        return out
    return plan


def _scatter_plan(axes):
    m = len(axes)

    def plan(refs):
        out = []
        for t in range(m):
            grad, recv = refs[t], refs[m + t]
            _, pr, pc = recv.shape
            for k in range(1, N_DEV):
                px, py, pcore = _peer(k)
                if axes[t] == 0:
                    piece = grad.at[pl.ds(((2 * px + py) * 2 + pcore) * pr, pr), :]
                else:
                    piece = grad.at[pl.ds(pcore * pr, pr), pl.ds(pl.multiple_of((2 * px + py) * pc, 128), pc)]
                out.append((piece, recv.at[k - 1], (px, py, pcore)))
        return out
    return plan


def _broadcast_plan(refs):
    small, recv = refs
    return [(small, recv.at[k - 1], _peer(k)) for k in range(1, N_DEV)]


def _start_all(plan, refs, send_sems, recv_sems):
    for q, (src, dst, dev) in enumerate(plan(refs)):
        pltpu.make_async_remote_copy(src_ref=src, dst_ref=dst, send_sem=send_sems.at[q], recv_sem=recv_sems.at[q], device_id=dev, device_id_type=MESH).start()


def _wait_all(plan, refs, send_sems, recv_sems):
    for q, (src, dst, dev) in enumerate(plan(refs)):
        cp = pltpu.make_async_remote_copy(src_ref=src, dst_ref=dst, send_sem=send_sems.at[q], recv_sem=recv_sems.at[q], device_id=dev, device_id_type=MESH)
        cp.wait_send()
        cp.wait_recv()


def _push(bufs, plan, ncopies, name):
    n = len(bufs)

    def body(*refs):
        outs = refs[n:2 * n]
        send_sems, recv_sems = refs[2 * n:]
        _start_all(plan, outs, send_sems, recv_sems)
        _wait_all(plan, outs, send_sems, recv_sems)

    return pl.pallas_call(
        body,
        in_specs=[ANY] * n,
        out_specs=[ANY] * n,
        out_shape=[jax.ShapeDtypeStruct(b.shape, b.dtype) for b in bufs],
        input_output_aliases={t: t for t in range(n)},
        scratch_shapes=[pltpu.SemaphoreType.DMA((ncopies,)), pltpu.SemaphoreType.DMA((ncopies,))],
        name=name,
    )(*bufs)


def _push_start(bufs, plan, ncopies, name, after=None):
    n = len(bufs)
    extra = [] if after is None else [after]

    def body(*refs):
        ins = refs[:n]
        first_out = n + len(extra)
        send_sems, recv_sems, token = refs[first_out], refs[first_out + 1], refs[-1]
        _start_all(plan, ins, send_sems, recv_sems)
        token[...] = jnp.zeros_like(token)

    res = pl.pallas_call(
        body,
        name=name,
        out_shape=(pltpu.SemaphoreType.DMA((ncopies,)), pltpu.SemaphoreType.DMA((ncopies,)), *[pltpu.HBM(b.shape, b.dtype) for b in bufs],
                   jax.ShapeDtypeStruct((8, 128), F32)),
        in_specs=[HBM] * n + [ANY] * len(extra),
        out_specs=(SEM, SEM, *[HBM] * n, pl.BlockSpec(memory_space=pltpu.VMEM)),
        input_output_aliases={t: t + 2 for t in range(n)},
        compiler_params=pltpu.CompilerParams(has_side_effects=DATAFLOW),
    )(*[pltpu.with_memory_space_constraint(b, pltpu.HBM) for b in bufs], *extra)
    return res[0], res[1], list(res[2:2 + n]), res[-1]


def _push_wait(send_sems, recv_sems, bufs, plan, after, name):
    n = len(bufs)

    def body(*refs):
        ins = refs[:n]
        _wait_all(plan, ins, refs[n], refs[n + 1])

    return pl.pallas_call(
        body,
        name=name,
        out_shape=tuple(pltpu.HBM(b.shape, b.dtype) for b in bufs),
        in_specs=[HBM] * n + [SEM, SEM, ANY],
        out_specs=tuple([HBM] * n),
        input_output_aliases={t: t for t in range(n)},
        compiler_params=pltpu.CompilerParams(has_side_effects=DATAFLOW),
    )(*bufs, send_sems, recv_sems, after)


EXCHANGE_CHUNKS = 2


def _exchange_plan(refs):
    x, y, c = _position()
    out = []
    for ref in refs:
        rows = ref.shape[0] // (2 * EXCHANGE_CHUNKS)
        for q in range(EXCHANGE_CHUNKS):
            mine = ref.at[pl.ds((c * EXCHANGE_CHUNKS + q) * rows, rows), :]
            out.append((mine, mine, (x, y, 1 - c)))
    return out


LATE_WEIGHTS = ("w_pool_lin", "w_pool_out", "w_attn_out", "w_out", "w_up", "conv_w", "w_down")


def _local_step(x, tgt, w, late_weights, send):
    S, D = x.shape
    PW = w["pool_scale"].shape[1]
    o_q = PW
    o_g = PW + 3 * ATTN_WIDTH
    QKV = 3 * ATTN_WIDTH

    h1 = _rms_fwd(x, w["g_mix"], "rms1")
    proj_tiles = (_tile(S, 1024), 512, D)
    u = _mm(h1, w["w_in"], mode="nn", dims=(S, PW, D), tiles=proj_tiles, out_dtypes=(F32,), name="proj_u")
    qkv = _mm(h1, w["w_in"], mode="nn", dims=(S, QKV, D), tiles=proj_tiles, b_off=(0, o_q), name="proj_qkv")

    def gate_epilogue(acc, ex, outs):
        outs[0][...] = 1.0 / (1.0 + jnp.exp(-(acc + ex[0][...])))

    gates = _mm(h1, w["w_in"], mode="nn", dims=(S, 2 * D, D), tiles=proj_tiles, b_off=(0, o_g), out_dtypes=(F32,), epilogue=gate_epilogue,
                extras=[(w["b_gate"], "n", (0, 0))], name="proj_gates")

    os_, lses = [], []
    for gi, (_, d) in enumerate(ATTN_GROUPS):
        o, lse = _attn_fwd(qkv, d, gi, f"attn_fwd{gi}")
        os_.append(o)
        lses.append(lse)
    attn, lse_tot = _attn_merge(os_, lses, "attn_merge")

    w = dict(w, **late_weights(attn))
    F = w["w_down"].shape[0]
    pool_out = _pool_fwd(u, w["w_pool_lin"], w["pool_scale"], "pool_fwd")
    y_pool = _mm(pool_out, w["w_pool_out"], mode="nn", dims=(S, D, PW), name="y_pool")

    def mix_epilogue(acc, ex, outs):
        outs[0][...] = acc.astype(BF16)
        outs[1][...] = (ex[0][...] * ex[2][...].astype(F32) + ex[1][...] * acc).astype(BF16)

    y_attn, mixed = _mm(attn, w["w_attn_out"], mode="nn", dims=(S, D, GROUP_WIDTH), out_dtypes=(BF16, BF16), epilogue=mix_epilogue,
                        extras=[(gates, "mn", (0, 0)), (gates, "mn", (0, D)), (y_pool, "mn", (0, 0))], name="y_attn_mix")

    def residual_epilogue(acc, ex, outs):
        outs[0][...] = ex[0][...] + acc

    x2 = _mm(mixed, w["w_out"], mode="nn", dims=(S, D, D), out_dtypes=(F32,), epilogue=residual_epilogue, extras=[(x, "mn", (0, 0))], name="out_proj")

    h2 = _rms_fwd(x2, w["g_ffn"], "rms2")
    up = _mm(h2, w["w_up"], mode="nn", dims=(S, 2 * F, D), name="up_proj")
    f = _convglu_fwd(up, w["conv_w"], w["conv_b"], "convglu_fwd")
    x3 = _mm(f, w["w_down"], mode="nn", dims=(S, D, F), out_dtypes=(F32,), epilogue=residual_epilogue, extras=[(x2, "mn", (0, 0))], name="down_proj")

    g = {}
    dx3, dx3b, g["g_final"], loss_cols = _loss_head(x3, tgt, w["g_final"], "loss_head")

    g["w_down"] = _mm(f, dx3b, mode="tn", dims=(F, D, S), name="dw_down")
    sent = send(("w_down",), g)
    df = _mm(dx3b, w["w_down"], mode="nt", dims=(S, F, D), name="d_f")
    da, db, dcb_a, dcb_b, dcw_a, dcw_b = _convglu_bwd(df, up, w["conv_w"], w["conv_b"] + sent, "convglu_bwd")
    g["conv_b"] = jnp.concatenate([dcb_a, dcb_b], axis=1)
    g["conv_w"] = jnp.concatenate([dcw_a, dcw_b], axis=1)
    dup = _conv_transpose(da, db, w["conv_w"], "conv_transpose")
    g["w_up"] = _mm(h2, dup, mode="tn", dims=(D, 2 * F, S), name="dw_up")
    sent = send(("w_up",), g)
    dh2 = _mm(dup, w["w_up"], mode="nt", dims=(S, D, 2 * F), name="d_h2")
    dx2, dx2b, g["g_ffn"] = _rms_bwd(dh2, x2, w["g_ffn"] + sent, dx3, "rms2_bwd", True)

    g["w_out"] = _mm(mixed, dx2b, mode="tn", dims=(D, D, S), name="dw_out")
    dmixed = _mm(dx2b, w["w_out"], mode="nt", dims=(S, D, D), name="d_mixed")
    IN = w["w_in"].shape[1]
    dy_both, dproj, g["b_gate"] = _gate_bwd(dmixed, gates, y_pool, y_attn, IN, "gate_bwd")

    g["w_pool_out"] = _mm(pool_out, dy_both, mode="tn", dims=(PW, D, S), name="dw_pool_out")
    g["w_attn_out"] = _mm(attn, dy_both, mode="tn", dims=(GROUP_WIDTH, D, S), b_off=(0, D), name="dw_attn_out")
    sent = send(("w_out", "w_pool_out", "w_attn_out"), g)
    dpool = _mm(dy_both, w["w_pool_out"], mode="nt", dims=(S, PW, D), name="d_pool")
    dattn = _mm(dy_both, w["w_attn_out"], mode="nt", dims=(S, GROUP_WIDTH, D), a_off=(0, D), name="d_attn")

    dproj, g["w_pool_lin"], g["pool_scale"] = _pool_bwd(u, dpool, w["w_pool_lin"], w["pool_scale"] + sent, dproj, "pool_bwd")
    g["loss_cols"] = loss_cols
    sent = send("small", g)

    for gi, (_, d) in enumerate(ATTN_GROUPS):
        dproj = _attn_bwd(qkv, dattn, attn, lse_tot, dproj, d, gi, PW // HEAD_DIM, f"attn_bwd{gi}")

    g["w_in"] = _mm(h1, dproj, mode="tn", dims=(D, IN, S), name="dw_in")
    sent = sent + send(("w_in",), g)
    dh1 = _mm(dproj, w["w_in"], mode="nt", dims=(S, D, IN), tiles=(_tile(S, 1024), _tile(D, 2048), _tile(IN, 512)), name="d_h1")
    (grad_x, g["g_mix"]) = _rms_bwd(dh1, x, w["g_mix"] + sent, dx2, "rms1_bwd", False)
    return loss_cols, grad_x, g


BIG = ("w_in", "w_pool_out", "w_attn_out", "w_out", "w_up", "w_down")
BIG_AXIS = {"w_in": 1, "w_pool_out": 1, "w_attn_out": 1, "w_out": 0, "w_up": 1, "w_down": 0}
GATHER_AXIS = dict(BIG_AXIS, w_pool_lin=1, conv_w=1)
SMALL = ("loss_cols", "b_gate", "w_pool_lin", "pool_scale", "g_ffn", "conv_w", "conv_b", "g_final")
SMALL_COLS = 1024
ORDER = ("g_mix", "w_in", "b_gate", "w_pool_lin", "pool_scale", "w_pool_out", "w_attn_out", "w_out", "g_ffn", "w_up", "conv_w", "conv_b", "w_down", "g_final")


def _as_rows(parts):
    flat = jnp.concatenate([p.astype(F32).reshape(-1) for p in parts])
    rows = -(-flat.shape[0] // (8 * SMALL_COLS)) * 8
    return jnp.pad(flat, (0, rows * SMALL_COLS - flat.shape[0])).reshape(rows, SMALL_COLS)


def kernel(x, g_mix, w_in, b_gate, w_pool_lin, pool_scale, w_pool_out, w_attn_out, w_out, g_ffn, w_up, conv_w, conv_b, w_down, g_final, loss_target, m_g_mix, m_w_in, m_b_gate, m_w_pool_lin, m_pool_scale, m_w_pool_out, m_w_attn_out, m_w_out, m_g_ffn, m_w_up, m_conv_w, m_conv_b, m_w_down, m_g_final, v_g_mix, v_w_in, v_b_gate, v_w_pool_lin, v_pool_scale, v_w_pool_out, v_w_attn_out, v_w_out, v_g_ffn, v_w_up, v_conv_w, v_conv_b, v_w_down, v_g_final):
    shard = dict(g_mix=g_mix, w_in=w_in, b_gate=b_gate, w_pool_lin=w_pool_lin, pool_scale=pool_scale, w_pool_out=w_pool_out, w_attn_out=w_attn_out,
                 w_out=w_out, g_ffn=g_ffn, w_up=w_up, conv_w=conv_w, conv_b=conv_b, w_down=w_down, g_final=g_final)
    mom = dict(g_mix=m_g_mix, w_in=m_w_in, b_gate=m_b_gate, w_pool_lin=m_w_pool_lin, pool_scale=m_pool_scale, w_pool_out=m_w_pool_out, w_attn_out=m_w_attn_out,
               w_out=m_w_out, g_ffn=m_g_ffn, w_up=m_w_up, conv_w=m_conv_w, conv_b=m_conv_b, w_down=m_w_down, g_final=m_g_final)
    vel = dict(g_mix=v_g_mix, w_in=v_w_in, b_gate=v_b_gate, w_pool_lin=v_w_pool_lin, pool_scale=v_pool_scale, w_pool_out=v_w_pool_out, w_attn_out=v_w_attn_out,
               w_out=v_w_out, g_ffn=v_g_ffn, w_up=v_w_up, conv_w=v_conv_w, conv_b=v_conv_b, w_down=v_w_down, g_final=v_g_final)
    chip = 2 * lax.axis_index("x") + lax.axis_index("y")
    pos = jnp.stack([chip, lax.axis_index("c")]).astype(jnp.int32)
    me = (2 * chip + lax.axis_index("c")).astype(jnp.int32).reshape(1)
    D = x.shape[2]

    placed = {k: _place(shard[k][0], GATHER_AXIS[k], pos, F32 if k == "conv_w" else BF16, f"place_{k}") for k in GATHER_AXIS}
    w_in_full, = _push([placed["w_in"]], _gather_plan([GATHER_AXIS["w_in"]]), 3, "comm_gather_w_in")
    late_plan = _gather_plan([GATHER_AXIS[k] for k in LATE_WEIGHTS])
    late_send, late_recv, late_bufs, late_token = _push_start([placed[k] for k in LATE_WEIGHTS], late_plan, 3 * len(LATE_WEIGHTS), "comm_gather_late_start",
                                                              after=w_in_full)

    def late_weights(after):
        return dict(zip(LATE_WEIGHTS, _push_wait(late_send, late_recv, late_bufs, late_plan, after, "comm_gather_late_wait")))

    pending = []

    def send(names, g):
        if names == "small":
            bufs = [_as_rows([g[k] for k in SMALL])]
            bufs.append(lax.empty((N_DEV - 1,) + bufs[0].shape, F32))
            plan, tag = _broadcast_plan, "small"
        else:
            bufs = [g[k] for k in names]
            for k in names:
                R, C = g[k].shape
                piece = (R // (2 * N_CHIPS), C) if BIG_AXIS[k] == 0 else (R // 2, C // N_CHIPS)
                bufs.append(lax.empty((N_DEV - 1,) + piece, BF16))
            plan, tag = _scatter_plan([BIG_AXIS[k] for k in names]), names[0]
        ncopies = (N_DEV - 1) * (len(bufs) // 2)
        send_sems, recv_sems, thru, token = _push_start(bufs, plan, ncopies, f"comm_scatter_start_{tag}")
        pending.append((names, send_sems, recv_sems, thru, plan, tag))
        return token[0, 0]

    w0 = dict(g_mix=shard["g_mix"] + late_token[0, 0], w_in=w_in_full, b_gate=shard["b_gate"], pool_scale=shard["pool_scale"], g_ffn=shard["g_ffn"],
              conv_b=shard["conv_b"], g_final=shard["g_final"].reshape(1, D))
    _, grad_x, gr = _local_step(x[0], loss_target[0], w0, late_weights, send)

    halves, small_parts = {}, None
    for names, send_sems, recv_sems, thru, plan, tag in pending:
        done = _push_wait(send_sems, recv_sems, thru, plan, grad_x, f"comm_scatter_wait_{tag}")
        if names == "small":
            small_parts = _sum_small(done[0], done[1], me, "sum_small").reshape(-1)
        else:
            m = len(names)
            for t, k in enumerate(names):
                halves[k] = _sum_pieces(done[t], BIG_AXIS[k], done[m + t], pos, f"sum_{k}")
    g_mix_own = _as_rows([gr["g_mix"]])
    _, g_mix_recv = _push([g_mix_own, lax.empty((N_DEV - 1,) + g_mix_own.shape, F32)], _broadcast_plan, N_DEV - 1, "comm_gather_g_mix")
    g_mix_sum = _sum_small(g_mix_own, g_mix_recv, me, "sum_g_mix").reshape(-1)[:D]
    wholes = _push([halves[k] for k in BIG], _exchange_plan, EXCHANGE_CHUNKS * len(BIG), "comm_exchange_halves")

    grads = {"g_mix": g_mix_sum.reshape(shard["g_mix"].shape)}
    for k, whole in zip(BIG, wholes):
        grads[k] = whole.reshape(shard[k].shape)
    off = 0
    loss = None
    for k in SMALL:
        sz = math.prod(gr[k].shape)
        fullg = small_parts[off:off + sz].reshape(gr[k].shape)
        off += sz
        if k == "loss_cols":
            loss = jnp.sum(fullg)
            continue
        if k in ("w_pool_lin", "conv_w"):
            n = shard[k].shape[2]
            fullg = lax.dynamic_slice_in_dim(fullg, chip * n, n, axis=1)
        grads[k] = fullg.reshape(shard[k].shape)

    deltas, new_m, new_v = {}, {}, {}
    for k in ORDER:
        shp = shard[k].shape
        two_d = (-1, shp[-1])
        dl, nm, nv = _adamw(shard[k].reshape(two_d), grads[k].reshape(two_d), mom[k].reshape(two_d), vel[k].reshape(two_d), f"adamw_{k}")
        deltas[k], new_m[k], new_v[k] = dl.reshape(shp), nm.reshape(shp), nv.reshape(shp)

    return (loss, grad_x[None], *[grads[k] for k in ORDER], *[deltas[k] for k in ORDER], *[new_m[k] for k in ORDER], *[new_v[k] for k in ORDER])
```

```python
import functools
import math

import jax
import jax.numpy as jnp
from jax import lax
from jax.experimental import pallas as pl
from jax.experimental.pallas import tpu as pltpu

F32 = jnp.float32
BF16 = jnp.bfloat16

RMS_EPS = 1e-6
POOL_WINDOWS = (2, 4, 8, 16)
ATTN_GROUPS = ((128, 1), (512, 4), (2048, 16))
HEADS_PER_GROUP = 4
HEAD_DIM = 128
N_ATTN_HEADS = HEADS_PER_GROUP * len(ATTN_GROUPS)
SPAN = 128
GROUP_WIDTH = HEADS_PER_GROUP * HEAD_DIM
ATTN_WIDTH = N_ATTN_HEADS * HEAD_DIM
ATTN_SCALE = HEAD_DIM ** -0.5
NEG_BIG = -1e30
ALIBI_SLOPES = tuple(2.0 ** (-8.0 * (h + 1) / N_ATTN_HEADS) for h in range(N_ATTN_HEADS))

ADAM_LR = 0.001
ADAM_B1 = 0.9
ADAM_B2 = 0.999
ADAM_EPS = 1e-08
ADAM_WD = 0.01
ADAM_STEP = 10

INV_SQRT2 = 1.0 / math.sqrt(2.0)
INV_SQRT_2PI = 1.0 / math.sqrt(2.0 * math.pi)

HALO = 16
VMEM_LIMIT = 56 * 1024 * 1024
N_CHIPS = 4
N_DEV = 8
MESH = pl.DeviceIdType.MESH
ANY = pl.BlockSpec(memory_space=pl.ANY)


def _cparams(*sem):
    return pltpu.CompilerParams(dimension_semantics=sem, vmem_limit_bytes=VMEM_LIMIT)


def _tile(n, pref, mult=128):
    t = (min(pref, n) // mult) * mult
    while t >= mult:
        if n % t == 0:
            return t
        t -= mult
    return n


def _dot(a, b, contract):
    return lax.dot_general(a, b, (contract, ((), ())), preferred_element_type=F32)


def _dot_nn(a, b):
    return _dot(a, b, ((1,), (0,)))


def _dot_nt(a, b):
    return _dot(a, b, ((1,), (1,)))


def _mm(a, b, *, mode, dims, name, tiles=None, out_dtypes=(BF16,), epilogue=None, extras=(), a_off=(0, 0), b_off=(0, 0)):
    M, N, K = dims
    if tiles is None:
        tiles = (_tile(M, 1408), _tile(N, 2816), _tile(K, 512)) if mode == "tn" else (_tile(M, 1024), _tile(N, 1536), _tile(K, 1408))
    tm, tn, tk = tiles
    assert M % tm == 0 and N % tn == 0 and K % tk == 0, (name, dims, tiles)
    nk = K // tk
    if mode == "nn":
        ab, bb, contract = (tm, tk), (tk, tn), ((1,), (0,))
        amap = lambda i, j, k: (i + a_off[0] // tm, k + a_off[1] // tk)
        bmap = lambda i, j, k: (k + b_off[0] // tk, j + b_off[1] // tn)
    elif mode == "nt":
        ab, bb, contract = (tm, tk), (tn, tk), ((1,), (1,))
        amap = lambda i, j, k: (i + a_off[0] // tm, k + a_off[1] // tk)
        bmap = lambda i, j, k: (j + b_off[0] // tn, k + b_off[1] // tk)
    else:
        ab, bb, contract = (tk, tm), (tk, tn), ((0,), (0,))
        amap = lambda i, j, k: (k + a_off[0] // tk, i + a_off[1] // tm)
        bmap = lambda i, j, k: (k + b_off[0] // tk, j + b_off[1] // tn)
    assert a_off[0] % ab[0] == 0 and a_off[1] % ab[1] == 0 and b_off[0] % bb[0] == 0 and b_off[1] % bb[1] == 0, name
    in_specs = [pl.BlockSpec(ab, amap), pl.BlockSpec(bb, bmap)]
    ex_arrays = []
    for arr, kind, off in extras:
        if kind == "mn":
            assert off[0] % tm == 0 and off[1] % tn == 0, name
            in_specs.append(pl.BlockSpec((tm, tn), lambda i, j, k, off=off: (i + off[0] // tm, j + off[1] // tn)))
        else:
            assert off[1] % tn == 0, name
            in_specs.append(pl.BlockSpec((1, tn), lambda i, j, k, off=off: (0, j + off[1] // tn)))
        ex_arrays.append(arr)
    ne, no = len(ex_arrays), len(out_dtypes)
    if epilogue is None:
        def epilogue(acc, ex, outs):
            outs[0][...] = acc.astype(outs[0].dtype)

    def body(*refs):
        a_ref, b_ref = refs[0], refs[1]
        ex, outs, acc = refs[2:2 + ne], refs[2 + ne:2 + ne + no], refs[-1]
        k = pl.program_id(2)

        @pl.when(k == 0)
        def _():
            acc[...] = jnp.zeros_like(acc)

        acc[...] += _dot(a_ref[...], b_ref[...], contract)

        @pl.when(k == nk - 1)
        def _():
            epilogue(acc[...], ex, outs)

    res = pl.pallas_call(
        body,
        grid=(M // tm, N // tn, nk),
        in_specs=in_specs,
        out_specs=[pl.BlockSpec((tm, tn), lambda i, j, k: (i, j)) for _ in out_dtypes],
        out_shape=[jax.ShapeDtypeStruct((M, N), dt) for dt in out_dtypes],
        scratch_shapes=[pltpu.VMEM((tm, tn), F32)],
        compiler_params=_cparams("parallel", "parallel", "arbitrary"),
        name=name,
    )(a, b, *ex_arrays)
    return res[0] if no == 1 else res


def _rms_fwd(x, g, name):
    S, D = x.shape
    tm = _tile(S, 256)

    def body(x_ref, g_ref, h_ref):
        xv = x_ref[...]
        r = lax.rsqrt(jnp.mean(xv * xv, axis=-1, keepdims=True) + RMS_EPS)
        h_ref[...] = (xv * r * g_ref[...]).astype(h_ref.dtype)

    return pl.pallas_call(
        body,
        grid=(S // tm,),
        in_specs=[pl.BlockSpec((tm, D), lambda i: (i, 0)), pl.BlockSpec((1, D), lambda i: (0, 0))],
        out_specs=pl.BlockSpec((tm, D), lambda i: (i, 0)),
        out_shape=jax.ShapeDtypeStruct((S, D), BF16),
        compiler_params=_cparams("parallel"),
        name=name,
    )(x, g)


def _rms_bwd(dh, x, g, dres, name, with_bf16):
    S, D = x.shape
    tm = _tile(S, 256)

    def body(dh_ref, x_ref, g_ref, dres_ref, *outs):
        dx_ref, dg_ref = outs[0], outs[-1]
        xv = x_ref[...]
        r = lax.rsqrt(jnp.mean(xv * xv, axis=-1, keepdims=True) + RMS_EPS)
        xr = xv * r
        dhv = dh_ref[...].astype(F32)

        @pl.when(pl.program_id(0) == 0)
        def _():
            dg_ref[...] = jnp.zeros_like(dg_ref)

        dg_ref[...] += jnp.sum(dhv * xr, axis=0, keepdims=True)
        u = dhv * g_ref[...]
        c = jnp.mean(u * xr, axis=-1, keepdims=True)
        dx = dres_ref[...] + r * (u - xr * c)
        dx_ref[...] = dx
        if with_bf16:
            outs[1][...] = dx.astype(BF16)

    row = pl.BlockSpec((tm, D), lambda i: (i, 0))
    vec = pl.BlockSpec((1, D), lambda i: (0, 0))
    out_specs = [row] + ([row] if with_bf16 else []) + [vec]
    out_shape = [jax.ShapeDtypeStruct((S, D), F32)] + ([jax.ShapeDtypeStruct((S, D), BF16)] if with_bf16 else []) + [jax.ShapeDtypeStruct((1, D), F32)]
    return pl.pallas_call(
        body,
        grid=(S // tm,),
        in_specs=[row, row, vec, row],
        out_specs=out_specs,
        out_shape=out_shape,
        compiler_params=_cparams("arbitrary"),
        name=name,
    )(dh, x, g, dres)


def _loss_head(x3, tgt, g, name):
    S, D = x3.shape
    tm = _tile(S, 256)

    def body(x_ref, t_ref, g_ref, dx_ref, dxb_ref, dg_ref, loss_ref):
        xv = x_ref[...]
        gv = g_ref[...]
        r = lax.rsqrt(jnp.mean(xv * xv, axis=-1, keepdims=True) + RMS_EPS)
        xr = xv * r
        e = xr * gv - t_ref[...]

        @pl.when(pl.program_id(0) == 0)
        def _():
            dg_ref[...] = jnp.zeros_like(dg_ref)
            loss_ref[...] = jnp.zeros_like(loss_ref)

        loss_ref[...] += jnp.sum(e * e, axis=0, keepdims=True) * (0.5 / D)
        dy = e * (1.0 / D)
        dg_ref[...] += jnp.sum(dy * xr, axis=0, keepdims=True)
        u = dy * gv
        c = jnp.mean(u * xr, axis=-1, keepdims=True)
        dx = r * (u - xr * c)
        dx_ref[...] = dx
        dxb_ref[...] = dx.astype(BF16)

    row = pl.BlockSpec((tm, D), lambda i: (i, 0))
    vec = pl.BlockSpec((1, D), lambda i: (0, 0))
    return pl.pallas_call(
        body,
        grid=(S // tm,),
        in_specs=[row, row, vec],
        out_specs=[row, row, vec, vec],
        out_shape=[jax.ShapeDtypeStruct((S, D), F32), jax.ShapeDtypeStruct((S, D), BF16), jax.ShapeDtypeStruct((1, D), F32), jax.ShapeDtypeStruct((1, D), F32)],
        compiler_params=_cparams("arbitrary"),
        name=name,
    )(x3, tgt, g)


def _conv_taps(cur_ref, halo_ref, w_ref, b_ref, first):
    cur = cur_ref[...].astype(F32)
    halo = jnp.where(first, 0.0, halo_ref[...].astype(F32))
    xx = jnp.concatenate([halo, cur], axis=0)
    p1 = pltpu.roll(xx, 1, 0)[HALO:]
    p2 = pltpu.roll(xx, 2, 0)[HALO:]
    w = w_ref[...]
    y = b_ref[...] + w[0:1] * p2 + w[1:2] * p1 + w[2:3] * cur
    return y, (cur, p1, p2)


def _convglu_specs(S, F, tm, tn, rows_axis):
    nj = F // tn
    if rows_axis == 0:
        ij = lambda f: (lambda i, j: f(i, j))
    else:
        ij = lambda f: (lambda j, i: f(i, j))
    hb = tm // HALO
    return [
        pl.BlockSpec((tm, tn), ij(lambda i, j: (i, j))),
        pl.BlockSpec((tm, tn), ij(lambda i, j: (i, j + nj))),
        pl.BlockSpec((HALO, tn), ij(lambda i, j: (jnp.maximum(i * hb - 1, 0), j))),
        pl.BlockSpec((HALO, tn), ij(lambda i, j: (jnp.maximum(i * hb - 1, 0), j + nj))),
        pl.BlockSpec((3, tn), ij(lambda i, j: (0, j))),
        pl.BlockSpec((3, tn), ij(lambda i, j: (0, j + nj))),
        pl.BlockSpec((1, tn), ij(lambda i, j: (0, j))),
        pl.BlockSpec((1, tn), ij(lambda i, j: (0, j + nj))),
    ]


def _convglu_fwd(up, cw, cb, name):
    S, F2 = up.shape
    F = F2 // 2
    tm, tn = _tile(S, 512), _tile(F, 512)

    def body(ua, ub, ha, hb, wa, wb, ba, bb, f_ref):
        first = pl.program_id(0) == 0
        a, _ = _conv_taps(ua, ha, wa, ba, first)
        b, _ = _conv_taps(ub, hb, wb, bb, first)
        f_ref[...] = (0.5 * a * (1.0 + lax.erf(a * INV_SQRT2)) * b).astype(f_ref.dtype)

    return pl.pallas_call(
        body,
        grid=(S // tm, F // tn),
        in_specs=_convglu_specs(S, F, tm, tn, 0),
        out_specs=pl.BlockSpec((tm, tn), lambda i, j: (i, j)),
        out_shape=jax.ShapeDtypeStruct((S, F), BF16),
        compiler_params=_cparams("parallel", "parallel"),
        name=name,
    )(up, up, up, up, cw, cw, cb, cb)


def _convglu_bwd(df, up, cw, cb, name):
    S, F2 = up.shape
    F = F2 // 2
    tm, tn = _tile(S, 512), _tile(F, 512)

    def body(df_ref, ua, ub, ha, hb, wa, wb, ba, bb, da_ref, db_ref, dba_ref, dbb_ref, dwa_ref, dwb_ref):
        first = pl.program_id(1) == 0
        a, pa = _conv_taps(ua, ha, wa, ba, first)
        b, pb = _conv_taps(ub, hb, wb, bb, first)
        dfv = df_ref[...].astype(F32)
        cdf = 0.5 * (1.0 + lax.erf(a * INV_SQRT2))
        pdf = jnp.exp(-0.5 * a * a) * INV_SQRT_2PI
        da = dfv * b * (cdf + a * pdf)
        db = dfv * (a * cdf)
        da_ref[...] = da.astype(BF16)
        db_ref[...] = db.astype(BF16)

        @pl.when(first)
        def _():
            for r in (dba_ref, dbb_ref, dwa_ref, dwb_ref):
                r[...] = jnp.zeros_like(r)

        for d, taps, dbias, dw in ((da, pa, dba_ref, dwa_ref), (db, pb, dbb_ref, dwb_ref)):
            dbias[...] += jnp.sum(d, axis=0, keepdims=True)
            dw[0:1, :] += jnp.sum(d * taps[2], axis=0, keepdims=True)
            dw[1:2, :] += jnp.sum(d * taps[1], axis=0, keepdims=True)
            dw[2:3, :] += jnp.sum(d * taps[0], axis=0, keepdims=True)

    tile = pl.BlockSpec((tm, tn), lambda j, i: (i, j))
    b1 = pl.BlockSpec((1, tn), lambda j, i: (0, j))
    b3 = pl.BlockSpec((3, tn), lambda j, i: (0, j))
    return pl.pallas_call(
        body,
        grid=(F // tn, S // tm),
        in_specs=[tile] + _convglu_specs(S, F, tm, tn, 1),
        out_specs=[tile, tile, b1, b1, b3, b3],
        out_shape=[jax.ShapeDtypeStruct((S, F), BF16)] * 2 + [jax.ShapeDtypeStruct((1, F), F32)] * 2 + [jax.ShapeDtypeStruct((3, F), F32)] * 2,
        compiler_params=_cparams("parallel", "arbitrary"),
        name=name,
    )(df, up, up, up, up, cw, cw, cb, cb)


def _conv_transpose(da, db, cw, name):
    S, F = da.shape
    tm, tn = _tile(S, 512), _tile(F, 512)
    nj, ni, hb = F // tn, S // tm, tm // HALO
    n = tm + HALO

    def body(a_ref, b_ref, an_ref, bn_ref, w_ref, o_ref):
        j, i = pl.program_id(0), pl.program_id(1)

        def run(c_ref, h_ref):
            cur = c_ref[...].astype(F32)
            halo = jnp.where(i == ni - 1, 0.0, h_ref[...].astype(F32))
            xx = jnp.concatenate([cur, halo], axis=0)
            n1 = pltpu.roll(xx, n - 1, 0)[:tm]
            n2 = pltpu.roll(xx, n - 2, 0)[:tm]
            w = w_ref[...]
            o_ref[...] = (w[2:3] * cur + w[1:2] * n1 + w[0:1] * n2).astype(o_ref.dtype)

        @pl.when(j < nj)
        def _():
            run(a_ref, an_ref)

        @pl.when(j >= nj)
        def _():
            run(b_ref, bn_ref)

    ja = lambda j: jnp.minimum(j, nj - 1)
    jb = lambda j: jnp.maximum(j - nj, 0)
    nxt = lambda i: jnp.minimum((i + 1) * hb, S // HALO - 1)
    return pl.pallas_call(
        body,
        grid=(2 * nj, ni),
        in_specs=[
            pl.BlockSpec((tm, tn), lambda j, i: (i, ja(j))),
            pl.BlockSpec((tm, tn), lambda j, i: (i, jb(j))),
            pl.BlockSpec((HALO, tn), lambda j, i: (nxt(i), ja(j))),
            pl.BlockSpec((HALO, tn), lambda j, i: (nxt(i), jb(j))),
            pl.BlockSpec((3, tn), lambda j, i: (0, j)),
        ],
        out_specs=pl.BlockSpec((tm, tn), lambda j, i: (i, j)),
        out_shape=jax.ShapeDtypeStruct((S, 2 * F), BF16),
        compiler_params=_cparams("parallel", "parallel"),
        name=name,
    )(da, db, da, db, cw)


def _gate_bwd(dmixed, gates, y_pool, y_attn, in_width, name):
    S, D = dmixed.shape
    tm, tn = _tile(S, 512), _tile(D, 512)
    nj = D // tn
    pre0 = (in_width - 2 * D) // tn
    assert pre0 * tn == in_width - 2 * D

    def body(dm_ref, g_ref, yp_ref, ya_ref, dy_ref, dpre_ref, db_ref):
        j = pl.program_id(0)

        @pl.when(pl.program_id(1) == 0)
        def _():
            db_ref[...] = jnp.zeros_like(db_ref)

        def run(y_ref):
            dm = dm_ref[...].astype(F32)
            gv = g_ref[...]
            dy_ref[...] = (dm * gv).astype(BF16)
            dpre = dm * y_ref[...].astype(F32) * gv * (1.0 - gv)
            dpre_ref[...] = dpre.astype(BF16)
            db_ref[...] += jnp.sum(dpre, axis=0, keepdims=True)

        @pl.when(j < nj)
        def _():
            run(yp_ref)

        @pl.when(j >= nj)
        def _():
            run(ya_ref)

    tile2 = pl.BlockSpec((tm, tn), lambda j, i: (i, j))
    return pl.pallas_call(
        body,
        grid=(2 * nj, S // tm),
        in_specs=[
            pl.BlockSpec((tm, tn), lambda j, i: (i, lax.rem(j, nj))),
            tile2,
            pl.BlockSpec((tm, tn), lambda j, i: (i, jnp.minimum(j, nj - 1))),
            pl.BlockSpec((tm, tn), lambda j, i: (i, jnp.maximum(j - nj, 0))),
        ],
        out_specs=[tile2, pl.BlockSpec((tm, tn), lambda j, i: (i, pre0 + j)), pl.BlockSpec((1, tn), lambda j, i: (0, j))],
        out_shape=[jax.ShapeDtypeStruct((S, 2 * D), BF16), jax.ShapeDtypeStruct((S, in_width), BF16), jax.ShapeDtypeStruct((1, 2 * D), F32)],
        compiler_params=_cparams("parallel", "arbitrary"),
        name=name,
    )(dmixed, gates, y_pool, y_attn)


def _pool_counts(i, tm, rows, w):
    t = i * tm + lax.broadcasted_iota(jnp.int32, (rows, 1), 0)
    return jnp.minimum(t + 1, w).astype(F32)


def _pooled_groups(u_ref, uh_ref, i, tm, C):
    cur = u_ref[...]
    halo = jnp.where(i == 0, 0.0, uh_ref[...])
    xx = jnp.concatenate([halo, cur], axis=0)
    out = []
    s = xx
    for gi, w in enumerate(POOL_WINDOWS):
        s = s + pltpu.roll(s, w // 2, 0)
        tot = s[HALO:, 0:C]
        out.append(tot / _pool_counts(i, tm, tm, w) - cur[:, gi * C:(gi + 1) * C])
        s = s[:, C:] if gi + 1 < len(POOL_WINDOWS) else s
    return out


def _pool_fwd(u, wl, scale, name):
    S, PW = u.shape
    C = PW // len(POOL_WINDOWS)
    tm = _tile(S, 512)
    hb = tm // HALO

    def body(u_ref, uh_ref, wl_ref, sc_ref, o_ref):
        i = pl.program_id(0)
        pooled = _pooled_groups(u_ref, uh_ref, i, tm, C)
        for gi in range(len(POOL_WINDOWS)):
            y = _dot_nn(pooled[gi].astype(BF16), wl_ref[gi])
            o_ref[:, gi * C:(gi + 1) * C] = (y * sc_ref[:, gi * C:(gi + 1) * C]).astype(o_ref.dtype)

    return pl.pallas_call(
        body,
        grid=(S // tm,),
        in_specs=[
            pl.BlockSpec((tm, PW), lambda i: (i, 0)),
            pl.BlockSpec((HALO, PW), lambda i: (jnp.maximum(i * hb - 1, 0), 0)),
            pl.BlockSpec((len(POOL_WINDOWS), C, C), lambda i: (0, 0, 0)),
            pl.BlockSpec((1, PW), lambda i: (0, 0)),
        ],
        out_specs=pl.BlockSpec((tm, PW), lambda i: (i, 0)),
        out_shape=jax.ShapeDtypeStruct((S, PW), BF16),
        compiler_params=_cparams("parallel"),
        name=name,
    )(u, u, wl, scale)


def _pool_bwd(u, dp, wl, scale, dproj, name):
    S, PW = u.shape
    G = len(POOL_WINDOWS)
    C = PW // G
    tm = _tile(S, 512)
    hb, ni = tm // HALO, S // tm
    n = tm + HALO

    def body(u_ref, uh_ref, dp_ref, dpn_ref, wl_ref, sc_ref, _, du_ref, dwl_ref, dsc_ref):
        i = pl.program_id(0)

        @pl.when(i == 0)
        def _():
            dwl_ref[...] = jnp.zeros_like(dwl_ref)
            dsc_ref[...] = jnp.zeros_like(dsc_ref)

        pooled = _pooled_groups(u_ref, uh_ref, i, tm, C)
        dpc = dp_ref[...].astype(F32)
        dpn = jnp.where(i == ni - 1, 0.0, dpn_ref[...].astype(F32))
        sc = sc_ref[...]
        dyl = jnp.concatenate([dpc, dpn], axis=0) * sc
        for gi, w in enumerate(POOL_WINDOWS):
            cols = slice(gi * C, (gi + 1) * C)
            pb = pooled[gi].astype(BF16)
            ylin = _dot_nn(pb, wl_ref[gi])
            dsc_ref[:, cols] += jnp.sum(dpc[:, cols] * ylin, axis=0, keepdims=True)
            dylg = dyl[:, cols].astype(BF16)
            dwl_ref[gi] += _dot(pb, dylg[:tm], ((0,), (0,)))
            dpool = _dot_nt(dylg, wl_ref[gi])
            e = dpool / _pool_counts(i, tm, n, w)
            k = 1
            while k < w:
                e = e + pltpu.roll(e, n - k, 0)
                k *= 2
            du_ref[:, cols] = (e[:tm] - dpool[:tm]).astype(du_ref.dtype)

    return pl.pallas_call(
        body,
        grid=(ni,),
        in_specs=[
            pl.BlockSpec((tm, PW), lambda i: (i, 0)),
            pl.BlockSpec((HALO, PW), lambda i: (jnp.maximum(i * hb - 1, 0), 0)),
            pl.BlockSpec((tm, PW), lambda i: (i, 0)),
            pl.BlockSpec((HALO, PW), lambda i: (jnp.minimum((i + 1) * hb, S // HALO - 1), 0)),
            pl.BlockSpec((G, C, C), lambda i: (0, 0, 0)),
            pl.BlockSpec((1, PW), lambda i: (0, 0)),
            ANY,
        ],
        out_specs=[pl.BlockSpec((tm, PW), lambda i: (i, 0)), pl.BlockSpec((G, C, C), lambda i: (0, 0, 0)), pl.BlockSpec((1, PW), lambda i: (0, 0))],
        out_shape=[jax.ShapeDtypeStruct(dproj.shape, dproj.dtype), jax.ShapeDtypeStruct((G, C, C), F32), jax.ShapeDtypeStruct((1, PW), F32)],
        input_output_aliases={6: 0},
        compiler_params=_cparams("arbitrary"),
        name=name,
    )(u, u, dp, dp, wl, scale, dproj)


def _band_masks():
    ii = lax.broadcasted_iota(jnp.int32, (SPAN, SPAN), 0)
    kk = lax.broadcasted_iota(jnp.int32, (SPAN, SPAN), 1)
    return ((ii + SPAN - kk).astype(F32), kk >= ii), ((ii - kk).astype(F32), kk <= ii)


ATTN_TILE = 16 * SPAN


def _unit_rows(r, b, d):
    return pl.ds(d * SPAN * b + r, SPAN, stride=d) if d > 1 else pl.ds(SPAN * b, SPAN)


def _f32_copies(refs, scratch, d):
    if d == 1:
        return list(refs)
    for ref, s in zip(refs, scratch):
        s[...] = ref[...].astype(F32)
    return list(scratch)


def _attn_fwd(qkv, d, g, name):
    S = qkv.shape[0]
    T = min(ATTN_TILE, S)
    P = SPAN * d
    nbk = T // P

    def body(q_ref, k_ref, v_ref, kp_ref, vp_ref, o_ref, lse_ref, *scratch):
        c = pl.program_id(0)
        (jp, mp), (jc, mc) = _band_masks()
        slopes = [ALIBI_SLOPES[g * HEADS_PER_GROUP + h] * d for h in range(HEADS_PER_GROUP)]
        slope = slopes[0]
        for h in range(1, HEADS_PER_GROUP):
            slope = jnp.where(pl.program_id(1) == h, slopes[h], slope)
        q_s, k_s, v_s, kp_s, vp_s = _f32_copies((q_ref, k_ref, v_ref, kp_ref, vp_ref), scratch[:5], d)
        o_s, l_s = (o_ref, lse_ref) if d == 1 else scratch[5:7]
        for r in range(d):
            for b in range(nbk):
                rows = _unit_rows(r, b, d)
                q = q_s[rows, :].astype(BF16)
                kc, vc = k_s[rows, :].astype(BF16), v_s[rows, :].astype(BF16)
                if b == 0:
                    prev = _unit_rows(r, 0, d)
                    kp, vp, okp = kp_s[prev, :].astype(BF16), vp_s[prev, :].astype(BF16), jnp.logical_and(mp, c > 0)
                else:
                    prev = _unit_rows(r, b - 1, d)
                    kp, vp, okp = k_s[prev, :].astype(BF16), v_s[prev, :].astype(BF16), mp
                sc = jnp.where(mc, _dot_nt(q, kc) * ATTN_SCALE - slope * jc, NEG_BIG)
                sp = jnp.where(okp, _dot_nt(q, kp) * ATTN_SCALE - slope * jp, NEG_BIG)
                m = jnp.maximum(jnp.max(sc, axis=-1, keepdims=True), jnp.max(sp, axis=-1, keepdims=True))
                pc, pp = jnp.exp(sc - m), jnp.exp(sp - m)
                l = jnp.sum(pc, axis=-1, keepdims=True) + jnp.sum(pp, axis=-1, keepdims=True)
                o_s[rows, :] = (_dot_nn(pc.astype(BF16), vc) + _dot_nn(pp.astype(BF16), vp)) / l
                l_s[rows, :] = jnp.broadcast_to(m + jnp.log(l), (SPAN, HEAD_DIM))
        if d > 1:
            o_ref[...] = o_s[...]
            lse_ref[...] = l_s[...]

    col = lambda kind: (lambda c, h: (c, kind * N_ATTN_HEADS + g * HEADS_PER_GROUP + h))
    pcol = lambda kind: (lambda c, h: (jnp.maximum(c * nbk - 1, 0), kind * N_ATTN_HEADS + g * HEADS_PER_GROUP + h))
    cur = lambda kind: pl.BlockSpec((T, HEAD_DIM), col(kind))
    prv = lambda kind: pl.BlockSpec((P, HEAD_DIM), pcol(kind))
    out = pl.BlockSpec((T, HEAD_DIM), lambda c, h: (c, h))
    scratch = [] if d == 1 else [pltpu.VMEM((T, HEAD_DIM), F32)] * 3 + [pltpu.VMEM((P, HEAD_DIM), F32)] * 2 + [pltpu.VMEM((T, HEAD_DIM), F32)] * 2
    return pl.pallas_call(
        body,
        grid=(S // T, HEADS_PER_GROUP),
        in_specs=[cur(0), cur(1), cur(2), prv(1), prv(2)],
        out_specs=[out, out],
        out_shape=[jax.ShapeDtypeStruct((S, GROUP_WIDTH), F32)] * 2,
        scratch_shapes=scratch,
        compiler_params=_cparams("parallel", "parallel"),
        name=name,
    )(qkv, qkv, qkv, qkv, qkv)


def _attn_merge(os_, lses, name):
    S, W = os_[0].shape
    tm = _tile(S, 512)

    def body(o0, o1, o2, l0, l1, l2, y_ref, lse_ref):
        ls = [l0[...], l1[...], l2[...]]
        m = jnp.maximum(jnp.maximum(ls[0], ls[1]), ls[2])
        es = [jnp.exp(v - m) for v in ls]
        tot = es[0] + es[1] + es[2]
        y = (es[0] * o0[...] + es[1] * o1[...] + es[2] * o2[...]) / tot
        y_ref[...] = y.astype(y_ref.dtype)
        lse_ref[...] = m + jnp.log(tot)

    row = pl.BlockSpec((tm, W), lambda i: (i, 0))
    return pl.pallas_call(
        body,
        grid=(S // tm,),
        in_specs=[row] * 6,
        out_specs=[row, row],
        out_shape=[jax.ShapeDtypeStruct((S, W), BF16), jax.ShapeDtypeStruct((S, W), F32)],
        compiler_params=_cparams("parallel"),
        name=name,
    )(*os_, *lses)


def _attn_bwd(qkv, dattn, y, lse, dproj, d, g, col0, name):
    S = qkv.shape[0]
    T = min(ATTN_TILE, S)
    P = SPAN * d
    nbk = T // P
    ntile = S // T

    def body(q_ref, k_ref, v_ref, kp_ref, vp_ref, qn_ref, da_ref, dan_ref, y_ref, yn_ref, lse_ref, lsen_ref, _, out_ref, dq_s, dk_s, dv_s, *scratch):
        c = pl.program_id(0)
        head_id = pl.program_id(1)
        kind = pl.program_id(2)

        @pl.when(kind == 0)
        def _():
            (jp, mp), (jc, mc) = _band_masks()
            slopes = [ALIBI_SLOPES[g * HEADS_PER_GROUP + h] * d for h in range(HEADS_PER_GROUP)]
            slope = slopes[0]
            for h in range(1, HEADS_PER_GROUP):
                slope = jnp.where(head_id == h, slopes[h], slope)
            q_s, k_s, v_s, da_s, y_s, kp_s, vp_s, qn_s, dan_s, yn_s = _f32_copies(
                (q_ref, k_ref, v_ref, da_ref, y_ref, kp_ref, vp_ref, qn_ref, dan_ref, yn_ref), scratch, d)
            for r in range(d):
                dq = [None] * nbk
                dk = [None] * nbk
                dv = [None] * nbk

                def add(lst, idx, val):
                    lst[idx] = val if lst[idx] is None else lst[idx] + val

                for qb in range(nbk + 1):
                    if qb < nbk:
                        rows = _unit_rows(r, qb, d)
                        q, da, yy, lse_blk = q_s[rows, :], da_s[rows, :], y_s[rows, :], lse_ref[rows, :]
                    else:
                        rows = _unit_rows(r, 0, d)
                        q, da, yy, lse_blk = qn_s[rows, :], dan_s[rows, :], yn_s[rows, :], lsen_ref[rows, :]
                    lse_col = lse_blk[:, 0:1]
                    dd = jnp.sum(da.astype(F32) * yy.astype(F32), axis=-1, keepdims=True)
                    q, da = q.astype(BF16), da.astype(BF16)
                    for kb in (qb - 1, qb):
                        if kb >= nbk:
                            continue
                        if kb < 0:
                            krows = _unit_rows(r, 0, d)
                            kk, vv, ok = kp_s[krows, :].astype(BF16), vp_s[krows, :].astype(BF16), jnp.logical_and(mp, c > 0)
                        else:
                            krows = _unit_rows(r, kb, d)
                            kk, vv = k_s[krows, :].astype(BF16), v_s[krows, :].astype(BF16)
                            ok = mc if kb == qb else (mp if qb < nbk else jnp.logical_and(mp, c < ntile - 1))
                        jj = jc if kb == qb else jp
                        s = jnp.where(ok, _dot_nt(q, kk) * ATTN_SCALE - slope * jj, NEG_BIG)
                        p = jnp.exp(s - lse_col)
                        ds = p * (_dot_nt(da, vv) - dd)
                        if qb < nbk:
                            add(dq, qb, _dot_nn(ds.astype(BF16), kk))
                        if kb >= 0:
                            add(dv, kb, _dot_nn(p.T.astype(BF16), da))
                            add(dk, kb, _dot_nn(ds.T.astype(BF16), q))
                for b in range(nbk):
                    rows = _unit_rows(r, b, d)
                    dq_s[rows, :] = dq[b] * ATTN_SCALE
                    dk_s[rows, :] = dk[b] * ATTN_SCALE
                    dv_s[rows, :] = dv[b]
            out_ref[...] = dq_s[...].astype(out_ref.dtype)

        @pl.when(kind == 1)
        def _():
            out_ref[...] = dk_s[...].astype(out_ref.dtype)

        @pl.when(kind == 2)
        def _():
            out_ref[...] = dv_s[...].astype(out_ref.dtype)

    head = lambda h: g * HEADS_PER_GROUP + h
    cur = lambda kind: pl.BlockSpec((T, HEAD_DIM), lambda c, h, kd: (c, kind * N_ATTN_HEADS + head(h)))
    prv = lambda kind: pl.BlockSpec((P, HEAD_DIM), lambda c, h, kd: (jnp.maximum(c * nbk - 1, 0), kind * N_ATTN_HEADS + head(h)))
    nxt_row = lambda c: jnp.minimum((c + 1) * nbk, S // P - 1)
    qnext = pl.BlockSpec((P, HEAD_DIM), lambda c, h, kd: (nxt_row(c), head(h)))
    hcur = pl.BlockSpec((T, HEAD_DIM), lambda c, h, kd: (c, h))
    hnext = pl.BlockSpec((P, HEAD_DIM), lambda c, h, kd: (nxt_row(c), h))
    out = pl.BlockSpec((T, HEAD_DIM), lambda c, h, kd: (c, col0 + kd * N_ATTN_HEADS + head(h)))
    stage = [pltpu.VMEM((T, HEAD_DIM), F32)] * 3
    copies = [] if d == 1 else [pltpu.VMEM((T, HEAD_DIM), F32)] * 5 + [pltpu.VMEM((P, HEAD_DIM), F32)] * 5
    return pl.pallas_call(
        body,
        grid=(ntile, HEADS_PER_GROUP, 3),
        in_specs=[cur(0), cur(1), cur(2), prv(1), prv(2), qnext, hcur, hnext, hcur, hnext, hcur, hnext, ANY],
        out_specs=out,
        out_shape=jax.ShapeDtypeStruct(dproj.shape, dproj.dtype),
        input_output_aliases={12: 0},
        scratch_shapes=stage + copies,
        compiler_params=_cparams("parallel", "parallel", "arbitrary"),
        name=name,
    )(qkv, qkv, qkv, qkv, qkv, qkv, dattn, dattn, y, y, lse, lse, dproj)


def _row_block(R, C, bytes_per_row_elem=4, budget=1 << 20):
    if R % 8:
        return R
    best = 8
    t = 8
    while t <= R:
        if R % t == 0 and t * C * bytes_per_row_elem <= budget:
            best = t
        t += 8
    return best


def _adamw(w, g, m, v, name):
    R, C = w.shape
    tr = _row_block(R, C)
    c1 = 1.0 - ADAM_B1 ** ADAM_STEP
    c2 = 1.0 - ADAM_B2 ** ADAM_STEP

    def body(w_ref, g_ref, m_ref, v_ref, d_ref, nm_ref, nv_ref):
        gv = g_ref[...]
        nm = ADAM_B1 * m_ref[...] + (1.0 - ADAM_B1) * gv
        nv = ADAM_B2 * v_ref[...] + (1.0 - ADAM_B2) * (gv * gv)
        d_ref[...] = -ADAM_LR * ((nm / c1) / (jnp.sqrt(nv / c2) + ADAM_EPS) + ADAM_WD * w_ref[...])
        nm_ref[...] = nm
        nv_ref[...] = nv

    blk = pl.BlockSpec((tr, C), lambda i: (i, 0))
    return pl.pallas_call(
        body,
        grid=(R // tr,),
        in_specs=[blk] * 4,
        out_specs=[blk] * 3,
        out_shape=[jax.ShapeDtypeStruct((R, C), F32)] * 3,
        compiler_params=_cparams("parallel"),
        name=name,
    )(w, g, m, v)


def _sum_pieces(grad, axis, recv, pos, name):
    n, pr, pc = recv.shape
    tr = _row_block(pr, pc, bytes_per_row_elem=(n + 1) * recv.dtype.itemsize, budget=4 << 20)
    nblk = pr // tr
    if axis == 1:
        own_map = lambda i, p: (p[1] * nblk + i, p[0])
    else:
        own_map = lambda i, p: ((2 * p[0] + p[1]) * nblk + i, 0)

    def body(p_ref, own_ref, r_ref, o_ref):
        acc = own_ref[...].astype(F32)
        for s in range(n):
            acc = acc + r_ref[s].astype(F32)
        o_ref[...] = acc

    return pl.pallas_call(
        body,
        grid_spec=pltpu.PrefetchScalarGridSpec(
            num_scalar_prefetch=1,
            grid=(nblk,),
            in_specs=[pl.BlockSpec((tr, pc), own_map), pl.BlockSpec((n, tr, pc), lambda i, p: (0, i, 0))],
            out_specs=pl.BlockSpec((tr, pc), lambda i, p: (p[1] * nblk + i, 0)),
        ),
        out_shape=jax.ShapeDtypeStruct((2 * pr, pc), F32),
        compiler_params=_cparams("parallel"),
        name=name,
    )(pos, grad, recv)


def _sum_small(own, recv, me, name):
    n, R, C = recv.shape
    tr = _row_block(R, C, bytes_per_row_elem=(n + 1) * 4, budget=4 << 20)

    def body(me_ref, own_ref, r_ref, o_ref):
        acc = None
        for dev in range(n + 1):
            k = jnp.bitwise_xor(me_ref[0], dev)
            term = jnp.where(k == 0, own_ref[...], r_ref[jnp.maximum(k - 1, 0)])
            acc = term if acc is None else acc + term
        o_ref[...] = acc

    return pl.pallas_call(
        body,
        grid_spec=pltpu.PrefetchScalarGridSpec(
            num_scalar_prefetch=1,
            grid=(R // tr,),
            in_specs=[pl.BlockSpec((tr, C), lambda i, m: (i, 0)), pl.BlockSpec((n, tr, C), lambda i, m: (0, i, 0))],
            out_specs=pl.BlockSpec((tr, C), lambda i, m: (i, 0)),
        ),
        out_shape=jax.ShapeDtypeStruct((R, C), F32),
        compiler_params=_cparams("parallel"),
        name=name,
    )(me, own, recv)


def _place(shard, axis, pos, dtype, name):
    shp = list(shard.shape)
    shp[axis] *= N_CHIPS
    if shard.ndim == 3:
        assert axis == 1
        in_spec = pl.BlockSpec(shard.shape, lambda i, p: (0, 0, 0))
        out_spec = pl.BlockSpec(shard.shape, lambda i, p: (0, p[0], 0))
        grid = (1,)
    else:
        R, C = shard.shape
        tr = _row_block(R, C, bytes_per_row_elem=4, budget=2 << 20)
        nblk = R // tr
        in_spec = pl.BlockSpec((tr, C), lambda i, p: (i, 0))
        out_spec = pl.BlockSpec((tr, C), (lambda i, p: (i, p[0])) if axis == 1 else (lambda i, p: (p[0] * nblk + i, 0)))
        grid = (nblk,)

    def body(p_ref, s_ref, o_ref):
        o_ref[...] = s_ref[...].astype(o_ref.dtype)

    return pl.pallas_call(
        body,
        grid_spec=pltpu.PrefetchScalarGridSpec(num_scalar_prefetch=1, grid=grid, in_specs=[in_spec], out_specs=out_spec),
        out_shape=jax.ShapeDtypeStruct(tuple(shp), dtype),
        compiler_params=_cparams("parallel"),
        name=name,
    )(pos, shard)


HBM = pl.BlockSpec(memory_space=pltpu.HBM)
SEM = pl.BlockSpec(memory_space=pltpu.SEMAPHORE)
DATAFLOW = pltpu.SideEffectType.DATAFLOW_SIDE_EFFECTING


def _position():
    return lax.axis_index("x"), lax.axis_index("y"), lax.axis_index("c")


def _peer(k):
    x, y, c = _position()
    return ((1 - x) if k & 4 else x, (1 - y) if k & 2 else y, (1 - c) if k & 1 else c)


def _shard_slice(ref, axis, idx, size):
    start = idx * size
    if axis == ref.ndim - 1:
        start = pl.multiple_of(start, 128)
    ix = [slice(None)] * ref.ndim
    ix[axis] = pl.ds(start, size)
    return ref.at[tuple(ix)]


def _gather_plan(axes):
    def plan(refs):
        x, y, c = _position()
        out = []
        for ref, ax in zip(refs, axes):
            mine = _shard_slice(ref, ax, 2 * x + y, ref.shape[ax] // N_CHIPS)
            for k in (4, 2, 6):
                px, py, _ = _peer(k)
                out.append((mine, mine, (px, py, c)))
        return out
    return plan


def _scatter_plan(axes):
    m = len(axes)

    def plan(refs):
        out = []
        for t in range(m):
            grad, recv = refs[t], refs[m + t]
            _, pr, pc = recv.shape
            for k in range(1, N_DEV):
                px, py, pcore = _peer(k)
                if axes[t] == 0:
                    piece = grad.at[pl.ds(((2 * px + py) * 2 + pcore) * pr, pr), :]
                else:
                    piece = grad.at[pl.ds(pcore * pr, pr), pl.ds(pl.multiple_of((2 * px + py) * pc, 128), pc)]
                out.append((piece, recv.at[k - 1], (px, py, pcore)))
        return out
    return plan


def _broadcast_plan(refs):
    small, recv = refs
    return [(small, recv.at[k - 1], _peer(k)) for k in range(1, N_DEV)]


def _start_all(plan, refs, send_sems, recv_sems):
    for q, (src, dst, dev) in enumerate(plan(refs)):
        pltpu.make_async_remote_copy(src_ref=src, dst_ref=dst, send_sem=send_sems.at[q], recv_sem=recv_sems.at[q], device_id=dev, device_id_type=MESH).start()


def _wait_all(plan, refs, send_sems, recv_sems):
    for q, (src, dst, dev) in enumerate(plan(refs)):
        cp = pltpu.make_async_remote_copy(src_ref=src, dst_ref=dst, send_sem=send_sems.at[q], recv_sem=recv_sems.at[q], device_id=dev, device_id_type=MESH)
        cp.wait_send()
        cp.wait_recv()


def _push(bufs, plan, ncopies, name):
    n = len(bufs)

    def body(*refs):
        outs = refs[n:2 * n]
        send_sems, recv_sems = refs[2 * n:]
        _start_all(plan, outs, send_sems, recv_sems)
        _wait_all(plan, outs, send_sems, recv_sems)

    return pl.pallas_call(
        body,
        in_specs=[ANY] * n,
        out_specs=[ANY] * n,
        out_shape=[jax.ShapeDtypeStruct(b.shape, b.dtype) for b in bufs],
        input_output_aliases={t: t for t in range(n)},
        scratch_shapes=[pltpu.SemaphoreType.DMA((ncopies,)), pltpu.SemaphoreType.DMA((ncopies,))],
        name=name,
    )(*bufs)


def _gather_once_per_chip(full, name):
    R, C = full.shape
    R2, C4 = R // 2, C // N_CHIPS

    def body(_, ref, send_sems, recv_sems):
        x, y, c = _position()
        chips = [_peer(k)[:2] for k in (4, 2, 6)]

        def half(chip, core):
            return ref.at[pl.ds(core * R2, R2), pl.ds(pl.multiple_of(chip * C4, 128), C4)]

        def copy(q, chip, core, to):
            return pltpu.make_async_remote_copy(src_ref=half(chip, core), dst_ref=half(chip, core), send_sem=send_sems.at[q], recv_sem=recv_sems.at[q],
                                                device_id=to, device_id_type=MESH)

        sends = [copy(q, 2 * x + y, c, (px, py, c)) for q, (px, py) in enumerate(chips)]
        for cp in sends:
            cp.start()
        for q, (px, py) in enumerate(chips):
            copy(q, 2 * px + py, c, (px, py, c)).wait_recv()
            passed = copy(3 + q, 2 * px + py, c, (x, y, 1 - c))
            passed.start()
            sends.append(passed)
        for q, (px, py) in enumerate(chips):
            copy(3 + q, 2 * px + py, 1 - c, (x, y, 1 - c)).wait_recv()
        for cp in sends:
            cp.wait_send()

    return pl.pallas_call(
        body,
        in_specs=[ANY],
        out_specs=ANY,
        out_shape=jax.ShapeDtypeStruct(full.shape, full.dtype),
        input_output_aliases={0: 0},
        scratch_shapes=[pltpu.SemaphoreType.DMA((6,)), pltpu.SemaphoreType.DMA((6,))],
        name=name,
    )(full)


def _push_start(bufs, plan, ncopies, name, after=None):
    n = len(bufs)
    extra = [] if after is None else [after]

    def body(*refs):
        ins = refs[:n]
        first_out = n + len(extra)
        send_sems, recv_sems, token = refs[first_out], refs[first_out + 1], refs[-1]
        _start_all(plan, ins, send_sems, recv_sems)
        token[...] = jnp.zeros_like(token)

    res = pl.pallas_call(
        body,
        name=name,
        out_shape=(pltpu.SemaphoreType.DMA((ncopies,)), pltpu.SemaphoreType.DMA((ncopies,)), *[pltpu.HBM(b.shape, b.dtype) for b in bufs],
                   jax.ShapeDtypeStruct((8, 128), F32)),
        in_specs=[HBM] * n + [ANY] * len(extra),
        out_specs=(SEM, SEM, *[HBM] * n, pl.BlockSpec(memory_space=pltpu.VMEM)),
        input_output_aliases={t: t + 2 for t in range(n)},
        compiler_params=pltpu.CompilerParams(has_side_effects=DATAFLOW),
    )(*[pltpu.with_memory_space_constraint(b, pltpu.HBM) for b in bufs], *extra)
    return res[0], res[1], list(res[2:2 + n]), res[-1]


def _push_wait(send_sems, recv_sems, bufs, plan, after, name):
    n = len(bufs)

    def body(*refs):
        ins = refs[:n]
        _wait_all(plan, ins, refs[n], refs[n + 1])

    return pl.pallas_call(
        body,
        name=name,
        out_shape=tuple(pltpu.HBM(b.shape, b.dtype) for b in bufs),
        in_specs=[HBM] * n + [SEM, SEM, ANY],
        out_specs=tuple([HBM] * n),
        input_output_aliases={t: t for t in range(n)},
        compiler_params=pltpu.CompilerParams(has_side_effects=DATAFLOW),
    )(*bufs, send_sems, recv_sems, after)


EXCHANGE_CHUNKS = 2


def _exchange_plan(refs):
    x, y, c = _position()
    out = []
    for ref in refs:
        rows = ref.shape[0] // (2 * EXCHANGE_CHUNKS)
        for q in range(EXCHANGE_CHUNKS):
            mine = ref.at[pl.ds((c * EXCHANGE_CHUNKS + q) * rows, rows), :]
            out.append((mine, mine, (x, y, 1 - c)))
    return out


LATE_WEIGHTS = (("w_pool_lin", "w_pool_out", "w_attn_out", "w_out"), ("w_up", "conv_w", "w_down"))


def _local_step(x, tgt, w, late_weights, send):
    S, D = x.shape
    PW = w["pool_scale"].shape[1]
    o_q = PW
    o_g = PW + 3 * ATTN_WIDTH
    QKV = 3 * ATTN_WIDTH

    h1 = _rms_fwd(x, w["g_mix"], "rms1")
    proj_tiles = (_tile(S, 1024), 512, D)
    u = _mm(h1, w["w_in"], mode="nn", dims=(S, PW, D), tiles=proj_tiles, out_dtypes=(F32,), name="proj_u")
    qkv = _mm(h1, w["w_in"], mode="nn", dims=(S, QKV, D), tiles=proj_tiles, b_off=(0, o_q), name="proj_qkv")

    def gate_epilogue(acc, ex, outs):
        outs[0][...] = 1.0 / (1.0 + jnp.exp(-(acc + ex[0][...])))

    gates = _mm(h1, w["w_in"], mode="nn", dims=(S, 2 * D, D), tiles=proj_tiles, b_off=(0, o_g), out_dtypes=(F32,), epilogue=gate_epilogue,
                extras=[(w["b_gate"], "n", (0, 0))], name="proj_gates")

    os_, lses = [], []
    for gi, (_, d) in enumerate(ATTN_GROUPS):
        o, lse = _attn_fwd(qkv, d, gi, f"attn_fwd{gi}")
        os_.append(o)
        lses.append(lse)
    attn, lse_tot = _attn_merge(os_, lses, "attn_merge")

    w = dict(w, **late_weights(0, attn))
    pool_out = _pool_fwd(u, w["w_pool_lin"], w["pool_scale"], "pool_fwd")
    y_pool = _mm(pool_out, w["w_pool_out"], mode="nn", dims=(S, D, PW), name="y_pool")

    def mix_epilogue(acc, ex, outs):
        outs[0][...] = acc.astype(BF16)
        outs[1][...] = (ex[0][...] * ex[2][...].astype(F32) + ex[1][...] * acc).astype(BF16)

    y_attn, mixed = _mm(attn, w["w_attn_out"], mode="nn", dims=(S, D, GROUP_WIDTH), out_dtypes=(BF16, BF16), epilogue=mix_epilogue,
                        extras=[(gates, "mn", (0, 0)), (gates, "mn", (0, D)), (y_pool, "mn", (0, 0))], name="y_attn_mix")

    def residual_epilogue(acc, ex, outs):
        outs[0][...] = ex[0][...] + acc

    x2 = _mm(mixed, w["w_out"], mode="nn", dims=(S, D, D), out_dtypes=(F32,), epilogue=residual_epilogue, extras=[(x, "mn", (0, 0))], name="out_proj")

    h2 = _rms_fwd(x2, w["g_ffn"], "rms2")
    w = dict(w, **late_weights(1, h2))
    F = w["w_down"].shape[0]
    up = _mm(h2, w["w_up"], mode="nn", dims=(S, 2 * F, D), name="up_proj")
    f = _convglu_fwd(up, w["conv_w"], w["conv_b"], "convglu_fwd")
    x3 = _mm(f, w["w_down"], mode="nn", dims=(S, D, F), out_dtypes=(F32,), epilogue=residual_epilogue, extras=[(x2, "mn", (0, 0))], name="down_proj")

    g = {}
    dx3, dx3b, g["g_final"], loss_cols = _loss_head(x3, tgt, w["g_final"], "loss_head")

    g["w_down"] = _mm(f, dx3b, mode="tn", dims=(F, D, S), name="dw_down")
    sent = send(("w_down",), g)
    df = _mm(dx3b, w["w_down"], mode="nt", dims=(S, F, D), name="d_f")
    da, db, dcb_a, dcb_b, dcw_a, dcw_b = _convglu_bwd(df, up, w["conv_w"], w["conv_b"] + sent, "convglu_bwd")
    g["conv_b"] = jnp.concatenate([dcb_a, dcb_b], axis=1)
    g["conv_w"] = jnp.concatenate([dcw_a, dcw_b], axis=1)
    dup = _conv_transpose(da, db, w["conv_w"], "conv_transpose")
    g["w_up"] = _mm(h2, dup, mode="tn", dims=(D, 2 * F, S), name="dw_up")
    sent = send(("w_up",), g)
    dh2 = _mm(dup, w["w_up"], mode="nt", dims=(S, D, 2 * F), name="d_h2")
    dx2, dx2b, g["g_ffn"] = _rms_bwd(dh2, x2, w["g_ffn"] + sent, dx3, "rms2_bwd", True)

    g["w_out"] = _mm(mixed, dx2b, mode="tn", dims=(D, D, S), name="dw_out")
    dmixed = _mm(dx2b, w["w_out"], mode="nt", dims=(S, D, D), name="d_mixed")
    IN = w["w_in"].shape[1]
    dy_both, dproj, g["b_gate"] = _gate_bwd(dmixed, gates, y_pool, y_attn, IN, "gate_bwd")

    g["w_pool_out"] = _mm(pool_out, dy_both, mode="tn", dims=(PW, D, S), name="dw_pool_out")
    g["w_attn_out"] = _mm(attn, dy_both, mode="tn", dims=(GROUP_WIDTH, D, S), b_off=(0, D), name="dw_attn_out")
    sent = send(("w_out", "w_pool_out", "w_attn_out"), g)
    dpool = _mm(dy_both, w["w_pool_out"], mode="nt", dims=(S, PW, D), name="d_pool")
    dattn = _mm(dy_both, w["w_attn_out"], mode="nt", dims=(S, GROUP_WIDTH, D), a_off=(0, D), name="d_attn")

    dproj, g["w_pool_lin"], g["pool_scale"] = _pool_bwd(u, dpool, w["w_pool_lin"], w["pool_scale"] + sent, dproj, "pool_bwd")
    g["loss_cols"] = loss_cols
    sent = send("small", g)

    for gi, (_, d) in enumerate(ATTN_GROUPS):
        dproj = _attn_bwd(qkv, dattn, attn, lse_tot, dproj, d, gi, PW // HEAD_DIM, f"attn_bwd{gi}")

    g["w_in"] = _mm(h1, dproj, mode="tn", dims=(D, IN, S), name="dw_in")
    sent = sent + send(("w_in",), g)
    dh1 = _mm(dproj, w["w_in"], mode="nt", dims=(S, D, IN), tiles=(_tile(S, 1024), _tile(D, 2048), _tile(IN, 512)), name="d_h1")
    (grad_x, g["g_mix"]) = _rms_bwd(dh1, x, w["g_mix"] + sent, dx2, "rms1_bwd", False)
    return loss_cols, grad_x, g


BIG = ("w_in", "w_pool_out", "w_attn_out", "w_out", "w_up", "w_down")
BIG_AXIS = {"w_in": 1, "w_pool_out": 1, "w_attn_out": 1, "w_out": 0, "w_up": 1, "w_down": 0}
GATHER_AXIS = dict(BIG_AXIS, w_pool_lin=1, conv_w=1)
SMALL = ("loss_cols", "b_gate", "w_pool_lin", "pool_scale", "g_ffn", "conv_w", "conv_b", "g_final")
SMALL_COLS = 1024
ORDER = ("g_mix", "w_in", "b_gate", "w_pool_lin", "pool_scale", "w_pool_out", "w_attn_out", "w_out", "g_ffn", "w_up", "conv_w", "conv_b", "w_down", "g_final")


def _as_rows(parts):
    flat = jnp.concatenate([p.astype(F32).reshape(-1) for p in parts])
    rows = -(-flat.shape[0] // (8 * SMALL_COLS)) * 8
    return jnp.pad(flat, (0, rows * SMALL_COLS - flat.shape[0])).reshape(rows, SMALL_COLS)


def kernel(x, g_mix, w_in, b_gate, w_pool_lin, pool_scale, w_pool_out, w_attn_out, w_out, g_ffn, w_up, conv_w, conv_b, w_down, g_final, loss_target, m_g_mix, m_w_in, m_b_gate, m_w_pool_lin, m_pool_scale, m_w_pool_out, m_w_attn_out, m_w_out, m_g_ffn, m_w_up, m_conv_w, m_conv_b, m_w_down, m_g_final, v_g_mix, v_w_in, v_b_gate, v_w_pool_lin, v_pool_scale, v_w_pool_out, v_w_attn_out, v_w_out, v_g_ffn, v_w_up, v_conv_w, v_conv_b, v_w_down, v_g_final):
    shard = dict(g_mix=g_mix, w_in=w_in, b_gate=b_gate, w_pool_lin=w_pool_lin, pool_scale=pool_scale, w_pool_out=w_pool_out, w_attn_out=w_attn_out,
                 w_out=w_out, g_ffn=g_ffn, w_up=w_up, conv_w=conv_w, conv_b=conv_b, w_down=w_down, g_final=g_final)
    mom = dict(g_mix=m_g_mix, w_in=m_w_in, b_gate=m_b_gate, w_pool_lin=m_w_pool_lin, pool_scale=m_pool_scale, w_pool_out=m_w_pool_out, w_attn_out=m_w_attn_out,
               w_out=m_w_out, g_ffn=m_g_ffn, w_up=m_w_up, conv_w=m_conv_w, conv_b=m_conv_b, w_down=m_w_down, g_final=m_g_final)
    vel = dict(g_mix=v_g_mix, w_in=v_w_in, b_gate=v_b_gate, w_pool_lin=v_w_pool_lin, pool_scale=v_pool_scale, w_pool_out=v_w_pool_out, w_attn_out=v_w_attn_out,
               w_out=v_w_out, g_ffn=v_g_ffn, w_up=v_w_up, conv_w=v_conv_w, conv_b=v_conv_b, w_down=v_w_down, g_final=v_g_final)
    chip = 2 * lax.axis_index("x") + lax.axis_index("y")
    pos = jnp.stack([chip, lax.axis_index("c")]).astype(jnp.int32)
    me = (2 * chip + lax.axis_index("c")).astype(jnp.int32).reshape(1)
    D = x.shape[2]

    placed = {k: _place(shard[k][0], GATHER_AXIS[k], pos, F32 if k == "conv_w" else BF16, f"place_{k}") for k in GATHER_AXIS}
    w_in_full = _gather_once_per_chip(placed["w_in"], "comm_gather_w_in")
    late, late_token, prior = [], 0.0, w_in_full
    for stage, names in enumerate(LATE_WEIGHTS):
        plan = _gather_plan([GATHER_AXIS[k] for k in names])
        send_sems, recv_sems, bufs, token = _push_start([placed[k] for k in names], plan, 3 * len(names), f"comm_gather_late{stage}_start", after=prior)
        late.append((names, send_sems, recv_sems, bufs, plan))
        late_token, prior = late_token + token[0, 0], token

    def late_weights(stage, after):
        names, send_sems, recv_sems, bufs, plan = late[stage]
        return dict(zip(names, _push_wait(send_sems, recv_sems, bufs, plan, after, f"comm_gather_late{stage}_wait")))

    pending = []

    def send(names, g):
        if names == "small":
            bufs = [_as_rows([g[k] for k in SMALL])]
            bufs.append(lax.empty((N_DEV - 1,) + bufs[0].shape, F32))
            plan, tag = _broadcast_plan, "small"
        else:
            bufs = [g[k] for k in names]
            for k in names:
                R, C = g[k].shape
                piece = (R // (2 * N_CHIPS), C) if BIG_AXIS[k] == 0 else (R // 2, C // N_CHIPS)
                bufs.append(lax.empty((N_DEV - 1,) + piece, BF16))
            plan, tag = _scatter_plan([BIG_AXIS[k] for k in names]), names[0]
        ncopies = (N_DEV - 1) * (len(bufs) // 2)
        send_sems, recv_sems, thru, token = _push_start(bufs, plan, ncopies, f"comm_scatter_start_{tag}")
        pending.append((names, send_sems, recv_sems, thru, plan, tag))
        return token[0, 0]

    w0 = dict(g_mix=shard["g_mix"] + late_token, w_in=w_in_full, b_gate=shard["b_gate"], pool_scale=shard["pool_scale"], g_ffn=shard["g_ffn"],
              conv_b=shard["conv_b"], g_final=shard["g_final"].reshape(1, D))
    _, grad_x, gr = _local_step(x[0], loss_target[0], w0, late_weights, send)

    halves, small_parts = {}, None
    for names, send_sems, recv_sems, thru, plan, tag in pending:
        done = _push_wait(send_sems, recv_sems, thru, plan, grad_x, f"comm_scatter_wait_{tag}")
        if names == "small":
            small_parts = _sum_small(done[0], done[1], me, "sum_small").reshape(-1)
        else:
            m = len(names)
            for t, k in enumerate(names):
                halves[k] = _sum_pieces(done[t], BIG_AXIS[k], done[m + t], pos, f"sum_{k}")
    g_mix_own = _as_rows([gr["g_mix"]])
    _, g_mix_recv = _push([g_mix_own, lax.empty((N_DEV - 1,) + g_mix_own.shape, F32)], _broadcast_plan, N_DEV - 1, "comm_gather_g_mix")
    g_mix_sum = _sum_small(g_mix_own, g_mix_recv, me, "sum_g_mix").reshape(-1)[:D]
    wholes = _push([halves[k] for k in BIG], _exchange_plan, EXCHANGE_CHUNKS * len(BIG), "comm_exchange_halves")

    grads = {"g_mix": g_mix_sum.reshape(shard["g_mix"].shape)}
    for k, whole in zip(BIG, wholes):
        grads[k] = whole.reshape(shard[k].shape)
    off = 0
    loss = None
    for k in SMALL:
        sz = math.prod(gr[k].shape)
        fullg = small_parts[off:off + sz].reshape(gr[k].shape)
        off += sz
        if k == "loss_cols":
            loss = jnp.sum(fullg)
            continue
        if k in ("w_pool_lin", "conv_w"):
            n = shard[k].shape[2]
            fullg = lax.dynamic_slice_in_dim(fullg, chip * n, n, axis=1)
        grads[k] = fullg.reshape(shard[k].shape)

    deltas, new_m, new_v = {}, {}, {}
    for k in ORDER:
        shp = shard[k].shape
        two_d = (-1, shp[-1])
        dl, nm, nv = _adamw(shard[k].reshape(two_d), grads[k].reshape(two_d), mom[k].reshape(two_d), vel[k].reshape(two_d), f"adamw_{k}")
        deltas[k], new_m[k], new_v[k] = dl.reshape(shp), nm.reshape(shp), nv.reshape(shp)

    return (loss, grad_x[None], *[grads[k] for k in ORDER], *[deltas[k] for k in ORDER], *[new_m[k] for k in ORDER], *[new_v[k] for k in ORDER])
```

```python
import functools
import math

import jax
import jax.numpy as jnp
from jax import lax
from jax.experimental import pallas as pl
from jax.experimental.pallas import tpu as pltpu

F32 = jnp.float32
BF16 = jnp.bfloat16

RMS_EPS = 1e-6
POOL_WINDOWS = (2, 4, 8, 16)
ATTN_GROUPS = ((128, 1), (512, 4), (2048, 16))
HEADS_PER_GROUP = 4
HEAD_DIM = 128
N_ATTN_HEADS = HEADS_PER_GROUP * len(ATTN_GROUPS)
SPAN = 128
GROUP_WIDTH = HEADS_PER_GROUP * HEAD_DIM
ATTN_WIDTH = N_ATTN_HEADS * HEAD_DIM
ATTN_SCALE = HEAD_DIM ** -0.5
NEG_BIG = -1e30
ALIBI_SLOPES = tuple(2.0 ** (-8.0 * (h + 1) / N_ATTN_HEADS) for h in range(N_ATTN_HEADS))

ADAM_LR = 0.001
ADAM_B1 = 0.9
ADAM_B2 = 0.999
ADAM_EPS = 1e-08
ADAM_WD = 0.01
ADAM_STEP = 10

INV_SQRT2 = 1.0 / math.sqrt(2.0)
INV_SQRT_2PI = 1.0 / math.sqrt(2.0 * math.pi)

HALO = 16
VMEM_LIMIT = 56 * 1024 * 1024
N_CHIPS = 4
N_DEV = 8
MESH = pl.DeviceIdType.MESH
ANY = pl.BlockSpec(memory_space=pl.ANY)


def _cparams(*sem):
    return pltpu.CompilerParams(dimension_semantics=sem, vmem_limit_bytes=VMEM_LIMIT)


def _tile(n, pref, mult=128):
    t = (min(pref, n) // mult) * mult
    while t >= mult:
        if n % t == 0:
            return t
        t -= mult
    return n


def _dot(a, b, contract):
    return lax.dot_general(a, b, (contract, ((), ())), preferred_element_type=F32)


def _dot_nn(a, b):
    return _dot(a, b, ((1,), (0,)))


def _dot_nt(a, b):
    return _dot(a, b, ((1,), (1,)))


def _mm(a, b, *, mode, dims, name, tiles=None, out_dtypes=(BF16,), epilogue=None, extras=(), a_off=(0, 0), b_off=(0, 0)):
    M, N, K = dims
    if tiles is None:
        tiles = (_tile(M, 1408), _tile(N, 2816), _tile(K, 1024)) if mode == "tn" else (_tile(M, 1024), _tile(N, 1536), _tile(K, 2816))
    tm, tn, tk = tiles
    assert M % tm == 0 and N % tn == 0 and K % tk == 0, (name, dims, tiles)
    nk = K // tk
    if mode == "nn":
        ab, bb, contract = (tm, tk), (tk, tn), ((1,), (0,))
        amap = lambda i, j, k: (i + a_off[0] // tm, k + a_off[1] // tk)
        bmap = lambda i, j, k: (k + b_off[0] // tk, j + b_off[1] // tn)
    elif mode == "nt":
        ab, bb, contract = (tm, tk), (tn, tk), ((1,), (1,))
        amap = lambda i, j, k: (i + a_off[0] // tm, k + a_off[1] // tk)
        bmap = lambda i, j, k: (j + b_off[0] // tn, k + b_off[1] // tk)
    else:
        ab, bb, contract = (tk, tm), (tk, tn), ((0,), (0,))
        amap = lambda i, j, k: (k + a_off[0] // tk, i + a_off[1] // tm)
        bmap = lambda i, j, k: (k + b_off[0] // tk, j + b_off[1] // tn)
    assert a_off[0] % ab[0] == 0 and a_off[1] % ab[1] == 0 and b_off[0] % bb[0] == 0 and b_off[1] % bb[1] == 0, name
    in_specs = [pl.BlockSpec(ab, amap), pl.BlockSpec(bb, bmap)]
    ex_arrays = []
    for arr, kind, off in extras:
        if kind == "mn":
            assert off[0] % tm == 0 and off[1] % tn == 0, name
            in_specs.append(pl.BlockSpec((tm, tn), lambda i, j, k, off=off: (i + off[0] // tm, j + off[1] // tn)))
        else:
            assert off[1] % tn == 0, name
            in_specs.append(pl.BlockSpec((1, tn), lambda i, j, k, off=off: (0, j + off[1] // tn)))
        ex_arrays.append(arr)
    ne, no = len(ex_arrays), len(out_dtypes)
    if epilogue is None:
        def epilogue(acc, ex, outs):
            outs[0][...] = acc.astype(outs[0].dtype)

    def body(*refs):
        a_ref, b_ref = refs[0], refs[1]
        ex, outs = refs[2:2 + ne], refs[2 + ne:2 + ne + no]
        part = _dot(a_ref[...], b_ref[...], contract)
        if nk == 1:
            epilogue(part, ex, outs)
            return
        acc = refs[-1]
        k = pl.program_id(2)

        @pl.when(k == 0)
        def _():
            acc[...] = part

        @pl.when(jnp.logical_and(k > 0, k < nk - 1))
        def _():
            acc[...] += part

        @pl.when(k == nk - 1)
        def _():
            epilogue(acc[...] + part, ex, outs)

    res = pl.pallas_call(
        body,
        grid=(M // tm, N // tn, nk),
        in_specs=in_specs,
        out_specs=[pl.BlockSpec((tm, tn), lambda i, j, k: (i, j)) for _ in out_dtypes],
        out_shape=[jax.ShapeDtypeStruct((M, N), dt) for dt in out_dtypes],
        scratch_shapes=[pltpu.VMEM((tm, tn), F32)] if nk > 1 else [],
        compiler_params=_cparams("parallel", "parallel", "arbitrary"),
        name=name,
    )(a, b, *ex_arrays)
    return res[0] if no == 1 else res


def _rms_fwd(x, g, name):
    S, D = x.shape
    tm = _tile(S, 256)

    def body(x_ref, g_ref, h_ref):
        xv = x_ref[...]
        r = lax.rsqrt(jnp.mean(xv * xv, axis=-1, keepdims=True) + RMS_EPS)
        h_ref[...] = (xv * r * g_ref[...]).astype(h_ref.dtype)

    return pl.pallas_call(
        body,
        grid=(S // tm,),
        in_specs=[pl.BlockSpec((tm, D), lambda i: (i, 0)), pl.BlockSpec((1, D), lambda i: (0, 0))],
        out_specs=pl.BlockSpec((tm, D), lambda i: (i, 0)),
        out_shape=jax.ShapeDtypeStruct((S, D), BF16),
        compiler_params=_cparams("parallel"),
        name=name,
    )(x, g)


def _rms_bwd(dh, x, g, dres, name, with_bf16):
    S, D = x.shape
    tm = _tile(S, 256)

    def body(dh_ref, x_ref, g_ref, dres_ref, *outs):
        dx_ref, dg_ref = outs[0], outs[-1]
        xv = x_ref[...]
        r = lax.rsqrt(jnp.mean(xv * xv, axis=-1, keepdims=True) + RMS_EPS)
        xr = xv * r
        dhv = dh_ref[...].astype(F32)

        @pl.when(pl.program_id(0) == 0)
        def _():
            dg_ref[...] = jnp.zeros_like(dg_ref)

        dg_ref[...] += jnp.sum(dhv * xr, axis=0, keepdims=True)
        u = dhv * g_ref[...]
        c = jnp.mean(u * xr, axis=-1, keepdims=True)
        dx = dres_ref[...] + r * (u - xr * c)
        dx_ref[...] = dx
        if with_bf16:
            outs[1][...] = dx.astype(BF16)

    row = pl.BlockSpec((tm, D), lambda i: (i, 0))
    vec = pl.BlockSpec((1, D), lambda i: (0, 0))
    out_specs = [row] + ([row] if with_bf16 else []) + [vec]
    out_shape = [jax.ShapeDtypeStruct((S, D), F32)] + ([jax.ShapeDtypeStruct((S, D), BF16)] if with_bf16 else []) + [jax.ShapeDtypeStruct((1, D), F32)]
    return pl.pallas_call(
        body,
        grid=(S // tm,),
        in_specs=[row, row, vec, row],
        out_specs=out_specs,
        out_shape=out_shape,
        compiler_params=_cparams("arbitrary"),
        name=name,
    )(dh, x, g, dres)


def _loss_head(x3, tgt, g, name):
    S, D = x3.shape
    tm = _tile(S, 256)

    def body(x_ref, t_ref, g_ref, dx_ref, dxb_ref, dg_ref, loss_ref):
        xv = x_ref[...]
        gv = g_ref[...]
        r = lax.rsqrt(jnp.mean(xv * xv, axis=-1, keepdims=True) + RMS_EPS)
        xr = xv * r
        e = xr * gv - t_ref[...]

        @pl.when(pl.program_id(0) == 0)
        def _():
            dg_ref[...] = jnp.zeros_like(dg_ref)
            loss_ref[...] = jnp.zeros_like(loss_ref)

        loss_ref[...] += jnp.sum(e * e, axis=0, keepdims=True) * (0.5 / D)
        dy = e * (1.0 / D)
        dg_ref[...] += jnp.sum(dy * xr, axis=0, keepdims=True)
        u = dy * gv
        c = jnp.mean(u * xr, axis=-1, keepdims=True)
        dx = r * (u - xr * c)
        dx_ref[...] = dx
        dxb_ref[...] = dx.astype(BF16)

    row = pl.BlockSpec((tm, D), lambda i: (i, 0))
    vec = pl.BlockSpec((1, D), lambda i: (0, 0))
    return pl.pallas_call(
        body,
        grid=(S // tm,),
        in_specs=[row, row, vec],
        out_specs=[row, row, vec, vec],
        out_shape=[jax.ShapeDtypeStruct((S, D), F32), jax.ShapeDtypeStruct((S, D), BF16), jax.ShapeDtypeStruct((1, D), F32), jax.ShapeDtypeStruct((1, D), F32)],
        compiler_params=_cparams("arbitrary"),
        name=name,
    )(x3, tgt, g)


def _conv_taps(cur_ref, halo_ref, w_ref, b_ref, first):
    cur = cur_ref[...].astype(F32)
    halo = jnp.where(first, 0.0, halo_ref[...].astype(F32))
    xx = jnp.concatenate([halo, cur], axis=0)
    p1 = pltpu.roll(xx, 1, 0)[HALO:]
    p2 = pltpu.roll(xx, 2, 0)[HALO:]
    w = w_ref[...]
    y = b_ref[...] + w[0:1] * p2 + w[1:2] * p1 + w[2:3] * cur
    return y, (cur, p1, p2)


def _convglu_specs(S, F, tm, tn, rows_axis):
    nj = F // tn
    if rows_axis == 0:
        ij = lambda f: (lambda i, j: f(i, j))
    else:
        ij = lambda f: (lambda j, i: f(i, j))
    hb = tm // HALO
    return [
        pl.BlockSpec((tm, tn), ij(lambda i, j: (i, j))),
        pl.BlockSpec((tm, tn), ij(lambda i, j: (i, j + nj))),
        pl.BlockSpec((HALO, tn), ij(lambda i, j: (jnp.maximum(i * hb - 1, 0), j))),
        pl.BlockSpec((HALO, tn), ij(lambda i, j: (jnp.maximum(i * hb - 1, 0), j + nj))),
        pl.BlockSpec((3, tn), ij(lambda i, j: (0, j))),
        pl.BlockSpec((3, tn), ij(lambda i, j: (0, j + nj))),
        pl.BlockSpec((1, tn), ij(lambda i, j: (0, j))),
        pl.BlockSpec((1, tn), ij(lambda i, j: (0, j + nj))),
    ]


def _convglu_fwd(up, cw, cb, name):
    S, F2 = up.shape
    F = F2 // 2
    tm, tn = _tile(S, 512), _tile(F, 512)

    def body(ua, ub, ha, hb, wa, wb, ba, bb, f_ref):
        first = pl.program_id(0) == 0
        a, _ = _conv_taps(ua, ha, wa, ba, first)
        b, _ = _conv_taps(ub, hb, wb, bb, first)
        f_ref[...] = (0.5 * a * (1.0 + lax.erf(a * INV_SQRT2)) * b).astype(f_ref.dtype)

    return pl.pallas_call(
        body,
        grid=(S // tm, F // tn),
        in_specs=_convglu_specs(S, F, tm, tn, 0),
        out_specs=pl.BlockSpec((tm, tn), lambda i, j: (i, j)),
        out_shape=jax.ShapeDtypeStruct((S, F), BF16),
        compiler_params=_cparams("parallel", "parallel"),
        name=name,
    )(up, up, up, up, cw, cw, cb, cb)


def _convglu_bwd(df, up, cw, cb, name):
    S, F2 = up.shape
    F = F2 // 2
    tm, tn = _tile(S, 512), _tile(F, 512)

    def body(df_ref, ua, ub, ha, hb, wa, wb, ba, bb, da_ref, db_ref, dba_ref, dbb_ref, dwa_ref, dwb_ref):
        first = pl.program_id(1) == 0
        a, pa = _conv_taps(ua, ha, wa, ba, first)
        b, pb = _conv_taps(ub, hb, wb, bb, first)
        dfv = df_ref[...].astype(F32)
        cdf = 0.5 * (1.0 + lax.erf(a * INV_SQRT2))
        pdf = jnp.exp(-0.5 * a * a) * INV_SQRT_2PI
        da = dfv * b * (cdf + a * pdf)
        db = dfv * (a * cdf)
        da_ref[...] = da.astype(BF16)
        db_ref[...] = db.astype(BF16)

        @pl.when(first)
        def _():
            for r in (dba_ref, dbb_ref, dwa_ref, dwb_ref):
                r[...] = jnp.zeros_like(r)

        for d, taps, dbias, dw in ((da, pa, dba_ref, dwa_ref), (db, pb, dbb_ref, dwb_ref)):
            dbias[...] += jnp.sum(d, axis=0, keepdims=True)
            dw[0:1, :] += jnp.sum(d * taps[2], axis=0, keepdims=True)
            dw[1:2, :] += jnp.sum(d * taps[1], axis=0, keepdims=True)
            dw[2:3, :] += jnp.sum(d * taps[0], axis=0, keepdims=True)

    tile = pl.BlockSpec((tm, tn), lambda j, i: (i, j))
    b1 = pl.BlockSpec((1, tn), lambda j, i: (0, j))
    b3 = pl.BlockSpec((3, tn), lambda j, i: (0, j))
    return pl.pallas_call(
        body,
        grid=(F // tn, S // tm),
        in_specs=[tile] + _convglu_specs(S, F, tm, tn, 1),
        out_specs=[tile, tile, b1, b1, b3, b3],
        out_shape=[jax.ShapeDtypeStruct((S, F), BF16)] * 2 + [jax.ShapeDtypeStruct((1, F), F32)] * 2 + [jax.ShapeDtypeStruct((3, F), F32)] * 2,
        compiler_params=_cparams("parallel", "arbitrary"),
        name=name,
    )(df, up, up, up, up, cw, cw, cb, cb)


def _conv_transpose(da, db, cw, name):
    S, F = da.shape
    tm, tn = _tile(S, 512), _tile(F, 512)
    nj, ni, hb = F // tn, S // tm, tm // HALO
    n = tm + HALO

    def body(a_ref, b_ref, an_ref, bn_ref, w_ref, o_ref):
        j, i = pl.program_id(0), pl.program_id(1)

        def run(c_ref, h_ref):
            cur = c_ref[...].astype(F32)
            halo = jnp.where(i == ni - 1, 0.0, h_ref[...].astype(F32))
            xx = jnp.concatenate([cur, halo], axis=0)
            n1 = pltpu.roll(xx, n - 1, 0)[:tm]
            n2 = pltpu.roll(xx, n - 2, 0)[:tm]
            w = w_ref[...]
            o_ref[...] = (w[2:3] * cur + w[1:2] * n1 + w[0:1] * n2).astype(o_ref.dtype)

        @pl.when(j < nj)
        def _():
            run(a_ref, an_ref)

        @pl.when(j >= nj)
        def _():
            run(b_ref, bn_ref)

    ja = lambda j: jnp.minimum(j, nj - 1)
    jb = lambda j: jnp.maximum(j - nj, 0)
    nxt = lambda i: jnp.minimum((i + 1) * hb, S // HALO - 1)
    return pl.pallas_call(
        body,
        grid=(2 * nj, ni),
        in_specs=[
            pl.BlockSpec((tm, tn), lambda j, i: (i, ja(j))),
            pl.BlockSpec((tm, tn), lambda j, i: (i, jb(j))),
            pl.BlockSpec((HALO, tn), lambda j, i: (nxt(i), ja(j))),
            pl.BlockSpec((HALO, tn), lambda j, i: (nxt(i), jb(j))),
            pl.BlockSpec((3, tn), lambda j, i: (0, j)),
        ],
        out_specs=pl.BlockSpec((tm, tn), lambda j, i: (i, j)),
        out_shape=jax.ShapeDtypeStruct((S, 2 * F), BF16),
        compiler_params=_cparams("parallel", "parallel"),
        name=name,
    )(da, db, da, db, cw)


def _gate_bwd(dmixed, gates, y_pool, y_attn, in_width, name):
    S, D = dmixed.shape
    tm, tn = _tile(S, 512), _tile(D, 512)
    nj = D // tn
    pre0 = (in_width - 2 * D) // tn
    assert pre0 * tn == in_width - 2 * D

    def body(dm_ref, g_ref, yp_ref, ya_ref, dy_ref, dpre_ref, db_ref):
        j = pl.program_id(0)

        @pl.when(pl.program_id(1) == 0)
        def _():
            db_ref[...] = jnp.zeros_like(db_ref)

        def run(y_ref):
            dm = dm_ref[...].astype(F32)
            gv = g_ref[...]
            dy_ref[...] = (dm * gv).astype(BF16)
            dpre = dm * y_ref[...].astype(F32) * gv * (1.0 - gv)
            dpre_ref[...] = dpre.astype(BF16)
            db_ref[...] += jnp.sum(dpre, axis=0, keepdims=True)

        @pl.when(j < nj)
        def _():
            run(yp_ref)

        @pl.when(j >= nj)
        def _():
            run(ya_ref)

    tile2 = pl.BlockSpec((tm, tn), lambda j, i: (i, j))
    return pl.pallas_call(
        body,
        grid=(2 * nj, S // tm),
        in_specs=[
            pl.BlockSpec((tm, tn), lambda j, i: (i, lax.rem(j, nj))),
            tile2,
            pl.BlockSpec((tm, tn), lambda j, i: (i, jnp.minimum(j, nj - 1))),
            pl.BlockSpec((tm, tn), lambda j, i: (i, jnp.maximum(j - nj, 0))),
        ],
        out_specs=[tile2, pl.BlockSpec((tm, tn), lambda j, i: (i, pre0 + j)), pl.BlockSpec((1, tn), lambda j, i: (0, j))],
        out_shape=[jax.ShapeDtypeStruct((S, 2 * D), BF16), jax.ShapeDtypeStruct((S, in_width), BF16), jax.ShapeDtypeStruct((1, 2 * D), F32)],
        compiler_params=_cparams("parallel", "arbitrary"),
        name=name,
    )(dmixed, gates, y_pool, y_attn)


def _pool_counts(i, tm, rows, w):
    t = i * tm + lax.broadcasted_iota(jnp.int32, (rows, 1), 0)
    return jnp.minimum(t + 1, w).astype(F32)


def _pooled_groups(u_ref, uh_ref, i, tm, C):
    cur = u_ref[...]
    halo = jnp.where(i == 0, 0.0, uh_ref[...])
    xx = jnp.concatenate([halo, cur], axis=0)
    out = []
    s = xx
    for gi, w in enumerate(POOL_WINDOWS):
        s = s + pltpu.roll(s, w // 2, 0)
        tot = s[HALO:, 0:C]
        out.append(tot / _pool_counts(i, tm, tm, w) - cur[:, gi * C:(gi + 1) * C])
        s = s[:, C:] if gi + 1 < len(POOL_WINDOWS) else s
    return out


def _pool_fwd(u, wl, scale, name):
    S, PW = u.shape
    C = PW // len(POOL_WINDOWS)
    tm = _tile(S, 512)
    hb = tm // HALO

    def body(u_ref, uh_ref, wl_ref, sc_ref, o_ref):
        i = pl.program_id(0)
        pooled = _pooled_groups(u_ref, uh_ref, i, tm, C)
        for gi in range(len(POOL_WINDOWS)):
            y = _dot_nn(pooled[gi].astype(BF16), wl_ref[gi])
            o_ref[:, gi * C:(gi + 1) * C] = (y * sc_ref[:, gi * C:(gi + 1) * C]).astype(o_ref.dtype)

    return pl.pallas_call(
        body,
        grid=(S // tm,),
        in_specs=[
            pl.BlockSpec((tm, PW), lambda i: (i, 0)),
            pl.BlockSpec((HALO, PW), lambda i: (jnp.maximum(i * hb - 1, 0), 0)),
            pl.BlockSpec((len(POOL_WINDOWS), C, C), lambda i: (0, 0, 0)),
            pl.BlockSpec((1, PW), lambda i: (0, 0)),
        ],
        out_specs=pl.BlockSpec((tm, PW), lambda i: (i, 0)),
        out_shape=jax.ShapeDtypeStruct((S, PW), BF16),
        compiler_params=_cparams("parallel"),
        name=name,
    )(u, u, wl, scale)


def _pool_bwd(u, dp, wl, scale, dproj, name):
    S, PW = u.shape
    G = len(POOL_WINDOWS)
    C = PW // G
    tm = _tile(S, 512)
    hb, ni = tm // HALO, S // tm
    n = tm + HALO

    def body(u_ref, uh_ref, dp_ref, dpn_ref, wl_ref, sc_ref, _, du_ref, dwl_ref, dsc_ref):
        i = pl.program_id(0)

        @pl.when(i == 0)
        def _():
            dwl_ref[...] = jnp.zeros_like(dwl_ref)
            dsc_ref[...] = jnp.zeros_like(dsc_ref)

        pooled = _pooled_groups(u_ref, uh_ref, i, tm, C)
        dpc = dp_ref[...].astype(F32)
        dpn = jnp.where(i == ni - 1, 0.0, dpn_ref[...].astype(F32))
        sc = sc_ref[...]
        dyl = jnp.concatenate([dpc, dpn], axis=0) * sc
        for gi, w in enumerate(POOL_WINDOWS):
            cols = slice(gi * C, (gi + 1) * C)
            pb = pooled[gi].astype(BF16)
            ylin = _dot_nn(pb, wl_ref[gi])
            dsc_ref[:, cols] += jnp.sum(dpc[:, cols] * ylin, axis=0, keepdims=True)
            dylg = dyl[:, cols].astype(BF16)
            dwl_ref[gi] += _dot(pb, dylg[:tm], ((0,), (0,)))
            dpool = _dot_nt(dylg, wl_ref[gi])
            e = dpool / _pool_counts(i, tm, n, w)
            k = 1
            while k < w:
                e = e + pltpu.roll(e, n - k, 0)
                k *= 2
            du_ref[:, cols] = (e[:tm] - dpool[:tm]).astype(du_ref.dtype)

    return pl.pallas_call(
        body,
        grid=(ni,),
        in_specs=[
            pl.BlockSpec((tm, PW), lambda i: (i, 0)),
            pl.BlockSpec((HALO, PW), lambda i: (jnp.maximum(i * hb - 1, 0), 0)),
            pl.BlockSpec((tm, PW), lambda i: (i, 0)),
            pl.BlockSpec((HALO, PW), lambda i: (jnp.minimum((i + 1) * hb, S // HALO - 1), 0)),
            pl.BlockSpec((G, C, C), lambda i: (0, 0, 0)),
            pl.BlockSpec((1, PW), lambda i: (0, 0)),
            ANY,
        ],
        out_specs=[pl.BlockSpec((tm, PW), lambda i: (i, 0)), pl.BlockSpec((G, C, C), lambda i: (0, 0, 0)), pl.BlockSpec((1, PW), lambda i: (0, 0))],
        out_shape=[jax.ShapeDtypeStruct(dproj.shape, dproj.dtype), jax.ShapeDtypeStruct((G, C, C), F32), jax.ShapeDtypeStruct((1, PW), F32)],
        input_output_aliases={6: 0},
        compiler_params=_cparams("arbitrary"),
        name=name,
    )(u, u, dp, dp, wl, scale, dproj)


def _band_masks():
    ii = lax.broadcasted_iota(jnp.int32, (SPAN, SPAN), 0)
    kk = lax.broadcasted_iota(jnp.int32, (SPAN, SPAN), 1)
    return ((ii + SPAN - kk).astype(F32), kk >= ii), ((ii - kk).astype(F32), kk <= ii)


ATTN_TILE = 16 * SPAN


def _unit_rows(r, b, d):
    return pl.ds(d * SPAN * b + r, SPAN, stride=d) if d > 1 else pl.ds(SPAN * b, SPAN)


def _f32_copies(refs, scratch, d):
    if d == 1:
        return list(refs)
    for ref, s in zip(refs, scratch):
        s[...] = ref[...].astype(F32)
    return list(scratch)


def _attn_fwd(qkv, d, g, name):
    S = qkv.shape[0]
    T = min(ATTN_TILE, S)
    P = SPAN * d
    nbk = T // P

    def body(q_ref, k_ref, v_ref, kp_ref, vp_ref, o_ref, lse_ref, *scratch):
        c = pl.program_id(0)
        (jp, mp), (jc, mc) = _band_masks()
        slopes = [ALIBI_SLOPES[g * HEADS_PER_GROUP + h] * d for h in range(HEADS_PER_GROUP)]
        slope = slopes[0]
        for h in range(1, HEADS_PER_GROUP):
            slope = jnp.where(pl.program_id(1) == h, slopes[h], slope)
        q_s, k_s, v_s, kp_s, vp_s = _f32_copies((q_ref, k_ref, v_ref, kp_ref, vp_ref), scratch[:5], d)
        o_s, l_s = (o_ref, lse_ref) if d == 1 else scratch[5:7]
        for r in range(d):
            for b in range(nbk):
                rows = _unit_rows(r, b, d)
                q = q_s[rows, :].astype(BF16)
                kc, vc = k_s[rows, :].astype(BF16), v_s[rows, :].astype(BF16)
                if b == 0:
                    prev = _unit_rows(r, 0, d)
                    kp, vp, okp = kp_s[prev, :].astype(BF16), vp_s[prev, :].astype(BF16), jnp.logical_and(mp, c > 0)
                else:
                    prev = _unit_rows(r, b - 1, d)
                    kp, vp, okp = k_s[prev, :].astype(BF16), v_s[prev, :].astype(BF16), mp
                sc = jnp.where(mc, _dot_nt(q, kc) * ATTN_SCALE - slope * jc, NEG_BIG)
                sp = jnp.where(okp, _dot_nt(q, kp) * ATTN_SCALE - slope * jp, NEG_BIG)
                m = jnp.maximum(jnp.max(sc, axis=-1, keepdims=True), jnp.max(sp, axis=-1, keepdims=True))
                pc, pp = jnp.exp(sc - m), jnp.exp(sp - m)
                l = jnp.sum(pc, axis=-1, keepdims=True) + jnp.sum(pp, axis=-1, keepdims=True)
                o_s[rows, :] = (_dot_nn(pc.astype(BF16), vc) + _dot_nn(pp.astype(BF16), vp)) / l
                l_s[rows, :] = jnp.broadcast_to(m + jnp.log(l), (SPAN, HEAD_DIM))
        if d > 1:
            o_ref[...] = o_s[...]
            lse_ref[...] = l_s[...]

    col = lambda kind: (lambda c, h: (c, kind * N_ATTN_HEADS + g * HEADS_PER_GROUP + h))
    pcol = lambda kind: (lambda c, h: (jnp.maximum(c * nbk - 1, 0), kind * N_ATTN_HEADS + g * HEADS_PER_GROUP + h))
    cur = lambda kind: pl.BlockSpec((T, HEAD_DIM), col(kind))
    prv = lambda kind: pl.BlockSpec((P, HEAD_DIM), pcol(kind))
    out = pl.BlockSpec((T, HEAD_DIM), lambda c, h: (c, h))
    scratch = [] if d == 1 else [pltpu.VMEM((T, HEAD_DIM), F32)] * 3 + [pltpu.VMEM((P, HEAD_DIM), F32)] * 2 + [pltpu.VMEM((T, HEAD_DIM), F32)] * 2
    return pl.pallas_call(
        body,
        grid=(S // T, HEADS_PER_GROUP),
        in_specs=[cur(0), cur(1), cur(2), prv(1), prv(2)],
        out_specs=[out, out],
        out_shape=[jax.ShapeDtypeStruct((S, GROUP_WIDTH), F32)] * 2,
        scratch_shapes=scratch,
        compiler_params=_cparams("parallel", "parallel"),
        name=name,
    )(qkv, qkv, qkv, qkv, qkv)


def _attn_merge(os_, lses, name):
    S, W = os_[0].shape
    tm = _tile(S, 512)

    def body(o0, o1, o2, l0, l1, l2, y_ref, lse_ref):
        ls = [l0[...], l1[...], l2[...]]
        m = jnp.maximum(jnp.maximum(ls[0], ls[1]), ls[2])
        es = [jnp.exp(v - m) for v in ls]
        tot = es[0] + es[1] + es[2]
        y = (es[0] * o0[...] + es[1] * o1[...] + es[2] * o2[...]) / tot
        y_ref[...] = y.astype(y_ref.dtype)
        lse_ref[...] = m + jnp.log(tot)

    row = pl.BlockSpec((tm, W), lambda i: (i, 0))
    return pl.pallas_call(
        body,
        grid=(S // tm,),
        in_specs=[row] * 6,
        out_specs=[row, row],
        out_shape=[jax.ShapeDtypeStruct((S, W), BF16), jax.ShapeDtypeStruct((S, W), F32)],
        compiler_params=_cparams("parallel"),
        name=name,
    )(*os_, *lses)


def _attn_bwd(qkv, dattn, y, lse, dproj, d, g, col0, name):
    S = qkv.shape[0]
    T = min(ATTN_TILE, S)
    P = SPAN * d
    nbk = T // P
    ntile = S // T

    def body(q_ref, k_ref, v_ref, kp_ref, vp_ref, qn_ref, da_ref, dan_ref, y_ref, yn_ref, lse_ref, lsen_ref, _, out_ref, dq_s, dk_s, dv_s, *scratch):
        c = pl.program_id(0)
        head_id = pl.program_id(1)
        kind = pl.program_id(2)

        @pl.when(kind == 0)
        def _():
            (jp, mp), (jc, mc) = _band_masks()
            slopes = [ALIBI_SLOPES[g * HEADS_PER_GROUP + h] * d for h in range(HEADS_PER_GROUP)]
            slope = slopes[0]
            for h in range(1, HEADS_PER_GROUP):
                slope = jnp.where(head_id == h, slopes[h], slope)
            q_s, k_s, v_s, da_s, y_s, kp_s, vp_s, qn_s, dan_s, yn_s = _f32_copies(
                (q_ref, k_ref, v_ref, da_ref, y_ref, kp_ref, vp_ref, qn_ref, dan_ref, yn_ref), scratch, d)
            for r in range(d):
                dq = [None] * nbk
                dk = [None] * nbk
                dv = [None] * nbk

                def add(lst, idx, val):
                    lst[idx] = val if lst[idx] is None else lst[idx] + val

                for qb in range(nbk + 1):
                    if qb < nbk:
                        rows = _unit_rows(r, qb, d)
                        q, da, yy, lse_blk = q_s[rows, :], da_s[rows, :], y_s[rows, :], lse_ref[rows, :]
                    else:
                        rows = _unit_rows(r, 0, d)
                        q, da, yy, lse_blk = qn_s[rows, :], dan_s[rows, :], yn_s[rows, :], lsen_ref[rows, :]
                    lse_col = lse_blk[:, 0:1]
                    dd = jnp.sum(da.astype(F32) * yy.astype(F32), axis=-1, keepdims=True)
                    q, da = q.astype(BF16), da.astype(BF16)
                    for kb in (qb - 1, qb):
                        if kb >= nbk:
                            continue
                        if kb < 0:
                            krows = _unit_rows(r, 0, d)
                            kk, vv, ok = kp_s[krows, :].astype(BF16), vp_s[krows, :].astype(BF16), jnp.logical_and(mp, c > 0)
                        else:
                            krows = _unit_rows(r, kb, d)
                            kk, vv = k_s[krows, :].astype(BF16), v_s[krows, :].astype(BF16)
                            ok = mc if kb == qb else (mp if qb < nbk else jnp.logical_and(mp, c < ntile - 1))
                        jj = jc if kb == qb else jp
                        s = jnp.where(ok, _dot_nt(q, kk) * ATTN_SCALE - slope * jj, NEG_BIG)
                        p = jnp.exp(s - lse_col)
                        ds = p * (_dot_nt(da, vv) - dd)
                        if qb < nbk:
                            add(dq, qb, _dot_nn(ds.astype(BF16), kk))
                        if kb >= 0:
                            add(dv, kb, _dot_nn(p.T.astype(BF16), da))
                            add(dk, kb, _dot_nn(ds.T.astype(BF16), q))
                for b in range(nbk):
                    rows = _unit_rows(r, b, d)
                    dq_s[rows, :] = dq[b] * ATTN_SCALE
                    dk_s[rows, :] = dk[b] * ATTN_SCALE
                    dv_s[rows, :] = dv[b]
            out_ref[...] = dq_s[...].astype(out_ref.dtype)

        @pl.when(kind == 1)
        def _():
            out_ref[...] = dk_s[...].astype(out_ref.dtype)

        @pl.when(kind == 2)
        def _():
            out_ref[...] = dv_s[...].astype(out_ref.dtype)

    head = lambda h: g * HEADS_PER_GROUP + h
    cur = lambda kind: pl.BlockSpec((T, HEAD_DIM), lambda c, h, kd: (c, kind * N_ATTN_HEADS + head(h)))
    prv = lambda kind: pl.BlockSpec((P, HEAD_DIM), lambda c, h, kd: (jnp.maximum(c * nbk - 1, 0), kind * N_ATTN_HEADS + head(h)))
    nxt_row = lambda c: jnp.minimum((c + 1) * nbk, S // P - 1)
    qnext = pl.BlockSpec((P, HEAD_DIM), lambda c, h, kd: (nxt_row(c), head(h)))
    hcur = pl.BlockSpec((T, HEAD_DIM), lambda c, h, kd: (c, h))
    hnext = pl.BlockSpec((P, HEAD_DIM), lambda c, h, kd: (nxt_row(c), h))
    out = pl.BlockSpec((T, HEAD_DIM), lambda c, h, kd: (c, col0 + kd * N_ATTN_HEADS + head(h)))
    stage = [pltpu.VMEM((T, HEAD_DIM), F32)] * 3
    copies = [] if d == 1 else [pltpu.VMEM((T, HEAD_DIM), F32)] * 5 + [pltpu.VMEM((P, HEAD_DIM), F32)] * 5
    return pl.pallas_call(
        body,
        grid=(ntile, HEADS_PER_GROUP, 3),
        in_specs=[cur(0), cur(1), cur(2), prv(1), prv(2), qnext, hcur, hnext, hcur, hnext, hcur, hnext, ANY],
        out_specs=out,
        out_shape=jax.ShapeDtypeStruct(dproj.shape, dproj.dtype),
        input_output_aliases={12: 0},
        scratch_shapes=stage + copies,
        compiler_params=_cparams("parallel", "parallel", "arbitrary"),
        name=name,
    )(qkv, qkv, qkv, qkv, qkv, qkv, dattn, dattn, y, y, lse, lse, dproj)


def _row_block(R, C, bytes_per_row_elem=4, budget=1 << 20):
    if R % 8:
        return R
    best = 8
    t = 8
    while t <= R:
        if R % t == 0 and t * C * bytes_per_row_elem <= budget:
            best = t
        t += 8
    return best


def _adamw(w, g, m, v, name):
    R, C = w.shape
    tr = _row_block(R, C)
    c1 = 1.0 - ADAM_B1 ** ADAM_STEP
    c2 = 1.0 - ADAM_B2 ** ADAM_STEP

    def body(w_ref, g_ref, m_ref, v_ref, d_ref, nm_ref, nv_ref):
        gv = g_ref[...]
        nm = ADAM_B1 * m_ref[...] + (1.0 - ADAM_B1) * gv
        nv = ADAM_B2 * v_ref[...] + (1.0 - ADAM_B2) * (gv * gv)
        d_ref[...] = -ADAM_LR * ((nm / c1) / (jnp.sqrt(nv / c2) + ADAM_EPS) + ADAM_WD * w_ref[...])
        nm_ref[...] = nm
        nv_ref[...] = nv

    blk = pl.BlockSpec((tr, C), lambda i: (i, 0))
    return pl.pallas_call(
        body,
        grid=(R // tr,),
        in_specs=[blk] * 4,
        out_specs=[blk] * 3,
        out_shape=[jax.ShapeDtypeStruct((R, C), F32)] * 3,
        compiler_params=_cparams("parallel"),
        name=name,
    )(w, g, m, v)


def _sum_pieces(grad, axis, recv, pos, name):
    n, pr, pc = recv.shape
    tr = _row_block(pr, pc, bytes_per_row_elem=(n + 1) * recv.dtype.itemsize, budget=4 << 20)
    nblk = pr // tr
    if axis == 1:
        own_map = lambda i, p: (p[1] * nblk + i, p[0])
    else:
        own_map = lambda i, p: ((2 * p[0] + p[1]) * nblk + i, 0)

    def body(p_ref, own_ref, r_ref, o_ref):
        acc = own_ref[...].astype(F32)
        for s in range(n):
            acc = acc + r_ref[s].astype(F32)
        o_ref[...] = acc

    return pl.pallas_call(
        body,
        grid_spec=pltpu.PrefetchScalarGridSpec(
            num_scalar_prefetch=1,
            grid=(nblk,),
            in_specs=[pl.BlockSpec((tr, pc), own_map), pl.BlockSpec((n, tr, pc), lambda i, p: (0, i, 0))],
            out_specs=pl.BlockSpec((tr, pc), lambda i, p: (p[1] * nblk + i, 0)),
        ),
        out_shape=jax.ShapeDtypeStruct((2 * pr, pc), F32),
        compiler_params=_cparams("parallel"),
        name=name,
    )(pos, grad, recv)


def _sum_small(own, recv, me, name):
    n, R, C = recv.shape
    tr = _row_block(R, C, bytes_per_row_elem=(n + 1) * 4, budget=4 << 20)

    def body(me_ref, own_ref, r_ref, o_ref):
        acc = None
        for dev in range(n + 1):
            k = jnp.bitwise_xor(me_ref[0], dev)
            term = jnp.where(k == 0, own_ref[...], r_ref[jnp.maximum(k - 1, 0)])
            acc = term if acc is None else acc + term
        o_ref[...] = acc

    return pl.pallas_call(
        body,
        grid_spec=pltpu.PrefetchScalarGridSpec(
            num_scalar_prefetch=1,
            grid=(R // tr,),
            in_specs=[pl.BlockSpec((tr, C), lambda i, m: (i, 0)), pl.BlockSpec((n, tr, C), lambda i, m: (0, i, 0))],
            out_specs=pl.BlockSpec((tr, C), lambda i, m: (i, 0)),
        ),
        out_shape=jax.ShapeDtypeStruct((R, C), F32),
        compiler_params=_cparams("parallel"),
        name=name,
    )(me, own, recv)


def _place(shard, axis, pos, dtype, name):
    shp = list(shard.shape)
    shp[axis] *= N_CHIPS
    if shard.ndim == 3:
        assert axis == 1
        in_spec = pl.BlockSpec(shard.shape, lambda i, p: (0, 0, 0))
        out_spec = pl.BlockSpec(shard.shape, lambda i, p: (0, p[0], 0))
        grid = (1,)
    else:
        R, C = shard.shape
        tr = _row_block(R, C, bytes_per_row_elem=4, budget=2 << 20)
        nblk = R // tr
        in_spec = pl.BlockSpec((tr, C), lambda i, p: (i, 0))
        out_spec = pl.BlockSpec((tr, C), (lambda i, p: (i, p[0])) if axis == 1 else (lambda i, p: (p[0] * nblk + i, 0)))
        grid = (nblk,)

    def body(p_ref, s_ref, o_ref):
        o_ref[...] = s_ref[...].astype(o_ref.dtype)

    return pl.pallas_call(
        body,
        grid_spec=pltpu.PrefetchScalarGridSpec(num_scalar_prefetch=1, grid=grid, in_specs=[in_spec], out_specs=out_spec),
        out_shape=jax.ShapeDtypeStruct(tuple(shp), dtype),
        compiler_params=_cparams("parallel"),
        name=name,
    )(pos, shard)


HBM = pl.BlockSpec(memory_space=pltpu.HBM)
SEM = pl.BlockSpec(memory_space=pltpu.SEMAPHORE)
DATAFLOW = pltpu.SideEffectType.DATAFLOW_SIDE_EFFECTING


def _position():
    return lax.axis_index("x"), lax.axis_index("y"), lax.axis_index("c")


def _peer(k):
    x, y, c = _position()
    return ((1 - x) if k & 4 else x, (1 - y) if k & 2 else y, (1 - c) if k & 1 else c)


def _shard_slice(ref, axis, idx, size):
    start = idx * size
    if axis == ref.ndim - 1:
        start = pl.multiple_of(start, 128)
    ix = [slice(None)] * ref.ndim
    ix[axis] = pl.ds(start, size)
    return ref.at[tuple(ix)]


def _gather_plan(axes):
    def plan(refs):
        x, y, c = _position()
        out = []
        for ref, ax in zip(refs, axes):
            mine = _shard_slice(ref, ax, 2 * x + y, ref.shape[ax] // N_CHIPS)
            for k in (4, 2, 6):
                px, py, _ = _peer(k)
                out.append((mine, mine, (px, py, c)))
        return out
    return plan


def _scatter_plan(axes):
    m = len(axes)

    def plan(refs):
        out = []
        for t in range(m):
            grad, recv = refs[t], refs[m + t]
            _, pr, pc = recv.shape
            for k in range(1, N_DEV):
                px, py, pcore = _peer(k)
                if axes[t] == 0:
                    piece = grad.at[pl.ds(((2 * px + py) * 2 + pcore) * pr, pr), :]
                else:
                    piece = grad.at[pl.ds(pcore * pr, pr), pl.ds(pl.multiple_of((2 * px + py) * pc, 128), pc)]
                out.append((piece, recv.at[k - 1], (px, py, pcore)))
        return out
    return plan


def _broadcast_plan(refs):
    small, recv = refs
    return [(small, recv.at[k - 1], _peer(k)) for k in range(1, N_DEV)]


def _start_all(plan, refs, send_sems, recv_sems):
    for q, (src, dst, dev) in enumerate(plan(refs)):
        pltpu.make_async_remote_copy(src_ref=src, dst_ref=dst, send_sem=send_sems.at[q], recv_sem=recv_sems.at[q], device_id=dev, device_id_type=MESH).start()


def _wait_all(plan, refs, send_sems, recv_sems):
    for q, (src, dst, dev) in enumerate(plan(refs)):
        cp = pltpu.make_async_remote_copy(src_ref=src, dst_ref=dst, send_sem=send_sems.at[q], recv_sem=recv_sems.at[q], device_id=dev, device_id_type=MESH)
        cp.wait_send()
        cp.wait_recv()


def _push(bufs, plan, ncopies, name):
    n = len(bufs)

    def body(*refs):
        outs = refs[n:2 * n]
        send_sems, recv_sems = refs[2 * n:]
        _start_all(plan, outs, send_sems, recv_sems)
        _wait_all(plan, outs, send_sems, recv_sems)

    return pl.pallas_call(
        body,
        in_specs=[ANY] * n,
        out_specs=[ANY] * n,
        out_shape=[jax.ShapeDtypeStruct(b.shape, b.dtype) for b in bufs],
        input_output_aliases={t: t for t in range(n)},
        scratch_shapes=[pltpu.SemaphoreType.DMA((ncopies,)), pltpu.SemaphoreType.DMA((ncopies,))],
        name=name,
    )(*bufs)


def _gather_once_per_chip(full, name):
    R, C = full.shape
    R2, C4 = R // 2, C // N_CHIPS

    def body(_, ref, send_sems, recv_sems):
        x, y, c = _position()
        chips = [_peer(k)[:2] for k in (4, 2, 6)]

        def half(chip, core):
            return ref.at[pl.ds(core * R2, R2), pl.ds(pl.multiple_of(chip * C4, 128), C4)]

        def copy(q, chip, core, to):
            return pltpu.make_async_remote_copy(src_ref=half(chip, core), dst_ref=half(chip, core), send_sem=send_sems.at[q], recv_sem=recv_sems.at[q],
                                                device_id=to, device_id_type=MESH)

        sends = [copy(q, 2 * x + y, c, (px, py, c)) for q, (px, py) in enumerate(chips)]
        for cp in sends:
            cp.start()
        for q, (px, py) in enumerate(chips):
            copy(q, 2 * px + py, c, (px, py, c)).wait_recv()
            passed = copy(3 + q, 2 * px + py, c, (x, y, 1 - c))
            passed.start()
            sends.append(passed)
        for q, (px, py) in enumerate(chips):
            copy(3 + q, 2 * px + py, 1 - c, (x, y, 1 - c)).wait_recv()
        for cp in sends:
            cp.wait_send()

    return pl.pallas_call(
        body,
        in_specs=[ANY],
        out_specs=ANY,
        out_shape=jax.ShapeDtypeStruct(full.shape, full.dtype),
        input_output_aliases={0: 0},
        scratch_shapes=[pltpu.SemaphoreType.DMA((6,)), pltpu.SemaphoreType.DMA((6,))],
        name=name,
    )(full)


def _push_start(bufs, plan, ncopies, name, after=None):
    n = len(bufs)
    extra = [] if after is None else [after]

    def body(*refs):
        ins = refs[:n]
        first_out = n + len(extra)
        send_sems, recv_sems, token = refs[first_out], refs[first_out + 1], refs[-1]
        _start_all(plan, ins, send_sems, recv_sems)
        token[...] = jnp.zeros_like(token)

    res = pl.pallas_call(
        body,
        name=name,
        out_shape=(pltpu.SemaphoreType.DMA((ncopies,)), pltpu.SemaphoreType.DMA((ncopies,)), *[pltpu.HBM(b.shape, b.dtype) for b in bufs],
                   jax.ShapeDtypeStruct((8, 128), F32)),
        in_specs=[HBM] * n + [ANY] * len(extra),
        out_specs=(SEM, SEM, *[HBM] * n, pl.BlockSpec(memory_space=pltpu.VMEM)),
        input_output_aliases={t: t + 2 for t in range(n)},
        compiler_params=pltpu.CompilerParams(has_side_effects=DATAFLOW),
    )(*[pltpu.with_memory_space_constraint(b, pltpu.HBM) for b in bufs], *extra)
    return res[0], res[1], list(res[2:2 + n]), res[-1]


def _push_wait(send_sems, recv_sems, bufs, plan, after, name):
    n = len(bufs)

    def body(*refs):
        ins = refs[:n]
        _wait_all(plan, ins, refs[n], refs[n + 1])

    return pl.pallas_call(
        body,
        name=name,
        out_shape=tuple(pltpu.HBM(b.shape, b.dtype) for b in bufs),
        in_specs=[HBM] * n + [SEM, SEM, ANY],
        out_specs=tuple([HBM] * n),
        input_output_aliases={t: t for t in range(n)},
        compiler_params=pltpu.CompilerParams(has_side_effects=DATAFLOW),
    )(*bufs, send_sems, recv_sems, after)


EXCHANGE_CHUNKS = 2


def _exchange_plan(refs):
    x, y, c = _position()
    out = []
    for ref in refs:
        rows = ref.shape[0] // (2 * EXCHANGE_CHUNKS)
        for q in range(EXCHANGE_CHUNKS):
            mine = ref.at[pl.ds((c * EXCHANGE_CHUNKS + q) * rows, rows), :]
            out.append((mine, mine, (x, y, 1 - c)))
    return out


LATE_WEIGHTS = (("w_pool_lin", "w_pool_out", "w_attn_out", "w_out"), ("w_up", "conv_w", "w_down"))


def _local_step(x, tgt, w, late_weights, send):
    S, D = x.shape
    PW = w["pool_scale"].shape[1]
    o_q = PW
    o_g = PW + 3 * ATTN_WIDTH
    QKV = 3 * ATTN_WIDTH

    h1 = _rms_fwd(x, w["g_mix"], "rms1")
    proj_tiles = (_tile(S, 1024), 512, D)
    u = _mm(h1, w["w_in"], mode="nn", dims=(S, PW, D), tiles=proj_tiles, out_dtypes=(F32,), name="proj_u")
    qkv = _mm(h1, w["w_in"], mode="nn", dims=(S, QKV, D), tiles=proj_tiles, b_off=(0, o_q), name="proj_qkv")

    def gate_epilogue(acc, ex, outs):
        outs[0][...] = 1.0 / (1.0 + jnp.exp(-(acc + ex[0][...])))

    gates = _mm(h1, w["w_in"], mode="nn", dims=(S, 2 * D, D), tiles=proj_tiles, b_off=(0, o_g), out_dtypes=(F32,), epilogue=gate_epilogue,
                extras=[(w["b_gate"], "n", (0, 0))], name="proj_gates")

    os_, lses = [], []
    for gi, (_, d) in enumerate(ATTN_GROUPS):
        o, lse = _attn_fwd(qkv, d, gi, f"attn_fwd{gi}")
        os_.append(o)
        lses.append(lse)
    attn, lse_tot = _attn_merge(os_, lses, "attn_merge")

    w = dict(w, **late_weights(0, attn))
    pool_out = _pool_fwd(u, w["w_pool_lin"], w["pool_scale"], "pool_fwd")
    y_pool = _mm(pool_out, w["w_pool_out"], mode="nn", dims=(S, D, PW), name="y_pool")

    def mix_epilogue(acc, ex, outs):
        outs[0][...] = acc.astype(BF16)
        outs[1][...] = (ex[0][...] * ex[2][...].astype(F32) + ex[1][...] * acc).astype(BF16)

    y_attn, mixed = _mm(attn, w["w_attn_out"], mode="nn", dims=(S, D, GROUP_WIDTH), out_dtypes=(BF16, BF16), epilogue=mix_epilogue,
                        extras=[(gates, "mn", (0, 0)), (gates, "mn", (0, D)), (y_pool, "mn", (0, 0))], name="y_attn_mix")

    def residual_epilogue(acc, ex, outs):
        outs[0][...] = ex[0][...] + acc

    x2 = _mm(mixed, w["w_out"], mode="nn", dims=(S, D, D), out_dtypes=(F32,), epilogue=residual_epilogue, extras=[(x, "mn", (0, 0))], name="out_proj")

    h2 = _rms_fwd(x2, w["g_ffn"], "rms2")
    w = dict(w, **late_weights(1, h2))
    F = w["w_down"].shape[0]
    up = _mm(h2, w["w_up"], mode="nn", dims=(S, 2 * F, D), name="up_proj")
    f = _convglu_fwd(up, w["conv_w"], w["conv_b"], "convglu_fwd")
    x3 = _mm(f, w["w_down"], mode="nn", dims=(S, D, F), out_dtypes=(F32,), epilogue=residual_epilogue, extras=[(x2, "mn", (0, 0))], name="down_proj")

    g = {}
    dx3, dx3b, g["g_final"], loss_cols = _loss_head(x3, tgt, w["g_final"], "loss_head")

    g["w_down"] = _mm(f, dx3b, mode="tn", dims=(F, D, S), name="dw_down")
    sent = send(("w_down",), g)
    df = _mm(dx3b, w["w_down"], mode="nt", dims=(S, F, D), name="d_f")
    da, db, dcb_a, dcb_b, dcw_a, dcw_b = _convglu_bwd(df, up, w["conv_w"], w["conv_b"] + sent, "convglu_bwd")
    g["conv_b"] = jnp.concatenate([dcb_a, dcb_b], axis=1)
    g["conv_w"] = jnp.concatenate([dcw_a, dcw_b], axis=1)
    dup = _conv_transpose(da, db, w["conv_w"], "conv_transpose")
    g["w_up"] = _mm(h2, dup, mode="tn", dims=(D, 2 * F, S), name="dw_up")
    sent = send(("w_up",), g)
    dh2 = _mm(dup, w["w_up"], mode="nt", dims=(S, D, 2 * F), name="d_h2")
    dx2, dx2b, g["g_ffn"] = _rms_bwd(dh2, x2, w["g_ffn"] + sent, dx3, "rms2_bwd", True)

    g["w_out"] = _mm(mixed, dx2b, mode="tn", dims=(D, D, S), name="dw_out")
    dmixed = _mm(dx2b, w["w_out"], mode="nt", dims=(S, D, D), name="d_mixed")
    IN = w["w_in"].shape[1]
    dy_both, dproj, g["b_gate"] = _gate_bwd(dmixed, gates, y_pool, y_attn, IN, "gate_bwd")

    g["w_pool_out"] = _mm(pool_out, dy_both, mode="tn", dims=(PW, D, S), name="dw_pool_out")
    g["w_attn_out"] = _mm(attn, dy_both, mode="tn", dims=(GROUP_WIDTH, D, S), b_off=(0, D), name="dw_attn_out")
    sent = send(("w_out", "w_pool_out", "w_attn_out"), g)
    dpool = _mm(dy_both, w["w_pool_out"], mode="nt", dims=(S, PW, D), name="d_pool")
    dattn = _mm(dy_both, w["w_attn_out"], mode="nt", dims=(S, GROUP_WIDTH, D), a_off=(0, D), name="d_attn")

    dproj, g["w_pool_lin"], g["pool_scale"] = _pool_bwd(u, dpool, w["w_pool_lin"], w["pool_scale"] + sent, dproj, "pool_bwd")
    g["loss_cols"] = loss_cols
    sent = send("small", g)

    for gi, (_, d) in enumerate(ATTN_GROUPS):
        dproj = _attn_bwd(qkv, dattn, attn, lse_tot, dproj, d, gi, PW // HEAD_DIM, f"attn_bwd{gi}")

    g["w_in"] = _mm(h1, dproj, mode="tn", dims=(D, IN, S), name="dw_in")
    sent = sent + send(("w_in",), g)
    dh1 = _mm(dproj, w["w_in"], mode="nt", dims=(S, D, IN), tiles=(_tile(S, 1024), _tile(D, 2048), _tile(IN, 2432)), name="d_h1")
    (grad_x, g["g_mix"]) = _rms_bwd(dh1, x, w["g_mix"] + sent, dx2, "rms1_bwd", False)
    return loss_cols, grad_x, g


BIG = ("w_in", "w_pool_out", "w_attn_out", "w_out", "w_up", "w_down")
BIG_AXIS = {"w_in": 1, "w_pool_out": 1, "w_attn_out": 1, "w_out": 0, "w_up": 1, "w_down": 0}
GATHER_AXIS = dict(BIG_AXIS, w_pool_lin=1, conv_w=1)
SMALL = ("loss_cols", "b_gate", "w_pool_lin", "pool_scale", "g_ffn", "conv_w", "conv_b", "g_final")
SMALL_COLS = 1024
ORDER = ("g_mix", "w_in", "b_gate", "w_pool_lin", "pool_scale", "w_pool_out", "w_attn_out", "w_out", "g_ffn", "w_up", "conv_w", "conv_b", "w_down", "g_final")


def _as_rows(parts):
    flat = jnp.concatenate([p.astype(F32).reshape(-1) for p in parts])
    rows = -(-flat.shape[0] // (8 * SMALL_COLS)) * 8
    return jnp.pad(flat, (0, rows * SMALL_COLS - flat.shape[0])).reshape(rows, SMALL_COLS)


def kernel(x, g_mix, w_in, b_gate, w_pool_lin, pool_scale, w_pool_out, w_attn_out, w_out, g_ffn, w_up, conv_w, conv_b, w_down, g_final, loss_target, m_g_mix, m_w_in, m_b_gate, m_w_pool_lin, m_pool_scale, m_w_pool_out, m_w_attn_out, m_w_out, m_g_ffn, m_w_up, m_conv_w, m_conv_b, m_w_down, m_g_final, v_g_mix, v_w_in, v_b_gate, v_w_pool_lin, v_pool_scale, v_w_pool_out, v_w_attn_out, v_w_out, v_g_ffn, v_w_up, v_conv_w, v_conv_b, v_w_down, v_g_final):
    shard = dict(g_mix=g_mix, w_in=w_in, b_gate=b_gate, w_pool_lin=w_pool_lin, pool_scale=pool_scale, w_pool_out=w_pool_out, w_attn_out=w_attn_out,
                 w_out=w_out, g_ffn=g_ffn, w_up=w_up, conv_w=conv_w, conv_b=conv_b, w_down=w_down, g_final=g_final)
    mom = dict(g_mix=m_g_mix, w_in=m_w_in, b_gate=m_b_gate, w_pool_lin=m_w_pool_lin, pool_scale=m_pool_scale, w_pool_out=m_w_pool_out, w_attn_out=m_w_attn_out,
               w_out=m_w_out, g_ffn=m_g_ffn, w_up=m_w_up, conv_w=m_conv_w, conv_b=m_conv_b, w_down=m_w_down, g_final=m_g_final)
    vel = dict(g_mix=v_g_mix, w_in=v_w_in, b_gate=v_b_gate, w_pool_lin=v_w_pool_lin, pool_scale=v_pool_scale, w_pool_out=v_w_pool_out, w_attn_out=v_w_attn_out,
               w_out=v_w_out, g_ffn=v_g_ffn, w_up=v_w_up, conv_w=v_conv_w, conv_b=v_conv_b, w_down=v_w_down, g_final=v_g_final)
    chip = 2 * lax.axis_index("x") + lax.axis_index("y")
    pos = jnp.stack([chip, lax.axis_index("c")]).astype(jnp.int32)
    me = (2 * chip + lax.axis_index("c")).astype(jnp.int32).reshape(1)
    D = x.shape[2]

    placed = {k: _place(shard[k][0], GATHER_AXIS[k], pos, F32 if k == "conv_w" else BF16, f"place_{k}") for k in GATHER_AXIS}
    w_in_full = _gather_once_per_chip(placed["w_in"], "comm_gather_w_in")
    late, late_token, prior = [], 0.0, w_in_full
    for stage, names in enumerate(LATE_WEIGHTS):
        plan = _gather_plan([GATHER_AXIS[k] for k in names])
        send_sems, recv_sems, bufs, token = _push_start([placed[k] for k in names], plan, 3 * len(names), f"comm_gather_late{stage}_start", after=prior)
        late.append((names, send_sems, recv_sems, bufs, plan))
        late_token, prior = late_token + token[0, 0], token

    def late_weights(stage, after):
        names, send_sems, recv_sems, bufs, plan = late[stage]
        return dict(zip(names, _push_wait(send_sems, recv_sems, bufs, plan, after, f"comm_gather_late{stage}_wait")))

    pending = []

    def send(names, g):
        if names == "small":
            bufs = [_as_rows([g[k] for k in SMALL])]
            bufs.append(lax.empty((N_DEV - 1,) + bufs[0].shape, F32))
            plan, tag = _broadcast_plan, "small"
        else:
            bufs = [g[k] for k in names]
            for k in names:
                R, C = g[k].shape
                piece = (R // (2 * N_CHIPS), C) if BIG_AXIS[k] == 0 else (R // 2, C // N_CHIPS)
                bufs.append(lax.empty((N_DEV - 1,) + piece, BF16))
            plan, tag = _scatter_plan([BIG_AXIS[k] for k in names]), names[0]
        ncopies = (N_DEV - 1) * (len(bufs) // 2)
        send_sems, recv_sems, thru, token = _push_start(bufs, plan, ncopies, f"comm_scatter_start_{tag}")
        pending.append((names, send_sems, recv_sems, thru, plan, tag))
        return token[0, 0]

    w0 = dict(g_mix=shard["g_mix"] + late_token, w_in=w_in_full, b_gate=shard["b_gate"], pool_scale=shard["pool_scale"], g_ffn=shard["g_ffn"],
              conv_b=shard["conv_b"], g_final=shard["g_final"].reshape(1, D))
    _, grad_x, gr = _local_step(x[0], loss_target[0], w0, late_weights, send)

    halves, small_parts = {}, None
    for names, send_sems, recv_sems, thru, plan, tag in pending:
        done = _push_wait(send_sems, recv_sems, thru, plan, grad_x, f"comm_scatter_wait_{tag}")
        if names == "small":
            small_parts = _sum_small(done[0], done[1], me, "sum_small").reshape(-1)
        else:
            m = len(names)
            for t, k in enumerate(names):
                halves[k] = _sum_pieces(done[t], BIG_AXIS[k], done[m + t], pos, f"sum_{k}")
    g_mix_own = _as_rows([gr["g_mix"]])
    _, g_mix_recv = _push([g_mix_own, lax.empty((N_DEV - 1,) + g_mix_own.shape, F32)], _broadcast_plan, N_DEV - 1, "comm_gather_g_mix")
    g_mix_sum = _sum_small(g_mix_own, g_mix_recv, me, "sum_g_mix").reshape(-1)[:D]
    wholes = _push([halves[k] for k in BIG], _exchange_plan, EXCHANGE_CHUNKS * len(BIG), "comm_exchange_halves")

    grads = {"g_mix": g_mix_sum.reshape(shard["g_mix"].shape)}
    for k, whole in zip(BIG, wholes):
        grads[k] = whole.reshape(shard[k].shape)
    off = 0
    loss = None
    for k in SMALL:
        sz = math.prod(gr[k].shape)
        fullg = small_parts[off:off + sz].reshape(gr[k].shape)
        off += sz
        if k == "loss_cols":
            loss = jnp.sum(fullg)
            continue
        if k in ("w_pool_lin", "conv_w"):
            n = shard[k].shape[2]
            fullg = lax.dynamic_slice_in_dim(fullg, chip * n, n, axis=1)
        grads[k] = fullg.reshape(shard[k].shape)

    deltas, new_m, new_v = {}, {}, {}
    for k in ORDER:
        shp = shard[k].shape
        two_d = (-1, shp[-1])
        dl, nm, nv = _adamw(shard[k].reshape(two_d), grads[k].reshape(two_d), mom[k].reshape(two_d), vel[k].reshape(two_d), f"adamw_{k}")
        deltas[k], new_m[k], new_v[k] = dl.reshape(shp), nm.reshape(shp), nv.reshape(shp)

    return (loss, grad_x[None], *[grads[k] for k in ORDER], *[deltas[k] for k in ORDER], *[new_m[k] for k in ORDER], *[new_v[k] for k in ORDER])
```

```python
import functools
import math

import jax
import jax.numpy as jnp
from jax import lax
from jax.experimental import pallas as pl
from jax.experimental.pallas import tpu as pltpu

F32 = jnp.float32
BF16 = jnp.bfloat16

RMS_EPS = 1e-6
POOL_WINDOWS = (2, 4, 8, 16)
ATTN_GROUPS = ((128, 1), (512, 4), (2048, 16))
HEADS_PER_GROUP = 4
HEAD_DIM = 128
N_ATTN_HEADS = HEADS_PER_GROUP * len(ATTN_GROUPS)
SPAN = 128
GROUP_WIDTH = HEADS_PER_GROUP * HEAD_DIM
ATTN_WIDTH = N_ATTN_HEADS * HEAD_DIM
ATTN_SCALE = HEAD_DIM ** -0.5
NEG_BIG = -1e30
ALIBI_SLOPES = tuple(2.0 ** (-8.0 * (h + 1) / N_ATTN_HEADS) for h in range(N_ATTN_HEADS))

ADAM_LR = 0.001
ADAM_B1 = 0.9
ADAM_B2 = 0.999
ADAM_EPS = 1e-08
ADAM_WD = 0.01
ADAM_STEP = 10

INV_SQRT2 = 1.0 / math.sqrt(2.0)
INV_SQRT_2PI = 1.0 / math.sqrt(2.0 * math.pi)

HALO = 16
VMEM_LIMIT = 56 * 1024 * 1024
N_CHIPS = 4
N_DEV = 8
MESH = pl.DeviceIdType.MESH
ANY = pl.BlockSpec(memory_space=pl.ANY)


def _cparams(*sem):
    return pltpu.CompilerParams(dimension_semantics=sem, vmem_limit_bytes=VMEM_LIMIT)


def _tile(n, pref, mult=128):
    t = (min(pref, n) // mult) * mult
    while t >= mult:
        if n % t == 0:
            return t
        t -= mult
    return n


def _dot(a, b, contract):
    return lax.dot_general(a, b, (contract, ((), ())), preferred_element_type=F32)


def _dot_nn(a, b):
    return _dot(a, b, ((1,), (0,)))


def _dot_nt(a, b):
    return _dot(a, b, ((1,), (1,)))


def _mm(a, b, *, mode, dims, name, tiles=None, out_dtypes=(BF16,), epilogue=None, extras=(), a_off=(0, 0), b_off=(0, 0)):
    M, N, K = dims
    if tiles is None:
        tiles = (_tile(M, 1408), _tile(N, 2816), _tile(K, 512)) if mode == "tn" else (_tile(M, 1024), _tile(N, 1536), _tile(K, 2816))
    tm, tn, tk = tiles
    assert M % tm == 0 and N % tn == 0 and K % tk == 0, (name, dims, tiles)
    nk = K // tk
    if mode == "nn":
        ab, bb, contract = (tm, tk), (tk, tn), ((1,), (0,))
        amap = lambda i, j, k: (i + a_off[0] // tm, k + a_off[1] // tk)
        bmap = lambda i, j, k: (k + b_off[0] // tk, j + b_off[1] // tn)
    elif mode == "nt":
        ab, bb, contract = (tm, tk), (tn, tk), ((1,), (1,))
        amap = lambda i, j, k: (i + a_off[0] // tm, k + a_off[1] // tk)
        bmap = lambda i, j, k: (j + b_off[0] // tn, k + b_off[1] // tk)
    else:
        ab, bb, contract = (tk, tm), (tk, tn), ((0,), (0,))
        amap = lambda i, j, k: (k + a_off[0] // tk, i + a_off[1] // tm)
        bmap = lambda i, j, k: (k + b_off[0] // tk, j + b_off[1] // tn)
    assert a_off[0] % ab[0] == 0 and a_off[1] % ab[1] == 0 and b_off[0] % bb[0] == 0 and b_off[1] % bb[1] == 0, name
    in_specs = [pl.BlockSpec(ab, amap), pl.BlockSpec(bb, bmap)]
    ex_arrays = []
    for arr, kind, off in extras:
        if kind == "mn":
            assert off[0] % tm == 0 and off[1] % tn == 0, name
            in_specs.append(pl.BlockSpec((tm, tn), lambda i, j, k, off=off: (i + off[0] // tm, j + off[1] // tn)))
        else:
            assert off[1] % tn == 0, name
            in_specs.append(pl.BlockSpec((1, tn), lambda i, j, k, off=off: (0, j + off[1] // tn)))
        ex_arrays.append(arr)
    ne, no = len(ex_arrays), len(out_dtypes)
    if epilogue is None:
        def epilogue(acc, ex, outs):
            outs[0][...] = acc.astype(outs[0].dtype)

    def body(*refs):
        a_ref, b_ref = refs[0], refs[1]
        ex, outs = refs[2:2 + ne], refs[2 + ne:2 + ne + no]
        part = _dot(a_ref[...], b_ref[...], contract)
        if nk == 1:
            epilogue(part, ex, outs)
            return
        acc = refs[-1]
        k = pl.program_id(2)

        @pl.when(k == 0)
        def _():
            acc[...] = part

        @pl.when(jnp.logical_and(k > 0, k < nk - 1))
        def _():
            acc[...] += part

        @pl.when(k == nk - 1)
        def _():
            epilogue(acc[...] + part, ex, outs)

    res = pl.pallas_call(
        body,
        grid=(M // tm, N // tn, nk),
        in_specs=in_specs,
        out_specs=[pl.BlockSpec((tm, tn), lambda i, j, k: (i, j)) for _ in out_dtypes],
        out_shape=[jax.ShapeDtypeStruct((M, N), dt) for dt in out_dtypes],
        scratch_shapes=[pltpu.VMEM((tm, tn), F32)] if nk > 1 else [],
        compiler_params=_cparams("parallel", "parallel", "arbitrary"),
        name=name,
    )(a, b, *ex_arrays)
    return res[0] if no == 1 else res


def _rms_fwd(x, g, name):
    S, D = x.shape
    tm = _tile(S, 256)

    def body(x_ref, g_ref, h_ref):
        xv = x_ref[...]
        r = lax.rsqrt(jnp.mean(xv * xv, axis=-1, keepdims=True) + RMS_EPS)
        h_ref[...] = (xv * r * g_ref[...]).astype(h_ref.dtype)

    return pl.pallas_call(
        body,
        grid=(S // tm,),
        in_specs=[pl.BlockSpec((tm, D), lambda i: (i, 0)), pl.BlockSpec((1, D), lambda i: (0, 0))],
        out_specs=pl.BlockSpec((tm, D), lambda i: (i, 0)),
        out_shape=jax.ShapeDtypeStruct((S, D), BF16),
        compiler_params=_cparams("parallel"),
        name=name,
    )(x, g)


def _rms_bwd(dh, x, g, dres, name, with_bf16):
    S, D = x.shape
    tm = _tile(S, 256)

    def body(dh_ref, x_ref, g_ref, dres_ref, *outs):
        dx_ref, dg_ref = outs[0], outs[-1]
        xv = x_ref[...]
        r = lax.rsqrt(jnp.mean(xv * xv, axis=-1, keepdims=True) + RMS_EPS)
        xr = xv * r
        dhv = dh_ref[...].astype(F32)

        @pl.when(pl.program_id(0) == 0)
        def _():
            dg_ref[...] = jnp.zeros_like(dg_ref)

        dg_ref[...] += jnp.sum(dhv * xr, axis=0, keepdims=True)
        u = dhv * g_ref[...]
        c = jnp.mean(u * xr, axis=-1, keepdims=True)
        dx = dres_ref[...] + r * (u - xr * c)
        dx_ref[...] = dx
        if with_bf16:
            outs[1][...] = dx.astype(BF16)

    row = pl.BlockSpec((tm, D), lambda i: (i, 0))
    vec = pl.BlockSpec((1, D), lambda i: (0, 0))
    out_specs = [row] + ([row] if with_bf16 else []) + [vec]
    out_shape = [jax.ShapeDtypeStruct((S, D), F32)] + ([jax.ShapeDtypeStruct((S, D), BF16)] if with_bf16 else []) + [jax.ShapeDtypeStruct((1, D), F32)]
    return pl.pallas_call(
        body,
        grid=(S // tm,),
        in_specs=[row, row, vec, row],
        out_specs=out_specs,
        out_shape=out_shape,
        compiler_params=_cparams("arbitrary"),
        name=name,
    )(dh, x, g, dres)


def _loss_head(x3, tgt, g, name):
    S, D = x3.shape
    tm = _tile(S, 256)

    def body(x_ref, t_ref, g_ref, dx_ref, dxb_ref, dg_ref, loss_ref):
        xv = x_ref[...]
        gv = g_ref[...]
        r = lax.rsqrt(jnp.mean(xv * xv, axis=-1, keepdims=True) + RMS_EPS)
        xr = xv * r
        e = xr * gv - t_ref[...]

        @pl.when(pl.program_id(0) == 0)
        def _():
            dg_ref[...] = jnp.zeros_like(dg_ref)
            loss_ref[...] = jnp.zeros_like(loss_ref)

        loss_ref[...] += jnp.sum(e * e, axis=0, keepdims=True) * (0.5 / D)
        dy = e * (1.0 / D)
        dg_ref[...] += jnp.sum(dy * xr, axis=0, keepdims=True)
        u = dy * gv
        c = jnp.mean(u * xr, axis=-1, keepdims=True)
        dx = r * (u - xr * c)
        dx_ref[...] = dx
        dxb_ref[...] = dx.astype(BF16)

    row = pl.BlockSpec((tm, D), lambda i: (i, 0))
    vec = pl.BlockSpec((1, D), lambda i: (0, 0))
    return pl.pallas_call(
        body,
        grid=(S // tm,),
        in_specs=[row, row, vec],
        out_specs=[row, row, vec, vec],
        out_shape=[jax.ShapeDtypeStruct((S, D), F32), jax.ShapeDtypeStruct((S, D), BF16), jax.ShapeDtypeStruct((1, D), F32), jax.ShapeDtypeStruct((1, D), F32)],
        compiler_params=_cparams("arbitrary"),
        name=name,
    )(x3, tgt, g)


def _conv_taps(cur_ref, halo_ref, w_ref, b_ref, first):
    cur = cur_ref[...].astype(F32)
    halo = jnp.where(first, 0.0, halo_ref[...].astype(F32))
    xx = jnp.concatenate([halo, cur], axis=0)
    p1 = pltpu.roll(xx, 1, 0)[HALO:]
    p2 = pltpu.roll(xx, 2, 0)[HALO:]
    w = w_ref[...]
    y = b_ref[...] + w[0:1] * p2 + w[1:2] * p1 + w[2:3] * cur
    return y, (cur, p1, p2)


def _convglu_fwd(up, cw, cb, name):
    S, F2 = up.shape
    F = F2 // 2
    tm, tn = _tile(S, 512), _tile(F, 512)
    nj, hb = F // tn, tm // HALO

    def body(ua, ub, ha, hb_, wa, wb, ba, bb, f_ref, a_ref, b_ref):
        first = pl.program_id(0) == 0
        a, _ = _conv_taps(ua, ha, wa, ba, first)
        b, _ = _conv_taps(ub, hb_, wb, bb, first)
        f_ref[...] = (0.5 * a * (1.0 + lax.erf(a * INV_SQRT2)) * b).astype(f_ref.dtype)
        a_ref[...] = a.astype(a_ref.dtype)
        b_ref[...] = b.astype(b_ref.dtype)

    tile = pl.BlockSpec((tm, tn), lambda i, j: (i, j))
    return pl.pallas_call(
        body,
        grid=(S // tm, nj),
        in_specs=[
            tile,
            pl.BlockSpec((tm, tn), lambda i, j: (i, j + nj)),
            pl.BlockSpec((HALO, tn), lambda i, j: (jnp.maximum(i * hb - 1, 0), j)),
            pl.BlockSpec((HALO, tn), lambda i, j: (jnp.maximum(i * hb - 1, 0), j + nj)),
            pl.BlockSpec((3, tn), lambda i, j: (0, j)),
            pl.BlockSpec((3, tn), lambda i, j: (0, j + nj)),
            pl.BlockSpec((1, tn), lambda i, j: (0, j)),
            pl.BlockSpec((1, tn), lambda i, j: (0, j + nj)),
        ],
        out_specs=[tile, tile, tile],
        out_shape=[jax.ShapeDtypeStruct((S, F), BF16)] * 3,
        compiler_params=_cparams("parallel", "parallel"),
        name=name,
    )(up, up, up, up, cw, cw, cb, cb)


def _convglu_bwd(df, a, b, up, cw, name):
    S, F = df.shape
    tm, tn = _tile(S, 512), _tile(F, 512)
    nj, ni, hb = F // tn, S // tm, tm // HALO
    n = tm + HALO

    def body(df_ref, dfn_ref, a_ref, an_ref, b_ref, bn_ref, up_ref, w_ref, o_ref, db_ref, dw_ref):
        j, i = pl.program_id(0), pl.program_id(1)
        last = i == ni - 1

        def rows(c_ref, n_ref):
            return jnp.concatenate([c_ref[...].astype(F32), jnp.where(last, 0.0, n_ref[...].astype(F32))], axis=0)

        @pl.when(i == 0)
        def _():
            db_ref[...] = jnp.zeros_like(db_ref)
            dw_ref[...] = jnp.zeros_like(dw_ref)

        def finish(d):
            d0 = d[:tm]
            d1 = pltpu.roll(d, n - 1, 0)[:tm]
            d2 = pltpu.roll(d, n - 2, 0)[:tm]
            w = w_ref[...]
            o_ref[...] = (w[2:3] * d0 + w[1:2] * d1 + w[0:1] * d2).astype(o_ref.dtype)
            upv = up_ref[...].astype(F32)
            db_ref[...] += jnp.sum(d0, axis=0, keepdims=True)
            dw_ref[0:1, :] += jnp.sum(d2 * upv, axis=0, keepdims=True)
            dw_ref[1:2, :] += jnp.sum(d1 * upv, axis=0, keepdims=True)
            dw_ref[2:3, :] += jnp.sum(d0 * upv, axis=0, keepdims=True)

        av, dfv = rows(a_ref, an_ref), rows(df_ref, dfn_ref)
        cdf = 0.5 * (1.0 + lax.erf(av * INV_SQRT2))

        @pl.when(j < nj)
        def _():
            pdf = jnp.exp(-0.5 * av * av) * INV_SQRT_2PI
            finish(dfv * rows(b_ref, bn_ref) * (cdf + av * pdf))

        @pl.when(j >= nj)
        def _():
            finish(dfv * (av * cdf))

    jh = lambda j: lax.rem(j, nj)
    nxt = lambda i: jnp.minimum((i + 1) * hb, S // HALO - 1)
    cur = pl.BlockSpec((tm, tn), lambda j, i: (i, jh(j)))
    halo = pl.BlockSpec((HALO, tn), lambda j, i: (nxt(i), jh(j)))
    return pl.pallas_call(
        body,
        grid=(2 * nj, ni),
        in_specs=[cur, halo, cur, halo, cur, halo, pl.BlockSpec((tm, tn), lambda j, i: (i, j)), pl.BlockSpec((3, tn), lambda j, i: (0, j))],
        out_specs=[pl.BlockSpec((tm, tn), lambda j, i: (i, j)), pl.BlockSpec((1, tn), lambda j, i: (0, j)), pl.BlockSpec((3, tn), lambda j, i: (0, j))],
        out_shape=[jax.ShapeDtypeStruct((S, 2 * F), BF16), jax.ShapeDtypeStruct((1, 2 * F), F32), jax.ShapeDtypeStruct((3, 2 * F), F32)],
        compiler_params=_cparams("parallel", "arbitrary"),
        name=name,
    )(df, df, a, a, b, b, up, cw)


def _gate_bwd(dmixed, gates, y_pool, y_attn, in_width, name):
    S, D = dmixed.shape
    tm, tn = _tile(S, 512), _tile(D, 512)
    nj = D // tn
    pre0 = (in_width - 2 * D) // tn
    assert pre0 * tn == in_width - 2 * D

    def body(dm_ref, g_ref, yp_ref, ya_ref, dy_ref, dpre_ref, db_ref):
        j = pl.program_id(0)

        @pl.when(pl.program_id(1) == 0)
        def _():
            db_ref[...] = jnp.zeros_like(db_ref)

        def run(y_ref):
            dm = dm_ref[...].astype(F32)
            gv = g_ref[...]
            dy_ref[...] = (dm * gv).astype(BF16)
            dpre = dm * y_ref[...].astype(F32) * gv * (1.0 - gv)
            dpre_ref[...] = dpre.astype(BF16)
            db_ref[...] += jnp.sum(dpre, axis=0, keepdims=True)

        @pl.when(j < nj)
        def _():
            run(yp_ref)

        @pl.when(j >= nj)
        def _():
            run(ya_ref)

    tile2 = pl.BlockSpec((tm, tn), lambda j, i: (i, j))
    return pl.pallas_call(
        body,
        grid=(2 * nj, S // tm),
        in_specs=[
            pl.BlockSpec((tm, tn), lambda j, i: (i, lax.rem(j, nj))),
            tile2,
            pl.BlockSpec((tm, tn), lambda j, i: (i, jnp.minimum(j, nj - 1))),
            pl.BlockSpec((tm, tn), lambda j, i: (i, jnp.maximum(j - nj, 0))),
        ],
        out_specs=[tile2, pl.BlockSpec((tm, tn), lambda j, i: (i, pre0 + j)), pl.BlockSpec((1, tn), lambda j, i: (0, j))],
        out_shape=[jax.ShapeDtypeStruct((S, 2 * D), BF16), jax.ShapeDtypeStruct((S, in_width), BF16), jax.ShapeDtypeStruct((1, 2 * D), F32)],
        compiler_params=_cparams("parallel", "arbitrary"),
        name=name,
    )(dmixed, gates, y_pool, y_attn)


def _pool_counts(i, tm, rows, w):
    t = i * tm + lax.broadcasted_iota(jnp.int32, (rows, 1), 0)
    return jnp.minimum(t + 1, w).astype(F32)


def _pooled_groups(u_ref, uh_ref, i, tm, C):
    cur = u_ref[...]
    halo = jnp.where(i == 0, 0.0, uh_ref[...])
    xx = jnp.concatenate([halo, cur], axis=0)
    out = []
    s = xx
    for gi, w in enumerate(POOL_WINDOWS):
        s = s + pltpu.roll(s, w // 2, 0)
        tot = s[HALO:, 0:C]
        out.append(tot / _pool_counts(i, tm, tm, w) - cur[:, gi * C:(gi + 1) * C])
        s = s[:, C:] if gi + 1 < len(POOL_WINDOWS) else s
    return out


def _pool_fwd(u, wl, scale, name):
    S, PW = u.shape
    C = PW // len(POOL_WINDOWS)
    tm = _tile(S, 512)
    hb = tm // HALO

    def body(u_ref, uh_ref, wl_ref, sc_ref, o_ref):
        i = pl.program_id(0)
        pooled = _pooled_groups(u_ref, uh_ref, i, tm, C)
        for gi in range(len(POOL_WINDOWS)):
            y = _dot_nn(pooled[gi].astype(BF16), wl_ref[gi])
            o_ref[:, gi * C:(gi + 1) * C] = (y * sc_ref[:, gi * C:(gi + 1) * C]).astype(o_ref.dtype)

    return pl.pallas_call(
        body,
        grid=(S // tm,),
        in_specs=[
            pl.BlockSpec((tm, PW), lambda i: (i, 0)),
            pl.BlockSpec((HALO, PW), lambda i: (jnp.maximum(i * hb - 1, 0), 0)),
            pl.BlockSpec((len(POOL_WINDOWS), C, C), lambda i: (0, 0, 0)),
            pl.BlockSpec((1, PW), lambda i: (0, 0)),
        ],
        out_specs=pl.BlockSpec((tm, PW), lambda i: (i, 0)),
        out_shape=jax.ShapeDtypeStruct((S, PW), BF16),
        compiler_params=_cparams("parallel"),
        name=name,
    )(u, u, wl, scale)


def _pool_bwd(u, dp, wl, scale, dproj, name):
    S, PW = u.shape
    G = len(POOL_WINDOWS)
    C = PW // G
    tm = _tile(S, 512)
    hb, ni = tm // HALO, S // tm
    n = tm + HALO

    def body(u_ref, uh_ref, dp_ref, dpn_ref, wl_ref, sc_ref, _, du_ref, dwl_ref, dsc_ref):
        i = pl.program_id(0)

        @pl.when(i == 0)
        def _():
            dwl_ref[...] = jnp.zeros_like(dwl_ref)
            dsc_ref[...] = jnp.zeros_like(dsc_ref)

        pooled = _pooled_groups(u_ref, uh_ref, i, tm, C)
        dpc = dp_ref[...].astype(F32)
        dpn = jnp.where(i == ni - 1, 0.0, dpn_ref[...].astype(F32))
        sc = sc_ref[...]
        dyl = jnp.concatenate([dpc, dpn], axis=0) * sc
        for gi, w in enumerate(POOL_WINDOWS):
            cols = slice(gi * C, (gi + 1) * C)
            pb = pooled[gi].astype(BF16)
            ylin = _dot_nn(pb, wl_ref[gi])
            dsc_ref[:, cols] += jnp.sum(dpc[:, cols] * ylin, axis=0, keepdims=True)
            dylg = dyl[:, cols].astype(BF16)
            dwl_ref[gi] += _dot(pb, dylg[:tm], ((0,), (0,)))
            dpool = _dot_nt(dylg, wl_ref[gi])
            e = dpool / _pool_counts(i, tm, n, w)
            k = 1
            while k < w:
                e = e + pltpu.roll(e, n - k, 0)
                k *= 2
            du_ref[:, cols] = (e[:tm] - dpool[:tm]).astype(du_ref.dtype)

    return pl.pallas_call(
        body,
        grid=(ni,),
        in_specs=[
            pl.BlockSpec((tm, PW), lambda i: (i, 0)),
            pl.BlockSpec((HALO, PW), lambda i: (jnp.maximum(i * hb - 1, 0), 0)),
            pl.BlockSpec((tm, PW), lambda i: (i, 0)),
            pl.BlockSpec((HALO, PW), lambda i: (jnp.minimum((i + 1) * hb, S // HALO - 1), 0)),
            pl.BlockSpec((G, C, C), lambda i: (0, 0, 0)),
            pl.BlockSpec((1, PW), lambda i: (0, 0)),
            ANY,
        ],
        out_specs=[pl.BlockSpec((tm, PW), lambda i: (i, 0)), pl.BlockSpec((G, C, C), lambda i: (0, 0, 0)), pl.BlockSpec((1, PW), lambda i: (0, 0))],
        out_shape=[jax.ShapeDtypeStruct(dproj.shape, dproj.dtype), jax.ShapeDtypeStruct((G, C, C), F32), jax.ShapeDtypeStruct((1, PW), F32)],
        input_output_aliases={6: 0},
        compiler_params=_cparams("arbitrary"),
        name=name,
    )(u, u, dp, dp, wl, scale, dproj)


def _band_masks():
    ii = lax.broadcasted_iota(jnp.int32, (SPAN, SPAN), 0)
    kk = lax.broadcasted_iota(jnp.int32, (SPAN, SPAN), 1)
    return ((ii + SPAN - kk).astype(F32), kk >= ii), ((ii - kk).astype(F32), kk <= ii)


ATTN_TILE = 16 * SPAN


def _unit_rows(r, b, d):
    return pl.ds(d * SPAN * b + r, SPAN, stride=d) if d > 1 else pl.ds(SPAN * b, SPAN)


def _f32_copies(refs, scratch, d):
    if d == 1:
        return list(refs)
    for ref, s in zip(refs, scratch):
        s[...] = ref[...].astype(F32)
    return list(scratch)


def _attn_fwd(qkv, d, g, name):
    S = qkv.shape[0]
    T = min(ATTN_TILE, S)
    P = SPAN * d
    nbk = T // P

    def body(q_ref, k_ref, v_ref, kp_ref, vp_ref, o_ref, lse_ref, *scratch):
        c = pl.program_id(0)
        (jp, mp), (jc, mc) = _band_masks()
        slopes = [ALIBI_SLOPES[g * HEADS_PER_GROUP + h] * d for h in range(HEADS_PER_GROUP)]
        slope = slopes[0]
        for h in range(1, HEADS_PER_GROUP):
            slope = jnp.where(pl.program_id(1) == h, slopes[h], slope)
        q_s, k_s, v_s, kp_s, vp_s = _f32_copies((q_ref, k_ref, v_ref, kp_ref, vp_ref), scratch[:5], d)
        o_s, l_s = (o_ref, lse_ref) if d == 1 else scratch[5:7]
        for r in range(d):
            for b in range(nbk):
                rows = _unit_rows(r, b, d)
                q = q_s[rows, :].astype(BF16)
                kc, vc = k_s[rows, :].astype(BF16), v_s[rows, :].astype(BF16)
                if b == 0:
                    prev = _unit_rows(r, 0, d)
                    kp, vp, okp = kp_s[prev, :].astype(BF16), vp_s[prev, :].astype(BF16), jnp.logical_and(mp, c > 0)
                else:
                    prev = _unit_rows(r, b - 1, d)
                    kp, vp, okp = k_s[prev, :].astype(BF16), v_s[prev, :].astype(BF16), mp
                sc = jnp.where(mc, _dot_nt(q, kc) * ATTN_SCALE - slope * jc, NEG_BIG)
                sp = jnp.where(okp, _dot_nt(q, kp) * ATTN_SCALE - slope * jp, NEG_BIG)
                m = jnp.maximum(jnp.max(sc, axis=-1, keepdims=True), jnp.max(sp, axis=-1, keepdims=True))
                pc, pp = jnp.exp(sc - m), jnp.exp(sp - m)
                l = jnp.sum(pc, axis=-1, keepdims=True) + jnp.sum(pp, axis=-1, keepdims=True)
                o_s[rows, :] = (_dot_nn(pc.astype(BF16), vc) + _dot_nn(pp.astype(BF16), vp)) / l
                l_s[rows, :] = jnp.broadcast_to(m + jnp.log(l), (SPAN, HEAD_DIM))
        if d > 1:
            o_ref[...] = o_s[...]
            lse_ref[...] = l_s[...]

    col = lambda kind: (lambda c, h: (c, kind * N_ATTN_HEADS + g * HEADS_PER_GROUP + h))
    pcol = lambda kind: (lambda c, h: (jnp.maximum(c * nbk - 1, 0), kind * N_ATTN_HEADS + g * HEADS_PER_GROUP + h))
    cur = lambda kind: pl.BlockSpec((T, HEAD_DIM), col(kind))
    prv = lambda kind: pl.BlockSpec((P, HEAD_DIM), pcol(kind))
    out = pl.BlockSpec((T, HEAD_DIM), lambda c, h: (c, h))
    scratch = [] if d == 1 else [pltpu.VMEM((T, HEAD_DIM), F32)] * 3 + [pltpu.VMEM((P, HEAD_DIM), F32)] * 2 + [pltpu.VMEM((T, HEAD_DIM), F32)] * 2
    return pl.pallas_call(
        body,
        grid=(S // T, HEADS_PER_GROUP),
        in_specs=[cur(0), cur(1), cur(2), prv(1), prv(2)],
        out_specs=[out, out],
        out_shape=[jax.ShapeDtypeStruct((S, GROUP_WIDTH), F32)] * 2,
        scratch_shapes=scratch,
        compiler_params=_cparams("parallel", "parallel"),
        name=name,
    )(qkv, qkv, qkv, qkv, qkv)


def _attn_merge(os_, lses, name):
    S, W = os_[0].shape
    tm = _tile(S, 512)

    def body(o0, o1, o2, l0, l1, l2, y_ref, lse_ref):
        ls = [l0[...], l1[...], l2[...]]
        m = jnp.maximum(jnp.maximum(ls[0], ls[1]), ls[2])
        es = [jnp.exp(v - m) for v in ls]
        tot = es[0] + es[1] + es[2]
        y = (es[0] * o0[...] + es[1] * o1[...] + es[2] * o2[...]) / tot
        y_ref[...] = y.astype(y_ref.dtype)
        lse_ref[...] = m + jnp.log(tot)

    row = pl.BlockSpec((tm, W), lambda i: (i, 0))
    return pl.pallas_call(
        body,
        grid=(S // tm,),
        in_specs=[row] * 6,
        out_specs=[row, row],
        out_shape=[jax.ShapeDtypeStruct((S, W), BF16), jax.ShapeDtypeStruct((S, W), F32)],
        compiler_params=_cparams("parallel"),
        name=name,
    )(*os_, *lses)


def _attn_bwd(qkv, dattn, y, lse, dproj, d, g, col0, name):
    S = qkv.shape[0]
    T = min(ATTN_TILE, S)
    P = SPAN * d
    nbk = T // P
    ntile = S // T

    def body(q_ref, k_ref, v_ref, kp_ref, vp_ref, qn_ref, da_ref, dan_ref, y_ref, yn_ref, lse_ref, lsen_ref, _, out_ref, dq_s, dk_s, dv_s, *scratch):
        c = pl.program_id(0)
        head_id = pl.program_id(1)
        kind = pl.program_id(2)

        @pl.when(kind == 0)
        def _():
            (jp, mp), (jc, mc) = _band_masks()
            slopes = [ALIBI_SLOPES[g * HEADS_PER_GROUP + h] * d for h in range(HEADS_PER_GROUP)]
            slope = slopes[0]
            for h in range(1, HEADS_PER_GROUP):
                slope = jnp.where(head_id == h, slopes[h], slope)
            q_s, k_s, v_s, da_s, y_s, kp_s, vp_s, qn_s, dan_s, yn_s = _f32_copies(
                (q_ref, k_ref, v_ref, da_ref, y_ref, kp_ref, vp_ref, qn_ref, dan_ref, yn_ref), scratch, d)
            for r in range(d):
                dq = [None] * nbk
                dk = [None] * nbk
                dv = [None] * nbk

                def add(lst, idx, val):
                    lst[idx] = val if lst[idx] is None else lst[idx] + val

                for qb in range(nbk + 1):
                    if qb < nbk:
                        rows = _unit_rows(r, qb, d)
                        q, da, yy, lse_blk = q_s[rows, :], da_s[rows, :], y_s[rows, :], lse_ref[rows, :]
                    else:
                        rows = _unit_rows(r, 0, d)
                        q, da, yy, lse_blk = qn_s[rows, :], dan_s[rows, :], yn_s[rows, :], lsen_ref[rows, :]
                    lse_col = lse_blk[:, 0:1]
                    dd = jnp.sum(da.astype(F32) * yy.astype(F32), axis=-1, keepdims=True)
                    q, da = q.astype(BF16), da.astype(BF16)
                    for kb in (qb - 1, qb):
                        if kb >= nbk:
                            continue
                        if kb < 0:
                            krows = _unit_rows(r, 0, d)
                            kk, vv, ok = kp_s[krows, :].astype(BF16), vp_s[krows, :].astype(BF16), jnp.logical_and(mp, c > 0)
                        else:
                            krows = _unit_rows(r, kb, d)
                            kk, vv = k_s[krows, :].astype(BF16), v_s[krows, :].astype(BF16)
                            ok = mc if kb == qb else (mp if qb < nbk else jnp.logical_and(mp, c < ntile - 1))
                        jj = jc if kb == qb else jp
                        s = jnp.where(ok, _dot_nt(q, kk) * ATTN_SCALE - slope * jj, NEG_BIG)
                        p = jnp.exp(s - lse_col)
                        ds = p * (_dot_nt(da, vv) - dd)
                        if qb < nbk:
                            add(dq, qb, _dot_nn(ds.astype(BF16), kk))
                        if kb >= 0:
                            add(dv, kb, _dot_nn(p.T.astype(BF16), da))
                            add(dk, kb, _dot_nn(ds.T.astype(BF16), q))
                for b in range(nbk):
                    rows = _unit_rows(r, b, d)
                    dq_s[rows, :] = dq[b] * ATTN_SCALE
                    dk_s[rows, :] = dk[b] * ATTN_SCALE
                    dv_s[rows, :] = dv[b]
            out_ref[...] = dq_s[...].astype(out_ref.dtype)

        @pl.when(kind == 1)
        def _():
            out_ref[...] = dk_s[...].astype(out_ref.dtype)

        @pl.when(kind == 2)
        def _():
            out_ref[...] = dv_s[...].astype(out_ref.dtype)

    head = lambda h: g * HEADS_PER_GROUP + h
    cur = lambda kind: pl.BlockSpec((T, HEAD_DIM), lambda c, h, kd: (c, kind * N_ATTN_HEADS + head(h)))
    prv = lambda kind: pl.BlockSpec((P, HEAD_DIM), lambda c, h, kd: (jnp.maximum(c * nbk - 1, 0), kind * N_ATTN_HEADS + head(h)))
    nxt_row = lambda c: jnp.minimum((c + 1) * nbk, S // P - 1)
    qnext = pl.BlockSpec((P, HEAD_DIM), lambda c, h, kd: (nxt_row(c), head(h)))
    hcur = pl.BlockSpec((T, HEAD_DIM), lambda c, h, kd: (c, h))
    hnext = pl.BlockSpec((P, HEAD_DIM), lambda c, h, kd: (nxt_row(c), h))
    out = pl.BlockSpec((T, HEAD_DIM), lambda c, h, kd: (c, col0 + kd * N_ATTN_HEADS + head(h)))
    stage = [pltpu.VMEM((T, HEAD_DIM), F32)] * 3
    copies = [] if d == 1 else [pltpu.VMEM((T, HEAD_DIM), F32)] * 5 + [pltpu.VMEM((P, HEAD_DIM), F32)] * 5
    return pl.pallas_call(
        body,
        grid=(ntile, HEADS_PER_GROUP, 3),
        in_specs=[cur(0), cur(1), cur(2), prv(1), prv(2), qnext, hcur, hnext, hcur, hnext, hcur, hnext, ANY],
        out_specs=out,
        out_shape=jax.ShapeDtypeStruct(dproj.shape, dproj.dtype),
        input_output_aliases={12: 0},
        scratch_shapes=stage + copies,
        compiler_params=_cparams("parallel", "parallel", "arbitrary"),
        name=name,
    )(qkv, qkv, qkv, qkv, qkv, qkv, dattn, dattn, y, y, lse, lse, dproj)


def _row_block(R, C, bytes_per_row_elem=4, budget=1 << 20):
    if R % 8:
        return R
    best = 8
    t = 8
    while t <= R:
        if R % t == 0 and t * C * bytes_per_row_elem <= budget:
            best = t
        t += 8
    return best


def _adamw(w, g, m, v, name):
    R, C = w.shape
    tr = _row_block(R, C)
    c1 = 1.0 - ADAM_B1 ** ADAM_STEP
    c2 = 1.0 - ADAM_B2 ** ADAM_STEP

    def body(w_ref, g_ref, m_ref, v_ref, d_ref, nm_ref, nv_ref):
        gv = g_ref[...]
        nm = ADAM_B1 * m_ref[...] + (1.0 - ADAM_B1) * gv
        nv = ADAM_B2 * v_ref[...] + (1.0 - ADAM_B2) * (gv * gv)
        d_ref[...] = -ADAM_LR * ((nm / c1) / (jnp.sqrt(nv / c2) + ADAM_EPS) + ADAM_WD * w_ref[...])
        nm_ref[...] = nm
        nv_ref[...] = nv

    blk = pl.BlockSpec((tr, C), lambda i: (i, 0))
    return pl.pallas_call(
        body,
        grid=(R // tr,),
        in_specs=[blk] * 4,
        out_specs=[blk] * 3,
        out_shape=[jax.ShapeDtypeStruct((R, C), F32)] * 3,
        compiler_params=_cparams("parallel"),
        name=name,
    )(w, g, m, v)


def _sum_pieces(grad, axis, recv, pos, name):
    n, pr, pc = recv.shape
    tr = _row_block(pr, pc, bytes_per_row_elem=(n + 1) * recv.dtype.itemsize, budget=4 << 20)
    nblk = pr // tr
    if axis == 1:
        own_map = lambda i, p: (p[1] * nblk + i, p[0])
    else:
        own_map = lambda i, p: ((2 * p[0] + p[1]) * nblk + i, 0)

    def body(p_ref, own_ref, r_ref, o_ref):
        acc = own_ref[...].astype(F32)
        for s in range(n):
            acc = acc + r_ref[s].astype(F32)
        o_ref[...] = acc

    return pl.pallas_call(
        body,
        grid_spec=pltpu.PrefetchScalarGridSpec(
            num_scalar_prefetch=1,
            grid=(nblk,),
            in_specs=[pl.BlockSpec((tr, pc), own_map), pl.BlockSpec((n, tr, pc), lambda i, p: (0, i, 0))],
            out_specs=pl.BlockSpec((tr, pc), lambda i, p: (p[1] * nblk + i, 0)),
        ),
        out_shape=jax.ShapeDtypeStruct((2 * pr, pc), F32),
        compiler_params=_cparams("parallel"),
        name=name,
    )(pos, grad, recv)


def _sum_small(own, recv, me, name):
    n, R, C = recv.shape
    tr = _row_block(R, C, bytes_per_row_elem=(n + 1) * 4, budget=4 << 20)

    def body(me_ref, own_ref, r_ref, o_ref):
        acc = None
        for dev in range(n + 1):
            k = jnp.bitwise_xor(me_ref[0], dev)
            term = jnp.where(k == 0, own_ref[...], r_ref[jnp.maximum(k - 1, 0)])
            acc = term if acc is None else acc + term
        o_ref[...] = acc

    return pl.pallas_call(
        body,
        grid_spec=pltpu.PrefetchScalarGridSpec(
            num_scalar_prefetch=1,
            grid=(R // tr,),
            in_specs=[pl.BlockSpec((tr, C), lambda i, m: (i, 0)), pl.BlockSpec((n, tr, C), lambda i, m: (0, i, 0))],
            out_specs=pl.BlockSpec((tr, C), lambda i, m: (i, 0)),
        ),
        out_shape=jax.ShapeDtypeStruct((R, C), F32),
        compiler_params=_cparams("parallel"),
        name=name,
    )(me, own, recv)


def _place(shard, axis, pos, dtype, name):
    shp = list(shard.shape)
    shp[axis] *= N_CHIPS
    if shard.ndim == 3:
        assert axis == 1
        in_spec = pl.BlockSpec(shard.shape, lambda i, p: (0, 0, 0))
        out_spec = pl.BlockSpec(shard.shape, lambda i, p: (0, p[0], 0))
        grid = (1,)
    else:
        R, C = shard.shape
        tr = _row_block(R, C, bytes_per_row_elem=4, budget=2 << 20)
        nblk = R // tr
        in_spec = pl.BlockSpec((tr, C), lambda i, p: (i, 0))
        out_spec = pl.BlockSpec((tr, C), (lambda i, p: (i, p[0])) if axis == 1 else (lambda i, p: (p[0] * nblk + i, 0)))
        grid = (nblk,)

    def body(p_ref, s_ref, o_ref):
        o_ref[...] = s_ref[...].astype(o_ref.dtype)

    return pl.pallas_call(
        body,
        grid_spec=pltpu.PrefetchScalarGridSpec(num_scalar_prefetch=1, grid=grid, in_specs=[in_spec], out_specs=out_spec),
        out_shape=jax.ShapeDtypeStruct(tuple(shp), dtype),
        compiler_params=_cparams("parallel"),
        name=name,
    )(pos, shard)


HBM = pl.BlockSpec(memory_space=pltpu.HBM)
SEM = pl.BlockSpec(memory_space=pltpu.SEMAPHORE)
DATAFLOW = pltpu.SideEffectType.DATAFLOW_SIDE_EFFECTING


def _position():
    return lax.axis_index("x"), lax.axis_index("y"), lax.axis_index("c")


def _peer(k):
    x, y, c = _position()
    return ((1 - x) if k & 4 else x, (1 - y) if k & 2 else y, (1 - c) if k & 1 else c)


def _shard_slice(ref, axis, idx, size):
    start = idx * size
    if axis == ref.ndim - 1:
        start = pl.multiple_of(start, 128)
    ix = [slice(None)] * ref.ndim
    ix[axis] = pl.ds(start, size)
    return ref.at[tuple(ix)]


def _gather_plan(axes):
    def plan(refs):
        x, y, c = _position()
        out = []
        for ref, ax in zip(refs, axes):
            mine = _shard_slice(ref, ax, 2 * x + y, ref.shape[ax] // N_CHIPS)
            for k in (4, 2, 6):
                px, py, _ = _peer(k)
                out.append((mine, mine, (px, py, c)))
        return out
    return plan


def _scatter_plan(axes):
    m = len(axes)

    def plan(refs):
        out = []
        for t in range(m):
            grad, recv = refs[t], refs[m + t]
            _, pr, pc = recv.shape
            for k in range(1, N_DEV):
                px, py, pcore = _peer(k)
                if axes[t] == 0:
                    piece = grad.at[pl.ds(((2 * px + py) * 2 + pcore) * pr, pr), :]
                else:
                    piece = grad.at[pl.ds(pcore * pr, pr), pl.ds(pl.multiple_of((2 * px + py) * pc, 128), pc)]
                out.append((piece, recv.at[k - 1], (px, py, pcore)))
        return out
    return plan


def _broadcast_plan(refs):
    small, recv = refs
    return [(small, recv.at[k - 1], _peer(k)) for k in range(1, N_DEV)]


def _start_all(plan, refs, send_sems, recv_sems):
    for q, (src, dst, dev) in enumerate(plan(refs)):
        pltpu.make_async_remote_copy(src_ref=src, dst_ref=dst, send_sem=send_sems.at[q], recv_sem=recv_sems.at[q], device_id=dev, device_id_type=MESH).start()


def _wait_all(plan, refs, send_sems, recv_sems):
    for q, (src, dst, dev) in enumerate(plan(refs)):
        cp = pltpu.make_async_remote_copy(src_ref=src, dst_ref=dst, send_sem=send_sems.at[q], recv_sem=recv_sems.at[q], device_id=dev, device_id_type=MESH)
        cp.wait_send()
        cp.wait_recv()


def _push(bufs, plan, ncopies, name):
    n = len(bufs)

    def body(*refs):
        outs = refs[n:2 * n]
        send_sems, recv_sems = refs[2 * n:]
        _start_all(plan, outs, send_sems, recv_sems)
        _wait_all(plan, outs, send_sems, recv_sems)

    return pl.pallas_call(
        body,
        in_specs=[ANY] * n,
        out_specs=[ANY] * n,
        out_shape=[jax.ShapeDtypeStruct(b.shape, b.dtype) for b in bufs],
        input_output_aliases={t: t for t in range(n)},
        scratch_shapes=[pltpu.SemaphoreType.DMA((ncopies,)), pltpu.SemaphoreType.DMA((ncopies,))],
        name=name,
    )(*bufs)


def _gather_once_per_chip(full, name):
    R, C = full.shape
    R2, C4 = R // 2, C // N_CHIPS

    def body(_, ref, send_sems, recv_sems):
        x, y, c = _position()
        chips = [_peer(k)[:2] for k in (4, 2, 6)]

        def half(chip, core):
            return ref.at[pl.ds(core * R2, R2), pl.ds(pl.multiple_of(chip * C4, 128), C4)]

        def copy(q, chip, core, to):
            return pltpu.make_async_remote_copy(src_ref=half(chip, core), dst_ref=half(chip, core), send_sem=send_sems.at[q], recv_sem=recv_sems.at[q],
                                                device_id=to, device_id_type=MESH)

        sends = [copy(q, 2 * x + y, c, (px, py, c)) for q, (px, py) in enumerate(chips)]
        for cp in sends:
            cp.start()
        for q, (px, py) in enumerate(chips):
            copy(q, 2 * px + py, c, (px, py, c)).wait_recv()
            passed = copy(3 + q, 2 * px + py, c, (x, y, 1 - c))
            passed.start()
            sends.append(passed)
        for q, (px, py) in enumerate(chips):
            copy(3 + q, 2 * px + py, 1 - c, (x, y, 1 - c)).wait_recv()
        for cp in sends:
            cp.wait_send()

    return pl.pallas_call(
        body,
        in_specs=[ANY],
        out_specs=ANY,
        out_shape=jax.ShapeDtypeStruct(full.shape, full.dtype),
        input_output_aliases={0: 0},
        scratch_shapes=[pltpu.SemaphoreType.DMA((6,)), pltpu.SemaphoreType.DMA((6,))],
        name=name,
    )(full)


def _push_start(bufs, plan, ncopies, name, after=None):
    n = len(bufs)
    extra = [] if after is None else [after]

    def body(*refs):
        ins = refs[:n]
        first_out = n + len(extra)
        send_sems, recv_sems, token = refs[first_out], refs[first_out + 1], refs[-1]
        _start_all(plan, ins, send_sems, recv_sems)
        token[...] = jnp.zeros_like(token)

    res = pl.pallas_call(
        body,
        name=name,
        out_shape=(pltpu.SemaphoreType.DMA((ncopies,)), pltpu.SemaphoreType.DMA((ncopies,)), *[pltpu.HBM(b.shape, b.dtype) for b in bufs],
                   jax.ShapeDtypeStruct((8, 128), F32)),
        in_specs=[HBM] * n + [ANY] * len(extra),
        out_specs=(SEM, SEM, *[HBM] * n, pl.BlockSpec(memory_space=pltpu.VMEM)),
        input_output_aliases={t: t + 2 for t in range(n)},
        compiler_params=pltpu.CompilerParams(has_side_effects=DATAFLOW),
    )(*[pltpu.with_memory_space_constraint(b, pltpu.HBM) for b in bufs], *extra)
    return res[0], res[1], list(res[2:2 + n]), res[-1]


def _push_wait(send_sems, recv_sems, bufs, plan, after, name):
    n = len(bufs)

    def body(*refs):
        ins = refs[:n]
        _wait_all(plan, ins, refs[n], refs[n + 1])

    return pl.pallas_call(
        body,
        name=name,
        out_shape=tuple(pltpu.HBM(b.shape, b.dtype) for b in bufs),
        in_specs=[HBM] * n + [SEM, SEM, ANY],
        out_specs=tuple([HBM] * n),
        input_output_aliases={t: t for t in range(n)},
        compiler_params=pltpu.CompilerParams(has_side_effects=DATAFLOW),
    )(*bufs, send_sems, recv_sems, after)


EXCHANGE_CHUNKS = 2


def _exchange_plan(refs):
    x, y, c = _position()
    out = []
    for ref in refs:
        rows = ref.shape[0] // (2 * EXCHANGE_CHUNKS)
        for q in range(EXCHANGE_CHUNKS):
            mine = ref.at[pl.ds((c * EXCHANGE_CHUNKS + q) * rows, rows), :]
            out.append((mine, mine, (x, y, 1 - c)))
    return out


LATE_WEIGHTS = (("w_pool_lin", "w_pool_out", "w_attn_out", "w_out"), ("w_up", "conv_w", "w_down"))


def _local_step(x, tgt, w, late_weights, send):
    S, D = x.shape
    PW = w["pool_scale"].shape[1]
    o_q = PW
    o_g = PW + 3 * ATTN_WIDTH
    QKV = 3 * ATTN_WIDTH

    h1 = _rms_fwd(x, w["g_mix"], "rms1")
    proj_tiles = (_tile(S, 1024), 512, D)
    u = _mm(h1, w["w_in"], mode="nn", dims=(S, PW, D), tiles=proj_tiles, out_dtypes=(F32,), name="proj_u")
    qkv = _mm(h1, w["w_in"], mode="nn", dims=(S, QKV, D), tiles=proj_tiles, b_off=(0, o_q), name="proj_qkv")

    def gate_epilogue(acc, ex, outs):
        outs[0][...] = 1.0 / (1.0 + jnp.exp(-(acc + ex[0][...])))

    gates = _mm(h1, w["w_in"], mode="nn", dims=(S, 2 * D, D), tiles=proj_tiles, b_off=(0, o_g), out_dtypes=(F32,), epilogue=gate_epilogue,
                extras=[(w["b_gate"], "n", (0, 0))], name="proj_gates")

    os_, lses = [], []
    for gi, (_, d) in enumerate(ATTN_GROUPS):
        o, lse = _attn_fwd(qkv, d, gi, f"attn_fwd{gi}")
        os_.append(o)
        lses.append(lse)
    attn, lse_tot = _attn_merge(os_, lses, "attn_merge")

    w = dict(w, **late_weights(0, attn))
    pool_out = _pool_fwd(u, w["w_pool_lin"], w["pool_scale"], "pool_fwd")
    y_pool = _mm(pool_out, w["w_pool_out"], mode="nn", dims=(S, D, PW), name="y_pool")

    def mix_epilogue(acc, ex, outs):
        outs[0][...] = acc.astype(BF16)
        outs[1][...] = (ex[0][...] * ex[2][...].astype(F32) + ex[1][...] * acc).astype(BF16)

    y_attn, mixed = _mm(attn, w["w_attn_out"], mode="nn", dims=(S, D, GROUP_WIDTH), out_dtypes=(BF16, BF16), epilogue=mix_epilogue,
                        extras=[(gates, "mn", (0, 0)), (gates, "mn", (0, D)), (y_pool, "mn", (0, 0))], name="y_attn_mix")

    def residual_epilogue(acc, ex, outs):
        outs[0][...] = ex[0][...] + acc

    x2 = _mm(mixed, w["w_out"], mode="nn", dims=(S, D, D), out_dtypes=(F32,), epilogue=residual_epilogue, extras=[(x, "mn", (0, 0))], name="out_proj")

    h2 = _rms_fwd(x2, w["g_ffn"], "rms2")
    w = dict(w, **late_weights(1, h2))
    F = w["w_down"].shape[0]
    up = _mm(h2, w["w_up"], mode="nn", dims=(S, 2 * F, D), name="up_proj")
    f, act_a, act_b = _convglu_fwd(up, w["conv_w"], w["conv_b"], "convglu_fwd")
    x3 = _mm(f, w["w_down"], mode="nn", dims=(S, D, F), out_dtypes=(F32,), epilogue=residual_epilogue, extras=[(x2, "mn", (0, 0))], name="down_proj")

    g = {}
    dx3, dx3b, g["g_final"], loss_cols = _loss_head(x3, tgt, w["g_final"], "loss_head")

    g["w_down"] = _mm(f, dx3b, mode="tn", dims=(F, D, S), name="dw_down")
    sent = send(("w_down",), g)
    df = _mm(dx3b, w["w_down"], mode="nt", dims=(S, F, D), name="d_f")
    dup, g["conv_b"], g["conv_w"] = _convglu_bwd(df, act_a, act_b, up, w["conv_w"] + sent, "convglu_bwd")
    g["w_up"] = _mm(h2, dup, mode="tn", dims=(D, 2 * F, S), name="dw_up")
    sent = send(("w_up",), g)
    dh2 = _mm(dup, w["w_up"], mode="nt", dims=(S, D, 2 * F), name="d_h2")
    dx2, dx2b, g["g_ffn"] = _rms_bwd(dh2, x2, w["g_ffn"] + sent, dx3, "rms2_bwd", True)

    g["w_out"] = _mm(mixed, dx2b, mode="tn", dims=(D, D, S), name="dw_out")
    dmixed = _mm(dx2b, w["w_out"], mode="nt", dims=(S, D, D), name="d_mixed")
    IN = w["w_in"].shape[1]
    dy_both, dproj, g["b_gate"] = _gate_bwd(dmixed, gates, y_pool, y_attn, IN, "gate_bwd")

    g["w_pool_out"] = _mm(pool_out, dy_both, mode="tn", dims=(PW, D, S), name="dw_pool_out")
    g["w_attn_out"] = _mm(attn, dy_both, mode="tn", dims=(GROUP_WIDTH, D, S), b_off=(0, D), name="dw_attn_out")
    sent = send(("w_out", "w_pool_out", "w_attn_out"), g)
    dpool = _mm(dy_both, w["w_pool_out"], mode="nt", dims=(S, PW, D), name="d_pool")
    dattn = _mm(dy_both, w["w_attn_out"], mode="nt", dims=(S, GROUP_WIDTH, D), a_off=(0, D), name="d_attn")

    dproj, g["w_pool_lin"], g["pool_scale"] = _pool_bwd(u, dpool, w["w_pool_lin"], w["pool_scale"] + sent, dproj, "pool_bwd")
    g["loss_cols"] = loss_cols
    sent = send("small", g)

    for gi, (_, d) in enumerate(ATTN_GROUPS):
        dproj = _attn_bwd(qkv, dattn, attn, lse_tot, dproj, d, gi, PW // HEAD_DIM, f"attn_bwd{gi}")

    g["w_in"] = _mm(h1, dproj, mode="tn", dims=(D, IN, S), name="dw_in")
    sent = sent + send(("w_in",), g)
    dh1 = _mm(dproj, w["w_in"], mode="nt", dims=(S, D, IN), tiles=(_tile(S, 1024), _tile(D, 2048), _tile(IN, 2432)), name="d_h1")
    (grad_x, g["g_mix"]) = _rms_bwd(dh1, x, w["g_mix"] + sent, dx2, "rms1_bwd", False)
    return loss_cols, grad_x, g


BIG = ("w_in", "w_pool_out", "w_attn_out", "w_out", "w_up", "w_down")
BIG_AXIS = {"w_in": 1, "w_pool_out": 1, "w_attn_out": 1, "w_out": 0, "w_up": 1, "w_down": 0}
GATHER_AXIS = dict(BIG_AXIS, w_pool_lin=1, conv_w=1)
SMALL = ("loss_cols", "b_gate", "w_pool_lin", "pool_scale", "g_ffn", "conv_w", "conv_b", "g_final")
SMALL_COLS = 1024
ORDER = ("g_mix", "w_in", "b_gate", "w_pool_lin", "pool_scale", "w_pool_out", "w_attn_out", "w_out", "g_ffn", "w_up", "conv_w", "conv_b", "w_down", "g_final")


def _as_rows(parts):
    flat = jnp.concatenate([p.astype(F32).reshape(-1) for p in parts])
    rows = -(-flat.shape[0] // (8 * SMALL_COLS)) * 8
    return jnp.pad(flat, (0, rows * SMALL_COLS - flat.shape[0])).reshape(rows, SMALL_COLS)


def kernel(x, g_mix, w_in, b_gate, w_pool_lin, pool_scale, w_pool_out, w_attn_out, w_out, g_ffn, w_up, conv_w, conv_b, w_down, g_final, loss_target, m_g_mix, m_w_in, m_b_gate, m_w_pool_lin, m_pool_scale, m_w_pool_out, m_w_attn_out, m_w_out, m_g_ffn, m_w_up, m_conv_w, m_conv_b, m_w_down, m_g_final, v_g_mix, v_w_in, v_b_gate, v_w_pool_lin, v_pool_scale, v_w_pool_out, v_w_attn_out, v_w_out, v_g_ffn, v_w_up, v_conv_w, v_conv_b, v_w_down, v_g_final):
    shard = dict(g_mix=g_mix, w_in=w_in, b_gate=b_gate, w_pool_lin=w_pool_lin, pool_scale=pool_scale, w_pool_out=w_pool_out, w_attn_out=w_attn_out,
                 w_out=w_out, g_ffn=g_ffn, w_up=w_up, conv_w=conv_w, conv_b=conv_b, w_down=w_down, g_final=g_final)
    mom = dict(g_mix=m_g_mix, w_in=m_w_in, b_gate=m_b_gate, w_pool_lin=m_w_pool_lin, pool_scale=m_pool_scale, w_pool_out=m_w_pool_out, w_attn_out=m_w_attn_out,
               w_out=m_w_out, g_ffn=m_g_ffn, w_up=m_w_up, conv_w=m_conv_w, conv_b=m_conv_b, w_down=m_w_down, g_final=m_g_final)
    vel = dict(g_mix=v_g_mix, w_in=v_w_in, b_gate=v_b_gate, w_pool_lin=v_w_pool_lin, pool_scale=v_pool_scale, w_pool_out=v_w_pool_out, w_attn_out=v_w_attn_out,
               w_out=v_w_out, g_ffn=v_g_ffn, w_up=v_w_up, conv_w=v_conv_w, conv_b=v_conv_b, w_down=v_w_down, g_final=v_g_final)
    chip = 2 * lax.axis_index("x") + lax.axis_index("y")
    pos = jnp.stack([chip, lax.axis_index("c")]).astype(jnp.int32)
    me = (2 * chip + lax.axis_index("c")).astype(jnp.int32).reshape(1)
    D = x.shape[2]

    placed = {k: _place(shard[k][0], GATHER_AXIS[k], pos, F32 if k == "conv_w" else BF16, f"place_{k}") for k in GATHER_AXIS}
    w_in_full = _gather_once_per_chip(placed["w_in"], "comm_gather_w_in")
    late, late_token, prior = [], 0.0, w_in_full
    for stage, names in enumerate(LATE_WEIGHTS):
        plan = _gather_plan([GATHER_AXIS[k] for k in names])
        send_sems, recv_sems, bufs, token = _push_start([placed[k] for k in names], plan, 3 * len(names), f"comm_gather_late{stage}_start", after=prior)
        late.append((names, send_sems, recv_sems, bufs, plan))
        late_token, prior = late_token + token[0, 0], token

    def late_weights(stage, after):
        names, send_sems, recv_sems, bufs, plan = late[stage]
        return dict(zip(names, _push_wait(send_sems, recv_sems, bufs, plan, after, f"comm_gather_late{stage}_wait")))

    pending = []

    def send(names, g):
        if names == "small":
            bufs = [_as_rows([g[k] for k in SMALL])]
            bufs.append(lax.empty((N_DEV - 1,) + bufs[0].shape, F32))
            plan, tag = _broadcast_plan, "small"
        else:
            bufs = [g[k] for k in names]
            for k in names:
                R, C = g[k].shape
                piece = (R // (2 * N_CHIPS), C) if BIG_AXIS[k] == 0 else (R // 2, C // N_CHIPS)
                bufs.append(lax.empty((N_DEV - 1,) + piece, BF16))
            plan, tag = _scatter_plan([BIG_AXIS[k] for k in names]), names[0]
        ncopies = (N_DEV - 1) * (len(bufs) // 2)
        send_sems, recv_sems, thru, token = _push_start(bufs, plan, ncopies, f"comm_scatter_start_{tag}")
        pending.append((names, send_sems, recv_sems, thru, plan, tag))
        return token[0, 0]

    w0 = dict(g_mix=shard["g_mix"] + late_token, w_in=w_in_full, b_gate=shard["b_gate"], pool_scale=shard["pool_scale"], g_ffn=shard["g_ffn"],
              conv_b=shard["conv_b"], g_final=shard["g_final"].reshape(1, D))
    _, grad_x, gr = _local_step(x[0], loss_target[0], w0, late_weights, send)

    halves, small_parts = {}, None
    for names, send_sems, recv_sems, thru, plan, tag in pending:
        done = _push_wait(send_sems, recv_sems, thru, plan, grad_x, f"comm_scatter_wait_{tag}")
        if names == "small":
            small_parts = _sum_small(done[0], done[1], me, "sum_small").reshape(-1)
        else:
            m = len(names)
            for t, k in enumerate(names):
                halves[k] = _sum_pieces(done[t], BIG_AXIS[k], done[m + t], pos, f"sum_{k}")
    g_mix_own = _as_rows([gr["g_mix"]])
    _, g_mix_recv = _push([g_mix_own, lax.empty((N_DEV - 1,) + g_mix_own.shape, F32)], _broadcast_plan, N_DEV - 1, "comm_gather_g_mix")
    g_mix_sum = _sum_small(g_mix_own, g_mix_recv, me, "sum_g_mix").reshape(-1)[:D]
    wholes = _push([halves[k] for k in BIG], _exchange_plan, EXCHANGE_CHUNKS * len(BIG), "comm_exchange_halves")

    grads = {"g_mix": g_mix_sum.reshape(shard["g_mix"].shape)}
    for k, whole in zip(BIG, wholes):
        grads[k] = whole.reshape(shard[k].shape)
    off = 0
    loss = None
    for k in SMALL:
        sz = math.prod(gr[k].shape)
        fullg = small_parts[off:off + sz].reshape(gr[k].shape)
        off += sz
        if k == "loss_cols":
            loss = jnp.sum(fullg)
            continue
        if k in ("w_pool_lin", "conv_w"):
            n = shard[k].shape[2]
            fullg = lax.dynamic_slice_in_dim(fullg, chip * n, n, axis=1)
        grads[k] = fullg.reshape(shard[k].shape)

    deltas, new_m, new_v = {}, {}, {}
    for k in ORDER:
        shp = shard[k].shape
        two_d = (-1, shp[-1])
        dl, nm, nv = _adamw(shard[k].reshape(two_d), grads[k].reshape(two_d), mom[k].reshape(two_d), vel[k].reshape(two_d), f"adamw_{k}")
        deltas[k], new_m[k], new_v[k] = dl.reshape(shp), nm.reshape(shp), nv.reshape(shp)

    return (loss, grad_x[None], *[grads[k] for k in ORDER], *[deltas[k] for k in ORDER], *[new_m[k] for k in ORDER], *[new_v[k] for k in ORDER])
```

```python
import functools
import math

import jax
import jax.numpy as jnp
from jax import lax
from jax.experimental import pallas as pl
from jax.experimental.pallas import tpu as pltpu

F32 = jnp.float32
BF16 = jnp.bfloat16

RMS_EPS = 1e-6
POOL_WINDOWS = (2, 4, 8, 16)
ATTN_GROUPS = ((128, 1), (512, 4), (2048, 16))
HEADS_PER_GROUP = 4
HEAD_DIM = 128
N_ATTN_HEADS = HEADS_PER_GROUP * len(ATTN_GROUPS)
SPAN = 128
GROUP_WIDTH = HEADS_PER_GROUP * HEAD_DIM
ATTN_WIDTH = N_ATTN_HEADS * HEAD_DIM
ATTN_SCALE = HEAD_DIM ** -0.5
NEG_BIG = -1e30
ALIBI_SLOPES = tuple(2.0 ** (-8.0 * (h + 1) / N_ATTN_HEADS) for h in range(N_ATTN_HEADS))

ADAM_LR = 0.001
ADAM_B1 = 0.9
ADAM_B2 = 0.999
ADAM_EPS = 1e-08
ADAM_WD = 0.01
ADAM_STEP = 10

INV_SQRT2 = 1.0 / math.sqrt(2.0)
INV_SQRT_2PI = 1.0 / math.sqrt(2.0 * math.pi)

HALO = 16
VMEM_LIMIT = 56 * 1024 * 1024
N_CHIPS = 4
N_DEV = 8
MESH = pl.DeviceIdType.MESH
ANY = pl.BlockSpec(memory_space=pl.ANY)


def _cparams(*sem):
    return pltpu.CompilerParams(dimension_semantics=sem, vmem_limit_bytes=VMEM_LIMIT)


def _tile(n, pref, mult=128):
    t = (min(pref, n) // mult) * mult
    while t >= mult:
        if n % t == 0:
            return t
        t -= mult
    return n


def _dot(a, b, contract):
    return lax.dot_general(a, b, (contract, ((), ())), preferred_element_type=F32)


def _dot_nn(a, b):
    return _dot(a, b, ((1,), (0,)))


def _dot_nt(a, b):
    return _dot(a, b, ((1,), (1,)))


def _mm(a, b, *, mode, dims, name, tiles=None, out_dtypes=(BF16,), epilogue=None, extras=(), a_off=(0, 0), b_off=(0, 0)):
    M, N, K = dims
    if tiles is None:
        tiles = (_tile(M, 1408), _tile(N, 2816), _tile(K, 512)) if mode == "tn" else (_tile(M, 1024), _tile(N, 1536), _tile(K, 2816))
    tm, tn, tk = tiles
    assert M % tm == 0 and N % tn == 0 and K % tk == 0, (name, dims, tiles)
    nk = K // tk
    if mode == "nn":
        ab, bb, contract = (tm, tk), (tk, tn), ((1,), (0,))
        amap = lambda i, j, k: (i + a_off[0] // tm, k + a_off[1] // tk)
        bmap = lambda i, j, k: (k + b_off[0] // tk, j + b_off[1] // tn)
    elif mode == "nt":
        ab, bb, contract = (tm, tk), (tn, tk), ((1,), (1,))
        amap = lambda i, j, k: (i + a_off[0] // tm, k + a_off[1] // tk)
        bmap = lambda i, j, k: (j + b_off[0] // tn, k + b_off[1] // tk)
    else:
        ab, bb, contract = (tk, tm), (tk, tn), ((0,), (0,))
        amap = lambda i, j, k: (k + a_off[0] // tk, i + a_off[1] // tm)
        bmap = lambda i, j, k: (k + b_off[0] // tk, j + b_off[1] // tn)
    assert a_off[0] % ab[0] == 0 and a_off[1] % ab[1] == 0 and b_off[0] % bb[0] == 0 and b_off[1] % bb[1] == 0, name
    in_specs = [pl.BlockSpec(ab, amap), pl.BlockSpec(bb, bmap)]
    ex_arrays = []
    for arr, kind, off in extras:
        if kind == "mn":
            assert off[0] % tm == 0 and off[1] % tn == 0, name
            in_specs.append(pl.BlockSpec((tm, tn), lambda i, j, k, off=off: (i + off[0] // tm, j + off[1] // tn)))
        else:
            assert off[1] % tn == 0, name
            in_specs.append(pl.BlockSpec((1, tn), lambda i, j, k, off=off: (0, j + off[1] // tn)))
        ex_arrays.append(arr)
    ne, no = len(ex_arrays), len(out_dtypes)
    if epilogue is None:
        def epilogue(acc, ex, outs):
            outs[0][...] = acc.astype(outs[0].dtype)

    def body(*refs):
        a_ref, b_ref = refs[0], refs[1]
        ex, outs = refs[2:2 + ne], refs[2 + ne:2 + ne + no]
        if nk == 1:
            epilogue(_dot(a_ref[...], b_ref[...], contract), ex, outs)
            return
        acc = refs[-1]
        k = pl.program_id(2)
        if nk <= 4:
            part = _dot(a_ref[...], b_ref[...], contract)

            @pl.when(k == 0)
            def _():
                acc[...] = part

            @pl.when(jnp.logical_and(k > 0, k < nk - 1))
            def _():
                acc[...] += part

            @pl.when(k == nk - 1)
            def _():
                epilogue(acc[...] + part, ex, outs)
        else:
            @pl.when(k == 0)
            def _():
                acc[...] = _dot(a_ref[...], b_ref[...], contract)

            @pl.when(k > 0)
            def _():
                acc[...] += _dot(a_ref[...], b_ref[...], contract)

            @pl.when(k == nk - 1)
            def _():
                epilogue(acc[...], ex, outs)

    res = pl.pallas_call(
        body,
        grid=(M // tm, N // tn, nk),
        in_specs=in_specs,
        out_specs=[pl.BlockSpec((tm, tn), lambda i, j, k: (i, j)) for _ in out_dtypes],
        out_shape=[jax.ShapeDtypeStruct((M, N), dt) for dt in out_dtypes],
        scratch_shapes=[pltpu.VMEM((tm, tn), F32)] if nk > 1 else [],
        compiler_params=_cparams("parallel", "parallel", "arbitrary"),
        name=name,
    )(a, b, *ex_arrays)
    return res[0] if no == 1 else res


def _rms_fwd(x, g, name):
    S, D = x.shape
    tm = _tile(S, 256)

    def body(x_ref, g_ref, h_ref):
        xv = x_ref[...]
        r = lax.rsqrt(jnp.mean(xv * xv, axis=-1, keepdims=True) + RMS_EPS)
        h_ref[...] = (xv * r * g_ref[...]).astype(h_ref.dtype)

    return pl.pallas_call(
        body,
        grid=(S // tm,),
        in_specs=[pl.BlockSpec((tm, D), lambda i: (i, 0)), pl.BlockSpec((1, D), lambda i: (0, 0))],
        out_specs=pl.BlockSpec((tm, D), lambda i: (i, 0)),
        out_shape=jax.ShapeDtypeStruct((S, D), BF16),
        compiler_params=_cparams("parallel"),
        name=name,
    )(x, g)


def _rms_bwd(dh, x, g, dres, name, with_bf16):
    S, D = x.shape
    tm = _tile(S, 256)

    def body(dh_ref, x_ref, g_ref, dres_ref, *outs):
        dx_ref, dg_ref = outs[0], outs[-1]
        xv = x_ref[...]
        r = lax.rsqrt(jnp.mean(xv * xv, axis=-1, keepdims=True) + RMS_EPS)
        xr = xv * r
        dhv = dh_ref[...].astype(F32)

        @pl.when(pl.program_id(0) == 0)
        def _():
            dg_ref[...] = jnp.zeros_like(dg_ref)

        dg_ref[...] += jnp.sum(dhv * xr, axis=0, keepdims=True)
        u = dhv * g_ref[...]
        c = jnp.mean(u * xr, axis=-1, keepdims=True)
        dx = dres_ref[...] + r * (u - xr * c)
        dx_ref[...] = dx
        if with_bf16:
            outs[1][...] = dx.astype(BF16)

    row = pl.BlockSpec((tm, D), lambda i: (i, 0))
    vec = pl.BlockSpec((1, D), lambda i: (0, 0))
    out_specs = [row] + ([row] if with_bf16 else []) + [vec]
    out_shape = [jax.ShapeDtypeStruct((S, D), F32)] + ([jax.ShapeDtypeStruct((S, D), BF16)] if with_bf16 else []) + [jax.ShapeDtypeStruct((1, D), F32)]
    return pl.pallas_call(
        body,
        grid=(S // tm,),
        in_specs=[row, row, vec, row],
        out_specs=out_specs,
        out_shape=out_shape,
        compiler_params=_cparams("arbitrary"),
        name=name,
    )(dh, x, g, dres)


def _loss_head(x3, tgt, g, name):
    S, D = x3.shape
    tm = _tile(S, 256)

    def body(x_ref, t_ref, g_ref, dx_ref, dxb_ref, dg_ref, loss_ref):
        xv = x_ref[...]
        gv = g_ref[...]
        r = lax.rsqrt(jnp.mean(xv * xv, axis=-1, keepdims=True) + RMS_EPS)
        xr = xv * r
        e = xr * gv - t_ref[...]

        @pl.when(pl.program_id(0) == 0)
        def _():
            dg_ref[...] = jnp.zeros_like(dg_ref)
            loss_ref[...] = jnp.zeros_like(loss_ref)

        loss_ref[...] += jnp.sum(e * e, axis=0, keepdims=True) * (0.5 / D)
        dy = e * (1.0 / D)
        dg_ref[...] += jnp.sum(dy * xr, axis=0, keepdims=True)
        u = dy * gv
        c = jnp.mean(u * xr, axis=-1, keepdims=True)
        dx = r * (u - xr * c)
        dx_ref[...] = dx
        dxb_ref[...] = dx.astype(BF16)

    row = pl.BlockSpec((tm, D), lambda i: (i, 0))
    vec = pl.BlockSpec((1, D), lambda i: (0, 0))
    return pl.pallas_call(
        body,
        grid=(S // tm,),
        in_specs=[row, row, vec],
        out_specs=[row, row, vec, vec],
        out_shape=[jax.ShapeDtypeStruct((S, D), F32), jax.ShapeDtypeStruct((S, D), BF16), jax.ShapeDtypeStruct((1, D), F32), jax.ShapeDtypeStruct((1, D), F32)],
        compiler_params=_cparams("arbitrary"),
        name=name,
    )(x3, tgt, g)


def _conv_taps(cur_ref, halo_ref, w_ref, b_ref, first):
    cur = cur_ref[...].astype(F32)
    halo = jnp.where(first, 0.0, halo_ref[...].astype(F32))
    xx = jnp.concatenate([halo, cur], axis=0)
    p1 = pltpu.roll(xx, 1, 0)[HALO:]
    p2 = pltpu.roll(xx, 2, 0)[HALO:]
    w = w_ref[...]
    y = b_ref[...] + w[0:1] * p2 + w[1:2] * p1 + w[2:3] * cur
    return y, (cur, p1, p2)


def _convglu_fwd(up, cw, cb, name):
    S, F2 = up.shape
    F = F2 // 2
    tm, tn = _tile(S, 512), _tile(F, 512)
    nj, hb = F // tn, tm // HALO

    def body(ua, ub, ha, hb_, wa, wb, ba, bb, f_ref, a_ref, b_ref):
        first = pl.program_id(0) == 0
        a, _ = _conv_taps(ua, ha, wa, ba, first)
        b, _ = _conv_taps(ub, hb_, wb, bb, first)
        f_ref[...] = (0.5 * a * (1.0 + lax.erf(a * INV_SQRT2)) * b).astype(f_ref.dtype)
        a_ref[...] = a.astype(a_ref.dtype)
        b_ref[...] = b.astype(b_ref.dtype)

    tile = pl.BlockSpec((tm, tn), lambda i, j: (i, j))
    return pl.pallas_call(
        body,
        grid=(S // tm, nj),
        in_specs=[
            tile,
            pl.BlockSpec((tm, tn), lambda i, j: (i, j + nj)),
            pl.BlockSpec((HALO, tn), lambda i, j: (jnp.maximum(i * hb - 1, 0), j)),
            pl.BlockSpec((HALO, tn), lambda i, j: (jnp.maximum(i * hb - 1, 0), j + nj)),
            pl.BlockSpec((3, tn), lambda i, j: (0, j)),
            pl.BlockSpec((3, tn), lambda i, j: (0, j + nj)),
            pl.BlockSpec((1, tn), lambda i, j: (0, j)),
            pl.BlockSpec((1, tn), lambda i, j: (0, j + nj)),
        ],
        out_specs=[tile, tile, tile],
        out_shape=[jax.ShapeDtypeStruct((S, F), BF16)] * 3,
        compiler_params=_cparams("parallel", "parallel"),
        name=name,
    )(up, up, up, up, cw, cw, cb, cb)


def _convglu_bwd(df, a, b, up, cw, name):
    S, F = df.shape
    tm, tn = _tile(S, 512), _tile(F, 512)
    nj, ni, hb = F // tn, S // tm, tm // HALO
    n = tm + HALO

    def body(df_ref, dfn_ref, a_ref, an_ref, b_ref, bn_ref, up_ref, w_ref, o_ref, db_ref, dw_ref):
        j, i = pl.program_id(0), pl.program_id(1)
        last = i == ni - 1

        def rows(c_ref, n_ref):
            return jnp.concatenate([c_ref[...].astype(F32), jnp.where(last, 0.0, n_ref[...].astype(F32))], axis=0)

        @pl.when(i == 0)
        def _():
            db_ref[...] = jnp.zeros_like(db_ref)
            dw_ref[...] = jnp.zeros_like(dw_ref)

        def finish(d):
            d0 = d[:tm]
            d1 = pltpu.roll(d, n - 1, 0)[:tm]
            d2 = pltpu.roll(d, n - 2, 0)[:tm]
            w = w_ref[...]
            o_ref[...] = (w[2:3] * d0 + w[1:2] * d1 + w[0:1] * d2).astype(o_ref.dtype)
            upv = up_ref[...].astype(F32)
            db_ref[...] += jnp.sum(d0, axis=0, keepdims=True)
            dw_ref[0:1, :] += jnp.sum(d2 * upv, axis=0, keepdims=True)
            dw_ref[1:2, :] += jnp.sum(d1 * upv, axis=0, keepdims=True)
            dw_ref[2:3, :] += jnp.sum(d0 * upv, axis=0, keepdims=True)

        av, dfv = rows(a_ref, an_ref), rows(df_ref, dfn_ref)
        cdf = 0.5 * (1.0 + lax.erf(av * INV_SQRT2))

        @pl.when(j < nj)
        def _():
            pdf = jnp.exp(-0.5 * av * av) * INV_SQRT_2PI
            finish(dfv * rows(b_ref, bn_ref) * (cdf + av * pdf))

        @pl.when(j >= nj)
        def _():
            finish(dfv * (av * cdf))

    jh = lambda j: lax.rem(j, nj)
    nxt = lambda i: jnp.minimum((i + 1) * hb, S // HALO - 1)
    cur = pl.BlockSpec((tm, tn), lambda j, i: (i, jh(j)))
    halo = pl.BlockSpec((HALO, tn), lambda j, i: (nxt(i), jh(j)))
    return pl.pallas_call(
        body,
        grid=(2 * nj, ni),
        in_specs=[cur, halo, cur, halo, cur, halo, pl.BlockSpec((tm, tn), lambda j, i: (i, j)), pl.BlockSpec((3, tn), lambda j, i: (0, j))],
        out_specs=[pl.BlockSpec((tm, tn), lambda j, i: (i, j)), pl.BlockSpec((1, tn), lambda j, i: (0, j)), pl.BlockSpec((3, tn), lambda j, i: (0, j))],
        out_shape=[jax.ShapeDtypeStruct((S, 2 * F), BF16), jax.ShapeDtypeStruct((1, 2 * F), F32), jax.ShapeDtypeStruct((3, 2 * F), F32)],
        compiler_params=_cparams("parallel", "arbitrary"),
        name=name,
    )(df, df, a, a, b, b, up, cw)


def _gate_bwd(dmixed, gates, y_pool, y_attn, in_width, name):
    S, D = dmixed.shape
    tm, tn = _tile(S, 512), _tile(D, 512)
    nj = D // tn
    pre0 = (in_width - 2 * D) // tn
    assert pre0 * tn == in_width - 2 * D

    def body(dm_ref, g_ref, yp_ref, ya_ref, dy_ref, dpre_ref, db_ref):
        j = pl.program_id(0)

        @pl.when(pl.program_id(1) == 0)
        def _():
            db_ref[...] = jnp.zeros_like(db_ref)

        def run(y_ref):
            dm = dm_ref[...].astype(F32)
            gv = g_ref[...].astype(F32)
            dy_ref[...] = (dm * gv).astype(BF16)
            dpre = dm * y_ref[...].astype(F32) * gv * (1.0 - gv)
            dpre_ref[...] = dpre.astype(BF16)
            db_ref[...] += jnp.sum(dpre, axis=0, keepdims=True)

        @pl.when(j < nj)
        def _():
            run(yp_ref)

        @pl.when(j >= nj)
        def _():
            run(ya_ref)

    tile2 = pl.BlockSpec((tm, tn), lambda j, i: (i, j))
    return pl.pallas_call(
        body,
        grid=(2 * nj, S // tm),
        in_specs=[
            pl.BlockSpec((tm, tn), lambda j, i: (i, lax.rem(j, nj))),
            tile2,
            pl.BlockSpec((tm, tn), lambda j, i: (i, jnp.minimum(j, nj - 1))),
            pl.BlockSpec((tm, tn), lambda j, i: (i, jnp.maximum(j - nj, 0))),
        ],
        out_specs=[tile2, pl.BlockSpec((tm, tn), lambda j, i: (i, pre0 + j)), pl.BlockSpec((1, tn), lambda j, i: (0, j))],
        out_shape=[jax.ShapeDtypeStruct((S, 2 * D), BF16), jax.ShapeDtypeStruct((S, in_width), BF16), jax.ShapeDtypeStruct((1, 2 * D), F32)],
        compiler_params=_cparams("parallel", "arbitrary"),
        name=name,
    )(dmixed, gates, y_pool, y_attn)


def _pool_counts(i, tm, rows, w):
    t = i * tm + lax.broadcasted_iota(jnp.int32, (rows, 1), 0)
    return jnp.minimum(t + 1, w).astype(F32)


def _pooled_groups(u_ref, uh_ref, i, tm, C):
    cur = u_ref[...]
    halo = jnp.where(i == 0, 0.0, uh_ref[...])
    xx = jnp.concatenate([halo, cur], axis=0)
    out = []
    s = xx
    for gi, w in enumerate(POOL_WINDOWS):
        s = s + pltpu.roll(s, w // 2, 0)
        tot = s[HALO:, 0:C]
        out.append(tot / _pool_counts(i, tm, tm, w) - cur[:, gi * C:(gi + 1) * C])
        s = s[:, C:] if gi + 1 < len(POOL_WINDOWS) else s
    return out


def _pool_fwd(u, wl, scale, name):
    S, PW = u.shape
    C = PW // len(POOL_WINDOWS)
    tm = _tile(S, 512)
    hb = tm // HALO

    def body(u_ref, uh_ref, wl_ref, sc_ref, o_ref):
        i = pl.program_id(0)
        pooled = _pooled_groups(u_ref, uh_ref, i, tm, C)
        for gi in range(len(POOL_WINDOWS)):
            y = _dot_nn(pooled[gi].astype(BF16), wl_ref[gi])
            o_ref[:, gi * C:(gi + 1) * C] = (y * sc_ref[:, gi * C:(gi + 1) * C]).astype(o_ref.dtype)

    return pl.pallas_call(
        body,
        grid=(S // tm,),
        in_specs=[
            pl.BlockSpec((tm, PW), lambda i: (i, 0)),
            pl.BlockSpec((HALO, PW), lambda i: (jnp.maximum(i * hb - 1, 0), 0)),
            pl.BlockSpec((len(POOL_WINDOWS), C, C), lambda i: (0, 0, 0)),
            pl.BlockSpec((1, PW), lambda i: (0, 0)),
        ],
        out_specs=pl.BlockSpec((tm, PW), lambda i: (i, 0)),
        out_shape=jax.ShapeDtypeStruct((S, PW), BF16),
        compiler_params=_cparams("parallel"),
        name=name,
    )(u, u, wl, scale)


def _pool_bwd(u, dp, wl, scale, dproj, name):
    S, PW = u.shape
    G = len(POOL_WINDOWS)
    C = PW // G
    tm = _tile(S, 512)
    hb, ni = tm // HALO, S // tm
    n = tm + HALO

    def body(u_ref, uh_ref, dp_ref, dpn_ref, wl_ref, sc_ref, _, du_ref, dwl_ref, dsc_ref):
        i = pl.program_id(0)

        @pl.when(i == 0)
        def _():
            dwl_ref[...] = jnp.zeros_like(dwl_ref)
            dsc_ref[...] = jnp.zeros_like(dsc_ref)

        pooled = _pooled_groups(u_ref, uh_ref, i, tm, C)
        dpc = dp_ref[...].astype(F32)
        dpn = jnp.where(i == ni - 1, 0.0, dpn_ref[...].astype(F32))
        sc = sc_ref[...]
        dyl = jnp.concatenate([dpc, dpn], axis=0) * sc
        for gi, w in enumerate(POOL_WINDOWS):
            cols = slice(gi * C, (gi + 1) * C)
            pb = pooled[gi].astype(BF16)
            ylin = _dot_nn(pb, wl_ref[gi])
            dsc_ref[:, cols] += jnp.sum(dpc[:, cols] * ylin, axis=0, keepdims=True)
            dylg = dyl[:, cols].astype(BF16)
            dwl_ref[gi] += _dot(pb, dylg[:tm], ((0,), (0,)))
            dpool = _dot_nt(dylg, wl_ref[gi])
            e = dpool / _pool_counts(i, tm, n, w)
            k = 1
            while k < w:
                e = e + pltpu.roll(e, n - k, 0)
                k *= 2
            du_ref[:, cols] = (e[:tm] - dpool[:tm]).astype(du_ref.dtype)

    return pl.pallas_call(
        body,
        grid=(ni,),
        in_specs=[
            pl.BlockSpec((tm, PW), lambda i: (i, 0)),
            pl.BlockSpec((HALO, PW), lambda i: (jnp.maximum(i * hb - 1, 0), 0)),
            pl.BlockSpec((tm, PW), lambda i: (i, 0)),
            pl.BlockSpec((HALO, PW), lambda i: (jnp.minimum((i + 1) * hb, S // HALO - 1), 0)),
            pl.BlockSpec((G, C, C), lambda i: (0, 0, 0)),
            pl.BlockSpec((1, PW), lambda i: (0, 0)),
            ANY,
        ],
        out_specs=[pl.BlockSpec((tm, PW), lambda i: (i, 0)), pl.BlockSpec((G, C, C), lambda i: (0, 0, 0)), pl.BlockSpec((1, PW), lambda i: (0, 0))],
        out_shape=[jax.ShapeDtypeStruct(dproj.shape, dproj.dtype), jax.ShapeDtypeStruct((G, C, C), F32), jax.ShapeDtypeStruct((1, PW), F32)],
        input_output_aliases={6: 0},
        compiler_params=_cparams("arbitrary"),
        name=name,
    )(u, u, dp, dp, wl, scale, dproj)


def _band_masks():
    ii = lax.broadcasted_iota(jnp.int32, (SPAN, SPAN), 0)
    kk = lax.broadcasted_iota(jnp.int32, (SPAN, SPAN), 1)
    return ((ii + SPAN - kk).astype(F32), kk >= ii), ((ii - kk).astype(F32), kk <= ii)


ATTN_TILE = 16 * SPAN


def _unit_rows(r, b, d):
    return pl.ds(d * SPAN * b + r, SPAN, stride=d) if d > 1 else pl.ds(SPAN * b, SPAN)


def _f32_copies(refs, scratch, d):
    if d == 1:
        return list(refs)
    for ref, s in zip(refs, scratch):
        s[...] = ref[...].astype(F32)
    return list(scratch)


def _attn_fwd(qkv, d, g, name):
    S = qkv.shape[0]
    T = min(ATTN_TILE, S)
    P = SPAN * d
    nbk = T // P

    def body(q_ref, k_ref, v_ref, kp_ref, vp_ref, o_ref, lse_ref, *scratch):
        c = pl.program_id(0)
        (jp, mp), (jc, mc) = _band_masks()
        slopes = [ALIBI_SLOPES[g * HEADS_PER_GROUP + h] * d for h in range(HEADS_PER_GROUP)]
        slope = slopes[0]
        for h in range(1, HEADS_PER_GROUP):
            slope = jnp.where(pl.program_id(1) == h, slopes[h], slope)
        q_s, k_s, v_s, kp_s, vp_s = _f32_copies((q_ref, k_ref, v_ref, kp_ref, vp_ref), scratch[:5], d)
        o_s, l_s = (o_ref, lse_ref) if d == 1 else scratch[5:7]
        for r in range(d):
            for b in range(nbk):
                rows = _unit_rows(r, b, d)
                q = q_s[rows, :].astype(BF16)
                kc, vc = k_s[rows, :].astype(BF16), v_s[rows, :].astype(BF16)
                if b == 0:
                    prev = _unit_rows(r, 0, d)
                    kp, vp, okp = kp_s[prev, :].astype(BF16), vp_s[prev, :].astype(BF16), jnp.logical_and(mp, c > 0)
                else:
                    prev = _unit_rows(r, b - 1, d)
                    kp, vp, okp = k_s[prev, :].astype(BF16), v_s[prev, :].astype(BF16), mp
                sc = jnp.where(mc, _dot_nt(q, kc) * ATTN_SCALE - slope * jc, NEG_BIG)
                sp = jnp.where(okp, _dot_nt(q, kp) * ATTN_SCALE - slope * jp, NEG_BIG)
                m = jnp.maximum(jnp.max(sc, axis=-1, keepdims=True), jnp.max(sp, axis=-1, keepdims=True))
                pc, pp = jnp.exp(sc - m), jnp.exp(sp - m)
                l = jnp.sum(pc, axis=-1, keepdims=True) + jnp.sum(pp, axis=-1, keepdims=True)
                o_s[rows, :] = (_dot_nn(pc.astype(BF16), vc) + _dot_nn(pp.astype(BF16), vp)) / l
                l_s[rows, :] = jnp.broadcast_to(m + jnp.log(l), (SPAN, HEAD_DIM))
        if d > 1:
            o_ref[...] = o_s[...]
            lse_ref[...] = l_s[...]

    col = lambda kind: (lambda c, h: (c, kind * N_ATTN_HEADS + g * HEADS_PER_GROUP + h))
    pcol = lambda kind: (lambda c, h: (jnp.maximum(c * nbk - 1, 0), kind * N_ATTN_HEADS + g * HEADS_PER_GROUP + h))
    cur = lambda kind: pl.BlockSpec((T, HEAD_DIM), col(kind))
    prv = lambda kind: pl.BlockSpec((P, HEAD_DIM), pcol(kind))
    out = pl.BlockSpec((T, HEAD_DIM), lambda c, h: (c, h))
    scratch = [] if d == 1 else [pltpu.VMEM((T, HEAD_DIM), F32)] * 3 + [pltpu.VMEM((P, HEAD_DIM), F32)] * 2 + [pltpu.VMEM((T, HEAD_DIM), F32)] * 2
    return pl.pallas_call(
        body,
        grid=(S // T, HEADS_PER_GROUP),
        in_specs=[cur(0), cur(1), cur(2), prv(1), prv(2)],
        out_specs=[out, out],
        out_shape=[jax.ShapeDtypeStruct((S, GROUP_WIDTH), F32)] * 2,
        scratch_shapes=scratch,
        compiler_params=_cparams("parallel", "parallel"),
        name=name,
    )(qkv, qkv, qkv, qkv, qkv)


def _attn_merge(os_, lses, name):
    S, W = os_[0].shape
    tm = _tile(S, 512)

    def body(o0, o1, o2, l0, l1, l2, y_ref, lse_ref):
        ls = [l0[...], l1[...], l2[...]]
        m = jnp.maximum(jnp.maximum(ls[0], ls[1]), ls[2])
        es = [jnp.exp(v - m) for v in ls]
        tot = es[0] + es[1] + es[2]
        y = (es[0] * o0[...] + es[1] * o1[...] + es[2] * o2[...]) / tot
        y_ref[...] = y.astype(y_ref.dtype)
        lse_ref[...] = m + jnp.log(tot)

    row = pl.BlockSpec((tm, W), lambda i: (i, 0))
    return pl.pallas_call(
        body,
        grid=(S // tm,),
        in_specs=[row] * 6,
        out_specs=[row, row],
        out_shape=[jax.ShapeDtypeStruct((S, W), BF16), jax.ShapeDtypeStruct((S, W), F32)],
        compiler_params=_cparams("parallel"),
        name=name,
    )(*os_, *lses)


def _attn_bwd(qkv, dattn, y, lse, dproj, d, g, col0, name):
    S = qkv.shape[0]
    T = min(ATTN_TILE, S)
    P = SPAN * d
    nbk = T // P
    ntile = S // T

    def body(q_ref, k_ref, v_ref, kp_ref, vp_ref, qn_ref, da_ref, dan_ref, y_ref, yn_ref, lse_ref, lsen_ref, _, out_ref, dq_s, dk_s, dv_s, *scratch):
        c = pl.program_id(0)
        head_id = pl.program_id(1)
        kind = pl.program_id(2)

        @pl.when(kind == 0)
        def _():
            (jp, mp), (jc, mc) = _band_masks()
            slopes = [ALIBI_SLOPES[g * HEADS_PER_GROUP + h] * d for h in range(HEADS_PER_GROUP)]
            slope = slopes[0]
            for h in range(1, HEADS_PER_GROUP):
                slope = jnp.where(head_id == h, slopes[h], slope)
            q_s, k_s, v_s, da_s, y_s, kp_s, vp_s, qn_s, dan_s, yn_s = _f32_copies(
                (q_ref, k_ref, v_ref, da_ref, y_ref, kp_ref, vp_ref, qn_ref, dan_ref, yn_ref), scratch, d)
            for r in range(d):
                dq = [None] * nbk
                dk = [None] * nbk
                dv = [None] * nbk

                def add(lst, idx, val):
                    lst[idx] = val if lst[idx] is None else lst[idx] + val

                for qb in range(nbk + 1):
                    if qb < nbk:
                        rows = _unit_rows(r, qb, d)
                        q, da, yy, lse_blk = q_s[rows, :], da_s[rows, :], y_s[rows, :], lse_ref[rows, :]
                    else:
                        rows = _unit_rows(r, 0, d)
                        q, da, yy, lse_blk = qn_s[rows, :], dan_s[rows, :], yn_s[rows, :], lsen_ref[rows, :]
                    lse_col = lse_blk[:, 0:1]
                    dd = jnp.sum(da.astype(F32) * yy.astype(F32), axis=-1, keepdims=True)
                    q, da = q.astype(BF16), da.astype(BF16)
                    for kb in (qb - 1, qb):
                        if kb >= nbk:
                            continue
                        if kb < 0:
                            krows = _unit_rows(r, 0, d)
                            kk, vv, ok = kp_s[krows, :].astype(BF16), vp_s[krows, :].astype(BF16), jnp.logical_and(mp, c > 0)
                        else:
                            krows = _unit_rows(r, kb, d)
                            kk, vv = k_s[krows, :].astype(BF16), v_s[krows, :].astype(BF16)
                            ok = mc if kb == qb else (mp if qb < nbk else jnp.logical_and(mp, c < ntile - 1))
                        jj = jc if kb == qb else jp
                        s = jnp.where(ok, _dot_nt(q, kk) * ATTN_SCALE - slope * jj, NEG_BIG)
                        p = jnp.exp(s - lse_col)
                        ds = p * (_dot_nt(da, vv) - dd)
                        if qb < nbk:
                            add(dq, qb, _dot_nn(ds.astype(BF16), kk))
                        if kb >= 0:
                            add(dv, kb, _dot_nn(p.T.astype(BF16), da))
                            add(dk, kb, _dot_nn(ds.T.astype(BF16), q))
                for b in range(nbk):
                    rows = _unit_rows(r, b, d)
                    dq_s[rows, :] = dq[b] * ATTN_SCALE
                    dk_s[rows, :] = dk[b] * ATTN_SCALE
                    dv_s[rows, :] = dv[b]
            out_ref[...] = dq_s[...].astype(out_ref.dtype)

        @pl.when(kind == 1)
        def _():
            out_ref[...] = dk_s[...].astype(out_ref.dtype)

        @pl.when(kind == 2)
        def _():
            out_ref[...] = dv_s[...].astype(out_ref.dtype)

    head = lambda h: g * HEADS_PER_GROUP + h
    cur = lambda kind: pl.BlockSpec((T, HEAD_DIM), lambda c, h, kd: (c, kind * N_ATTN_HEADS + head(h)))
    prv = lambda kind: pl.BlockSpec((P, HEAD_DIM), lambda c, h, kd: (jnp.maximum(c * nbk - 1, 0), kind * N_ATTN_HEADS + head(h)))
    nxt_row = lambda c: jnp.minimum((c + 1) * nbk, S // P - 1)
    qnext = pl.BlockSpec((P, HEAD_DIM), lambda c, h, kd: (nxt_row(c), head(h)))
    hcur = pl.BlockSpec((T, HEAD_DIM), lambda c, h, kd: (c, h))
    hnext = pl.BlockSpec((P, HEAD_DIM), lambda c, h, kd: (nxt_row(c), h))
    out = pl.BlockSpec((T, HEAD_DIM), lambda c, h, kd: (c, col0 + kd * N_ATTN_HEADS + head(h)))
    stage = [pltpu.VMEM((T, HEAD_DIM), F32)] * 3
    copies = [] if d == 1 else [pltpu.VMEM((T, HEAD_DIM), F32)] * 5 + [pltpu.VMEM((P, HEAD_DIM), F32)] * 5
    return pl.pallas_call(
        body,
        grid=(ntile, HEADS_PER_GROUP, 3),
        in_specs=[cur(0), cur(1), cur(2), prv(1), prv(2), qnext, hcur, hnext, hcur, hnext, hcur, hnext, ANY],
        out_specs=out,
        out_shape=jax.ShapeDtypeStruct(dproj.shape, dproj.dtype),
        input_output_aliases={12: 0},
        scratch_shapes=stage + copies,
        compiler_params=_cparams("parallel", "parallel", "arbitrary"),
        name=name,
    )(qkv, qkv, qkv, qkv, qkv, qkv, dattn, dattn, y, y, lse, lse, dproj)


def _row_block(R, C, bytes_per_row_elem=4, budget=1 << 20):
    if R % 8:
        return R
    best = 8
    t = 8
    while t <= R:
        if R % t == 0 and t * C * bytes_per_row_elem <= budget:
            best = t
        t += 8
    return best


def _adamw(w, g, m, v, name):
    R, C = w.shape
    tr = _row_block(R, C)
    c1 = 1.0 - ADAM_B1 ** ADAM_STEP
    c2 = 1.0 - ADAM_B2 ** ADAM_STEP

    def body(w_ref, g_ref, m_ref, v_ref, d_ref, nm_ref, nv_ref):
        gv = g_ref[...]
        nm = ADAM_B1 * m_ref[...] + (1.0 - ADAM_B1) * gv
        nv = ADAM_B2 * v_ref[...] + (1.0 - ADAM_B2) * (gv * gv)
        d_ref[...] = -ADAM_LR * ((nm / c1) / (jnp.sqrt(nv / c2) + ADAM_EPS) + ADAM_WD * w_ref[...])
        nm_ref[...] = nm
        nv_ref[...] = nv

    blk = pl.BlockSpec((tr, C), lambda i: (i, 0))
    return pl.pallas_call(
        body,
        grid=(R // tr,),
        in_specs=[blk] * 4,
        out_specs=[blk] * 3,
        out_shape=[jax.ShapeDtypeStruct((R, C), F32)] * 3,
        compiler_params=_cparams("parallel"),
        name=name,
    )(w, g, m, v)


def _sum_pieces(grad, axis, recv, pos, name):
    n, pr, pc = recv.shape
    tr = _row_block(pr, pc, bytes_per_row_elem=(n + 1) * recv.dtype.itemsize, budget=4 << 20)
    nblk = pr // tr
    if axis == 1:
        own_map = lambda i, p: (p[1] * nblk + i, p[0])
    else:
        own_map = lambda i, p: ((2 * p[0] + p[1]) * nblk + i, 0)

    def body(p_ref, own_ref, r_ref, o_ref):
        acc = own_ref[...].astype(F32)
        for s in range(n):
            acc = acc + r_ref[s].astype(F32)
        o_ref[...] = acc

    return pl.pallas_call(
        body,
        grid_spec=pltpu.PrefetchScalarGridSpec(
            num_scalar_prefetch=1,
            grid=(nblk,),
            in_specs=[pl.BlockSpec((tr, pc), own_map), pl.BlockSpec((n, tr, pc), lambda i, p: (0, i, 0))],
            out_specs=pl.BlockSpec((tr, pc), lambda i, p: (p[1] * nblk + i, 0)),
        ),
        out_shape=jax.ShapeDtypeStruct((2 * pr, pc), F32),
        compiler_params=_cparams("parallel"),
        name=name,
    )(pos, grad, recv)


def _sum_small(own, recv, me, name):
    n, R, C = recv.shape
    tr = _row_block(R, C, bytes_per_row_elem=(n + 1) * 4, budget=4 << 20)

    def body(me_ref, own_ref, r_ref, o_ref):
        acc = None
        for dev in range(n + 1):
            k = jnp.bitwise_xor(me_ref[0], dev)
            term = jnp.where(k == 0, own_ref[...], r_ref[jnp.maximum(k - 1, 0)])
            acc = term if acc is None else acc + term
        o_ref[...] = acc

    return pl.pallas_call(
        body,
        grid_spec=pltpu.PrefetchScalarGridSpec(
            num_scalar_prefetch=1,
            grid=(R // tr,),
            in_specs=[pl.BlockSpec((tr, C), lambda i, m: (i, 0)), pl.BlockSpec((n, tr, C), lambda i, m: (0, i, 0))],
            out_specs=pl.BlockSpec((tr, C), lambda i, m: (i, 0)),
        ),
        out_shape=jax.ShapeDtypeStruct((R, C), F32),
        compiler_params=_cparams("parallel"),
        name=name,
    )(me, own, recv)


def _place(shard, axis, pos, dtype, name):
    shp = list(shard.shape)
    shp[axis] *= N_CHIPS
    if shard.ndim == 3:
        assert axis == 1
        in_spec = pl.BlockSpec(shard.shape, lambda i, p: (0, 0, 0))
        out_spec = pl.BlockSpec(shard.shape, lambda i, p: (0, p[0], 0))
        grid = (1,)
    else:
        R, C = shard.shape
        tr = _row_block(R, C, bytes_per_row_elem=4, budget=2 << 20)
        nblk = R // tr
        in_spec = pl.BlockSpec((tr, C), lambda i, p: (i, 0))
        out_spec = pl.BlockSpec((tr, C), (lambda i, p: (i, p[0])) if axis == 1 else (lambda i, p: (p[0] * nblk + i, 0)))
        grid = (nblk,)

    def body(p_ref, s_ref, o_ref):
        o_ref[...] = s_ref[...].astype(o_ref.dtype)

    return pl.pallas_call(
        body,
        grid_spec=pltpu.PrefetchScalarGridSpec(num_scalar_prefetch=1, grid=grid, in_specs=[in_spec], out_specs=out_spec),
        out_shape=jax.ShapeDtypeStruct(tuple(shp), dtype),
        compiler_params=_cparams("parallel"),
        name=name,
    )(pos, shard)


HBM = pl.BlockSpec(memory_space=pltpu.HBM)
SEM = pl.BlockSpec(memory_space=pltpu.SEMAPHORE)
DATAFLOW = pltpu.SideEffectType.DATAFLOW_SIDE_EFFECTING


def _position():
    return lax.axis_index("x"), lax.axis_index("y"), lax.axis_index("c")


def _peer(k):
    x, y, c = _position()
    return ((1 - x) if k & 4 else x, (1 - y) if k & 2 else y, (1 - c) if k & 1 else c)


def _shard_slice(ref, axis, idx, size):
    start = idx * size
    if axis == ref.ndim - 1:
        start = pl.multiple_of(start, 128)
    ix = [slice(None)] * ref.ndim
    ix[axis] = pl.ds(start, size)
    return ref.at[tuple(ix)]


def _gather_plan(axes):
    def plan(refs):
        x, y, c = _position()
        out = []
        for ref, ax in zip(refs, axes):
            mine = _shard_slice(ref, ax, 2 * x + y, ref.shape[ax] // N_CHIPS)
            for k in (4, 2, 6):
                px, py, _ = _peer(k)
                out.append((mine, mine, (px, py, c)))
        return out
    return plan


def _scatter_plan(axes):
    m = len(axes)

    def plan(refs):
        out = []
        for t in range(m):
            grad, recv = refs[t], refs[m + t]
            _, pr, pc = recv.shape
            for k in range(1, N_DEV):
                px, py, pcore = _peer(k)
                if axes[t] == 0:
                    piece = grad.at[pl.ds(((2 * px + py) * 2 + pcore) * pr, pr), :]
                else:
                    piece = grad.at[pl.ds(pcore * pr, pr), pl.ds(pl.multiple_of((2 * px + py) * pc, 128), pc)]
                out.append((piece, recv.at[k - 1], (px, py, pcore)))
        return out
    return plan


def _broadcast_plan(refs):
    small, recv = refs
    return [(small, recv.at[k - 1], _peer(k)) for k in range(1, N_DEV)]


def _start_all(plan, refs, send_sems, recv_sems):
    for q, (src, dst, dev) in enumerate(plan(refs)):
        pltpu.make_async_remote_copy(src_ref=src, dst_ref=dst, send_sem=send_sems.at[q], recv_sem=recv_sems.at[q], device_id=dev, device_id_type=MESH).start()


def _wait_all(plan, refs, send_sems, recv_sems):
    for q, (src, dst, dev) in enumerate(plan(refs)):
        cp = pltpu.make_async_remote_copy(src_ref=src, dst_ref=dst, send_sem=send_sems.at[q], recv_sem=recv_sems.at[q], device_id=dev, device_id_type=MESH)
        cp.wait_send()
        cp.wait_recv()


def _push(bufs, plan, ncopies, name):
    n = len(bufs)

    def body(*refs):
        outs = refs[n:2 * n]
        send_sems, recv_sems = refs[2 * n:]
        _start_all(plan, outs, send_sems, recv_sems)
        _wait_all(plan, outs, send_sems, recv_sems)

    return pl.pallas_call(
        body,
        in_specs=[ANY] * n,
        out_specs=[ANY] * n,
        out_shape=[jax.ShapeDtypeStruct(b.shape, b.dtype) for b in bufs],
        input_output_aliases={t: t for t in range(n)},
        scratch_shapes=[pltpu.SemaphoreType.DMA((ncopies,)), pltpu.SemaphoreType.DMA((ncopies,))],
        name=name,
    )(*bufs)


def _gather_once_per_chip(full, name):
    R, C = full.shape
    R2, C4 = R // 2, C // N_CHIPS

    def body(_, ref, send_sems, recv_sems):
        x, y, c = _position()
        chips = [_peer(k)[:2] for k in (4, 2, 6)]

        def half(chip, core):
            return ref.at[pl.ds(core * R2, R2), pl.ds(pl.multiple_of(chip * C4, 128), C4)]

        def copy(q, chip, core, to):
            return pltpu.make_async_remote_copy(src_ref=half(chip, core), dst_ref=half(chip, core), send_sem=send_sems.at[q], recv_sem=recv_sems.at[q],
                                                device_id=to, device_id_type=MESH)

        sends = [copy(q, 2 * x + y, c, (px, py, c)) for q, (px, py) in enumerate(chips)]
        for cp in sends:
            cp.start()
        for q, (px, py) in enumerate(chips):
            copy(q, 2 * px + py, c, (px, py, c)).wait_recv()
            passed = copy(3 + q, 2 * px + py, c, (x, y, 1 - c))
            passed.start()
            sends.append(passed)
        for q, (px, py) in enumerate(chips):
            copy(3 + q, 2 * px + py, 1 - c, (x, y, 1 - c)).wait_recv()
        for cp in sends:
            cp.wait_send()

    return pl.pallas_call(
        body,
        in_specs=[ANY],
        out_specs=ANY,
        out_shape=jax.ShapeDtypeStruct(full.shape, full.dtype),
        input_output_aliases={0: 0},
        scratch_shapes=[pltpu.SemaphoreType.DMA((6,)), pltpu.SemaphoreType.DMA((6,))],
        name=name,
    )(full)


def _push_start(bufs, plan, ncopies, name, after=None):
    n = len(bufs)
    extra = [] if after is None else [after]

    def body(*refs):
        ins = refs[:n]
        first_out = n + len(extra)
        send_sems, recv_sems, token = refs[first_out], refs[first_out + 1], refs[-1]
        _start_all(plan, ins, send_sems, recv_sems)
        token[...] = jnp.zeros_like(token)

    res = pl.pallas_call(
        body,
        name=name,
        out_shape=(pltpu.SemaphoreType.DMA((ncopies,)), pltpu.SemaphoreType.DMA((ncopies,)), *[pltpu.HBM(b.shape, b.dtype) for b in bufs],
                   jax.ShapeDtypeStruct((8, 128), F32)),
        in_specs=[HBM] * n + [ANY] * len(extra),
        out_specs=(SEM, SEM, *[HBM] * n, pl.BlockSpec(memory_space=pltpu.VMEM)),
        input_output_aliases={t: t + 2 for t in range(n)},
        compiler_params=pltpu.CompilerParams(has_side_effects=DATAFLOW),
    )(*[pltpu.with_memory_space_constraint(b, pltpu.HBM) for b in bufs], *extra)
    return res[0], res[1], list(res[2:2 + n]), res[-1]


def _push_wait(send_sems, recv_sems, bufs, plan, after, name):
    n = len(bufs)

    def body(*refs):
        ins = refs[:n]
        _wait_all(plan, ins, refs[n], refs[n + 1])

    return pl.pallas_call(
        body,
        name=name,
        out_shape=tuple(pltpu.HBM(b.shape, b.dtype) for b in bufs),
        in_specs=[HBM] * n + [SEM, SEM, ANY],
        out_specs=tuple([HBM] * n),
        input_output_aliases={t: t for t in range(n)},
        compiler_params=pltpu.CompilerParams(has_side_effects=DATAFLOW),
    )(*bufs, send_sems, recv_sems, after)


EXCHANGE_CHUNKS = 2


def _exchange_plan(refs):
    x, y, c = _position()
    out = []
    for ref in refs:
        rows = ref.shape[0] // (2 * EXCHANGE_CHUNKS)
        for q in range(EXCHANGE_CHUNKS):
            mine = ref.at[pl.ds((c * EXCHANGE_CHUNKS + q) * rows, rows), :]
            out.append((mine, mine, (x, y, 1 - c)))
    return out


LATE_WEIGHTS = (("w_pool_lin", "w_pool_out", "w_attn_out", "w_out"), ("w_up", "conv_w", "w_down"))


def _local_step(x, tgt, w, late_weights, send):
    S, D = x.shape
    PW = w["pool_scale"].shape[1]
    o_q = PW
    o_g = PW + 3 * ATTN_WIDTH
    QKV = 3 * ATTN_WIDTH

    h1 = _rms_fwd(x, w["g_mix"], "rms1")
    proj_tiles = (_tile(S, 1024), 512, D)
    u = _mm(h1, w["w_in"], mode="nn", dims=(S, PW, D), tiles=proj_tiles, out_dtypes=(F32,), name="proj_u")
    qkv = _mm(h1, w["w_in"], mode="nn", dims=(S, QKV, D), tiles=proj_tiles, b_off=(0, o_q), name="proj_qkv")

    def gate_epilogue(acc, ex, outs):
        outs[0][...] = (1.0 / (1.0 + jnp.exp(-(acc + ex[0][...])))).astype(outs[0].dtype)

    gates = _mm(h1, w["w_in"], mode="nn", dims=(S, 2 * D, D), tiles=proj_tiles, b_off=(0, o_g), epilogue=gate_epilogue,
                extras=[(w["b_gate"], "n", (0, 0))], name="proj_gates")

    os_, lses = [], []
    for gi, (_, d) in enumerate(ATTN_GROUPS):
        o, lse = _attn_fwd(qkv, d, gi, f"attn_fwd{gi}")
        os_.append(o)
        lses.append(lse)
    attn, lse_tot = _attn_merge(os_, lses, "attn_merge")

    w = dict(w, **late_weights(0, attn))
    pool_out = _pool_fwd(u, w["w_pool_lin"], w["pool_scale"], "pool_fwd")
    y_pool = _mm(pool_out, w["w_pool_out"], mode="nn", dims=(S, D, PW), name="y_pool")

    def mix_epilogue(acc, ex, outs):
        outs[0][...] = acc.astype(BF16)
        outs[1][...] = (ex[0][...].astype(F32) * ex[2][...].astype(F32) + ex[1][...].astype(F32) * acc).astype(BF16)

    y_attn, mixed = _mm(attn, w["w_attn_out"], mode="nn", dims=(S, D, GROUP_WIDTH), out_dtypes=(BF16, BF16), epilogue=mix_epilogue,
                        extras=[(gates, "mn", (0, 0)), (gates, "mn", (0, D)), (y_pool, "mn", (0, 0))], name="y_attn_mix")

    def residual_epilogue(acc, ex, outs):
        outs[0][...] = ex[0][...] + acc

    x2 = _mm(mixed, w["w_out"], mode="nn", dims=(S, D, D), out_dtypes=(F32,), epilogue=residual_epilogue, extras=[(x, "mn", (0, 0))], name="out_proj")

    h2 = _rms_fwd(x2, w["g_ffn"], "rms2")
    w = dict(w, **late_weights(1, h2))
    F = w["w_down"].shape[0]
    up = _mm(h2, w["w_up"], mode="nn", dims=(S, 2 * F, D), name="up_proj")
    f, act_a, act_b = _convglu_fwd(up, w["conv_w"], w["conv_b"], "convglu_fwd")
    x3 = _mm(f, w["w_down"], mode="nn", dims=(S, D, F), out_dtypes=(F32,), epilogue=residual_epilogue, extras=[(x2, "mn", (0, 0))], name="down_proj")

    g = {}
    dx3, dx3b, g["g_final"], loss_cols = _loss_head(x3, tgt, w["g_final"], "loss_head")

    g["w_down"] = _mm(f, dx3b, mode="tn", dims=(F, D, S), name="dw_down")
    sent = send(("w_down",), g)
    df = _mm(dx3b, w["w_down"], mode="nt", dims=(S, F, D), name="d_f")
    dup, g["conv_b"], g["conv_w"] = _convglu_bwd(df, act_a, act_b, up, w["conv_w"] + sent, "convglu_bwd")
    g["w_up"] = _mm(h2, dup, mode="tn", dims=(D, 2 * F, S), name="dw_up")
    sent = send(("w_up",), g)
    dh2 = _mm(dup, w["w_up"], mode="nt", dims=(S, D, 2 * F), name="d_h2")
    dx2, dx2b, g["g_ffn"] = _rms_bwd(dh2, x2, w["g_ffn"] + sent, dx3, "rms2_bwd", True)

    g["w_out"] = _mm(mixed, dx2b, mode="tn", dims=(D, D, S), name="dw_out")
    dmixed = _mm(dx2b, w["w_out"], mode="nt", dims=(S, D, D), name="d_mixed")
    IN = w["w_in"].shape[1]
    dy_both, dproj, g["b_gate"] = _gate_bwd(dmixed, gates, y_pool, y_attn, IN, "gate_bwd")

    g["w_pool_out"] = _mm(pool_out, dy_both, mode="tn", dims=(PW, D, S), name="dw_pool_out")
    g["w_attn_out"] = _mm(attn, dy_both, mode="tn", dims=(GROUP_WIDTH, D, S), b_off=(0, D), name="dw_attn_out")
    sent = send(("w_out", "w_pool_out", "w_attn_out"), g)
    dpool = _mm(dy_both, w["w_pool_out"], mode="nt", dims=(S, PW, D), name="d_pool")
    dattn = _mm(dy_both, w["w_attn_out"], mode="nt", dims=(S, GROUP_WIDTH, D), a_off=(0, D), name="d_attn")

    dproj, g["w_pool_lin"], g["pool_scale"] = _pool_bwd(u, dpool, w["w_pool_lin"], w["pool_scale"] + sent, dproj, "pool_bwd")
    g["loss_cols"] = loss_cols
    sent = send("small", g)

    for gi, (_, d) in enumerate(ATTN_GROUPS):
        dproj = _attn_bwd(qkv, dattn, attn, lse_tot, dproj, d, gi, PW // HEAD_DIM, f"attn_bwd{gi}")

    g["w_in"] = _mm(h1, dproj, mode="tn", dims=(D, IN, S), name="dw_in")
    sent = sent + send(("w_in",), g)
    dh1 = _mm(dproj, w["w_in"], mode="nt", dims=(S, D, IN), tiles=(_tile(S, 1024), _tile(D, 2048), _tile(IN, 2432)), name="d_h1")
    (grad_x, g["g_mix"]) = _rms_bwd(dh1, x, w["g_mix"] + sent, dx2, "rms1_bwd", False)
    return loss_cols, grad_x, g


BIG = ("w_in", "w_pool_out", "w_attn_out", "w_out", "w_up", "w_down")
BIG_AXIS = {"w_in": 1, "w_pool_out": 1, "w_attn_out": 1, "w_out": 0, "w_up": 1, "w_down": 0}
GATHER_AXIS = dict(BIG_AXIS, w_pool_lin=1, conv_w=1)
SMALL = ("loss_cols", "b_gate", "w_pool_lin", "pool_scale", "g_ffn", "conv_w", "conv_b", "g_final")
SMALL_COLS = 1024
ORDER = ("g_mix", "w_in", "b_gate", "w_pool_lin", "pool_scale", "w_pool_out", "w_attn_out", "w_out", "g_ffn", "w_up", "conv_w", "conv_b", "w_down", "g_final")


def _as_rows(parts):
    flat = jnp.concatenate([p.astype(F32).reshape(-1) for p in parts])
    rows = -(-flat.shape[0] // (8 * SMALL_COLS)) * 8
    return jnp.pad(flat, (0, rows * SMALL_COLS - flat.shape[0])).reshape(rows, SMALL_COLS)


def kernel(x, g_mix, w_in, b_gate, w_pool_lin, pool_scale, w_pool_out, w_attn_out, w_out, g_ffn, w_up, conv_w, conv_b, w_down, g_final, loss_target, m_g_mix, m_w_in, m_b_gate, m_w_pool_lin, m_pool_scale, m_w_pool_out, m_w_attn_out, m_w_out, m_g_ffn, m_w_up, m_conv_w, m_conv_b, m_w_down, m_g_final, v_g_mix, v_w_in, v_b_gate, v_w_pool_lin, v_pool_scale, v_w_pool_out, v_w_attn_out, v_w_out, v_g_ffn, v_w_up, v_conv_w, v_conv_b, v_w_down, v_g_final):
    shard = dict(g_mix=g_mix, w_in=w_in, b_gate=b_gate, w_pool_lin=w_pool_lin, pool_scale=pool_scale, w_pool_out=w_pool_out, w_attn_out=w_attn_out,
                 w_out=w_out, g_ffn=g_ffn, w_up=w_up, conv_w=conv_w, conv_b=conv_b, w_down=w_down, g_final=g_final)
    mom = dict(g_mix=m_g_mix, w_in=m_w_in, b_gate=m_b_gate, w_pool_lin=m_w_pool_lin, pool_scale=m_pool_scale, w_pool_out=m_w_pool_out, w_attn_out=m_w_attn_out,
               w_out=m_w_out, g_ffn=m_g_ffn, w_up=m_w_up, conv_w=m_conv_w, conv_b=m_conv_b, w_down=m_w_down, g_final=m_g_final)
    vel = dict(g_mix=v_g_mix, w_in=v_w_in, b_gate=v_b_gate, w_pool_lin=v_w_pool_lin, pool_scale=v_pool_scale, w_pool_out=v_w_pool_out, w_attn_out=v_w_attn_out,
               w_out=v_w_out, g_ffn=v_g_ffn, w_up=v_w_up, conv_w=v_conv_w, conv_b=v_conv_b, w_down=v_w_down, g_final=v_g_final)
    chip = 2 * lax.axis_index("x") + lax.axis_index("y")
    pos = jnp.stack([chip, lax.axis_index("c")]).astype(jnp.int32)
    me = (2 * chip + lax.axis_index("c")).astype(jnp.int32).reshape(1)
    D = x.shape[2]

    placed = {k: _place(shard[k][0], GATHER_AXIS[k], pos, F32 if k == "conv_w" else BF16, f"place_{k}") for k in GATHER_AXIS}
    w_in_full = _gather_once_per_chip(placed["w_in"], "comm_gather_w_in")
    late, late_token, prior = [], 0.0, w_in_full
    for stage, names in enumerate(LATE_WEIGHTS):
        plan = _gather_plan([GATHER_AXIS[k] for k in names])
        send_sems, recv_sems, bufs, token = _push_start([placed[k] for k in names], plan, 3 * len(names), f"comm_gather_late{stage}_start", after=prior)
        late.append((names, send_sems, recv_sems, bufs, plan))
        late_token, prior = late_token + token[0, 0], token

    def late_weights(stage, after):
        names, send_sems, recv_sems, bufs, plan = late[stage]
        return dict(zip(names, _push_wait(send_sems, recv_sems, bufs, plan, after, f"comm_gather_late{stage}_wait")))

    pending = []

    def send(names, g):
        if names == "small":
            bufs = [_as_rows([g[k] for k in SMALL])]
            bufs.append(lax.empty((N_DEV - 1,) + bufs[0].shape, F32))
            plan, tag = _broadcast_plan, "small"
        else:
            bufs = [g[k] for k in names]
            for k in names:
                R, C = g[k].shape
                piece = (R // (2 * N_CHIPS), C) if BIG_AXIS[k] == 0 else (R // 2, C // N_CHIPS)
                bufs.append(lax.empty((N_DEV - 1,) + piece, BF16))
            plan, tag = _scatter_plan([BIG_AXIS[k] for k in names]), names[0]
        ncopies = (N_DEV - 1) * (len(bufs) // 2)
        send_sems, recv_sems, thru, token = _push_start(bufs, plan, ncopies, f"comm_scatter_start_{tag}")
        pending.append((names, send_sems, recv_sems, thru, plan, tag))
        return token[0, 0]

    w0 = dict(g_mix=shard["g_mix"] + late_token, w_in=w_in_full, b_gate=shard["b_gate"], pool_scale=shard["pool_scale"], g_ffn=shard["g_ffn"],
              conv_b=shard["conv_b"], g_final=shard["g_final"].reshape(1, D))
    _, grad_x, gr = _local_step(x[0], loss_target[0], w0, late_weights, send)

    halves, small_parts = {}, None
    for names, send_sems, recv_sems, thru, plan, tag in pending:
        done = _push_wait(send_sems, recv_sems, thru, plan, grad_x, f"comm_scatter_wait_{tag}")
        if names == "small":
            small_parts = _sum_small(done[0], done[1], me, "sum_small").reshape(-1)
        else:
            m = len(names)
            for t, k in enumerate(names):
                halves[k] = _sum_pieces(done[t], BIG_AXIS[k], done[m + t], pos, f"sum_{k}")
    g_mix_own = _as_rows([gr["g_mix"]])
    _, g_mix_recv = _push([g_mix_own, lax.empty((N_DEV - 1,) + g_mix_own.shape, F32)], _broadcast_plan, N_DEV - 1, "comm_gather_g_mix")
    g_mix_sum = _sum_small(g_mix_own, g_mix_recv, me, "sum_g_mix").reshape(-1)[:D]
    wholes = _push([halves[k] for k in BIG], _exchange_plan, EXCHANGE_CHUNKS * len(BIG), "comm_exchange_halves")

    grads = {"g_mix": g_mix_sum.reshape(shard["g_mix"].shape)}
    for k, whole in zip(BIG, wholes):
        grads[k] = whole.reshape(shard[k].shape)
    off = 0
    loss = None
    for k in SMALL:
        sz = math.prod(gr[k].shape)
        fullg = small_parts[off:off + sz].reshape(gr[k].shape)
        off += sz
        if k == "loss_cols":
            loss = jnp.sum(fullg)
            continue
        if k in ("w_pool_lin", "conv_w"):
            n = shard[k].shape[2]
            fullg = lax.dynamic_slice_in_dim(fullg, chip * n, n, axis=1)
        grads[k] = fullg.reshape(shard[k].shape)

    deltas, new_m, new_v = {}, {}, {}
    for k in ORDER:
        shp = shard[k].shape
        two_d = (-1, shp[-1])
        dl, nm, nv = _adamw(shard[k].reshape(two_d), grads[k].reshape(two_d), mom[k].reshape(two_d), vel[k].reshape(two_d), f"adamw_{k}")
        deltas[k], new_m[k], new_v[k] = dl.reshape(shp), nm.reshape(shp), nv.reshape(shp)

    return (loss, grad_x[None], *[grads[k] for k in ORDER], *[deltas[k] for k in ORDER], *[new_m[k] for k in ORDER], *[new_v[k] for k in ORDER])
```

```python
import functools
import math

import jax
import jax.numpy as jnp
from jax import lax
from jax.experimental import pallas as pl
from jax.experimental.pallas import tpu as pltpu

F32 = jnp.float32
BF16 = jnp.bfloat16

RMS_EPS = 1e-6
POOL_WINDOWS = (2, 4, 8, 16)
ATTN_GROUPS = ((128, 1), (512, 4), (2048, 16))
HEADS_PER_GROUP = 4
HEAD_DIM = 128
N_ATTN_HEADS = HEADS_PER_GROUP * len(ATTN_GROUPS)
SPAN = 128
GROUP_WIDTH = HEADS_PER_GROUP * HEAD_DIM
ATTN_WIDTH = N_ATTN_HEADS * HEAD_DIM
ATTN_SCALE = HEAD_DIM ** -0.5
NEG_BIG = -1e30
ALIBI_SLOPES = tuple(2.0 ** (-8.0 * (h + 1) / N_ATTN_HEADS) for h in range(N_ATTN_HEADS))

ADAM_LR = 0.001
ADAM_B1 = 0.9
ADAM_B2 = 0.999
ADAM_EPS = 1e-08
ADAM_WD = 0.01
ADAM_STEP = 10

INV_SQRT2 = 1.0 / math.sqrt(2.0)
INV_SQRT_2PI = 1.0 / math.sqrt(2.0 * math.pi)

HALO = 16
VMEM_LIMIT = 56 * 1024 * 1024
N_CHIPS = 4
N_DEV = 8
MESH = pl.DeviceIdType.MESH
ANY = pl.BlockSpec(memory_space=pl.ANY)


def _cparams(*sem):
    return pltpu.CompilerParams(dimension_semantics=sem, vmem_limit_bytes=VMEM_LIMIT)


def _tile(n, pref, mult=128):
    t = (min(pref, n) // mult) * mult
    while t >= mult:
        if n % t == 0:
            return t
        t -= mult
    return n


def _dot(a, b, contract):
    return lax.dot_general(a, b, (contract, ((), ())), preferred_element_type=F32)


def _dot_nn(a, b):
    return _dot(a, b, ((1,), (0,)))


def _dot_nt(a, b):
    return _dot(a, b, ((1,), (1,)))


def _mm(a, b, *, mode, dims, name, tiles=None, out_dtypes=(BF16,), epilogue=None, extras=(), a_off=(0, 0), b_off=(0, 0)):
    M, N, K = dims
    if tiles is None:
        tiles = (_tile(M, 1408), _tile(N, 2816), _tile(K, 512)) if mode == "tn" else (_tile(M, 1024), _tile(N, 1536), _tile(K, 2816))
    tm, tn, tk = tiles
    assert M % tm == 0 and N % tn == 0 and K % tk == 0, (name, dims, tiles)
    nk = K // tk
    if mode == "nn":
        ab, bb, contract = (tm, tk), (tk, tn), ((1,), (0,))
        amap = lambda i, j, k: (i + a_off[0] // tm, k + a_off[1] // tk)
        bmap = lambda i, j, k: (k + b_off[0] // tk, j + b_off[1] // tn)
    elif mode == "nt":
        ab, bb, contract = (tm, tk), (tn, tk), ((1,), (1,))
        amap = lambda i, j, k: (i + a_off[0] // tm, k + a_off[1] // tk)
        bmap = lambda i, j, k: (j + b_off[0] // tn, k + b_off[1] // tk)
    else:
        ab, bb, contract = (tk, tm), (tk, tn), ((0,), (0,))
        amap = lambda i, j, k: (k + a_off[0] // tk, i + a_off[1] // tm)
        bmap = lambda i, j, k: (k + b_off[0] // tk, j + b_off[1] // tn)
    assert a_off[0] % ab[0] == 0 and a_off[1] % ab[1] == 0 and b_off[0] % bb[0] == 0 and b_off[1] % bb[1] == 0, name
    in_specs = [pl.BlockSpec(ab, amap), pl.BlockSpec(bb, bmap)]
    ex_arrays = []
    for arr, kind, off in extras:
        if kind == "mn":
            assert off[0] % tm == 0 and off[1] % tn == 0, name
            in_specs.append(pl.BlockSpec((tm, tn), lambda i, j, k, off=off: (i + off[0] // tm, j + off[1] // tn)))
        else:
            assert off[1] % tn == 0, name
            in_specs.append(pl.BlockSpec((1, tn), lambda i, j, k, off=off: (0, j + off[1] // tn)))
        ex_arrays.append(arr)
    ne, no = len(ex_arrays), len(out_dtypes)
    if epilogue is None:
        def epilogue(acc, ex, outs):
            outs[0][...] = acc.astype(outs[0].dtype)

    def body(*refs):
        a_ref, b_ref = refs[0], refs[1]
        ex, outs = refs[2:2 + ne], refs[2 + ne:2 + ne + no]
        if nk == 1:
            epilogue(_dot(a_ref[...], b_ref[...], contract), ex, outs)
            return
        acc = refs[-1]
        k = pl.program_id(2)
        if nk <= 4:
            part = _dot(a_ref[...], b_ref[...], contract)

            @pl.when(k == 0)
            def _():
                acc[...] = part

            @pl.when(jnp.logical_and(k > 0, k < nk - 1))
            def _():
                acc[...] += part

            @pl.when(k == nk - 1)
            def _():
                epilogue(acc[...] + part, ex, outs)
        else:
            @pl.when(k == 0)
            def _():
                acc[...] = _dot(a_ref[...], b_ref[...], contract)

            @pl.when(k > 0)
            def _():
                acc[...] += _dot(a_ref[...], b_ref[...], contract)

            @pl.when(k == nk - 1)
            def _():
                epilogue(acc[...], ex, outs)

    res = pl.pallas_call(
        body,
        grid=(M // tm, N // tn, nk),
        in_specs=in_specs,
        out_specs=[pl.BlockSpec((tm, tn), lambda i, j, k: (i, j)) for _ in out_dtypes],
        out_shape=[jax.ShapeDtypeStruct((M, N), dt) for dt in out_dtypes],
        scratch_shapes=[pltpu.VMEM((tm, tn), F32)] if nk > 1 else [],
        compiler_params=_cparams("parallel", "parallel", "arbitrary"),
        name=name,
    )(a, b, *ex_arrays)
    return res[0] if no == 1 else res


def _rms_fwd(x, g, name):
    S, D = x.shape
    tm = _tile(S, 256)

    def body(x_ref, g_ref, h_ref):
        xv = x_ref[...]
        r = lax.rsqrt(jnp.mean(xv * xv, axis=-1, keepdims=True) + RMS_EPS)
        h_ref[...] = (xv * r * g_ref[...]).astype(h_ref.dtype)

    return pl.pallas_call(
        body,
        grid=(S // tm,),
        in_specs=[pl.BlockSpec((tm, D), lambda i: (i, 0)), pl.BlockSpec((1, D), lambda i: (0, 0))],
        out_specs=pl.BlockSpec((tm, D), lambda i: (i, 0)),
        out_shape=jax.ShapeDtypeStruct((S, D), BF16),
        compiler_params=_cparams("parallel"),
        name=name,
    )(x, g)


def _rms_bwd(dh, x, g, dres, name, with_bf16):
    S, D = x.shape
    tm = _tile(S, 256)

    def body(dh_ref, x_ref, g_ref, dres_ref, *outs):
        dx_ref, dg_ref = outs[0], outs[-1]
        xv = x_ref[...]
        r = lax.rsqrt(jnp.mean(xv * xv, axis=-1, keepdims=True) + RMS_EPS)
        xr = xv * r
        dhv = dh_ref[...].astype(F32)

        @pl.when(pl.program_id(0) == 0)
        def _():
            dg_ref[...] = jnp.zeros_like(dg_ref)

        dg_ref[...] += jnp.sum(dhv * xr, axis=0, keepdims=True)
        u = dhv * g_ref[...]
        c = jnp.mean(u * xr, axis=-1, keepdims=True)
        dx = dres_ref[...] + r * (u - xr * c)
        dx_ref[...] = dx
        if with_bf16:
            outs[1][...] = dx.astype(BF16)

    row = pl.BlockSpec((tm, D), lambda i: (i, 0))
    vec = pl.BlockSpec((1, D), lambda i: (0, 0))
    out_specs = [row] + ([row] if with_bf16 else []) + [vec]
    out_shape = [jax.ShapeDtypeStruct((S, D), F32)] + ([jax.ShapeDtypeStruct((S, D), BF16)] if with_bf16 else []) + [jax.ShapeDtypeStruct((1, D), F32)]
    return pl.pallas_call(
        body,
        grid=(S // tm,),
        in_specs=[row, row, vec, row],
        out_specs=out_specs,
        out_shape=out_shape,
        compiler_params=_cparams("arbitrary"),
        name=name,
    )(dh, x, g, dres)


def _loss_head(x3, tgt, g, name):
    S, D = x3.shape
    tm = _tile(S, 256)

    def body(x_ref, t_ref, g_ref, dx_ref, dxb_ref, dg_ref, loss_ref):
        xv = x_ref[...]
        gv = g_ref[...]
        r = lax.rsqrt(jnp.mean(xv * xv, axis=-1, keepdims=True) + RMS_EPS)
        xr = xv * r
        e = xr * gv - t_ref[...]

        @pl.when(pl.program_id(0) == 0)
        def _():
            dg_ref[...] = jnp.zeros_like(dg_ref)
            loss_ref[...] = jnp.zeros_like(loss_ref)

        loss_ref[...] += jnp.sum(e * e, axis=0, keepdims=True) * (0.5 / D)
        dy = e * (1.0 / D)
        dg_ref[...] += jnp.sum(dy * xr, axis=0, keepdims=True)
        u = dy * gv
        c = jnp.mean(u * xr, axis=-1, keepdims=True)
        dx = r * (u - xr * c)
        dx_ref[...] = dx
        dxb_ref[...] = dx.astype(BF16)

    row = pl.BlockSpec((tm, D), lambda i: (i, 0))
    vec = pl.BlockSpec((1, D), lambda i: (0, 0))
    return pl.pallas_call(
        body,
        grid=(S // tm,),
        in_specs=[row, row, vec],
        out_specs=[row, row, vec, vec],
        out_shape=[jax.ShapeDtypeStruct((S, D), F32), jax.ShapeDtypeStruct((S, D), BF16), jax.ShapeDtypeStruct((1, D), F32), jax.ShapeDtypeStruct((1, D), F32)],
        compiler_params=_cparams("arbitrary"),
        name=name,
    )(x3, tgt, g)


def _conv_taps(cur_ref, halo_ref, w_ref, b_ref, first):
    cur = cur_ref[...].astype(F32)
    halo = jnp.where(first, 0.0, halo_ref[...].astype(F32))
    xx = jnp.concatenate([halo, cur], axis=0)
    p1 = pltpu.roll(xx, 1, 0)[HALO:]
    p2 = pltpu.roll(xx, 2, 0)[HALO:]
    w = w_ref[...]
    y = b_ref[...] + w[0:1] * p2 + w[1:2] * p1 + w[2:3] * cur
    return y, (cur, p1, p2)


def _convglu_fwd(up, cw, cb, name):
    S, F2 = up.shape
    F = F2 // 2
    tm, tn = _tile(S, 512), _tile(F, 512)
    nj, hb = F // tn, tm // HALO

    def body(ua, ub, ha, hb_, wa, wb, ba, bb, f_ref, a_ref, b_ref):
        first = pl.program_id(0) == 0
        a, _ = _conv_taps(ua, ha, wa, ba, first)
        b, _ = _conv_taps(ub, hb_, wb, bb, first)
        f_ref[...] = (0.5 * a * (1.0 + lax.erf(a * INV_SQRT2)) * b).astype(f_ref.dtype)
        a_ref[...] = a.astype(a_ref.dtype)
        b_ref[...] = b.astype(b_ref.dtype)

    tile = pl.BlockSpec((tm, tn), lambda i, j: (i, j))
    return pl.pallas_call(
        body,
        grid=(S // tm, nj),
        in_specs=[
            tile,
            pl.BlockSpec((tm, tn), lambda i, j: (i, j + nj)),
            pl.BlockSpec((HALO, tn), lambda i, j: (jnp.maximum(i * hb - 1, 0), j)),
            pl.BlockSpec((HALO, tn), lambda i, j: (jnp.maximum(i * hb - 1, 0), j + nj)),
            pl.BlockSpec((3, tn), lambda i, j: (0, j)),
            pl.BlockSpec((3, tn), lambda i, j: (0, j + nj)),
            pl.BlockSpec((1, tn), lambda i, j: (0, j)),
            pl.BlockSpec((1, tn), lambda i, j: (0, j + nj)),
        ],
        out_specs=[tile, tile, tile],
        out_shape=[jax.ShapeDtypeStruct((S, F), BF16)] * 3,
        compiler_params=_cparams("parallel", "parallel"),
        name=name,
    )(up, up, up, up, cw, cw, cb, cb)


def _convglu_bwd(df, a, b, up, cw, w_up, name):
    S, F = df.shape
    D = w_up.shape[0]
    tm, tk = _tile(S, 512), _tile(F, 512)
    nj, ni, hb = F // tk, S // tm, tm // HALO
    nk = 2 * nj
    n = tm + HALO

    def body(df_ref, dfn_ref, a_ref, an_ref, b_ref, bn_ref, up_ref, w_ref, wup_ref, o_ref, dh_ref, db_ref, dw_ref, acc):
        i, k = pl.program_id(0), pl.program_id(1)
        last = i == ni - 1

        def rows(c_ref, n_ref):
            return jnp.concatenate([c_ref[...].astype(F32), jnp.where(last, 0.0, n_ref[...].astype(F32))], axis=0)

        def finish(d):
            d0 = d[:tm]
            d1 = pltpu.roll(d, n - 1, 0)[:tm]
            d2 = pltpu.roll(d, n - 2, 0)[:tm]
            w = w_ref[...]
            o_ref[...] = (w[2:3] * d0 + w[1:2] * d1 + w[0:1] * d2).astype(o_ref.dtype)
            upv = up_ref[...].astype(F32)
            db_ref[0] = jnp.sum(d0, axis=0, keepdims=True)
            dw_ref[0, 0:1, :] = jnp.sum(d2 * upv, axis=0, keepdims=True)
            dw_ref[0, 1:2, :] = jnp.sum(d1 * upv, axis=0, keepdims=True)
            dw_ref[0, 2:3, :] = jnp.sum(d0 * upv, axis=0, keepdims=True)

        av, dfv = rows(a_ref, an_ref), rows(df_ref, dfn_ref)
        cdf = 0.5 * (1.0 + lax.erf(av * INV_SQRT2))

        @pl.when(k < nj)
        def _():
            pdf = jnp.exp(-0.5 * av * av) * INV_SQRT_2PI
            finish(dfv * rows(b_ref, bn_ref) * (cdf + av * pdf))

        @pl.when(k >= nj)
        def _():
            finish(dfv * (av * cdf))

        @pl.when(k == 0)
        def _():
            acc[...] = _dot_nt(o_ref[...], wup_ref[...])

        @pl.when(k > 0)
        def _():
            acc[...] += _dot_nt(o_ref[...], wup_ref[...])

        @pl.when(k == nk - 1)
        def _():
            dh_ref[...] = acc[...].astype(dh_ref.dtype)

    kh = lambda k: lax.rem(k, nj)
    nxt = lambda i: jnp.minimum((i + 1) * hb, S // HALO - 1)
    cur = pl.BlockSpec((tm, tk), lambda i, k: (i, kh(k)))
    halo = pl.BlockSpec((HALO, tk), lambda i, k: (nxt(i), kh(k)))
    return pl.pallas_call(
        body,
        grid=(ni, nk),
        in_specs=[cur, halo, cur, halo, cur, halo, pl.BlockSpec((tm, tk), lambda i, k: (i, k)), pl.BlockSpec((3, tk), lambda i, k: (0, k)),
                  pl.BlockSpec((D, tk), lambda i, k: (0, k))],
        out_specs=[pl.BlockSpec((tm, tk), lambda i, k: (i, k)), pl.BlockSpec((tm, D), lambda i, k: (i, 0)),
                   pl.BlockSpec((1, 1, tk), lambda i, k: (i, 0, k)), pl.BlockSpec((1, 3, tk), lambda i, k: (i, 0, k))],
        out_shape=[jax.ShapeDtypeStruct((S, 2 * F), BF16), jax.ShapeDtypeStruct((S, D), BF16), jax.ShapeDtypeStruct((ni, 1, 2 * F), F32),
                   jax.ShapeDtypeStruct((ni, 3, 2 * F), F32)],
        scratch_shapes=[pltpu.VMEM((tm, D), F32)],
        compiler_params=_cparams("parallel", "arbitrary"),
        name=name,
    )(df, df, a, a, b, b, up, cw, w_up)


def _gate_bwd(dmixed, gates, y_pool, y_attn, in_width, name):
    S, D = dmixed.shape
    tm, tn = _tile(S, 512), _tile(D, 512)
    nj = D // tn
    pre0 = (in_width - 2 * D) // tn
    assert pre0 * tn == in_width - 2 * D

    def body(dm_ref, g_ref, yp_ref, ya_ref, dy_ref, dpre_ref, db_ref):
        j = pl.program_id(0)

        @pl.when(pl.program_id(1) == 0)
        def _():
            db_ref[...] = jnp.zeros_like(db_ref)

        def run(y_ref):
            dm = dm_ref[...].astype(F32)
            gv = g_ref[...].astype(F32)
            dy_ref[...] = (dm * gv).astype(BF16)
            dpre = dm * y_ref[...].astype(F32) * gv * (1.0 - gv)
            dpre_ref[...] = dpre.astype(BF16)
            db_ref[...] += jnp.sum(dpre, axis=0, keepdims=True)

        @pl.when(j < nj)
        def _():
            run(yp_ref)

        @pl.when(j >= nj)
        def _():
            run(ya_ref)

    tile2 = pl.BlockSpec((tm, tn), lambda j, i: (i, j))
    return pl.pallas_call(
        body,
        grid=(2 * nj, S // tm),
        in_specs=[
            pl.BlockSpec((tm, tn), lambda j, i: (i, lax.rem(j, nj))),
            tile2,
            pl.BlockSpec((tm, tn), lambda j, i: (i, jnp.minimum(j, nj - 1))),
            pl.BlockSpec((tm, tn), lambda j, i: (i, jnp.maximum(j - nj, 0))),
        ],
        out_specs=[tile2, pl.BlockSpec((tm, tn), lambda j, i: (i, pre0 + j)), pl.BlockSpec((1, tn), lambda j, i: (0, j))],
        out_shape=[jax.ShapeDtypeStruct((S, 2 * D), BF16), jax.ShapeDtypeStruct((S, in_width), BF16), jax.ShapeDtypeStruct((1, 2 * D), F32)],
        compiler_params=_cparams("parallel", "arbitrary"),
        name=name,
    )(dmixed, gates, y_pool, y_attn)


def _pool_counts(i, tm, rows, w):
    t = i * tm + lax.broadcasted_iota(jnp.int32, (rows, 1), 0)
    return jnp.minimum(t + 1, w).astype(F32)


def _pooled_groups(u_ref, uh_ref, i, tm, C):
    cur = u_ref[...]
    halo = jnp.where(i == 0, 0.0, uh_ref[...])
    xx = jnp.concatenate([halo, cur], axis=0)
    out = []
    s = xx
    for gi, w in enumerate(POOL_WINDOWS):
        s = s + pltpu.roll(s, w // 2, 0)
        tot = s[HALO:, 0:C]
        out.append(tot / _pool_counts(i, tm, tm, w) - cur[:, gi * C:(gi + 1) * C])
        s = s[:, C:] if gi + 1 < len(POOL_WINDOWS) else s
    return out


def _pool_fwd(u, wl, scale, name):
    S, PW = u.shape
    C = PW // len(POOL_WINDOWS)
    tm = _tile(S, 512)
    hb = tm // HALO

    def body(u_ref, uh_ref, wl_ref, sc_ref, o_ref):
        i = pl.program_id(0)
        pooled = _pooled_groups(u_ref, uh_ref, i, tm, C)
        for gi in range(len(POOL_WINDOWS)):
            y = _dot_nn(pooled[gi].astype(BF16), wl_ref[gi])
            o_ref[:, gi * C:(gi + 1) * C] = (y * sc_ref[:, gi * C:(gi + 1) * C]).astype(o_ref.dtype)

    return pl.pallas_call(
        body,
        grid=(S // tm,),
        in_specs=[
            pl.BlockSpec((tm, PW), lambda i: (i, 0)),
            pl.BlockSpec((HALO, PW), lambda i: (jnp.maximum(i * hb - 1, 0), 0)),
            pl.BlockSpec((len(POOL_WINDOWS), C, C), lambda i: (0, 0, 0)),
            pl.BlockSpec((1, PW), lambda i: (0, 0)),
        ],
        out_specs=pl.BlockSpec((tm, PW), lambda i: (i, 0)),
        out_shape=jax.ShapeDtypeStruct((S, PW), BF16),
        compiler_params=_cparams("parallel"),
        name=name,
    )(u, u, wl, scale)


def _pool_bwd(u, dp, wl, scale, dproj, name):
    S, PW = u.shape
    G = len(POOL_WINDOWS)
    C = PW // G
    tm = _tile(S, 512)
    hb, ni = tm // HALO, S // tm
    n = tm + HALO

    def body(u_ref, uh_ref, dp_ref, dpn_ref, wl_ref, sc_ref, _, du_ref, dwl_ref, dsc_ref):
        i = pl.program_id(0)

        @pl.when(i == 0)
        def _():
            dwl_ref[...] = jnp.zeros_like(dwl_ref)
            dsc_ref[...] = jnp.zeros_like(dsc_ref)

        pooled = _pooled_groups(u_ref, uh_ref, i, tm, C)
        dpc = dp_ref[...].astype(F32)
        dpn = jnp.where(i == ni - 1, 0.0, dpn_ref[...].astype(F32))
        sc = sc_ref[...]
        dyl = jnp.concatenate([dpc, dpn], axis=0) * sc
        for gi, w in enumerate(POOL_WINDOWS):
            cols = slice(gi * C, (gi + 1) * C)
            pb = pooled[gi].astype(BF16)
            ylin = _dot_nn(pb, wl_ref[gi])
            dsc_ref[:, cols] += jnp.sum(dpc[:, cols] * ylin, axis=0, keepdims=True)
            dylg = dyl[:, cols].astype(BF16)
            dwl_ref[gi] += _dot(pb, dylg[:tm], ((0,), (0,)))
            dpool = _dot_nt(dylg, wl_ref[gi])
            e = dpool / _pool_counts(i, tm, n, w)
            k = 1
            while k < w:
                e = e + pltpu.roll(e, n - k, 0)
                k *= 2
            du_ref[:, cols] = (e[:tm] - dpool[:tm]).astype(du_ref.dtype)

    return pl.pallas_call(
        body,
        grid=(ni,),
        in_specs=[
            pl.BlockSpec((tm, PW), lambda i: (i, 0)),
            pl.BlockSpec((HALO, PW), lambda i: (jnp.maximum(i * hb - 1, 0), 0)),
            pl.BlockSpec((tm, PW), lambda i: (i, 0)),
            pl.BlockSpec((HALO, PW), lambda i: (jnp.minimum((i + 1) * hb, S // HALO - 1), 0)),
            pl.BlockSpec((G, C, C), lambda i: (0, 0, 0)),
            pl.BlockSpec((1, PW), lambda i: (0, 0)),
            ANY,
        ],
        out_specs=[pl.BlockSpec((tm, PW), lambda i: (i, 0)), pl.BlockSpec((G, C, C), lambda i: (0, 0, 0)), pl.BlockSpec((1, PW), lambda i: (0, 0))],
        out_shape=[jax.ShapeDtypeStruct(dproj.shape, dproj.dtype), jax.ShapeDtypeStruct((G, C, C), F32), jax.ShapeDtypeStruct((1, PW), F32)],
        input_output_aliases={6: 0},
        compiler_params=_cparams("arbitrary"),
        name=name,
    )(u, u, dp, dp, wl, scale, dproj)


def _band_masks():
    ii = lax.broadcasted_iota(jnp.int32, (SPAN, SPAN), 0)
    kk = lax.broadcasted_iota(jnp.int32, (SPAN, SPAN), 1)
    return ((ii + SPAN - kk).astype(F32), kk >= ii), ((ii - kk).astype(F32), kk <= ii)


ATTN_TILE = 16 * SPAN


def _unit_rows(r, b, d):
    return pl.ds(d * SPAN * b + r, SPAN, stride=d) if d > 1 else pl.ds(SPAN * b, SPAN)


def _f32_copies(refs, scratch, d):
    if d == 1:
        return list(refs)
    for ref, s in zip(refs, scratch):
        s[...] = ref[...].astype(F32)
    return list(scratch)


def _attn_fwd(qkv, d, g, name):
    S = qkv.shape[0]
    T = min(ATTN_TILE, S)
    P = SPAN * d
    nbk = T // P

    def body(q_ref, k_ref, v_ref, kp_ref, vp_ref, o_ref, lse_ref, *scratch):
        c = pl.program_id(0)
        (jp, mp), (jc, mc) = _band_masks()
        slopes = [ALIBI_SLOPES[g * HEADS_PER_GROUP + h] * d for h in range(HEADS_PER_GROUP)]
        slope = slopes[0]
        for h in range(1, HEADS_PER_GROUP):
            slope = jnp.where(pl.program_id(1) == h, slopes[h], slope)
        q_s, k_s, v_s, kp_s, vp_s = _f32_copies((q_ref, k_ref, v_ref, kp_ref, vp_ref), scratch[:5], d)
        o_s, l_s = (o_ref, lse_ref) if d == 1 else scratch[5:7]
        for r in range(d):
            for b in range(nbk):
                rows = _unit_rows(r, b, d)
                q = q_s[rows, :].astype(BF16)
                kc, vc = k_s[rows, :].astype(BF16), v_s[rows, :].astype(BF16)
                if b == 0:
                    prev = _unit_rows(r, 0, d)
                    kp, vp, okp = kp_s[prev, :].astype(BF16), vp_s[prev, :].astype(BF16), jnp.logical_and(mp, c > 0)
                else:
                    prev = _unit_rows(r, b - 1, d)
                    kp, vp, okp = k_s[prev, :].astype(BF16), v_s[prev, :].astype(BF16), mp
                sc = jnp.where(mc, _dot_nt(q, kc) * ATTN_SCALE - slope * jc, NEG_BIG)
                sp = jnp.where(okp, _dot_nt(q, kp) * ATTN_SCALE - slope * jp, NEG_BIG)
                m = jnp.maximum(jnp.max(sc, axis=-1, keepdims=True), jnp.max(sp, axis=-1, keepdims=True))
                pc, pp = jnp.exp(sc - m), jnp.exp(sp - m)
                l = jnp.sum(pc, axis=-1, keepdims=True) + jnp.sum(pp, axis=-1, keepdims=True)
                o_s[rows, :] = (_dot_nn(pc.astype(BF16), vc) + _dot_nn(pp.astype(BF16), vp)) / l
                l_s[rows, :] = jnp.broadcast_to(m + jnp.log(l), (SPAN, HEAD_DIM))
        if d > 1:
            o_ref[...] = o_s[...]
            lse_ref[...] = l_s[...]

    col = lambda kind: (lambda c, h: (c, kind * N_ATTN_HEADS + g * HEADS_PER_GROUP + h))
    pcol = lambda kind: (lambda c, h: (jnp.maximum(c * nbk - 1, 0), kind * N_ATTN_HEADS + g * HEADS_PER_GROUP + h))
    cur = lambda kind: pl.BlockSpec((T, HEAD_DIM), col(kind))
    prv = lambda kind: pl.BlockSpec((P, HEAD_DIM), pcol(kind))
    out = pl.BlockSpec((T, HEAD_DIM), lambda c, h: (c, h))
    scratch = [] if d == 1 else [pltpu.VMEM((T, HEAD_DIM), F32)] * 3 + [pltpu.VMEM((P, HEAD_DIM), F32)] * 2 + [pltpu.VMEM((T, HEAD_DIM), F32)] * 2
    return pl.pallas_call(
        body,
        grid=(S // T, HEADS_PER_GROUP),
        in_specs=[cur(0), cur(1), cur(2), prv(1), prv(2)],
        out_specs=[out, out],
        out_shape=[jax.ShapeDtypeStruct((S, GROUP_WIDTH), F32)] * 2,
        scratch_shapes=scratch,
        compiler_params=_cparams("parallel", "parallel"),
        name=name,
    )(qkv, qkv, qkv, qkv, qkv)


def _attn_merge(os_, lses, name):
    S, W = os_[0].shape
    tm = _tile(S, 512)

    def body(o0, o1, o2, l0, l1, l2, y_ref, lse_ref):
        ls = [l0[...], l1[...], l2[...]]
        m = jnp.maximum(jnp.maximum(ls[0], ls[1]), ls[2])
        es = [jnp.exp(v - m) for v in ls]
        tot = es[0] + es[1] + es[2]
        y = (es[0] * o0[...] + es[1] * o1[...] + es[2] * o2[...]) / tot
        y_ref[...] = y.astype(y_ref.dtype)
        lse_ref[...] = m + jnp.log(tot)

    row = pl.BlockSpec((tm, W), lambda i: (i, 0))
    return pl.pallas_call(
        body,
        grid=(S // tm,),
        in_specs=[row] * 6,
        out_specs=[row, row],
        out_shape=[jax.ShapeDtypeStruct((S, W), BF16), jax.ShapeDtypeStruct((S, W), F32)],
        compiler_params=_cparams("parallel"),
        name=name,
    )(*os_, *lses)


def _attn_bwd(qkv, dattn, y, lse, dproj, d, g, col0, name):
    S = qkv.shape[0]
    T = min(ATTN_TILE, S)
    P = SPAN * d
    nbk = T // P
    ntile = S // T

    def body(q_ref, k_ref, v_ref, kp_ref, vp_ref, qn_ref, da_ref, dan_ref, y_ref, yn_ref, lse_ref, lsen_ref, _, out_ref, dq_s, dk_s, dv_s, *scratch):
        c = pl.program_id(0)
        head_id = pl.program_id(1)
        kind = pl.program_id(2)

        @pl.when(kind == 0)
        def _():
            (jp, mp), (jc, mc) = _band_masks()
            slopes = [ALIBI_SLOPES[g * HEADS_PER_GROUP + h] * d for h in range(HEADS_PER_GROUP)]
            slope = slopes[0]
            for h in range(1, HEADS_PER_GROUP):
                slope = jnp.where(head_id == h, slopes[h], slope)
            q_s, k_s, v_s, da_s, y_s, kp_s, vp_s, qn_s, dan_s, yn_s = _f32_copies(
                (q_ref, k_ref, v_ref, da_ref, y_ref, kp_ref, vp_ref, qn_ref, dan_ref, yn_ref), scratch, d)
            for r in range(d):
                dq = [None] * nbk
                dk = [None] * nbk
                dv = [None] * nbk

                def add(lst, idx, val):
                    lst[idx] = val if lst[idx] is None else lst[idx] + val

                for qb in range(nbk + 1):
                    if qb < nbk:
                        rows = _unit_rows(r, qb, d)
                        q, da, yy, lse_blk = q_s[rows, :], da_s[rows, :], y_s[rows, :], lse_ref[rows, :]
                    else:
                        rows = _unit_rows(r, 0, d)
                        q, da, yy, lse_blk = qn_s[rows, :], dan_s[rows, :], yn_s[rows, :], lsen_ref[rows, :]
                    lse_col = lse_blk[:, 0:1]
                    dd = jnp.sum(da.astype(F32) * yy.astype(F32), axis=-1, keepdims=True)
                    q, da = q.astype(BF16), da.astype(BF16)
                    for kb in (qb - 1, qb):
                        if kb >= nbk:
                            continue
                        if kb < 0:
                            krows = _unit_rows(r, 0, d)
                            kk, vv, ok = kp_s[krows, :].astype(BF16), vp_s[krows, :].astype(BF16), jnp.logical_and(mp, c > 0)
                        else:
                            krows = _unit_rows(r, kb, d)
                            kk, vv = k_s[krows, :].astype(BF16), v_s[krows, :].astype(BF16)
                            ok = mc if kb == qb else (mp if qb < nbk else jnp.logical_and(mp, c < ntile - 1))
                        jj = jc if kb == qb else jp
                        s = jnp.where(ok, _dot_nt(q, kk) * ATTN_SCALE - slope * jj, NEG_BIG)
                        p = jnp.exp(s - lse_col)
                        ds = p * (_dot_nt(da, vv) - dd)
                        if qb < nbk:
                            add(dq, qb, _dot_nn(ds.astype(BF16), kk))
                        if kb >= 0:
                            add(dv, kb, _dot_nn(p.T.astype(BF16), da))
                            add(dk, kb, _dot_nn(ds.T.astype(BF16), q))
                for b in range(nbk):
                    rows = _unit_rows(r, b, d)
                    dq_s[rows, :] = dq[b] * ATTN_SCALE
                    dk_s[rows, :] = dk[b] * ATTN_SCALE
                    dv_s[rows, :] = dv[b]
            out_ref[...] = dq_s[...].astype(out_ref.dtype)

        @pl.when(kind == 1)
        def _():
            out_ref[...] = dk_s[...].astype(out_ref.dtype)

        @pl.when(kind == 2)
        def _():
            out_ref[...] = dv_s[...].astype(out_ref.dtype)

    head = lambda h: g * HEADS_PER_GROUP + h
    cur = lambda kind: pl.BlockSpec((T, HEAD_DIM), lambda c, h, kd: (c, kind * N_ATTN_HEADS + head(h)))
    prv = lambda kind: pl.BlockSpec((P, HEAD_DIM), lambda c, h, kd: (jnp.maximum(c * nbk - 1, 0), kind * N_ATTN_HEADS + head(h)))
    nxt_row = lambda c: jnp.minimum((c + 1) * nbk, S // P - 1)
    qnext = pl.BlockSpec((P, HEAD_DIM), lambda c, h, kd: (nxt_row(c), head(h)))
    hcur = pl.BlockSpec((T, HEAD_DIM), lambda c, h, kd: (c, h))
    hnext = pl.BlockSpec((P, HEAD_DIM), lambda c, h, kd: (nxt_row(c), h))
    out = pl.BlockSpec((T, HEAD_DIM), lambda c, h, kd: (c, col0 + kd * N_ATTN_HEADS + head(h)))
    stage = [pltpu.VMEM((T, HEAD_DIM), F32)] * 3
    copies = [] if d == 1 else [pltpu.VMEM((T, HEAD_DIM), F32)] * 5 + [pltpu.VMEM((P, HEAD_DIM), F32)] * 5
    return pl.pallas_call(
        body,
        grid=(ntile, HEADS_PER_GROUP, 3),
        in_specs=[cur(0), cur(1), cur(2), prv(1), prv(2), qnext, hcur, hnext, hcur, hnext, hcur, hnext, ANY],
        out_specs=out,
        out_shape=jax.ShapeDtypeStruct(dproj.shape, dproj.dtype),
        input_output_aliases={12: 0},
        scratch_shapes=stage + copies,
        compiler_params=_cparams("parallel", "parallel", "arbitrary"),
        name=name,
    )(qkv, qkv, qkv, qkv, qkv, qkv, dattn, dattn, y, y, lse, lse, dproj)


def _row_block(R, C, bytes_per_row_elem=4, budget=1 << 20):
    if R % 8:
        return R
    best = 8
    t = 8
    while t <= R:
        if R % t == 0 and t * C * bytes_per_row_elem <= budget:
            best = t
        t += 8
    return best


def _adamw(w, g, m, v, name):
    R, C = w.shape
    tr = _row_block(R, C)
    c1 = 1.0 - ADAM_B1 ** ADAM_STEP
    c2 = 1.0 - ADAM_B2 ** ADAM_STEP

    def body(w_ref, g_ref, m_ref, v_ref, d_ref, nm_ref, nv_ref):
        gv = g_ref[...]
        nm = ADAM_B1 * m_ref[...] + (1.0 - ADAM_B1) * gv
        nv = ADAM_B2 * v_ref[...] + (1.0 - ADAM_B2) * (gv * gv)
        d_ref[...] = -ADAM_LR * ((nm / c1) / (jnp.sqrt(nv / c2) + ADAM_EPS) + ADAM_WD * w_ref[...])
        nm_ref[...] = nm
        nv_ref[...] = nv

    blk = pl.BlockSpec((tr, C), lambda i: (i, 0))
    return pl.pallas_call(
        body,
        grid=(R // tr,),
        in_specs=[blk] * 4,
        out_specs=[blk] * 3,
        out_shape=[jax.ShapeDtypeStruct((R, C), F32)] * 3,
        compiler_params=_cparams("parallel"),
        name=name,
    )(w, g, m, v)


def _sum_pieces(grad, axis, recv, pos, name):
    n, pr, pc = recv.shape
    tr = _row_block(pr, pc, bytes_per_row_elem=(n + 1) * recv.dtype.itemsize, budget=4 << 20)
    nblk = pr // tr
    if axis == 1:
        own_map = lambda i, p: (p[1] * nblk + i, p[0])
    else:
        own_map = lambda i, p: ((2 * p[0] + p[1]) * nblk + i, 0)

    def body(p_ref, own_ref, r_ref, o_ref):
        acc = own_ref[...].astype(F32)
        for s in range(n):
            acc = acc + r_ref[s].astype(F32)
        o_ref[...] = acc

    return pl.pallas_call(
        body,
        grid_spec=pltpu.PrefetchScalarGridSpec(
            num_scalar_prefetch=1,
            grid=(nblk,),
            in_specs=[pl.BlockSpec((tr, pc), own_map), pl.BlockSpec((n, tr, pc), lambda i, p: (0, i, 0))],
            out_specs=pl.BlockSpec((tr, pc), lambda i, p: (p[1] * nblk + i, 0)),
        ),
        out_shape=jax.ShapeDtypeStruct((2 * pr, pc), F32),
        compiler_params=_cparams("parallel"),
        name=name,
    )(pos, grad, recv)


def _sum_small(own, recv, me, name):
    n, R, C = recv.shape
    tr = _row_block(R, C, bytes_per_row_elem=(n + 1) * 4, budget=4 << 20)

    def body(me_ref, own_ref, r_ref, o_ref):
        acc = None
        for dev in range(n + 1):
            k = jnp.bitwise_xor(me_ref[0], dev)
            term = jnp.where(k == 0, own_ref[...], r_ref[jnp.maximum(k - 1, 0)])
            acc = term if acc is None else acc + term
        o_ref[...] = acc

    return pl.pallas_call(
        body,
        grid_spec=pltpu.PrefetchScalarGridSpec(
            num_scalar_prefetch=1,
            grid=(R // tr,),
            in_specs=[pl.BlockSpec((tr, C), lambda i, m: (i, 0)), pl.BlockSpec((n, tr, C), lambda i, m: (0, i, 0))],
            out_specs=pl.BlockSpec((tr, C), lambda i, m: (i, 0)),
        ),
        out_shape=jax.ShapeDtypeStruct((R, C), F32),
        compiler_params=_cparams("parallel"),
        name=name,
    )(me, own, recv)


def _place(shard, axis, pos, dtype, name):
    shp = list(shard.shape)
    shp[axis] *= N_CHIPS
    if shard.ndim == 3:
        assert axis == 1
        in_spec = pl.BlockSpec(shard.shape, lambda i, p: (0, 0, 0))
        out_spec = pl.BlockSpec(shard.shape, lambda i, p: (0, p[0], 0))
        grid = (1,)
    else:
        R, C = shard.shape
        tr = _row_block(R, C, bytes_per_row_elem=4, budget=2 << 20)
        nblk = R // tr
        in_spec = pl.BlockSpec((tr, C), lambda i, p: (i, 0))
        out_spec = pl.BlockSpec((tr, C), (lambda i, p: (i, p[0])) if axis == 1 else (lambda i, p: (p[0] * nblk + i, 0)))
        grid = (nblk,)

    def body(p_ref, s_ref, o_ref):
        o_ref[...] = s_ref[...].astype(o_ref.dtype)

    return pl.pallas_call(
        body,
        grid_spec=pltpu.PrefetchScalarGridSpec(num_scalar_prefetch=1, grid=grid, in_specs=[in_spec], out_specs=out_spec),
        out_shape=jax.ShapeDtypeStruct(tuple(shp), dtype),
        compiler_params=_cparams("parallel"),
        name=name,
    )(pos, shard)


HBM = pl.BlockSpec(memory_space=pltpu.HBM)
SEM = pl.BlockSpec(memory_space=pltpu.SEMAPHORE)
DATAFLOW = pltpu.SideEffectType.DATAFLOW_SIDE_EFFECTING


def _position():
    return lax.axis_index("x"), lax.axis_index("y"), lax.axis_index("c")


def _peer(k):
    x, y, c = _position()
    return ((1 - x) if k & 4 else x, (1 - y) if k & 2 else y, (1 - c) if k & 1 else c)


def _shard_slice(ref, axis, idx, size):
    start = idx * size
    if axis == ref.ndim - 1:
        start = pl.multiple_of(start, 128)
    ix = [slice(None)] * ref.ndim
    ix[axis] = pl.ds(start, size)
    return ref.at[tuple(ix)]


def _gather_plan(axes):
    def plan(refs):
        x, y, c = _position()
        out = []
        for ref, ax in zip(refs, axes):
            mine = _shard_slice(ref, ax, 2 * x + y, ref.shape[ax] // N_CHIPS)
            for k in (4, 2, 6):
                px, py, _ = _peer(k)
                out.append((mine, mine, (px, py, c)))
        return out
    return plan


def _scatter_plan(axes):
    m = len(axes)

    def plan(refs):
        out = []
        for t in range(m):
            grad, recv = refs[t], refs[m + t]
            _, pr, pc = recv.shape
            for k in range(1, N_DEV):
                px, py, pcore = _peer(k)
                if axes[t] == 0:
                    piece = grad.at[pl.ds(((2 * px + py) * 2 + pcore) * pr, pr), :]
                else:
                    piece = grad.at[pl.ds(pcore * pr, pr), pl.ds(pl.multiple_of((2 * px + py) * pc, 128), pc)]
                out.append((piece, recv.at[k - 1], (px, py, pcore)))
        return out
    return plan


def _broadcast_plan(refs):
    small, recv = refs
    return [(small, recv.at[k - 1], _peer(k)) for k in range(1, N_DEV)]


def _start_all(plan, refs, send_sems, recv_sems):
    for q, (src, dst, dev) in enumerate(plan(refs)):
        pltpu.make_async_remote_copy(src_ref=src, dst_ref=dst, send_sem=send_sems.at[q], recv_sem=recv_sems.at[q], device_id=dev, device_id_type=MESH).start()


def _wait_all(plan, refs, send_sems, recv_sems):
    for q, (src, dst, dev) in enumerate(plan(refs)):
        cp = pltpu.make_async_remote_copy(src_ref=src, dst_ref=dst, send_sem=send_sems.at[q], recv_sem=recv_sems.at[q], device_id=dev, device_id_type=MESH)
        cp.wait_send()
        cp.wait_recv()


def _push(bufs, plan, ncopies, name):
    n = len(bufs)

    def body(*refs):
        outs = refs[n:2 * n]
        send_sems, recv_sems = refs[2 * n:]
        _start_all(plan, outs, send_sems, recv_sems)
        _wait_all(plan, outs, send_sems, recv_sems)

    return pl.pallas_call(
        body,
        in_specs=[ANY] * n,
        out_specs=[ANY] * n,
        out_shape=[jax.ShapeDtypeStruct(b.shape, b.dtype) for b in bufs],
        input_output_aliases={t: t for t in range(n)},
        scratch_shapes=[pltpu.SemaphoreType.DMA((ncopies,)), pltpu.SemaphoreType.DMA((ncopies,))],
        name=name,
    )(*bufs)


def _gather_once_per_chip(full, name):
    R, C = full.shape
    R2, C4 = R // 2, C // N_CHIPS

    def body(_, ref, send_sems, recv_sems):
        x, y, c = _position()
        chips = [_peer(k)[:2] for k in (4, 2, 6)]

        def half(chip, core):
            return ref.at[pl.ds(core * R2, R2), pl.ds(pl.multiple_of(chip * C4, 128), C4)]

        def copy(q, chip, core, to):
            return pltpu.make_async_remote_copy(src_ref=half(chip, core), dst_ref=half(chip, core), send_sem=send_sems.at[q], recv_sem=recv_sems.at[q],
                                                device_id=to, device_id_type=MESH)

        sends = [copy(q, 2 * x + y, c, (px, py, c)) for q, (px, py) in enumerate(chips)]
        for cp in sends:
            cp.start()
        for q, (px, py) in enumerate(chips):
            copy(q, 2 * px + py, c, (px, py, c)).wait_recv()
            passed = copy(3 + q, 2 * px + py, c, (x, y, 1 - c))
            passed.start()
            sends.append(passed)
        for q, (px, py) in enumerate(chips):
            copy(3 + q, 2 * px + py, 1 - c, (x, y, 1 - c)).wait_recv()
        for cp in sends:
            cp.wait_send()

    return pl.pallas_call(
        body,
        in_specs=[ANY],
        out_specs=ANY,
        out_shape=jax.ShapeDtypeStruct(full.shape, full.dtype),
        input_output_aliases={0: 0},
        scratch_shapes=[pltpu.SemaphoreType.DMA((6,)), pltpu.SemaphoreType.DMA((6,))],
        name=name,
    )(full)


def _push_start(bufs, plan, ncopies, name, after=None):
    n = len(bufs)
    extra = [] if after is None else [after]

    def body(*refs):
        ins = refs[:n]
        first_out = n + len(extra)
        send_sems, recv_sems, token = refs[first_out], refs[first_out + 1], refs[-1]
        _start_all(plan, ins, send_sems, recv_sems)
        token[...] = jnp.zeros_like(token)

    res = pl.pallas_call(
        body,
        name=name,
        out_shape=(pltpu.SemaphoreType.DMA((ncopies,)), pltpu.SemaphoreType.DMA((ncopies,)), *[pltpu.HBM(b.shape, b.dtype) for b in bufs],
                   jax.ShapeDtypeStruct((8, 128), F32)),
        in_specs=[HBM] * n + [ANY] * len(extra),
        out_specs=(SEM, SEM, *[HBM] * n, pl.BlockSpec(memory_space=pltpu.VMEM)),
        input_output_aliases={t: t + 2 for t in range(n)},
        compiler_params=pltpu.CompilerParams(has_side_effects=DATAFLOW),
    )(*[pltpu.with_memory_space_constraint(b, pltpu.HBM) for b in bufs], *extra)
    return res[0], res[1], list(res[2:2 + n]), res[-1]


def _push_wait(send_sems, recv_sems, bufs, plan, after, name):
    n = len(bufs)

    def body(*refs):
        ins = refs[:n]
        _wait_all(plan, ins, refs[n], refs[n + 1])

    return pl.pallas_call(
        body,
        name=name,
        out_shape=tuple(pltpu.HBM(b.shape, b.dtype) for b in bufs),
        in_specs=[HBM] * n + [SEM, SEM, ANY],
        out_specs=tuple([HBM] * n),
        input_output_aliases={t: t for t in range(n)},
        compiler_params=pltpu.CompilerParams(has_side_effects=DATAFLOW),
    )(*bufs, send_sems, recv_sems, after)


EXCHANGE_CHUNKS = 2


def _exchange_plan(refs):
    x, y, c = _position()
    out = []
    for ref in refs:
        rows = ref.shape[0] // (2 * EXCHANGE_CHUNKS)
        for q in range(EXCHANGE_CHUNKS):
            mine = ref.at[pl.ds((c * EXCHANGE_CHUNKS + q) * rows, rows), :]
            out.append((mine, mine, (x, y, 1 - c)))
    return out


LATE_WEIGHTS = (("w_pool_lin", "w_pool_out", "w_attn_out", "w_out"), ("w_up", "conv_w", "w_down"))


def _local_step(x, tgt, w, late_weights, send):
    S, D = x.shape
    PW = w["pool_scale"].shape[1]
    o_q = PW
    o_g = PW + 3 * ATTN_WIDTH
    QKV = 3 * ATTN_WIDTH

    h1 = _rms_fwd(x, w["g_mix"], "rms1")
    proj_tiles = (_tile(S, 1024), 512, D)
    u = _mm(h1, w["w_in"], mode="nn", dims=(S, PW, D), tiles=proj_tiles, out_dtypes=(F32,), name="proj_u")
    qkv = _mm(h1, w["w_in"], mode="nn", dims=(S, QKV, D), tiles=proj_tiles, b_off=(0, o_q), name="proj_qkv")

    def gate_epilogue(acc, ex, outs):
        outs[0][...] = (1.0 / (1.0 + jnp.exp(-(acc + ex[0][...])))).astype(outs[0].dtype)

    gates = _mm(h1, w["w_in"], mode="nn", dims=(S, 2 * D, D), tiles=proj_tiles, b_off=(0, o_g), epilogue=gate_epilogue,
                extras=[(w["b_gate"], "n", (0, 0))], name="proj_gates")

    os_, lses = [], []
    for gi, (_, d) in enumerate(ATTN_GROUPS):
        o, lse = _attn_fwd(qkv, d, gi, f"attn_fwd{gi}")
        os_.append(o)
        lses.append(lse)
    attn, lse_tot = _attn_merge(os_, lses, "attn_merge")

    w = dict(w, **late_weights(0, attn))
    pool_out = _pool_fwd(u, w["w_pool_lin"], w["pool_scale"], "pool_fwd")
    y_pool = _mm(pool_out, w["w_pool_out"], mode="nn", dims=(S, D, PW), name="y_pool")

    def mix_epilogue(acc, ex, outs):
        outs[0][...] = acc.astype(BF16)
        outs[1][...] = (ex[0][...].astype(F32) * ex[2][...].astype(F32) + ex[1][...].astype(F32) * acc).astype(BF16)

    y_attn, mixed = _mm(attn, w["w_attn_out"], mode="nn", dims=(S, D, GROUP_WIDTH), out_dtypes=(BF16, BF16), epilogue=mix_epilogue,
                        extras=[(gates, "mn", (0, 0)), (gates, "mn", (0, D)), (y_pool, "mn", (0, 0))], name="y_attn_mix")

    def residual_epilogue(acc, ex, outs):
        outs[0][...] = ex[0][...] + acc

    x2 = _mm(mixed, w["w_out"], mode="nn", dims=(S, D, D), out_dtypes=(F32,), epilogue=residual_epilogue, extras=[(x, "mn", (0, 0))], name="out_proj")

    h2 = _rms_fwd(x2, w["g_ffn"], "rms2")
    w = dict(w, **late_weights(1, h2))
    F = w["w_down"].shape[0]
    up = _mm(h2, w["w_up"], mode="nn", dims=(S, 2 * F, D), name="up_proj")
    f, act_a, act_b = _convglu_fwd(up, w["conv_w"], w["conv_b"], "convglu_fwd")
    x3 = _mm(f, w["w_down"], mode="nn", dims=(S, D, F), out_dtypes=(F32,), epilogue=residual_epilogue, extras=[(x2, "mn", (0, 0))], name="down_proj")

    g = {}
    dx3, dx3b, g["g_final"], loss_cols = _loss_head(x3, tgt, w["g_final"], "loss_head")

    g["w_down"] = _mm(f, dx3b, mode="tn", dims=(F, D, S), name="dw_down")
    sent = send(("w_down",), g)
    df = _mm(dx3b, w["w_down"], mode="nt", dims=(S, F, D), name="d_f")
    dup, dh2, dcb_rows, dcw_rows = _convglu_bwd(df, act_a, act_b, up, w["conv_w"] + sent, w["w_up"], "convglu_bwd_d_h2")
    g["conv_b"], g["conv_w"] = jnp.sum(dcb_rows, axis=0), jnp.sum(dcw_rows, axis=0)
    g["w_up"] = _mm(h2, dup, mode="tn", dims=(D, 2 * F, S), name="dw_up")
    sent = send(("w_up",), g)
    dx2, dx2b, g["g_ffn"] = _rms_bwd(dh2, x2, w["g_ffn"] + sent, dx3, "rms2_bwd", True)

    g["w_out"] = _mm(mixed, dx2b, mode="tn", dims=(D, D, S), name="dw_out")
    dmixed = _mm(dx2b, w["w_out"], mode="nt", dims=(S, D, D), name="d_mixed")
    IN = w["w_in"].shape[1]
    dy_both, dproj, g["b_gate"] = _gate_bwd(dmixed, gates, y_pool, y_attn, IN, "gate_bwd")

    g["w_pool_out"] = _mm(pool_out, dy_both, mode="tn", dims=(PW, D, S), name="dw_pool_out")
    g["w_attn_out"] = _mm(attn, dy_both, mode="tn", dims=(GROUP_WIDTH, D, S), b_off=(0, D), name="dw_attn_out")
    sent = send(("w_out", "w_pool_out", "w_attn_out"), g)
    dpool = _mm(dy_both, w["w_pool_out"], mode="nt", dims=(S, PW, D), name="d_pool")
    dattn = _mm(dy_both, w["w_attn_out"], mode="nt", dims=(S, GROUP_WIDTH, D), a_off=(0, D), name="d_attn")

    dproj, g["w_pool_lin"], g["pool_scale"] = _pool_bwd(u, dpool, w["w_pool_lin"], w["pool_scale"] + sent, dproj, "pool_bwd")
    g["loss_cols"] = loss_cols
    sent = send("small", g)

    for gi, (_, d) in enumerate(ATTN_GROUPS):
        dproj = _attn_bwd(qkv, dattn, attn, lse_tot, dproj, d, gi, PW // HEAD_DIM, f"attn_bwd{gi}")

    g["w_in"] = _mm(h1, dproj, mode="tn", dims=(D, IN, S), name="dw_in")
    sent = sent + send(("w_in",), g)
    dh1 = _mm(dproj, w["w_in"], mode="nt", dims=(S, D, IN), tiles=(_tile(S, 1024), _tile(D, 2048), _tile(IN, 2432)), name="d_h1")
    (grad_x, g["g_mix"]) = _rms_bwd(dh1, x, w["g_mix"] + sent, dx2, "rms1_bwd", False)
    return loss_cols, grad_x, g


BIG = ("w_in", "w_pool_out", "w_attn_out", "w_out", "w_up", "w_down")
BIG_AXIS = {"w_in": 1, "w_pool_out": 1, "w_attn_out": 1, "w_out": 0, "w_up": 1, "w_down": 0}
GATHER_AXIS = dict(BIG_AXIS, w_pool_lin=1, conv_w=1)
SMALL = ("loss_cols", "b_gate", "w_pool_lin", "pool_scale", "g_ffn", "conv_w", "conv_b", "g_final")
SMALL_COLS = 1024
ORDER = ("g_mix", "w_in", "b_gate", "w_pool_lin", "pool_scale", "w_pool_out", "w_attn_out", "w_out", "g_ffn", "w_up", "conv_w", "conv_b", "w_down", "g_final")


def _as_rows(parts):
    flat = jnp.concatenate([p.astype(F32).reshape(-1) for p in parts])
    rows = -(-flat.shape[0] // (8 * SMALL_COLS)) * 8
    return jnp.pad(flat, (0, rows * SMALL_COLS - flat.shape[0])).reshape(rows, SMALL_COLS)


def kernel(x, g_mix, w_in, b_gate, w_pool_lin, pool_scale, w_pool_out, w_attn_out, w_out, g_ffn, w_up, conv_w, conv_b, w_down, g_final, loss_target, m_g_mix, m_w_in, m_b_gate, m_w_pool_lin, m_pool_scale, m_w_pool_out, m_w_attn_out, m_w_out, m_g_ffn, m_w_up, m_conv_w, m_conv_b, m_w_down, m_g_final, v_g_mix, v_w_in, v_b_gate, v_w_pool_lin, v_pool_scale, v_w_pool_out, v_w_attn_out, v_w_out, v_g_ffn, v_w_up, v_conv_w, v_conv_b, v_w_down, v_g_final):
    shard = dict(g_mix=g_mix, w_in=w_in, b_gate=b_gate, w_pool_lin=w_pool_lin, pool_scale=pool_scale, w_pool_out=w_pool_out, w_attn_out=w_attn_out,
                 w_out=w_out, g_ffn=g_ffn, w_up=w_up, conv_w=conv_w, conv_b=conv_b, w_down=w_down, g_final=g_final)
    mom = dict(g_mix=m_g_mix, w_in=m_w_in, b_gate=m_b_gate, w_pool_lin=m_w_pool_lin, pool_scale=m_pool_scale, w_pool_out=m_w_pool_out, w_attn_out=m_w_attn_out,
               w_out=m_w_out, g_ffn=m_g_ffn, w_up=m_w_up, conv_w=m_conv_w, conv_b=m_conv_b, w_down=m_w_down, g_final=m_g_final)
    vel = dict(g_mix=v_g_mix, w_in=v_w_in, b_gate=v_b_gate, w_pool_lin=v_w_pool_lin, pool_scale=v_pool_scale, w_pool_out=v_w_pool_out, w_attn_out=v_w_attn_out,
               w_out=v_w_out, g_ffn=v_g_ffn, w_up=v_w_up, conv_w=v_conv_w, conv_b=v_conv_b, w_down=v_w_down, g_final=v_g_final)
    chip = 2 * lax.axis_index("x") + lax.axis_index("y")
    pos = jnp.stack([chip, lax.axis_index("c")]).astype(jnp.int32)
    me = (2 * chip + lax.axis_index("c")).astype(jnp.int32).reshape(1)
    D = x.shape[2]

    placed = {k: _place(shard[k][0], GATHER_AXIS[k], pos, F32 if k == "conv_w" else BF16, f"place_{k}") for k in GATHER_AXIS}
    w_in_full = _gather_once_per_chip(placed["w_in"], "comm_gather_w_in")
    late, late_token, prior = [], 0.0, w_in_full
    for stage, names in enumerate(LATE_WEIGHTS):
        plan = _gather_plan([GATHER_AXIS[k] for k in names])
        send_sems, recv_sems, bufs, token = _push_start([placed[k] for k in names], plan, 3 * len(names), f"comm_gather_late{stage}_start", after=prior)
        late.append((names, send_sems, recv_sems, bufs, plan))
        late_token, prior = late_token + token[0, 0], token

    def late_weights(stage, after):
        names, send_sems, recv_sems, bufs, plan = late[stage]
        return dict(zip(names, _push_wait(send_sems, recv_sems, bufs, plan, after, f"comm_gather_late{stage}_wait")))

    pending = []

    def send(names, g):
        if names == "small":
            bufs = [_as_rows([g[k] for k in SMALL])]
            bufs.append(lax.empty((N_DEV - 1,) + bufs[0].shape, F32))
            plan, tag = _broadcast_plan, "small"
        else:
            bufs = [g[k] for k in names]
            for k in names:
                R, C = g[k].shape
                piece = (R // (2 * N_CHIPS), C) if BIG_AXIS[k] == 0 else (R // 2, C // N_CHIPS)
                bufs.append(lax.empty((N_DEV - 1,) + piece, BF16))
            plan, tag = _scatter_plan([BIG_AXIS[k] for k in names]), names[0]
        ncopies = (N_DEV - 1) * (len(bufs) // 2)
        send_sems, recv_sems, thru, token = _push_start(bufs, plan, ncopies, f"comm_scatter_start_{tag}")
        pending.append((names, send_sems, recv_sems, thru, plan, tag))
        return token[0, 0]

    w0 = dict(g_mix=shard["g_mix"] + late_token, w_in=w_in_full, b_gate=shard["b_gate"], pool_scale=shard["pool_scale"], g_ffn=shard["g_ffn"],
              conv_b=shard["conv_b"], g_final=shard["g_final"].reshape(1, D))
    _, grad_x, gr = _local_step(x[0], loss_target[0], w0, late_weights, send)

    halves, small_parts = {}, None
    for names, send_sems, recv_sems, thru, plan, tag in pending:
        done = _push_wait(send_sems, recv_sems, thru, plan, grad_x, f"comm_scatter_wait_{tag}")
        if names == "small":
            small_parts = _sum_small(done[0], done[1], me, "sum_small").reshape(-1)
        else:
            m = len(names)
            for t, k in enumerate(names):
                halves[k] = _sum_pieces(done[t], BIG_AXIS[k], done[m + t], pos, f"sum_{k}")
    g_mix_own = _as_rows([gr["g_mix"]])
    _, g_mix_recv = _push([g_mix_own, lax.empty((N_DEV - 1,) + g_mix_own.shape, F32)], _broadcast_plan, N_DEV - 1, "comm_gather_g_mix")
    g_mix_sum = _sum_small(g_mix_own, g_mix_recv, me, "sum_g_mix").reshape(-1)[:D]
    wholes = _push([halves[k] for k in BIG], _exchange_plan, EXCHANGE_CHUNKS * len(BIG), "comm_exchange_halves")

    grads = {"g_mix": g_mix_sum.reshape(shard["g_mix"].shape)}
    for k, whole in zip(BIG, wholes):
        grads[k] = whole.reshape(shard[k].shape)
    off = 0
    loss = None
    for k in SMALL:
        sz = math.prod(gr[k].shape)
        fullg = small_parts[off:off + sz].reshape(gr[k].shape)
        off += sz
        if k == "loss_cols":
            loss = jnp.sum(fullg)
            continue
        if k in ("w_pool_lin", "conv_w"):
            n = shard[k].shape[2]
            fullg = lax.dynamic_slice_in_dim(fullg, chip * n, n, axis=1)
        grads[k] = fullg.reshape(shard[k].shape)

    deltas, new_m, new_v = {}, {}, {}
    for k in ORDER:
        shp = shard[k].shape
        two_d = (-1, shp[-1])
        dl, nm, nv = _adamw(shard[k].reshape(two_d), grads[k].reshape(two_d), mom[k].reshape(two_d), vel[k].reshape(two_d), f"adamw_{k}")
        deltas[k], new_m[k], new_v[k] = dl.reshape(shp), nm.reshape(shp), nv.reshape(shp)

    return (loss, grad_x[None], *[grads[k] for k in ORDER], *[deltas[k] for k in ORDER], *[new_m[k] for k in ORDER], *[new_v[k] for k in ORDER])
```

```python
import functools
import math

import jax
import jax.numpy as jnp
from jax import lax
from jax.experimental import pallas as pl
from jax.experimental.pallas import tpu as pltpu

F32 = jnp.float32
BF16 = jnp.bfloat16

RMS_EPS = 1e-6
POOL_WINDOWS = (2, 4, 8, 16)
ATTN_GROUPS = ((128, 1), (512, 4), (2048, 16))
HEADS_PER_GROUP = 4
HEAD_DIM = 128
N_ATTN_HEADS = HEADS_PER_GROUP * len(ATTN_GROUPS)
SPAN = 128
GROUP_WIDTH = HEADS_PER_GROUP * HEAD_DIM
ATTN_WIDTH = N_ATTN_HEADS * HEAD_DIM
ATTN_SCALE = HEAD_DIM ** -0.5
NEG_BIG = -1e30
ALIBI_SLOPES = tuple(2.0 ** (-8.0 * (h + 1) / N_ATTN_HEADS) for h in range(N_ATTN_HEADS))

ADAM_LR = 0.001
ADAM_B1 = 0.9
ADAM_B2 = 0.999
ADAM_EPS = 1e-08
ADAM_WD = 0.01
ADAM_STEP = 10

INV_SQRT2 = 1.0 / math.sqrt(2.0)
INV_SQRT_2PI = 1.0 / math.sqrt(2.0 * math.pi)

HALO = 16
VMEM_LIMIT = 56 * 1024 * 1024
N_CHIPS = 4
N_DEV = 8
MESH = pl.DeviceIdType.MESH
ANY = pl.BlockSpec(memory_space=pl.ANY)


def _cparams(*sem):
    return pltpu.CompilerParams(dimension_semantics=sem, vmem_limit_bytes=VMEM_LIMIT)


def _tile(n, pref, mult=128):
    t = (min(pref, n) // mult) * mult
    while t >= mult:
        if n % t == 0:
            return t
        t -= mult
    return n


def _dot(a, b, contract):
    return lax.dot_general(a, b, (contract, ((), ())), preferred_element_type=F32)


def _dot_nn(a, b):
    return _dot(a, b, ((1,), (0,)))


def _dot_nt(a, b):
    return _dot(a, b, ((1,), (1,)))


def _mm(a, b, *, mode, dims, name, tiles=None, out_dtypes=(BF16,), epilogue=None, extras=(), a_off=(0, 0), b_off=(0, 0)):
    M, N, K = dims
    if tiles is None:
        tiles = (_tile(M, 1408), _tile(N, 2816), _tile(K, 512)) if mode == "tn" else (_tile(M, 1024), _tile(N, 1536), _tile(K, 2816))
    tm, tn, tk = tiles
    assert M % tm == 0 and N % tn == 0 and K % tk == 0, (name, dims, tiles)
    nk = K // tk
    if mode == "nn":
        ab, bb, contract = (tm, tk), (tk, tn), ((1,), (0,))
        amap = lambda i, j, k: (i + a_off[0] // tm, k + a_off[1] // tk)
        bmap = lambda i, j, k: (k + b_off[0] // tk, j + b_off[1] // tn)
    elif mode == "nt":
        ab, bb, contract = (tm, tk), (tn, tk), ((1,), (1,))
        amap = lambda i, j, k: (i + a_off[0] // tm, k + a_off[1] // tk)
        bmap = lambda i, j, k: (j + b_off[0] // tn, k + b_off[1] // tk)
    else:
        ab, bb, contract = (tk, tm), (tk, tn), ((0,), (0,))
        amap = lambda i, j, k: (k + a_off[0] // tk, i + a_off[1] // tm)
        bmap = lambda i, j, k: (k + b_off[0] // tk, j + b_off[1] // tn)
    assert a_off[0] % ab[0] == 0 and a_off[1] % ab[1] == 0 and b_off[0] % bb[0] == 0 and b_off[1] % bb[1] == 0, name
    in_specs = [pl.BlockSpec(ab, amap), pl.BlockSpec(bb, bmap)]
    ex_arrays = []
    for arr, kind, off in extras:
        if kind == "mn":
            assert off[0] % tm == 0 and off[1] % tn == 0, name
            in_specs.append(pl.BlockSpec((tm, tn), lambda i, j, k, off=off: (i + off[0] // tm, j + off[1] // tn)))
        else:
            assert off[1] % tn == 0, name
            in_specs.append(pl.BlockSpec((1, tn), lambda i, j, k, off=off: (0, j + off[1] // tn)))
        ex_arrays.append(arr)
    ne, no = len(ex_arrays), len(out_dtypes)
    if epilogue is None:
        def epilogue(acc, ex, outs):
            outs[0][...] = acc.astype(outs[0].dtype)

    def body(*refs):
        a_ref, b_ref = refs[0], refs[1]
        ex, outs = refs[2:2 + ne], refs[2 + ne:2 + ne + no]
        if nk == 1:
            epilogue(_dot(a_ref[...], b_ref[...], contract), ex, outs)
            return
        acc = refs[-1]
        k = pl.program_id(2)
        if nk <= 4:
            part = _dot(a_ref[...], b_ref[...], contract)

            @pl.when(k == 0)
            def _():
                acc[...] = part

            @pl.when(jnp.logical_and(k > 0, k < nk - 1))
            def _():
                acc[...] += part

            @pl.when(k == nk - 1)
            def _():
                epilogue(acc[...] + part, ex, outs)
        else:
            @pl.when(k == 0)
            def _():
                acc[...] = _dot(a_ref[...], b_ref[...], contract)

            @pl.when(k > 0)
            def _():
                acc[...] += _dot(a_ref[...], b_ref[...], contract)

            @pl.when(k == nk - 1)
            def _():
                epilogue(acc[...], ex, outs)

    res = pl.pallas_call(
        body,
        grid=(M // tm, N // tn, nk),
        in_specs=in_specs,
        out_specs=[pl.BlockSpec((tm, tn), lambda i, j, k: (i, j)) for _ in out_dtypes],
        out_shape=[jax.ShapeDtypeStruct((M, N), dt) for dt in out_dtypes],
        scratch_shapes=[pltpu.VMEM((tm, tn), F32)] if nk > 1 else [],
        compiler_params=_cparams("parallel", "parallel", "arbitrary"),
        name=name,
    )(a, b, *ex_arrays)
    return res[0] if no == 1 else res


def _rms_fwd(x, g, name):
    S, D = x.shape
    tm = _tile(S, 256)

    def body(x_ref, g_ref, h_ref):
        xv = x_ref[...]
        r = lax.rsqrt(jnp.mean(xv * xv, axis=-1, keepdims=True) + RMS_EPS)
        h_ref[...] = (xv * r * g_ref[...]).astype(h_ref.dtype)

    return pl.pallas_call(
        body,
        grid=(S // tm,),
        in_specs=[pl.BlockSpec((tm, D), lambda i: (i, 0)), pl.BlockSpec((1, D), lambda i: (0, 0))],
        out_specs=pl.BlockSpec((tm, D), lambda i: (i, 0)),
        out_shape=jax.ShapeDtypeStruct((S, D), BF16),
        compiler_params=_cparams("parallel"),
        name=name,
    )(x, g)


def _rms_bwd(dh, x, g, dres, name, with_bf16):
    S, D = x.shape
    tm = _tile(S, 256)

    def body(dh_ref, x_ref, g_ref, dres_ref, *outs):
        dx_ref, dg_ref = outs[0], outs[-1]
        xv = x_ref[...]
        r = lax.rsqrt(jnp.mean(xv * xv, axis=-1, keepdims=True) + RMS_EPS)
        xr = xv * r
        dhv = dh_ref[...].astype(F32)

        @pl.when(pl.program_id(0) == 0)
        def _():
            dg_ref[...] = jnp.zeros_like(dg_ref)

        dg_ref[...] += jnp.sum(dhv * xr, axis=0, keepdims=True)
        u = dhv * g_ref[...]
        c = jnp.mean(u * xr, axis=-1, keepdims=True)
        dx = dres_ref[...] + r * (u - xr * c)
        dx_ref[...] = dx
        if with_bf16:
            outs[1][...] = dx.astype(BF16)

    row = pl.BlockSpec((tm, D), lambda i: (i, 0))
    vec = pl.BlockSpec((1, D), lambda i: (0, 0))
    out_specs = [row] + ([row] if with_bf16 else []) + [vec]
    out_shape = [jax.ShapeDtypeStruct((S, D), F32)] + ([jax.ShapeDtypeStruct((S, D), BF16)] if with_bf16 else []) + [jax.ShapeDtypeStruct((1, D), F32)]
    return pl.pallas_call(
        body,
        grid=(S // tm,),
        in_specs=[row, row, vec, row],
        out_specs=out_specs,
        out_shape=out_shape,
        compiler_params=_cparams("arbitrary"),
        name=name,
    )(dh, x, g, dres)


def _loss_head(x3, tgt, g, name):
    S, D = x3.shape
    tm = _tile(S, 256)

    def body(x_ref, t_ref, g_ref, dx_ref, dxb_ref, dg_ref, loss_ref):
        xv = x_ref[...]
        gv = g_ref[...]
        r = lax.rsqrt(jnp.mean(xv * xv, axis=-1, keepdims=True) + RMS_EPS)
        xr = xv * r
        e = xr * gv - t_ref[...]

        @pl.when(pl.program_id(0) == 0)
        def _():
            dg_ref[...] = jnp.zeros_like(dg_ref)
            loss_ref[...] = jnp.zeros_like(loss_ref)

        loss_ref[...] += jnp.sum(e * e, axis=0, keepdims=True) * (0.5 / D)
        dy = e * (1.0 / D)
        dg_ref[...] += jnp.sum(dy * xr, axis=0, keepdims=True)
        u = dy * gv
        c = jnp.mean(u * xr, axis=-1, keepdims=True)
        dx = r * (u - xr * c)
        dx_ref[...] = dx
        dxb_ref[...] = dx.astype(BF16)

    row = pl.BlockSpec((tm, D), lambda i: (i, 0))
    vec = pl.BlockSpec((1, D), lambda i: (0, 0))
    return pl.pallas_call(
        body,
        grid=(S // tm,),
        in_specs=[row, row, vec],
        out_specs=[row, row, vec, vec],
        out_shape=[jax.ShapeDtypeStruct((S, D), F32), jax.ShapeDtypeStruct((S, D), BF16), jax.ShapeDtypeStruct((1, D), F32), jax.ShapeDtypeStruct((1, D), F32)],
        compiler_params=_cparams("arbitrary"),
        name=name,
    )(x3, tgt, g)


def _conv_taps(cur_ref, halo_ref, w_ref, b_ref, first):
    cur = cur_ref[...].astype(F32)
    halo = jnp.where(first, 0.0, halo_ref[...].astype(F32))
    xx = jnp.concatenate([halo, cur], axis=0)
    p1 = pltpu.roll(xx, 1, 0)[HALO:]
    p2 = pltpu.roll(xx, 2, 0)[HALO:]
    w = w_ref[...]
    y = b_ref[...] + w[0:1] * p2 + w[1:2] * p1 + w[2:3] * cur
    return y, (cur, p1, p2)


def _convglu_fwd(up, cw, cb, name):
    S, F2 = up.shape
    F = F2 // 2
    tm, tn = _tile(S, 512), _tile(F, 512)
    nj, hb = F // tn, tm // HALO

    def body(ua, ub, ha, hb_, wa, wb, ba, bb, f_ref, a_ref, b_ref):
        first = pl.program_id(0) == 0
        a, _ = _conv_taps(ua, ha, wa, ba, first)
        b, _ = _conv_taps(ub, hb_, wb, bb, first)
        f_ref[...] = (0.5 * a * (1.0 + lax.erf(a * INV_SQRT2)) * b).astype(f_ref.dtype)
        a_ref[...] = a.astype(a_ref.dtype)
        b_ref[...] = b.astype(b_ref.dtype)

    tile = pl.BlockSpec((tm, tn), lambda i, j: (i, j))
    return pl.pallas_call(
        body,
        grid=(S // tm, nj),
        in_specs=[
            tile,
            pl.BlockSpec((tm, tn), lambda i, j: (i, j + nj)),
            pl.BlockSpec((HALO, tn), lambda i, j: (jnp.maximum(i * hb - 1, 0), j)),
            pl.BlockSpec((HALO, tn), lambda i, j: (jnp.maximum(i * hb - 1, 0), j + nj)),
            pl.BlockSpec((3, tn), lambda i, j: (0, j)),
            pl.BlockSpec((3, tn), lambda i, j: (0, j + nj)),
            pl.BlockSpec((1, tn), lambda i, j: (0, j)),
            pl.BlockSpec((1, tn), lambda i, j: (0, j + nj)),
        ],
        out_specs=[tile, tile, tile],
        out_shape=[jax.ShapeDtypeStruct((S, F), BF16)] * 3,
        compiler_params=_cparams("parallel", "parallel"),
        name=name,
    )(up, up, up, up, cw, cw, cb, cb)


def _convglu_bwd(df, a, b, up, cw, name):
    S, F = df.shape
    tm, tn = _tile(S, 512), _tile(F, 512)
    nj, ni, hb = F // tn, S // tm, tm // HALO
    n = tm + HALO

    def body(df_ref, dfn_ref, a_ref, an_ref, b_ref, bn_ref, up_ref, w_ref, o_ref, db_ref, dw_ref):
        j, i = pl.program_id(0), pl.program_id(1)
        last = i == ni - 1

        def rows(c_ref, n_ref):
            return jnp.concatenate([c_ref[...].astype(F32), jnp.where(last, 0.0, n_ref[...].astype(F32))], axis=0)

        @pl.when(i == 0)
        def _():
            db_ref[...] = jnp.zeros_like(db_ref)
            dw_ref[...] = jnp.zeros_like(dw_ref)

        def finish(d):
            d0 = d[:tm]
            d1 = pltpu.roll(d, n - 1, 0)[:tm]
            d2 = pltpu.roll(d, n - 2, 0)[:tm]
            w = w_ref[...]
            o_ref[...] = (w[2:3] * d0 + w[1:2] * d1 + w[0:1] * d2).astype(o_ref.dtype)
            upv = up_ref[...].astype(F32)
            db_ref[...] += jnp.sum(d0, axis=0, keepdims=True)
            dw_ref[0:1, :] += jnp.sum(d2 * upv, axis=0, keepdims=True)
            dw_ref[1:2, :] += jnp.sum(d1 * upv, axis=0, keepdims=True)
            dw_ref[2:3, :] += jnp.sum(d0 * upv, axis=0, keepdims=True)

        av, dfv = rows(a_ref, an_ref), rows(df_ref, dfn_ref)
        cdf = 0.5 * (1.0 + lax.erf(av * INV_SQRT2))

        @pl.when(j < nj)
        def _():
            pdf = jnp.exp(-0.5 * av * av) * INV_SQRT_2PI
            finish(dfv * rows(b_ref, bn_ref) * (cdf + av * pdf))

        @pl.when(j >= nj)
        def _():
            finish(dfv * (av * cdf))

    jh = lambda j: lax.rem(j, nj)
    nxt = lambda i: jnp.minimum((i + 1) * hb, S // HALO - 1)
    cur = pl.BlockSpec((tm, tn), lambda j, i: (i, jh(j)))
    halo = pl.BlockSpec((HALO, tn), lambda j, i: (nxt(i), jh(j)))
    return pl.pallas_call(
        body,
        grid=(2 * nj, ni),
        in_specs=[cur, halo, cur, halo, cur, halo, pl.BlockSpec((tm, tn), lambda j, i: (i, j)), pl.BlockSpec((3, tn), lambda j, i: (0, j))],
        out_specs=[pl.BlockSpec((tm, tn), lambda j, i: (i, j)), pl.BlockSpec((1, tn), lambda j, i: (0, j)), pl.BlockSpec((3, tn), lambda j, i: (0, j))],
        out_shape=[jax.ShapeDtypeStruct((S, 2 * F), BF16), jax.ShapeDtypeStruct((1, 2 * F), F32), jax.ShapeDtypeStruct((3, 2 * F), F32)],
        compiler_params=_cparams("parallel", "arbitrary"),
        name=name,
    )(df, df, a, a, b, b, up, cw)


def _gate_bwd(dmixed, gates, y_pool, y_attn, in_width, name):
    S, D = dmixed.shape
    tm, tn = _tile(S, 512), _tile(D, 512)
    nj = D // tn
    pre0 = (in_width - 2 * D) // tn
    assert pre0 * tn == in_width - 2 * D

    def body(dm_ref, g_ref, yp_ref, ya_ref, dy_ref, dpre_ref, db_ref):
        j = pl.program_id(0)

        @pl.when(pl.program_id(1) == 0)
        def _():
            db_ref[...] = jnp.zeros_like(db_ref)

        def run(y_ref):
            dm = dm_ref[...].astype(F32)
            gv = g_ref[...].astype(F32)
            dy_ref[...] = (dm * gv).astype(BF16)
            dpre = dm * y_ref[...].astype(F32) * gv * (1.0 - gv)
            dpre_ref[...] = dpre.astype(BF16)
            db_ref[...] += jnp.sum(dpre, axis=0, keepdims=True)

        @pl.when(j < nj)
        def _():
            run(yp_ref)

        @pl.when(j >= nj)
        def _():
            run(ya_ref)

    tile2 = pl.BlockSpec((tm, tn), lambda j, i: (i, j))
    return pl.pallas_call(
        body,
        grid=(2 * nj, S // tm),
        in_specs=[
            pl.BlockSpec((tm, tn), lambda j, i: (i, lax.rem(j, nj))),
            tile2,
            pl.BlockSpec((tm, tn), lambda j, i: (i, jnp.minimum(j, nj - 1))),
            pl.BlockSpec((tm, tn), lambda j, i: (i, jnp.maximum(j - nj, 0))),
        ],
        out_specs=[tile2, pl.BlockSpec((tm, tn), lambda j, i: (i, pre0 + j)), pl.BlockSpec((1, tn), lambda j, i: (0, j))],
        out_shape=[jax.ShapeDtypeStruct((S, 2 * D), BF16), jax.ShapeDtypeStruct((S, in_width), BF16), jax.ShapeDtypeStruct((1, 2 * D), F32)],
        compiler_params=_cparams("parallel", "arbitrary"),
        name=name,
    )(dmixed, gates, y_pool, y_attn)


def _pool_counts(i, tm, rows, w):
    t = i * tm + lax.broadcasted_iota(jnp.int32, (rows, 1), 0)
    return jnp.minimum(t + 1, w).astype(F32)


def _pooled_groups(u_ref, uh_ref, i, tm, C):
    cur = u_ref[...]
    halo = jnp.where(i == 0, 0.0, uh_ref[...])
    xx = jnp.concatenate([halo, cur], axis=0)
    out = []
    s = xx
    for gi, w in enumerate(POOL_WINDOWS):
        s = s + pltpu.roll(s, w // 2, 0)
        tot = s[HALO:, 0:C]
        out.append(tot / _pool_counts(i, tm, tm, w) - cur[:, gi * C:(gi + 1) * C])
        s = s[:, C:] if gi + 1 < len(POOL_WINDOWS) else s
    return out


def _pool_fwd(u, wl, scale, name):
    S, PW = u.shape
    C = PW // len(POOL_WINDOWS)
    tm = _tile(S, 512)
    hb = tm // HALO

    def body(u_ref, uh_ref, wl_ref, sc_ref, o_ref):
        i = pl.program_id(0)
        pooled = _pooled_groups(u_ref, uh_ref, i, tm, C)
        for gi in range(len(POOL_WINDOWS)):
            y = _dot_nn(pooled[gi].astype(BF16), wl_ref[gi])
            o_ref[:, gi * C:(gi + 1) * C] = (y * sc_ref[:, gi * C:(gi + 1) * C]).astype(o_ref.dtype)

    return pl.pallas_call(
        body,
        grid=(S // tm,),
        in_specs=[
            pl.BlockSpec((tm, PW), lambda i: (i, 0)),
            pl.BlockSpec((HALO, PW), lambda i: (jnp.maximum(i * hb - 1, 0), 0)),
            pl.BlockSpec((len(POOL_WINDOWS), C, C), lambda i: (0, 0, 0)),
            pl.BlockSpec((1, PW), lambda i: (0, 0)),
        ],
        out_specs=pl.BlockSpec((tm, PW), lambda i: (i, 0)),
        out_shape=jax.ShapeDtypeStruct((S, PW), BF16),
        compiler_params=_cparams("parallel"),
        name=name,
    )(u, u, wl, scale)


def _pool_bwd(u, dp, wl, scale, dproj, name):
    S, PW = u.shape
    G = len(POOL_WINDOWS)
    C = PW // G
    tm = _tile(S, 512)
    hb, ni = tm // HALO, S // tm
    n = tm + HALO

    def body(u_ref, uh_ref, dp_ref, dpn_ref, wl_ref, sc_ref, _, du_ref, dwl_ref, dsc_ref):
        i = pl.program_id(0)

        @pl.when(i == 0)
        def _():
            dwl_ref[...] = jnp.zeros_like(dwl_ref)
            dsc_ref[...] = jnp.zeros_like(dsc_ref)

        pooled = _pooled_groups(u_ref, uh_ref, i, tm, C)
        dpc = dp_ref[...].astype(F32)
        dpn = jnp.where(i == ni - 1, 0.0, dpn_ref[...].astype(F32))
        sc = sc_ref[...]
        dyl = jnp.concatenate([dpc, dpn], axis=0) * sc
        for gi, w in enumerate(POOL_WINDOWS):
            cols = slice(gi * C, (gi + 1) * C)
            pb = pooled[gi].astype(BF16)
            ylin = _dot_nn(pb, wl_ref[gi])
            dsc_ref[:, cols] += jnp.sum(dpc[:, cols] * ylin, axis=0, keepdims=True)
            dylg = dyl[:, cols].astype(BF16)
            dwl_ref[gi] += _dot(pb, dylg[:tm], ((0,), (0,)))
            dpool = _dot_nt(dylg, wl_ref[gi])
            e = dpool / _pool_counts(i, tm, n, w)
            k = 1
            while k < w:
                e = e + pltpu.roll(e, n - k, 0)
                k *= 2
            du_ref[:, cols] = (e[:tm] - dpool[:tm]).astype(du_ref.dtype)

    return pl.pallas_call(
        body,
        grid=(ni,),
        in_specs=[
            pl.BlockSpec((tm, PW), lambda i: (i, 0)),
            pl.BlockSpec((HALO, PW), lambda i: (jnp.maximum(i * hb - 1, 0), 0)),
            pl.BlockSpec((tm, PW), lambda i: (i, 0)),
            pl.BlockSpec((HALO, PW), lambda i: (jnp.minimum((i + 1) * hb, S // HALO - 1), 0)),
            pl.BlockSpec((G, C, C), lambda i: (0, 0, 0)),
            pl.BlockSpec((1, PW), lambda i: (0, 0)),
            ANY,
        ],
        out_specs=[pl.BlockSpec((tm, PW), lambda i: (i, 0)), pl.BlockSpec((G, C, C), lambda i: (0, 0, 0)), pl.BlockSpec((1, PW), lambda i: (0, 0))],
        out_shape=[jax.ShapeDtypeStruct(dproj.shape, dproj.dtype), jax.ShapeDtypeStruct((G, C, C), F32), jax.ShapeDtypeStruct((1, PW), F32)],
        input_output_aliases={6: 0},
        compiler_params=_cparams("arbitrary"),
        name=name,
    )(u, u, dp, dp, wl, scale, dproj)


def _band_masks():
    ii = lax.broadcasted_iota(jnp.int32, (SPAN, SPAN), 0)
    kk = lax.broadcasted_iota(jnp.int32, (SPAN, SPAN), 1)
    return ((ii + SPAN - kk).astype(F32), kk >= ii), ((ii - kk).astype(F32), kk <= ii)


ATTN_TILE = 16 * SPAN


def _unit_rows(r, b, d, blocks=1):
    return pl.ds(d * SPAN * b + r, blocks * SPAN, stride=d) if d > 1 else pl.ds(SPAN * b, blocks * SPAN)


def _f32_copies(refs, scratch, d):
    if d == 1:
        return list(refs)
    for ref, s in zip(refs, scratch):
        s[...] = ref[...].astype(F32)
    return list(scratch)


def _attn_fwd(qkv, d, g, name):
    S = qkv.shape[0]
    T = min(ATTN_TILE, S)
    P = SPAN * d
    nbk = T // P

    def body(q_ref, k_ref, v_ref, kp_ref, vp_ref, o_ref, lse_ref, *scratch):
        c = pl.program_id(0)
        (jp, mp), (jc, mc) = _band_masks()
        slopes = [ALIBI_SLOPES[g * HEADS_PER_GROUP + h] * d for h in range(HEADS_PER_GROUP)]
        slope = slopes[0]
        for h in range(1, HEADS_PER_GROUP):
            slope = jnp.where(pl.program_id(1) == h, slopes[h], slope)
        q_s, k_s, v_s, kp_s, vp_s = _f32_copies((q_ref, k_ref, v_ref, kp_ref, vp_ref), scratch[:5], d)
        o_s, l_s = (o_ref, lse_ref) if d == 1 else scratch[5:7]
        bias_p, bias_c = jnp.where(mp, -slope * jp, NEG_BIG), jnp.where(mc, -slope * jc, NEG_BIG)
        bias = jnp.concatenate([bias_p, bias_c], axis=1)
        bias_first = jnp.concatenate([jnp.where(c > 0, bias_p, NEG_BIG), bias_c], axis=1)
        for r in range(d):
            for b in range(nbk):
                rows = _unit_rows(r, b, d)
                q = q_s[rows, :].astype(BF16)
                if b == 0:
                    prev = _unit_rows(r, 0, d)
                    kk = jnp.concatenate([kp_s[prev, :], k_s[rows, :]], axis=0).astype(BF16)
                    vv = jnp.concatenate([vp_s[prev, :], v_s[rows, :]], axis=0).astype(BF16)
                else:
                    both = _unit_rows(r, b - 1, d, 2)
                    kk, vv = k_s[both, :].astype(BF16), v_s[both, :].astype(BF16)
                s = _dot_nt(q, kk) * ATTN_SCALE + (bias_first if b == 0 else bias)
                m = jnp.max(s, axis=-1, keepdims=True)
                p = jnp.exp(s - m)
                l = jnp.sum(p, axis=-1, keepdims=True)
                o_s[rows, :] = _dot_nn(p.astype(BF16), vv) / l
                l_s[rows, :] = jnp.broadcast_to(m + jnp.log(l), (SPAN, HEAD_DIM))
        if d > 1:
            o_ref[...] = o_s[...]
            lse_ref[...] = l_s[...]

    col = lambda kind: (lambda c, h: (c, kind * N_ATTN_HEADS + g * HEADS_PER_GROUP + h))
    pcol = lambda kind: (lambda c, h: (jnp.maximum(c * nbk - 1, 0), kind * N_ATTN_HEADS + g * HEADS_PER_GROUP + h))
    cur = lambda kind: pl.BlockSpec((T, HEAD_DIM), col(kind))
    prv = lambda kind: pl.BlockSpec((P, HEAD_DIM), pcol(kind))
    out = pl.BlockSpec((T, HEAD_DIM), lambda c, h: (c, h))
    scratch = [] if d == 1 else [pltpu.VMEM((T, HEAD_DIM), F32)] * 3 + [pltpu.VMEM((P, HEAD_DIM), F32)] * 2 + [pltpu.VMEM((T, HEAD_DIM), F32)] * 2
    return pl.pallas_call(
        body,
        grid=(S // T, HEADS_PER_GROUP),
        in_specs=[cur(0), cur(1), cur(2), prv(1), prv(2)],
        out_specs=[out, out],
        out_shape=[jax.ShapeDtypeStruct((S, GROUP_WIDTH), F32)] * 2,
        scratch_shapes=scratch,
        compiler_params=_cparams("parallel", "parallel"),
        name=name,
    )(qkv, qkv, qkv, qkv, qkv)


def _attn_merge(os_, lses, name):
    S, W = os_[0].shape
    tm = _tile(S, 512)

    def body(o0, o1, o2, l0, l1, l2, y_ref, lse_ref):
        ls = [l0[...], l1[...], l2[...]]
        m = jnp.maximum(jnp.maximum(ls[0], ls[1]), ls[2])
        es = [jnp.exp(v - m) for v in ls]
        tot = es[0] + es[1] + es[2]
        y = (es[0] * o0[...] + es[1] * o1[...] + es[2] * o2[...]) / tot
        y_ref[...] = y.astype(y_ref.dtype)
        lse_ref[...] = m + jnp.log(tot)

    row = pl.BlockSpec((tm, W), lambda i: (i, 0))
    return pl.pallas_call(
        body,
        grid=(S // tm,),
        in_specs=[row] * 6,
        out_specs=[row, row],
        out_shape=[jax.ShapeDtypeStruct((S, W), BF16), jax.ShapeDtypeStruct((S, W), F32)],
        compiler_params=_cparams("parallel"),
        name=name,
    )(*os_, *lses)


def _attn_bwd(qkv, dattn, y, lse, dproj, d, g, col0, name):
    S = qkv.shape[0]
    T = min(ATTN_TILE, S)
    P = SPAN * d
    nbk = T // P
    ntile = S // T

    def body(q_ref, k_ref, v_ref, kp_ref, vp_ref, qn_ref, da_ref, dan_ref, y_ref, yn_ref, lse_ref, lsen_ref, _, out_ref, dq_s, dk_s, dv_s, *scratch):
        c = pl.program_id(0)
        head_id = pl.program_id(1)
        kind = pl.program_id(2)

        @pl.when(kind == 0)
        def _():
            (jp, mp), (jc, mc) = _band_masks()
            slopes = [ALIBI_SLOPES[g * HEADS_PER_GROUP + h] * d for h in range(HEADS_PER_GROUP)]
            slope = slopes[0]
            for h in range(1, HEADS_PER_GROUP):
                slope = jnp.where(head_id == h, slopes[h], slope)
            q_s, k_s, v_s, da_s, y_s, kp_s, vp_s, qn_s, dan_s, yn_s = _f32_copies(
                (q_ref, k_ref, v_ref, da_ref, y_ref, kp_ref, vp_ref, qn_ref, dan_ref, yn_ref), scratch, d)
            bias_p, bias_c = jnp.where(mp, -slope * jp, NEG_BIG), jnp.where(mc, -slope * jc, NEG_BIG)
            bias = jnp.concatenate([bias_c, bias_p], axis=0)
            bias_last = jnp.concatenate([bias_c, jnp.where(c < ntile - 1, bias_p, NEG_BIG)], axis=0)
            bias_first = jnp.where(c > 0, bias_p, NEG_BIG)

            def pair(q, da, yy, lse_blk, kk, vv, b):
                dd = jnp.sum(da.astype(F32) * yy.astype(F32), axis=-1, keepdims=True)
                p = jnp.exp(_dot_nt(q, kk) * ATTN_SCALE + b - lse_blk[:, 0:1])
                return p, p * (_dot_nt(da, vv) - dd)

            for r in range(d):
                first = _unit_rows(r, 0, d)
                kk, vv = kp_s[first, :].astype(BF16), vp_s[first, :].astype(BF16)
                _, ds = pair(q_s[first, :].astype(BF16), da_s[first, :].astype(BF16), y_s[first, :], lse_ref[first, :], kk, vv, bias_first)
                dq_next = _dot_nn(ds.astype(BF16), kk)
                for kb in range(nbk):
                    rows = _unit_rows(r, kb, d)
                    if kb + 1 < nbk:
                        both = _unit_rows(r, kb, d, 2)
                        q, da, yy, lse_blk = q_s[both, :], da_s[both, :], y_s[both, :], lse_ref[both, :]
                    else:
                        q = jnp.concatenate([q_s[rows, :], qn_s[first, :]], axis=0)
                        da = jnp.concatenate([da_s[rows, :], dan_s[first, :]], axis=0)
                        yy = jnp.concatenate([y_s[rows, :], yn_s[first, :]], axis=0)
                        lse_blk = jnp.concatenate([lse_ref[rows, :], lsen_ref[first, :]], axis=0)
                    q, da = q.astype(BF16), da.astype(BF16)
                    kk, vv = k_s[rows, :].astype(BF16), v_s[rows, :].astype(BF16)
                    p, ds = pair(q, da, yy, lse_blk, kk, vv, bias if kb + 1 < nbk else bias_last)
                    dv_s[rows, :] = _dot_nn(p.T.astype(BF16), da)
                    dk_s[rows, :] = _dot_nn(ds.T.astype(BF16), q) * ATTN_SCALE
                    dq_both = _dot_nn(ds.astype(BF16), kk)
                    dq_s[rows, :] = (dq_next + dq_both[:SPAN]) * ATTN_SCALE
                    dq_next = dq_both[SPAN:]
            out_ref[...] = dq_s[...].astype(out_ref.dtype)

        @pl.when(kind == 1)
        def _():
            out_ref[...] = dk_s[...].astype(out_ref.dtype)

        @pl.when(kind == 2)
        def _():
            out_ref[...] = dv_s[...].astype(out_ref.dtype)

    head = lambda h: g * HEADS_PER_GROUP + h
    cur = lambda kind: pl.BlockSpec((T, HEAD_DIM), lambda c, h, kd: (c, kind * N_ATTN_HEADS + head(h)))
    prv = lambda kind: pl.BlockSpec((P, HEAD_DIM), lambda c, h, kd: (jnp.maximum(c * nbk - 1, 0), kind * N_ATTN_HEADS + head(h)))
    nxt_row = lambda c: jnp.minimum((c + 1) * nbk, S // P - 1)
    qnext = pl.BlockSpec((P, HEAD_DIM), lambda c, h, kd: (nxt_row(c), head(h)))
    hcur = pl.BlockSpec((T, HEAD_DIM), lambda c, h, kd: (c, h))
    hnext = pl.BlockSpec((P, HEAD_DIM), lambda c, h, kd: (nxt_row(c), h))
    out = pl.BlockSpec((T, HEAD_DIM), lambda c, h, kd: (c, col0 + kd * N_ATTN_HEADS + head(h)))
    stage = [pltpu.VMEM((T, HEAD_DIM), F32)] * 3
    copies = [] if d == 1 else [pltpu.VMEM((T, HEAD_DIM), F32)] * 5 + [pltpu.VMEM((P, HEAD_DIM), F32)] * 5
    return pl.pallas_call(
        body,
        grid=(ntile, HEADS_PER_GROUP, 3),
        in_specs=[cur(0), cur(1), cur(2), prv(1), prv(2), qnext, hcur, hnext, hcur, hnext, hcur, hnext, ANY],
        out_specs=out,
        out_shape=jax.ShapeDtypeStruct(dproj.shape, dproj.dtype),
        input_output_aliases={12: 0},
        scratch_shapes=stage + copies,
        compiler_params=_cparams("parallel", "parallel", "arbitrary"),
        name=name,
    )(qkv, qkv, qkv, qkv, qkv, qkv, dattn, dattn, y, y, lse, lse, dproj)


def _row_block(R, C, bytes_per_row_elem=4, budget=1 << 20):
    if R % 8:
        return R
    best = 8
    t = 8
    while t <= R:
        if R % t == 0 and t * C * bytes_per_row_elem <= budget:
            best = t
        t += 8
    return best


def _adamw(w, g, m, v, name):
    R, C = w.shape
    tr = _row_block(R, C)
    c1 = 1.0 - ADAM_B1 ** ADAM_STEP
    c2 = 1.0 - ADAM_B2 ** ADAM_STEP

    def body(w_ref, g_ref, m_ref, v_ref, d_ref, nm_ref, nv_ref):
        gv = g_ref[...]
        nm = ADAM_B1 * m_ref[...] + (1.0 - ADAM_B1) * gv
        nv = ADAM_B2 * v_ref[...] + (1.0 - ADAM_B2) * (gv * gv)
        d_ref[...] = -ADAM_LR * ((nm / c1) / (jnp.sqrt(nv / c2) + ADAM_EPS) + ADAM_WD * w_ref[...])
        nm_ref[...] = nm
        nv_ref[...] = nv

    blk = pl.BlockSpec((tr, C), lambda i: (i, 0))
    return pl.pallas_call(
        body,
        grid=(R // tr,),
        in_specs=[blk] * 4,
        out_specs=[blk] * 3,
        out_shape=[jax.ShapeDtypeStruct((R, C), F32)] * 3,
        compiler_params=_cparams("parallel"),
        name=name,
    )(w, g, m, v)


def _sum_pieces(grad, axis, recv, pos, name):
    n, pr, pc = recv.shape
    tr = _row_block(pr, pc, bytes_per_row_elem=(n + 1) * recv.dtype.itemsize, budget=4 << 20)
    nblk = pr // tr
    if axis == 1:
        own_map = lambda i, p: (p[1] * nblk + i, p[0])
    else:
        own_map = lambda i, p: ((2 * p[0] + p[1]) * nblk + i, 0)

    def body(p_ref, own_ref, r_ref, o_ref):
        acc = own_ref[...].astype(F32)
        for s in range(n):
            acc = acc + r_ref[s].astype(F32)
        o_ref[...] = acc

    return pl.pallas_call(
        body,
        grid_spec=pltpu.PrefetchScalarGridSpec(
            num_scalar_prefetch=1,
            grid=(nblk,),
            in_specs=[pl.BlockSpec((tr, pc), own_map), pl.BlockSpec((n, tr, pc), lambda i, p: (0, i, 0))],
            out_specs=pl.BlockSpec((tr, pc), lambda i, p: (p[1] * nblk + i, 0)),
        ),
        out_shape=jax.ShapeDtypeStruct((2 * pr, pc), F32),
        compiler_params=_cparams("parallel"),
        name=name,
    )(pos, grad, recv)


def _sum_small(own, recv, me, name):
    n, R, C = recv.shape
    tr = _row_block(R, C, bytes_per_row_elem=(n + 1) * 4, budget=4 << 20)

    def body(me_ref, own_ref, r_ref, o_ref):
        acc = None
        for dev in range(n + 1):
            k = jnp.bitwise_xor(me_ref[0], dev)
            term = jnp.where(k == 0, own_ref[...], r_ref[jnp.maximum(k - 1, 0)])
            acc = term if acc is None else acc + term
        o_ref[...] = acc

    return pl.pallas_call(
        body,
        grid_spec=pltpu.PrefetchScalarGridSpec(
            num_scalar_prefetch=1,
            grid=(R // tr,),
            in_specs=[pl.BlockSpec((tr, C), lambda i, m: (i, 0)), pl.BlockSpec((n, tr, C), lambda i, m: (0, i, 0))],
            out_specs=pl.BlockSpec((tr, C), lambda i, m: (i, 0)),
        ),
        out_shape=jax.ShapeDtypeStruct((R, C), F32),
        compiler_params=_cparams("parallel"),
        name=name,
    )(me, own, recv)


def _place(shard, axis, pos, dtype, name):
    shp = list(shard.shape)
    shp[axis] *= N_CHIPS
    if shard.ndim == 3:
        assert axis == 1
        in_spec = pl.BlockSpec(shard.shape, lambda i, p: (0, 0, 0))
        out_spec = pl.BlockSpec(shard.shape, lambda i, p: (0, p[0], 0))
        grid = (1,)
    else:
        R, C = shard.shape
        tr = _row_block(R, C, bytes_per_row_elem=4, budget=2 << 20)
        nblk = R // tr
        in_spec = pl.BlockSpec((tr, C), lambda i, p: (i, 0))
        out_spec = pl.BlockSpec((tr, C), (lambda i, p: (i, p[0])) if axis == 1 else (lambda i, p: (p[0] * nblk + i, 0)))
        grid = (nblk,)

    def body(p_ref, s_ref, o_ref):
        o_ref[...] = s_ref[...].astype(o_ref.dtype)

    return pl.pallas_call(
        body,
        grid_spec=pltpu.PrefetchScalarGridSpec(num_scalar_prefetch=1, grid=grid, in_specs=[in_spec], out_specs=out_spec),
        out_shape=jax.ShapeDtypeStruct(tuple(shp), dtype),
        compiler_params=_cparams("parallel"),
        name=name,
    )(pos, shard)


HBM = pl.BlockSpec(memory_space=pltpu.HBM)
SEM = pl.BlockSpec(memory_space=pltpu.SEMAPHORE)
DATAFLOW = pltpu.SideEffectType.DATAFLOW_SIDE_EFFECTING


def _position():
    return lax.axis_index("x"), lax.axis_index("y"), lax.axis_index("c")


def _peer(k):
    x, y, c = _position()
    return ((1 - x) if k & 4 else x, (1 - y) if k & 2 else y, (1 - c) if k & 1 else c)


def _shard_slice(ref, axis, idx, size):
    start = idx * size
    if axis == ref.ndim - 1:
        start = pl.multiple_of(start, 128)
    ix = [slice(None)] * ref.ndim
    ix[axis] = pl.ds(start, size)
    return ref.at[tuple(ix)]


def _gather_plan(axes):
    def plan(refs):
        x, y, c = _position()
        out = []
        for ref, ax in zip(refs, axes):
            mine = _shard_slice(ref, ax, 2 * x + y, ref.shape[ax] // N_CHIPS)
            for k in (4, 2, 6):
                px, py, _ = _peer(k)
                out.append((mine, mine, (px, py, c)))
        return out
    return plan


def _scatter_plan(axes):
    m = len(axes)

    def plan(refs):
        out = []
        for t in range(m):
            grad, recv = refs[t], refs[m + t]
            _, pr, pc = recv.shape
            for k in range(1, N_DEV):
                px, py, pcore = _peer(k)
                if axes[t] == 0:
                    piece = grad.at[pl.ds(((2 * px + py) * 2 + pcore) * pr, pr), :]
                else:
                    piece = grad.at[pl.ds(pcore * pr, pr), pl.ds(pl.multiple_of((2 * px + py) * pc, 128), pc)]
                out.append((piece, recv.at[k - 1], (px, py, pcore)))
        return out
    return plan


def _broadcast_plan(refs):
    small, recv = refs
    return [(small, recv.at[k - 1], _peer(k)) for k in range(1, N_DEV)]


def _start_all(plan, refs, send_sems, recv_sems):
    for q, (src, dst, dev) in enumerate(plan(refs)):
        pltpu.make_async_remote_copy(src_ref=src, dst_ref=dst, send_sem=send_sems.at[q], recv_sem=recv_sems.at[q], device_id=dev, device_id_type=MESH).start()


def _wait_all(plan, refs, send_sems, recv_sems):
    for q, (src, dst, dev) in enumerate(plan(refs)):
        cp = pltpu.make_async_remote_copy(src_ref=src, dst_ref=dst, send_sem=send_sems.at[q], recv_sem=recv_sems.at[q], device_id=dev, device_id_type=MESH)
        cp.wait_send()
        cp.wait_recv()


def _push(bufs, plan, ncopies, name):
    n = len(bufs)

    def body(*refs):
        outs = refs[n:2 * n]
        send_sems, recv_sems = refs[2 * n:]
        _start_all(plan, outs, send_sems, recv_sems)
        _wait_all(plan, outs, send_sems, recv_sems)

    return pl.pallas_call(
        body,
        in_specs=[ANY] * n,
        out_specs=[ANY] * n,
        out_shape=[jax.ShapeDtypeStruct(b.shape, b.dtype) for b in bufs],
        input_output_aliases={t: t for t in range(n)},
        scratch_shapes=[pltpu.SemaphoreType.DMA((ncopies,)), pltpu.SemaphoreType.DMA((ncopies,))],
        name=name,
    )(*bufs)


def _gather_once_per_chip(full, name):
    R, C = full.shape
    R2, C4 = R // 2, C // N_CHIPS

    def body(_, ref, send_sems, recv_sems):
        x, y, c = _position()
        chips = [_peer(k)[:2] for k in (4, 2, 6)]

        def half(chip, core):
            return ref.at[pl.ds(core * R2, R2), pl.ds(pl.multiple_of(chip * C4, 128), C4)]

        def copy(q, chip, core, to):
            return pltpu.make_async_remote_copy(src_ref=half(chip, core), dst_ref=half(chip, core), send_sem=send_sems.at[q], recv_sem=recv_sems.at[q],
                                                device_id=to, device_id_type=MESH)

        sends = [copy(q, 2 * x + y, c, (px, py, c)) for q, (px, py) in enumerate(chips)]
        for cp in sends:
            cp.start()
        for q, (px, py) in enumerate(chips):
            copy(q, 2 * px + py, c, (px, py, c)).wait_recv()
            passed = copy(3 + q, 2 * px + py, c, (x, y, 1 - c))
            passed.start()
            sends.append(passed)
        for q, (px, py) in enumerate(chips):
            copy(3 + q, 2 * px + py, 1 - c, (x, y, 1 - c)).wait_recv()
        for cp in sends:
            cp.wait_send()

    return pl.pallas_call(
        body,
        in_specs=[ANY],
        out_specs=ANY,
        out_shape=jax.ShapeDtypeStruct(full.shape, full.dtype),
        input_output_aliases={0: 0},
        scratch_shapes=[pltpu.SemaphoreType.DMA((6,)), pltpu.SemaphoreType.DMA((6,))],
        name=name,
    )(full)


def _push_start(bufs, plan, ncopies, name, after=None):
    n = len(bufs)
    extra = [] if after is None else [after]

    def body(*refs):
        ins = refs[:n]
        first_out = n + len(extra)
        send_sems, recv_sems, token = refs[first_out], refs[first_out + 1], refs[-1]
        _start_all(plan, ins, send_sems, recv_sems)
        token[...] = jnp.zeros_like(token)

    res = pl.pallas_call(
        body,
        name=name,
        out_shape=(pltpu.SemaphoreType.DMA((ncopies,)), pltpu.SemaphoreType.DMA((ncopies,)), *[pltpu.HBM(b.shape, b.dtype) for b in bufs],
                   jax.ShapeDtypeStruct((8, 128), F32)),
        in_specs=[HBM] * n + [ANY] * len(extra),
        out_specs=(SEM, SEM, *[HBM] * n, pl.BlockSpec(memory_space=pltpu.VMEM)),
        input_output_aliases={t: t + 2 for t in range(n)},
        compiler_params=pltpu.CompilerParams(has_side_effects=DATAFLOW),
    )(*[pltpu.with_memory_space_constraint(b, pltpu.HBM) for b in bufs], *extra)
    return res[0], res[1], list(res[2:2 + n]), res[-1]


def _push_wait(send_sems, recv_sems, bufs, plan, after, name):
    n = len(bufs)

    def body(*refs):
        ins = refs[:n]
        _wait_all(plan, ins, refs[n], refs[n + 1])

    return pl.pallas_call(
        body,
        name=name,
        out_shape=tuple(pltpu.HBM(b.shape, b.dtype) for b in bufs),
        in_specs=[HBM] * n + [SEM, SEM, ANY],
        out_specs=tuple([HBM] * n),
        input_output_aliases={t: t for t in range(n)},
        compiler_params=pltpu.CompilerParams(has_side_effects=DATAFLOW),
    )(*bufs, send_sems, recv_sems, after)


EXCHANGE_CHUNKS = 2


def _exchange_plan(refs):
    x, y, c = _position()
    out = []
    for ref in refs:
        rows = ref.shape[0] // (2 * EXCHANGE_CHUNKS)
        for q in range(EXCHANGE_CHUNKS):
            mine = ref.at[pl.ds((c * EXCHANGE_CHUNKS + q) * rows, rows), :]
            out.append((mine, mine, (x, y, 1 - c)))
    return out


LATE_WEIGHTS = (("w_pool_lin", "w_pool_out", "w_attn_out", "w_out"), ("w_up", "conv_w", "w_down"))


def _local_step(x, tgt, w, late_weights, send):
    S, D = x.shape
    PW = w["pool_scale"].shape[1]
    o_q = PW
    o_g = PW + 3 * ATTN_WIDTH
    QKV = 3 * ATTN_WIDTH

    h1 = _rms_fwd(x, w["g_mix"], "rms1")
    proj_tiles = (_tile(S, 1024), 512, D)
    u = _mm(h1, w["w_in"], mode="nn", dims=(S, PW, D), tiles=proj_tiles, out_dtypes=(F32,), name="proj_u")
    qkv = _mm(h1, w["w_in"], mode="nn", dims=(S, QKV, D), tiles=proj_tiles, b_off=(0, o_q), name="proj_qkv")

    def gate_epilogue(acc, ex, outs):
        outs[0][...] = (1.0 / (1.0 + jnp.exp(-(acc + ex[0][...])))).astype(outs[0].dtype)

    gates = _mm(h1, w["w_in"], mode="nn", dims=(S, 2 * D, D), tiles=proj_tiles, b_off=(0, o_g), epilogue=gate_epilogue,
                extras=[(w["b_gate"], "n", (0, 0))], name="proj_gates")

    os_, lses = [], []
    for gi, (_, d) in enumerate(ATTN_GROUPS):
        o, lse = _attn_fwd(qkv, d, gi, f"attn_fwd{gi}")
        os_.append(o)
        lses.append(lse)
    attn, lse_tot = _attn_merge(os_, lses, "attn_merge")

    w = dict(w, **late_weights(0, attn))
    pool_out = _pool_fwd(u, w["w_pool_lin"], w["pool_scale"], "pool_fwd")
    y_pool = _mm(pool_out, w["w_pool_out"], mode="nn", dims=(S, D, PW), name="y_pool")

    def mix_epilogue(acc, ex, outs):
        outs[0][...] = acc.astype(BF16)
        outs[1][...] = (ex[0][...].astype(F32) * ex[2][...].astype(F32) + ex[1][...].astype(F32) * acc).astype(BF16)

    y_attn, mixed = _mm(attn, w["w_attn_out"], mode="nn", dims=(S, D, GROUP_WIDTH), out_dtypes=(BF16, BF16), epilogue=mix_epilogue,
                        extras=[(gates, "mn", (0, 0)), (gates, "mn", (0, D)), (y_pool, "mn", (0, 0))], name="y_attn_mix")

    def residual_epilogue(acc, ex, outs):
        outs[0][...] = ex[0][...] + acc

    x2 = _mm(mixed, w["w_out"], mode="nn", dims=(S, D, D), out_dtypes=(F32,), epilogue=residual_epilogue, extras=[(x, "mn", (0, 0))], name="out_proj")

    h2 = _rms_fwd(x2, w["g_ffn"], "rms2")
    w = dict(w, **late_weights(1, h2))
    F = w["w_down"].shape[0]
    up = _mm(h2, w["w_up"], mode="nn", dims=(S, 2 * F, D), name="up_proj")
    f, act_a, act_b = _convglu_fwd(up, w["conv_w"], w["conv_b"], "convglu_fwd")
    x3 = _mm(f, w["w_down"], mode="nn", dims=(S, D, F), out_dtypes=(F32,), epilogue=residual_epilogue, extras=[(x2, "mn", (0, 0))], name="down_proj")

    g = {}
    dx3, dx3b, g["g_final"], loss_cols = _loss_head(x3, tgt, w["g_final"], "loss_head")

    g["w_down"] = _mm(f, dx3b, mode="tn", dims=(F, D, S), name="dw_down")
    sent = send(("w_down",), g)
    df = _mm(dx3b, w["w_down"], mode="nt", dims=(S, F, D), name="d_f")
    dup, g["conv_b"], g["conv_w"] = _convglu_bwd(df, act_a, act_b, up, w["conv_w"] + sent, "convglu_bwd")
    g["w_up"] = _mm(h2, dup, mode="tn", dims=(D, 2 * F, S), name="dw_up")
    sent = send(("w_up",), g)
    dh2 = _mm(dup, w["w_up"], mode="nt", dims=(S, D, 2 * F), name="d_h2")
    dx2, dx2b, g["g_ffn"] = _rms_bwd(dh2, x2, w["g_ffn"] + sent, dx3, "rms2_bwd", True)

    g["w_out"] = _mm(mixed, dx2b, mode="tn", dims=(D, D, S), name="dw_out")
    dmixed = _mm(dx2b, w["w_out"], mode="nt", dims=(S, D, D), name="d_mixed")
    IN = w["w_in"].shape[1]
    dy_both, dproj, g["b_gate"] = _gate_bwd(dmixed, gates, y_pool, y_attn, IN, "gate_bwd")

    g["w_pool_out"] = _mm(pool_out, dy_both, mode="tn", dims=(PW, D, S), name="dw_pool_out")
    g["w_attn_out"] = _mm(attn, dy_both, mode="tn", dims=(GROUP_WIDTH, D, S), b_off=(0, D), name="dw_attn_out")
    sent = send(("w_out", "w_pool_out", "w_attn_out"), g)
    dpool = _mm(dy_both, w["w_pool_out"], mode="nt", dims=(S, PW, D), name="d_pool")
    dattn = _mm(dy_both, w["w_attn_out"], mode="nt", dims=(S, GROUP_WIDTH, D), a_off=(0, D), name="d_attn")

    dproj, g["w_pool_lin"], g["pool_scale"] = _pool_bwd(u, dpool, w["w_pool_lin"], w["pool_scale"] + sent, dproj, "pool_bwd")
    g["loss_cols"] = loss_cols
    sent = send("small", g)

    for gi, (_, d) in enumerate(ATTN_GROUPS):
        dproj = _attn_bwd(qkv, dattn, attn, lse_tot, dproj, d, gi, PW // HEAD_DIM, f"attn_bwd{gi}")

    g["w_in"] = _mm(h1, dproj, mode="tn", dims=(D, IN, S), name="dw_in")
    sent = sent + send(("w_in",), g)
    dh1 = _mm(dproj, w["w_in"], mode="nt", dims=(S, D, IN), tiles=(_tile(S, 1024), _tile(D, 2048), _tile(IN, 2432)), name="d_h1")
    (grad_x, g["g_mix"]) = _rms_bwd(dh1, x, w["g_mix"] + sent, dx2, "rms1_bwd", False)
    return loss_cols, grad_x, g


BIG = ("w_in", "w_pool_out", "w_attn_out", "w_out", "w_up", "w_down")
BIG_AXIS = {"w_in": 1, "w_pool_out": 1, "w_attn_out": 1, "w_out": 0, "w_up": 1, "w_down": 0}
GATHER_AXIS = dict(BIG_AXIS, w_pool_lin=1, conv_w=1)
SMALL = ("loss_cols", "b_gate", "w_pool_lin", "pool_scale", "g_ffn", "conv_w", "conv_b", "g_final")
SMALL_COLS = 1024
ORDER = ("g_mix", "w_in", "b_gate", "w_pool_lin", "pool_scale", "w_pool_out", "w_attn_out", "w_out", "g_ffn", "w_up", "conv_w", "conv_b", "w_down", "g_final")


def _as_rows(parts):
    flat = jnp.concatenate([p.astype(F32).reshape(-1) for p in parts])
    rows = -(-flat.shape[0] // (8 * SMALL_COLS)) * 8
    return jnp.pad(flat, (0, rows * SMALL_COLS - flat.shape[0])).reshape(rows, SMALL_COLS)


def kernel(x, g_mix, w_in, b_gate, w_pool_lin, pool_scale, w_pool_out, w_attn_out, w_out, g_ffn, w_up, conv_w, conv_b, w_down, g_final, loss_target, m_g_mix, m_w_in, m_b_gate, m_w_pool_lin, m_pool_scale, m_w_pool_out, m_w_attn_out, m_w_out, m_g_ffn, m_w_up, m_conv_w, m_conv_b, m_w_down, m_g_final, v_g_mix, v_w_in, v_b_gate, v_w_pool_lin, v_pool_scale, v_w_pool_out, v_w_attn_out, v_w_out, v_g_ffn, v_w_up, v_conv_w, v_conv_b, v_w_down, v_g_final):
    shard = dict(g_mix=g_mix, w_in=w_in, b_gate=b_gate, w_pool_lin=w_pool_lin, pool_scale=pool_scale, w_pool_out=w_pool_out, w_attn_out=w_attn_out,
                 w_out=w_out, g_ffn=g_ffn, w_up=w_up, conv_w=conv_w, conv_b=conv_b, w_down=w_down, g_final=g_final)
    mom = dict(g_mix=m_g_mix, w_in=m_w_in, b_gate=m_b_gate, w_pool_lin=m_w_pool_lin, pool_scale=m_pool_scale, w_pool_out=m_w_pool_out, w_attn_out=m_w_attn_out,
               w_out=m_w_out, g_ffn=m_g_ffn, w_up=m_w_up, conv_w=m_conv_w, conv_b=m_conv_b, w_down=m_w_down, g_final=m_g_final)
    vel = dict(g_mix=v_g_mix, w_in=v_w_in, b_gate=v_b_gate, w_pool_lin=v_w_pool_lin, pool_scale=v_pool_scale, w_pool_out=v_w_pool_out, w_attn_out=v_w_attn_out,
               w_out=v_w_out, g_ffn=v_g_ffn, w_up=v_w_up, conv_w=v_conv_w, conv_b=v_conv_b, w_down=v_w_down, g_final=v_g_final)
    chip = 2 * lax.axis_index("x") + lax.axis_index("y")
    pos = jnp.stack([chip, lax.axis_index("c")]).astype(jnp.int32)
    me = (2 * chip + lax.axis_index("c")).astype(jnp.int32).reshape(1)
    D = x.shape[2]

    placed = {k: _place(shard[k][0], GATHER_AXIS[k], pos, F32 if k == "conv_w" else BF16, f"place_{k}") for k in GATHER_AXIS}
    w_in_full = _gather_once_per_chip(placed["w_in"], "comm_gather_w_in")
    late, late_token, prior = [], 0.0, w_in_full
    for stage, names in enumerate(LATE_WEIGHTS):
        plan = _gather_plan([GATHER_AXIS[k] for k in names])
        send_sems, recv_sems, bufs, token = _push_start([placed[k] for k in names], plan, 3 * len(names), f"comm_gather_late{stage}_start", after=prior)
        late.append((names, send_sems, recv_sems, bufs, plan))
        late_token, prior = late_token + token[0, 0], token

    def late_weights(stage, after):
        names, send_sems, recv_sems, bufs, plan = late[stage]
        return dict(zip(names, _push_wait(send_sems, recv_sems, bufs, plan, after, f"comm_gather_late{stage}_wait")))

    pending = []

    def send(names, g):
        if names == "small":
            bufs = [_as_rows([g[k] for k in SMALL])]
            bufs.append(lax.empty((N_DEV - 1,) + bufs[0].shape, F32))
            plan, tag = _broadcast_plan, "small"
        else:
            bufs = [g[k] for k in names]
            for k in names:
                R, C = g[k].shape
                piece = (R // (2 * N_CHIPS), C) if BIG_AXIS[k] == 0 else (R // 2, C // N_CHIPS)
                bufs.append(lax.empty((N_DEV - 1,) + piece, BF16))
            plan, tag = _scatter_plan([BIG_AXIS[k] for k in names]), names[0]
        ncopies = (N_DEV - 1) * (len(bufs) // 2)
        send_sems, recv_sems, thru, token = _push_start(bufs, plan, ncopies, f"comm_scatter_start_{tag}")
        pending.append((names, send_sems, recv_sems, thru, plan, tag))
        return token[0, 0]

    w0 = dict(g_mix=shard["g_mix"] + late_token, w_in=w_in_full, b_gate=shard["b_gate"], pool_scale=shard["pool_scale"], g_ffn=shard["g_ffn"],
              conv_b=shard["conv_b"], g_final=shard["g_final"].reshape(1, D))
    _, grad_x, gr = _local_step(x[0], loss_target[0], w0, late_weights, send)

    halves, small_parts = {}, None
    for names, send_sems, recv_sems, thru, plan, tag in pending:
        done = _push_wait(send_sems, recv_sems, thru, plan, grad_x, f"comm_scatter_wait_{tag}")
        if names == "small":
            small_parts = _sum_small(done[0], done[1], me, "sum_small").reshape(-1)
        else:
            m = len(names)
            for t, k in enumerate(names):
                halves[k] = _sum_pieces(done[t], BIG_AXIS[k], done[m + t], pos, f"sum_{k}")
    g_mix_own = _as_rows([gr["g_mix"]])
    _, g_mix_recv = _push([g_mix_own, lax.empty((N_DEV - 1,) + g_mix_own.shape, F32)], _broadcast_plan, N_DEV - 1, "comm_gather_g_mix")
    g_mix_sum = _sum_small(g_mix_own, g_mix_recv, me, "sum_g_mix").reshape(-1)[:D]
    wholes = _push([halves[k] for k in BIG], _exchange_plan, EXCHANGE_CHUNKS * len(BIG), "comm_exchange_halves")

    grads = {"g_mix": g_mix_sum.reshape(shard["g_mix"].shape)}
    for k, whole in zip(BIG, wholes):
        grads[k] = whole.reshape(shard[k].shape)
    off = 0
    loss = None
    for k in SMALL:
        sz = math.prod(gr[k].shape)
        fullg = small_parts[off:off + sz].reshape(gr[k].shape)
        off += sz
        if k == "loss_cols":
            loss = jnp.sum(fullg)
            continue
        if k in ("w_pool_lin", "conv_w"):
            n = shard[k].shape[2]
            fullg = lax.dynamic_slice_in_dim(fullg, chip * n, n, axis=1)
        grads[k] = fullg.reshape(shard[k].shape)

    deltas, new_m, new_v = {}, {}, {}
    for k in ORDER:
        shp = shard[k].shape
        two_d = (-1, shp[-1])
        dl, nm, nv = _adamw(shard[k].reshape(two_d), grads[k].reshape(two_d), mom[k].reshape(two_d), vel[k].reshape(two_d), f"adamw_{k}")
        deltas[k], new_m[k], new_v[k] = dl.reshape(shp), nm.reshape(shp), nv.reshape(shp)

    return (loss, grad_x[None], *[grads[k] for k in ORDER], *[deltas[k] for k in ORDER], *[new_m[k] for k in ORDER], *[new_v[k] for k in ORDER])
```

```python
import functools
import math

import jax
import jax.numpy as jnp
from jax import lax
from jax.experimental import pallas as pl
from jax.experimental.pallas import tpu as pltpu

F32 = jnp.float32
BF16 = jnp.bfloat16

RMS_EPS = 1e-6
POOL_WINDOWS = (2, 4, 8, 16)
ATTN_GROUPS = ((128, 1), (512, 4), (2048, 16))
HEADS_PER_GROUP = 4
HEAD_DIM = 128
N_ATTN_HEADS = HEADS_PER_GROUP * len(ATTN_GROUPS)
SPAN = 128
GROUP_WIDTH = HEADS_PER_GROUP * HEAD_DIM
ATTN_WIDTH = N_ATTN_HEADS * HEAD_DIM
ATTN_SCALE = HEAD_DIM ** -0.5
NEG_BIG = -1e30
ALIBI_SLOPES = tuple(2.0 ** (-8.0 * (h + 1) / N_ATTN_HEADS) for h in range(N_ATTN_HEADS))

ADAM_LR = 0.001
ADAM_B1 = 0.9
ADAM_B2 = 0.999
ADAM_EPS = 1e-08
ADAM_WD = 0.01
ADAM_STEP = 10

INV_SQRT2 = 1.0 / math.sqrt(2.0)
INV_SQRT_2PI = 1.0 / math.sqrt(2.0 * math.pi)

HALO = 16
VMEM_LIMIT = 56 * 1024 * 1024
N_CHIPS = 4
N_DEV = 8
MESH = pl.DeviceIdType.MESH
ANY = pl.BlockSpec(memory_space=pl.ANY)


def _cparams(*sem):
    return pltpu.CompilerParams(dimension_semantics=sem, vmem_limit_bytes=VMEM_LIMIT)


def _tile(n, pref, mult=128):
    t = (min(pref, n) // mult) * mult
    while t >= mult:
        if n % t == 0:
            return t
        t -= mult
    return n


def _dot(a, b, contract):
    return lax.dot_general(a, b, (contract, ((), ())), preferred_element_type=F32)


def _dot_nn(a, b):
    return _dot(a, b, ((1,), (0,)))


def _dot_nt(a, b):
    return _dot(a, b, ((1,), (1,)))


def _mm(a, b, *, mode, dims, name, tiles=None, out_dtypes=(BF16,), epilogue=None, extras=(), a_off=(0, 0), b_off=(0, 0)):
    M, N, K = dims
    if tiles is None:
        tiles = (_tile(M, 1408), _tile(N, 2816), _tile(K, 512)) if mode == "tn" else (_tile(M, 1024), _tile(N, 1536), _tile(K, 2816))
    tm, tn, tk = tiles
    assert M % tm == 0 and N % tn == 0 and K % tk == 0, (name, dims, tiles)
    nk = K // tk
    if mode == "nn":
        ab, bb, contract = (tm, tk), (tk, tn), ((1,), (0,))
        amap = lambda i, j, k: (i + a_off[0] // tm, k + a_off[1] // tk)
        bmap = lambda i, j, k: (k + b_off[0] // tk, j + b_off[1] // tn)
    elif mode == "nt":
        ab, bb, contract = (tm, tk), (tn, tk), ((1,), (1,))
        amap = lambda i, j, k: (i + a_off[0] // tm, k + a_off[1] // tk)
        bmap = lambda i, j, k: (j + b_off[0] // tn, k + b_off[1] // tk)
    else:
        ab, bb, contract = (tk, tm), (tk, tn), ((0,), (0,))
        amap = lambda i, j, k: (k + a_off[0] // tk, i + a_off[1] // tm)
        bmap = lambda i, j, k: (k + b_off[0] // tk, j + b_off[1] // tn)
    assert a_off[0] % ab[0] == 0 and a_off[1] % ab[1] == 0 and b_off[0] % bb[0] == 0 and b_off[1] % bb[1] == 0, name
    in_specs = [pl.BlockSpec(ab, amap), pl.BlockSpec(bb, bmap)]
    ex_arrays = []
    for arr, kind, off in extras:
        if kind == "mn":
            assert off[0] % tm == 0 and off[1] % tn == 0, name
            in_specs.append(pl.BlockSpec((tm, tn), lambda i, j, k, off=off: (i + off[0] // tm, j + off[1] // tn)))
        else:
            assert off[1] % tn == 0, name
            in_specs.append(pl.BlockSpec((1, tn), lambda i, j, k, off=off: (0, j + off[1] // tn)))
        ex_arrays.append(arr)
    ne, no = len(ex_arrays), len(out_dtypes)
    if epilogue is None:
        def epilogue(acc, ex, outs):
            outs[0][...] = acc.astype(outs[0].dtype)

    def body(*refs):
        a_ref, b_ref = refs[0], refs[1]
        ex, outs = refs[2:2 + ne], refs[2 + ne:2 + ne + no]
        if nk == 1:
            epilogue(_dot(a_ref[...], b_ref[...], contract), ex, outs)
            return
        acc = refs[-1]
        k = pl.program_id(2)
        if nk <= 4:
            part = _dot(a_ref[...], b_ref[...], contract)

            @pl.when(k == 0)
            def _():
                acc[...] = part

            @pl.when(jnp.logical_and(k > 0, k < nk - 1))
            def _():
                acc[...] += part

            @pl.when(k == nk - 1)
            def _():
                epilogue(acc[...] + part, ex, outs)
        else:
            @pl.when(k == 0)
            def _():
                acc[...] = _dot(a_ref[...], b_ref[...], contract)

            @pl.when(k > 0)
            def _():
                acc[...] += _dot(a_ref[...], b_ref[...], contract)

            @pl.when(k == nk - 1)
            def _():
                epilogue(acc[...], ex, outs)

    res = pl.pallas_call(
        body,
        grid=(M // tm, N // tn, nk),
        in_specs=in_specs,
        out_specs=[pl.BlockSpec((tm, tn), lambda i, j, k: (i, j)) for _ in out_dtypes],
        out_shape=[jax.ShapeDtypeStruct((M, N), dt) for dt in out_dtypes],
        scratch_shapes=[pltpu.VMEM((tm, tn), F32)] if nk > 1 else [],
        compiler_params=_cparams("parallel", "parallel", "arbitrary"),
        name=name,
    )(a, b, *ex_arrays)
    return res[0] if no == 1 else res


def _rms_fwd(x, g, name):
    S, D = x.shape
    tm = _tile(S, 256)

    def body(x_ref, g_ref, h_ref):
        xv = x_ref[...]
        r = lax.rsqrt(jnp.mean(xv * xv, axis=-1, keepdims=True) + RMS_EPS)
        h_ref[...] = (xv * r * g_ref[...]).astype(h_ref.dtype)

    return pl.pallas_call(
        body,
        grid=(S // tm,),
        in_specs=[pl.BlockSpec((tm, D), lambda i: (i, 0)), pl.BlockSpec((1, D), lambda i: (0, 0))],
        out_specs=pl.BlockSpec((tm, D), lambda i: (i, 0)),
        out_shape=jax.ShapeDtypeStruct((S, D), BF16),
        compiler_params=_cparams("parallel"),
        name=name,
    )(x, g)


def _rms_bwd(dh, x, g, dres, name, out_dtype):
    S, D = x.shape
    tm = _tile(S, 256)

    def body(dh_ref, x_ref, g_ref, dres_ref, dx_ref, dg_ref):
        xv = x_ref[...]
        r = lax.rsqrt(jnp.mean(xv * xv, axis=-1, keepdims=True) + RMS_EPS)
        xr = xv * r
        dhv = dh_ref[...].astype(F32)

        @pl.when(pl.program_id(0) == 0)
        def _():
            dg_ref[...] = jnp.zeros_like(dg_ref)

        dg_ref[...] += jnp.sum(dhv * xr, axis=0, keepdims=True)
        u = dhv * g_ref[...]
        c = jnp.mean(u * xr, axis=-1, keepdims=True)
        dx_ref[...] = (dres_ref[...].astype(F32) + r * (u - xr * c)).astype(dx_ref.dtype)

    row = pl.BlockSpec((tm, D), lambda i: (i, 0))
    vec = pl.BlockSpec((1, D), lambda i: (0, 0))
    return pl.pallas_call(
        body,
        grid=(S // tm,),
        in_specs=[row, row, vec, row],
        out_specs=[row, vec],
        out_shape=[jax.ShapeDtypeStruct((S, D), out_dtype), jax.ShapeDtypeStruct((1, D), F32)],
        compiler_params=_cparams("arbitrary"),
        name=name,
    )(dh, x, g, dres)


def _loss_head(x3, tgt, g, name):
    S, D = x3.shape
    tm = _tile(S, 256)

    def body(x_ref, t_ref, g_ref, dxb_ref, dg_ref, loss_ref):
        xv = x_ref[...]
        gv = g_ref[...]
        r = lax.rsqrt(jnp.mean(xv * xv, axis=-1, keepdims=True) + RMS_EPS)
        xr = xv * r
        e = xr * gv - t_ref[...]

        @pl.when(pl.program_id(0) == 0)
        def _():
            dg_ref[...] = jnp.zeros_like(dg_ref)
            loss_ref[...] = jnp.zeros_like(loss_ref)

        loss_ref[...] += jnp.sum(e * e, axis=0, keepdims=True) * (0.5 / D)
        dy = e * (1.0 / D)
        dg_ref[...] += jnp.sum(dy * xr, axis=0, keepdims=True)
        u = dy * gv
        c = jnp.mean(u * xr, axis=-1, keepdims=True)
        dxb_ref[...] = (r * (u - xr * c)).astype(BF16)

    row = pl.BlockSpec((tm, D), lambda i: (i, 0))
    vec = pl.BlockSpec((1, D), lambda i: (0, 0))
    return pl.pallas_call(
        body,
        grid=(S // tm,),
        in_specs=[row, row, vec],
        out_specs=[row, vec, vec],
        out_shape=[jax.ShapeDtypeStruct((S, D), BF16), jax.ShapeDtypeStruct((1, D), F32), jax.ShapeDtypeStruct((1, D), F32)],
        compiler_params=_cparams("arbitrary"),
        name=name,
    )(x3, tgt, g)


def _conv_taps(cur_ref, halo_ref, w_ref, b_ref, first):
    cur = cur_ref[...].astype(F32)
    halo = jnp.where(first, 0.0, halo_ref[...].astype(F32))
    xx = jnp.concatenate([halo, cur], axis=0)
    p1 = pltpu.roll(xx, 1, 0)[HALO:]
    p2 = pltpu.roll(xx, 2, 0)[HALO:]
    w = w_ref[...]
    y = b_ref[...] + w[0:1] * p2 + w[1:2] * p1 + w[2:3] * cur
    return y, (cur, p1, p2)


def _convglu_fwd(up, cw, cb, name):
    S, F2 = up.shape
    F = F2 // 2
    tm, tn = _tile(S, 512), _tile(F, 512)
    nj, hb = F // tn, tm // HALO

    def body(ua, ub, ha, hb_, wa, wb, ba, bb, f_ref, a_ref, b_ref):
        first = pl.program_id(0) == 0
        a, _ = _conv_taps(ua, ha, wa, ba, first)
        b, _ = _conv_taps(ub, hb_, wb, bb, first)
        f_ref[...] = (0.5 * a * (1.0 + lax.erf(a * INV_SQRT2)) * b).astype(f_ref.dtype)
        a_ref[...] = a.astype(a_ref.dtype)
        b_ref[...] = b.astype(b_ref.dtype)

    tile = pl.BlockSpec((tm, tn), lambda i, j: (i, j))
    return pl.pallas_call(
        body,
        grid=(S // tm, nj),
        in_specs=[
            tile,
            pl.BlockSpec((tm, tn), lambda i, j: (i, j + nj)),
            pl.BlockSpec((HALO, tn), lambda i, j: (jnp.maximum(i * hb - 1, 0), j)),
            pl.BlockSpec((HALO, tn), lambda i, j: (jnp.maximum(i * hb - 1, 0), j + nj)),
            pl.BlockSpec((3, tn), lambda i, j: (0, j)),
            pl.BlockSpec((3, tn), lambda i, j: (0, j + nj)),
            pl.BlockSpec((1, tn), lambda i, j: (0, j)),
            pl.BlockSpec((1, tn), lambda i, j: (0, j + nj)),
        ],
        out_specs=[tile, tile, tile],
        out_shape=[jax.ShapeDtypeStruct((S, F), BF16)] * 3,
        compiler_params=_cparams("parallel", "parallel"),
        name=name,
    )(up, up, up, up, cw, cw, cb, cb)


def _convglu_bwd(df, a, b, up, cw, name):
    S, F = df.shape
    tm, tn = _tile(S, 512), _tile(F, 512)
    nj, ni, hb = F // tn, S // tm, tm // HALO
    n = tm + HALO

    def body(df_ref, dfn_ref, a_ref, an_ref, b_ref, bn_ref, up_ref, w_ref, o_ref, db_ref, dw_ref):
        j, i = pl.program_id(0), pl.program_id(1)
        last = i == ni - 1

        def rows(c_ref, n_ref):
            return jnp.concatenate([c_ref[...].astype(F32), jnp.where(last, 0.0, n_ref[...].astype(F32))], axis=0)

        @pl.when(i == 0)
        def _():
            db_ref[...] = jnp.zeros_like(db_ref)
            dw_ref[...] = jnp.zeros_like(dw_ref)

        def finish(d):
            d0 = d[:tm]
            d1 = pltpu.roll(d, n - 1, 0)[:tm]
            d2 = pltpu.roll(d, n - 2, 0)[:tm]
            w = w_ref[...]
            o_ref[...] = (w[2:3] * d0 + w[1:2] * d1 + w[0:1] * d2).astype(o_ref.dtype)
            upv = up_ref[...].astype(F32)
            db_ref[...] += jnp.sum(d0, axis=0, keepdims=True)
            dw_ref[0:1, :] += jnp.sum(d2 * upv, axis=0, keepdims=True)
            dw_ref[1:2, :] += jnp.sum(d1 * upv, axis=0, keepdims=True)
            dw_ref[2:3, :] += jnp.sum(d0 * upv, axis=0, keepdims=True)

        av, dfv = rows(a_ref, an_ref), rows(df_ref, dfn_ref)
        cdf = 0.5 * (1.0 + lax.erf(av * INV_SQRT2))

        @pl.when(j < nj)
        def _():
            pdf = jnp.exp(-0.5 * av * av) * INV_SQRT_2PI
            finish(dfv * rows(b_ref, bn_ref) * (cdf + av * pdf))

        @pl.when(j >= nj)
        def _():
            finish(dfv * (av * cdf))

    jh = lambda j: lax.rem(j, nj)
    nxt = lambda i: jnp.minimum((i + 1) * hb, S // HALO - 1)
    cur = pl.BlockSpec((tm, tn), lambda j, i: (i, jh(j)))
    halo = pl.BlockSpec((HALO, tn), lambda j, i: (nxt(i), jh(j)))
    return pl.pallas_call(
        body,
        grid=(2 * nj, ni),
        in_specs=[cur, halo, cur, halo, cur, halo, pl.BlockSpec((tm, tn), lambda j, i: (i, j)), pl.BlockSpec((3, tn), lambda j, i: (0, j))],
        out_specs=[pl.BlockSpec((tm, tn), lambda j, i: (i, j)), pl.BlockSpec((1, tn), lambda j, i: (0, j)), pl.BlockSpec((3, tn), lambda j, i: (0, j))],
        out_shape=[jax.ShapeDtypeStruct((S, 2 * F), BF16), jax.ShapeDtypeStruct((1, 2 * F), F32), jax.ShapeDtypeStruct((3, 2 * F), F32)],
        compiler_params=_cparams("parallel", "arbitrary"),
        name=name,
    )(df, df, a, a, b, b, up, cw)


def _gate_bwd(dmixed, gates, y_pool, y_attn, in_width, name):
    S, D = dmixed.shape
    tm, tn = _tile(S, 512), _tile(D, 512)
    nj = D // tn
    pre0 = (in_width - 2 * D) // tn
    assert pre0 * tn == in_width - 2 * D

    def body(dm_ref, g_ref, yp_ref, ya_ref, dy_ref, dpre_ref, db_ref):
        j = pl.program_id(0)

        @pl.when(pl.program_id(1) == 0)
        def _():
            db_ref[...] = jnp.zeros_like(db_ref)

        def run(y_ref):
            dm = dm_ref[...].astype(F32)
            gv = g_ref[...].astype(F32)
            dy_ref[...] = (dm * gv).astype(BF16)
            dpre = dm * y_ref[...].astype(F32) * gv * (1.0 - gv)
            dpre_ref[...] = dpre.astype(BF16)
            db_ref[...] += jnp.sum(dpre, axis=0, keepdims=True)

        @pl.when(j < nj)
        def _():
            run(yp_ref)

        @pl.when(j >= nj)
        def _():
            run(ya_ref)

    tile2 = pl.BlockSpec((tm, tn), lambda j, i: (i, j))
    return pl.pallas_call(
        body,
        grid=(2 * nj, S // tm),
        in_specs=[
            pl.BlockSpec((tm, tn), lambda j, i: (i, lax.rem(j, nj))),
            tile2,
            pl.BlockSpec((tm, tn), lambda j, i: (i, jnp.minimum(j, nj - 1))),
            pl.BlockSpec((tm, tn), lambda j, i: (i, jnp.maximum(j - nj, 0))),
        ],
        out_specs=[tile2, pl.BlockSpec((tm, tn), lambda j, i: (i, pre0 + j)), pl.BlockSpec((1, tn), lambda j, i: (0, j))],
        out_shape=[jax.ShapeDtypeStruct((S, 2 * D), BF16), jax.ShapeDtypeStruct((S, in_width), BF16), jax.ShapeDtypeStruct((1, 2 * D), F32)],
        compiler_params=_cparams("parallel", "arbitrary"),
        name=name,
    )(dmixed, gates, y_pool, y_attn)


def _pool_counts(i, tm, rows, w):
    t = i * tm + lax.broadcasted_iota(jnp.int32, (rows, 1), 0)
    return jnp.minimum(t + 1, w).astype(F32)


def _pooled_groups(u_ref, uh_ref, i, tm, C):
    cur = u_ref[...]
    halo = jnp.where(i == 0, 0.0, uh_ref[...])
    xx = jnp.concatenate([halo, cur], axis=0)
    out = []
    s = xx
    for gi, w in enumerate(POOL_WINDOWS):
        s = s + pltpu.roll(s, w // 2, 0)
        tot = s[HALO:, 0:C]
        out.append(tot / _pool_counts(i, tm, tm, w) - cur[:, gi * C:(gi + 1) * C])
        s = s[:, C:] if gi + 1 < len(POOL_WINDOWS) else s
    return out


def _pool_fwd(u, wl, scale, name):
    S, PW = u.shape
    C = PW // len(POOL_WINDOWS)
    tm = _tile(S, 512)
    hb = tm // HALO

    def body(u_ref, uh_ref, wl_ref, sc_ref, o_ref):
        i = pl.program_id(0)
        pooled = _pooled_groups(u_ref, uh_ref, i, tm, C)
        for gi in range(len(POOL_WINDOWS)):
            y = _dot_nn(pooled[gi].astype(BF16), wl_ref[gi])
            o_ref[:, gi * C:(gi + 1) * C] = (y * sc_ref[:, gi * C:(gi + 1) * C]).astype(o_ref.dtype)

    return pl.pallas_call(
        body,
        grid=(S // tm,),
        in_specs=[
            pl.BlockSpec((tm, PW), lambda i: (i, 0)),
            pl.BlockSpec((HALO, PW), lambda i: (jnp.maximum(i * hb - 1, 0), 0)),
            pl.BlockSpec((len(POOL_WINDOWS), C, C), lambda i: (0, 0, 0)),
            pl.BlockSpec((1, PW), lambda i: (0, 0)),
        ],
        out_specs=pl.BlockSpec((tm, PW), lambda i: (i, 0)),
        out_shape=jax.ShapeDtypeStruct((S, PW), BF16),
        compiler_params=_cparams("parallel"),
        name=name,
    )(u, u, wl, scale)


def _pool_bwd(u, dp, wl, scale, dproj, name):
    S, PW = u.shape
    G = len(POOL_WINDOWS)
    C = PW // G
    tm = _tile(S, 512)
    hb, ni = tm // HALO, S // tm
    n = tm + HALO

    def body(u_ref, uh_ref, dp_ref, dpn_ref, wl_ref, sc_ref, _, du_ref, dwl_ref, dsc_ref):
        i = pl.program_id(0)

        @pl.when(i == 0)
        def _():
            dwl_ref[...] = jnp.zeros_like(dwl_ref)
            dsc_ref[...] = jnp.zeros_like(dsc_ref)

        pooled = _pooled_groups(u_ref, uh_ref, i, tm, C)
        dpc = dp_ref[...].astype(F32)
        dpn = jnp.where(i == ni - 1, 0.0, dpn_ref[...].astype(F32))
        sc = sc_ref[...]
        dyl = jnp.concatenate([dpc, dpn], axis=0) * sc
        for gi, w in enumerate(POOL_WINDOWS):
            cols = slice(gi * C, (gi + 1) * C)
            pb = pooled[gi].astype(BF16)
            ylin = _dot_nn(pb, wl_ref[gi])
            dsc_ref[:, cols] += jnp.sum(dpc[:, cols] * ylin, axis=0, keepdims=True)
            dylg = dyl[:, cols].astype(BF16)
            dwl_ref[gi] += _dot(pb, dylg[:tm], ((0,), (0,)))
            dpool = _dot_nt(dylg, wl_ref[gi])
            e = dpool / _pool_counts(i, tm, n, w)
            k = 1
            while k < w:
                e = e + pltpu.roll(e, n - k, 0)
                k *= 2
            du_ref[:, cols] = (e[:tm] - dpool[:tm]).astype(du_ref.dtype)

    return pl.pallas_call(
        body,
        grid=(ni,),
        in_specs=[
            pl.BlockSpec((tm, PW), lambda i: (i, 0)),
            pl.BlockSpec((HALO, PW), lambda i: (jnp.maximum(i * hb - 1, 0), 0)),
            pl.BlockSpec((tm, PW), lambda i: (i, 0)),
            pl.BlockSpec((HALO, PW), lambda i: (jnp.minimum((i + 1) * hb, S // HALO - 1), 0)),
            pl.BlockSpec((G, C, C), lambda i: (0, 0, 0)),
            pl.BlockSpec((1, PW), lambda i: (0, 0)),
            ANY,
        ],
        out_specs=[pl.BlockSpec((tm, PW), lambda i: (i, 0)), pl.BlockSpec((G, C, C), lambda i: (0, 0, 0)), pl.BlockSpec((1, PW), lambda i: (0, 0))],
        out_shape=[jax.ShapeDtypeStruct(dproj.shape, dproj.dtype), jax.ShapeDtypeStruct((G, C, C), F32), jax.ShapeDtypeStruct((1, PW), F32)],
        input_output_aliases={6: 0},
        compiler_params=_cparams("arbitrary"),
        name=name,
    )(u, u, dp, dp, wl, scale, dproj)


def _band_masks():
    ii = lax.broadcasted_iota(jnp.int32, (SPAN, SPAN), 0)
    kk = lax.broadcasted_iota(jnp.int32, (SPAN, SPAN), 1)
    return ((ii + SPAN - kk).astype(F32), kk >= ii), ((ii - kk).astype(F32), kk <= ii)


ATTN_TILE = 16 * SPAN


def _unit_rows(r, b, d, blocks=1):
    return pl.ds(d * SPAN * b + r, blocks * SPAN, stride=d) if d > 1 else pl.ds(SPAN * b, blocks * SPAN)


def _f32_copies(refs, scratch, d):
    if d == 1:
        return list(refs)
    for ref, s in zip(refs, scratch):
        s[...] = ref[...].astype(F32)
    return list(scratch)


def _attn_fwd(qkv, d, g, name):
    S = qkv.shape[0]
    T = min(ATTN_TILE, S)
    P = SPAN * d
    nbk = T // P

    def body(q_ref, k_ref, v_ref, kp_ref, vp_ref, o_ref, lse_ref, *scratch):
        c = pl.program_id(0)
        (jp, mp), (jc, mc) = _band_masks()
        slopes = [ALIBI_SLOPES[g * HEADS_PER_GROUP + h] * d for h in range(HEADS_PER_GROUP)]
        slope = slopes[0]
        for h in range(1, HEADS_PER_GROUP):
            slope = jnp.where(pl.program_id(1) == h, slopes[h], slope)
        q_s, k_s, v_s, kp_s, vp_s = _f32_copies((q_ref, k_ref, v_ref, kp_ref, vp_ref), scratch[:5], d)
        o_s, l_s = (o_ref, lse_ref) if d == 1 else scratch[5:7]
        bias_p, bias_c = jnp.where(mp, -slope * jp, NEG_BIG), jnp.where(mc, -slope * jc, NEG_BIG)
        bias = jnp.concatenate([bias_p, bias_c], axis=1)
        bias_first = jnp.concatenate([jnp.where(c > 0, bias_p, NEG_BIG), bias_c], axis=1)
        for r in range(d):
            for b in range(nbk):
                rows = _unit_rows(r, b, d)
                q = q_s[rows, :].astype(BF16)
                if b == 0:
                    prev = _unit_rows(r, 0, d)
                    kk = jnp.concatenate([kp_s[prev, :], k_s[rows, :]], axis=0).astype(BF16)
                    vv = jnp.concatenate([vp_s[prev, :], v_s[rows, :]], axis=0).astype(BF16)
                else:
                    both = _unit_rows(r, b - 1, d, 2)
                    kk, vv = k_s[both, :].astype(BF16), v_s[both, :].astype(BF16)
                s = _dot_nt(q, kk) * ATTN_SCALE + (bias_first if b == 0 else bias)
                m = jnp.max(s, axis=-1, keepdims=True)
                p = jnp.exp(s - m)
                l = jnp.sum(p, axis=-1, keepdims=True)
                o_s[rows, :] = _dot_nn(p.astype(BF16), vv) / l
                l_s[rows, :] = jnp.broadcast_to(m + jnp.log(l), (SPAN, HEAD_DIM))
        if d > 1:
            o_ref[...] = o_s[...]
            lse_ref[...] = l_s[...]

    col = lambda kind: (lambda c, h: (c, kind * N_ATTN_HEADS + g * HEADS_PER_GROUP + h))
    pcol = lambda kind: (lambda c, h: (jnp.maximum(c * nbk - 1, 0), kind * N_ATTN_HEADS + g * HEADS_PER_GROUP + h))
    cur = lambda kind: pl.BlockSpec((T, HEAD_DIM), col(kind))
    prv = lambda kind: pl.BlockSpec((P, HEAD_DIM), pcol(kind))
    out = pl.BlockSpec((T, HEAD_DIM), lambda c, h: (c, h))
    scratch = [] if d == 1 else [pltpu.VMEM((T, HEAD_DIM), F32)] * 3 + [pltpu.VMEM((P, HEAD_DIM), F32)] * 2 + [pltpu.VMEM((T, HEAD_DIM), F32)] * 2
    return pl.pallas_call(
        body,
        grid=(S // T, HEADS_PER_GROUP),
        in_specs=[cur(0), cur(1), cur(2), prv(1), prv(2)],
        out_specs=[out, out],
        out_shape=[jax.ShapeDtypeStruct((S, GROUP_WIDTH), F32)] * 2,
        scratch_shapes=scratch,
        compiler_params=_cparams("parallel", "parallel"),
        name=name,
    )(qkv, qkv, qkv, qkv, qkv)


def _attn_merge(os_, lses, name):
    S, W = os_[0].shape
    tm = _tile(S, 512)

    def body(o0, o1, o2, l0, l1, l2, y_ref, lse_ref):
        ls = [l0[...], l1[...], l2[...]]
        m = jnp.maximum(jnp.maximum(ls[0], ls[1]), ls[2])
        es = [jnp.exp(v - m) for v in ls]
        tot = es[0] + es[1] + es[2]
        y = (es[0] * o0[...] + es[1] * o1[...] + es[2] * o2[...]) / tot
        y_ref[...] = y.astype(y_ref.dtype)
        lse_ref[...] = m + jnp.log(tot)

    row = pl.BlockSpec((tm, W), lambda i: (i, 0))
    return pl.pallas_call(
        body,
        grid=(S // tm,),
        in_specs=[row] * 6,
        out_specs=[row, row],
        out_shape=[jax.ShapeDtypeStruct((S, W), BF16), jax.ShapeDtypeStruct((S, W), F32)],
        compiler_params=_cparams("parallel"),
        name=name,
    )(*os_, *lses)


def _attn_bwd(qkv, dattn, y, lse, dproj, d, g, col0, name):
    S = qkv.shape[0]
    T = min(ATTN_TILE, S)
    P = SPAN * d
    nbk = T // P
    ntile = S // T

    def body(q_ref, k_ref, v_ref, kp_ref, vp_ref, qn_ref, da_ref, dan_ref, y_ref, yn_ref, lse_ref, lsen_ref, _, out_ref, dq_s, dk_s, dv_s, *scratch):
        c = pl.program_id(0)
        head_id = pl.program_id(1)
        kind = pl.program_id(2)

        @pl.when(kind == 0)
        def _():
            (jp, mp), (jc, mc) = _band_masks()
            slopes = [ALIBI_SLOPES[g * HEADS_PER_GROUP + h] * d for h in range(HEADS_PER_GROUP)]
            slope = slopes[0]
            for h in range(1, HEADS_PER_GROUP):
                slope = jnp.where(head_id == h, slopes[h], slope)
            q_s, k_s, v_s, da_s, y_s, kp_s, vp_s, qn_s, dan_s, yn_s = _f32_copies(
                (q_ref, k_ref, v_ref, da_ref, y_ref, kp_ref, vp_ref, qn_ref, dan_ref, yn_ref), scratch, d)
            bias_p, bias_c = jnp.where(mp, -slope * jp, NEG_BIG), jnp.where(mc, -slope * jc, NEG_BIG)
            bias = jnp.concatenate([bias_c, bias_p], axis=0)
            bias_last = jnp.concatenate([bias_c, jnp.where(c < ntile - 1, bias_p, NEG_BIG)], axis=0)
            bias_first = jnp.where(c > 0, bias_p, NEG_BIG)

            def pair(q, da, yy, lse_blk, kk, vv, b):
                dd = jnp.sum(da.astype(F32) * yy.astype(F32), axis=-1, keepdims=True)
                p = jnp.exp(_dot_nt(q, kk) * ATTN_SCALE + b - lse_blk[:, 0:1])
                return p, p * (_dot_nt(da, vv) - dd)

            for r in range(d):
                first = _unit_rows(r, 0, d)
                kk, vv = kp_s[first, :].astype(BF16), vp_s[first, :].astype(BF16)
                _, ds = pair(q_s[first, :].astype(BF16), da_s[first, :].astype(BF16), y_s[first, :], lse_ref[first, :], kk, vv, bias_first)
                dq_next = _dot_nn(ds.astype(BF16), kk)
                for kb in range(nbk):
                    rows = _unit_rows(r, kb, d)
                    if kb + 1 < nbk:
                        both = _unit_rows(r, kb, d, 2)
                        q, da, yy, lse_blk = q_s[both, :], da_s[both, :], y_s[both, :], lse_ref[both, :]
                    else:
                        q = jnp.concatenate([q_s[rows, :], qn_s[first, :]], axis=0)
                        da = jnp.concatenate([da_s[rows, :], dan_s[first, :]], axis=0)
                        yy = jnp.concatenate([y_s[rows, :], yn_s[first, :]], axis=0)
                        lse_blk = jnp.concatenate([lse_ref[rows, :], lsen_ref[first, :]], axis=0)
                    q, da = q.astype(BF16), da.astype(BF16)
                    kk, vv = k_s[rows, :].astype(BF16), v_s[rows, :].astype(BF16)
                    p, ds = pair(q, da, yy, lse_blk, kk, vv, bias if kb + 1 < nbk else bias_last)
                    dv_s[rows, :] = _dot_nn(p.T.astype(BF16), da)
                    dk_s[rows, :] = _dot_nn(ds.T.astype(BF16), q) * ATTN_SCALE
                    dq_both = _dot_nn(ds.astype(BF16), kk)
                    dq_s[rows, :] = (dq_next + dq_both[:SPAN]) * ATTN_SCALE
                    dq_next = dq_both[SPAN:]
            out_ref[...] = dq_s[...].astype(out_ref.dtype)

        @pl.when(kind == 1)
        def _():
            out_ref[...] = dk_s[...].astype(out_ref.dtype)

        @pl.when(kind == 2)
        def _():
            out_ref[...] = dv_s[...].astype(out_ref.dtype)

    head = lambda h: g * HEADS_PER_GROUP + h
    cur = lambda kind: pl.BlockSpec((T, HEAD_DIM), lambda c, h, kd: (c, kind * N_ATTN_HEADS + head(h)))
    prv = lambda kind: pl.BlockSpec((P, HEAD_DIM), lambda c, h, kd: (jnp.maximum(c * nbk - 1, 0), kind * N_ATTN_HEADS + head(h)))
    nxt_row = lambda c: jnp.minimum((c + 1) * nbk, S // P - 1)
    qnext = pl.BlockSpec((P, HEAD_DIM), lambda c, h, kd: (nxt_row(c), head(h)))
    hcur = pl.BlockSpec((T, HEAD_DIM), lambda c, h, kd: (c, h))
    hnext = pl.BlockSpec((P, HEAD_DIM), lambda c, h, kd: (nxt_row(c), h))
    out = pl.BlockSpec((T, HEAD_DIM), lambda c, h, kd: (c, col0 + kd * N_ATTN_HEADS + head(h)))
    stage = [pltpu.VMEM((T, HEAD_DIM), F32)] * 3
    copies = [] if d == 1 else [pltpu.VMEM((T, HEAD_DIM), F32)] * 5 + [pltpu.VMEM((P, HEAD_DIM), F32)] * 5
    return pl.pallas_call(
        body,
        grid=(ntile, HEADS_PER_GROUP, 3),
        in_specs=[cur(0), cur(1), cur(2), prv(1), prv(2), qnext, hcur, hnext, hcur, hnext, hcur, hnext, ANY],
        out_specs=out,
        out_shape=jax.ShapeDtypeStruct(dproj.shape, dproj.dtype),
        input_output_aliases={12: 0},
        scratch_shapes=stage + copies,
        compiler_params=_cparams("parallel", "parallel", "arbitrary"),
        name=name,
    )(qkv, qkv, qkv, qkv, qkv, qkv, dattn, dattn, y, y, lse, lse, dproj)


def _row_block(R, C, bytes_per_row_elem=4, budget=1 << 20):
    if R % 8:
        return R
    best = 8
    t = 8
    while t <= R:
        if R % t == 0 and t * C * bytes_per_row_elem <= budget:
            best = t
        t += 8
    return best


def _adamw(w, g, m, v, name):
    R, C = w.shape
    tr = _row_block(R, C)
    c1 = 1.0 - ADAM_B1 ** ADAM_STEP
    c2 = 1.0 - ADAM_B2 ** ADAM_STEP

    def body(w_ref, g_ref, m_ref, v_ref, d_ref, nm_ref, nv_ref):
        gv = g_ref[...]
        nm = ADAM_B1 * m_ref[...] + (1.0 - ADAM_B1) * gv
        nv = ADAM_B2 * v_ref[...] + (1.0 - ADAM_B2) * (gv * gv)
        d_ref[...] = -ADAM_LR * ((nm / c1) / (jnp.sqrt(nv / c2) + ADAM_EPS) + ADAM_WD * w_ref[...])
        nm_ref[...] = nm
        nv_ref[...] = nv

    blk = pl.BlockSpec((tr, C), lambda i: (i, 0))
    return pl.pallas_call(
        body,
        grid=(R // tr,),
        in_specs=[blk] * 4,
        out_specs=[blk] * 3,
        out_shape=[jax.ShapeDtypeStruct((R, C), F32)] * 3,
        compiler_params=_cparams("parallel"),
        name=name,
    )(w, g, m, v)


def _sum_pieces(grad, axis, recv, pos, name):
    n, pr, pc = recv.shape
    tr = _row_block(pr, pc, bytes_per_row_elem=(n + 1) * recv.dtype.itemsize, budget=4 << 20)
    nblk = pr // tr
    if axis == 1:
        own_map = lambda i, p: (p[1] * nblk + i, p[0])
    else:
        own_map = lambda i, p: ((2 * p[0] + p[1]) * nblk + i, 0)

    def body(p_ref, own_ref, r_ref, o_ref):
        acc = own_ref[...].astype(F32)
        for s in range(n):
            acc = acc + r_ref[s].astype(F32)
        o_ref[...] = acc

    return pl.pallas_call(
        body,
        grid_spec=pltpu.PrefetchScalarGridSpec(
            num_scalar_prefetch=1,
            grid=(nblk,),
            in_specs=[pl.BlockSpec((tr, pc), own_map), pl.BlockSpec((n, tr, pc), lambda i, p: (0, i, 0))],
            out_specs=pl.BlockSpec((tr, pc), lambda i, p: (p[1] * nblk + i, 0)),
        ),
        out_shape=jax.ShapeDtypeStruct((2 * pr, pc), F32),
        compiler_params=_cparams("parallel"),
        name=name,
    )(pos, grad, recv)


def _sum_small(own, recv, me, name):
    n, R, C = recv.shape
    tr = _row_block(R, C, bytes_per_row_elem=(n + 1) * 4, budget=4 << 20)

    def body(me_ref, own_ref, r_ref, o_ref):
        acc = None
        for dev in range(n + 1):
            k = jnp.bitwise_xor(me_ref[0], dev)
            term = jnp.where(k == 0, own_ref[...], r_ref[jnp.maximum(k - 1, 0)])
            acc = term if acc is None else acc + term
        o_ref[...] = acc

    return pl.pallas_call(
        body,
        grid_spec=pltpu.PrefetchScalarGridSpec(
            num_scalar_prefetch=1,
            grid=(R // tr,),
            in_specs=[pl.BlockSpec((tr, C), lambda i, m: (i, 0)), pl.BlockSpec((n, tr, C), lambda i, m: (0, i, 0))],
            out_specs=pl.BlockSpec((tr, C), lambda i, m: (i, 0)),
        ),
        out_shape=jax.ShapeDtypeStruct((R, C), F32),
        compiler_params=_cparams("parallel"),
        name=name,
    )(me, own, recv)


def _place(shard, axis, pos, dtype, name):
    shp = list(shard.shape)
    shp[axis] *= N_CHIPS
    if shard.ndim == 3:
        assert axis == 1
        in_spec = pl.BlockSpec(shard.shape, lambda i, p: (0, 0, 0))
        out_spec = pl.BlockSpec(shard.shape, lambda i, p: (0, p[0], 0))
        grid = (1,)
    else:
        R, C = shard.shape
        tr = _row_block(R, C, bytes_per_row_elem=4, budget=2 << 20)
        nblk = R // tr
        in_spec = pl.BlockSpec((tr, C), lambda i, p: (i, 0))
        out_spec = pl.BlockSpec((tr, C), (lambda i, p: (i, p[0])) if axis == 1 else (lambda i, p: (p[0] * nblk + i, 0)))
        grid = (nblk,)

    def body(p_ref, s_ref, o_ref):
        o_ref[...] = s_ref[...].astype(o_ref.dtype)

    return pl.pallas_call(
        body,
        grid_spec=pltpu.PrefetchScalarGridSpec(num_scalar_prefetch=1, grid=grid, in_specs=[in_spec], out_specs=out_spec),
        out_shape=jax.ShapeDtypeStruct(tuple(shp), dtype),
        compiler_params=_cparams("parallel"),
        name=name,
    )(pos, shard)


HBM = pl.BlockSpec(memory_space=pltpu.HBM)
SEM = pl.BlockSpec(memory_space=pltpu.SEMAPHORE)
DATAFLOW = pltpu.SideEffectType.DATAFLOW_SIDE_EFFECTING


def _position():
    return lax.axis_index("x"), lax.axis_index("y"), lax.axis_index("c")


def _peer(k):
    x, y, c = _position()
    return ((1 - x) if k & 4 else x, (1 - y) if k & 2 else y, (1 - c) if k & 1 else c)


def _shard_slice(ref, axis, idx, size):
    start = idx * size
    if axis == ref.ndim - 1:
        start = pl.multiple_of(start, 128)
    ix = [slice(None)] * ref.ndim
    ix[axis] = pl.ds(start, size)
    return ref.at[tuple(ix)]


def _gather_plan(axes):
    def plan(refs):
        x, y, c = _position()
        out = []
        for ref, ax in zip(refs, axes):
            mine = _shard_slice(ref, ax, 2 * x + y, ref.shape[ax] // N_CHIPS)
            for k in (4, 2, 6):
                px, py, _ = _peer(k)
                out.append((mine, mine, (px, py, c)))
        return out
    return plan


def _scatter_plan(axes):
    m = len(axes)

    def plan(refs):
        out = []
        for t in range(m):
            grad, recv = refs[t], refs[m + t]
            _, pr, pc = recv.shape
            for k in range(1, N_DEV):
                px, py, pcore = _peer(k)
                if axes[t] == 0:
                    piece = grad.at[pl.ds(((2 * px + py) * 2 + pcore) * pr, pr), :]
                else:
                    piece = grad.at[pl.ds(pcore * pr, pr), pl.ds(pl.multiple_of((2 * px + py) * pc, 128), pc)]
                out.append((piece, recv.at[k - 1], (px, py, pcore)))
        return out
    return plan


def _broadcast_plan(refs):
    small, recv = refs
    return [(small, recv.at[k - 1], _peer(k)) for k in range(1, N_DEV)]


def _start_all(plan, refs, send_sems, recv_sems):
    for q, (src, dst, dev) in enumerate(plan(refs)):
        pltpu.make_async_remote_copy(src_ref=src, dst_ref=dst, send_sem=send_sems.at[q], recv_sem=recv_sems.at[q], device_id=dev, device_id_type=MESH).start()


def _wait_all(plan, refs, send_sems, recv_sems):
    for q, (src, dst, dev) in enumerate(plan(refs)):
        cp = pltpu.make_async_remote_copy(src_ref=src, dst_ref=dst, send_sem=send_sems.at[q], recv_sem=recv_sems.at[q], device_id=dev, device_id_type=MESH)
        cp.wait_send()
        cp.wait_recv()


def _push(bufs, plan, ncopies, name):
    n = len(bufs)

    def body(*refs):
        outs = refs[n:2 * n]
        send_sems, recv_sems = refs[2 * n:]
        _start_all(plan, outs, send_sems, recv_sems)
        _wait_all(plan, outs, send_sems, recv_sems)

    return pl.pallas_call(
        body,
        in_specs=[ANY] * n,
        out_specs=[ANY] * n,
        out_shape=[jax.ShapeDtypeStruct(b.shape, b.dtype) for b in bufs],
        input_output_aliases={t: t for t in range(n)},
        scratch_shapes=[pltpu.SemaphoreType.DMA((ncopies,)), pltpu.SemaphoreType.DMA((ncopies,))],
        name=name,
    )(*bufs)


def _gather_once_per_chip(full, name):
    R, C = full.shape
    R2, C4 = R // 2, C // N_CHIPS

    def body(_, ref, send_sems, recv_sems):
        x, y, c = _position()
        chips = [_peer(k)[:2] for k in (4, 2, 6)]

        def half(chip, core):
            return ref.at[pl.ds(core * R2, R2), pl.ds(pl.multiple_of(chip * C4, 128), C4)]

        def copy(q, chip, core, to):
            return pltpu.make_async_remote_copy(src_ref=half(chip, core), dst_ref=half(chip, core), send_sem=send_sems.at[q], recv_sem=recv_sems.at[q],
                                                device_id=to, device_id_type=MESH)

        sends = [copy(q, 2 * x + y, c, (px, py, c)) for q, (px, py) in enumerate(chips)]
        for cp in sends:
            cp.start()
        for q, (px, py) in enumerate(chips):
            copy(q, 2 * px + py, c, (px, py, c)).wait_recv()
            passed = copy(3 + q, 2 * px + py, c, (x, y, 1 - c))
            passed.start()
            sends.append(passed)
        for q, (px, py) in enumerate(chips):
            copy(3 + q, 2 * px + py, 1 - c, (x, y, 1 - c)).wait_recv()
        for cp in sends:
            cp.wait_send()

    return pl.pallas_call(
        body,
        in_specs=[ANY],
        out_specs=ANY,
        out_shape=jax.ShapeDtypeStruct(full.shape, full.dtype),
        input_output_aliases={0: 0},
        scratch_shapes=[pltpu.SemaphoreType.DMA((6,)), pltpu.SemaphoreType.DMA((6,))],
        name=name,
    )(full)


def _push_start(bufs, plan, ncopies, name, after=None):
    n = len(bufs)
    extra = [] if after is None else [after]

    def body(*refs):
        ins = refs[:n]
        first_out = n + len(extra)
        send_sems, recv_sems, token = refs[first_out], refs[first_out + 1], refs[-1]
        _start_all(plan, ins, send_sems, recv_sems)
        token[...] = jnp.zeros_like(token)

    res = pl.pallas_call(
        body,
        name=name,
        out_shape=(pltpu.SemaphoreType.DMA((ncopies,)), pltpu.SemaphoreType.DMA((ncopies,)), *[pltpu.HBM(b.shape, b.dtype) for b in bufs],
                   jax.ShapeDtypeStruct((8, 128), F32)),
        in_specs=[HBM] * n + [ANY] * len(extra),
        out_specs=(SEM, SEM, *[HBM] * n, pl.BlockSpec(memory_space=pltpu.VMEM)),
        input_output_aliases={t: t + 2 for t in range(n)},
        compiler_params=pltpu.CompilerParams(has_side_effects=DATAFLOW),
    )(*[pltpu.with_memory_space_constraint(b, pltpu.HBM) for b in bufs], *extra)
    return res[0], res[1], list(res[2:2 + n]), res[-1]


def _push_wait(send_sems, recv_sems, bufs, plan, after, name):
    n = len(bufs)

    def body(*refs):
        ins = refs[:n]
        _wait_all(plan, ins, refs[n], refs[n + 1])

    return pl.pallas_call(
        body,
        name=name,
        out_shape=tuple(pltpu.HBM(b.shape, b.dtype) for b in bufs),
        in_specs=[HBM] * n + [SEM, SEM, ANY],
        out_specs=tuple([HBM] * n),
        input_output_aliases={t: t for t in range(n)},
        compiler_params=pltpu.CompilerParams(has_side_effects=DATAFLOW),
    )(*bufs, send_sems, recv_sems, after)


EXCHANGE_CHUNKS = 2


def _exchange_plan(refs):
    x, y, c = _position()
    out = []
    for ref in refs:
        rows = ref.shape[0] // (2 * EXCHANGE_CHUNKS)
        for q in range(EXCHANGE_CHUNKS):
            mine = ref.at[pl.ds((c * EXCHANGE_CHUNKS + q) * rows, rows), :]
            out.append((mine, mine, (x, y, 1 - c)))
    return out


LATE_WEIGHTS = (("w_pool_lin", "w_pool_out", "w_attn_out", "w_out"), ("w_up", "conv_w", "w_down"))


def _local_step(x, tgt, w, late_weights, send):
    S, D = x.shape
    PW = w["pool_scale"].shape[1]
    o_q = PW
    o_g = PW + 3 * ATTN_WIDTH
    QKV = 3 * ATTN_WIDTH

    h1 = _rms_fwd(x, w["g_mix"], "rms1")
    proj_tiles = (_tile(S, 1024), 512, D)
    u = _mm(h1, w["w_in"], mode="nn", dims=(S, PW, D), tiles=proj_tiles, out_dtypes=(F32,), name="proj_u")
    qkv = _mm(h1, w["w_in"], mode="nn", dims=(S, QKV, D), tiles=proj_tiles, b_off=(0, o_q), name="proj_qkv")

    def gate_epilogue(acc, ex, outs):
        outs[0][...] = (1.0 / (1.0 + jnp.exp(-(acc + ex[0][...])))).astype(outs[0].dtype)

    gates = _mm(h1, w["w_in"], mode="nn", dims=(S, 2 * D, D), tiles=proj_tiles, b_off=(0, o_g), epilogue=gate_epilogue,
                extras=[(w["b_gate"], "n", (0, 0))], name="proj_gates")

    os_, lses = [], []
    for gi, (_, d) in enumerate(ATTN_GROUPS):
        o, lse = _attn_fwd(qkv, d, gi, f"attn_fwd{gi}")
        os_.append(o)
        lses.append(lse)
    attn, lse_tot = _attn_merge(os_, lses, "attn_merge")

    w = dict(w, **late_weights(0, attn))
    pool_out = _pool_fwd(u, w["w_pool_lin"], w["pool_scale"], "pool_fwd")
    y_pool = _mm(pool_out, w["w_pool_out"], mode="nn", dims=(S, D, PW), name="y_pool")

    def mix_epilogue(acc, ex, outs):
        outs[0][...] = acc.astype(BF16)
        outs[1][...] = (ex[0][...].astype(F32) * ex[2][...].astype(F32) + ex[1][...].astype(F32) * acc).astype(BF16)

    y_attn, mixed = _mm(attn, w["w_attn_out"], mode="nn", dims=(S, D, GROUP_WIDTH), out_dtypes=(BF16, BF16), epilogue=mix_epilogue,
                        extras=[(gates, "mn", (0, 0)), (gates, "mn", (0, D)), (y_pool, "mn", (0, 0))], name="y_attn_mix")

    def residual_epilogue(acc, ex, outs):
        outs[0][...] = ex[0][...] + acc

    x2 = _mm(mixed, w["w_out"], mode="nn", dims=(S, D, D), out_dtypes=(F32,), epilogue=residual_epilogue, extras=[(x, "mn", (0, 0))], name="out_proj")

    h2 = _rms_fwd(x2, w["g_ffn"], "rms2")
    w = dict(w, **late_weights(1, h2))
    F = w["w_down"].shape[0]
    up = _mm(h2, w["w_up"], mode="nn", dims=(S, 2 * F, D), name="up_proj")
    f, act_a, act_b = _convglu_fwd(up, w["conv_w"], w["conv_b"], "convglu_fwd")
    x3 = _mm(f, w["w_down"], mode="nn", dims=(S, D, F), out_dtypes=(F32,), epilogue=residual_epilogue, extras=[(x2, "mn", (0, 0))], name="down_proj")

    g = {}
    dx3b, g["g_final"], loss_cols = _loss_head(x3, tgt, w["g_final"], "loss_head")

    g["w_down"] = _mm(f, dx3b, mode="tn", dims=(F, D, S), name="dw_down")
    sent = send(("w_down",), g)
    df = _mm(dx3b, w["w_down"], mode="nt", dims=(S, F, D), name="d_f")
    dup, g["conv_b"], g["conv_w"] = _convglu_bwd(df, act_a, act_b, up, w["conv_w"] + sent, "convglu_bwd")
    g["w_up"] = _mm(h2, dup, mode="tn", dims=(D, 2 * F, S), name="dw_up")
    sent = send(("w_up",), g)
    dh2 = _mm(dup, w["w_up"], mode="nt", dims=(S, D, 2 * F), name="d_h2")
    dx2b, g["g_ffn"] = _rms_bwd(dh2, x2, w["g_ffn"] + sent, dx3b, "rms2_bwd", BF16)

    g["w_out"] = _mm(mixed, dx2b, mode="tn", dims=(D, D, S), name="dw_out")
    dmixed = _mm(dx2b, w["w_out"], mode="nt", dims=(S, D, D), name="d_mixed")
    IN = w["w_in"].shape[1]
    dy_both, dproj, g["b_gate"] = _gate_bwd(dmixed, gates, y_pool, y_attn, IN, "gate_bwd")

    g["w_pool_out"] = _mm(pool_out, dy_both, mode="tn", dims=(PW, D, S), name="dw_pool_out")
    g["w_attn_out"] = _mm(attn, dy_both, mode="tn", dims=(GROUP_WIDTH, D, S), b_off=(0, D), name="dw_attn_out")
    sent = send(("w_out", "w_pool_out", "w_attn_out"), g)
    dpool = _mm(dy_both, w["w_pool_out"], mode="nt", dims=(S, PW, D), name="d_pool")
    dattn = _mm(dy_both, w["w_attn_out"], mode="nt", dims=(S, GROUP_WIDTH, D), a_off=(0, D), name="d_attn")

    dproj, g["w_pool_lin"], g["pool_scale"] = _pool_bwd(u, dpool, w["w_pool_lin"], w["pool_scale"] + sent, dproj, "pool_bwd")
    g["loss_cols"] = loss_cols
    sent = send("small", g)

    for gi, (_, d) in enumerate(ATTN_GROUPS):
        dproj = _attn_bwd(qkv, dattn, attn, lse_tot, dproj, d, gi, PW // HEAD_DIM, f"attn_bwd{gi}")

    g["w_in"] = _mm(h1, dproj, mode="tn", dims=(D, IN, S), name="dw_in")
    sent = sent + send(("w_in",), g)
    dh1 = _mm(dproj, w["w_in"], mode="nt", dims=(S, D, IN), tiles=(_tile(S, 1024), _tile(D, 2048), _tile(IN, 2432)), name="d_h1")
    (grad_x, g["g_mix"]) = _rms_bwd(dh1, x, w["g_mix"] + sent, dx2b, "rms1_bwd", F32)
    return loss_cols, grad_x, g


BIG = ("w_in", "w_pool_out", "w_attn_out", "w_out", "w_up", "w_down")
BIG_AXIS = {"w_in": 1, "w_pool_out": 1, "w_attn_out": 1, "w_out": 0, "w_up": 1, "w_down": 0}
GATHER_AXIS = dict(BIG_AXIS, w_pool_lin=1, conv_w=1)
SMALL = ("loss_cols", "b_gate", "w_pool_lin", "pool_scale", "g_ffn", "conv_w", "conv_b", "g_final")
SMALL_COLS = 1024
ORDER = ("g_mix", "w_in", "b_gate", "w_pool_lin", "pool_scale", "w_pool_out", "w_attn_out", "w_out", "g_ffn", "w_up", "conv_w", "conv_b", "w_down", "g_final")


def _as_rows(parts):
    flat = jnp.concatenate([p.astype(F32).reshape(-1) for p in parts])
    rows = -(-flat.shape[0] // (8 * SMALL_COLS)) * 8
    return jnp.pad(flat, (0, rows * SMALL_COLS - flat.shape[0])).reshape(rows, SMALL_COLS)


def kernel(x, g_mix, w_in, b_gate, w_pool_lin, pool_scale, w_pool_out, w_attn_out, w_out, g_ffn, w_up, conv_w, conv_b, w_down, g_final, loss_target, m_g_mix, m_w_in, m_b_gate, m_w_pool_lin, m_pool_scale, m_w_pool_out, m_w_attn_out, m_w_out, m_g_ffn, m_w_up, m_conv_w, m_conv_b, m_w_down, m_g_final, v_g_mix, v_w_in, v_b_gate, v_w_pool_lin, v_pool_scale, v_w_pool_out, v_w_attn_out, v_w_out, v_g_ffn, v_w_up, v_conv_w, v_conv_b, v_w_down, v_g_final):
    shard = dict(g_mix=g_mix, w_in=w_in, b_gate=b_gate, w_pool_lin=w_pool_lin, pool_scale=pool_scale, w_pool_out=w_pool_out, w_attn_out=w_attn_out,
                 w_out=w_out, g_ffn=g_ffn, w_up=w_up, conv_w=conv_w, conv_b=conv_b, w_down=w_down, g_final=g_final)
    mom = dict(g_mix=m_g_mix, w_in=m_w_in, b_gate=m_b_gate, w_pool_lin=m_w_pool_lin, pool_scale=m_pool_scale, w_pool_out=m_w_pool_out, w_attn_out=m_w_attn_out,
               w_out=m_w_out, g_ffn=m_g_ffn, w_up=m_w_up, conv_w=m_conv_w, conv_b=m_conv_b, w_down=m_w_down, g_final=m_g_final)
    vel = dict(g_mix=v_g_mix, w_in=v_w_in, b_gate=v_b_gate, w_pool_lin=v_w_pool_lin, pool_scale=v_pool_scale, w_pool_out=v_w_pool_out, w_attn_out=v_w_attn_out,
               w_out=v_w_out, g_ffn=v_g_ffn, w_up=v_w_up, conv_w=v_conv_w, conv_b=v_conv_b, w_down=v_w_down, g_final=v_g_final)
    chip = 2 * lax.axis_index("x") + lax.axis_index("y")
    pos = jnp.stack([chip, lax.axis_index("c")]).astype(jnp.int32)
    me = (2 * chip + lax.axis_index("c")).astype(jnp.int32).reshape(1)
    D = x.shape[2]

    placed = {k: _place(shard[k][0], GATHER_AXIS[k], pos, F32 if k == "conv_w" else BF16, f"place_{k}") for k in GATHER_AXIS}
    w_in_full = _gather_once_per_chip(placed["w_in"], "comm_gather_w_in")
    late, late_token, prior = [], 0.0, w_in_full
    for stage, names in enumerate(LATE_WEIGHTS):
        plan = _gather_plan([GATHER_AXIS[k] for k in names])
        send_sems, recv_sems, bufs, token = _push_start([placed[k] for k in names], plan, 3 * len(names), f"comm_gather_late{stage}_start", after=prior)
        late.append((names, send_sems, recv_sems, bufs, plan))
        late_token, prior = late_token + token[0, 0], token

    def late_weights(stage, after):
        names, send_sems, recv_sems, bufs, plan = late[stage]
        return dict(zip(names, _push_wait(send_sems, recv_sems, bufs, plan, after, f"comm_gather_late{stage}_wait")))

    pending = []

    def send(names, g):
        if names == "small":
            bufs = [_as_rows([g[k] for k in SMALL])]
            bufs.append(lax.empty((N_DEV - 1,) + bufs[0].shape, F32))
            plan, tag = _broadcast_plan, "small"
        else:
            bufs = [g[k] for k in names]
            for k in names:
                R, C = g[k].shape
                piece = (R // (2 * N_CHIPS), C) if BIG_AXIS[k] == 0 else (R // 2, C // N_CHIPS)
                bufs.append(lax.empty((N_DEV - 1,) + piece, BF16))
            plan, tag = _scatter_plan([BIG_AXIS[k] for k in names]), names[0]
        ncopies = (N_DEV - 1) * (len(bufs) // 2)
        send_sems, recv_sems, thru, token = _push_start(bufs, plan, ncopies, f"comm_scatter_start_{tag}")
        pending.append((names, send_sems, recv_sems, thru, plan, tag))
        return token[0, 0]

    w0 = dict(g_mix=shard["g_mix"] + late_token, w_in=w_in_full, b_gate=shard["b_gate"], pool_scale=shard["pool_scale"], g_ffn=shard["g_ffn"],
              conv_b=shard["conv_b"], g_final=shard["g_final"].reshape(1, D))
    _, grad_x, gr = _local_step(x[0], loss_target[0], w0, late_weights, send)

    halves, small_parts = {}, None
    for names, send_sems, recv_sems, thru, plan, tag in pending:
        done = _push_wait(send_sems, recv_sems, thru, plan, grad_x, f"comm_scatter_wait_{tag}")
        if names == "small":
            small_parts = _sum_small(done[0], done[1], me, "sum_small").reshape(-1)
        else:
            m = len(names)
            for t, k in enumerate(names):
                halves[k] = _sum_pieces(done[t], BIG_AXIS[k], done[m + t], pos, f"sum_{k}")
    g_mix_own = _as_rows([gr["g_mix"]])
    _, g_mix_recv = _push([g_mix_own, lax.empty((N_DEV - 1,) + g_mix_own.shape, F32)], _broadcast_plan, N_DEV - 1, "comm_gather_g_mix")
    g_mix_sum = _sum_small(g_mix_own, g_mix_recv, me, "sum_g_mix").reshape(-1)[:D]
    wholes = _push([halves[k] for k in BIG], _exchange_plan, EXCHANGE_CHUNKS * len(BIG), "comm_exchange_halves")

    grads = {"g_mix": g_mix_sum.reshape(shard["g_mix"].shape)}
    for k, whole in zip(BIG, wholes):
        grads[k] = whole.reshape(shard[k].shape)
    off = 0
    loss = None
    for k in SMALL:
        sz = math.prod(gr[k].shape)
        fullg = small_parts[off:off + sz].reshape(gr[k].shape)
        off += sz
        if k == "loss_cols":
            loss = jnp.sum(fullg)
            continue
        if k in ("w_pool_lin", "conv_w"):
            n = shard[k].shape[2]
            fullg = lax.dynamic_slice_in_dim(fullg, chip * n, n, axis=1)
        grads[k] = fullg.reshape(shard[k].shape)

    deltas, new_m, new_v = {}, {}, {}
    for k in ORDER:
        shp = shard[k].shape
        two_d = (-1, shp[-1])
        dl, nm, nv = _adamw(shard[k].reshape(two_d), grads[k].reshape(two_d), mom[k].reshape(two_d), vel[k].reshape(two_d), f"adamw_{k}")
        deltas[k], new_m[k], new_v[k] = dl.reshape(shp), nm.reshape(shp), nv.reshape(shp)

    return (loss, grad_x[None], *[grads[k] for k in ORDER], *[deltas[k] for k in ORDER], *[new_m[k] for k in ORDER], *[new_v[k] for k in ORDER])
```

```python
import functools
import math

import jax
import jax.numpy as jnp
from jax import lax
from jax.experimental import pallas as pl
from jax.experimental.pallas import tpu as pltpu

F32 = jnp.float32
BF16 = jnp.bfloat16

RMS_EPS = 1e-6
POOL_WINDOWS = (2, 4, 8, 16)
ATTN_GROUPS = ((128, 1), (512, 4), (2048, 16))
HEADS_PER_GROUP = 4
HEAD_DIM = 128
N_ATTN_HEADS = HEADS_PER_GROUP * len(ATTN_GROUPS)
SPAN = 128
GROUP_WIDTH = HEADS_PER_GROUP * HEAD_DIM
ATTN_WIDTH = N_ATTN_HEADS * HEAD_DIM
ATTN_SCALE = HEAD_DIM ** -0.5
NEG_BIG = -1e30
ALIBI_SLOPES = tuple(2.0 ** (-8.0 * (h + 1) / N_ATTN_HEADS) for h in range(N_ATTN_HEADS))

ADAM_LR = 0.001
ADAM_B1 = 0.9
ADAM_B2 = 0.999
ADAM_EPS = 1e-08
ADAM_WD = 0.01
ADAM_STEP = 10

INV_SQRT2 = 1.0 / math.sqrt(2.0)
INV_SQRT_2PI = 1.0 / math.sqrt(2.0 * math.pi)

HALO = 16
VMEM_LIMIT = 56 * 1024 * 1024
N_CHIPS = 4
N_DEV = 8
MESH = pl.DeviceIdType.MESH
ANY = pl.BlockSpec(memory_space=pl.ANY)


def _cparams(*sem):
    return pltpu.CompilerParams(dimension_semantics=sem, vmem_limit_bytes=VMEM_LIMIT)


def _tile(n, pref, mult=128):
    t = (min(pref, n) // mult) * mult
    while t >= mult:
        if n % t == 0:
            return t
        t -= mult
    return n


def _dot(a, b, contract):
    return lax.dot_general(a, b, (contract, ((), ())), preferred_element_type=F32)


def _dot_nn(a, b):
    return _dot(a, b, ((1,), (0,)))


def _dot_nt(a, b):
    return _dot(a, b, ((1,), (1,)))


def _mm(a, b, *, mode, dims, name, tiles=None, out_dtypes=(BF16,), epilogue=None, extras=(), a_off=(0, 0), b_off=(0, 0)):
    M, N, K = dims
    if tiles is None:
        tiles = (_tile(M, 1408), _tile(N, 2816), _tile(K, 512)) if mode == "tn" else (_tile(M, 1024), _tile(N, 1536), _tile(K, 2816))
    tm, tn, tk = tiles
    assert M % tm == 0 and N % tn == 0 and K % tk == 0, (name, dims, tiles)
    nk = K // tk
    if mode == "nn":
        ab, bb, contract = (tm, tk), (tk, tn), ((1,), (0,))
        amap = lambda i, j, k: (i + a_off[0] // tm, k + a_off[1] // tk)
        bmap = lambda i, j, k: (k + b_off[0] // tk, j + b_off[1] // tn)
    elif mode == "nt":
        ab, bb, contract = (tm, tk), (tn, tk), ((1,), (1,))
        amap = lambda i, j, k: (i + a_off[0] // tm, k + a_off[1] // tk)
        bmap = lambda i, j, k: (j + b_off[0] // tn, k + b_off[1] // tk)
    else:
        ab, bb, contract = (tk, tm), (tk, tn), ((0,), (0,))
        amap = lambda i, j, k: (k + a_off[0] // tk, i + a_off[1] // tm)
        bmap = lambda i, j, k: (k + b_off[0] // tk, j + b_off[1] // tn)
    assert a_off[0] % ab[0] == 0 and a_off[1] % ab[1] == 0 and b_off[0] % bb[0] == 0 and b_off[1] % bb[1] == 0, name
    in_specs = [pl.BlockSpec(ab, amap), pl.BlockSpec(bb, bmap)]
    ex_arrays = []
    for arr, kind, off in extras:
        if kind == "mn":
            assert off[0] % tm == 0 and off[1] % tn == 0, name
            in_specs.append(pl.BlockSpec((tm, tn), lambda i, j, k, off=off: (i + off[0] // tm, j + off[1] // tn)))
        else:
            assert off[1] % tn == 0, name
            in_specs.append(pl.BlockSpec((1, tn), lambda i, j, k, off=off: (0, j + off[1] // tn)))
        ex_arrays.append(arr)
    ne, no = len(ex_arrays), len(out_dtypes)
    if epilogue is None:
        def epilogue(acc, ex, outs):
            outs[0][...] = acc.astype(outs[0].dtype)

    def body(*refs):
        a_ref, b_ref = refs[0], refs[1]
        ex, outs = refs[2:2 + ne], refs[2 + ne:2 + ne + no]
        if nk == 1:
            epilogue(_dot(a_ref[...], b_ref[...], contract), ex, outs)
            return
        acc = refs[-1]
        k = pl.program_id(2)
        if nk <= 4:
            part = _dot(a_ref[...], b_ref[...], contract)

            @pl.when(k == 0)
            def _():
                acc[...] = part

            @pl.when(jnp.logical_and(k > 0, k < nk - 1))
            def _():
                acc[...] += part

            @pl.when(k == nk - 1)
            def _():
                epilogue(acc[...] + part, ex, outs)
        else:
            @pl.when(k == 0)
            def _():
                acc[...] = _dot(a_ref[...], b_ref[...], contract)

            @pl.when(k > 0)
            def _():
                acc[...] += _dot(a_ref[...], b_ref[...], contract)

            @pl.when(k == nk - 1)
            def _():
                epilogue(acc[...], ex, outs)

    res = pl.pallas_call(
        body,
        grid=(M // tm, N // tn, nk),
        in_specs=in_specs,
        out_specs=[pl.BlockSpec((tm, tn), lambda i, j, k: (i, j)) for _ in out_dtypes],
        out_shape=[jax.ShapeDtypeStruct((M, N), dt) for dt in out_dtypes],
        scratch_shapes=[pltpu.VMEM((tm, tn), F32)] if nk > 1 else [],
        compiler_params=_cparams("parallel", "parallel", "arbitrary"),
        name=name,
    )(a, b, *ex_arrays)
    return res[0] if no == 1 else res


def _rms_fwd(x, g, name):
    S, D = x.shape
    tm = _tile(S, 256)

    def body(x_ref, g_ref, h_ref):
        xv = x_ref[...]
        r = lax.rsqrt(jnp.mean(xv * xv, axis=-1, keepdims=True) + RMS_EPS)
        h_ref[...] = (xv * r * g_ref[...]).astype(h_ref.dtype)

    return pl.pallas_call(
        body,
        grid=(S // tm,),
        in_specs=[pl.BlockSpec((tm, D), lambda i: (i, 0)), pl.BlockSpec((1, D), lambda i: (0, 0))],
        out_specs=pl.BlockSpec((tm, D), lambda i: (i, 0)),
        out_shape=jax.ShapeDtypeStruct((S, D), BF16),
        compiler_params=_cparams("parallel"),
        name=name,
    )(x, g)


def _rms_bwd(dh, x, g, dres, name, out_dtype):
    S, D = x.shape
    tm = _tile(S, 256)

    def body(dh_ref, x_ref, g_ref, dres_ref, dx_ref, dg_ref):
        xv = x_ref[...]
        r = lax.rsqrt(jnp.mean(xv * xv, axis=-1, keepdims=True) + RMS_EPS)
        xr = xv * r
        dhv = dh_ref[...].astype(F32)

        @pl.when(pl.program_id(0) == 0)
        def _():
            dg_ref[...] = jnp.zeros_like(dg_ref)

        dg_ref[...] += jnp.sum(dhv * xr, axis=0, keepdims=True)
        u = dhv * g_ref[...]
        c = jnp.mean(u * xr, axis=-1, keepdims=True)
        dx_ref[...] = (dres_ref[...].astype(F32) + r * (u - xr * c)).astype(dx_ref.dtype)

    row = pl.BlockSpec((tm, D), lambda i: (i, 0))
    vec = pl.BlockSpec((1, D), lambda i: (0, 0))
    return pl.pallas_call(
        body,
        grid=(S // tm,),
        in_specs=[row, row, vec, row],
        out_specs=[row, vec],
        out_shape=[jax.ShapeDtypeStruct((S, D), out_dtype), jax.ShapeDtypeStruct((1, D), F32)],
        compiler_params=_cparams("arbitrary"),
        name=name,
    )(dh, x, g, dres)


def _loss_head(x3, tgt, g, name):
    S, D = x3.shape
    tm = _tile(S, 256)

    def body(x_ref, t_ref, g_ref, dxb_ref, dg_ref, loss_ref):
        xv = x_ref[...]
        gv = g_ref[...]
        r = lax.rsqrt(jnp.mean(xv * xv, axis=-1, keepdims=True) + RMS_EPS)
        xr = xv * r
        e = xr * gv - t_ref[...]

        @pl.when(pl.program_id(0) == 0)
        def _():
            dg_ref[...] = jnp.zeros_like(dg_ref)
            loss_ref[...] = jnp.zeros_like(loss_ref)

        loss_ref[...] += jnp.sum(e * e, axis=0, keepdims=True) * (0.5 / D)
        dy = e * (1.0 / D)
        dg_ref[...] += jnp.sum(dy * xr, axis=0, keepdims=True)
        u = dy * gv
        c = jnp.mean(u * xr, axis=-1, keepdims=True)
        dxb_ref[...] = (r * (u - xr * c)).astype(BF16)

    row = pl.BlockSpec((tm, D), lambda i: (i, 0))
    vec = pl.BlockSpec((1, D), lambda i: (0, 0))
    return pl.pallas_call(
        body,
        grid=(S // tm,),
        in_specs=[row, row, vec],
        out_specs=[row, vec, vec],
        out_shape=[jax.ShapeDtypeStruct((S, D), BF16), jax.ShapeDtypeStruct((1, D), F32), jax.ShapeDtypeStruct((1, D), F32)],
        compiler_params=_cparams("arbitrary"),
        name=name,
    )(x3, tgt, g)


def _conv_taps(cur_ref, halo_ref, w_ref, b_ref, first):
    cur = cur_ref[...].astype(F32)
    halo = jnp.where(first, 0.0, halo_ref[...].astype(F32))
    xx = jnp.concatenate([halo, cur], axis=0)
    p1 = pltpu.roll(xx, 1, 0)[HALO:]
    p2 = pltpu.roll(xx, 2, 0)[HALO:]
    w = w_ref[...]
    y = b_ref[...] + w[0:1] * p2 + w[1:2] * p1 + w[2:3] * cur
    return y, (cur, p1, p2)


def _convglu_fwd(up, cw, cb, name):
    S, F2 = up.shape
    F = F2 // 2
    tm, tn = _tile(S, 512), _tile(F, 512)
    nj, hb = F // tn, tm // HALO

    def body(ua, ub, ha, hb_, wa, wb, ba, bb, f_ref, a_ref, b_ref):
        first = pl.program_id(0) == 0
        a, _ = _conv_taps(ua, ha, wa, ba, first)
        b, _ = _conv_taps(ub, hb_, wb, bb, first)
        f_ref[...] = (0.5 * a * (1.0 + lax.erf(a * INV_SQRT2)) * b).astype(f_ref.dtype)
        a_ref[...] = a.astype(a_ref.dtype)
        b_ref[...] = b.astype(b_ref.dtype)

    tile = pl.BlockSpec((tm, tn), lambda i, j: (i, j))
    return pl.pallas_call(
        body,
        grid=(S // tm, nj),
        in_specs=[
            tile,
            pl.BlockSpec((tm, tn), lambda i, j: (i, j + nj)),
            pl.BlockSpec((HALO, tn), lambda i, j: (jnp.maximum(i * hb - 1, 0), j)),
            pl.BlockSpec((HALO, tn), lambda i, j: (jnp.maximum(i * hb - 1, 0), j + nj)),
            pl.BlockSpec((3, tn), lambda i, j: (0, j)),
            pl.BlockSpec((3, tn), lambda i, j: (0, j + nj)),
            pl.BlockSpec((1, tn), lambda i, j: (0, j)),
            pl.BlockSpec((1, tn), lambda i, j: (0, j + nj)),
        ],
        out_specs=[tile, tile, tile],
        out_shape=[jax.ShapeDtypeStruct((S, F), BF16)] * 3,
        compiler_params=_cparams("parallel", "parallel"),
        name=name,
    )(up, up, up, up, cw, cw, cb, cb)


def _convglu_bwd(df, a, b, up, cw, name):
    S, F = df.shape
    tm, tn = _tile(S, 512), _tile(F, 512)
    nj, ni, hb = F // tn, S // tm, tm // HALO
    n = tm + HALO

    def body(df_ref, dfn_ref, a_ref, an_ref, b_ref, bn_ref, up_ref, w_ref, o_ref, db_ref, dw_ref):
        j, i = pl.program_id(0), pl.program_id(1)
        last = i == ni - 1

        def rows(c_ref, n_ref):
            return jnp.concatenate([c_ref[...].astype(F32), jnp.where(last, 0.0, n_ref[...].astype(F32))], axis=0)

        @pl.when(i == 0)
        def _():
            db_ref[...] = jnp.zeros_like(db_ref)
            dw_ref[...] = jnp.zeros_like(dw_ref)

        def finish(d):
            d0 = d[:tm]
            d1 = pltpu.roll(d, n - 1, 0)[:tm]
            d2 = pltpu.roll(d, n - 2, 0)[:tm]
            w = w_ref[...]
            o_ref[...] = (w[2:3] * d0 + w[1:2] * d1 + w[0:1] * d2).astype(o_ref.dtype)
            upv = up_ref[...].astype(F32)
            db_ref[...] += jnp.sum(d0, axis=0, keepdims=True)
            dw_ref[0:1, :] += jnp.sum(d2 * upv, axis=0, keepdims=True)
            dw_ref[1:2, :] += jnp.sum(d1 * upv, axis=0, keepdims=True)
            dw_ref[2:3, :] += jnp.sum(d0 * upv, axis=0, keepdims=True)

        av, dfv = rows(a_ref, an_ref), rows(df_ref, dfn_ref)
        cdf = 0.5 * (1.0 + lax.erf(av * INV_SQRT2))

        @pl.when(j < nj)
        def _():
            pdf = jnp.exp(-0.5 * av * av) * INV_SQRT_2PI
            finish(dfv * rows(b_ref, bn_ref) * (cdf + av * pdf))

        @pl.when(j >= nj)
        def _():
            finish(dfv * (av * cdf))

    jh = lambda j: lax.rem(j, nj)
    nxt = lambda i: jnp.minimum((i + 1) * hb, S // HALO - 1)
    cur = pl.BlockSpec((tm, tn), lambda j, i: (i, jh(j)))
    halo = pl.BlockSpec((HALO, tn), lambda j, i: (nxt(i), jh(j)))
    return pl.pallas_call(
        body,
        grid=(2 * nj, ni),
        in_specs=[cur, halo, cur, halo, cur, halo, pl.BlockSpec((tm, tn), lambda j, i: (i, j)), pl.BlockSpec((3, tn), lambda j, i: (0, j))],
        out_specs=[pl.BlockSpec((tm, tn), lambda j, i: (i, j)), pl.BlockSpec((1, tn), lambda j, i: (0, j)), pl.BlockSpec((3, tn), lambda j, i: (0, j))],
        out_shape=[jax.ShapeDtypeStruct((S, 2 * F), BF16), jax.ShapeDtypeStruct((1, 2 * F), F32), jax.ShapeDtypeStruct((3, 2 * F), F32)],
        compiler_params=_cparams("parallel", "arbitrary"),
        name=name,
    )(df, df, a, a, b, b, up, cw)


def _gate_bwd(dmixed, gates, y_pool, y_attn, in_width, name):
    S, D = dmixed.shape
    tm, tn = _tile(S, 512), _tile(D, 512)
    nj = D // tn
    pre0 = (in_width - 2 * D) // tn
    assert pre0 * tn == in_width - 2 * D

    def body(dm_ref, g_ref, yp_ref, ya_ref, dy_ref, dpre_ref, db_ref):
        j = pl.program_id(0)

        @pl.when(pl.program_id(1) == 0)
        def _():
            db_ref[...] = jnp.zeros_like(db_ref)

        def run(y_ref):
            dm = dm_ref[...].astype(F32)
            gv = g_ref[...].astype(F32)
            dy_ref[...] = (dm * gv).astype(BF16)
            dpre = dm * y_ref[...].astype(F32) * gv * (1.0 - gv)
            dpre_ref[...] = dpre.astype(BF16)
            db_ref[...] += jnp.sum(dpre, axis=0, keepdims=True)

        @pl.when(j < nj)
        def _():
            run(yp_ref)

        @pl.when(j >= nj)
        def _():
            run(ya_ref)

    tile2 = pl.BlockSpec((tm, tn), lambda j, i: (i, j))
    return pl.pallas_call(
        body,
        grid=(2 * nj, S // tm),
        in_specs=[
            pl.BlockSpec((tm, tn), lambda j, i: (i, lax.rem(j, nj))),
            tile2,
            pl.BlockSpec((tm, tn), lambda j, i: (i, jnp.minimum(j, nj - 1))),
            pl.BlockSpec((tm, tn), lambda j, i: (i, jnp.maximum(j - nj, 0))),
        ],
        out_specs=[tile2, pl.BlockSpec((tm, tn), lambda j, i: (i, pre0 + j)), pl.BlockSpec((1, tn), lambda j, i: (0, j))],
        out_shape=[jax.ShapeDtypeStruct((S, 2 * D), BF16), jax.ShapeDtypeStruct((S, in_width), BF16), jax.ShapeDtypeStruct((1, 2 * D), F32)],
        compiler_params=_cparams("parallel", "arbitrary"),
        name=name,
    )(dmixed, gates, y_pool, y_attn)


def _pool_counts(i, tm, rows, w):
    t = i * tm + lax.broadcasted_iota(jnp.int32, (rows, 1), 0)
    return jnp.minimum(t + 1, w).astype(F32)


def _pooled_groups(u_ref, uh_ref, i, tm, C):
    cur = u_ref[...]
    halo = jnp.where(i == 0, 0.0, uh_ref[...])
    xx = jnp.concatenate([halo, cur], axis=0)
    out = []
    s = xx
    for gi, w in enumerate(POOL_WINDOWS):
        s = s + pltpu.roll(s, w // 2, 0)
        tot = s[HALO:, 0:C]
        out.append(tot / _pool_counts(i, tm, tm, w) - cur[:, gi * C:(gi + 1) * C])
        s = s[:, C:] if gi + 1 < len(POOL_WINDOWS) else s
    return out


def _pool_fwd(u, wl, scale, name):
    S, PW = u.shape
    C = PW // len(POOL_WINDOWS)
    tm = _tile(S, 512)
    hb = tm // HALO

    def body(u_ref, uh_ref, wl_ref, sc_ref, o_ref):
        i = pl.program_id(0)
        pooled = _pooled_groups(u_ref, uh_ref, i, tm, C)
        for gi in range(len(POOL_WINDOWS)):
            y = _dot_nn(pooled[gi].astype(BF16), wl_ref[gi])
            o_ref[:, gi * C:(gi + 1) * C] = (y * sc_ref[:, gi * C:(gi + 1) * C]).astype(o_ref.dtype)

    return pl.pallas_call(
        body,
        grid=(S // tm,),
        in_specs=[
            pl.BlockSpec((tm, PW), lambda i: (i, 0)),
            pl.BlockSpec((HALO, PW), lambda i: (jnp.maximum(i * hb - 1, 0), 0)),
            pl.BlockSpec((len(POOL_WINDOWS), C, C), lambda i: (0, 0, 0)),
            pl.BlockSpec((1, PW), lambda i: (0, 0)),
        ],
        out_specs=pl.BlockSpec((tm, PW), lambda i: (i, 0)),
        out_shape=jax.ShapeDtypeStruct((S, PW), BF16),
        compiler_params=_cparams("parallel"),
        name=name,
    )(u, u, wl, scale)


def _pool_bwd(u, dp, wl, scale, dproj, name):
    S, PW = u.shape
    G = len(POOL_WINDOWS)
    C = PW // G
    tm = _tile(S, 512)
    hb, ni = tm // HALO, S // tm
    n = tm + HALO

    def body(u_ref, uh_ref, dp_ref, dpn_ref, wl_ref, sc_ref, _, du_ref, dwl_ref, dsc_ref):
        i = pl.program_id(0)

        @pl.when(i == 0)
        def _():
            dwl_ref[...] = jnp.zeros_like(dwl_ref)
            dsc_ref[...] = jnp.zeros_like(dsc_ref)

        pooled = _pooled_groups(u_ref, uh_ref, i, tm, C)
        dpc = dp_ref[...].astype(F32)
        dpn = jnp.where(i == ni - 1, 0.0, dpn_ref[...].astype(F32))
        sc = sc_ref[...]
        dyl = jnp.concatenate([dpc, dpn], axis=0) * sc
        for gi, w in enumerate(POOL_WINDOWS):
            cols = slice(gi * C, (gi + 1) * C)
            pb = pooled[gi].astype(BF16)
            ylin = _dot_nn(pb, wl_ref[gi])
            dsc_ref[:, cols] += jnp.sum(dpc[:, cols] * ylin, axis=0, keepdims=True)
            dylg = dyl[:, cols].astype(BF16)
            dwl_ref[gi] += _dot(pb, dylg[:tm], ((0,), (0,)))
            dpool = _dot_nt(dylg, wl_ref[gi])
            e = dpool / _pool_counts(i, tm, n, w)
            k = 1
            while k < w:
                e = e + pltpu.roll(e, n - k, 0)
                k *= 2
            du_ref[:, cols] = (e[:tm] - dpool[:tm]).astype(du_ref.dtype)

    return pl.pallas_call(
        body,
        grid=(ni,),
        in_specs=[
            pl.BlockSpec((tm, PW), lambda i: (i, 0)),
            pl.BlockSpec((HALO, PW), lambda i: (jnp.maximum(i * hb - 1, 0), 0)),
            pl.BlockSpec((tm, PW), lambda i: (i, 0)),
            pl.BlockSpec((HALO, PW), lambda i: (jnp.minimum((i + 1) * hb, S // HALO - 1), 0)),
            pl.BlockSpec((G, C, C), lambda i: (0, 0, 0)),
            pl.BlockSpec((1, PW), lambda i: (0, 0)),
            ANY,
        ],
        out_specs=[pl.BlockSpec((tm, PW), lambda i: (i, 0)), pl.BlockSpec((G, C, C), lambda i: (0, 0, 0)), pl.BlockSpec((1, PW), lambda i: (0, 0))],
        out_shape=[jax.ShapeDtypeStruct(dproj.shape, dproj.dtype), jax.ShapeDtypeStruct((G, C, C), F32), jax.ShapeDtypeStruct((1, PW), F32)],
        input_output_aliases={6: 0},
        compiler_params=_cparams("arbitrary"),
        name=name,
    )(u, u, dp, dp, wl, scale, dproj)


def _band_masks():
    ii = lax.broadcasted_iota(jnp.int32, (SPAN, SPAN), 0)
    kk = lax.broadcasted_iota(jnp.int32, (SPAN, SPAN), 1)
    return ((ii + SPAN - kk).astype(F32), kk >= ii), ((ii - kk).astype(F32), kk <= ii)


ATTN_TILE = 16 * SPAN


def _unit_rows(r, b, d, blocks=1):
    return pl.ds(d * SPAN * b + r, blocks * SPAN, stride=d) if d > 1 else pl.ds(SPAN * b, blocks * SPAN)


def _f32_copies(refs, scratch, d):
    if d == 1:
        return list(refs)
    for ref, s in zip(refs, scratch):
        s[...] = ref[...].astype(F32)
    return list(scratch)


def _attn_fwd(qkv, d, g, name):
    S = qkv.shape[0]
    T = min(ATTN_TILE, S)
    P = SPAN * d
    nbk = T // P

    def body(q_ref, k_ref, v_ref, kp_ref, vp_ref, o_ref, lse_ref, *scratch):
        c = pl.program_id(0)
        (jp, mp), (jc, mc) = _band_masks()
        slopes = [ALIBI_SLOPES[g * HEADS_PER_GROUP + h] * d for h in range(HEADS_PER_GROUP)]
        slope = slopes[0]
        for h in range(1, HEADS_PER_GROUP):
            slope = jnp.where(pl.program_id(1) == h, slopes[h], slope)
        q_s, k_s, v_s, kp_s, vp_s = _f32_copies((q_ref, k_ref, v_ref, kp_ref, vp_ref), scratch[:5], d)
        o_s, l_s = (o_ref, lse_ref) if d == 1 else scratch[5:7]
        bias_p, bias_c = jnp.where(mp, -slope * jp, NEG_BIG), jnp.where(mc, -slope * jc, NEG_BIG)
        bias = jnp.concatenate([bias_p, bias_c], axis=1)
        bias_first = jnp.concatenate([jnp.where(c > 0, bias_p, NEG_BIG), bias_c], axis=1)
        for r in range(d):
            for b in range(nbk):
                rows = _unit_rows(r, b, d)
                q = q_s[rows, :].astype(BF16)
                if b == 0:
                    prev = _unit_rows(r, 0, d)
                    kk = jnp.concatenate([kp_s[prev, :], k_s[rows, :]], axis=0).astype(BF16)
                    vv = jnp.concatenate([vp_s[prev, :], v_s[rows, :]], axis=0).astype(BF16)
                else:
                    both = _unit_rows(r, b - 1, d, 2)
                    kk, vv = k_s[both, :].astype(BF16), v_s[both, :].astype(BF16)
                s = _dot_nt(q, kk) * ATTN_SCALE + (bias_first if b == 0 else bias)
                m = jnp.max(s, axis=-1, keepdims=True)
                p = jnp.exp(s - m)
                l = jnp.sum(p, axis=-1, keepdims=True)
                o_s[rows, :] = _dot_nn(p.astype(BF16), vv) / l
                l_s[rows, :] = jnp.broadcast_to(m + jnp.log(l), (SPAN, HEAD_DIM))
        if d > 1:
            o_ref[...] = o_s[...]
            lse_ref[...] = l_s[...]

    col = lambda kind: (lambda c, h: (c, kind * N_ATTN_HEADS + g * HEADS_PER_GROUP + h))
    pcol = lambda kind: (lambda c, h: (jnp.maximum(c * nbk - 1, 0), kind * N_ATTN_HEADS + g * HEADS_PER_GROUP + h))
    cur = lambda kind: pl.BlockSpec((T, HEAD_DIM), col(kind))
    prv = lambda kind: pl.BlockSpec((P, HEAD_DIM), pcol(kind))
    out = pl.BlockSpec((T, HEAD_DIM), lambda c, h: (c, h))
    scratch = [] if d == 1 else [pltpu.VMEM((T, HEAD_DIM), F32)] * 3 + [pltpu.VMEM((P, HEAD_DIM), F32)] * 2 + [pltpu.VMEM((T, HEAD_DIM), F32)] * 2
    return pl.pallas_call(
        body,
        grid=(S // T, HEADS_PER_GROUP),
        in_specs=[cur(0), cur(1), cur(2), prv(1), prv(2)],
        out_specs=[out, out],
        out_shape=[jax.ShapeDtypeStruct((S, GROUP_WIDTH), F32)] * 2,
        scratch_shapes=scratch,
        compiler_params=_cparams("parallel", "parallel"),
        name=name,
    )(qkv, qkv, qkv, qkv, qkv)


def _attn_merge(os_, lses, name):
    S, W = os_[0].shape
    tm = _tile(S, 512)

    def body(o0, o1, o2, l0, l1, l2, y_ref, lse_ref):
        ls = [l0[...], l1[...], l2[...]]
        m = jnp.maximum(jnp.maximum(ls[0], ls[1]), ls[2])
        es = [jnp.exp(v - m) for v in ls]
        tot = es[0] + es[1] + es[2]
        y = (es[0] * o0[...] + es[1] * o1[...] + es[2] * o2[...]) / tot
        y_ref[...] = y.astype(y_ref.dtype)
        lse_ref[...] = m + jnp.log(tot)

    row = pl.BlockSpec((tm, W), lambda i: (i, 0))
    return pl.pallas_call(
        body,
        grid=(S // tm,),
        in_specs=[row] * 6,
        out_specs=[row, row],
        out_shape=[jax.ShapeDtypeStruct((S, W), BF16), jax.ShapeDtypeStruct((S, W), F32)],
        compiler_params=_cparams("parallel"),
        name=name,
    )(*os_, *lses)


def _attn_bwd(qkv, dattn, y, lse, dproj, d, g, col0, name):
    S = qkv.shape[0]
    T = min(ATTN_TILE, S)
    P = SPAN * d
    nbk = T // P
    ntile = S // T

    def body(q_ref, k_ref, v_ref, kp_ref, vp_ref, qn_ref, da_ref, dan_ref, y_ref, yn_ref, lse_ref, lsen_ref, _, out_ref, dq_s, dk_s, dv_s, *scratch):
        c = pl.program_id(0)
        head_id = pl.program_id(1)
        kind = pl.program_id(2)

        @pl.when(kind == 0)
        def _():
            (jp, mp), (jc, mc) = _band_masks()
            slopes = [ALIBI_SLOPES[g * HEADS_PER_GROUP + h] * d for h in range(HEADS_PER_GROUP)]
            slope = slopes[0]
            for h in range(1, HEADS_PER_GROUP):
                slope = jnp.where(head_id == h, slopes[h], slope)
            q_s, k_s, v_s, da_s, y_s, kp_s, vp_s, qn_s, dan_s, yn_s = _f32_copies(
                (q_ref, k_ref, v_ref, da_ref, y_ref, kp_ref, vp_ref, qn_ref, dan_ref, yn_ref), scratch, d)
            bias_p, bias_c = jnp.where(mp, -slope * jp, NEG_BIG), jnp.where(mc, -slope * jc, NEG_BIG)
            bias = jnp.concatenate([bias_c, bias_p], axis=0)
            bias_last = jnp.concatenate([bias_c, jnp.where(c < ntile - 1, bias_p, NEG_BIG)], axis=0)
            bias_first = jnp.where(c > 0, bias_p, NEG_BIG)

            def pair(q, da, yy, lse_blk, kk, vv, b):
                dd = jnp.sum(da.astype(F32) * yy.astype(F32), axis=-1, keepdims=True)
                p = jnp.exp(_dot_nt(q, kk) * ATTN_SCALE + b - lse_blk[:, 0:1])
                return p, p * (_dot_nt(da, vv) - dd)

            for r in range(d):
                first = _unit_rows(r, 0, d)
                kk, vv = kp_s[first, :].astype(BF16), vp_s[first, :].astype(BF16)
                _, ds = pair(q_s[first, :].astype(BF16), da_s[first, :].astype(BF16), y_s[first, :], lse_ref[first, :], kk, vv, bias_first)
                dq_next = _dot_nn(ds.astype(BF16), kk)
                for kb in range(nbk):
                    rows = _unit_rows(r, kb, d)
                    if kb + 1 < nbk:
                        both = _unit_rows(r, kb, d, 2)
                        q, da, yy, lse_blk = q_s[both, :], da_s[both, :], y_s[both, :], lse_ref[both, :]
                    else:
                        q = jnp.concatenate([q_s[rows, :], qn_s[first, :]], axis=0)
                        da = jnp.concatenate([da_s[rows, :], dan_s[first, :]], axis=0)
                        yy = jnp.concatenate([y_s[rows, :], yn_s[first, :]], axis=0)
                        lse_blk = jnp.concatenate([lse_ref[rows, :], lsen_ref[first, :]], axis=0)
                    q, da = q.astype(BF16), da.astype(BF16)
                    kk, vv = k_s[rows, :].astype(BF16), v_s[rows, :].astype(BF16)
                    p, ds = pair(q, da, yy, lse_blk, kk, vv, bias if kb + 1 < nbk else bias_last)
                    dv_s[rows, :] = _dot_nn(p.T.astype(BF16), da)
                    dk_s[rows, :] = _dot_nn(ds.T.astype(BF16), q) * ATTN_SCALE
                    dq_both = _dot_nn(ds.astype(BF16), kk)
                    dq_s[rows, :] = (dq_next + dq_both[:SPAN]) * ATTN_SCALE
                    dq_next = dq_both[SPAN:]
            out_ref[...] = dq_s[...].astype(out_ref.dtype)

        @pl.when(kind == 1)
        def _():
            out_ref[...] = dk_s[...].astype(out_ref.dtype)

        @pl.when(kind == 2)
        def _():
            out_ref[...] = dv_s[...].astype(out_ref.dtype)

    head = lambda h: g * HEADS_PER_GROUP + h
    cur = lambda kind: pl.BlockSpec((T, HEAD_DIM), lambda c, h, kd: (c, kind * N_ATTN_HEADS + head(h)))
    prv = lambda kind: pl.BlockSpec((P, HEAD_DIM), lambda c, h, kd: (jnp.maximum(c * nbk - 1, 0), kind * N_ATTN_HEADS + head(h)))
    nxt_row = lambda c: jnp.minimum((c + 1) * nbk, S // P - 1)
    qnext = pl.BlockSpec((P, HEAD_DIM), lambda c, h, kd: (nxt_row(c), head(h)))
    hcur = pl.BlockSpec((T, HEAD_DIM), lambda c, h, kd: (c, h))
    hnext = pl.BlockSpec((P, HEAD_DIM), lambda c, h, kd: (nxt_row(c), h))
    out = pl.BlockSpec((T, HEAD_DIM), lambda c, h, kd: (c, col0 + kd * N_ATTN_HEADS + head(h)))
    stage = [pltpu.VMEM((T, HEAD_DIM), F32)] * 3
    copies = [] if d == 1 else [pltpu.VMEM((T, HEAD_DIM), F32)] * 5 + [pltpu.VMEM((P, HEAD_DIM), F32)] * 5
    return pl.pallas_call(
        body,
        grid=(ntile, HEADS_PER_GROUP, 3),
        in_specs=[cur(0), cur(1), cur(2), prv(1), prv(2), qnext, hcur, hnext, hcur, hnext, hcur, hnext, ANY],
        out_specs=out,
        out_shape=jax.ShapeDtypeStruct(dproj.shape, dproj.dtype),
        input_output_aliases={12: 0},
        scratch_shapes=stage + copies,
        compiler_params=_cparams("parallel", "parallel", "arbitrary"),
        name=name,
    )(qkv, qkv, qkv, qkv, qkv, qkv, dattn, dattn, y, y, lse, lse, dproj)


def _row_block(R, C, bytes_per_row_elem=4, budget=1 << 20):
    if R % 8:
        return R
    best = 8
    t = 8
    while t <= R:
        if R % t == 0 and t * C * bytes_per_row_elem <= budget:
            best = t
        t += 8
    return best


def _adamw(w, g, m, v, name):
    R, C = w.shape
    tr = _row_block(R, C)
    c1 = 1.0 - ADAM_B1 ** ADAM_STEP
    c2 = 1.0 - ADAM_B2 ** ADAM_STEP

    def body(w_ref, g_ref, m_ref, v_ref, d_ref, nm_ref, nv_ref):
        gv = g_ref[...]
        nm = ADAM_B1 * m_ref[...] + (1.0 - ADAM_B1) * gv
        nv = ADAM_B2 * v_ref[...] + (1.0 - ADAM_B2) * (gv * gv)
        d_ref[...] = -ADAM_LR * ((nm / c1) / (jnp.sqrt(nv / c2) + ADAM_EPS) + ADAM_WD * w_ref[...])
        nm_ref[...] = nm
        nv_ref[...] = nv

    blk = pl.BlockSpec((tr, C), lambda i: (i, 0))
    return pl.pallas_call(
        body,
        grid=(R // tr,),
        in_specs=[blk] * 4,
        out_specs=[blk] * 3,
        out_shape=[jax.ShapeDtypeStruct((R, C), F32)] * 3,
        compiler_params=_cparams("parallel"),
        name=name,
    )(w, g, m, v)


def _sum_pieces(grad, axis, recv, pos, name):
    n, pr, pc = recv.shape
    tr = _row_block(pr, pc, bytes_per_row_elem=(n + 1) * recv.dtype.itemsize, budget=4 << 20)
    nblk = pr // tr
    if axis == 1:
        own_map = lambda i, p: (p[1] * nblk + i, p[0])
    else:
        own_map = lambda i, p: ((2 * p[0] + p[1]) * nblk + i, 0)

    def body(p_ref, own_ref, r_ref, o_ref):
        acc = own_ref[...].astype(F32)
        for s in range(n):
            acc = acc + r_ref[s].astype(F32)
        o_ref[...] = acc

    return pl.pallas_call(
        body,
        grid_spec=pltpu.PrefetchScalarGridSpec(
            num_scalar_prefetch=1,
            grid=(nblk,),
            in_specs=[pl.BlockSpec((tr, pc), own_map), pl.BlockSpec((n, tr, pc), lambda i, p: (0, i, 0))],
            out_specs=pl.BlockSpec((tr, pc), lambda i, p: (p[1] * nblk + i, 0)),
        ),
        out_shape=jax.ShapeDtypeStruct((2 * pr, pc), F32),
        compiler_params=_cparams("parallel"),
        name=name,
    )(pos, grad, recv)


def _sum_small(own, recv, me, name):
    n, R, C = recv.shape
    tr = _row_block(R, C, bytes_per_row_elem=(n + 1) * 4, budget=4 << 20)

    def body(me_ref, own_ref, r_ref, o_ref):
        acc = None
        for dev in range(n + 1):
            k = jnp.bitwise_xor(me_ref[0], dev)
            term = jnp.where(k == 0, own_ref[...], r_ref[jnp.maximum(k - 1, 0)])
            acc = term if acc is None else acc + term
        o_ref[...] = acc

    return pl.pallas_call(
        body,
        grid_spec=pltpu.PrefetchScalarGridSpec(
            num_scalar_prefetch=1,
            grid=(R // tr,),
            in_specs=[pl.BlockSpec((tr, C), lambda i, m: (i, 0)), pl.BlockSpec((n, tr, C), lambda i, m: (0, i, 0))],
            out_specs=pl.BlockSpec((tr, C), lambda i, m: (i, 0)),
        ),
        out_shape=jax.ShapeDtypeStruct((R, C), F32),
        compiler_params=_cparams("parallel"),
        name=name,
    )(me, own, recv)


def _place(shard, axis, pos, dtype, name, after=None):
    extra = [] if after is None else [after]
    shp = list(shard.shape)
    shp[axis] *= N_CHIPS
    if shard.ndim == 3:
        assert axis == 1
        in_spec = pl.BlockSpec(shard.shape, lambda i, p: (0, 0, 0))
        out_spec = pl.BlockSpec(shard.shape, lambda i, p: (0, p[0], 0))
        grid = (1,)
    else:
        R, C = shard.shape
        tr = _row_block(R, C, bytes_per_row_elem=4, budget=2 << 20)
        nblk = R // tr
        in_spec = pl.BlockSpec((tr, C), lambda i, p: (i, 0))
        out_spec = pl.BlockSpec((tr, C), (lambda i, p: (i, p[0])) if axis == 1 else (lambda i, p: (p[0] * nblk + i, 0)))
        grid = (nblk,)

    def body(*refs):
        s_ref, o_ref = refs[1], refs[-1]
        o_ref[...] = s_ref[...].astype(o_ref.dtype)

    return pl.pallas_call(
        body,
        grid_spec=pltpu.PrefetchScalarGridSpec(num_scalar_prefetch=1, grid=grid, in_specs=[in_spec] + [ANY] * len(extra), out_specs=out_spec),
        out_shape=jax.ShapeDtypeStruct(tuple(shp), dtype),
        compiler_params=_cparams("parallel"),
        name=name,
    )(pos, shard, *extra)


HBM = pl.BlockSpec(memory_space=pltpu.HBM)
SEM = pl.BlockSpec(memory_space=pltpu.SEMAPHORE)
DATAFLOW = pltpu.SideEffectType.DATAFLOW_SIDE_EFFECTING


def _position():
    return lax.axis_index("x"), lax.axis_index("y"), lax.axis_index("c")


def _peer(k):
    x, y, c = _position()
    return ((1 - x) if k & 4 else x, (1 - y) if k & 2 else y, (1 - c) if k & 1 else c)


def _shard_slice(ref, axis, idx, size):
    start = idx * size
    if axis == ref.ndim - 1:
        start = pl.multiple_of(start, 128)
    ix = [slice(None)] * ref.ndim
    ix[axis] = pl.ds(start, size)
    return ref.at[tuple(ix)]


def _gather_plan(axes):
    def plan(refs):
        x, y, c = _position()
        out = []
        for ref, ax in zip(refs, axes):
            mine = _shard_slice(ref, ax, 2 * x + y, ref.shape[ax] // N_CHIPS)
            for k in (4, 2, 6):
                px, py, _ = _peer(k)
                out.append((mine, mine, (px, py, c)))
        return out
    return plan


def _scatter_plan(axes):
    m = len(axes)

    def plan(refs):
        out = []
        for t in range(m):
            grad, recv = refs[t], refs[m + t]
            _, pr, pc = recv.shape
            for k in range(1, N_DEV):
                px, py, pcore = _peer(k)
                if axes[t] == 0:
                    piece = grad.at[pl.ds(((2 * px + py) * 2 + pcore) * pr, pr), :]
                else:
                    piece = grad.at[pl.ds(pcore * pr, pr), pl.ds(pl.multiple_of((2 * px + py) * pc, 128), pc)]
                out.append((piece, recv.at[k - 1], (px, py, pcore)))
        return out
    return plan


def _broadcast_plan(refs):
    small, recv = refs
    return [(small, recv.at[k - 1], _peer(k)) for k in range(1, N_DEV)]


def _start_all(plan, refs, send_sems, recv_sems):
    for q, (src, dst, dev) in enumerate(plan(refs)):
        pltpu.make_async_remote_copy(src_ref=src, dst_ref=dst, send_sem=send_sems.at[q], recv_sem=recv_sems.at[q], device_id=dev, device_id_type=MESH).start()


def _wait_all(plan, refs, send_sems, recv_sems):
    for q, (src, dst, dev) in enumerate(plan(refs)):
        cp = pltpu.make_async_remote_copy(src_ref=src, dst_ref=dst, send_sem=send_sems.at[q], recv_sem=recv_sems.at[q], device_id=dev, device_id_type=MESH)
        cp.wait_send()
        cp.wait_recv()


def _push(bufs, plan, ncopies, name):
    n = len(bufs)

    def body(*refs):
        outs = refs[n:2 * n]
        send_sems, recv_sems = refs[2 * n:]
        _start_all(plan, outs, send_sems, recv_sems)
        _wait_all(plan, outs, send_sems, recv_sems)

    return pl.pallas_call(
        body,
        in_specs=[ANY] * n,
        out_specs=[ANY] * n,
        out_shape=[jax.ShapeDtypeStruct(b.shape, b.dtype) for b in bufs],
        input_output_aliases={t: t for t in range(n)},
        scratch_shapes=[pltpu.SemaphoreType.DMA((ncopies,)), pltpu.SemaphoreType.DMA((ncopies,))],
        name=name,
    )(*bufs)


def _half_slices_plan(onward):
    def plan(refs):
        ref, = refs
        x, y, c = _position()
        R2, C4 = ref.shape[0] // 2, ref.shape[1] // N_CHIPS
        out = []
        for k in (4, 2, 6):
            px, py, _ = _peer(k)
            chip = (2 * px + py) if onward else (2 * x + y)
            half = ref.at[pl.ds(c * R2, R2), pl.ds(pl.multiple_of(chip * C4, 128), C4)]
            out.append((half, half, (x, y, 1 - c) if onward else (px, py, c)))
        return out
    return plan


def _push_start(bufs, plan, ncopies, name, after=None):
    n = len(bufs)
    extra = [] if after is None else [after]

    def body(*refs):
        ins = refs[:n]
        first_out = n + len(extra)
        send_sems, recv_sems, token = refs[first_out], refs[first_out + 1], refs[-1]
        _start_all(plan, ins, send_sems, recv_sems)
        token[...] = jnp.zeros_like(token)

    res = pl.pallas_call(
        body,
        name=name,
        out_shape=(pltpu.SemaphoreType.DMA((ncopies,)), pltpu.SemaphoreType.DMA((ncopies,)), *[pltpu.HBM(b.shape, b.dtype) for b in bufs],
                   jax.ShapeDtypeStruct((8, 128), F32)),
        in_specs=[HBM] * n + [ANY] * len(extra),
        out_specs=(SEM, SEM, *[HBM] * n, pl.BlockSpec(memory_space=pltpu.VMEM)),
        input_output_aliases={t: t + 2 for t in range(n)},
        compiler_params=pltpu.CompilerParams(has_side_effects=DATAFLOW),
    )(*[pltpu.with_memory_space_constraint(b, pltpu.HBM) for b in bufs], *extra)
    return res[0], res[1], list(res[2:2 + n]), res[-1]


def _push_wait(send_sems, recv_sems, bufs, plan, after, name):
    n = len(bufs)
    after = list(after) if isinstance(after, (list, tuple)) else [after]

    def body(*refs):
        ins = refs[:n]
        _wait_all(plan, ins, refs[n], refs[n + 1])

    return pl.pallas_call(
        body,
        name=name,
        out_shape=tuple(pltpu.HBM(b.shape, b.dtype) for b in bufs),
        in_specs=[HBM] * n + [SEM, SEM] + [ANY] * len(after),
        out_specs=tuple([HBM] * n),
        input_output_aliases={t: t for t in range(n)},
        compiler_params=pltpu.CompilerParams(has_side_effects=DATAFLOW),
    )(*bufs, send_sems, recv_sems, *after)


EXCHANGE_CHUNKS = 2


def _exchange_plan(refs):
    x, y, c = _position()
    out = []
    for ref in refs:
        rows = ref.shape[0] // (2 * EXCHANGE_CHUNKS)
        for q in range(EXCHANGE_CHUNKS):
            mine = ref.at[pl.ds((c * EXCHANGE_CHUNKS + q) * rows, rows), :]
            out.append((mine, mine, (x, y, 1 - c)))
    return out


LATE_WEIGHTS = (("w_pool_lin", "w_pool_out", "w_attn_out", "w_out"), ("w_up", "conv_w", "w_down"))


def _local_step(x, tgt, w, late_weights, send):
    S, D = x.shape
    PW = w["pool_scale"].shape[1]
    o_q = PW
    o_g = PW + 3 * ATTN_WIDTH
    QKV = 3 * ATTN_WIDTH

    h1 = _rms_fwd(x, w["g_mix"], "rms1")
    w = dict(w, **late_weights("w_in", h1))
    proj_tiles = (_tile(S, 1024), 512, D)
    u = _mm(h1, w["w_in"], mode="nn", dims=(S, PW, D), tiles=proj_tiles, out_dtypes=(F32,), name="proj_u")
    qkv = _mm(h1, w["w_in"], mode="nn", dims=(S, QKV, D), tiles=proj_tiles, b_off=(0, o_q), name="proj_qkv")

    def gate_epilogue(acc, ex, outs):
        outs[0][...] = (1.0 / (1.0 + jnp.exp(-(acc + ex[0][...])))).astype(outs[0].dtype)

    gates = _mm(h1, w["w_in"], mode="nn", dims=(S, 2 * D, D), tiles=proj_tiles, b_off=(0, o_g), epilogue=gate_epilogue,
                extras=[(w["b_gate"], "n", (0, 0))], name="proj_gates")

    os_, lses = [], []
    for gi, (_, d) in enumerate(ATTN_GROUPS):
        o, lse = _attn_fwd(qkv, d, gi, f"attn_fwd{gi}")
        os_.append(o)
        lses.append(lse)
    attn, lse_tot = _attn_merge(os_, lses, "attn_merge")

    w = dict(w, **late_weights(0, attn))
    pool_out = _pool_fwd(u, w["w_pool_lin"], w["pool_scale"], "pool_fwd")
    y_pool = _mm(pool_out, w["w_pool_out"], mode="nn", dims=(S, D, PW), name="y_pool")

    def mix_epilogue(acc, ex, outs):
        outs[0][...] = acc.astype(BF16)
        outs[1][...] = (ex[0][...].astype(F32) * ex[2][...].astype(F32) + ex[1][...].astype(F32) * acc).astype(BF16)

    y_attn, mixed = _mm(attn, w["w_attn_out"], mode="nn", dims=(S, D, GROUP_WIDTH), out_dtypes=(BF16, BF16), epilogue=mix_epilogue,
                        extras=[(gates, "mn", (0, 0)), (gates, "mn", (0, D)), (y_pool, "mn", (0, 0))], name="y_attn_mix")

    def residual_epilogue(acc, ex, outs):
        outs[0][...] = ex[0][...] + acc

    x2 = _mm(mixed, w["w_out"], mode="nn", dims=(S, D, D), out_dtypes=(F32,), epilogue=residual_epilogue, extras=[(x, "mn", (0, 0))], name="out_proj")

    h2 = _rms_fwd(x2, w["g_ffn"], "rms2")
    w = dict(w, **late_weights(1, h2))
    F = w["w_down"].shape[0]
    up = _mm(h2, w["w_up"], mode="nn", dims=(S, 2 * F, D), name="up_proj")
    f, act_a, act_b = _convglu_fwd(up, w["conv_w"], w["conv_b"], "convglu_fwd")
    x3 = _mm(f, w["w_down"], mode="nn", dims=(S, D, F), out_dtypes=(F32,), epilogue=residual_epilogue, extras=[(x2, "mn", (0, 0))], name="down_proj")

    g = {}
    dx3b, g["g_final"], loss_cols = _loss_head(x3, tgt, w["g_final"], "loss_head")

    g["w_down"] = _mm(f, dx3b, mode="tn", dims=(F, D, S), name="dw_down")
    sent = send(("w_down",), g)
    df = _mm(dx3b, w["w_down"], mode="nt", dims=(S, F, D), name="d_f")
    dup, g["conv_b"], g["conv_w"] = _convglu_bwd(df, act_a, act_b, up, w["conv_w"] + sent, "convglu_bwd")
    g["w_up"] = _mm(h2, dup, mode="tn", dims=(D, 2 * F, S), name="dw_up")
    sent = send(("w_up",), g)
    dh2 = _mm(dup, w["w_up"], mode="nt", dims=(S, D, 2 * F), name="d_h2")
    dx2b, g["g_ffn"] = _rms_bwd(dh2, x2, w["g_ffn"] + sent, dx3b, "rms2_bwd", BF16)

    g["w_out"] = _mm(mixed, dx2b, mode="tn", dims=(D, D, S), name="dw_out")
    dmixed = _mm(dx2b, w["w_out"], mode="nt", dims=(S, D, D), name="d_mixed")
    IN = w["w_in"].shape[1]
    dy_both, dproj, g["b_gate"] = _gate_bwd(dmixed, gates, y_pool, y_attn, IN, "gate_bwd")

    g["w_pool_out"] = _mm(pool_out, dy_both, mode="tn", dims=(PW, D, S), name="dw_pool_out")
    g["w_attn_out"] = _mm(attn, dy_both, mode="tn", dims=(GROUP_WIDTH, D, S), b_off=(0, D), name="dw_attn_out")
    sent = send(("w_out", "w_pool_out", "w_attn_out"), g)
    dpool = _mm(dy_both, w["w_pool_out"], mode="nt", dims=(S, PW, D), name="d_pool")
    dattn = _mm(dy_both, w["w_attn_out"], mode="nt", dims=(S, GROUP_WIDTH, D), a_off=(0, D), name="d_attn")

    dproj, g["w_pool_lin"], g["pool_scale"] = _pool_bwd(u, dpool, w["w_pool_lin"], w["pool_scale"] + sent, dproj, "pool_bwd")
    g["loss_cols"] = loss_cols
    sent = send("small", g)

    for gi, (_, d) in enumerate(ATTN_GROUPS):
        dproj = _attn_bwd(qkv, dattn, attn, lse_tot, dproj, d, gi, PW // HEAD_DIM, f"attn_bwd{gi}")

    g["w_in"] = _mm(h1, dproj, mode="tn", dims=(D, IN, S), name="dw_in")
    sent = sent + send(("w_in",), g)
    dh1 = _mm(dproj, w["w_in"], mode="nt", dims=(S, D, IN), tiles=(_tile(S, 1024), _tile(D, 2048), _tile(IN, 2432)), name="d_h1")
    (grad_x, g["g_mix"]) = _rms_bwd(dh1, x, w["g_mix"] + sent, dx2b, "rms1_bwd", F32)
    return loss_cols, grad_x, g


BIG = ("w_in", "w_pool_out", "w_attn_out", "w_out", "w_up", "w_down")
BIG_AXIS = {"w_in": 1, "w_pool_out": 1, "w_attn_out": 1, "w_out": 0, "w_up": 1, "w_down": 0}
GATHER_AXIS = dict(BIG_AXIS, w_pool_lin=1, conv_w=1)
SMALL = ("loss_cols", "b_gate", "w_pool_lin", "pool_scale", "g_ffn", "conv_w", "conv_b", "g_final")
SMALL_COLS = 1024
ORDER = ("g_mix", "w_in", "b_gate", "w_pool_lin", "pool_scale", "w_pool_out", "w_attn_out", "w_out", "g_ffn", "w_up", "conv_w", "conv_b", "w_down", "g_final")


def _as_rows(parts):
    flat = jnp.concatenate([p.astype(F32).reshape(-1) for p in parts])
    rows = -(-flat.shape[0] // (8 * SMALL_COLS)) * 8
    return jnp.pad(flat, (0, rows * SMALL_COLS - flat.shape[0])).reshape(rows, SMALL_COLS)


def kernel(x, g_mix, w_in, b_gate, w_pool_lin, pool_scale, w_pool_out, w_attn_out, w_out, g_ffn, w_up, conv_w, conv_b, w_down, g_final, loss_target, m_g_mix, m_w_in, m_b_gate, m_w_pool_lin, m_pool_scale, m_w_pool_out, m_w_attn_out, m_w_out, m_g_ffn, m_w_up, m_conv_w, m_conv_b, m_w_down, m_g_final, v_g_mix, v_w_in, v_b_gate, v_w_pool_lin, v_pool_scale, v_w_pool_out, v_w_attn_out, v_w_out, v_g_ffn, v_w_up, v_conv_w, v_conv_b, v_w_down, v_g_final):
    shard = dict(g_mix=g_mix, w_in=w_in, b_gate=b_gate, w_pool_lin=w_pool_lin, pool_scale=pool_scale, w_pool_out=w_pool_out, w_attn_out=w_attn_out,
                 w_out=w_out, g_ffn=g_ffn, w_up=w_up, conv_w=conv_w, conv_b=conv_b, w_down=w_down, g_final=g_final)
    mom = dict(g_mix=m_g_mix, w_in=m_w_in, b_gate=m_b_gate, w_pool_lin=m_w_pool_lin, pool_scale=m_pool_scale, w_pool_out=m_w_pool_out, w_attn_out=m_w_attn_out,
               w_out=m_w_out, g_ffn=m_g_ffn, w_up=m_w_up, conv_w=m_conv_w, conv_b=m_conv_b, w_down=m_w_down, g_final=m_g_final)
    vel = dict(g_mix=v_g_mix, w_in=v_w_in, b_gate=v_b_gate, w_pool_lin=v_w_pool_lin, pool_scale=v_pool_scale, w_pool_out=v_w_pool_out, w_attn_out=v_w_attn_out,
               w_out=v_w_out, g_ffn=v_g_ffn, w_up=v_w_up, conv_w=v_conv_w, conv_b=v_conv_b, w_down=v_w_down, g_final=v_g_final)
    chip = 2 * lax.axis_index("x") + lax.axis_index("y")
    pos = jnp.stack([chip, lax.axis_index("c")]).astype(jnp.int32)
    me = (2 * chip + lax.axis_index("c")).astype(jnp.int32).reshape(1)
    D = x.shape[2]

    out_plan, on_plan = _half_slices_plan(False), _half_slices_plan(True)
    in_send, in_recv, in_bufs, in_token = _push_start([_place(shard["w_in"][0], GATHER_AXIS["w_in"], pos, BF16, "place_w_in")], out_plan, 3,
                                                      "comm_gather_w_in_start")
    placed = {k: _place(shard[k][0], GATHER_AXIS[k], pos, F32 if k == "conv_w" else BF16, f"place_{k}", after=in_token)
              for names in LATE_WEIGHTS for k in names}
    late = []

    def late_weights(stage, after):
        if stage == "w_in":
            landed = _push_wait(in_send, in_recv, in_bufs, out_plan, [after] + list(placed.values()), "comm_gather_w_in_wait")
            w_in_full, = _push(list(landed), on_plan, 3, "comm_gather_w_in_pass")
            late_token, prior = 0.0, w_in_full
            for st, names in enumerate(LATE_WEIGHTS):
                plan = _gather_plan([GATHER_AXIS[k] for k in names])
                send_sems, recv_sems, bufs, token = _push_start([placed[k] for k in names], plan, 3 * len(names), f"comm_gather_late{st}_start", after=prior)
                late.append((names, send_sems, recv_sems, bufs, plan))
                late_token, prior = late_token + token[0, 0], token
            return dict(w_in=w_in_full, b_gate=shard["b_gate"] + late_token)
        names, send_sems, recv_sems, bufs, plan = late[stage]
        return dict(zip(names, _push_wait(send_sems, recv_sems, bufs, plan, after, f"comm_gather_late{stage}_wait")))

    pending = []

    def send(names, g):
        if names == "small":
            bufs = [_as_rows([g[k] for k in SMALL])]
            bufs.append(lax.empty((N_DEV - 1,) + bufs[0].shape, F32))
            plan, tag = _broadcast_plan, "small"
        else:
            bufs = [g[k] for k in names]
            for k in names:
                R, C = g[k].shape
                piece = (R // (2 * N_CHIPS), C) if BIG_AXIS[k] == 0 else (R // 2, C // N_CHIPS)
                bufs.append(lax.empty((N_DEV - 1,) + piece, BF16))
            plan, tag = _scatter_plan([BIG_AXIS[k] for k in names]), names[0]
        ncopies = (N_DEV - 1) * (len(bufs) // 2)
        send_sems, recv_sems, thru, token = _push_start(bufs, plan, ncopies, f"comm_scatter_start_{tag}")
        pending.append((names, send_sems, recv_sems, thru, plan, tag))
        return token[0, 0]

    w0 = dict(g_mix=shard["g_mix"] + in_token[0, 0], pool_scale=shard["pool_scale"], g_ffn=shard["g_ffn"],
              conv_b=shard["conv_b"], g_final=shard["g_final"].reshape(1, D))
    _, grad_x, gr = _local_step(x[0], loss_target[0], w0, late_weights, send)

    halves, small_parts = {}, None
    for names, send_sems, recv_sems, thru, plan, tag in pending:
        done = _push_wait(send_sems, recv_sems, thru, plan, grad_x, f"comm_scatter_wait_{tag}")
        if names == "small":
            small_parts = _sum_small(done[0], done[1], me, "sum_small").reshape(-1)
        else:
            m = len(names)
            for t, k in enumerate(names):
                halves[k] = _sum_pieces(done[t], BIG_AXIS[k], done[m + t], pos, f"sum_{k}")
    g_mix_own = _as_rows([gr["g_mix"]])
    _, g_mix_recv = _push([g_mix_own, lax.empty((N_DEV - 1,) + g_mix_own.shape, F32)], _broadcast_plan, N_DEV - 1, "comm_gather_g_mix")
    g_mix_sum = _sum_small(g_mix_own, g_mix_recv, me, "sum_g_mix").reshape(-1)[:D]
    wholes = _push([halves[k] for k in BIG], _exchange_plan, EXCHANGE_CHUNKS * len(BIG), "comm_exchange_halves")

    grads = {"g_mix": g_mix_sum.reshape(shard["g_mix"].shape)}
    for k, whole in zip(BIG, wholes):
        grads[k] = whole.reshape(shard[k].shape)
    off = 0
    loss = None
    for k in SMALL:
        sz = math.prod(gr[k].shape)
        fullg = small_parts[off:off + sz].reshape(gr[k].shape)
        off += sz
        if k == "loss_cols":
            loss = jnp.sum(fullg)
            continue
        if k in ("w_pool_lin", "conv_w"):
            n = shard[k].shape[2]
            fullg = lax.dynamic_slice_in_dim(fullg, chip * n, n, axis=1)
        grads[k] = fullg.reshape(shard[k].shape)

    deltas, new_m, new_v = {}, {}, {}
    for k in ORDER:
        shp = shard[k].shape
        two_d = (-1, shp[-1])
        dl, nm, nv = _adamw(shard[k].reshape(two_d), grads[k].reshape(two_d), mom[k].reshape(two_d), vel[k].reshape(two_d), f"adamw_{k}")
        deltas[k], new_m[k], new_v[k] = dl.reshape(shp), nm.reshape(shp), nv.reshape(shp)

    return (loss, grad_x[None], *[grads[k] for k in ORDER], *[deltas[k] for k in ORDER], *[new_m[k] for k in ORDER], *[new_v[k] for k in ORDER])
```

```python
import functools
import math

import jax
import jax.numpy as jnp
from jax import lax
from jax.experimental import pallas as pl
from jax.experimental.pallas import tpu as pltpu

F32 = jnp.float32
BF16 = jnp.bfloat16

RMS_EPS = 1e-6
POOL_WINDOWS = (2, 4, 8, 16)
ATTN_GROUPS = ((128, 1), (512, 4), (2048, 16))
HEADS_PER_GROUP = 4
HEAD_DIM = 128
N_ATTN_HEADS = HEADS_PER_GROUP * len(ATTN_GROUPS)
SPAN = 128
GROUP_WIDTH = HEADS_PER_GROUP * HEAD_DIM
ATTN_WIDTH = N_ATTN_HEADS * HEAD_DIM
ATTN_SCALE = HEAD_DIM ** -0.5
NEG_BIG = -1e30
ALIBI_SLOPES = tuple(2.0 ** (-8.0 * (h + 1) / N_ATTN_HEADS) for h in range(N_ATTN_HEADS))

ADAM_LR = 0.001
ADAM_B1 = 0.9
ADAM_B2 = 0.999
ADAM_EPS = 1e-08
ADAM_WD = 0.01
ADAM_STEP = 10

INV_SQRT2 = 1.0 / math.sqrt(2.0)
INV_SQRT_2PI = 1.0 / math.sqrt(2.0 * math.pi)

HALO = 16
VMEM_LIMIT = 56 * 1024 * 1024
N_CHIPS = 4
N_DEV = 8
MESH = pl.DeviceIdType.MESH
ANY = pl.BlockSpec(memory_space=pl.ANY)


def _cparams(*sem):
    return pltpu.CompilerParams(dimension_semantics=sem, vmem_limit_bytes=VMEM_LIMIT)


def _tile(n, pref, mult=128):
    t = (min(pref, n) // mult) * mult
    while t >= mult:
        if n % t == 0:
            return t
        t -= mult
    return n


def _dot(a, b, contract):
    return lax.dot_general(a, b, (contract, ((), ())), preferred_element_type=F32)


def _dot_nn(a, b):
    return _dot(a, b, ((1,), (0,)))


def _dot_nt(a, b):
    return _dot(a, b, ((1,), (1,)))


def _mm(a, b, *, mode, dims, name, tiles=None, out_dtypes=(BF16,), epilogue=None, extras=(), a_off=(0, 0), b_off=(0, 0), after=None):
    M, N, K = dims
    if tiles is None:
        tiles = (_tile(M, 1408), _tile(N, 2816), _tile(K, 512)) if mode == "tn" else (_tile(M, 1024), _tile(N, 1536), _tile(K, 2816))
    tm, tn, tk = tiles
    assert M % tm == 0 and N % tn == 0 and K % tk == 0, (name, dims, tiles)
    nk = K // tk
    if mode == "nn":
        ab, bb, contract = (tm, tk), (tk, tn), ((1,), (0,))
        amap = lambda i, j, k: (i + a_off[0] // tm, k + a_off[1] // tk)
        bmap = lambda i, j, k: (k + b_off[0] // tk, j + b_off[1] // tn)
    elif mode == "nt":
        ab, bb, contract = (tm, tk), (tn, tk), ((1,), (1,))
        amap = lambda i, j, k: (i + a_off[0] // tm, k + a_off[1] // tk)
        bmap = lambda i, j, k: (j + b_off[0] // tn, k + b_off[1] // tk)
    else:
        ab, bb, contract = (tk, tm), (tk, tn), ((0,), (0,))
        amap = lambda i, j, k: (k + a_off[0] // tk, i + a_off[1] // tm)
        bmap = lambda i, j, k: (k + b_off[0] // tk, j + b_off[1] // tn)
    assert a_off[0] % ab[0] == 0 and a_off[1] % ab[1] == 0 and b_off[0] % bb[0] == 0 and b_off[1] % bb[1] == 0, name
    in_specs = [pl.BlockSpec(ab, amap), pl.BlockSpec(bb, bmap)]
    ex_arrays = []
    for arr, kind, off in extras:
        if kind == "mn":
            assert off[0] % tm == 0 and off[1] % tn == 0, name
            in_specs.append(pl.BlockSpec((tm, tn), lambda i, j, k, off=off: (i + off[0] // tm, j + off[1] // tn)))
        else:
            assert off[1] % tn == 0, name
            in_specs.append(pl.BlockSpec((1, tn), lambda i, j, k, off=off: (0, j + off[1] // tn)))
        ex_arrays.append(arr)
    ne, no = len(ex_arrays), len(out_dtypes)
    if after is not None:
        in_specs.append(ANY)
        ex_arrays.append(after)
    first_out = 2 + len(ex_arrays)
    if epilogue is None:
        def epilogue(acc, ex, outs):
            outs[0][...] = acc.astype(outs[0].dtype)

    def body(*refs):
        a_ref, b_ref = refs[0], refs[1]
        ex, outs = refs[2:2 + ne], refs[first_out:first_out + no]
        if nk == 1:
            epilogue(_dot(a_ref[...], b_ref[...], contract), ex, outs)
            return
        acc = refs[-1]
        k = pl.program_id(2)
        if nk <= 4:
            part = _dot(a_ref[...], b_ref[...], contract)

            @pl.when(k == 0)
            def _():
                acc[...] = part

            @pl.when(jnp.logical_and(k > 0, k < nk - 1))
            def _():
                acc[...] += part

            @pl.when(k == nk - 1)
            def _():
                epilogue(acc[...] + part, ex, outs)
        else:
            @pl.when(k == 0)
            def _():
                acc[...] = _dot(a_ref[...], b_ref[...], contract)

            @pl.when(k > 0)
            def _():
                acc[...] += _dot(a_ref[...], b_ref[...], contract)

            @pl.when(k == nk - 1)
            def _():
                epilogue(acc[...], ex, outs)

    res = pl.pallas_call(
        body,
        grid=(M // tm, N // tn, nk),
        in_specs=in_specs,
        out_specs=[pl.BlockSpec((tm, tn), lambda i, j, k: (i, j)) for _ in out_dtypes],
        out_shape=[jax.ShapeDtypeStruct((M, N), dt) for dt in out_dtypes],
        scratch_shapes=[pltpu.VMEM((tm, tn), F32)] if nk > 1 else [],
        compiler_params=_cparams("parallel", "parallel", "arbitrary"),
        name=name,
    )(a, b, *ex_arrays)
    return res[0] if no == 1 else res


def _rms_fwd(x, g, name):
    S, D = x.shape
    tm = _tile(S, 256)

    def body(x_ref, g_ref, h_ref):
        xv = x_ref[...]
        r = lax.rsqrt(jnp.mean(xv * xv, axis=-1, keepdims=True) + RMS_EPS)
        h_ref[...] = (xv * r * g_ref[...]).astype(h_ref.dtype)

    return pl.pallas_call(
        body,
        grid=(S // tm,),
        in_specs=[pl.BlockSpec((tm, D), lambda i: (i, 0)), pl.BlockSpec((1, D), lambda i: (0, 0))],
        out_specs=pl.BlockSpec((tm, D), lambda i: (i, 0)),
        out_shape=jax.ShapeDtypeStruct((S, D), BF16),
        compiler_params=_cparams("parallel"),
        name=name,
    )(x, g)


def _rms_bwd(dh, x, g, dres, name, out_dtype):
    S, D = x.shape
    tm = _tile(S, 256)

    def body(dh_ref, x_ref, g_ref, dres_ref, dx_ref, dg_ref):
        xv = x_ref[...]
        r = lax.rsqrt(jnp.mean(xv * xv, axis=-1, keepdims=True) + RMS_EPS)
        xr = xv * r
        dhv = dh_ref[...].astype(F32)

        @pl.when(pl.program_id(0) == 0)
        def _():
            dg_ref[...] = jnp.zeros_like(dg_ref)

        dg_ref[...] += jnp.sum(dhv * xr, axis=0, keepdims=True)
        u = dhv * g_ref[...]
        c = jnp.mean(u * xr, axis=-1, keepdims=True)
        dx_ref[...] = (dres_ref[...].astype(F32) + r * (u - xr * c)).astype(dx_ref.dtype)

    row = pl.BlockSpec((tm, D), lambda i: (i, 0))
    vec = pl.BlockSpec((1, D), lambda i: (0, 0))
    return pl.pallas_call(
        body,
        grid=(S // tm,),
        in_specs=[row, row, vec, row],
        out_specs=[row, vec],
        out_shape=[jax.ShapeDtypeStruct((S, D), out_dtype), jax.ShapeDtypeStruct((1, D), F32)],
        compiler_params=_cparams("arbitrary"),
        name=name,
    )(dh, x, g, dres)


def _loss_head(x3, tgt, g, name):
    S, D = x3.shape
    tm = _tile(S, 256)

    def body(x_ref, t_ref, g_ref, dxb_ref, dg_ref, loss_ref):
        xv = x_ref[...]
        gv = g_ref[...]
        r = lax.rsqrt(jnp.mean(xv * xv, axis=-1, keepdims=True) + RMS_EPS)
        xr = xv * r
        e = xr * gv - t_ref[...]

        @pl.when(pl.program_id(0) == 0)
        def _():
            dg_ref[...] = jnp.zeros_like(dg_ref)
            loss_ref[...] = jnp.zeros_like(loss_ref)

        loss_ref[...] += jnp.sum(e * e, axis=0, keepdims=True) * (0.5 / D)
        dy = e * (1.0 / D)
        dg_ref[...] += jnp.sum(dy * xr, axis=0, keepdims=True)
        u = dy * gv
        c = jnp.mean(u * xr, axis=-1, keepdims=True)
        dxb_ref[...] = (r * (u - xr * c)).astype(BF16)

    row = pl.BlockSpec((tm, D), lambda i: (i, 0))
    vec = pl.BlockSpec((1, D), lambda i: (0, 0))
    return pl.pallas_call(
        body,
        grid=(S // tm,),
        in_specs=[row, row, vec],
        out_specs=[row, vec, vec],
        out_shape=[jax.ShapeDtypeStruct((S, D), BF16), jax.ShapeDtypeStruct((1, D), F32), jax.ShapeDtypeStruct((1, D), F32)],
        compiler_params=_cparams("arbitrary"),
        name=name,
    )(x3, tgt, g)


def _conv_taps(cur_ref, halo_ref, w_ref, b_ref, first):
    cur = cur_ref[...].astype(F32)
    halo = jnp.where(first, 0.0, halo_ref[...].astype(F32))
    xx = jnp.concatenate([halo, cur], axis=0)
    p1 = pltpu.roll(xx, 1, 0)[HALO:]
    p2 = pltpu.roll(xx, 2, 0)[HALO:]
    w = w_ref[...]
    y = b_ref[...] + w[0:1] * p2 + w[1:2] * p1 + w[2:3] * cur
    return y, (cur, p1, p2)


def _convglu_fwd(up, cw, cb, name):
    S, F2 = up.shape
    F = F2 // 2
    tm, tn = _tile(S, 512), _tile(F, 512)
    nj, hb = F // tn, tm // HALO

    def body(ua, ub, ha, hb_, wa, wb, ba, bb, f_ref, a_ref, b_ref):
        first = pl.program_id(0) == 0
        a, _ = _conv_taps(ua, ha, wa, ba, first)
        b, _ = _conv_taps(ub, hb_, wb, bb, first)
        f_ref[...] = (0.5 * a * (1.0 + lax.erf(a * INV_SQRT2)) * b).astype(f_ref.dtype)
        a_ref[...] = a.astype(a_ref.dtype)
        b_ref[...] = b.astype(b_ref.dtype)

    tile = pl.BlockSpec((tm, tn), lambda i, j: (i, j))
    return pl.pallas_call(
        body,
        grid=(S // tm, nj),
        in_specs=[
            tile,
            pl.BlockSpec((tm, tn), lambda i, j: (i, j + nj)),
            pl.BlockSpec((HALO, tn), lambda i, j: (jnp.maximum(i * hb - 1, 0), j)),
            pl.BlockSpec((HALO, tn), lambda i, j: (jnp.maximum(i * hb - 1, 0), j + nj)),
            pl.BlockSpec((3, tn), lambda i, j: (0, j)),
            pl.BlockSpec((3, tn), lambda i, j: (0, j + nj)),
            pl.BlockSpec((1, tn), lambda i, j: (0, j)),
            pl.BlockSpec((1, tn), lambda i, j: (0, j + nj)),
        ],
        out_specs=[tile, tile, tile],
        out_shape=[jax.ShapeDtypeStruct((S, F), BF16)] * 3,
        compiler_params=_cparams("parallel", "parallel"),
        name=name,
    )(up, up, up, up, cw, cw, cb, cb)


def _convglu_bwd(df, a, b, up, cw, name):
    S, F = df.shape
    tm, tn = _tile(S, 512), _tile(F, 512)
    nj, ni, hb = F // tn, S // tm, tm // HALO
    n = tm + HALO

    def body(df_ref, dfn_ref, a_ref, an_ref, b_ref, bn_ref, up_ref, w_ref, o_ref, db_ref, dw_ref):
        j, i = pl.program_id(0), pl.program_id(1)
        last = i == ni - 1

        def rows(c_ref, n_ref):
            return jnp.concatenate([c_ref[...].astype(F32), jnp.where(last, 0.0, n_ref[...].astype(F32))], axis=0)

        @pl.when(i == 0)
        def _():
            db_ref[...] = jnp.zeros_like(db_ref)
            dw_ref[...] = jnp.zeros_like(dw_ref)

        def finish(d):
            d0 = d[:tm]
            d1 = pltpu.roll(d, n - 1, 0)[:tm]
            d2 = pltpu.roll(d, n - 2, 0)[:tm]
            w = w_ref[...]
            o_ref[...] = (w[2:3] * d0 + w[1:2] * d1 + w[0:1] * d2).astype(o_ref.dtype)
            upv = up_ref[...].astype(F32)
            db_ref[...] += jnp.sum(d0, axis=0, keepdims=True)
            dw_ref[0:1, :] += jnp.sum(d2 * upv, axis=0, keepdims=True)
            dw_ref[1:2, :] += jnp.sum(d1 * upv, axis=0, keepdims=True)
            dw_ref[2:3, :] += jnp.sum(d0 * upv, axis=0, keepdims=True)

        av, dfv = rows(a_ref, an_ref), rows(df_ref, dfn_ref)
        cdf = 0.5 * (1.0 + lax.erf(av * INV_SQRT2))

        @pl.when(j < nj)
        def _():
            pdf = jnp.exp(-0.5 * av * av) * INV_SQRT_2PI
            finish(dfv * rows(b_ref, bn_ref) * (cdf + av * pdf))

        @pl.when(j >= nj)
        def _():
            finish(dfv * (av * cdf))

    jh = lambda j: lax.rem(j, nj)
    nxt = lambda i: jnp.minimum((i + 1) * hb, S // HALO - 1)
    cur = pl.BlockSpec((tm, tn), lambda j, i: (i, jh(j)))
    halo = pl.BlockSpec((HALO, tn), lambda j, i: (nxt(i), jh(j)))
    return pl.pallas_call(
        body,
        grid=(2 * nj, ni),
        in_specs=[cur, halo, cur, halo, cur, halo, pl.BlockSpec((tm, tn), lambda j, i: (i, j)), pl.BlockSpec((3, tn), lambda j, i: (0, j))],
        out_specs=[pl.BlockSpec((tm, tn), lambda j, i: (i, j)), pl.BlockSpec((1, tn), lambda j, i: (0, j)), pl.BlockSpec((3, tn), lambda j, i: (0, j))],
        out_shape=[jax.ShapeDtypeStruct((S, 2 * F), BF16), jax.ShapeDtypeStruct((1, 2 * F), F32), jax.ShapeDtypeStruct((3, 2 * F), F32)],
        compiler_params=_cparams("parallel", "arbitrary"),
        name=name,
    )(df, df, a, a, b, b, up, cw)


def _gate_bwd(dmixed, gates, y_pool, y_attn, in_width, name):
    S, D = dmixed.shape
    tm, tn = _tile(S, 512), _tile(D, 512)
    nj = D // tn
    pre0 = (in_width - 2 * D) // tn
    assert pre0 * tn == in_width - 2 * D

    def body(dm_ref, g_ref, yp_ref, ya_ref, dy_ref, dpre_ref, db_ref):
        j = pl.program_id(0)

        @pl.when(pl.program_id(1) == 0)
        def _():
            db_ref[...] = jnp.zeros_like(db_ref)

        def run(y_ref):
            dm = dm_ref[...].astype(F32)
            gv = g_ref[...].astype(F32)
            dy_ref[...] = (dm * gv).astype(BF16)
            dpre = dm * y_ref[...].astype(F32) * gv * (1.0 - gv)
            dpre_ref[...] = dpre.astype(BF16)
            db_ref[...] += jnp.sum(dpre, axis=0, keepdims=True)

        @pl.when(j < nj)
        def _():
            run(yp_ref)

        @pl.when(j >= nj)
        def _():
            run(ya_ref)

    tile2 = pl.BlockSpec((tm, tn), lambda j, i: (i, j))
    return pl.pallas_call(
        body,
        grid=(2 * nj, S // tm),
        in_specs=[
            pl.BlockSpec((tm, tn), lambda j, i: (i, lax.rem(j, nj))),
            tile2,
            pl.BlockSpec((tm, tn), lambda j, i: (i, jnp.minimum(j, nj - 1))),
            pl.BlockSpec((tm, tn), lambda j, i: (i, jnp.maximum(j - nj, 0))),
        ],
        out_specs=[tile2, pl.BlockSpec((tm, tn), lambda j, i: (i, pre0 + j)), pl.BlockSpec((1, tn), lambda j, i: (0, j))],
        out_shape=[jax.ShapeDtypeStruct((S, 2 * D), BF16), jax.ShapeDtypeStruct((S, in_width), BF16), jax.ShapeDtypeStruct((1, 2 * D), F32)],
        compiler_params=_cparams("parallel", "arbitrary"),
        name=name,
    )(dmixed, gates, y_pool, y_attn)


def _pool_counts(i, tm, rows, w):
    t = i * tm + lax.broadcasted_iota(jnp.int32, (rows, 1), 0)
    return jnp.minimum(t + 1, w).astype(F32)


def _pooled_groups(u_ref, uh_ref, i, tm, C):
    cur = u_ref[...]
    halo = jnp.where(i == 0, 0.0, uh_ref[...])
    xx = jnp.concatenate([halo, cur], axis=0)
    out = []
    s = xx
    for gi, w in enumerate(POOL_WINDOWS):
        s = s + pltpu.roll(s, w // 2, 0)
        tot = s[HALO:, 0:C]
        out.append(tot / _pool_counts(i, tm, tm, w) - cur[:, gi * C:(gi + 1) * C])
        s = s[:, C:] if gi + 1 < len(POOL_WINDOWS) else s
    return out


def _pool_fwd(u, wl, scale, name):
    S, PW = u.shape
    C = PW // len(POOL_WINDOWS)
    tm = _tile(S, 512)
    hb = tm // HALO

    def body(u_ref, uh_ref, wl_ref, sc_ref, o_ref):
        i = pl.program_id(0)
        pooled = _pooled_groups(u_ref, uh_ref, i, tm, C)
        for gi in range(len(POOL_WINDOWS)):
            y = _dot_nn(pooled[gi].astype(BF16), wl_ref[gi])
            o_ref[:, gi * C:(gi + 1) * C] = (y * sc_ref[:, gi * C:(gi + 1) * C]).astype(o_ref.dtype)

    return pl.pallas_call(
        body,
        grid=(S // tm,),
        in_specs=[
            pl.BlockSpec((tm, PW), lambda i: (i, 0)),
            pl.BlockSpec((HALO, PW), lambda i: (jnp.maximum(i * hb - 1, 0), 0)),
            pl.BlockSpec((len(POOL_WINDOWS), C, C), lambda i: (0, 0, 0)),
            pl.BlockSpec((1, PW), lambda i: (0, 0)),
        ],
        out_specs=pl.BlockSpec((tm, PW), lambda i: (i, 0)),
        out_shape=jax.ShapeDtypeStruct((S, PW), BF16),
        compiler_params=_cparams("parallel"),
        name=name,
    )(u, u, wl, scale)


def _pool_bwd(u, dp, wl, scale, dproj, name):
    S, PW = u.shape
    G = len(POOL_WINDOWS)
    C = PW // G
    tm = _tile(S, 512)
    hb, ni = tm // HALO, S // tm
    n = tm + HALO

    def body(u_ref, uh_ref, dp_ref, dpn_ref, wl_ref, sc_ref, _, du_ref, dwl_ref, dsc_ref):
        i = pl.program_id(0)

        @pl.when(i == 0)
        def _():
            dwl_ref[...] = jnp.zeros_like(dwl_ref)
            dsc_ref[...] = jnp.zeros_like(dsc_ref)

        pooled = _pooled_groups(u_ref, uh_ref, i, tm, C)
        dpc = dp_ref[...].astype(F32)
        dpn = jnp.where(i == ni - 1, 0.0, dpn_ref[...].astype(F32))
        sc = sc_ref[...]
        dyl = jnp.concatenate([dpc, dpn], axis=0) * sc
        for gi, w in enumerate(POOL_WINDOWS):
            cols = slice(gi * C, (gi + 1) * C)
            pb = pooled[gi].astype(BF16)
            ylin = _dot_nn(pb, wl_ref[gi])
            dsc_ref[:, cols] += jnp.sum(dpc[:, cols] * ylin, axis=0, keepdims=True)
            dylg = dyl[:, cols].astype(BF16)
            dwl_ref[gi] += _dot(pb, dylg[:tm], ((0,), (0,)))
            dpool = _dot_nt(dylg, wl_ref[gi])
            e = dpool / _pool_counts(i, tm, n, w)
            k = 1
            while k < w:
                e = e + pltpu.roll(e, n - k, 0)
                k *= 2
            du_ref[:, cols] = (e[:tm] - dpool[:tm]).astype(du_ref.dtype)

    return pl.pallas_call(
        body,
        grid=(ni,),
        in_specs=[
            pl.BlockSpec((tm, PW), lambda i: (i, 0)),
            pl.BlockSpec((HALO, PW), lambda i: (jnp.maximum(i * hb - 1, 0), 0)),
            pl.BlockSpec((tm, PW), lambda i: (i, 0)),
            pl.BlockSpec((HALO, PW), lambda i: (jnp.minimum((i + 1) * hb, S // HALO - 1), 0)),
            pl.BlockSpec((G, C, C), lambda i: (0, 0, 0)),
            pl.BlockSpec((1, PW), lambda i: (0, 0)),
            ANY,
        ],
        out_specs=[pl.BlockSpec((tm, PW), lambda i: (i, 0)), pl.BlockSpec((G, C, C), lambda i: (0, 0, 0)), pl.BlockSpec((1, PW), lambda i: (0, 0))],
        out_shape=[jax.ShapeDtypeStruct(dproj.shape, dproj.dtype), jax.ShapeDtypeStruct((G, C, C), F32), jax.ShapeDtypeStruct((1, PW), F32)],
        input_output_aliases={6: 0},
        compiler_params=_cparams("arbitrary"),
        name=name,
    )(u, u, dp, dp, wl, scale, dproj)


def _band_masks():
    ii = lax.broadcasted_iota(jnp.int32, (SPAN, SPAN), 0)
    kk = lax.broadcasted_iota(jnp.int32, (SPAN, SPAN), 1)
    return ((ii + SPAN - kk).astype(F32), kk >= ii), ((ii - kk).astype(F32), kk <= ii)


ATTN_TILE = 16 * SPAN


def _unit_rows(r, b, d, blocks=1):
    return pl.ds(d * SPAN * b + r, blocks * SPAN, stride=d) if d > 1 else pl.ds(SPAN * b, blocks * SPAN)


def _f32_copies(refs, scratch, d):
    if d == 1:
        return list(refs)
    for ref, s in zip(refs, scratch):
        s[...] = ref[...].astype(F32)
    return list(scratch)


def _attn_fwd(qkv, d, g, name):
    S = qkv.shape[0]
    T = min(ATTN_TILE, S)
    P = SPAN * d
    nbk = T // P

    def body(q_ref, k_ref, v_ref, kp_ref, vp_ref, o_ref, lse_ref, *scratch):
        c = pl.program_id(0)
        (jp, mp), (jc, mc) = _band_masks()
        slopes = [ALIBI_SLOPES[g * HEADS_PER_GROUP + h] * d for h in range(HEADS_PER_GROUP)]
        slope = slopes[0]
        for h in range(1, HEADS_PER_GROUP):
            slope = jnp.where(pl.program_id(1) == h, slopes[h], slope)
        q_s, k_s, v_s, kp_s, vp_s = _f32_copies((q_ref, k_ref, v_ref, kp_ref, vp_ref), scratch[:5], d)
        o_s, l_s = (o_ref, lse_ref) if d == 1 else scratch[5:7]
        bias_p, bias_c = jnp.where(mp, -slope * jp, NEG_BIG), jnp.where(mc, -slope * jc, NEG_BIG)
        bias = jnp.concatenate([bias_p, bias_c], axis=1)
        bias_first = jnp.concatenate([jnp.where(c > 0, bias_p, NEG_BIG), bias_c], axis=1)
        for r in range(d):
            for b in range(nbk):
                rows = _unit_rows(r, b, d)
                q = q_s[rows, :].astype(BF16)
                if b == 0:
                    prev = _unit_rows(r, 0, d)
                    kk = jnp.concatenate([kp_s[prev, :], k_s[rows, :]], axis=0).astype(BF16)
                    vv = jnp.concatenate([vp_s[prev, :], v_s[rows, :]], axis=0).astype(BF16)
                else:
                    both = _unit_rows(r, b - 1, d, 2)
                    kk, vv = k_s[both, :].astype(BF16), v_s[both, :].astype(BF16)
                s = _dot_nt(q, kk) * ATTN_SCALE + (bias_first if b == 0 else bias)
                m = jnp.max(s, axis=-1, keepdims=True)
                p = jnp.exp(s - m)
                l = jnp.sum(p, axis=-1, keepdims=True)
                o_s[rows, :] = _dot_nn(p.astype(BF16), vv) / l
                l_s[rows, :] = jnp.broadcast_to(m + jnp.log(l), (SPAN, HEAD_DIM))
        if d > 1:
            o_ref[...] = o_s[...]
            lse_ref[...] = l_s[...]

    col = lambda kind: (lambda c, h: (c, kind * N_ATTN_HEADS + g * HEADS_PER_GROUP + h))
    pcol = lambda kind: (lambda c, h: (jnp.maximum(c * nbk - 1, 0), kind * N_ATTN_HEADS + g * HEADS_PER_GROUP + h))
    cur = lambda kind: pl.BlockSpec((T, HEAD_DIM), col(kind))
    prv = lambda kind: pl.BlockSpec((P, HEAD_DIM), pcol(kind))
    out = pl.BlockSpec((T, HEAD_DIM), lambda c, h: (c, h))
    scratch = [] if d == 1 else [pltpu.VMEM((T, HEAD_DIM), F32)] * 3 + [pltpu.VMEM((P, HEAD_DIM), F32)] * 2 + [pltpu.VMEM((T, HEAD_DIM), F32)] * 2
    return pl.pallas_call(
        body,
        grid=(S // T, HEADS_PER_GROUP),
        in_specs=[cur(0), cur(1), cur(2), prv(1), prv(2)],
        out_specs=[out, out],
        out_shape=[jax.ShapeDtypeStruct((S, GROUP_WIDTH), F32)] * 2,
        scratch_shapes=scratch,
        compiler_params=_cparams("parallel", "parallel"),
        name=name,
    )(qkv, qkv, qkv, qkv, qkv)


def _attn_merge(os_, lses, name):
    S, W = os_[0].shape
    tm = _tile(S, 512)

    def body(o0, o1, o2, l0, l1, l2, y_ref, lse_ref):
        ls = [l0[...], l1[...], l2[...]]
        m = jnp.maximum(jnp.maximum(ls[0], ls[1]), ls[2])
        es = [jnp.exp(v - m) for v in ls]
        tot = es[0] + es[1] + es[2]
        y = (es[0] * o0[...] + es[1] * o1[...] + es[2] * o2[...]) / tot
        y_ref[...] = y.astype(y_ref.dtype)
        lse_ref[...] = m + jnp.log(tot)

    row = pl.BlockSpec((tm, W), lambda i: (i, 0))
    return pl.pallas_call(
        body,
        grid=(S // tm,),
        in_specs=[row] * 6,
        out_specs=[row, row],
        out_shape=[jax.ShapeDtypeStruct((S, W), BF16), jax.ShapeDtypeStruct((S, W), F32)],
        compiler_params=_cparams("parallel"),
        name=name,
    )(*os_, *lses)


def _attn_bwd(qkv, dattn, y, lse, dproj, d, g, col0, name):
    S = qkv.shape[0]
    T = min(ATTN_TILE, S)
    P = SPAN * d
    nbk = T // P
    ntile = S // T

    def body(q_ref, k_ref, v_ref, kp_ref, vp_ref, qn_ref, da_ref, dan_ref, y_ref, yn_ref, lse_ref, lsen_ref, _, out_ref, dq_s, dk_s, dv_s, *scratch):
        c = pl.program_id(0)
        head_id = pl.program_id(1)
        kind = pl.program_id(2)

        @pl.when(kind == 0)
        def _():
            (jp, mp), (jc, mc) = _band_masks()
            slopes = [ALIBI_SLOPES[g * HEADS_PER_GROUP + h] * d for h in range(HEADS_PER_GROUP)]
            slope = slopes[0]
            for h in range(1, HEADS_PER_GROUP):
                slope = jnp.where(head_id == h, slopes[h], slope)
            q_s, k_s, v_s, da_s, y_s, kp_s, vp_s, qn_s, dan_s, yn_s = _f32_copies(
                (q_ref, k_ref, v_ref, da_ref, y_ref, kp_ref, vp_ref, qn_ref, dan_ref, yn_ref), scratch, d)
            bias_p, bias_c = jnp.where(mp, -slope * jp, NEG_BIG), jnp.where(mc, -slope * jc, NEG_BIG)
            bias = jnp.concatenate([bias_c, bias_p], axis=0)
            bias_last = jnp.concatenate([bias_c, jnp.where(c < ntile - 1, bias_p, NEG_BIG)], axis=0)
            bias_first = jnp.where(c > 0, bias_p, NEG_BIG)

            def pair(q, da, yy, lse_blk, kk, vv, b):
                dd = jnp.sum(da.astype(F32) * yy.astype(F32), axis=-1, keepdims=True)
                p = jnp.exp(_dot_nt(q, kk) * ATTN_SCALE + b - lse_blk[:, 0:1])
                return p, p * (_dot_nt(da, vv) - dd)

            for r in range(d):
                first = _unit_rows(r, 0, d)
                kk, vv = kp_s[first, :].astype(BF16), vp_s[first, :].astype(BF16)
                _, ds = pair(q_s[first, :].astype(BF16), da_s[first, :].astype(BF16), y_s[first, :], lse_ref[first, :], kk, vv, bias_first)
                dq_next = _dot_nn(ds.astype(BF16), kk)
                for kb in range(nbk):
                    rows = _unit_rows(r, kb, d)
                    if kb + 1 < nbk:
                        both = _unit_rows(r, kb, d, 2)
                        q, da, yy, lse_blk = q_s[both, :], da_s[both, :], y_s[both, :], lse_ref[both, :]
                    else:
                        q = jnp.concatenate([q_s[rows, :], qn_s[first, :]], axis=0)
                        da = jnp.concatenate([da_s[rows, :], dan_s[first, :]], axis=0)
                        yy = jnp.concatenate([y_s[rows, :], yn_s[first, :]], axis=0)
                        lse_blk = jnp.concatenate([lse_ref[rows, :], lsen_ref[first, :]], axis=0)
                    q, da = q.astype(BF16), da.astype(BF16)
                    kk, vv = k_s[rows, :].astype(BF16), v_s[rows, :].astype(BF16)
                    p, ds = pair(q, da, yy, lse_blk, kk, vv, bias if kb + 1 < nbk else bias_last)
                    dv_s[rows, :] = _dot_nn(p.T.astype(BF16), da)
                    dk_s[rows, :] = _dot_nn(ds.T.astype(BF16), q) * ATTN_SCALE
                    dq_both = _dot_nn(ds.astype(BF16), kk)
                    dq_s[rows, :] = (dq_next + dq_both[:SPAN]) * ATTN_SCALE
                    dq_next = dq_both[SPAN:]
            out_ref[...] = dq_s[...].astype(out_ref.dtype)

        @pl.when(kind == 1)
        def _():
            out_ref[...] = dk_s[...].astype(out_ref.dtype)

        @pl.when(kind == 2)
        def _():
            out_ref[...] = dv_s[...].astype(out_ref.dtype)

    head = lambda h: g * HEADS_PER_GROUP + h
    cur = lambda kind: pl.BlockSpec((T, HEAD_DIM), lambda c, h, kd: (c, kind * N_ATTN_HEADS + head(h)))
    prv = lambda kind: pl.BlockSpec((P, HEAD_DIM), lambda c, h, kd: (jnp.maximum(c * nbk - 1, 0), kind * N_ATTN_HEADS + head(h)))
    nxt_row = lambda c: jnp.minimum((c + 1) * nbk, S // P - 1)
    qnext = pl.BlockSpec((P, HEAD_DIM), lambda c, h, kd: (nxt_row(c), head(h)))
    hcur = pl.BlockSpec((T, HEAD_DIM), lambda c, h, kd: (c, h))
    hnext = pl.BlockSpec((P, HEAD_DIM), lambda c, h, kd: (nxt_row(c), h))
    out = pl.BlockSpec((T, HEAD_DIM), lambda c, h, kd: (c, col0 + kd * N_ATTN_HEADS + head(h)))
    stage = [pltpu.VMEM((T, HEAD_DIM), F32)] * 3
    copies = [] if d == 1 else [pltpu.VMEM((T, HEAD_DIM), F32)] * 5 + [pltpu.VMEM((P, HEAD_DIM), F32)] * 5
    return pl.pallas_call(
        body,
        grid=(ntile, HEADS_PER_GROUP, 3),
        in_specs=[cur(0), cur(1), cur(2), prv(1), prv(2), qnext, hcur, hnext, hcur, hnext, hcur, hnext, ANY],
        out_specs=out,
        out_shape=jax.ShapeDtypeStruct(dproj.shape, dproj.dtype),
        input_output_aliases={12: 0},
        scratch_shapes=stage + copies,
        compiler_params=_cparams("parallel", "parallel", "arbitrary"),
        name=name,
    )(qkv, qkv, qkv, qkv, qkv, qkv, dattn, dattn, y, y, lse, lse, dproj)


def _row_block(R, C, bytes_per_row_elem=4, budget=1 << 20):
    if R % 8:
        return R
    best = 8
    t = 8
    while t <= R:
        if R % t == 0 and t * C * bytes_per_row_elem <= budget:
            best = t
        t += 8
    return best


def _adamw(w, g, m, v, name):
    R, C = w.shape
    tr = _row_block(R, C)
    c1 = 1.0 - ADAM_B1 ** ADAM_STEP
    c2 = 1.0 - ADAM_B2 ** ADAM_STEP

    def body(w_ref, g_ref, m_ref, v_ref, d_ref, nm_ref, nv_ref):
        gv = g_ref[...]
        nm = ADAM_B1 * m_ref[...] + (1.0 - ADAM_B1) * gv
        nv = ADAM_B2 * v_ref[...] + (1.0 - ADAM_B2) * (gv * gv)
        d_ref[...] = -ADAM_LR * ((nm / c1) / (jnp.sqrt(nv / c2) + ADAM_EPS) + ADAM_WD * w_ref[...])
        nm_ref[...] = nm
        nv_ref[...] = nv

    blk = pl.BlockSpec((tr, C), lambda i: (i, 0))
    return pl.pallas_call(
        body,
        grid=(R // tr,),
        in_specs=[blk] * 4,
        out_specs=[blk] * 3,
        out_shape=[jax.ShapeDtypeStruct((R, C), F32)] * 3,
        compiler_params=_cparams("parallel"),
        name=name,
    )(w, g, m, v)


def _sum_pieces(grad, axis, recv, pos, name):
    n, pr, pc = recv.shape
    tr = _row_block(pr, pc, bytes_per_row_elem=(n + 1) * recv.dtype.itemsize, budget=4 << 20)
    nblk = pr // tr
    if axis == 1:
        own_map = lambda i, p: (p[1] * nblk + i, p[0])
    else:
        own_map = lambda i, p: ((2 * p[0] + p[1]) * nblk + i, 0)

    def body(p_ref, own_ref, r_ref, o_ref):
        acc = own_ref[...].astype(F32)
        for s in range(n):
            acc = acc + r_ref[s].astype(F32)
        o_ref[...] = acc

    return pl.pallas_call(
        body,
        grid_spec=pltpu.PrefetchScalarGridSpec(
            num_scalar_prefetch=1,
            grid=(nblk,),
            in_specs=[pl.BlockSpec((tr, pc), own_map), pl.BlockSpec((n, tr, pc), lambda i, p: (0, i, 0))],
            out_specs=pl.BlockSpec((tr, pc), lambda i, p: (p[1] * nblk + i, 0)),
        ),
        out_shape=jax.ShapeDtypeStruct((2 * pr, pc), F32),
        compiler_params=_cparams("parallel"),
        name=name,
    )(pos, grad, recv)


def _sum_small(own, recv, me, name):
    n, R, C = recv.shape
    tr = _row_block(R, C, bytes_per_row_elem=(n + 1) * 4, budget=4 << 20)

    def body(me_ref, own_ref, r_ref, o_ref):
        acc = None
        for dev in range(n + 1):
            k = jnp.bitwise_xor(me_ref[0], dev)
            term = jnp.where(k == 0, own_ref[...], r_ref[jnp.maximum(k - 1, 0)])
            acc = term if acc is None else acc + term
        o_ref[...] = acc

    return pl.pallas_call(
        body,
        grid_spec=pltpu.PrefetchScalarGridSpec(
            num_scalar_prefetch=1,
            grid=(R // tr,),
            in_specs=[pl.BlockSpec((tr, C), lambda i, m: (i, 0)), pl.BlockSpec((n, tr, C), lambda i, m: (0, i, 0))],
            out_specs=pl.BlockSpec((tr, C), lambda i, m: (i, 0)),
        ),
        out_shape=jax.ShapeDtypeStruct((R, C), F32),
        compiler_params=_cparams("parallel"),
        name=name,
    )(me, own, recv)


def _place(shard, axis, pos, dtype, name, after=None):
    extra = [] if after is None else [after]
    shp = list(shard.shape)
    shp[axis] *= N_CHIPS
    if shard.ndim == 3:
        assert axis == 1
        in_spec = pl.BlockSpec(shard.shape, lambda i, p: (0, 0, 0))
        out_spec = pl.BlockSpec(shard.shape, lambda i, p: (0, p[0], 0))
        grid = (1,)
    else:
        R, C = shard.shape
        tr = _row_block(R, C, bytes_per_row_elem=4, budget=2 << 20)
        nblk = R // tr
        in_spec = pl.BlockSpec((tr, C), lambda i, p: (i, 0))
        out_spec = pl.BlockSpec((tr, C), (lambda i, p: (i, p[0])) if axis == 1 else (lambda i, p: (p[0] * nblk + i, 0)))
        grid = (nblk,)

    def body(*refs):
        s_ref, o_ref = refs[1], refs[-1]
        o_ref[...] = s_ref[...].astype(o_ref.dtype)

    return pl.pallas_call(
        body,
        grid_spec=pltpu.PrefetchScalarGridSpec(num_scalar_prefetch=1, grid=grid, in_specs=[in_spec] + [ANY] * len(extra), out_specs=out_spec),
        out_shape=jax.ShapeDtypeStruct(tuple(shp), dtype),
        compiler_params=_cparams("parallel"),
        name=name,
    )(pos, shard, *extra)


HBM = pl.BlockSpec(memory_space=pltpu.HBM)
SEM = pl.BlockSpec(memory_space=pltpu.SEMAPHORE)
DATAFLOW = pltpu.SideEffectType.DATAFLOW_SIDE_EFFECTING


def _position():
    return lax.axis_index("x"), lax.axis_index("y"), lax.axis_index("c")


def _peer(k):
    x, y, c = _position()
    return ((1 - x) if k & 4 else x, (1 - y) if k & 2 else y, (1 - c) if k & 1 else c)


def _shard_slice(ref, axis, idx, size):
    start = idx * size
    if axis == ref.ndim - 1:
        start = pl.multiple_of(start, 128)
    ix = [slice(None)] * ref.ndim
    ix[axis] = pl.ds(start, size)
    return ref.at[tuple(ix)]


def _gather_plan(axes):
    def plan(refs):
        x, y, c = _position()
        out = []
        for ref, ax in zip(refs, axes):
            mine = _shard_slice(ref, ax, 2 * x + y, ref.shape[ax] // N_CHIPS)
            for k in (4, 2, 6):
                px, py, _ = _peer(k)
                out.append((mine, mine, (px, py, c)))
        return out
    return plan


def _scatter_plan(axes):
    m = len(axes)

    def plan(refs):
        out = []
        for t in range(m):
            grad, recv = refs[t], refs[m + t]
            _, pr, pc = recv.shape
            for k in range(1, N_DEV):
                px, py, pcore = _peer(k)
                if axes[t] == 0:
                    piece = grad.at[pl.ds(((2 * px + py) * 2 + pcore) * pr, pr), :]
                else:
                    piece = grad.at[pl.ds(pcore * pr, pr), pl.ds(pl.multiple_of((2 * px + py) * pc, 128), pc)]
                out.append((piece, recv.at[k - 1], (px, py, pcore)))
        return out
    return plan


def _broadcast_plan(refs):
    small, recv = refs
    return [(small, recv.at[k - 1], _peer(k)) for k in range(1, N_DEV)]


def _start_all(plan, refs, send_sems, recv_sems):
    for q, (src, dst, dev) in enumerate(plan(refs)):
        pltpu.make_async_remote_copy(src_ref=src, dst_ref=dst, send_sem=send_sems.at[q], recv_sem=recv_sems.at[q], device_id=dev, device_id_type=MESH).start()


def _wait_all(plan, refs, send_sems, recv_sems):
    for q, (src, dst, dev) in enumerate(plan(refs)):
        cp = pltpu.make_async_remote_copy(src_ref=src, dst_ref=dst, send_sem=send_sems.at[q], recv_sem=recv_sems.at[q], device_id=dev, device_id_type=MESH)
        cp.wait_send()
        cp.wait_recv()


def _push(bufs, plan, ncopies, name):
    n = len(bufs)

    def body(*refs):
        outs = refs[n:2 * n]
        send_sems, recv_sems = refs[2 * n:]
        _start_all(plan, outs, send_sems, recv_sems)
        _wait_all(plan, outs, send_sems, recv_sems)

    return pl.pallas_call(
        body,
        in_specs=[ANY] * n,
        out_specs=[ANY] * n,
        out_shape=[jax.ShapeDtypeStruct(b.shape, b.dtype) for b in bufs],
        input_output_aliases={t: t for t in range(n)},
        scratch_shapes=[pltpu.SemaphoreType.DMA((ncopies,)), pltpu.SemaphoreType.DMA((ncopies,))],
        name=name,
    )(*bufs)


def _half_slices_plan(onward):
    def plan(refs):
        ref, = refs
        x, y, c = _position()
        R2, C4 = ref.shape[0] // 2, ref.shape[1] // N_CHIPS
        out = []
        for k in (4, 2, 6):
            px, py, _ = _peer(k)
            chip = (2 * px + py) if onward else (2 * x + y)
            half = ref.at[pl.ds(c * R2, R2), pl.ds(pl.multiple_of(chip * C4, 128), C4)]
            out.append((half, half, (x, y, 1 - c) if onward else (px, py, c)))
        return out
    return plan


def _push_start(bufs, plan, ncopies, name, after=None):
    n = len(bufs)
    extra = [] if after is None else [after]

    def body(*refs):
        ins = refs[:n]
        first_out = n + len(extra)
        send_sems, recv_sems, token = refs[first_out], refs[first_out + 1], refs[-1]
        _start_all(plan, ins, send_sems, recv_sems)
        token[...] = jnp.zeros_like(token)

    res = pl.pallas_call(
        body,
        name=name,
        out_shape=(pltpu.SemaphoreType.DMA((ncopies,)), pltpu.SemaphoreType.DMA((ncopies,)), *[pltpu.HBM(b.shape, b.dtype) for b in bufs],
                   jax.ShapeDtypeStruct((8, 128), F32)),
        in_specs=[HBM] * n + [ANY] * len(extra),
        out_specs=(SEM, SEM, *[HBM] * n, pl.BlockSpec(memory_space=pltpu.VMEM)),
        input_output_aliases={t: t + 2 for t in range(n)},
        compiler_params=pltpu.CompilerParams(has_side_effects=DATAFLOW),
    )(*[pltpu.with_memory_space_constraint(b, pltpu.HBM) for b in bufs], *extra)
    return res[0], res[1], list(res[2:2 + n]), res[-1]


def _push_wait(send_sems, recv_sems, bufs, plan, after, name):
    n = len(bufs)
    after = list(after) if isinstance(after, (list, tuple)) else [after]

    def body(*refs):
        ins = refs[:n]
        _wait_all(plan, ins, refs[n], refs[n + 1])

    return pl.pallas_call(
        body,
        name=name,
        out_shape=tuple(pltpu.HBM(b.shape, b.dtype) for b in bufs),
        in_specs=[HBM] * n + [SEM, SEM] + [ANY] * len(after),
        out_specs=tuple([HBM] * n),
        input_output_aliases={t: t for t in range(n)},
        compiler_params=pltpu.CompilerParams(has_side_effects=DATAFLOW),
    )(*bufs, send_sems, recv_sems, *after)


EXCHANGE_CHUNKS = 2


def _exchange_plan(refs):
    x, y, c = _position()
    out = []
    for ref in refs:
        rows = ref.shape[0] // (2 * EXCHANGE_CHUNKS)
        for q in range(EXCHANGE_CHUNKS):
            mine = ref.at[pl.ds((c * EXCHANGE_CHUNKS + q) * rows, rows), :]
            out.append((mine, mine, (x, y, 1 - c)))
    return out


LATE_WEIGHTS = (("w_pool_lin", "w_pool_out", "w_attn_out", "w_out"), ("w_up", "conv_w", "w_down"))


def _local_step(x, tgt, w, late_weights, send):
    S, D = x.shape
    PW = w["pool_scale"].shape[1]
    o_q = PW
    o_g = PW + 3 * ATTN_WIDTH
    QKV = 3 * ATTN_WIDTH

    h1 = _rms_fwd(x, w["g_mix"], "rms1")
    w = dict(w, **late_weights("w_in", h1))
    proj_tiles = (_tile(S, 1024), 512, D)
    started = w.get("late_started")
    u = _mm(h1, w["w_in"], mode="nn", dims=(S, PW, D), tiles=proj_tiles, out_dtypes=(F32,), after=started, name="proj_u")
    qkv = _mm(h1, w["w_in"], mode="nn", dims=(S, QKV, D), tiles=proj_tiles, b_off=(0, o_q), after=started, name="proj_qkv")

    def gate_epilogue(acc, ex, outs):
        outs[0][...] = (1.0 / (1.0 + jnp.exp(-(acc + ex[0][...])))).astype(outs[0].dtype)

    gates = _mm(h1, w["w_in"], mode="nn", dims=(S, 2 * D, D), tiles=proj_tiles, b_off=(0, o_g), epilogue=gate_epilogue,
                extras=[(w["b_gate"], "n", (0, 0))], name="proj_gates")

    os_, lses = [], []
    for gi, (_, d) in enumerate(ATTN_GROUPS):
        o, lse = _attn_fwd(qkv, d, gi, f"attn_fwd{gi}")
        os_.append(o)
        lses.append(lse)
    attn, lse_tot = _attn_merge(os_, lses, "attn_merge")

    w = dict(w, **late_weights(0, attn))
    pool_out = _pool_fwd(u, w["w_pool_lin"], w["pool_scale"], "pool_fwd")
    y_pool = _mm(pool_out, w["w_pool_out"], mode="nn", dims=(S, D, PW), name="y_pool")

    def mix_epilogue(acc, ex, outs):
        outs[0][...] = acc.astype(BF16)
        outs[1][...] = (ex[0][...].astype(F32) * ex[2][...].astype(F32) + ex[1][...].astype(F32) * acc).astype(BF16)

    y_attn, mixed = _mm(attn, w["w_attn_out"], mode="nn", dims=(S, D, GROUP_WIDTH), out_dtypes=(BF16, BF16), epilogue=mix_epilogue,
                        extras=[(gates, "mn", (0, 0)), (gates, "mn", (0, D)), (y_pool, "mn", (0, 0))], name="y_attn_mix")

    def residual_epilogue(acc, ex, outs):
        outs[0][...] = ex[0][...] + acc

    x2 = _mm(mixed, w["w_out"], mode="nn", dims=(S, D, D), out_dtypes=(F32,), epilogue=residual_epilogue, extras=[(x, "mn", (0, 0))], name="out_proj")

    h2 = _rms_fwd(x2, w["g_ffn"], "rms2")
    w = dict(w, **late_weights(1, h2))
    F = w["w_down"].shape[0]
    up = _mm(h2, w["w_up"], mode="nn", dims=(S, 2 * F, D), name="up_proj")
    f, act_a, act_b = _convglu_fwd(up, w["conv_w"], w["conv_b"], "convglu_fwd")
    x3 = _mm(f, w["w_down"], mode="nn", dims=(S, D, F), out_dtypes=(F32,), epilogue=residual_epilogue, extras=[(x2, "mn", (0, 0))], name="down_proj")

    g = {}
    dx3b, g["g_final"], loss_cols = _loss_head(x3, tgt, w["g_final"], "loss_head")

    g["w_down"] = _mm(f, dx3b, mode="tn", dims=(F, D, S), name="dw_down")
    sent = send(("w_down",), g)
    df = _mm(dx3b, w["w_down"], mode="nt", dims=(S, F, D), name="d_f")
    dup, g["conv_b"], g["conv_w"] = _convglu_bwd(df, act_a, act_b, up, w["conv_w"] + sent, "convglu_bwd")
    g["w_up"] = _mm(h2, dup, mode="tn", dims=(D, 2 * F, S), name="dw_up")
    sent = send(("w_up",), g)
    dh2 = _mm(dup, w["w_up"], mode="nt", dims=(S, D, 2 * F), name="d_h2")
    dx2b, g["g_ffn"] = _rms_bwd(dh2, x2, w["g_ffn"] + sent, dx3b, "rms2_bwd", BF16)

    g["w_out"] = _mm(mixed, dx2b, mode="tn", dims=(D, D, S), name="dw_out")
    dmixed = _mm(dx2b, w["w_out"], mode="nt", dims=(S, D, D), name="d_mixed")
    IN = w["w_in"].shape[1]
    dy_both, dproj, g["b_gate"] = _gate_bwd(dmixed, gates, y_pool, y_attn, IN, "gate_bwd")

    g["w_pool_out"] = _mm(pool_out, dy_both, mode="tn", dims=(PW, D, S), name="dw_pool_out")
    g["w_attn_out"] = _mm(attn, dy_both, mode="tn", dims=(GROUP_WIDTH, D, S), b_off=(0, D), name="dw_attn_out")
    sent = send(("w_out", "w_pool_out", "w_attn_out"), g)
    dpool = _mm(dy_both, w["w_pool_out"], mode="nt", dims=(S, PW, D), name="d_pool")
    dattn = _mm(dy_both, w["w_attn_out"], mode="nt", dims=(S, GROUP_WIDTH, D), a_off=(0, D), name="d_attn")

    dproj, g["w_pool_lin"], g["pool_scale"] = _pool_bwd(u, dpool, w["w_pool_lin"], w["pool_scale"] + sent, dproj, "pool_bwd")
    g["loss_cols"] = loss_cols
    sent = send("small", g)

    for gi, (_, d) in enumerate(ATTN_GROUPS):
        dproj = _attn_bwd(qkv, dattn, attn, lse_tot, dproj, d, gi, PW // HEAD_DIM, f"attn_bwd{gi}")

    g["w_in"] = _mm(h1, dproj, mode="tn", dims=(D, IN, S), name="dw_in")
    sent = sent + send(("w_in",), g)
    dh1 = _mm(dproj, w["w_in"], mode="nt", dims=(S, D, IN), tiles=(_tile(S, 1024), _tile(D, 2048), _tile(IN, 2432)), name="d_h1")
    (grad_x, g["g_mix"]) = _rms_bwd(dh1, x, w["g_mix"] + sent, dx2b, "rms1_bwd", F32)
    return loss_cols, grad_x, g


BIG = ("w_in", "w_pool_out", "w_attn_out", "w_out", "w_up", "w_down")
BIG_AXIS = {"w_in": 1, "w_pool_out": 1, "w_attn_out": 1, "w_out": 0, "w_up": 1, "w_down": 0}
GATHER_AXIS = dict(BIG_AXIS, w_pool_lin=1, conv_w=1)
SMALL = ("loss_cols", "b_gate", "w_pool_lin", "pool_scale", "g_ffn", "conv_w", "conv_b", "g_final")
SMALL_COLS = 1024
ORDER = ("g_mix", "w_in", "b_gate", "w_pool_lin", "pool_scale", "w_pool_out", "w_attn_out", "w_out", "g_ffn", "w_up", "conv_w", "conv_b", "w_down", "g_final")


def _as_rows(parts):
    flat = jnp.concatenate([p.astype(F32).reshape(-1) for p in parts])
    rows = -(-flat.shape[0] // (8 * SMALL_COLS)) * 8
    return jnp.pad(flat, (0, rows * SMALL_COLS - flat.shape[0])).reshape(rows, SMALL_COLS)


def kernel(x, g_mix, w_in, b_gate, w_pool_lin, pool_scale, w_pool_out, w_attn_out, w_out, g_ffn, w_up, conv_w, conv_b, w_down, g_final, loss_target, m_g_mix, m_w_in, m_b_gate, m_w_pool_lin, m_pool_scale, m_w_pool_out, m_w_attn_out, m_w_out, m_g_ffn, m_w_up, m_conv_w, m_conv_b, m_w_down, m_g_final, v_g_mix, v_w_in, v_b_gate, v_w_pool_lin, v_pool_scale, v_w_pool_out, v_w_attn_out, v_w_out, v_g_ffn, v_w_up, v_conv_w, v_conv_b, v_w_down, v_g_final):
    shard = dict(g_mix=g_mix, w_in=w_in, b_gate=b_gate, w_pool_lin=w_pool_lin, pool_scale=pool_scale, w_pool_out=w_pool_out, w_attn_out=w_attn_out,
                 w_out=w_out, g_ffn=g_ffn, w_up=w_up, conv_w=conv_w, conv_b=conv_b, w_down=w_down, g_final=g_final)
    mom = dict(g_mix=m_g_mix, w_in=m_w_in, b_gate=m_b_gate, w_pool_lin=m_w_pool_lin, pool_scale=m_pool_scale, w_pool_out=m_w_pool_out, w_attn_out=m_w_attn_out,
               w_out=m_w_out, g_ffn=m_g_ffn, w_up=m_w_up, conv_w=m_conv_w, conv_b=m_conv_b, w_down=m_w_down, g_final=m_g_final)
    vel = dict(g_mix=v_g_mix, w_in=v_w_in, b_gate=v_b_gate, w_pool_lin=v_w_pool_lin, pool_scale=v_pool_scale, w_pool_out=v_w_pool_out, w_attn_out=v_w_attn_out,
               w_out=v_w_out, g_ffn=v_g_ffn, w_up=v_w_up, conv_w=v_conv_w, conv_b=v_conv_b, w_down=v_w_down, g_final=v_g_final)
    chip = 2 * lax.axis_index("x") + lax.axis_index("y")
    pos = jnp.stack([chip, lax.axis_index("c")]).astype(jnp.int32)
    me = (2 * chip + lax.axis_index("c")).astype(jnp.int32).reshape(1)
    D = x.shape[2]

    out_plan, on_plan = _half_slices_plan(False), _half_slices_plan(True)
    in_send, in_recv, in_bufs, in_token = _push_start([_place(shard["w_in"][0], GATHER_AXIS["w_in"], pos, BF16, "place_w_in")], out_plan, 3,
                                                      "comm_gather_w_in_start")
    placed = {k: _place(shard[k][0], GATHER_AXIS[k], pos, F32 if k == "conv_w" else BF16, f"place_{k}", after=in_token)
              for names in LATE_WEIGHTS for k in names}
    late = []

    def late_weights(stage, after):
        if stage == "w_in":
            landed = _push_wait(in_send, in_recv, in_bufs, out_plan, [after] + list(placed.values()), "comm_gather_w_in_wait")
            w_in_full, = _push(list(landed), on_plan, 3, "comm_gather_w_in_pass")
            late_token, prior = 0.0, w_in_full
            for st, names in enumerate(LATE_WEIGHTS):
                plan = _gather_plan([GATHER_AXIS[k] for k in names])
                send_sems, recv_sems, bufs, token = _push_start([placed[k] for k in names], plan, 3 * len(names), f"comm_gather_late{st}_start", after=prior)
                late.append((names, send_sems, recv_sems, bufs, plan))
                late_token, prior = late_token + token[0, 0], token
            return dict(w_in=w_in_full, b_gate=shard["b_gate"] + late_token, late_started=prior)
        names, send_sems, recv_sems, bufs, plan = late[stage]
        return dict(zip(names, _push_wait(send_sems, recv_sems, bufs, plan, after, f"comm_gather_late{stage}_wait")))

    pending = []

    def send(names, g):
        if names == "small":
            bufs = [_as_rows([g[k] for k in SMALL])]
            bufs.append(lax.empty((N_DEV - 1,) + bufs[0].shape, F32))
            plan, tag = _broadcast_plan, "small"
        else:
            bufs = [g[k] for k in names]
            for k in names:
                R, C = g[k].shape
                piece = (R // (2 * N_CHIPS), C) if BIG_AXIS[k] == 0 else (R // 2, C // N_CHIPS)
                bufs.append(lax.empty((N_DEV - 1,) + piece, BF16))
            plan, tag = _scatter_plan([BIG_AXIS[k] for k in names]), names[0]
        ncopies = (N_DEV - 1) * (len(bufs) // 2)
        send_sems, recv_sems, thru, token = _push_start(bufs, plan, ncopies, f"comm_scatter_start_{tag}")
        pending.append((names, send_sems, recv_sems, thru, plan, tag))
        return token[0, 0]

    w0 = dict(g_mix=shard["g_mix"] + in_token[0, 0], pool_scale=shard["pool_scale"], g_ffn=shard["g_ffn"],
              conv_b=shard["conv_b"], g_final=shard["g_final"].reshape(1, D))
    _, grad_x, gr = _local_step(x[0], loss_target[0], w0, late_weights, send)

    halves, small_parts = {}, None
    for names, send_sems, recv_sems, thru, plan, tag in pending:
        done = _push_wait(send_sems, recv_sems, thru, plan, grad_x, f"comm_scatter_wait_{tag}")
        if names == "small":
            small_parts = _sum_small(done[0], done[1], me, "sum_small").reshape(-1)
        else:
            m = len(names)
            for t, k in enumerate(names):
                halves[k] = _sum_pieces(done[t], BIG_AXIS[k], done[m + t], pos, f"sum_{k}")
    g_mix_own = _as_rows([gr["g_mix"]])
    _, g_mix_recv = _push([g_mix_own, lax.empty((N_DEV - 1,) + g_mix_own.shape, F32)], _broadcast_plan, N_DEV - 1, "comm_gather_g_mix")
    g_mix_sum = _sum_small(g_mix_own, g_mix_recv, me, "sum_g_mix").reshape(-1)[:D]
    wholes = _push([halves[k] for k in BIG], _exchange_plan, EXCHANGE_CHUNKS * len(BIG), "comm_exchange_halves")

    grads = {"g_mix": g_mix_sum.reshape(shard["g_mix"].shape)}
    for k, whole in zip(BIG, wholes):
        grads[k] = whole.reshape(shard[k].shape)
    off = 0
    loss = None
    for k in SMALL:
        sz = math.prod(gr[k].shape)
        fullg = small_parts[off:off + sz].reshape(gr[k].shape)
        off += sz
        if k == "loss_cols":
            loss = jnp.sum(fullg)
            continue
        if k in ("w_pool_lin", "conv_w"):
            n = shard[k].shape[2]
            fullg = lax.dynamic_slice_in_dim(fullg, chip * n, n, axis=1)
        grads[k] = fullg.reshape(shard[k].shape)

    deltas, new_m, new_v = {}, {}, {}
    for k in ORDER:
        shp = shard[k].shape
        two_d = (-1, shp[-1])
        dl, nm, nv = _adamw(shard[k].reshape(two_d), grads[k].reshape(two_d), mom[k].reshape(two_d), vel[k].reshape(two_d), f"adamw_{k}")
        deltas[k], new_m[k], new_v[k] = dl.reshape(shp), nm.reshape(shp), nv.reshape(shp)

    return (loss, grad_x[None], *[grads[k] for k in ORDER], *[deltas[k] for k in ORDER], *[new_m[k] for k in ORDER], *[new_v[k] for k in ORDER])
```

```python
import functools
import math

import jax
import jax.numpy as jnp
from jax import lax
from jax.experimental import pallas as pl
from jax.experimental.pallas import tpu as pltpu

F32 = jnp.float32
BF16 = jnp.bfloat16

RMS_EPS = 1e-6
POOL_WINDOWS = (2, 4, 8, 16)
ATTN_GROUPS = ((128, 1), (512, 4), (2048, 16))
HEADS_PER_GROUP = 4
HEAD_DIM = 128
N_ATTN_HEADS = HEADS_PER_GROUP * len(ATTN_GROUPS)
SPAN = 128
GROUP_WIDTH = HEADS_PER_GROUP * HEAD_DIM
ATTN_WIDTH = N_ATTN_HEADS * HEAD_DIM
ATTN_SCALE = HEAD_DIM ** -0.5
NEG_BIG = -1e30
ALIBI_SLOPES = tuple(2.0 ** (-8.0 * (h + 1) / N_ATTN_HEADS) for h in range(N_ATTN_HEADS))

ADAM_LR = 0.001
ADAM_B1 = 0.9
ADAM_B2 = 0.999
ADAM_EPS = 1e-08
ADAM_WD = 0.01
ADAM_STEP = 10

INV_SQRT2 = 1.0 / math.sqrt(2.0)
INV_SQRT_2PI = 1.0 / math.sqrt(2.0 * math.pi)

HALO = 16
VMEM_LIMIT = 56 * 1024 * 1024
N_CHIPS = 4
N_DEV = 8
MESH = pl.DeviceIdType.MESH
ANY = pl.BlockSpec(memory_space=pl.ANY)


def _cparams(*sem):
    return pltpu.CompilerParams(dimension_semantics=sem, vmem_limit_bytes=VMEM_LIMIT)


def _tile(n, pref, mult=128):
    t = (min(pref, n) // mult) * mult
    while t >= mult:
        if n % t == 0:
            return t
        t -= mult
    return n


def _dot(a, b, contract):
    return lax.dot_general(a, b, (contract, ((), ())), preferred_element_type=F32)


def _dot_nn(a, b):
    return _dot(a, b, ((1,), (0,)))


def _dot_nt(a, b):
    return _dot(a, b, ((1,), (1,)))


def _mm(a, b, *, mode, dims, name, tiles=None, out_dtypes=(BF16,), epilogue=None, extras=(), a_off=(0, 0), b_off=(0, 0), after=None):
    M, N, K = dims
    if tiles is None:
        tiles = (_tile(M, 1408), _tile(N, 2816), _tile(K, 512)) if mode == "tn" else (_tile(M, 1024), _tile(N, 1536), _tile(K, 2816))
    tm, tn, tk = tiles
    assert M % tm == 0 and N % tn == 0 and K % tk == 0, (name, dims, tiles)
    nk = K // tk
    if mode == "nn":
        ab, bb, contract = (tm, tk), (tk, tn), ((1,), (0,))
        amap = lambda i, j, k: (i + a_off[0] // tm, k + a_off[1] // tk)
        bmap = lambda i, j, k: (k + b_off[0] // tk, j + b_off[1] // tn)
    elif mode == "nt":
        ab, bb, contract = (tm, tk), (tn, tk), ((1,), (1,))
        amap = lambda i, j, k: (i + a_off[0] // tm, k + a_off[1] // tk)
        bmap = lambda i, j, k: (j + b_off[0] // tn, k + b_off[1] // tk)
    else:
        ab, bb, contract = (tk, tm), (tk, tn), ((0,), (0,))
        amap = lambda i, j, k: (k + a_off[0] // tk, i + a_off[1] // tm)
        bmap = lambda i, j, k: (k + b_off[0] // tk, j + b_off[1] // tn)
    assert a_off[0] % ab[0] == 0 and a_off[1] % ab[1] == 0 and b_off[0] % bb[0] == 0 and b_off[1] % bb[1] == 0, name
    in_specs = [pl.BlockSpec(ab, amap), pl.BlockSpec(bb, bmap)]
    ex_arrays = []
    for arr, kind, off in extras:
        if kind == "mn":
            assert off[0] % tm == 0 and off[1] % tn == 0, name
            in_specs.append(pl.BlockSpec((tm, tn), lambda i, j, k, off=off: (i + off[0] // tm, j + off[1] // tn)))
        else:
            assert off[1] % tn == 0, name
            in_specs.append(pl.BlockSpec((1, tn), lambda i, j, k, off=off: (0, j + off[1] // tn)))
        ex_arrays.append(arr)
    ne, no = len(ex_arrays), len(out_dtypes)
    if after is not None:
        in_specs.append(ANY)
        ex_arrays.append(after)
    first_out = 2 + len(ex_arrays)
    if epilogue is None:
        def epilogue(acc, ex, outs):
            outs[0][...] = acc.astype(outs[0].dtype)

    def body(*refs):
        a_ref, b_ref = refs[0], refs[1]
        ex, outs = refs[2:2 + ne], refs[first_out:first_out + no]
        if nk == 1:
            epilogue(_dot(a_ref[...], b_ref[...], contract), ex, outs)
            return
        acc = refs[-1]
        k = pl.program_id(2)
        if nk <= 4:
            part = _dot(a_ref[...], b_ref[...], contract)

            @pl.when(k == 0)
            def _():
                acc[...] = part

            @pl.when(jnp.logical_and(k > 0, k < nk - 1))
            def _():
                acc[...] += part

            @pl.when(k == nk - 1)
            def _():
                epilogue(acc[...] + part, ex, outs)
        else:
            @pl.when(k == 0)
            def _():
                acc[...] = _dot(a_ref[...], b_ref[...], contract)

            @pl.when(k > 0)
            def _():
                acc[...] += _dot(a_ref[...], b_ref[...], contract)

            @pl.when(k == nk - 1)
            def _():
                epilogue(acc[...], ex, outs)

    res = pl.pallas_call(
        body,
        grid=(M // tm, N // tn, nk),
        in_specs=in_specs,
        out_specs=[pl.BlockSpec((tm, tn), lambda i, j, k: (i, j)) for _ in out_dtypes],
        out_shape=[jax.ShapeDtypeStruct((M, N), dt) for dt in out_dtypes],
        scratch_shapes=[pltpu.VMEM((tm, tn), F32)] if nk > 1 else [],
        compiler_params=_cparams("parallel", "parallel", "arbitrary"),
        name=name,
    )(a, b, *ex_arrays)
    return res[0] if no == 1 else res


def _rms_fwd(x, g, name):
    S, D = x.shape
    tm = _tile(S, 256)

    def body(x_ref, g_ref, h_ref):
        xv = x_ref[...]
        r = lax.rsqrt(jnp.mean(xv * xv, axis=-1, keepdims=True) + RMS_EPS)
        h_ref[...] = (xv * r * g_ref[...]).astype(h_ref.dtype)

    return pl.pallas_call(
        body,
        grid=(S // tm,),
        in_specs=[pl.BlockSpec((tm, D), lambda i: (i, 0)), pl.BlockSpec((1, D), lambda i: (0, 0))],
        out_specs=pl.BlockSpec((tm, D), lambda i: (i, 0)),
        out_shape=jax.ShapeDtypeStruct((S, D), BF16),
        compiler_params=_cparams("parallel"),
        name=name,
    )(x, g)


def _rms_bwd(dh, x, g, dres, name, out_dtype):
    S, D = x.shape
    tm = _tile(S, 256)

    def body(dh_ref, x_ref, g_ref, dres_ref, dx_ref, dg_ref):
        xv = x_ref[...]
        r = lax.rsqrt(jnp.mean(xv * xv, axis=-1, keepdims=True) + RMS_EPS)
        xr = xv * r
        dhv = dh_ref[...].astype(F32)

        @pl.when(pl.program_id(0) == 0)
        def _():
            dg_ref[...] = jnp.zeros_like(dg_ref)

        dg_ref[...] += jnp.sum(dhv * xr, axis=0, keepdims=True)
        u = dhv * g_ref[...]
        c = jnp.mean(u * xr, axis=-1, keepdims=True)
        dx_ref[...] = (dres_ref[...].astype(F32) + r * (u - xr * c)).astype(dx_ref.dtype)

    row = pl.BlockSpec((tm, D), lambda i: (i, 0))
    vec = pl.BlockSpec((1, D), lambda i: (0, 0))
    return pl.pallas_call(
        body,
        grid=(S // tm,),
        in_specs=[row, row, vec, row],
        out_specs=[row, vec],
        out_shape=[jax.ShapeDtypeStruct((S, D), out_dtype), jax.ShapeDtypeStruct((1, D), F32)],
        compiler_params=_cparams("arbitrary"),
        name=name,
    )(dh, x, g, dres)


def _loss_head(x3, tgt, g, name):
    S, D = x3.shape
    tm = _tile(S, 256)

    def body(x_ref, t_ref, g_ref, dxb_ref, dg_ref, loss_ref):
        xv = x_ref[...]
        gv = g_ref[...]
        r = lax.rsqrt(jnp.mean(xv * xv, axis=-1, keepdims=True) + RMS_EPS)
        xr = xv * r
        e = xr * gv - t_ref[...]

        @pl.when(pl.program_id(0) == 0)
        def _():
            dg_ref[...] = jnp.zeros_like(dg_ref)
            loss_ref[...] = jnp.zeros_like(loss_ref)

        loss_ref[...] += jnp.sum(e * e, axis=0, keepdims=True) * (0.5 / D)
        dy = e * (1.0 / D)
        dg_ref[...] += jnp.sum(dy * xr, axis=0, keepdims=True)
        u = dy * gv
        c = jnp.mean(u * xr, axis=-1, keepdims=True)
        dxb_ref[...] = (r * (u - xr * c)).astype(BF16)

    row = pl.BlockSpec((tm, D), lambda i: (i, 0))
    vec = pl.BlockSpec((1, D), lambda i: (0, 0))
    return pl.pallas_call(
        body,
        grid=(S // tm,),
        in_specs=[row, row, vec],
        out_specs=[row, vec, vec],
        out_shape=[jax.ShapeDtypeStruct((S, D), BF16), jax.ShapeDtypeStruct((1, D), F32), jax.ShapeDtypeStruct((1, D), F32)],
        compiler_params=_cparams("arbitrary"),
        name=name,
    )(x3, tgt, g)


SHIFT_ROWS = 256


def _shift_matrix():
    i = lax.broadcasted_iota(jnp.int32, (2 * SHIFT_ROWS, SHIFT_ROWS), 0)
    j = lax.broadcasted_iota(jnp.int32, (2 * SHIFT_ROWS, SHIFT_ROWS), 1)
    src = jnp.where(i < SHIFT_ROWS, i - 1, i - SHIFT_ROWS - 2)
    return (j == src).astype(BF16)


def _conv_taps(cur_ref, halo_ref, w_ref, b_ref, first, shift):
    w, bias = w_ref[...], b_ref[...]
    before = jnp.where(first, 0.0, halo_ref[...].astype(F32)[HALO - 8:])
    sub = lax.broadcasted_iota(jnp.int32, before.shape, 0)
    out = []
    for blk in range(cur_ref.shape[0] // SHIFT_ROWS):
        xb = cur_ref[blk * SHIFT_ROWS:(blk + 1) * SHIFT_ROWS, :]
        xf = xb.astype(F32)
        both = _dot_nn(shift, xb)
        p1, p2 = both[:SHIFT_ROWS], both[SHIFT_ROWS:]
        p1 = jnp.concatenate([jnp.where(sub == 0, pltpu.roll(before, 1, 0), p1[:8]), p1[8:]], axis=0)
        p2 = jnp.concatenate([jnp.where(sub < 2, pltpu.roll(before, 2, 0), p2[:8]), p2[8:]], axis=0)
        out.append(bias + w[0:1] * p2 + w[1:2] * p1 + w[2:3] * xf)
        before = xf[SHIFT_ROWS - 8:]
    return jnp.concatenate(out, axis=0)


def _convglu_fwd(up, cw, cb, name):
    S, F2 = up.shape
    F = F2 // 2
    tm, tn = _tile(S, 512), _tile(F, 512)
    nj, hb = F // tn, tm // HALO

    def body(ua, ub, ha, hb_, wa, wb, ba, bb, f_ref, a_ref, b_ref):
        first = pl.program_id(0) == 0
        shift = _shift_matrix()
        a = _conv_taps(ua, ha, wa, ba, first, shift)
        b = _conv_taps(ub, hb_, wb, bb, first, shift)
        f_ref[...] = (0.5 * a * (1.0 + lax.erf(a * INV_SQRT2)) * b).astype(f_ref.dtype)
        a_ref[...] = a.astype(a_ref.dtype)
        b_ref[...] = b.astype(b_ref.dtype)

    tile = pl.BlockSpec((tm, tn), lambda i, j: (i, j))
    return pl.pallas_call(
        body,
        grid=(S // tm, nj),
        in_specs=[
            tile,
            pl.BlockSpec((tm, tn), lambda i, j: (i, j + nj)),
            pl.BlockSpec((HALO, tn), lambda i, j: (jnp.maximum(i * hb - 1, 0), j)),
            pl.BlockSpec((HALO, tn), lambda i, j: (jnp.maximum(i * hb - 1, 0), j + nj)),
            pl.BlockSpec((3, tn), lambda i, j: (0, j)),
            pl.BlockSpec((3, tn), lambda i, j: (0, j + nj)),
            pl.BlockSpec((1, tn), lambda i, j: (0, j)),
            pl.BlockSpec((1, tn), lambda i, j: (0, j + nj)),
        ],
        out_specs=[tile, tile, tile],
        out_shape=[jax.ShapeDtypeStruct((S, F), BF16)] * 3,
        compiler_params=_cparams("parallel", "parallel"),
        name=name,
    )(up, up, up, up, cw, cw, cb, cb)


def _convglu_bwd(df, a, b, up, cw, name):
    S, F = df.shape
    tm, tn = _tile(S, 512), _tile(F, 512)
    nj, ni, hb = F // tn, S // tm, tm // HALO
    n = tm + HALO

    def body(df_ref, dfn_ref, a_ref, an_ref, b_ref, bn_ref, up_ref, w_ref, o_ref, db_ref, dw_ref):
        j, i = pl.program_id(0), pl.program_id(1)
        last = i == ni - 1

        def rows(c_ref, n_ref):
            return jnp.concatenate([c_ref[...].astype(F32), jnp.where(last, 0.0, n_ref[...].astype(F32))], axis=0)

        @pl.when(i == 0)
        def _():
            db_ref[...] = jnp.zeros_like(db_ref)
            dw_ref[...] = jnp.zeros_like(dw_ref)

        def finish(d):
            i_ = lax.broadcasted_iota(jnp.int32, (2 * SHIFT_ROWS, SHIFT_ROWS), 0)
            j_ = lax.broadcasted_iota(jnp.int32, (2 * SHIFT_ROWS, SHIFT_ROWS), 1)
            ahead = (j_ == jnp.where(i_ < SHIFT_ROWS, i_ + 1, i_ - SHIFT_ROWS + 2)).astype(BF16)
            db = d.astype(BF16)
            sub = lax.broadcasted_iota(jnp.int32, (8, tn), 0)
            d1, d2 = [], []
            for blk in range(tm // SHIFT_ROWS):
                lo, hi = blk * SHIFT_ROWS, (blk + 1) * SHIFT_ROWS
                both = _dot_nn(ahead, db[lo:hi])
                n1, n2 = both[:SHIFT_ROWS], both[SHIFT_ROWS:]
                after = db[hi:hi + HALO].astype(F32)[:8]
                d1 += [n1[:-8], jnp.where(sub == 7, pltpu.roll(after, 7, 0), n1[-8:])]
                d2 += [n2[:-8], jnp.where(sub >= 6, pltpu.roll(after, 6, 0), n2[-8:])]
            d0, d1, d2 = d[:tm], jnp.concatenate(d1, axis=0), jnp.concatenate(d2, axis=0)
            w = w_ref[...]
            o_ref[...] = (w[2:3] * d0 + w[1:2] * d1 + w[0:1] * d2).astype(o_ref.dtype)
            upv = up_ref[...].astype(F32)
            db_ref[...] += jnp.sum(d0, axis=0, keepdims=True)
            dw_ref[0:1, :] += jnp.sum(d2 * upv, axis=0, keepdims=True)
            dw_ref[1:2, :] += jnp.sum(d1 * upv, axis=0, keepdims=True)
            dw_ref[2:3, :] += jnp.sum(d0 * upv, axis=0, keepdims=True)

        av, dfv = rows(a_ref, an_ref), rows(df_ref, dfn_ref)
        cdf = 0.5 * (1.0 + lax.erf(av * INV_SQRT2))

        @pl.when(j < nj)
        def _():
            pdf = jnp.exp(-0.5 * av * av) * INV_SQRT_2PI
            finish(dfv * rows(b_ref, bn_ref) * (cdf + av * pdf))

        @pl.when(j >= nj)
        def _():
            finish(dfv * (av * cdf))

    jh = lambda j: lax.rem(j, nj)
    nxt = lambda i: jnp.minimum((i + 1) * hb, S // HALO - 1)
    cur = pl.BlockSpec((tm, tn), lambda j, i: (i, jh(j)))
    halo = pl.BlockSpec((HALO, tn), lambda j, i: (nxt(i), jh(j)))
    return pl.pallas_call(
        body,
        grid=(2 * nj, ni),
        in_specs=[cur, halo, cur, halo, cur, halo, pl.BlockSpec((tm, tn), lambda j, i: (i, j)), pl.BlockSpec((3, tn), lambda j, i: (0, j))],
        out_specs=[pl.BlockSpec((tm, tn), lambda j, i: (i, j)), pl.BlockSpec((1, tn), lambda j, i: (0, j)), pl.BlockSpec((3, tn), lambda j, i: (0, j))],
        out_shape=[jax.ShapeDtypeStruct((S, 2 * F), BF16), jax.ShapeDtypeStruct((1, 2 * F), F32), jax.ShapeDtypeStruct((3, 2 * F), F32)],
        compiler_params=_cparams("parallel", "arbitrary"),
        name=name,
    )(df, df, a, a, b, b, up, cw)


def _gate_bwd(dmixed, gates, y_pool, y_attn, in_width, name):
    S, D = dmixed.shape
    tm, tn = _tile(S, 512), _tile(D, 512)
    nj = D // tn
    pre0 = (in_width - 2 * D) // tn
    assert pre0 * tn == in_width - 2 * D

    def body(dm_ref, g_ref, yp_ref, ya_ref, dy_ref, dpre_ref, db_ref):
        j = pl.program_id(0)

        @pl.when(pl.program_id(1) == 0)
        def _():
            db_ref[...] = jnp.zeros_like(db_ref)

        def run(y_ref):
            dm = dm_ref[...].astype(F32)
            gv = g_ref[...].astype(F32)
            dy_ref[...] = (dm * gv).astype(BF16)
            dpre = dm * y_ref[...].astype(F32) * gv * (1.0 - gv)
            dpre_ref[...] = dpre.astype(BF16)
            db_ref[...] += jnp.sum(dpre, axis=0, keepdims=True)

        @pl.when(j < nj)
        def _():
            run(yp_ref)

        @pl.when(j >= nj)
        def _():
            run(ya_ref)

    tile2 = pl.BlockSpec((tm, tn), lambda j, i: (i, j))
    return pl.pallas_call(
        body,
        grid=(2 * nj, S // tm),
        in_specs=[
            pl.BlockSpec((tm, tn), lambda j, i: (i, lax.rem(j, nj))),
            tile2,
            pl.BlockSpec((tm, tn), lambda j, i: (i, jnp.minimum(j, nj - 1))),
            pl.BlockSpec((tm, tn), lambda j, i: (i, jnp.maximum(j - nj, 0))),
        ],
        out_specs=[tile2, pl.BlockSpec((tm, tn), lambda j, i: (i, pre0 + j)), pl.BlockSpec((1, tn), lambda j, i: (0, j))],
        out_shape=[jax.ShapeDtypeStruct((S, 2 * D), BF16), jax.ShapeDtypeStruct((S, in_width), BF16), jax.ShapeDtypeStruct((1, 2 * D), F32)],
        compiler_params=_cparams("parallel", "arbitrary"),
        name=name,
    )(dmixed, gates, y_pool, y_attn)


def _pool_counts(i, tm, rows, w):
    t = i * tm + lax.broadcasted_iota(jnp.int32, (rows, 1), 0)
    return jnp.minimum(t + 1, w).astype(F32)


def _pooled_groups(u_ref, uh_ref, i, tm, C):
    cur = u_ref[...]
    halo = jnp.where(i == 0, 0.0, uh_ref[...])
    xx = jnp.concatenate([halo, cur], axis=0)
    out = []
    s = xx
    for gi, w in enumerate(POOL_WINDOWS):
        s = s + pltpu.roll(s, w // 2, 0)
        tot = s[HALO:, 0:C]
        out.append(tot / _pool_counts(i, tm, tm, w) - cur[:, gi * C:(gi + 1) * C])
        s = s[:, C:] if gi + 1 < len(POOL_WINDOWS) else s
    return out


def _pool_fwd(u, wl, scale, name):
    S, PW = u.shape
    C = PW // len(POOL_WINDOWS)
    tm = _tile(S, 512)
    hb = tm // HALO

    def body(u_ref, uh_ref, wl_ref, sc_ref, o_ref):
        i = pl.program_id(0)
        pooled = _pooled_groups(u_ref, uh_ref, i, tm, C)
        for gi in range(len(POOL_WINDOWS)):
            y = _dot_nn(pooled[gi].astype(BF16), wl_ref[gi])
            o_ref[:, gi * C:(gi + 1) * C] = (y * sc_ref[:, gi * C:(gi + 1) * C]).astype(o_ref.dtype)

    return pl.pallas_call(
        body,
        grid=(S // tm,),
        in_specs=[
            pl.BlockSpec((tm, PW), lambda i: (i, 0)),
            pl.BlockSpec((HALO, PW), lambda i: (jnp.maximum(i * hb - 1, 0), 0)),
            pl.BlockSpec((len(POOL_WINDOWS), C, C), lambda i: (0, 0, 0)),
            pl.BlockSpec((1, PW), lambda i: (0, 0)),
        ],
        out_specs=pl.BlockSpec((tm, PW), lambda i: (i, 0)),
        out_shape=jax.ShapeDtypeStruct((S, PW), BF16),
        compiler_params=_cparams("parallel"),
        name=name,
    )(u, u, wl, scale)


def _pool_bwd(u, dp, wl, scale, dproj, name):
    S, PW = u.shape
    G = len(POOL_WINDOWS)
    C = PW // G
    tm = _tile(S, 512)
    hb, ni = tm // HALO, S // tm
    n = tm + HALO

    def body(u_ref, uh_ref, dp_ref, dpn_ref, wl_ref, sc_ref, _, du_ref, dwl_ref, dsc_ref):
        i = pl.program_id(0)

        @pl.when(i == 0)
        def _():
            dwl_ref[...] = jnp.zeros_like(dwl_ref)
            dsc_ref[...] = jnp.zeros_like(dsc_ref)

        pooled = _pooled_groups(u_ref, uh_ref, i, tm, C)
        dpc = dp_ref[...].astype(F32)
        dpn = jnp.where(i == ni - 1, 0.0, dpn_ref[...].astype(F32))
        sc = sc_ref[...]
        dyl = jnp.concatenate([dpc, dpn], axis=0) * sc
        for gi, w in enumerate(POOL_WINDOWS):
            cols = slice(gi * C, (gi + 1) * C)
            pb = pooled[gi].astype(BF16)
            ylin = _dot_nn(pb, wl_ref[gi])
            dsc_ref[:, cols] += jnp.sum(dpc[:, cols] * ylin, axis=0, keepdims=True)
            dylg = dyl[:, cols].astype(BF16)
            dwl_ref[gi] += _dot(pb, dylg[:tm], ((0,), (0,)))
            dpool = _dot_nt(dylg, wl_ref[gi])
            e = dpool / _pool_counts(i, tm, n, w)
            k = 1
            while k < w:
                e = e + pltpu.roll(e, n - k, 0)
                k *= 2
            du_ref[:, cols] = (e[:tm] - dpool[:tm]).astype(du_ref.dtype)

    return pl.pallas_call(
        body,
        grid=(ni,),
        in_specs=[
            pl.BlockSpec((tm, PW), lambda i: (i, 0)),
            pl.BlockSpec((HALO, PW), lambda i: (jnp.maximum(i * hb - 1, 0), 0)),
            pl.BlockSpec((tm, PW), lambda i: (i, 0)),
            pl.BlockSpec((HALO, PW), lambda i: (jnp.minimum((i + 1) * hb, S // HALO - 1), 0)),
            pl.BlockSpec((G, C, C), lambda i: (0, 0, 0)),
            pl.BlockSpec((1, PW), lambda i: (0, 0)),
            ANY,
        ],
        out_specs=[pl.BlockSpec((tm, PW), lambda i: (i, 0)), pl.BlockSpec((G, C, C), lambda i: (0, 0, 0)), pl.BlockSpec((1, PW), lambda i: (0, 0))],
        out_shape=[jax.ShapeDtypeStruct(dproj.shape, dproj.dtype), jax.ShapeDtypeStruct((G, C, C), F32), jax.ShapeDtypeStruct((1, PW), F32)],
        input_output_aliases={6: 0},
        compiler_params=_cparams("arbitrary"),
        name=name,
    )(u, u, dp, dp, wl, scale, dproj)


def _band_masks():
    ii = lax.broadcasted_iota(jnp.int32, (SPAN, SPAN), 0)
    kk = lax.broadcasted_iota(jnp.int32, (SPAN, SPAN), 1)
    return ((ii + SPAN - kk).astype(F32), kk >= ii), ((ii - kk).astype(F32), kk <= ii)


ATTN_TILE = 16 * SPAN


def _unit_rows(r, b, d, blocks=1):
    return pl.ds(d * SPAN * b + r, blocks * SPAN, stride=d) if d > 1 else pl.ds(SPAN * b, blocks * SPAN)


def _f32_copies(refs, scratch, d):
    if d == 1:
        return list(refs)
    for ref, s in zip(refs, scratch):
        s[...] = ref[...].astype(F32)
    return list(scratch)


def _attn_fwd(qkv, d, g, name):
    S = qkv.shape[0]
    T = min(ATTN_TILE, S)
    P = SPAN * d
    nbk = T // P

    def body(q_ref, k_ref, v_ref, kp_ref, vp_ref, o_ref, lse_ref, *scratch):
        c = pl.program_id(0)
        (jp, mp), (jc, mc) = _band_masks()
        slopes = [ALIBI_SLOPES[g * HEADS_PER_GROUP + h] * d for h in range(HEADS_PER_GROUP)]
        slope = slopes[0]
        for h in range(1, HEADS_PER_GROUP):
            slope = jnp.where(pl.program_id(1) == h, slopes[h], slope)
        q_s, k_s, v_s, kp_s, vp_s = _f32_copies((q_ref, k_ref, v_ref, kp_ref, vp_ref), scratch[:5], d)
        o_s, l_s = (o_ref, lse_ref) if d == 1 else scratch[5:7]
        bias_p, bias_c = jnp.where(mp, -slope * jp, NEG_BIG), jnp.where(mc, -slope * jc, NEG_BIG)
        bias = jnp.concatenate([bias_p, bias_c], axis=1)
        bias_first = jnp.concatenate([jnp.where(c > 0, bias_p, NEG_BIG), bias_c], axis=1)
        for r in range(d):
            for b in range(nbk):
                rows = _unit_rows(r, b, d)
                q = q_s[rows, :].astype(BF16)
                if b == 0:
                    prev = _unit_rows(r, 0, d)
                    kk = jnp.concatenate([kp_s[prev, :], k_s[rows, :]], axis=0).astype(BF16)
                    vv = jnp.concatenate([vp_s[prev, :], v_s[rows, :]], axis=0).astype(BF16)
                else:
                    both = _unit_rows(r, b - 1, d, 2)
                    kk, vv = k_s[both, :].astype(BF16), v_s[both, :].astype(BF16)
                s = _dot_nt(q, kk) * ATTN_SCALE + (bias_first if b == 0 else bias)
                m = jnp.max(s, axis=-1, keepdims=True)
                p = jnp.exp(s - m)
                l = jnp.sum(p, axis=-1, keepdims=True)
                o_s[rows, :] = _dot_nn(p.astype(BF16), vv) / l
                l_s[rows, :] = jnp.broadcast_to(m + jnp.log(l), (SPAN, HEAD_DIM))
        if d > 1:
            o_ref[...] = o_s[...]
            lse_ref[...] = l_s[...]

    col = lambda kind: (lambda c, h: (c, kind * N_ATTN_HEADS + g * HEADS_PER_GROUP + h))
    pcol = lambda kind: (lambda c, h: (jnp.maximum(c * nbk - 1, 0), kind * N_ATTN_HEADS + g * HEADS_PER_GROUP + h))
    cur = lambda kind: pl.BlockSpec((T, HEAD_DIM), col(kind))
    prv = lambda kind: pl.BlockSpec((P, HEAD_DIM), pcol(kind))
    out = pl.BlockSpec((T, HEAD_DIM), lambda c, h: (c, h))
    scratch = [] if d == 1 else [pltpu.VMEM((T, HEAD_DIM), F32)] * 3 + [pltpu.VMEM((P, HEAD_DIM), F32)] * 2 + [pltpu.VMEM((T, HEAD_DIM), F32)] * 2
    return pl.pallas_call(
        body,
        grid=(S // T, HEADS_PER_GROUP),
        in_specs=[cur(0), cur(1), cur(2), prv(1), prv(2)],
        out_specs=[out, out],
        out_shape=[jax.ShapeDtypeStruct((S, GROUP_WIDTH), F32)] * 2,
        scratch_shapes=scratch,
        compiler_params=_cparams("parallel", "parallel"),
        name=name,
    )(qkv, qkv, qkv, qkv, qkv)


def _attn_merge(os_, lses, name):
    S, W = os_[0].shape
    tm = _tile(S, 512)

    def body(o0, o1, o2, l0, l1, l2, y_ref, lse_ref):
        ls = [l0[...], l1[...], l2[...]]
        m = jnp.maximum(jnp.maximum(ls[0], ls[1]), ls[2])
        es = [jnp.exp(v - m) for v in ls]
        tot = es[0] + es[1] + es[2]
        y = (es[0] * o0[...] + es[1] * o1[...] + es[2] * o2[...]) / tot
        y_ref[...] = y.astype(y_ref.dtype)
        lse_ref[...] = m + jnp.log(tot)

    row = pl.BlockSpec((tm, W), lambda i: (i, 0))
    return pl.pallas_call(
        body,
        grid=(S // tm,),
        in_specs=[row] * 6,
        out_specs=[row, row],
        out_shape=[jax.ShapeDtypeStruct((S, W), BF16), jax.ShapeDtypeStruct((S, W), F32)],
        compiler_params=_cparams("parallel"),
        name=name,
    )(*os_, *lses)


def _attn_bwd(qkv, dattn, y, lse, dproj, d, g, col0, name):
    S = qkv.shape[0]
    T = min(ATTN_TILE, S)
    P = SPAN * d
    nbk = T // P
    ntile = S // T

    def body(q_ref, k_ref, v_ref, kp_ref, vp_ref, qn_ref, da_ref, dan_ref, y_ref, yn_ref, lse_ref, lsen_ref, _, out_ref, dq_s, dk_s, dv_s, *scratch):
        c = pl.program_id(0)
        head_id = pl.program_id(1)
        kind = pl.program_id(2)

        @pl.when(kind == 0)
        def _():
            (jp, mp), (jc, mc) = _band_masks()
            slopes = [ALIBI_SLOPES[g * HEADS_PER_GROUP + h] * d for h in range(HEADS_PER_GROUP)]
            slope = slopes[0]
            for h in range(1, HEADS_PER_GROUP):
                slope = jnp.where(head_id == h, slopes[h], slope)
            q_s, k_s, v_s, da_s, y_s, kp_s, vp_s, qn_s, dan_s, yn_s = _f32_copies(
                (q_ref, k_ref, v_ref, da_ref, y_ref, kp_ref, vp_ref, qn_ref, dan_ref, yn_ref), scratch, d)
            bias_p, bias_c = jnp.where(mp, -slope * jp, NEG_BIG), jnp.where(mc, -slope * jc, NEG_BIG)
            bias = jnp.concatenate([bias_c, bias_p], axis=0)
            bias_last = jnp.concatenate([bias_c, jnp.where(c < ntile - 1, bias_p, NEG_BIG)], axis=0)
            bias_first = jnp.where(c > 0, bias_p, NEG_BIG)

            def pair(q, da, yy, lse_blk, kk, vv, b):
                dd = jnp.sum(da.astype(F32) * yy.astype(F32), axis=-1, keepdims=True)
                p = jnp.exp(_dot_nt(q, kk) * ATTN_SCALE + b - lse_blk[:, 0:1])
                return p, p * (_dot_nt(da, vv) - dd)

            for r in range(d):
                first = _unit_rows(r, 0, d)
                kk, vv = kp_s[first, :].astype(BF16), vp_s[first, :].astype(BF16)
                _, ds = pair(q_s[first, :].astype(BF16), da_s[first, :].astype(BF16), y_s[first, :], lse_ref[first, :], kk, vv, bias_first)
                dq_next = _dot_nn(ds.astype(BF16), kk)
                for kb in range(nbk):
                    rows = _unit_rows(r, kb, d)
                    if kb + 1 < nbk:
                        both = _unit_rows(r, kb, d, 2)
                        q, da, yy, lse_blk = q_s[both, :], da_s[both, :], y_s[both, :], lse_ref[both, :]
                    else:
                        q = jnp.concatenate([q_s[rows, :], qn_s[first, :]], axis=0)
                        da = jnp.concatenate([da_s[rows, :], dan_s[first, :]], axis=0)
                        yy = jnp.concatenate([y_s[rows, :], yn_s[first, :]], axis=0)
                        lse_blk = jnp.concatenate([lse_ref[rows, :], lsen_ref[first, :]], axis=0)
                    q, da = q.astype(BF16), da.astype(BF16)
                    kk, vv = k_s[rows, :].astype(BF16), v_s[rows, :].astype(BF16)
                    p, ds = pair(q, da, yy, lse_blk, kk, vv, bias if kb + 1 < nbk else bias_last)
                    dv_s[rows, :] = _dot_nn(p.T.astype(BF16), da)
                    dk_s[rows, :] = _dot_nn(ds.T.astype(BF16), q) * ATTN_SCALE
                    dq_both = _dot_nn(ds.astype(BF16), kk)
                    dq_s[rows, :] = (dq_next + dq_both[:SPAN]) * ATTN_SCALE
                    dq_next = dq_both[SPAN:]
            out_ref[...] = dq_s[...].astype(out_ref.dtype)

        @pl.when(kind == 1)
        def _():
            out_ref[...] = dk_s[...].astype(out_ref.dtype)

        @pl.when(kind == 2)
        def _():
            out_ref[...] = dv_s[...].astype(out_ref.dtype)

    head = lambda h: g * HEADS_PER_GROUP + h
    cur = lambda kind: pl.BlockSpec((T, HEAD_DIM), lambda c, h, kd: (c, kind * N_ATTN_HEADS + head(h)))
    prv = lambda kind: pl.BlockSpec((P, HEAD_DIM), lambda c, h, kd: (jnp.maximum(c * nbk - 1, 0), kind * N_ATTN_HEADS + head(h)))
    nxt_row = lambda c: jnp.minimum((c + 1) * nbk, S // P - 1)
    qnext = pl.BlockSpec((P, HEAD_DIM), lambda c, h, kd: (nxt_row(c), head(h)))
    hcur = pl.BlockSpec((T, HEAD_DIM), lambda c, h, kd: (c, h))
    hnext = pl.BlockSpec((P, HEAD_DIM), lambda c, h, kd: (nxt_row(c), h))
    out = pl.BlockSpec((T, HEAD_DIM), lambda c, h, kd: (c, col0 + kd * N_ATTN_HEADS + head(h)))
    stage = [pltpu.VMEM((T, HEAD_DIM), F32)] * 3
    copies = [] if d == 1 else [pltpu.VMEM((T, HEAD_DIM), F32)] * 5 + [pltpu.VMEM((P, HEAD_DIM), F32)] * 5
    return pl.pallas_call(
        body,
        grid=(ntile, HEADS_PER_GROUP, 3),
        in_specs=[cur(0), cur(1), cur(2), prv(1), prv(2), qnext, hcur, hnext, hcur, hnext, hcur, hnext, ANY],
        out_specs=out,
        out_shape=jax.ShapeDtypeStruct(dproj.shape, dproj.dtype),
        input_output_aliases={12: 0},
        scratch_shapes=stage + copies,
        compiler_params=_cparams("parallel", "parallel", "arbitrary"),
        name=name,
    )(qkv, qkv, qkv, qkv, qkv, qkv, dattn, dattn, y, y, lse, lse, dproj)


def _row_block(R, C, bytes_per_row_elem=4, budget=1 << 20):
    if R % 8:
        return R
    best = 8
    t = 8
    while t <= R:
        if R % t == 0 and t * C * bytes_per_row_elem <= budget:
            best = t
        t += 8
    return best


def _adamw(w, g, m, v, name):
    R, C = w.shape
    tr = _row_block(R, C)
    c1 = 1.0 - ADAM_B1 ** ADAM_STEP
    c2 = 1.0 - ADAM_B2 ** ADAM_STEP

    def body(w_ref, g_ref, m_ref, v_ref, d_ref, nm_ref, nv_ref):
        gv = g_ref[...]
        nm = ADAM_B1 * m_ref[...] + (1.0 - ADAM_B1) * gv
        nv = ADAM_B2 * v_ref[...] + (1.0 - ADAM_B2) * (gv * gv)
        d_ref[...] = -ADAM_LR * ((nm / c1) / (jnp.sqrt(nv / c2) + ADAM_EPS) + ADAM_WD * w_ref[...])
        nm_ref[...] = nm
        nv_ref[...] = nv

    blk = pl.BlockSpec((tr, C), lambda i: (i, 0))
    return pl.pallas_call(
        body,
        grid=(R // tr,),
        in_specs=[blk] * 4,
        out_specs=[blk] * 3,
        out_shape=[jax.ShapeDtypeStruct((R, C), F32)] * 3,
        compiler_params=_cparams("parallel"),
        name=name,
    )(w, g, m, v)


def _sum_pieces(grad, axis, recv, pos, name):
    n, pr, pc = recv.shape
    tr = _row_block(pr, pc, bytes_per_row_elem=(n + 1) * recv.dtype.itemsize, budget=4 << 20)
    nblk = pr // tr
    if axis == 1:
        own_map = lambda i, p: (p[1] * nblk + i, p[0])
    else:
        own_map = lambda i, p: ((2 * p[0] + p[1]) * nblk + i, 0)

    def body(p_ref, own_ref, r_ref, o_ref):
        acc = own_ref[...].astype(F32)
        for s in range(n):
            acc = acc + r_ref[s].astype(F32)
        o_ref[...] = acc

    return pl.pallas_call(
        body,
        grid_spec=pltpu.PrefetchScalarGridSpec(
            num_scalar_prefetch=1,
            grid=(nblk,),
            in_specs=[pl.BlockSpec((tr, pc), own_map), pl.BlockSpec((n, tr, pc), lambda i, p: (0, i, 0))],
            out_specs=pl.BlockSpec((tr, pc), lambda i, p: (p[1] * nblk + i, 0)),
        ),
        out_shape=jax.ShapeDtypeStruct((2 * pr, pc), F32),
        compiler_params=_cparams("parallel"),
        name=name,
    )(pos, grad, recv)


def _sum_small(own, recv, me, name):
    n, R, C = recv.shape
    tr = _row_block(R, C, bytes_per_row_elem=(n + 1) * 4, budget=4 << 20)

    def body(me_ref, own_ref, r_ref, o_ref):
        acc = None
        for dev in range(n + 1):
            k = jnp.bitwise_xor(me_ref[0], dev)
            term = jnp.where(k == 0, own_ref[...], r_ref[jnp.maximum(k - 1, 0)])
            acc = term if acc is None else acc + term
        o_ref[...] = acc

    return pl.pallas_call(
        body,
        grid_spec=pltpu.PrefetchScalarGridSpec(
            num_scalar_prefetch=1,
            grid=(R // tr,),
            in_specs=[pl.BlockSpec((tr, C), lambda i, m: (i, 0)), pl.BlockSpec((n, tr, C), lambda i, m: (0, i, 0))],
            out_specs=pl.BlockSpec((tr, C), lambda i, m: (i, 0)),
        ),
        out_shape=jax.ShapeDtypeStruct((R, C), F32),
        compiler_params=_cparams("parallel"),
        name=name,
    )(me, own, recv)


def _place(shard, axis, pos, dtype, name, after=None):
    extra = [] if after is None else [after]
    shp = list(shard.shape)
    shp[axis] *= N_CHIPS
    if shard.ndim == 3:
        assert axis == 1
        in_spec = pl.BlockSpec(shard.shape, lambda i, p: (0, 0, 0))
        out_spec = pl.BlockSpec(shard.shape, lambda i, p: (0, p[0], 0))
        grid = (1,)
    else:
        R, C = shard.shape
        tr = _row_block(R, C, bytes_per_row_elem=4, budget=2 << 20)
        nblk = R // tr
        in_spec = pl.BlockSpec((tr, C), lambda i, p: (i, 0))
        out_spec = pl.BlockSpec((tr, C), (lambda i, p: (i, p[0])) if axis == 1 else (lambda i, p: (p[0] * nblk + i, 0)))
        grid = (nblk,)

    def body(*refs):
        s_ref, o_ref = refs[1], refs[-1]
        o_ref[...] = s_ref[...].astype(o_ref.dtype)

    return pl.pallas_call(
        body,
        grid_spec=pltpu.PrefetchScalarGridSpec(num_scalar_prefetch=1, grid=grid, in_specs=[in_spec] + [ANY] * len(extra), out_specs=out_spec),
        out_shape=jax.ShapeDtypeStruct(tuple(shp), dtype),
        compiler_params=_cparams("parallel"),
        name=name,
    )(pos, shard, *extra)


HBM = pl.BlockSpec(memory_space=pltpu.HBM)
SEM = pl.BlockSpec(memory_space=pltpu.SEMAPHORE)
DATAFLOW = pltpu.SideEffectType.DATAFLOW_SIDE_EFFECTING


def _position():
    return lax.axis_index("x"), lax.axis_index("y"), lax.axis_index("c")


def _peer(k):
    x, y, c = _position()
    return ((1 - x) if k & 4 else x, (1 - y) if k & 2 else y, (1 - c) if k & 1 else c)


def _shard_slice(ref, axis, idx, size):
    start = idx * size
    if axis == ref.ndim - 1:
        start = pl.multiple_of(start, 128)
    ix = [slice(None)] * ref.ndim
    ix[axis] = pl.ds(start, size)
    return ref.at[tuple(ix)]


def _gather_plan(axes):
    def plan(refs):
        x, y, c = _position()
        out = []
        for ref, ax in zip(refs, axes):
            mine = _shard_slice(ref, ax, 2 * x + y, ref.shape[ax] // N_CHIPS)
            for k in (4, 2, 6):
                px, py, _ = _peer(k)
                out.append((mine, mine, (px, py, c)))
        return out
    return plan


def _scatter_plan(axes):
    m = len(axes)

    def plan(refs):
        out = []
        for t in range(m):
            grad, recv = refs[t], refs[m + t]
            _, pr, pc = recv.shape
            for k in range(1, N_DEV):
                px, py, pcore = _peer(k)
                if axes[t] == 0:
                    piece = grad.at[pl.ds(((2 * px + py) * 2 + pcore) * pr, pr), :]
                else:
                    piece = grad.at[pl.ds(pcore * pr, pr), pl.ds(pl.multiple_of((2 * px + py) * pc, 128), pc)]
                out.append((piece, recv.at[k - 1], (px, py, pcore)))
        return out
    return plan


def _broadcast_plan(refs):
    small, recv = refs
    return [(small, recv.at[k - 1], _peer(k)) for k in range(1, N_DEV)]


def _start_all(plan, refs, send_sems, recv_sems):
    for q, (src, dst, dev) in enumerate(plan(refs)):
        pltpu.make_async_remote_copy(src_ref=src, dst_ref=dst, send_sem=send_sems.at[q], recv_sem=recv_sems.at[q], device_id=dev, device_id_type=MESH).start()


def _wait_all(plan, refs, send_sems, recv_sems):
    for q, (src, dst, dev) in enumerate(plan(refs)):
        cp = pltpu.make_async_remote_copy(src_ref=src, dst_ref=dst, send_sem=send_sems.at[q], recv_sem=recv_sems.at[q], device_id=dev, device_id_type=MESH)
        cp.wait_send()
        cp.wait_recv()


def _push(bufs, plan, ncopies, name):
    n = len(bufs)

    def body(*refs):
        outs = refs[n:2 * n]
        send_sems, recv_sems = refs[2 * n:]
        _start_all(plan, outs, send_sems, recv_sems)
        _wait_all(plan, outs, send_sems, recv_sems)

    return pl.pallas_call(
        body,
        in_specs=[ANY] * n,
        out_specs=[ANY] * n,
        out_shape=[jax.ShapeDtypeStruct(b.shape, b.dtype) for b in bufs],
        input_output_aliases={t: t for t in range(n)},
        scratch_shapes=[pltpu.SemaphoreType.DMA((ncopies,)), pltpu.SemaphoreType.DMA((ncopies,))],
        name=name,
    )(*bufs)


def _half_slices_plan(onward):
    def plan(refs):
        ref, = refs
        x, y, c = _position()
        R2, C4 = ref.shape[0] // 2, ref.shape[1] // N_CHIPS
        out = []
        for k in (4, 2, 6):
            px, py, _ = _peer(k)
            chip = (2 * px + py) if onward else (2 * x + y)
            half = ref.at[pl.ds(c * R2, R2), pl.ds(pl.multiple_of(chip * C4, 128), C4)]
            out.append((half, half, (x, y, 1 - c) if onward else (px, py, c)))
        return out
    return plan


def _push_start(bufs, plan, ncopies, name, after=None):
    n = len(bufs)
    extra = [] if after is None else [after]

    def body(*refs):
        ins = refs[:n]
        first_out = n + len(extra)
        send_sems, recv_sems, token = refs[first_out], refs[first_out + 1], refs[-1]
        _start_all(plan, ins, send_sems, recv_sems)
        token[...] = jnp.zeros_like(token)

    res = pl.pallas_call(
        body,
        name=name,
        out_shape=(pltpu.SemaphoreType.DMA((ncopies,)), pltpu.SemaphoreType.DMA((ncopies,)), *[pltpu.HBM(b.shape, b.dtype) for b in bufs],
                   jax.ShapeDtypeStruct((8, 128), F32)),
        in_specs=[HBM] * n + [ANY] * len(extra),
        out_specs=(SEM, SEM, *[HBM] * n, pl.BlockSpec(memory_space=pltpu.VMEM)),
        input_output_aliases={t: t + 2 for t in range(n)},
        compiler_params=pltpu.CompilerParams(has_side_effects=DATAFLOW),
    )(*[pltpu.with_memory_space_constraint(b, pltpu.HBM) for b in bufs], *extra)
    return res[0], res[1], list(res[2:2 + n]), res[-1]


def _push_wait(send_sems, recv_sems, bufs, plan, after, name):
    n = len(bufs)
    after = list(after) if isinstance(after, (list, tuple)) else [after]

    def body(*refs):
        ins = refs[:n]
        _wait_all(plan, ins, refs[n], refs[n + 1])

    return pl.pallas_call(
        body,
        name=name,
        out_shape=tuple(pltpu.HBM(b.shape, b.dtype) for b in bufs),
        in_specs=[HBM] * n + [SEM, SEM] + [ANY] * len(after),
        out_specs=tuple([HBM] * n),
        input_output_aliases={t: t for t in range(n)},
        compiler_params=pltpu.CompilerParams(has_side_effects=DATAFLOW),
    )(*bufs, send_sems, recv_sems, *after)


EXCHANGE_CHUNKS = 2


def _exchange_plan(refs):
    x, y, c = _position()
    out = []
    for ref in refs:
        rows = ref.shape[0] // (2 * EXCHANGE_CHUNKS)
        for q in range(EXCHANGE_CHUNKS):
            mine = ref.at[pl.ds((c * EXCHANGE_CHUNKS + q) * rows, rows), :]
            out.append((mine, mine, (x, y, 1 - c)))
    return out


LATE_WEIGHTS = (("w_pool_lin", "w_pool_out", "w_attn_out", "w_out"), ("w_up", "conv_w", "w_down"))


def _local_step(x, tgt, w, late_weights, send):
    S, D = x.shape
    PW = w["pool_scale"].shape[1]
    o_q = PW
    o_g = PW + 3 * ATTN_WIDTH
    QKV = 3 * ATTN_WIDTH

    h1 = _rms_fwd(x, w["g_mix"], "rms1")
    w = dict(w, **late_weights("w_in", h1))
    proj_tiles = (_tile(S, 1024), 512, D)
    started = w.get("late_started")
    u = _mm(h1, w["w_in"], mode="nn", dims=(S, PW, D), tiles=proj_tiles, out_dtypes=(F32,), after=started, name="proj_u")
    qkv = _mm(h1, w["w_in"], mode="nn", dims=(S, QKV, D), tiles=proj_tiles, b_off=(0, o_q), after=started, name="proj_qkv")

    def gate_epilogue(acc, ex, outs):
        outs[0][...] = (1.0 / (1.0 + jnp.exp(-(acc + ex[0][...])))).astype(outs[0].dtype)

    gates = _mm(h1, w["w_in"], mode="nn", dims=(S, 2 * D, D), tiles=proj_tiles, b_off=(0, o_g), epilogue=gate_epilogue,
                extras=[(w["b_gate"], "n", (0, 0))], name="proj_gates")

    os_, lses = [], []
    for gi, (_, d) in enumerate(ATTN_GROUPS):
        o, lse = _attn_fwd(qkv, d, gi, f"attn_fwd{gi}")
        os_.append(o)
        lses.append(lse)
    attn, lse_tot = _attn_merge(os_, lses, "attn_merge")

    w = dict(w, **late_weights(0, attn))
    pool_out = _pool_fwd(u, w["w_pool_lin"], w["pool_scale"], "pool_fwd")
    y_pool = _mm(pool_out, w["w_pool_out"], mode="nn", dims=(S, D, PW), name="y_pool")

    def mix_epilogue(acc, ex, outs):
        outs[0][...] = acc.astype(BF16)
        outs[1][...] = (ex[0][...].astype(F32) * ex[2][...].astype(F32) + ex[1][...].astype(F32) * acc).astype(BF16)

    y_attn, mixed = _mm(attn, w["w_attn_out"], mode="nn", dims=(S, D, GROUP_WIDTH), out_dtypes=(BF16, BF16), epilogue=mix_epilogue,
                        extras=[(gates, "mn", (0, 0)), (gates, "mn", (0, D)), (y_pool, "mn", (0, 0))], name="y_attn_mix")

    def residual_epilogue(acc, ex, outs):
        outs[0][...] = ex[0][...] + acc

    x2 = _mm(mixed, w["w_out"], mode="nn", dims=(S, D, D), out_dtypes=(F32,), epilogue=residual_epilogue, extras=[(x, "mn", (0, 0))], name="out_proj")

    h2 = _rms_fwd(x2, w["g_ffn"], "rms2")
    w = dict(w, **late_weights(1, h2))
    F = w["w_down"].shape[0]
    up = _mm(h2, w["w_up"], mode="nn", dims=(S, 2 * F, D), name="up_proj")
    f, act_a, act_b = _convglu_fwd(up, w["conv_w"], w["conv_b"], "convglu_fwd")
    x3 = _mm(f, w["w_down"], mode="nn", dims=(S, D, F), out_dtypes=(F32,), epilogue=residual_epilogue, extras=[(x2, "mn", (0, 0))], name="down_proj")

    g = {}
    dx3b, g["g_final"], loss_cols = _loss_head(x3, tgt, w["g_final"], "loss_head")

    g["w_down"] = _mm(f, dx3b, mode="tn", dims=(F, D, S), name="dw_down")
    sent = send(("w_down",), g)
    df = _mm(dx3b, w["w_down"], mode="nt", dims=(S, F, D), name="d_f")
    dup, g["conv_b"], g["conv_w"] = _convglu_bwd(df, act_a, act_b, up, w["conv_w"] + sent, "convglu_bwd")
    g["w_up"] = _mm(h2, dup, mode="tn", dims=(D, 2 * F, S), name="dw_up")
    sent = send(("w_up",), g)
    dh2 = _mm(dup, w["w_up"], mode="nt", dims=(S, D, 2 * F), name="d_h2")
    dx2b, g["g_ffn"] = _rms_bwd(dh2, x2, w["g_ffn"] + sent, dx3b, "rms2_bwd", BF16)

    g["w_out"] = _mm(mixed, dx2b, mode="tn", dims=(D, D, S), name="dw_out")
    dmixed = _mm(dx2b, w["w_out"], mode="nt", dims=(S, D, D), name="d_mixed")
    IN = w["w_in"].shape[1]
    dy_both, dproj, g["b_gate"] = _gate_bwd(dmixed, gates, y_pool, y_attn, IN, "gate_bwd")

    g["w_pool_out"] = _mm(pool_out, dy_both, mode="tn", dims=(PW, D, S), name="dw_pool_out")
    g["w_attn_out"] = _mm(attn, dy_both, mode="tn", dims=(GROUP_WIDTH, D, S), b_off=(0, D), name="dw_attn_out")
    sent = send(("w_out", "w_pool_out", "w_attn_out"), g)
    dpool = _mm(dy_both, w["w_pool_out"], mode="nt", dims=(S, PW, D), name="d_pool")
    dattn = _mm(dy_both, w["w_attn_out"], mode="nt", dims=(S, GROUP_WIDTH, D), a_off=(0, D), name="d_attn")

    dproj, g["w_pool_lin"], g["pool_scale"] = _pool_bwd(u, dpool, w["w_pool_lin"], w["pool_scale"] + sent, dproj, "pool_bwd")
    g["loss_cols"] = loss_cols
    sent = send("small", g)

    for gi, (_, d) in enumerate(ATTN_GROUPS):
        dproj = _attn_bwd(qkv, dattn, attn, lse_tot, dproj, d, gi, PW // HEAD_DIM, f"attn_bwd{gi}")

    g["w_in"] = _mm(h1, dproj, mode="tn", dims=(D, IN, S), name="dw_in")
    sent = sent + send(("w_in",), g)
    dh1 = _mm(dproj, w["w_in"], mode="nt", dims=(S, D, IN), tiles=(_tile(S, 1024), _tile(D, 2048), _tile(IN, 2432)), name="d_h1")
    (grad_x, g["g_mix"]) = _rms_bwd(dh1, x, w["g_mix"] + sent, dx2b, "rms1_bwd", F32)
    return loss_cols, grad_x, g


BIG = ("w_in", "w_pool_out", "w_attn_out", "w_out", "w_up", "w_down")
BIG_AXIS = {"w_in": 1, "w_pool_out": 1, "w_attn_out": 1, "w_out": 0, "w_up": 1, "w_down": 0}
GATHER_AXIS = dict(BIG_AXIS, w_pool_lin=1, conv_w=1)
SMALL = ("loss_cols", "b_gate", "w_pool_lin", "pool_scale", "g_ffn", "conv_w", "conv_b", "g_final")
SMALL_COLS = 1024
ORDER = ("g_mix", "w_in", "b_gate", "w_pool_lin", "pool_scale", "w_pool_out", "w_attn_out", "w_out", "g_ffn", "w_up", "conv_w", "conv_b", "w_down", "g_final")


def _as_rows(parts):
    flat = jnp.concatenate([p.astype(F32).reshape(-1) for p in parts])
    rows = -(-flat.shape[0] // (8 * SMALL_COLS)) * 8
    return jnp.pad(flat, (0, rows * SMALL_COLS - flat.shape[0])).reshape(rows, SMALL_COLS)


def kernel(x, g_mix, w_in, b_gate, w_pool_lin, pool_scale, w_pool_out, w_attn_out, w_out, g_ffn, w_up, conv_w, conv_b, w_down, g_final, loss_target, m_g_mix, m_w_in, m_b_gate, m_w_pool_lin, m_pool_scale, m_w_pool_out, m_w_attn_out, m_w_out, m_g_ffn, m_w_up, m_conv_w, m_conv_b, m_w_down, m_g_final, v_g_mix, v_w_in, v_b_gate, v_w_pool_lin, v_pool_scale, v_w_pool_out, v_w_attn_out, v_w_out, v_g_ffn, v_w_up, v_conv_w, v_conv_b, v_w_down, v_g_final):
    shard = dict(g_mix=g_mix, w_in=w_in, b_gate=b_gate, w_pool_lin=w_pool_lin, pool_scale=pool_scale, w_pool_out=w_pool_out, w_attn_out=w_attn_out,
                 w_out=w_out, g_ffn=g_ffn, w_up=w_up, conv_w=conv_w, conv_b=conv_b, w_down=w_down, g_final=g_final)
    mom = dict(g_mix=m_g_mix, w_in=m_w_in, b_gate=m_b_gate, w_pool_lin=m_w_pool_lin, pool_scale=m_pool_scale, w_pool_out=m_w_pool_out, w_attn_out=m_w_attn_out,
               w_out=m_w_out, g_ffn=m_g_ffn, w_up=m_w_up, conv_w=m_conv_w, conv_b=m_conv_b, w_down=m_w_down, g_final=m_g_final)
    vel = dict(g_mix=v_g_mix, w_in=v_w_in, b_gate=v_b_gate, w_pool_lin=v_w_pool_lin, pool_scale=v_pool_scale, w_pool_out=v_w_pool_out, w_attn_out=v_w_attn_out,
               w_out=v_w_out, g_ffn=v_g_ffn, w_up=v_w_up, conv_w=v_conv_w, conv_b=v_conv_b, w_down=v_w_down, g_final=v_g_final)
    chip = 2 * lax.axis_index("x") + lax.axis_index("y")
    pos = jnp.stack([chip, lax.axis_index("c")]).astype(jnp.int32)
    me = (2 * chip + lax.axis_index("c")).astype(jnp.int32).reshape(1)
    D = x.shape[2]

    out_plan, on_plan = _half_slices_plan(False), _half_slices_plan(True)
    in_send, in_recv, in_bufs, in_token = _push_start([_place(shard["w_in"][0], GATHER_AXIS["w_in"], pos, BF16, "place_w_in")], out_plan, 3,
                                                      "comm_gather_w_in_start")
    placed = {k: _place(shard[k][0], GATHER_AXIS[k], pos, F32 if k == "conv_w" else BF16, f"place_{k}", after=in_token)
              for names in LATE_WEIGHTS for k in names}
    late = []

    def late_weights(stage, after):
        if stage == "w_in":
            landed = _push_wait(in_send, in_recv, in_bufs, out_plan, [after] + list(placed.values()), "comm_gather_w_in_wait")
            w_in_full, = _push(list(landed), on_plan, 3, "comm_gather_w_in_pass")
            late_token, prior = 0.0, w_in_full
            for st, names in enumerate(LATE_WEIGHTS):
                plan = _gather_plan([GATHER_AXIS[k] for k in names])
                send_sems, recv_sems, bufs, token = _push_start([placed[k] for k in names], plan, 3 * len(names), f"comm_gather_late{st}_start", after=prior)
                late.append((names, send_sems, recv_sems, bufs, plan))
                late_token, prior = late_token + token[0, 0], token
            return dict(w_in=w_in_full, b_gate=shard["b_gate"] + late_token, late_started=prior)
        names, send_sems, recv_sems, bufs, plan = late[stage]
        return dict(zip(names, _push_wait(send_sems, recv_sems, bufs, plan, after, f"comm_gather_late{stage}_wait")))

    pending = []

    def send(names, g):
        if names == "small":
            bufs = [_as_rows([g[k] for k in SMALL])]
            bufs.append(lax.empty((N_DEV - 1,) + bufs[0].shape, F32))
            plan, tag = _broadcast_plan, "small"
        else:
            bufs = [g[k] for k in names]
            for k in names:
                R, C = g[k].shape
                piece = (R // (2 * N_CHIPS), C) if BIG_AXIS[k] == 0 else (R // 2, C // N_CHIPS)
                bufs.append(lax.empty((N_DEV - 1,) + piece, BF16))
            plan, tag = _scatter_plan([BIG_AXIS[k] for k in names]), names[0]
        ncopies = (N_DEV - 1) * (len(bufs) // 2)
        send_sems, recv_sems, thru, token = _push_start(bufs, plan, ncopies, f"comm_scatter_start_{tag}")
        pending.append((names, send_sems, recv_sems, thru, plan, tag))
        return token[0, 0]

    w0 = dict(g_mix=shard["g_mix"] + in_token[0, 0], pool_scale=shard["pool_scale"], g_ffn=shard["g_ffn"],
              conv_b=shard["conv_b"], g_final=shard["g_final"].reshape(1, D))
    _, grad_x, gr = _local_step(x[0], loss_target[0], w0, late_weights, send)

    halves, small_parts = {}, None
    for names, send_sems, recv_sems, thru, plan, tag in pending:
        done = _push_wait(send_sems, recv_sems, thru, plan, grad_x, f"comm_scatter_wait_{tag}")
        if names == "small":
            small_parts = _sum_small(done[0], done[1], me, "sum_small").reshape(-1)
        else:
            m = len(names)
            for t, k in enumerate(names):
                halves[k] = _sum_pieces(done[t], BIG_AXIS[k], done[m + t], pos, f"sum_{k}")
    g_mix_own = _as_rows([gr["g_mix"]])
    _, g_mix_recv = _push([g_mix_own, lax.empty((N_DEV - 1,) + g_mix_own.shape, F32)], _broadcast_plan, N_DEV - 1, "comm_gather_g_mix")
    g_mix_sum = _sum_small(g_mix_own, g_mix_recv, me, "sum_g_mix").reshape(-1)[:D]
    wholes = _push([halves[k] for k in BIG], _exchange_plan, EXCHANGE_CHUNKS * len(BIG), "comm_exchange_halves")

    grads = {"g_mix": g_mix_sum.reshape(shard["g_mix"].shape)}
    for k, whole in zip(BIG, wholes):
        grads[k] = whole.reshape(shard[k].shape)
    off = 0
    loss = None
    for k in SMALL:
        sz = math.prod(gr[k].shape)
        fullg = small_parts[off:off + sz].reshape(gr[k].shape)
        off += sz
        if k == "loss_cols":
            loss = jnp.sum(fullg)
            continue
        if k in ("w_pool_lin", "conv_w"):
            n = shard[k].shape[2]
            fullg = lax.dynamic_slice_in_dim(fullg, chip * n, n, axis=1)
        grads[k] = fullg.reshape(shard[k].shape)

    deltas, new_m, new_v = {}, {}, {}
    for k in ORDER:
        shp = shard[k].shape
        two_d = (-1, shp[-1])
        dl, nm, nv = _adamw(shard[k].reshape(two_d), grads[k].reshape(two_d), mom[k].reshape(two_d), vel[k].reshape(two_d), f"adamw_{k}")
        deltas[k], new_m[k], new_v[k] = dl.reshape(shp), nm.reshape(shp), nv.reshape(shp)

    return (loss, grad_x[None], *[grads[k] for k in ORDER], *[deltas[k] for k in ORDER], *[new_m[k] for k in ORDER], *[new_v[k] for k in ORDER])
```

```python
import functools
import math

import jax
import jax.numpy as jnp
from jax import lax
from jax.experimental import pallas as pl
from jax.experimental.pallas import tpu as pltpu

F32 = jnp.float32
BF16 = jnp.bfloat16

RMS_EPS = 1e-6
POOL_WINDOWS = (2, 4, 8, 16)
ATTN_GROUPS = ((128, 1), (512, 4), (2048, 16))
HEADS_PER_GROUP = 4
HEAD_DIM = 128
N_ATTN_HEADS = HEADS_PER_GROUP * len(ATTN_GROUPS)
SPAN = 128
GROUP_WIDTH = HEADS_PER_GROUP * HEAD_DIM
ATTN_WIDTH = N_ATTN_HEADS * HEAD_DIM
ATTN_SCALE = HEAD_DIM ** -0.5
NEG_BIG = -1e30
ALIBI_SLOPES = tuple(2.0 ** (-8.0 * (h + 1) / N_ATTN_HEADS) for h in range(N_ATTN_HEADS))

ADAM_LR = 0.001
ADAM_B1 = 0.9
ADAM_B2 = 0.999
ADAM_EPS = 1e-08
ADAM_WD = 0.01
ADAM_STEP = 10

INV_SQRT2 = 1.0 / math.sqrt(2.0)
INV_SQRT_2PI = 1.0 / math.sqrt(2.0 * math.pi)

HALO = 16
VMEM_LIMIT = 56 * 1024 * 1024
N_CHIPS = 4
N_DEV = 8
MESH = pl.DeviceIdType.MESH
ANY = pl.BlockSpec(memory_space=pl.ANY)


def _cparams(*sem):
    return pltpu.CompilerParams(dimension_semantics=sem, vmem_limit_bytes=VMEM_LIMIT)


def _tile(n, pref, mult=128):
    t = (min(pref, n) // mult) * mult
    while t >= mult:
        if n % t == 0:
            return t
        t -= mult
    return n


def _dot(a, b, contract):
    return lax.dot_general(a, b, (contract, ((), ())), preferred_element_type=F32)


def _dot_nn(a, b):
    return _dot(a, b, ((1,), (0,)))


def _dot_nt(a, b):
    return _dot(a, b, ((1,), (1,)))


def _mm(a, b, *, mode, dims, name, tiles=None, out_dtypes=(BF16,), epilogue=None, extras=(), a_off=(0, 0), b_off=(0, 0), after=None):
    M, N, K = dims
    if tiles is None:
        tiles = (_tile(M, 1408), _tile(N, 2816), _tile(K, 512)) if mode == "tn" else (_tile(M, 1024), _tile(N, 1536), _tile(K, 2816))
    tm, tn, tk = tiles
    assert M % tm == 0 and N % tn == 0 and K % tk == 0, (name, dims, tiles)
    nk = K // tk
    if mode == "nn":
        ab, bb, contract = (tm, tk), (tk, tn), ((1,), (0,))
        amap = lambda i, j, k: (i + a_off[0] // tm, k + a_off[1] // tk)
        bmap = lambda i, j, k: (k + b_off[0] // tk, j + b_off[1] // tn)
    elif mode == "nt":
        ab, bb, contract = (tm, tk), (tn, tk), ((1,), (1,))
        amap = lambda i, j, k: (i + a_off[0] // tm, k + a_off[1] // tk)
        bmap = lambda i, j, k: (j + b_off[0] // tn, k + b_off[1] // tk)
    else:
        ab, bb, contract = (tk, tm), (tk, tn), ((0,), (0,))
        amap = lambda i, j, k: (k + a_off[0] // tk, i + a_off[1] // tm)
        bmap = lambda i, j, k: (k + b_off[0] // tk, j + b_off[1] // tn)
    assert a_off[0] % ab[0] == 0 and a_off[1] % ab[1] == 0 and b_off[0] % bb[0] == 0 and b_off[1] % bb[1] == 0, name
    in_specs = [pl.BlockSpec(ab, amap), pl.BlockSpec(bb, bmap)]
    ex_arrays = []
    for arr, kind, off in extras:
        if kind == "mn":
            assert off[0] % tm == 0 and off[1] % tn == 0, name
            in_specs.append(pl.BlockSpec((tm, tn), lambda i, j, k, off=off: (i + off[0] // tm, j + off[1] // tn)))
        else:
            assert off[1] % tn == 0, name
            in_specs.append(pl.BlockSpec((1, tn), lambda i, j, k, off=off: (0, j + off[1] // tn)))
        ex_arrays.append(arr)
    ne, no = len(ex_arrays), len(out_dtypes)
    if after is not None:
        in_specs.append(ANY)
        ex_arrays.append(after)
    first_out = 2 + len(ex_arrays)
    if epilogue is None:
        def epilogue(acc, ex, outs):
            outs[0][...] = acc.astype(outs[0].dtype)

    def body(*refs):
        a_ref, b_ref = refs[0], refs[1]
        ex, outs = refs[2:2 + ne], refs[first_out:first_out + no]
        if nk == 1:
            epilogue(_dot(a_ref[...], b_ref[...], contract), ex, outs)
            return
        acc = refs[-1]
        k = pl.program_id(2)
        if nk <= 4:
            part = _dot(a_ref[...], b_ref[...], contract)

            @pl.when(k == 0)
            def _():
                acc[...] = part

            @pl.when(jnp.logical_and(k > 0, k < nk - 1))
            def _():
                acc[...] += part

            @pl.when(k == nk - 1)
            def _():
                epilogue(acc[...] + part, ex, outs)
        else:
            @pl.when(k == 0)
            def _():
                acc[...] = _dot(a_ref[...], b_ref[...], contract)

            @pl.when(k > 0)
            def _():
                acc[...] += _dot(a_ref[...], b_ref[...], contract)

            @pl.when(k == nk - 1)
            def _():
                epilogue(acc[...], ex, outs)

    res = pl.pallas_call(
        body,
        grid=(M // tm, N // tn, nk),
        in_specs=in_specs,
        out_specs=[pl.BlockSpec((tm, tn), lambda i, j, k: (i, j)) for _ in out_dtypes],
        out_shape=[jax.ShapeDtypeStruct((M, N), dt) for dt in out_dtypes],
        scratch_shapes=[pltpu.VMEM((tm, tn), F32)] if nk > 1 else [],
        compiler_params=_cparams("parallel", "parallel", "arbitrary"),
        name=name,
    )(a, b, *ex_arrays)
    return res[0] if no == 1 else res


def _rms_fwd(x, g, name):
    S, D = x.shape
    tm = _tile(S, 256)

    def body(x_ref, g_ref, h_ref):
        xv = x_ref[...]
        r = lax.rsqrt(jnp.mean(xv * xv, axis=-1, keepdims=True) + RMS_EPS)
        h_ref[...] = (xv * r * g_ref[...]).astype(h_ref.dtype)

    return pl.pallas_call(
        body,
        grid=(S // tm,),
        in_specs=[pl.BlockSpec((tm, D), lambda i: (i, 0)), pl.BlockSpec((1, D), lambda i: (0, 0))],
        out_specs=pl.BlockSpec((tm, D), lambda i: (i, 0)),
        out_shape=jax.ShapeDtypeStruct((S, D), BF16),
        compiler_params=_cparams("parallel"),
        name=name,
    )(x, g)


def _rms_bwd(dh, x, g, dres, name, out_dtype):
    S, D = x.shape
    tm = _tile(S, 256)

    def body(dh_ref, x_ref, g_ref, dres_ref, dx_ref, dg_ref):
        xv = x_ref[...]
        r = lax.rsqrt(jnp.mean(xv * xv, axis=-1, keepdims=True) + RMS_EPS)
        xr = xv * r
        dhv = dh_ref[...].astype(F32)

        @pl.when(pl.program_id(0) == 0)
        def _():
            dg_ref[...] = jnp.zeros_like(dg_ref)

        dg_ref[...] += jnp.sum(dhv * xr, axis=0, keepdims=True)
        u = dhv * g_ref[...]
        c = jnp.mean(u * xr, axis=-1, keepdims=True)
        dx_ref[...] = (dres_ref[...].astype(F32) + r * (u - xr * c)).astype(dx_ref.dtype)

    row = pl.BlockSpec((tm, D), lambda i: (i, 0))
    vec = pl.BlockSpec((1, D), lambda i: (0, 0))
    return pl.pallas_call(
        body,
        grid=(S // tm,),
        in_specs=[row, row, vec, row],
        out_specs=[row, vec],
        out_shape=[jax.ShapeDtypeStruct((S, D), out_dtype), jax.ShapeDtypeStruct((1, D), F32)],
        compiler_params=_cparams("arbitrary"),
        name=name,
    )(dh, x, g, dres)


def _loss_head(x3, tgt, g, name):
    S, D = x3.shape
    tm = _tile(S, 256)

    def body(x_ref, t_ref, g_ref, dxb_ref, dg_ref, loss_ref):
        xv = x_ref[...]
        gv = g_ref[...]
        r = lax.rsqrt(jnp.mean(xv * xv, axis=-1, keepdims=True) + RMS_EPS)
        xr = xv * r
        e = xr * gv - t_ref[...]

        @pl.when(pl.program_id(0) == 0)
        def _():
            dg_ref[...] = jnp.zeros_like(dg_ref)
            loss_ref[...] = jnp.zeros_like(loss_ref)

        loss_ref[...] += jnp.sum(e * e, axis=0, keepdims=True) * (0.5 / D)
        dy = e * (1.0 / D)
        dg_ref[...] += jnp.sum(dy * xr, axis=0, keepdims=True)
        u = dy * gv
        c = jnp.mean(u * xr, axis=-1, keepdims=True)
        dxb_ref[...] = (r * (u - xr * c)).astype(BF16)

    row = pl.BlockSpec((tm, D), lambda i: (i, 0))
    vec = pl.BlockSpec((1, D), lambda i: (0, 0))
    return pl.pallas_call(
        body,
        grid=(S // tm,),
        in_specs=[row, row, vec],
        out_specs=[row, vec, vec],
        out_shape=[jax.ShapeDtypeStruct((S, D), BF16), jax.ShapeDtypeStruct((1, D), F32), jax.ShapeDtypeStruct((1, D), F32)],
        compiler_params=_cparams("arbitrary"),
        name=name,
    )(x3, tgt, g)


SHIFT_ROWS = 256


def _shift_matrix():
    i = lax.broadcasted_iota(jnp.int32, (2 * SHIFT_ROWS, SHIFT_ROWS), 0)
    j = lax.broadcasted_iota(jnp.int32, (2 * SHIFT_ROWS, SHIFT_ROWS), 1)
    src = jnp.where(i < SHIFT_ROWS, i - 1, i - SHIFT_ROWS - 2)
    return (j == src).astype(BF16)


def _conv_taps(cur_ref, halo_ref, w_ref, b_ref, first, shift):
    w, bias = w_ref[...], b_ref[...]
    before = jnp.where(first, 0.0, halo_ref[...].astype(F32)[HALO - 8:])
    sub = lax.broadcasted_iota(jnp.int32, before.shape, 0)
    out = []
    for blk in range(cur_ref.shape[0] // SHIFT_ROWS):
        xb = cur_ref[blk * SHIFT_ROWS:(blk + 1) * SHIFT_ROWS, :]
        xf = xb.astype(F32)
        both = _dot_nn(shift, xb)
        p1, p2 = both[:SHIFT_ROWS], both[SHIFT_ROWS:]
        p1 = jnp.concatenate([jnp.where(sub == 0, pltpu.roll(before, 1, 0), p1[:8]), p1[8:]], axis=0)
        p2 = jnp.concatenate([jnp.where(sub < 2, pltpu.roll(before, 2, 0), p2[:8]), p2[8:]], axis=0)
        out.append(bias + w[0:1] * p2 + w[1:2] * p1 + w[2:3] * xf)
        before = xf[SHIFT_ROWS - 8:]
    return jnp.concatenate(out, axis=0)


def _convglu_fwd(up, cw, cb, name):
    S, F2 = up.shape
    F = F2 // 2
    tm, tn = _tile(S, 1024), _tile(F, 512)
    nj, hb = F // tn, tm // HALO

    def body(ua, ub, ha, hb_, wa, wb, ba, bb, f_ref, a_ref, b_ref):
        first = pl.program_id(0) == 0
        shift = _shift_matrix()
        a = _conv_taps(ua, ha, wa, ba, first, shift)
        b = _conv_taps(ub, hb_, wb, bb, first, shift)
        f_ref[...] = (0.5 * a * (1.0 + lax.erf(a * INV_SQRT2)) * b).astype(f_ref.dtype)
        a_ref[...] = a.astype(a_ref.dtype)
        b_ref[...] = b.astype(b_ref.dtype)

    tile = pl.BlockSpec((tm, tn), lambda i, j: (i, j))
    return pl.pallas_call(
        body,
        grid=(S // tm, nj),
        in_specs=[
            tile,
            pl.BlockSpec((tm, tn), lambda i, j: (i, j + nj)),
            pl.BlockSpec((HALO, tn), lambda i, j: (jnp.maximum(i * hb - 1, 0), j)),
            pl.BlockSpec((HALO, tn), lambda i, j: (jnp.maximum(i * hb - 1, 0), j + nj)),
            pl.BlockSpec((3, tn), lambda i, j: (0, j)),
            pl.BlockSpec((3, tn), lambda i, j: (0, j + nj)),
            pl.BlockSpec((1, tn), lambda i, j: (0, j)),
            pl.BlockSpec((1, tn), lambda i, j: (0, j + nj)),
        ],
        out_specs=[tile, tile, tile],
        out_shape=[jax.ShapeDtypeStruct((S, F), BF16)] * 3,
        compiler_params=_cparams("parallel", "parallel"),
        name=name,
    )(up, up, up, up, cw, cw, cb, cb)


def _convglu_bwd(df, a, b, up, cw, name):
    S, F = df.shape
    tm, tn = _tile(S, 1024), _tile(F, 512)
    nj, ni, hb = F // tn, S // tm, tm // HALO
    n = tm + HALO

    def body(df_ref, dfn_ref, a_ref, an_ref, b_ref, bn_ref, up_ref, w_ref, o_ref, db_ref, dw_ref):
        j, i = pl.program_id(0), pl.program_id(1)
        last = i == ni - 1

        def rows(c_ref, n_ref):
            return jnp.concatenate([c_ref[...].astype(F32), jnp.where(last, 0.0, n_ref[...].astype(F32))], axis=0)

        @pl.when(i == 0)
        def _():
            db_ref[...] = jnp.zeros_like(db_ref)
            dw_ref[...] = jnp.zeros_like(dw_ref)

        def finish(d):
            i_ = lax.broadcasted_iota(jnp.int32, (2 * SHIFT_ROWS, SHIFT_ROWS), 0)
            j_ = lax.broadcasted_iota(jnp.int32, (2 * SHIFT_ROWS, SHIFT_ROWS), 1)
            ahead = (j_ == jnp.where(i_ < SHIFT_ROWS, i_ + 1, i_ - SHIFT_ROWS + 2)).astype(BF16)
            db = d.astype(BF16)
            sub = lax.broadcasted_iota(jnp.int32, (8, tn), 0)
            d1, d2 = [], []
            for blk in range(tm // SHIFT_ROWS):
                lo, hi = blk * SHIFT_ROWS, (blk + 1) * SHIFT_ROWS
                both = _dot_nn(ahead, db[lo:hi])
                n1, n2 = both[:SHIFT_ROWS], both[SHIFT_ROWS:]
                after = db[hi:hi + HALO].astype(F32)[:8]
                d1 += [n1[:-8], jnp.where(sub == 7, pltpu.roll(after, 7, 0), n1[-8:])]
                d2 += [n2[:-8], jnp.where(sub >= 6, pltpu.roll(after, 6, 0), n2[-8:])]
            d0, d1, d2 = d[:tm], jnp.concatenate(d1, axis=0), jnp.concatenate(d2, axis=0)
            w = w_ref[...]
            o_ref[...] = (w[2:3] * d0 + w[1:2] * d1 + w[0:1] * d2).astype(o_ref.dtype)
            upv = up_ref[...].astype(F32)
            db_ref[...] += jnp.sum(d0, axis=0, keepdims=True)
            dw_ref[0:1, :] += jnp.sum(d2 * upv, axis=0, keepdims=True)
            dw_ref[1:2, :] += jnp.sum(d1 * upv, axis=0, keepdims=True)
            dw_ref[2:3, :] += jnp.sum(d0 * upv, axis=0, keepdims=True)

        av, dfv = rows(a_ref, an_ref), rows(df_ref, dfn_ref)
        cdf = 0.5 * (1.0 + lax.erf(av * INV_SQRT2))

        @pl.when(j < nj)
        def _():
            pdf = jnp.exp(-0.5 * av * av) * INV_SQRT_2PI
            finish(dfv * rows(b_ref, bn_ref) * (cdf + av * pdf))

        @pl.when(j >= nj)
        def _():
            finish(dfv * (av * cdf))

    jh = lambda j: lax.rem(j, nj)
    nxt = lambda i: jnp.minimum((i + 1) * hb, S // HALO - 1)
    cur = pl.BlockSpec((tm, tn), lambda j, i: (i, jh(j)))
    halo = pl.BlockSpec((HALO, tn), lambda j, i: (nxt(i), jh(j)))
    return pl.pallas_call(
        body,
        grid=(2 * nj, ni),
        in_specs=[cur, halo, cur, halo, cur, halo, pl.BlockSpec((tm, tn), lambda j, i: (i, j)), pl.BlockSpec((3, tn), lambda j, i: (0, j))],
        out_specs=[pl.BlockSpec((tm, tn), lambda j, i: (i, j)), pl.BlockSpec((1, tn), lambda j, i: (0, j)), pl.BlockSpec((3, tn), lambda j, i: (0, j))],
        out_shape=[jax.ShapeDtypeStruct((S, 2 * F), BF16), jax.ShapeDtypeStruct((1, 2 * F), F32), jax.ShapeDtypeStruct((3, 2 * F), F32)],
        compiler_params=_cparams("parallel", "arbitrary"),
        name=name,
    )(df, df, a, a, b, b, up, cw)


def _gate_bwd(dmixed, gates, y_pool, y_attn, in_width, name):
    S, D = dmixed.shape
    tm, tn = _tile(S, 1024), _tile(D, 512)
    nj = D // tn
    pre0 = (in_width - 2 * D) // tn
    assert pre0 * tn == in_width - 2 * D

    def body(dm_ref, g_ref, yp_ref, ya_ref, dy_ref, dpre_ref, db_ref):
        j = pl.program_id(0)

        @pl.when(pl.program_id(1) == 0)
        def _():
            db_ref[...] = jnp.zeros_like(db_ref)

        def run(y_ref):
            dm = dm_ref[...].astype(F32)
            gv = g_ref[...].astype(F32)
            dy_ref[...] = (dm * gv).astype(BF16)
            dpre = dm * y_ref[...].astype(F32) * gv * (1.0 - gv)
            dpre_ref[...] = dpre.astype(BF16)
            db_ref[...] += jnp.sum(dpre, axis=0, keepdims=True)

        @pl.when(j < nj)
        def _():
            run(yp_ref)

        @pl.when(j >= nj)
        def _():
            run(ya_ref)

    tile2 = pl.BlockSpec((tm, tn), lambda j, i: (i, j))
    return pl.pallas_call(
        body,
        grid=(2 * nj, S // tm),
        in_specs=[
            pl.BlockSpec((tm, tn), lambda j, i: (i, lax.rem(j, nj))),
            tile2,
            pl.BlockSpec((tm, tn), lambda j, i: (i, jnp.minimum(j, nj - 1))),
            pl.BlockSpec((tm, tn), lambda j, i: (i, jnp.maximum(j - nj, 0))),
        ],
        out_specs=[tile2, pl.BlockSpec((tm, tn), lambda j, i: (i, pre0 + j)), pl.BlockSpec((1, tn), lambda j, i: (0, j))],
        out_shape=[jax.ShapeDtypeStruct((S, 2 * D), BF16), jax.ShapeDtypeStruct((S, in_width), BF16), jax.ShapeDtypeStruct((1, 2 * D), F32)],
        compiler_params=_cparams("parallel", "arbitrary"),
        name=name,
    )(dmixed, gates, y_pool, y_attn)


def _pool_counts(i, tm, rows, w):
    t = i * tm + lax.broadcasted_iota(jnp.int32, (rows, 1), 0)
    return jnp.minimum(t + 1, w).astype(F32)


def _pooled_groups(u_ref, uh_ref, i, tm, C):
    cur = u_ref[...]
    halo = jnp.where(i == 0, 0.0, uh_ref[...])
    xx = jnp.concatenate([halo, cur], axis=0)
    out = []
    s = xx
    for gi, w in enumerate(POOL_WINDOWS):
        s = s + pltpu.roll(s, w // 2, 0)
        tot = s[HALO:, 0:C]
        out.append(tot / _pool_counts(i, tm, tm, w) - cur[:, gi * C:(gi + 1) * C])
        s = s[:, C:] if gi + 1 < len(POOL_WINDOWS) else s
    return out


def _pool_fwd(u, wl, scale, name):
    S, PW = u.shape
    C = PW // len(POOL_WINDOWS)
    tm = _tile(S, 512)
    hb = tm // HALO

    def body(u_ref, uh_ref, wl_ref, sc_ref, o_ref):
        i = pl.program_id(0)
        pooled = _pooled_groups(u_ref, uh_ref, i, tm, C)
        for gi in range(len(POOL_WINDOWS)):
            y = _dot_nn(pooled[gi].astype(BF16), wl_ref[gi])
            o_ref[:, gi * C:(gi + 1) * C] = (y * sc_ref[:, gi * C:(gi + 1) * C]).astype(o_ref.dtype)

    return pl.pallas_call(
        body,
        grid=(S // tm,),
        in_specs=[
            pl.BlockSpec((tm, PW), lambda i: (i, 0)),
            pl.BlockSpec((HALO, PW), lambda i: (jnp.maximum(i * hb - 1, 0), 0)),
            pl.BlockSpec((len(POOL_WINDOWS), C, C), lambda i: (0, 0, 0)),
            pl.BlockSpec((1, PW), lambda i: (0, 0)),
        ],
        out_specs=pl.BlockSpec((tm, PW), lambda i: (i, 0)),
        out_shape=jax.ShapeDtypeStruct((S, PW), BF16),
        compiler_params=_cparams("parallel"),
        name=name,
    )(u, u, wl, scale)


def _pool_bwd(u, dp, wl, scale, dproj, name):
    S, PW = u.shape
    G = len(POOL_WINDOWS)
    C = PW // G
    tm = _tile(S, 512)
    hb, ni = tm // HALO, S // tm
    n = tm + HALO

    def body(u_ref, uh_ref, dp_ref, dpn_ref, wl_ref, sc_ref, _, du_ref, dwl_ref, dsc_ref):
        i = pl.program_id(0)

        @pl.when(i == 0)
        def _():
            dwl_ref[...] = jnp.zeros_like(dwl_ref)
            dsc_ref[...] = jnp.zeros_like(dsc_ref)

        pooled = _pooled_groups(u_ref, uh_ref, i, tm, C)
        dpc = dp_ref[...].astype(F32)
        dpn = jnp.where(i == ni - 1, 0.0, dpn_ref[...].astype(F32))
        sc = sc_ref[...]
        dyl = jnp.concatenate([dpc, dpn], axis=0) * sc
        for gi, w in enumerate(POOL_WINDOWS):
            cols = slice(gi * C, (gi + 1) * C)
            pb = pooled[gi].astype(BF16)
            ylin = _dot_nn(pb, wl_ref[gi])
            dsc_ref[:, cols] += jnp.sum(dpc[:, cols] * ylin, axis=0, keepdims=True)
            dylg = dyl[:, cols].astype(BF16)
            dwl_ref[gi] += _dot(pb, dylg[:tm], ((0,), (0,)))
            dpool = _dot_nt(dylg, wl_ref[gi])
            e = dpool / _pool_counts(i, tm, n, w)
            k = 1
            while k < w:
                e = e + pltpu.roll(e, n - k, 0)
                k *= 2
            du_ref[:, cols] = (e[:tm] - dpool[:tm]).astype(du_ref.dtype)

    return pl.pallas_call(
        body,
        grid=(ni,),
        in_specs=[
            pl.BlockSpec((tm, PW), lambda i: (i, 0)),
            pl.BlockSpec((HALO, PW), lambda i: (jnp.maximum(i * hb - 1, 0), 0)),
            pl.BlockSpec((tm, PW), lambda i: (i, 0)),
            pl.BlockSpec((HALO, PW), lambda i: (jnp.minimum((i + 1) * hb, S // HALO - 1), 0)),
            pl.BlockSpec((G, C, C), lambda i: (0, 0, 0)),
            pl.BlockSpec((1, PW), lambda i: (0, 0)),
            ANY,
        ],
        out_specs=[pl.BlockSpec((tm, PW), lambda i: (i, 0)), pl.BlockSpec((G, C, C), lambda i: (0, 0, 0)), pl.BlockSpec((1, PW), lambda i: (0, 0))],
        out_shape=[jax.ShapeDtypeStruct(dproj.shape, dproj.dtype), jax.ShapeDtypeStruct((G, C, C), F32), jax.ShapeDtypeStruct((1, PW), F32)],
        input_output_aliases={6: 0},
        compiler_params=_cparams("arbitrary"),
        name=name,
    )(u, u, dp, dp, wl, scale, dproj)


def _band_masks():
    ii = lax.broadcasted_iota(jnp.int32, (SPAN, SPAN), 0)
    kk = lax.broadcasted_iota(jnp.int32, (SPAN, SPAN), 1)
    return ((ii + SPAN - kk).astype(F32), kk >= ii), ((ii - kk).astype(F32), kk <= ii)


ATTN_TILE = 16 * SPAN


def _unit_rows(r, b, d, blocks=1):
    return pl.ds(d * SPAN * b + r, blocks * SPAN, stride=d) if d > 1 else pl.ds(SPAN * b, blocks * SPAN)


def _f32_copies(refs, scratch, d):
    if d == 1:
        return list(refs)
    for ref, s in zip(refs, scratch):
        s[...] = ref[...].astype(F32)
    return list(scratch)


def _attn_fwd(qkv, d, g, name):
    S = qkv.shape[0]
    T = min(ATTN_TILE, S)
    P = SPAN * d
    nbk = T // P

    def body(q_ref, k_ref, v_ref, kp_ref, vp_ref, o_ref, lse_ref, *scratch):
        c = pl.program_id(0)
        (jp, mp), (jc, mc) = _band_masks()
        slopes = [ALIBI_SLOPES[g * HEADS_PER_GROUP + h] * d for h in range(HEADS_PER_GROUP)]
        slope = slopes[0]
        for h in range(1, HEADS_PER_GROUP):
            slope = jnp.where(pl.program_id(1) == h, slopes[h], slope)
        q_s, k_s, v_s, kp_s, vp_s = _f32_copies((q_ref, k_ref, v_ref, kp_ref, vp_ref), scratch[:5], d)
        o_s, l_s = (o_ref, lse_ref) if d == 1 else scratch[5:7]
        bias_p, bias_c = jnp.where(mp, -slope * jp, NEG_BIG), jnp.where(mc, -slope * jc, NEG_BIG)
        bias = jnp.concatenate([bias_p, bias_c], axis=1)
        bias_first = jnp.concatenate([jnp.where(c > 0, bias_p, NEG_BIG), bias_c], axis=1)
        for r in range(d):
            for b in range(nbk):
                rows = _unit_rows(r, b, d)
                q = q_s[rows, :].astype(BF16)
                if b == 0:
                    prev = _unit_rows(r, 0, d)
                    kk = jnp.concatenate([kp_s[prev, :], k_s[rows, :]], axis=0).astype(BF16)
                    vv = jnp.concatenate([vp_s[prev, :], v_s[rows, :]], axis=0).astype(BF16)
                else:
                    both = _unit_rows(r, b - 1, d, 2)
                    kk, vv = k_s[both, :].astype(BF16), v_s[both, :].astype(BF16)
                s = _dot_nt(q, kk) * ATTN_SCALE + (bias_first if b == 0 else bias)
                m = jnp.max(s, axis=-1, keepdims=True)
                p = jnp.exp(s - m)
                l = jnp.sum(p, axis=-1, keepdims=True)
                o_s[rows, :] = _dot_nn(p.astype(BF16), vv) / l
                l_s[rows, :] = jnp.broadcast_to(m + jnp.log(l), (SPAN, HEAD_DIM))
        if d > 1:
            o_ref[...] = o_s[...]
            lse_ref[...] = l_s[...]

    col = lambda kind: (lambda c, h: (c, kind * N_ATTN_HEADS + g * HEADS_PER_GROUP + h))
    pcol = lambda kind: (lambda c, h: (jnp.maximum(c * nbk - 1, 0), kind * N_ATTN_HEADS + g * HEADS_PER_GROUP + h))
    cur = lambda kind: pl.BlockSpec((T, HEAD_DIM), col(kind))
    prv = lambda kind: pl.BlockSpec((P, HEAD_DIM), pcol(kind))
    out = pl.BlockSpec((T, HEAD_DIM), lambda c, h: (c, h))
    scratch = [] if d == 1 else [pltpu.VMEM((T, HEAD_DIM), F32)] * 3 + [pltpu.VMEM((P, HEAD_DIM), F32)] * 2 + [pltpu.VMEM((T, HEAD_DIM), F32)] * 2
    return pl.pallas_call(
        body,
        grid=(S // T, HEADS_PER_GROUP),
        in_specs=[cur(0), cur(1), cur(2), prv(1), prv(2)],
        out_specs=[out, out],
        out_shape=[jax.ShapeDtypeStruct((S, GROUP_WIDTH), F32)] * 2,
        scratch_shapes=scratch,
        compiler_params=_cparams("parallel", "parallel"),
        name=name,
    )(qkv, qkv, qkv, qkv, qkv)


def _attn_merge(os_, lses, name):
    S, W = os_[0].shape
    tm = _tile(S, 512)

    def body(o0, o1, o2, l0, l1, l2, y_ref, lse_ref):
        ls = [l0[...], l1[...], l2[...]]
        m = jnp.maximum(jnp.maximum(ls[0], ls[1]), ls[2])
        es = [jnp.exp(v - m) for v in ls]
        tot = es[0] + es[1] + es[2]
        y = (es[0] * o0[...] + es[1] * o1[...] + es[2] * o2[...]) / tot
        y_ref[...] = y.astype(y_ref.dtype)
        lse_ref[...] = m + jnp.log(tot)

    row = pl.BlockSpec((tm, W), lambda i: (i, 0))
    return pl.pallas_call(
        body,
        grid=(S // tm,),
        in_specs=[row] * 6,
        out_specs=[row, row],
        out_shape=[jax.ShapeDtypeStruct((S, W), BF16), jax.ShapeDtypeStruct((S, W), F32)],
        compiler_params=_cparams("parallel"),
        name=name,
    )(*os_, *lses)


def _attn_bwd(qkv, dattn, y, lse, dproj, d, g, col0, name):
    S = qkv.shape[0]
    T = min(ATTN_TILE, S)
    P = SPAN * d
    nbk = T // P
    ntile = S // T

    def body(q_ref, k_ref, v_ref, kp_ref, vp_ref, qn_ref, da_ref, dan_ref, y_ref, yn_ref, lse_ref, lsen_ref, _, out_ref, dq_s, dk_s, dv_s, *scratch):
        c = pl.program_id(0)
        head_id = pl.program_id(1)
        kind = pl.program_id(2)

        @pl.when(kind == 0)
        def _():
            (jp, mp), (jc, mc) = _band_masks()
            slopes = [ALIBI_SLOPES[g * HEADS_PER_GROUP + h] * d for h in range(HEADS_PER_GROUP)]
            slope = slopes[0]
            for h in range(1, HEADS_PER_GROUP):
                slope = jnp.where(head_id == h, slopes[h], slope)
            q_s, k_s, v_s, da_s, y_s, kp_s, vp_s, qn_s, dan_s, yn_s = _f32_copies(
                (q_ref, k_ref, v_ref, da_ref, y_ref, kp_ref, vp_ref, qn_ref, dan_ref, yn_ref), scratch, d)
            bias_p, bias_c = jnp.where(mp, -slope * jp, NEG_BIG), jnp.where(mc, -slope * jc, NEG_BIG)
            bias = jnp.concatenate([bias_c, bias_p], axis=0)
            bias_last = jnp.concatenate([bias_c, jnp.where(c < ntile - 1, bias_p, NEG_BIG)], axis=0)
            bias_first = jnp.where(c > 0, bias_p, NEG_BIG)

            def pair(q, da, yy, lse_blk, kk, vv, b):
                dd = jnp.sum(da.astype(F32) * yy.astype(F32), axis=-1, keepdims=True)
                p = jnp.exp(_dot_nt(q, kk) * ATTN_SCALE + b - lse_blk[:, 0:1])
                return p, p * (_dot_nt(da, vv) - dd)

            for r in range(d):
                first = _unit_rows(r, 0, d)
                kk, vv = kp_s[first, :].astype(BF16), vp_s[first, :].astype(BF16)
                _, ds = pair(q_s[first, :].astype(BF16), da_s[first, :].astype(BF16), y_s[first, :], lse_ref[first, :], kk, vv, bias_first)
                dq_next = _dot_nn(ds.astype(BF16), kk)
                for kb in range(nbk):
                    rows = _unit_rows(r, kb, d)
                    if kb + 1 < nbk:
                        both = _unit_rows(r, kb, d, 2)
                        q, da, yy, lse_blk = q_s[both, :], da_s[both, :], y_s[both, :], lse_ref[both, :]
                    else:
                        q = jnp.concatenate([q_s[rows, :], qn_s[first, :]], axis=0)
                        da = jnp.concatenate([da_s[rows, :], dan_s[first, :]], axis=0)
                        yy = jnp.concatenate([y_s[rows, :], yn_s[first, :]], axis=0)
                        lse_blk = jnp.concatenate([lse_ref[rows, :], lsen_ref[first, :]], axis=0)
                    q, da = q.astype(BF16), da.astype(BF16)
                    kk, vv = k_s[rows, :].astype(BF16), v_s[rows, :].astype(BF16)
                    p, ds = pair(q, da, yy, lse_blk, kk, vv, bias if kb + 1 < nbk else bias_last)
                    dv_s[rows, :] = _dot_nn(p.T.astype(BF16), da)
                    dk_s[rows, :] = _dot_nn(ds.T.astype(BF16), q) * ATTN_SCALE
                    dq_both = _dot_nn(ds.astype(BF16), kk)
                    dq_s[rows, :] = (dq_next + dq_both[:SPAN]) * ATTN_SCALE
                    dq_next = dq_both[SPAN:]
            out_ref[...] = dq_s[...].astype(out_ref.dtype)

        @pl.when(kind == 1)
        def _():
            out_ref[...] = dk_s[...].astype(out_ref.dtype)

        @pl.when(kind == 2)
        def _():
            out_ref[...] = dv_s[...].astype(out_ref.dtype)

    head = lambda h: g * HEADS_PER_GROUP + h
    cur = lambda kind: pl.BlockSpec((T, HEAD_DIM), lambda c, h, kd: (c, kind * N_ATTN_HEADS + head(h)))
    prv = lambda kind: pl.BlockSpec((P, HEAD_DIM), lambda c, h, kd: (jnp.maximum(c * nbk - 1, 0), kind * N_ATTN_HEADS + head(h)))
    nxt_row = lambda c: jnp.minimum((c + 1) * nbk, S // P - 1)
    qnext = pl.BlockSpec((P, HEAD_DIM), lambda c, h, kd: (nxt_row(c), head(h)))
    hcur = pl.BlockSpec((T, HEAD_DIM), lambda c, h, kd: (c, h))
    hnext = pl.BlockSpec((P, HEAD_DIM), lambda c, h, kd: (nxt_row(c), h))
    out = pl.BlockSpec((T, HEAD_DIM), lambda c, h, kd: (c, col0 + kd * N_ATTN_HEADS + head(h)))
    stage = [pltpu.VMEM((T, HEAD_DIM), F32)] * 3
    copies = [] if d == 1 else [pltpu.VMEM((T, HEAD_DIM), F32)] * 5 + [pltpu.VMEM((P, HEAD_DIM), F32)] * 5
    return pl.pallas_call(
        body,
        grid=(ntile, HEADS_PER_GROUP, 3),
        in_specs=[cur(0), cur(1), cur(2), prv(1), prv(2), qnext, hcur, hnext, hcur, hnext, hcur, hnext, ANY],
        out_specs=out,
        out_shape=jax.ShapeDtypeStruct(dproj.shape, dproj.dtype),
        input_output_aliases={12: 0},
        scratch_shapes=stage + copies,
        compiler_params=_cparams("parallel", "parallel", "arbitrary"),
        name=name,
    )(qkv, qkv, qkv, qkv, qkv, qkv, dattn, dattn, y, y, lse, lse, dproj)


def _row_block(R, C, bytes_per_row_elem=4, budget=1 << 20):
    if R % 8:
        return R
    best = 8
    t = 8
    while t <= R:
        if R % t == 0 and t * C * bytes_per_row_elem <= budget:
            best = t
        t += 8
    return best


def _adamw(w, g, m, v, name):
    R, C = w.shape
    tr = _row_block(R, C)
    c1 = 1.0 - ADAM_B1 ** ADAM_STEP
    c2 = 1.0 - ADAM_B2 ** ADAM_STEP

    def body(w_ref, g_ref, m_ref, v_ref, d_ref, nm_ref, nv_ref):
        gv = g_ref[...]
        nm = ADAM_B1 * m_ref[...] + (1.0 - ADAM_B1) * gv
        nv = ADAM_B2 * v_ref[...] + (1.0 - ADAM_B2) * (gv * gv)
        d_ref[...] = -ADAM_LR * ((nm / c1) / (jnp.sqrt(nv / c2) + ADAM_EPS) + ADAM_WD * w_ref[...])
        nm_ref[...] = nm
        nv_ref[...] = nv

    blk = pl.BlockSpec((tr, C), lambda i: (i, 0))
    return pl.pallas_call(
        body,
        grid=(R // tr,),
        in_specs=[blk] * 4,
        out_specs=[blk] * 3,
        out_shape=[jax.ShapeDtypeStruct((R, C), F32)] * 3,
        compiler_params=_cparams("parallel"),
        name=name,
    )(w, g, m, v)


def _sum_pieces(grad, axis, recv, pos, name):
    n, pr, pc = recv.shape
    tr = _row_block(pr, pc, bytes_per_row_elem=(n + 1) * recv.dtype.itemsize, budget=4 << 20)
    nblk = pr // tr
    if axis == 1:
        own_map = lambda i, p: (p[1] * nblk + i, p[0])
    else:
        own_map = lambda i, p: ((2 * p[0] + p[1]) * nblk + i, 0)

    def body(p_ref, own_ref, r_ref, o_ref):
        acc = own_ref[...].astype(F32)
        for s in range(n):
            acc = acc + r_ref[s].astype(F32)
        o_ref[...] = acc

    return pl.pallas_call(
        body,
        grid_spec=pltpu.PrefetchScalarGridSpec(
            num_scalar_prefetch=1,
            grid=(nblk,),
            in_specs=[pl.BlockSpec((tr, pc), own_map), pl.BlockSpec((n, tr, pc), lambda i, p: (0, i, 0))],
            out_specs=pl.BlockSpec((tr, pc), lambda i, p: (p[1] * nblk + i, 0)),
        ),
        out_shape=jax.ShapeDtypeStruct((2 * pr, pc), F32),
        compiler_params=_cparams("parallel"),
        name=name,
    )(pos, grad, recv)


def _sum_small(own, recv, me, name):
    n, R, C = recv.shape
    tr = _row_block(R, C, bytes_per_row_elem=(n + 1) * 4, budget=4 << 20)

    def body(me_ref, own_ref, r_ref, o_ref):
        acc = None
        for dev in range(n + 1):
            k = jnp.bitwise_xor(me_ref[0], dev)
            term = jnp.where(k == 0, own_ref[...], r_ref[jnp.maximum(k - 1, 0)])
            acc = term if acc is None else acc + term
        o_ref[...] = acc

    return pl.pallas_call(
        body,
        grid_spec=pltpu.PrefetchScalarGridSpec(
            num_scalar_prefetch=1,
            grid=(R // tr,),
            in_specs=[pl.BlockSpec((tr, C), lambda i, m: (i, 0)), pl.BlockSpec((n, tr, C), lambda i, m: (0, i, 0))],
            out_specs=pl.BlockSpec((tr, C), lambda i, m: (i, 0)),
        ),
        out_shape=jax.ShapeDtypeStruct((R, C), F32),
        compiler_params=_cparams("parallel"),
        name=name,
    )(me, own, recv)


def _place(shard, axis, pos, dtype, name, after=None):
    extra = [] if after is None else [after]
    shp = list(shard.shape)
    shp[axis] *= N_CHIPS
    if shard.ndim == 3:
        assert axis == 1
        in_spec = pl.BlockSpec(shard.shape, lambda i, p: (0, 0, 0))
        out_spec = pl.BlockSpec(shard.shape, lambda i, p: (0, p[0], 0))
        grid = (1,)
    else:
        R, C = shard.shape
        tr = _row_block(R, C, bytes_per_row_elem=4, budget=2 << 20)
        nblk = R // tr
        in_spec = pl.BlockSpec((tr, C), lambda i, p: (i, 0))
        out_spec = pl.BlockSpec((tr, C), (lambda i, p: (i, p[0])) if axis == 1 else (lambda i, p: (p[0] * nblk + i, 0)))
        grid = (nblk,)

    def body(*refs):
        s_ref, o_ref = refs[1], refs[-1]
        o_ref[...] = s_ref[...].astype(o_ref.dtype)

    return pl.pallas_call(
        body,
        grid_spec=pltpu.PrefetchScalarGridSpec(num_scalar_prefetch=1, grid=grid, in_specs=[in_spec] + [ANY] * len(extra), out_specs=out_spec),
        out_shape=jax.ShapeDtypeStruct(tuple(shp), dtype),
        compiler_params=_cparams("parallel"),
        name=name,
    )(pos, shard, *extra)


HBM = pl.BlockSpec(memory_space=pltpu.HBM)
SEM = pl.BlockSpec(memory_space=pltpu.SEMAPHORE)
DATAFLOW = pltpu.SideEffectType.DATAFLOW_SIDE_EFFECTING


def _position():
    return lax.axis_index("x"), lax.axis_index("y"), lax.axis_index("c")


def _peer(k):
    x, y, c = _position()
    return ((1 - x) if k & 4 else x, (1 - y) if k & 2 else y, (1 - c) if k & 1 else c)


def _shard_slice(ref, axis, idx, size):
    start = idx * size
    if axis == ref.ndim - 1:
        start = pl.multiple_of(start, 128)
    ix = [slice(None)] * ref.ndim
    ix[axis] = pl.ds(start, size)
    return ref.at[tuple(ix)]


def _gather_plan(axes):
    def plan(refs):
        x, y, c = _position()
        out = []
        for ref, ax in zip(refs, axes):
            mine = _shard_slice(ref, ax, 2 * x + y, ref.shape[ax] // N_CHIPS)
            for k in (4, 2, 6):
                px, py, _ = _peer(k)
                out.append((mine, mine, (px, py, c)))
        return out
    return plan


def _scatter_plan(axes):
    m = len(axes)

    def plan(refs):
        out = []
        for t in range(m):
            grad, recv = refs[t], refs[m + t]
            _, pr, pc = recv.shape
            for k in range(1, N_DEV):
                px, py, pcore = _peer(k)
                if axes[t] == 0:
                    piece = grad.at[pl.ds(((2 * px + py) * 2 + pcore) * pr, pr), :]
                else:
                    piece = grad.at[pl.ds(pcore * pr, pr), pl.ds(pl.multiple_of((2 * px + py) * pc, 128), pc)]
                out.append((piece, recv.at[k - 1], (px, py, pcore)))
        return out
    return plan


def _broadcast_plan(refs):
    small, recv = refs
    return [(small, recv.at[k - 1], _peer(k)) for k in range(1, N_DEV)]


def _start_all(plan, refs, send_sems, recv_sems):
    for q, (src, dst, dev) in enumerate(plan(refs)):
        pltpu.make_async_remote_copy(src_ref=src, dst_ref=dst, send_sem=send_sems.at[q], recv_sem=recv_sems.at[q], device_id=dev, device_id_type=MESH).start()


def _wait_all(plan, refs, send_sems, recv_sems):
    for q, (src, dst, dev) in enumerate(plan(refs)):
        cp = pltpu.make_async_remote_copy(src_ref=src, dst_ref=dst, send_sem=send_sems.at[q], recv_sem=recv_sems.at[q], device_id=dev, device_id_type=MESH)
        cp.wait_send()
        cp.wait_recv()


def _push(bufs, plan, ncopies, name):
    n = len(bufs)

    def body(*refs):
        outs = refs[n:2 * n]
        send_sems, recv_sems = refs[2 * n:]
        _start_all(plan, outs, send_sems, recv_sems)
        _wait_all(plan, outs, send_sems, recv_sems)

    return pl.pallas_call(
        body,
        in_specs=[ANY] * n,
        out_specs=[ANY] * n,
        out_shape=[jax.ShapeDtypeStruct(b.shape, b.dtype) for b in bufs],
        input_output_aliases={t: t for t in range(n)},
        scratch_shapes=[pltpu.SemaphoreType.DMA((ncopies,)), pltpu.SemaphoreType.DMA((ncopies,))],
        name=name,
    )(*bufs)


def _half_slices_plan(onward):
    def plan(refs):
        ref, = refs
        x, y, c = _position()
        R2, C4 = ref.shape[0] // 2, ref.shape[1] // N_CHIPS
        out = []
        for k in (4, 2, 6):
            px, py, _ = _peer(k)
            chip = (2 * px + py) if onward else (2 * x + y)
            half = ref.at[pl.ds(c * R2, R2), pl.ds(pl.multiple_of(chip * C4, 128), C4)]
            out.append((half, half, (x, y, 1 - c) if onward else (px, py, c)))
        return out
    return plan


def _push_start(bufs, plan, ncopies, name, after=None):
    n = len(bufs)
    extra = [] if after is None else [after]

    def body(*refs):
        ins = refs[:n]
        first_out = n + len(extra)
        send_sems, recv_sems, token = refs[first_out], refs[first_out + 1], refs[-1]
        _start_all(plan, ins, send_sems, recv_sems)
        token[...] = jnp.zeros_like(token)

    res = pl.pallas_call(
        body,
        name=name,
        out_shape=(pltpu.SemaphoreType.DMA((ncopies,)), pltpu.SemaphoreType.DMA((ncopies,)), *[pltpu.HBM(b.shape, b.dtype) for b in bufs],
                   jax.ShapeDtypeStruct((8, 128), F32)),
        in_specs=[HBM] * n + [ANY] * len(extra),
        out_specs=(SEM, SEM, *[HBM] * n, pl.BlockSpec(memory_space=pltpu.VMEM)),
        input_output_aliases={t: t + 2 for t in range(n)},
        compiler_params=pltpu.CompilerParams(has_side_effects=DATAFLOW),
    )(*[pltpu.with_memory_space_constraint(b, pltpu.HBM) for b in bufs], *extra)
    return res[0], res[1], list(res[2:2 + n]), res[-1]


def _push_wait(send_sems, recv_sems, bufs, plan, after, name):
    n = len(bufs)
    after = list(after) if isinstance(after, (list, tuple)) else [after]

    def body(*refs):
        ins = refs[:n]
        _wait_all(plan, ins, refs[n], refs[n + 1])

    return pl.pallas_call(
        body,
        name=name,
        out_shape=tuple(pltpu.HBM(b.shape, b.dtype) for b in bufs),
        in_specs=[HBM] * n + [SEM, SEM] + [ANY] * len(after),
        out_specs=tuple([HBM] * n),
        input_output_aliases={t: t for t in range(n)},
        compiler_params=pltpu.CompilerParams(has_side_effects=DATAFLOW),
    )(*bufs, send_sems, recv_sems, *after)


EXCHANGE_CHUNKS = 2


def _exchange_plan(refs):
    x, y, c = _position()
    out = []
    for ref in refs:
        rows = ref.shape[0] // (2 * EXCHANGE_CHUNKS)
        for q in range(EXCHANGE_CHUNKS):
            mine = ref.at[pl.ds((c * EXCHANGE_CHUNKS + q) * rows, rows), :]
            out.append((mine, mine, (x, y, 1 - c)))
    return out


LATE_WEIGHTS = (("w_pool_lin", "w_pool_out", "w_attn_out", "w_out"), ("w_up", "conv_w", "w_down"))


def _local_step(x, tgt, w, late_weights, send):
    S, D = x.shape
    PW = w["pool_scale"].shape[1]
    o_q = PW
    o_g = PW + 3 * ATTN_WIDTH
    QKV = 3 * ATTN_WIDTH

    h1 = _rms_fwd(x, w["g_mix"], "rms1")
    w = dict(w, **late_weights("w_in", h1))
    proj_tiles = (_tile(S, 1024), 512, D)
    started = w.get("late_started")
    u = _mm(h1, w["w_in"], mode="nn", dims=(S, PW, D), tiles=proj_tiles, out_dtypes=(F32,), after=started, name="proj_u")
    qkv = _mm(h1, w["w_in"], mode="nn", dims=(S, QKV, D), tiles=proj_tiles, b_off=(0, o_q), after=started, name="proj_qkv")

    def gate_epilogue(acc, ex, outs):
        outs[0][...] = (1.0 / (1.0 + jnp.exp(-(acc + ex[0][...])))).astype(outs[0].dtype)

    gates = _mm(h1, w["w_in"], mode="nn", dims=(S, 2 * D, D), tiles=proj_tiles, b_off=(0, o_g), epilogue=gate_epilogue,
                extras=[(w["b_gate"], "n", (0, 0))], name="proj_gates")

    os_, lses = [], []
    for gi, (_, d) in enumerate(ATTN_GROUPS):
        o, lse = _attn_fwd(qkv, d, gi, f"attn_fwd{gi}")
        os_.append(o)
        lses.append(lse)
    attn, lse_tot = _attn_merge(os_, lses, "attn_merge")

    w = dict(w, **late_weights(0, attn))
    pool_out = _pool_fwd(u, w["w_pool_lin"], w["pool_scale"], "pool_fwd")
    y_pool = _mm(pool_out, w["w_pool_out"], mode="nn", dims=(S, D, PW), name="y_pool")

    def mix_epilogue(acc, ex, outs):
        outs[0][...] = acc.astype(BF16)
        outs[1][...] = (ex[0][...].astype(F32) * ex[2][...].astype(F32) + ex[1][...].astype(F32) * acc).astype(BF16)

    y_attn, mixed = _mm(attn, w["w_attn_out"], mode="nn", dims=(S, D, GROUP_WIDTH), out_dtypes=(BF16, BF16), epilogue=mix_epilogue,
                        extras=[(gates, "mn", (0, 0)), (gates, "mn", (0, D)), (y_pool, "mn", (0, 0))], name="y_attn_mix")

    def residual_epilogue(acc, ex, outs):
        outs[0][...] = ex[0][...] + acc

    x2 = _mm(mixed, w["w_out"], mode="nn", dims=(S, D, D), out_dtypes=(F32,), epilogue=residual_epilogue, extras=[(x, "mn", (0, 0))], name="out_proj")

    h2 = _rms_fwd(x2, w["g_ffn"], "rms2")
    w = dict(w, **late_weights(1, h2))
    F = w["w_down"].shape[0]
    up = _mm(h2, w["w_up"], mode="nn", dims=(S, 2 * F, D), name="up_proj")
    f, act_a, act_b = _convglu_fwd(up, w["conv_w"], w["conv_b"], "convglu_fwd")
    x3 = _mm(f, w["w_down"], mode="nn", dims=(S, D, F), out_dtypes=(F32,), epilogue=residual_epilogue, extras=[(x2, "mn", (0, 0))], name="down_proj")

    g = {}
    dx3b, g["g_final"], loss_cols = _loss_head(x3, tgt, w["g_final"], "loss_head")

    g["w_down"] = _mm(f, dx3b, mode="tn", dims=(F, D, S), name="dw_down")
    sent = send(("w_down",), g)
    df = _mm(dx3b, w["w_down"], mode="nt", dims=(S, F, D), name="d_f")
    dup, g["conv_b"], g["conv_w"] = _convglu_bwd(df, act_a, act_b, up, w["conv_w"] + sent, "convglu_bwd")
    g["w_up"] = _mm(h2, dup, mode="tn", dims=(D, 2 * F, S), name="dw_up")
    sent = send(("w_up",), g)
    dh2 = _mm(dup, w["w_up"], mode="nt", dims=(S, D, 2 * F), name="d_h2")
    dx2b, g["g_ffn"] = _rms_bwd(dh2, x2, w["g_ffn"] + sent, dx3b, "rms2_bwd", BF16)

    g["w_out"] = _mm(mixed, dx2b, mode="tn", dims=(D, D, S), name="dw_out")
    dmixed = _mm(dx2b, w["w_out"], mode="nt", dims=(S, D, D), name="d_mixed")
    IN = w["w_in"].shape[1]
    dy_both, dproj, g["b_gate"] = _gate_bwd(dmixed, gates, y_pool, y_attn, IN, "gate_bwd")

    g["w_pool_out"] = _mm(pool_out, dy_both, mode="tn", dims=(PW, D, S), name="dw_pool_out")
    g["w_attn_out"] = _mm(attn, dy_both, mode="tn", dims=(GROUP_WIDTH, D, S), b_off=(0, D), name="dw_attn_out")
    sent = send(("w_out", "w_pool_out", "w_attn_out"), g)
    dpool = _mm(dy_both, w["w_pool_out"], mode="nt", dims=(S, PW, D), name="d_pool")
    dattn = _mm(dy_both, w["w_attn_out"], mode="nt", dims=(S, GROUP_WIDTH, D), a_off=(0, D), name="d_attn")

    dproj, g["w_pool_lin"], g["pool_scale"] = _pool_bwd(u, dpool, w["w_pool_lin"], w["pool_scale"] + sent, dproj, "pool_bwd")
    g["loss_cols"] = loss_cols
    sent = send("small", g)

    for gi, (_, d) in enumerate(ATTN_GROUPS):
        dproj = _attn_bwd(qkv, dattn, attn, lse_tot, dproj, d, gi, PW // HEAD_DIM, f"attn_bwd{gi}")

    g["w_in"] = _mm(h1, dproj, mode="tn", dims=(D, IN, S), name="dw_in")
    sent = sent + send(("w_in",), g)
    dh1 = _mm(dproj, w["w_in"], mode="nt", dims=(S, D, IN), tiles=(_tile(S, 1024), _tile(D, 2048), _tile(IN, 2432)), name="d_h1")
    (grad_x, g["g_mix"]) = _rms_bwd(dh1, x, w["g_mix"] + sent, dx2b, "rms1_bwd", F32)
    return loss_cols, grad_x, g


BIG = ("w_in", "w_pool_out", "w_attn_out", "w_out", "w_up", "w_down")
BIG_AXIS = {"w_in": 1, "w_pool_out": 1, "w_attn_out": 1, "w_out": 0, "w_up": 1, "w_down": 0}
GATHER_AXIS = dict(BIG_AXIS, w_pool_lin=1, conv_w=1)
SMALL = ("loss_cols", "b_gate", "w_pool_lin", "pool_scale", "g_ffn", "conv_w", "conv_b", "g_final")
SMALL_COLS = 1024
ORDER = ("g_mix", "w_in", "b_gate", "w_pool_lin", "pool_scale", "w_pool_out", "w_attn_out", "w_out", "g_ffn", "w_up", "conv_w", "conv_b", "w_down", "g_final")


def _as_rows(parts):
    flat = jnp.concatenate([p.astype(F32).reshape(-1) for p in parts])
    rows = -(-flat.shape[0] // (8 * SMALL_COLS)) * 8
    return jnp.pad(flat, (0, rows * SMALL_COLS - flat.shape[0])).reshape(rows, SMALL_COLS)


def kernel(x, g_mix, w_in, b_gate, w_pool_lin, pool_scale, w_pool_out, w_attn_out, w_out, g_ffn, w_up, conv_w, conv_b, w_down, g_final, loss_target, m_g_mix, m_w_in, m_b_gate, m_w_pool_lin, m_pool_scale, m_w_pool_out, m_w_attn_out, m_w_out, m_g_ffn, m_w_up, m_conv_w, m_conv_b, m_w_down, m_g_final, v_g_mix, v_w_in, v_b_gate, v_w_pool_lin, v_pool_scale, v_w_pool_out, v_w_attn_out, v_w_out, v_g_ffn, v_w_up, v_conv_w, v_conv_b, v_w_down, v_g_final):
    shard = dict(g_mix=g_mix, w_in=w_in, b_gate=b_gate, w_pool_lin=w_pool_lin, pool_scale=pool_scale, w_pool_out=w_pool_out, w_attn_out=w_attn_out,
                 w_out=w_out, g_ffn=g_ffn, w_up=w_up, conv_w=conv_w, conv_b=conv_b, w_down=w_down, g_final=g_final)
    mom = dict(g_mix=m_g_mix, w_in=m_w_in, b_gate=m_b_gate, w_pool_lin=m_w_pool_lin, pool_scale=m_pool_scale, w_pool_out=m_w_pool_out, w_attn_out=m_w_attn_out,
               w_out=m_w_out, g_ffn=m_g_ffn, w_up=m_w_up, conv_w=m_conv_w, conv_b=m_conv_b, w_down=m_w_down, g_final=m_g_final)
    vel = dict(g_mix=v_g_mix, w_in=v_w_in, b_gate=v_b_gate, w_pool_lin=v_w_pool_lin, pool_scale=v_pool_scale, w_pool_out=v_w_pool_out, w_attn_out=v_w_attn_out,
               w_out=v_w_out, g_ffn=v_g_ffn, w_up=v_w_up, conv_w=v_conv_w, conv_b=v_conv_b, w_down=v_w_down, g_final=v_g_final)
    chip = 2 * lax.axis_index("x") + lax.axis_index("y")
    pos = jnp.stack([chip, lax.axis_index("c")]).astype(jnp.int32)
    me = (2 * chip + lax.axis_index("c")).astype(jnp.int32).reshape(1)
    D = x.shape[2]

    out_plan, on_plan = _half_slices_plan(False), _half_slices_plan(True)
    in_send, in_recv, in_bufs, in_token = _push_start([_place(shard["w_in"][0], GATHER_AXIS["w_in"], pos, BF16, "place_w_in")], out_plan, 3,
                                                      "comm_gather_w_in_start")
    placed = {k: _place(shard[k][0], GATHER_AXIS[k], pos, F32 if k == "conv_w" else BF16, f"place_{k}", after=in_token)
              for names in LATE_WEIGHTS for k in names}
    late = []

    def late_weights(stage, after):
        if stage == "w_in":
            landed = _push_wait(in_send, in_recv, in_bufs, out_plan, [after] + list(placed.values()), "comm_gather_w_in_wait")
            w_in_full, = _push(list(landed), on_plan, 3, "comm_gather_w_in_pass")
            late_token, prior = 0.0, w_in_full
            for st, names in enumerate(LATE_WEIGHTS):
                plan = _gather_plan([GATHER_AXIS[k] for k in names])
                send_sems, recv_sems, bufs, token = _push_start([placed[k] for k in names], plan, 3 * len(names), f"comm_gather_late{st}_start", after=prior)
                late.append((names, send_sems, recv_sems, bufs, plan))
                late_token, prior = late_token + token[0, 0], token
            return dict(w_in=w_in_full, b_gate=shard["b_gate"] + late_token, late_started=prior)
        names, send_sems, recv_sems, bufs, plan = late[stage]
        return dict(zip(names, _push_wait(send_sems, recv_sems, bufs, plan, after, f"comm_gather_late{stage}_wait")))

    pending = []

    def send(names, g):
        if names == "small":
            bufs = [_as_rows([g[k] for k in SMALL])]
            bufs.append(lax.empty((N_DEV - 1,) + bufs[0].shape, F32))
            plan, tag = _broadcast_plan, "small"
        else:
            bufs = [g[k] for k in names]
            for k in names:
                R, C = g[k].shape
                piece = (R // (2 * N_CHIPS), C) if BIG_AXIS[k] == 0 else (R // 2, C // N_CHIPS)
                bufs.append(lax.empty((N_DEV - 1,) + piece, BF16))
            plan, tag = _scatter_plan([BIG_AXIS[k] for k in names]), names[0]
        ncopies = (N_DEV - 1) * (len(bufs) // 2)
        send_sems, recv_sems, thru, token = _push_start(bufs, plan, ncopies, f"comm_scatter_start_{tag}")
        pending.append((names, send_sems, recv_sems, thru, plan, tag))
        return token[0, 0]

    w0 = dict(g_mix=shard["g_mix"] + in_token[0, 0], pool_scale=shard["pool_scale"], g_ffn=shard["g_ffn"],
              conv_b=shard["conv_b"], g_final=shard["g_final"].reshape(1, D))
    _, grad_x, gr = _local_step(x[0], loss_target[0], w0, late_weights, send)

    halves, small_parts = {}, None
    for names, send_sems, recv_sems, thru, plan, tag in pending:
        done = _push_wait(send_sems, recv_sems, thru, plan, grad_x, f"comm_scatter_wait_{tag}")
        if names == "small":
            small_parts = _sum_small(done[0], done[1], me, "sum_small").reshape(-1)
        else:
            m = len(names)
            for t, k in enumerate(names):
                halves[k] = _sum_pieces(done[t], BIG_AXIS[k], done[m + t], pos, f"sum_{k}")
    g_mix_own = _as_rows([gr["g_mix"]])
    _, g_mix_recv = _push([g_mix_own, lax.empty((N_DEV - 1,) + g_mix_own.shape, F32)], _broadcast_plan, N_DEV - 1, "comm_gather_g_mix")
    g_mix_sum = _sum_small(g_mix_own, g_mix_recv, me, "sum_g_mix").reshape(-1)[:D]
    wholes = _push([halves[k] for k in BIG], _exchange_plan, EXCHANGE_CHUNKS * len(BIG), "comm_exchange_halves")

    grads = {"g_mix": g_mix_sum.reshape(shard["g_mix"].shape)}
    for k, whole in zip(BIG, wholes):
        grads[k] = whole.reshape(shard[k].shape)
    off = 0
    loss = None
    for k in SMALL:
        sz = math.prod(gr[k].shape)
        fullg = small_parts[off:off + sz].reshape(gr[k].shape)
        off += sz
        if k == "loss_cols":
            loss = jnp.sum(fullg)
            continue
        if k in ("w_pool_lin", "conv_w"):
            n = shard[k].shape[2]
            fullg = lax.dynamic_slice_in_dim(fullg, chip * n, n, axis=1)
        grads[k] = fullg.reshape(shard[k].shape)

    deltas, new_m, new_v = {}, {}, {}
    for k in ORDER:
        shp = shard[k].shape
        two_d = (-1, shp[-1])
        dl, nm, nv = _adamw(shard[k].reshape(two_d), grads[k].reshape(two_d), mom[k].reshape(two_d), vel[k].reshape(two_d), f"adamw_{k}")
        deltas[k], new_m[k], new_v[k] = dl.reshape(shp), nm.reshape(shp), nv.reshape(shp)

    return (loss, grad_x[None], *[grads[k] for k in ORDER], *[deltas[k] for k in ORDER], *[new_m[k] for k in ORDER], *[new_v[k] for k in ORDER])
```

```python
import functools
import math

import jax
import jax.numpy as jnp
from jax import lax
from jax.experimental import pallas as pl
from jax.experimental.pallas import tpu as pltpu

F32 = jnp.float32
BF16 = jnp.bfloat16

RMS_EPS = 1e-6
POOL_WINDOWS = (2, 4, 8, 16)
ATTN_GROUPS = ((128, 1), (512, 4), (2048, 16))
HEADS_PER_GROUP = 4
HEAD_DIM = 128
N_ATTN_HEADS = HEADS_PER_GROUP * len(ATTN_GROUPS)
SPAN = 128
GROUP_WIDTH = HEADS_PER_GROUP * HEAD_DIM
ATTN_WIDTH = N_ATTN_HEADS * HEAD_DIM
ATTN_SCALE = HEAD_DIM ** -0.5
NEG_BIG = -1e30
ALIBI_SLOPES = tuple(2.0 ** (-8.0 * (h + 1) / N_ATTN_HEADS) for h in range(N_ATTN_HEADS))

ADAM_LR = 0.001
ADAM_B1 = 0.9
ADAM_B2 = 0.999
ADAM_EPS = 1e-08
ADAM_WD = 0.01
ADAM_STEP = 10

INV_SQRT2 = 1.0 / math.sqrt(2.0)
INV_SQRT_2PI = 1.0 / math.sqrt(2.0 * math.pi)

HALO = 16
VMEM_LIMIT = 56 * 1024 * 1024
N_CHIPS = 4
N_DEV = 8
MESH = pl.DeviceIdType.MESH
ANY = pl.BlockSpec(memory_space=pl.ANY)


def _cparams(*sem):
    return pltpu.CompilerParams(dimension_semantics=sem, vmem_limit_bytes=VMEM_LIMIT)


def _tile(n, pref, mult=128):
    t = (min(pref, n) // mult) * mult
    while t >= mult:
        if n % t == 0:
            return t
        t -= mult
    return n


def _dot(a, b, contract):
    return lax.dot_general(a, b, (contract, ((), ())), preferred_element_type=F32)


def _dot_nn(a, b):
    return _dot(a, b, ((1,), (0,)))


def _dot_nt(a, b):
    return _dot(a, b, ((1,), (1,)))


def _mm(a, b, *, mode, dims, name, tiles=None, out_dtypes=(BF16,), epilogue=None, extras=(), a_off=(0, 0), b_off=(0, 0), after=None):
    M, N, K = dims
    if tiles is None:
        tiles = (_tile(M, 1408), _tile(N, 2816), _tile(K, 512)) if mode == "tn" else (_tile(M, 1024), _tile(N, 1536), _tile(K, 2816))
    tm, tn, tk = tiles
    assert M % tm == 0 and N % tn == 0 and K % tk == 0, (name, dims, tiles)
    nk = K // tk
    if mode == "nn":
        ab, bb, contract = (tm, tk), (tk, tn), ((1,), (0,))
        amap = lambda i, j, k: (i + a_off[0] // tm, k + a_off[1] // tk)
        bmap = lambda i, j, k: (k + b_off[0] // tk, j + b_off[1] // tn)
    elif mode == "nt":
        ab, bb, contract = (tm, tk), (tn, tk), ((1,), (1,))
        amap = lambda i, j, k: (i + a_off[0] // tm, k + a_off[1] // tk)
        bmap = lambda i, j, k: (j + b_off[0] // tn, k + b_off[1] // tk)
    else:
        ab, bb, contract = (tk, tm), (tk, tn), ((0,), (0,))
        amap = lambda i, j, k: (k + a_off[0] // tk, i + a_off[1] // tm)
        bmap = lambda i, j, k: (k + b_off[0] // tk, j + b_off[1] // tn)
    assert a_off[0] % ab[0] == 0 and a_off[1] % ab[1] == 0 and b_off[0] % bb[0] == 0 and b_off[1] % bb[1] == 0, name
    in_specs = [pl.BlockSpec(ab, amap), pl.BlockSpec(bb, bmap)]
    ex_arrays = []
    for arr, kind, off in extras:
        if kind == "mn":
            assert off[0] % tm == 0 and off[1] % tn == 0, name
            in_specs.append(pl.BlockSpec((tm, tn), lambda i, j, k, off=off: (i + off[0] // tm, j + off[1] // tn)))
        else:
            assert off[1] % tn == 0, name
            in_specs.append(pl.BlockSpec((1, tn), lambda i, j, k, off=off: (0, j + off[1] // tn)))
        ex_arrays.append(arr)
    ne, no = len(ex_arrays), len(out_dtypes)
    if after is not None:
        in_specs.append(ANY)
        ex_arrays.append(after)
    first_out = 2 + len(ex_arrays)
    if epilogue is None:
        def epilogue(acc, ex, outs):
            outs[0][...] = acc.astype(outs[0].dtype)

    def body(*refs):
        a_ref, b_ref = refs[0], refs[1]
        ex, outs = refs[2:2 + ne], refs[first_out:first_out + no]
        if nk == 1:
            epilogue(_dot(a_ref[...], b_ref[...], contract), ex, outs)
            return
        acc = refs[-1]
        k = pl.program_id(2)
        if nk <= 4:
            part = _dot(a_ref[...], b_ref[...], contract)

            @pl.when(k == 0)
            def _():
                acc[...] = part

            @pl.when(jnp.logical_and(k > 0, k < nk - 1))
            def _():
                acc[...] += part

            @pl.when(k == nk - 1)
            def _():
                epilogue(acc[...] + part, ex, outs)
        else:
            @pl.when(k == 0)
            def _():
                acc[...] = _dot(a_ref[...], b_ref[...], contract)

            @pl.when(k > 0)
            def _():
                acc[...] += _dot(a_ref[...], b_ref[...], contract)

            @pl.when(k == nk - 1)
            def _():
                epilogue(acc[...], ex, outs)

    res = pl.pallas_call(
        body,
        grid=(M // tm, N // tn, nk),
        in_specs=in_specs,
        out_specs=[pl.BlockSpec((tm, tn), lambda i, j, k: (i, j)) for _ in out_dtypes],
        out_shape=[jax.ShapeDtypeStruct((M, N), dt) for dt in out_dtypes],
        scratch_shapes=[pltpu.VMEM((tm, tn), F32)] if nk > 1 else [],
        compiler_params=_cparams("parallel", "parallel", "arbitrary"),
        name=name,
    )(a, b, *ex_arrays)
    return res[0] if no == 1 else res


def _rms_fwd(x, g, name):
    S, D = x.shape
    tm = _tile(S, 512)

    def body(x_ref, g_ref, h_ref):
        xv = x_ref[...]
        r = lax.rsqrt(jnp.mean(xv * xv, axis=-1, keepdims=True) + RMS_EPS)
        h_ref[...] = (xv * r * g_ref[...]).astype(h_ref.dtype)

    return pl.pallas_call(
        body,
        grid=(S // tm,),
        in_specs=[pl.BlockSpec((tm, D), lambda i: (i, 0)), pl.BlockSpec((1, D), lambda i: (0, 0))],
        out_specs=pl.BlockSpec((tm, D), lambda i: (i, 0)),
        out_shape=jax.ShapeDtypeStruct((S, D), BF16),
        compiler_params=_cparams("parallel"),
        name=name,
    )(x, g)


def _rms_bwd(dh, x, g, dres, name, out_dtype):
    S, D = x.shape
    tm = _tile(S, 512)

    def body(dh_ref, x_ref, g_ref, dres_ref, dx_ref, dg_ref):
        xv = x_ref[...]
        r = lax.rsqrt(jnp.mean(xv * xv, axis=-1, keepdims=True) + RMS_EPS)
        xr = xv * r
        dhv = dh_ref[...].astype(F32)

        @pl.when(pl.program_id(0) == 0)
        def _():
            dg_ref[...] = jnp.zeros_like(dg_ref)

        dg_ref[...] += jnp.sum(dhv * xr, axis=0, keepdims=True)
        u = dhv * g_ref[...]
        c = jnp.mean(u * xr, axis=-1, keepdims=True)
        dx_ref[...] = (dres_ref[...].astype(F32) + r * (u - xr * c)).astype(dx_ref.dtype)

    row = pl.BlockSpec((tm, D), lambda i: (i, 0))
    vec = pl.BlockSpec((1, D), lambda i: (0, 0))
    return pl.pallas_call(
        body,
        grid=(S // tm,),
        in_specs=[row, row, vec, row],
        out_specs=[row, vec],
        out_shape=[jax.ShapeDtypeStruct((S, D), out_dtype), jax.ShapeDtypeStruct((1, D), F32)],
        compiler_params=_cparams("arbitrary"),
        name=name,
    )(dh, x, g, dres)


def _loss_head(x3, tgt, g, name):
    S, D = x3.shape
    tm = _tile(S, 512)

    def body(x_ref, t_ref, g_ref, dxb_ref, dg_ref, loss_ref):
        xv = x_ref[...]
        gv = g_ref[...]
        r = lax.rsqrt(jnp.mean(xv * xv, axis=-1, keepdims=True) + RMS_EPS)
        xr = xv * r
        e = xr * gv - t_ref[...]

        @pl.when(pl.program_id(0) == 0)
        def _():
            dg_ref[...] = jnp.zeros_like(dg_ref)
            loss_ref[...] = jnp.zeros_like(loss_ref)

        loss_ref[...] += jnp.sum(e * e, axis=0, keepdims=True) * (0.5 / D)
        dy = e * (1.0 / D)
        dg_ref[...] += jnp.sum(dy * xr, axis=0, keepdims=True)
        u = dy * gv
        c = jnp.mean(u * xr, axis=-1, keepdims=True)
        dxb_ref[...] = (r * (u - xr * c)).astype(BF16)

    row = pl.BlockSpec((tm, D), lambda i: (i, 0))
    vec = pl.BlockSpec((1, D), lambda i: (0, 0))
    return pl.pallas_call(
        body,
        grid=(S // tm,),
        in_specs=[row, row, vec],
        out_specs=[row, vec, vec],
        out_shape=[jax.ShapeDtypeStruct((S, D), BF16), jax.ShapeDtypeStruct((1, D), F32), jax.ShapeDtypeStruct((1, D), F32)],
        compiler_params=_cparams("arbitrary"),
        name=name,
    )(x3, tgt, g)


SHIFT_ROWS = 256


def _shift_matrix():
    i = lax.broadcasted_iota(jnp.int32, (2 * SHIFT_ROWS, SHIFT_ROWS), 0)
    j = lax.broadcasted_iota(jnp.int32, (2 * SHIFT_ROWS, SHIFT_ROWS), 1)
    src = jnp.where(i < SHIFT_ROWS, i - 1, i - SHIFT_ROWS - 2)
    return (j == src).astype(BF16)


def _conv_taps(cur_ref, halo_ref, w_ref, b_ref, first, shift):
    w, bias = w_ref[...], b_ref[...]
    before = jnp.where(first, 0.0, halo_ref[...].astype(F32)[HALO - 8:])
    sub = lax.broadcasted_iota(jnp.int32, before.shape, 0)
    out = []
    for blk in range(cur_ref.shape[0] // SHIFT_ROWS):
        xb = cur_ref[blk * SHIFT_ROWS:(blk + 1) * SHIFT_ROWS, :]
        xf = xb.astype(F32)
        both = _dot_nn(shift, xb)
        p1, p2 = both[:SHIFT_ROWS], both[SHIFT_ROWS:]
        p1 = jnp.concatenate([jnp.where(sub == 0, pltpu.roll(before, 1, 0), p1[:8]), p1[8:]], axis=0)
        p2 = jnp.concatenate([jnp.where(sub < 2, pltpu.roll(before, 2, 0), p2[:8]), p2[8:]], axis=0)
        out.append(bias + w[0:1] * p2 + w[1:2] * p1 + w[2:3] * xf)
        before = xf[SHIFT_ROWS - 8:]
    return jnp.concatenate(out, axis=0)


def _convglu_fwd(up, cw, cb, name):
    S, F2 = up.shape
    F = F2 // 2
    tm, tn = _tile(S, 1024), _tile(F, 512)
    nj, hb = F // tn, tm // HALO

    def body(ua, ub, ha, hb_, wa, wb, ba, bb, f_ref, a_ref, b_ref):
        first = pl.program_id(0) == 0
        shift = _shift_matrix()
        a = _conv_taps(ua, ha, wa, ba, first, shift)
        b = _conv_taps(ub, hb_, wb, bb, first, shift)
        f_ref[...] = (0.5 * a * (1.0 + lax.erf(a * INV_SQRT2)) * b).astype(f_ref.dtype)
        a_ref[...] = a.astype(a_ref.dtype)
        b_ref[...] = b.astype(b_ref.dtype)

    tile = pl.BlockSpec((tm, tn), lambda i, j: (i, j))
    return pl.pallas_call(
        body,
        grid=(S // tm, nj),
        in_specs=[
            tile,
            pl.BlockSpec((tm, tn), lambda i, j: (i, j + nj)),
            pl.BlockSpec((HALO, tn), lambda i, j: (jnp.maximum(i * hb - 1, 0), j)),
            pl.BlockSpec((HALO, tn), lambda i, j: (jnp.maximum(i * hb - 1, 0), j + nj)),
            pl.BlockSpec((3, tn), lambda i, j: (0, j)),
            pl.BlockSpec((3, tn), lambda i, j: (0, j + nj)),
            pl.BlockSpec((1, tn), lambda i, j: (0, j)),
            pl.BlockSpec((1, tn), lambda i, j: (0, j + nj)),
        ],
        out_specs=[tile, tile, tile],
        out_shape=[jax.ShapeDtypeStruct((S, F), BF16)] * 3,
        compiler_params=_cparams("parallel", "parallel"),
        name=name,
    )(up, up, up, up, cw, cw, cb, cb)


def _convglu_bwd(df, a, b, up, cw, name):
    S, F = df.shape
    tm, tn = _tile(S, 1024), _tile(F, 512)
    nj, ni, hb = F // tn, S // tm, tm // HALO
    n = tm + HALO

    def body(df_ref, dfn_ref, a_ref, an_ref, b_ref, bn_ref, up_ref, w_ref, o_ref, db_ref, dw_ref):
        j, i = pl.program_id(0), pl.program_id(1)
        last = i == ni - 1

        def rows(c_ref, n_ref):
            return jnp.concatenate([c_ref[...].astype(F32), jnp.where(last, 0.0, n_ref[...].astype(F32))], axis=0)

        @pl.when(i == 0)
        def _():
            db_ref[...] = jnp.zeros_like(db_ref)
            dw_ref[...] = jnp.zeros_like(dw_ref)

        def finish(d):
            i_ = lax.broadcasted_iota(jnp.int32, (2 * SHIFT_ROWS, SHIFT_ROWS), 0)
            j_ = lax.broadcasted_iota(jnp.int32, (2 * SHIFT_ROWS, SHIFT_ROWS), 1)
            ahead = (j_ == jnp.where(i_ < SHIFT_ROWS, i_ + 1, i_ - SHIFT_ROWS + 2)).astype(BF16)
            db = d.astype(BF16)
            sub = lax.broadcasted_iota(jnp.int32, (8, tn), 0)
            d1, d2 = [], []
            for blk in range(tm // SHIFT_ROWS):
                lo, hi = blk * SHIFT_ROWS, (blk + 1) * SHIFT_ROWS
                both = _dot_nn(ahead, db[lo:hi])
                n1, n2 = both[:SHIFT_ROWS], both[SHIFT_ROWS:]
                after = db[hi:hi + HALO].astype(F32)[:8]
                d1 += [n1[:-8], jnp.where(sub == 7, pltpu.roll(after, 7, 0), n1[-8:])]
                d2 += [n2[:-8], jnp.where(sub >= 6, pltpu.roll(after, 6, 0), n2[-8:])]
            d0, d1, d2 = d[:tm], jnp.concatenate(d1, axis=0), jnp.concatenate(d2, axis=0)
            w = w_ref[...]
            o_ref[...] = (w[2:3] * d0 + w[1:2] * d1 + w[0:1] * d2).astype(o_ref.dtype)
            upv = up_ref[...].astype(F32)
            db_ref[...] += jnp.sum(d0, axis=0, keepdims=True)
            dw_ref[0:1, :] += jnp.sum(d2 * upv, axis=0, keepdims=True)
            dw_ref[1:2, :] += jnp.sum(d1 * upv, axis=0, keepdims=True)
            dw_ref[2:3, :] += jnp.sum(d0 * upv, axis=0, keepdims=True)

        av, dfv = rows(a_ref, an_ref), rows(df_ref, dfn_ref)
        cdf = 0.5 * (1.0 + lax.erf(av * INV_SQRT2))

        @pl.when(j < nj)
        def _():
            pdf = jnp.exp(-0.5 * av * av) * INV_SQRT_2PI
            finish(dfv * rows(b_ref, bn_ref) * (cdf + av * pdf))

        @pl.when(j >= nj)
        def _():
            finish(dfv * (av * cdf))

    jh = lambda j: lax.rem(j, nj)
    nxt = lambda i: jnp.minimum((i + 1) * hb, S // HALO - 1)
    cur = pl.BlockSpec((tm, tn), lambda j, i: (i, jh(j)))
    halo = pl.BlockSpec((HALO, tn), lambda j, i: (nxt(i), jh(j)))
    return pl.pallas_call(
        body,
        grid=(2 * nj, ni),
        in_specs=[cur, halo, cur, halo, cur, halo, pl.BlockSpec((tm, tn), lambda j, i: (i, j)), pl.BlockSpec((3, tn), lambda j, i: (0, j))],
        out_specs=[pl.BlockSpec((tm, tn), lambda j, i: (i, j)), pl.BlockSpec((1, tn), lambda j, i: (0, j)), pl.BlockSpec((3, tn), lambda j, i: (0, j))],
        out_shape=[jax.ShapeDtypeStruct((S, 2 * F), BF16), jax.ShapeDtypeStruct((1, 2 * F), F32), jax.ShapeDtypeStruct((3, 2 * F), F32)],
        compiler_params=_cparams("parallel", "arbitrary"),
        name=name,
    )(df, df, a, a, b, b, up, cw)


def _gate_bwd(dmixed, gates, y_pool, y_attn, in_width, name):
    S, D = dmixed.shape
    tm, tn = _tile(S, 1024), _tile(D, 512)
    nj = D // tn
    pre0 = (in_width - 2 * D) // tn
    assert pre0 * tn == in_width - 2 * D

    def body(dm_ref, g_ref, yp_ref, ya_ref, dy_ref, dpre_ref, db_ref):
        j = pl.program_id(0)

        @pl.when(pl.program_id(1) == 0)
        def _():
            db_ref[...] = jnp.zeros_like(db_ref)

        def run(y_ref):
            dm = dm_ref[...].astype(F32)
            gv = g_ref[...].astype(F32)
            dy_ref[...] = (dm * gv).astype(BF16)
            dpre = dm * y_ref[...].astype(F32) * gv * (1.0 - gv)
            dpre_ref[...] = dpre.astype(BF16)
            db_ref[...] += jnp.sum(dpre, axis=0, keepdims=True)

        @pl.when(j < nj)
        def _():
            run(yp_ref)

        @pl.when(j >= nj)
        def _():
            run(ya_ref)

    tile2 = pl.BlockSpec((tm, tn), lambda j, i: (i, j))
    return pl.pallas_call(
        body,
        grid=(2 * nj, S // tm),
        in_specs=[
            pl.BlockSpec((tm, tn), lambda j, i: (i, lax.rem(j, nj))),
            tile2,
            pl.BlockSpec((tm, tn), lambda j, i: (i, jnp.minimum(j, nj - 1))),
            pl.BlockSpec((tm, tn), lambda j, i: (i, jnp.maximum(j - nj, 0))),
        ],
        out_specs=[tile2, pl.BlockSpec((tm, tn), lambda j, i: (i, pre0 + j)), pl.BlockSpec((1, tn), lambda j, i: (0, j))],
        out_shape=[jax.ShapeDtypeStruct((S, 2 * D), BF16), jax.ShapeDtypeStruct((S, in_width), BF16), jax.ShapeDtypeStruct((1, 2 * D), F32)],
        compiler_params=_cparams("parallel", "arbitrary"),
        name=name,
    )(dmixed, gates, y_pool, y_attn)


def _pool_counts(i, tm, rows, w):
    t = i * tm + lax.broadcasted_iota(jnp.int32, (rows, 1), 0)
    return jnp.minimum(t + 1, w).astype(F32)


def _pooled_groups(u_ref, uh_ref, i, tm, C):
    cur = u_ref[...]
    halo = jnp.where(i == 0, 0.0, uh_ref[...])
    xx = jnp.concatenate([halo, cur], axis=0)
    out = []
    s = xx
    for gi, w in enumerate(POOL_WINDOWS):
        s = s + pltpu.roll(s, w // 2, 0)
        tot = s[HALO:, 0:C]
        out.append(tot / _pool_counts(i, tm, tm, w) - cur[:, gi * C:(gi + 1) * C])
        s = s[:, C:] if gi + 1 < len(POOL_WINDOWS) else s
    return out


def _pool_fwd(u, wl, scale, name):
    S, PW = u.shape
    C = PW // len(POOL_WINDOWS)
    tm = _tile(S, 512)
    hb = tm // HALO

    def body(u_ref, uh_ref, wl_ref, sc_ref, o_ref):
        i = pl.program_id(0)
        pooled = _pooled_groups(u_ref, uh_ref, i, tm, C)
        for gi in range(len(POOL_WINDOWS)):
            y = _dot_nn(pooled[gi].astype(BF16), wl_ref[gi])
            o_ref[:, gi * C:(gi + 1) * C] = (y * sc_ref[:, gi * C:(gi + 1) * C]).astype(o_ref.dtype)

    return pl.pallas_call(
        body,
        grid=(S // tm,),
        in_specs=[
            pl.BlockSpec((tm, PW), lambda i: (i, 0)),
            pl.BlockSpec((HALO, PW), lambda i: (jnp.maximum(i * hb - 1, 0), 0)),
            pl.BlockSpec((len(POOL_WINDOWS), C, C), lambda i: (0, 0, 0)),
            pl.BlockSpec((1, PW), lambda i: (0, 0)),
        ],
        out_specs=pl.BlockSpec((tm, PW), lambda i: (i, 0)),
        out_shape=jax.ShapeDtypeStruct((S, PW), BF16),
        compiler_params=_cparams("parallel"),
        name=name,
    )(u, u, wl, scale)


def _pool_bwd(u, dp, wl, scale, dproj, name):
    S, PW = u.shape
    G = len(POOL_WINDOWS)
    C = PW // G
    tm = _tile(S, 512)
    hb, ni = tm // HALO, S // tm
    n = tm + HALO

    def body(u_ref, uh_ref, dp_ref, dpn_ref, wl_ref, sc_ref, _, du_ref, dwl_ref, dsc_ref):
        i = pl.program_id(0)

        @pl.when(i == 0)
        def _():
            dwl_ref[...] = jnp.zeros_like(dwl_ref)
            dsc_ref[...] = jnp.zeros_like(dsc_ref)

        pooled = _pooled_groups(u_ref, uh_ref, i, tm, C)
        dpc = dp_ref[...].astype(F32)
        dpn = jnp.where(i == ni - 1, 0.0, dpn_ref[...].astype(F32))
        sc = sc_ref[...]
        dyl = jnp.concatenate([dpc, dpn], axis=0) * sc
        for gi, w in enumerate(POOL_WINDOWS):
            cols = slice(gi * C, (gi + 1) * C)
            pb = pooled[gi].astype(BF16)
            ylin = _dot_nn(pb, wl_ref[gi])
            dsc_ref[:, cols] += jnp.sum(dpc[:, cols] * ylin, axis=0, keepdims=True)
            dylg = dyl[:, cols].astype(BF16)
            dwl_ref[gi] += _dot(pb, dylg[:tm], ((0,), (0,)))
            dpool = _dot_nt(dylg, wl_ref[gi])
            e = dpool / _pool_counts(i, tm, n, w)
            k = 1
            while k < w:
                e = e + pltpu.roll(e, n - k, 0)
                k *= 2
            du_ref[:, cols] = (e[:tm] - dpool[:tm]).astype(du_ref.dtype)

    return pl.pallas_call(
        body,
        grid=(ni,),
        in_specs=[
            pl.BlockSpec((tm, PW), lambda i: (i, 0)),
            pl.BlockSpec((HALO, PW), lambda i: (jnp.maximum(i * hb - 1, 0), 0)),
            pl.BlockSpec((tm, PW), lambda i: (i, 0)),
            pl.BlockSpec((HALO, PW), lambda i: (jnp.minimum((i + 1) * hb, S // HALO - 1), 0)),
            pl.BlockSpec((G, C, C), lambda i: (0, 0, 0)),
            pl.BlockSpec((1, PW), lambda i: (0, 0)),
            ANY,
        ],
        out_specs=[pl.BlockSpec((tm, PW), lambda i: (i, 0)), pl.BlockSpec((G, C, C), lambda i: (0, 0, 0)), pl.BlockSpec((1, PW), lambda i: (0, 0))],
        out_shape=[jax.ShapeDtypeStruct(dproj.shape, dproj.dtype), jax.ShapeDtypeStruct((G, C, C), F32), jax.ShapeDtypeStruct((1, PW), F32)],
        input_output_aliases={6: 0},
        compiler_params=_cparams("arbitrary"),
        name=name,
    )(u, u, dp, dp, wl, scale, dproj)


def _band_masks():
    ii = lax.broadcasted_iota(jnp.int32, (SPAN, SPAN), 0)
    kk = lax.broadcasted_iota(jnp.int32, (SPAN, SPAN), 1)
    return ((ii + SPAN - kk).astype(F32), kk >= ii), ((ii - kk).astype(F32), kk <= ii)


ATTN_TILE = 16 * SPAN


def _unit_rows(r, b, d, blocks=1):
    return pl.ds(d * SPAN * b + r, blocks * SPAN, stride=d) if d > 1 else pl.ds(SPAN * b, blocks * SPAN)


def _f32_copies(refs, scratch, d):
    if d == 1:
        return list(refs)
    for ref, s in zip(refs, scratch):
        s[...] = ref[...].astype(F32)
    return list(scratch)


def _attn_fwd(qkv, d, g, name):
    S = qkv.shape[0]
    T = min(ATTN_TILE, S)
    P = SPAN * d
    nbk = T // P

    def body(q_ref, k_ref, v_ref, kp_ref, vp_ref, o_ref, lse_ref, *scratch):
        c = pl.program_id(0)
        (jp, mp), (jc, mc) = _band_masks()
        slopes = [ALIBI_SLOPES[g * HEADS_PER_GROUP + h] * d for h in range(HEADS_PER_GROUP)]
        slope = slopes[0]
        for h in range(1, HEADS_PER_GROUP):
            slope = jnp.where(pl.program_id(1) == h, slopes[h], slope)
        q_s, k_s, v_s, kp_s, vp_s = _f32_copies((q_ref, k_ref, v_ref, kp_ref, vp_ref), scratch[:5], d)
        o_s, l_s = (o_ref, lse_ref) if d == 1 else scratch[5:7]
        bias_p, bias_c = jnp.where(mp, -slope * jp, NEG_BIG), jnp.where(mc, -slope * jc, NEG_BIG)
        bias = jnp.concatenate([bias_p, bias_c], axis=1)
        bias_first = jnp.concatenate([jnp.where(c > 0, bias_p, NEG_BIG), bias_c], axis=1)
        for r in range(d):
            for b in range(nbk):
                rows = _unit_rows(r, b, d)
                q = q_s[rows, :].astype(BF16)
                if b == 0:
                    prev = _unit_rows(r, 0, d)
                    kk = jnp.concatenate([kp_s[prev, :], k_s[rows, :]], axis=0).astype(BF16)
                    vv = jnp.concatenate([vp_s[prev, :], v_s[rows, :]], axis=0).astype(BF16)
                else:
                    both = _unit_rows(r, b - 1, d, 2)
                    kk, vv = k_s[both, :].astype(BF16), v_s[both, :].astype(BF16)
                s = _dot_nt(q, kk) * ATTN_SCALE + (bias_first if b == 0 else bias)
                m = jnp.max(s, axis=-1, keepdims=True)
                p = jnp.exp(s - m)
                l = jnp.sum(p, axis=-1, keepdims=True)
                o_s[rows, :] = _dot_nn(p.astype(BF16), vv) / l
                l_s[rows, :] = jnp.broadcast_to(m + jnp.log(l), (SPAN, HEAD_DIM))
        if d > 1:
            o_ref[...] = o_s[...]
            lse_ref[...] = l_s[...]

    col = lambda kind: (lambda c, h: (c, kind * N_ATTN_HEADS + g * HEADS_PER_GROUP + h))
    pcol = lambda kind: (lambda c, h: (jnp.maximum(c * nbk - 1, 0), kind * N_ATTN_HEADS + g * HEADS_PER_GROUP + h))
    cur = lambda kind: pl.BlockSpec((T, HEAD_DIM), col(kind))
    prv = lambda kind: pl.BlockSpec((P, HEAD_DIM), pcol(kind))
    out = pl.BlockSpec((T, HEAD_DIM), lambda c, h: (c, h))
    scratch = [] if d == 1 else [pltpu.VMEM((T, HEAD_DIM), F32)] * 3 + [pltpu.VMEM((P, HEAD_DIM), F32)] * 2 + [pltpu.VMEM((T, HEAD_DIM), F32)] * 2
    return pl.pallas_call(
        body,
        grid=(S // T, HEADS_PER_GROUP),
        in_specs=[cur(0), cur(1), cur(2), prv(1), prv(2)],
        out_specs=[out, out],
        out_shape=[jax.ShapeDtypeStruct((S, GROUP_WIDTH), F32)] * 2,
        scratch_shapes=scratch,
        compiler_params=_cparams("parallel", "parallel"),
        name=name,
    )(qkv, qkv, qkv, qkv, qkv)


def _attn_merge(os_, lses, name):
    S, W = os_[0].shape
    tm = _tile(S, 1024)

    def body(o0, o1, o2, l0, l1, l2, y_ref, lse_ref):
        ls = [l0[...], l1[...], l2[...]]
        m = jnp.maximum(jnp.maximum(ls[0], ls[1]), ls[2])
        es = [jnp.exp(v - m) for v in ls]
        tot = es[0] + es[1] + es[2]
        y = (es[0] * o0[...] + es[1] * o1[...] + es[2] * o2[...]) / tot
        y_ref[...] = y.astype(y_ref.dtype)
        lse_ref[...] = m + jnp.log(tot)

    row = pl.BlockSpec((tm, W), lambda i: (i, 0))
    return pl.pallas_call(
        body,
        grid=(S // tm,),
        in_specs=[row] * 6,
        out_specs=[row, row],
        out_shape=[jax.ShapeDtypeStruct((S, W), BF16), jax.ShapeDtypeStruct((S, W), F32)],
        compiler_params=_cparams("parallel"),
        name=name,
    )(*os_, *lses)


def _attn_bwd(qkv, dattn, y, lse, dproj, d, g, col0, name):
    S = qkv.shape[0]
    T = min(ATTN_TILE, S)
    P = SPAN * d
    nbk = T // P
    ntile = S // T

    def body(q_ref, k_ref, v_ref, kp_ref, vp_ref, qn_ref, da_ref, dan_ref, y_ref, yn_ref, lse_ref, lsen_ref, _, out_ref, dq_s, dk_s, dv_s, *scratch):
        c = pl.program_id(0)
        head_id = pl.program_id(1)
        kind = pl.program_id(2)

        @pl.when(kind == 0)
        def _():
            (jp, mp), (jc, mc) = _band_masks()
            slopes = [ALIBI_SLOPES[g * HEADS_PER_GROUP + h] * d for h in range(HEADS_PER_GROUP)]
            slope = slopes[0]
            for h in range(1, HEADS_PER_GROUP):
                slope = jnp.where(head_id == h, slopes[h], slope)
            q_s, k_s, v_s, da_s, y_s, kp_s, vp_s, qn_s, dan_s, yn_s = _f32_copies(
                (q_ref, k_ref, v_ref, da_ref, y_ref, kp_ref, vp_ref, qn_ref, dan_ref, yn_ref), scratch, d)
            bias_p, bias_c = jnp.where(mp, -slope * jp, NEG_BIG), jnp.where(mc, -slope * jc, NEG_BIG)
            bias = jnp.concatenate([bias_c, bias_p], axis=0)
            bias_last = jnp.concatenate([bias_c, jnp.where(c < ntile - 1, bias_p, NEG_BIG)], axis=0)
            bias_first = jnp.where(c > 0, bias_p, NEG_BIG)

            def pair(q, da, yy, lse_blk, kk, vv, b):
                dd = jnp.sum(da.astype(F32) * yy.astype(F32), axis=-1, keepdims=True)
                p = jnp.exp(_dot_nt(q, kk) * ATTN_SCALE + b - lse_blk[:, 0:1])
                return p, p * (_dot_nt(da, vv) - dd)

            for r in range(d):
                first = _unit_rows(r, 0, d)
                kk, vv = kp_s[first, :].astype(BF16), vp_s[first, :].astype(BF16)
                _, ds = pair(q_s[first, :].astype(BF16), da_s[first, :].astype(BF16), y_s[first, :], lse_ref[first, :], kk, vv, bias_first)
                dq_next = _dot_nn(ds.astype(BF16), kk)
                for kb in range(nbk):
                    rows = _unit_rows(r, kb, d)
                    if kb + 1 < nbk:
                        both = _unit_rows(r, kb, d, 2)
                        q, da, yy, lse_blk = q_s[both, :], da_s[both, :], y_s[both, :], lse_ref[both, :]
                    else:
                        q = jnp.concatenate([q_s[rows, :], qn_s[first, :]], axis=0)
                        da = jnp.concatenate([da_s[rows, :], dan_s[first, :]], axis=0)
                        yy = jnp.concatenate([y_s[rows, :], yn_s[first, :]], axis=0)
                        lse_blk = jnp.concatenate([lse_ref[rows, :], lsen_ref[first, :]], axis=0)
                    q, da = q.astype(BF16), da.astype(BF16)
                    kk, vv = k_s[rows, :].astype(BF16), v_s[rows, :].astype(BF16)
                    p, ds = pair(q, da, yy, lse_blk, kk, vv, bias if kb + 1 < nbk else bias_last)
                    dv_s[rows, :] = _dot_nn(p.T.astype(BF16), da)
                    dk_s[rows, :] = _dot_nn(ds.T.astype(BF16), q) * ATTN_SCALE
                    dq_both = _dot_nn(ds.astype(BF16), kk)
                    dq_s[rows, :] = (dq_next + dq_both[:SPAN]) * ATTN_SCALE
                    dq_next = dq_both[SPAN:]
            out_ref[...] = dq_s[...].astype(out_ref.dtype)

        @pl.when(kind == 1)
        def _():
            out_ref[...] = dk_s[...].astype(out_ref.dtype)

        @pl.when(kind == 2)
        def _():
            out_ref[...] = dv_s[...].astype(out_ref.dtype)

    head = lambda h: g * HEADS_PER_GROUP + h
    cur = lambda kind: pl.BlockSpec((T, HEAD_DIM), lambda c, h, kd: (c, kind * N_ATTN_HEADS + head(h)))
    prv = lambda kind: pl.BlockSpec((P, HEAD_DIM), lambda c, h, kd: (jnp.maximum(c * nbk - 1, 0), kind * N_ATTN_HEADS + head(h)))
    nxt_row = lambda c: jnp.minimum((c + 1) * nbk, S // P - 1)
    qnext = pl.BlockSpec((P, HEAD_DIM), lambda c, h, kd: (nxt_row(c), head(h)))
    hcur = pl.BlockSpec((T, HEAD_DIM), lambda c, h, kd: (c, h))
    hnext = pl.BlockSpec((P, HEAD_DIM), lambda c, h, kd: (nxt_row(c), h))
    out = pl.BlockSpec((T, HEAD_DIM), lambda c, h, kd: (c, col0 + kd * N_ATTN_HEADS + head(h)))
    stage = [pltpu.VMEM((T, HEAD_DIM), F32)] * 3
    copies = [] if d == 1 else [pltpu.VMEM((T, HEAD_DIM), F32)] * 5 + [pltpu.VMEM((P, HEAD_DIM), F32)] * 5
    return pl.pallas_call(
        body,
        grid=(ntile, HEADS_PER_GROUP, 3),
        in_specs=[cur(0), cur(1), cur(2), prv(1), prv(2), qnext, hcur, hnext, hcur, hnext, hcur, hnext, ANY],
        out_specs=out,
        out_shape=jax.ShapeDtypeStruct(dproj.shape, dproj.dtype),
        input_output_aliases={12: 0},
        scratch_shapes=stage + copies,
        compiler_params=_cparams("parallel", "parallel", "arbitrary"),
        name=name,
    )(qkv, qkv, qkv, qkv, qkv, qkv, dattn, dattn, y, y, lse, lse, dproj)


def _row_block(R, C, bytes_per_row_elem=4, budget=1 << 20):
    if R % 8:
        return R
    best = 8
    t = 8
    while t <= R:
        if R % t == 0 and t * C * bytes_per_row_elem <= budget:
            best = t
        t += 8
    return best


def _adamw(w, g, m, v, name):
    R, C = w.shape
    tr = _row_block(R, C, budget=2 << 20)
    c1 = 1.0 - ADAM_B1 ** ADAM_STEP
    c2 = 1.0 - ADAM_B2 ** ADAM_STEP

    def body(w_ref, g_ref, m_ref, v_ref, d_ref, nm_ref, nv_ref):
        gv = g_ref[...]
        nm = ADAM_B1 * m_ref[...] + (1.0 - ADAM_B1) * gv
        nv = ADAM_B2 * v_ref[...] + (1.0 - ADAM_B2) * (gv * gv)
        d_ref[...] = -ADAM_LR * ((nm / c1) / (jnp.sqrt(nv / c2) + ADAM_EPS) + ADAM_WD * w_ref[...])
        nm_ref[...] = nm
        nv_ref[...] = nv

    blk = pl.BlockSpec((tr, C), lambda i: (i, 0))
    return pl.pallas_call(
        body,
        grid=(R // tr,),
        in_specs=[blk] * 4,
        out_specs=[blk] * 3,
        out_shape=[jax.ShapeDtypeStruct((R, C), F32)] * 3,
        compiler_params=_cparams("parallel"),
        name=name,
    )(w, g, m, v)


def _sum_pieces(grad, axis, recv, pos, name):
    n, pr, pc = recv.shape
    tr = _row_block(pr, pc, bytes_per_row_elem=(n + 1) * recv.dtype.itemsize, budget=4 << 20)
    nblk = pr // tr
    if axis == 1:
        own_map = lambda i, p: (p[1] * nblk + i, p[0])
    else:
        own_map = lambda i, p: ((2 * p[0] + p[1]) * nblk + i, 0)

    def body(p_ref, own_ref, r_ref, o_ref):
        acc = own_ref[...].astype(F32)
        for s in range(n):
            acc = acc + r_ref[s].astype(F32)
        o_ref[...] = acc

    return pl.pallas_call(
        body,
        grid_spec=pltpu.PrefetchScalarGridSpec(
            num_scalar_prefetch=1,
            grid=(nblk,),
            in_specs=[pl.BlockSpec((tr, pc), own_map), pl.BlockSpec((n, tr, pc), lambda i, p: (0, i, 0))],
            out_specs=pl.BlockSpec((tr, pc), lambda i, p: (p[1] * nblk + i, 0)),
        ),
        out_shape=jax.ShapeDtypeStruct((2 * pr, pc), F32),
        compiler_params=_cparams("parallel"),
        name=name,
    )(pos, grad, recv)


def _sum_small(own, recv, me, name):
    n, R, C = recv.shape
    tr = _row_block(R, C, bytes_per_row_elem=(n + 1) * 4, budget=4 << 20)

    def body(me_ref, own_ref, r_ref, o_ref):
        acc = None
        for dev in range(n + 1):
            k = jnp.bitwise_xor(me_ref[0], dev)
            term = jnp.where(k == 0, own_ref[...], r_ref[jnp.maximum(k - 1, 0)])
            acc = term if acc is None else acc + term
        o_ref[...] = acc

    return pl.pallas_call(
        body,
        grid_spec=pltpu.PrefetchScalarGridSpec(
            num_scalar_prefetch=1,
            grid=(R // tr,),
            in_specs=[pl.BlockSpec((tr, C), lambda i, m: (i, 0)), pl.BlockSpec((n, tr, C), lambda i, m: (0, i, 0))],
            out_specs=pl.BlockSpec((tr, C), lambda i, m: (i, 0)),
        ),
        out_shape=jax.ShapeDtypeStruct((R, C), F32),
        compiler_params=_cparams("parallel"),
        name=name,
    )(me, own, recv)


def _place(shard, axis, pos, dtype, name, after=None):
    extra = [] if after is None else [after]
    shp = list(shard.shape)
    shp[axis] *= N_CHIPS
    if shard.ndim == 3:
        assert axis == 1
        in_spec = pl.BlockSpec(shard.shape, lambda i, p: (0, 0, 0))
        out_spec = pl.BlockSpec(shard.shape, lambda i, p: (0, p[0], 0))
        grid = (1,)
    else:
        R, C = shard.shape
        tr = _row_block(R, C, bytes_per_row_elem=4, budget=2 << 20)
        nblk = R // tr
        in_spec = pl.BlockSpec((tr, C), lambda i, p: (i, 0))
        out_spec = pl.BlockSpec((tr, C), (lambda i, p: (i, p[0])) if axis == 1 else (lambda i, p: (p[0] * nblk + i, 0)))
        grid = (nblk,)

    def body(*refs):
        s_ref, o_ref = refs[1], refs[-1]
        o_ref[...] = s_ref[...].astype(o_ref.dtype)

    return pl.pallas_call(
        body,
        grid_spec=pltpu.PrefetchScalarGridSpec(num_scalar_prefetch=1, grid=grid, in_specs=[in_spec] + [ANY] * len(extra), out_specs=out_spec),
        out_shape=jax.ShapeDtypeStruct(tuple(shp), dtype),
        compiler_params=_cparams("parallel"),
        name=name,
    )(pos, shard, *extra)


HBM = pl.BlockSpec(memory_space=pltpu.HBM)
SEM = pl.BlockSpec(memory_space=pltpu.SEMAPHORE)
DATAFLOW = pltpu.SideEffectType.DATAFLOW_SIDE_EFFECTING


def _position():
    return lax.axis_index("x"), lax.axis_index("y"), lax.axis_index("c")


def _peer(k):
    x, y, c = _position()
    return ((1 - x) if k & 4 else x, (1 - y) if k & 2 else y, (1 - c) if k & 1 else c)


def _shard_slice(ref, axis, idx, size):
    start = idx * size
    if axis == ref.ndim - 1:
        start = pl.multiple_of(start, 128)
    ix = [slice(None)] * ref.ndim
    ix[axis] = pl.ds(start, size)
    return ref.at[tuple(ix)]


def _gather_plan(axes):
    def plan(refs):
        x, y, c = _position()
        out = []
        for ref, ax in zip(refs, axes):
            mine = _shard_slice(ref, ax, 2 * x + y, ref.shape[ax] // N_CHIPS)
            for k in (4, 2, 6):
                px, py, _ = _peer(k)
                out.append((mine, mine, (px, py, c)))
        return out
    return plan


def _scatter_plan(axes):
    m = len(axes)

    def plan(refs):
        out = []
        for t in range(m):
            grad, recv = refs[t], refs[m + t]
            _, pr, pc = recv.shape
            for k in range(1, N_DEV):
                px, py, pcore = _peer(k)
                if axes[t] == 0:
                    piece = grad.at[pl.ds(((2 * px + py) * 2 + pcore) * pr, pr), :]
                else:
                    piece = grad.at[pl.ds(pcore * pr, pr), pl.ds(pl.multiple_of((2 * px + py) * pc, 128), pc)]
                out.append((piece, recv.at[k - 1], (px, py, pcore)))
        return out
    return plan


def _broadcast_plan(refs):
    small, recv = refs
    return [(small, recv.at[k - 1], _peer(k)) for k in range(1, N_DEV)]


def _start_all(plan, refs, send_sems, recv_sems):
    for q, (src, dst, dev) in enumerate(plan(refs)):
        pltpu.make_async_remote_copy(src_ref=src, dst_ref=dst, send_sem=send_sems.at[q], recv_sem=recv_sems.at[q], device_id=dev, device_id_type=MESH).start()


def _wait_all(plan, refs, send_sems, recv_sems):
    for q, (src, dst, dev) in enumerate(plan(refs)):
        cp = pltpu.make_async_remote_copy(src_ref=src, dst_ref=dst, send_sem=send_sems.at[q], recv_sem=recv_sems.at[q], device_id=dev, device_id_type=MESH)
        cp.wait_send()
        cp.wait_recv()


def _push(bufs, plan, ncopies, name):
    n = len(bufs)

    def body(*refs):
        outs = refs[n:2 * n]
        send_sems, recv_sems = refs[2 * n:]
        _start_all(plan, outs, send_sems, recv_sems)
        _wait_all(plan, outs, send_sems, recv_sems)

    return pl.pallas_call(
        body,
        in_specs=[ANY] * n,
        out_specs=[ANY] * n,
        out_shape=[jax.ShapeDtypeStruct(b.shape, b.dtype) for b in bufs],
        input_output_aliases={t: t for t in range(n)},
        scratch_shapes=[pltpu.SemaphoreType.DMA((ncopies,)), pltpu.SemaphoreType.DMA((ncopies,))],
        name=name,
    )(*bufs)


def _half_slices_plan(onward):
    def plan(refs):
        ref, = refs
        x, y, c = _position()
        R2, C4 = ref.shape[0] // 2, ref.shape[1] // N_CHIPS
        out = []
        for k in (4, 2, 6):
            px, py, _ = _peer(k)
            chip = (2 * px + py) if onward else (2 * x + y)
            half = ref.at[pl.ds(c * R2, R2), pl.ds(pl.multiple_of(chip * C4, 128), C4)]
            out.append((half, half, (x, y, 1 - c) if onward else (px, py, c)))
        return out
    return plan


def _push_start(bufs, plan, ncopies, name, after=None):
    n = len(bufs)
    extra = [] if after is None else [after]

    def body(*refs):
        ins = refs[:n]
        first_out = n + len(extra)
        send_sems, recv_sems, token = refs[first_out], refs[first_out + 1], refs[-1]
        _start_all(plan, ins, send_sems, recv_sems)
        token[...] = jnp.zeros_like(token)

    res = pl.pallas_call(
        body,
        name=name,
        out_shape=(pltpu.SemaphoreType.DMA((ncopies,)), pltpu.SemaphoreType.DMA((ncopies,)), *[pltpu.HBM(b.shape, b.dtype) for b in bufs],
                   jax.ShapeDtypeStruct((8, 128), F32)),
        in_specs=[HBM] * n + [ANY] * len(extra),
        out_specs=(SEM, SEM, *[HBM] * n, pl.BlockSpec(memory_space=pltpu.VMEM)),
        input_output_aliases={t: t + 2 for t in range(n)},
        compiler_params=pltpu.CompilerParams(has_side_effects=DATAFLOW),
    )(*[pltpu.with_memory_space_constraint(b, pltpu.HBM) for b in bufs], *extra)
    return res[0], res[1], list(res[2:2 + n]), res[-1]


def _push_wait(send_sems, recv_sems, bufs, plan, after, name):
    n = len(bufs)
    after = list(after) if isinstance(after, (list, tuple)) else [after]

    def body(*refs):
        ins = refs[:n]
        _wait_all(plan, ins, refs[n], refs[n + 1])

    return pl.pallas_call(
        body,
        name=name,
        out_shape=tuple(pltpu.HBM(b.shape, b.dtype) for b in bufs),
        in_specs=[HBM] * n + [SEM, SEM] + [ANY] * len(after),
        out_specs=tuple([HBM] * n),
        input_output_aliases={t: t for t in range(n)},
        compiler_params=pltpu.CompilerParams(has_side_effects=DATAFLOW),
    )(*bufs, send_sems, recv_sems, *after)


EXCHANGE_CHUNKS = 2


def _exchange_plan(refs):
    x, y, c = _position()
    out = []
    for ref in refs:
        rows = ref.shape[0] // (2 * EXCHANGE_CHUNKS)
        for q in range(EXCHANGE_CHUNKS):
            mine = ref.at[pl.ds((c * EXCHANGE_CHUNKS + q) * rows, rows), :]
            out.append((mine, mine, (x, y, 1 - c)))
    return out


LATE_WEIGHTS = (("w_pool_lin", "w_pool_out", "w_attn_out", "w_out"), ("w_up", "conv_w", "w_down"))


def _local_step(x, tgt, w, late_weights, send):
    S, D = x.shape
    PW = w["pool_scale"].shape[1]
    o_q = PW
    o_g = PW + 3 * ATTN_WIDTH
    QKV = 3 * ATTN_WIDTH

    h1 = _rms_fwd(x, w["g_mix"], "rms1")
    w = dict(w, **late_weights("w_in", h1))
    proj_tiles = (_tile(S, 1024), 512, D)
    started = w.get("late_started")
    u = _mm(h1, w["w_in"], mode="nn", dims=(S, PW, D), tiles=proj_tiles, out_dtypes=(F32,), after=started, name="proj_u")
    qkv = _mm(h1, w["w_in"], mode="nn", dims=(S, QKV, D), tiles=proj_tiles, b_off=(0, o_q), after=started, name="proj_qkv")

    def gate_epilogue(acc, ex, outs):
        outs[0][...] = (1.0 / (1.0 + jnp.exp(-(acc + ex[0][...])))).astype(outs[0].dtype)

    gates = _mm(h1, w["w_in"], mode="nn", dims=(S, 2 * D, D), tiles=proj_tiles, b_off=(0, o_g), epilogue=gate_epilogue,
                extras=[(w["b_gate"], "n", (0, 0))], name="proj_gates")

    os_, lses = [], []
    for gi, (_, d) in enumerate(ATTN_GROUPS):
        o, lse = _attn_fwd(qkv, d, gi, f"attn_fwd{gi}")
        os_.append(o)
        lses.append(lse)
    attn, lse_tot = _attn_merge(os_, lses, "attn_merge")

    w = dict(w, **late_weights(0, attn))
    pool_out = _pool_fwd(u, w["w_pool_lin"], w["pool_scale"], "pool_fwd")
    y_pool = _mm(pool_out, w["w_pool_out"], mode="nn", dims=(S, D, PW), name="y_pool")

    def mix_epilogue(acc, ex, outs):
        outs[0][...] = acc.astype(BF16)
        outs[1][...] = (ex[0][...].astype(F32) * ex[2][...].astype(F32) + ex[1][...].astype(F32) * acc).astype(BF16)

    y_attn, mixed = _mm(attn, w["w_attn_out"], mode="nn", dims=(S, D, GROUP_WIDTH), out_dtypes=(BF16, BF16), epilogue=mix_epilogue,
                        extras=[(gates, "mn", (0, 0)), (gates, "mn", (0, D)), (y_pool, "mn", (0, 0))], name="y_attn_mix")

    def residual_epilogue(acc, ex, outs):
        outs[0][...] = ex[0][...] + acc

    x2 = _mm(mixed, w["w_out"], mode="nn", dims=(S, D, D), out_dtypes=(F32,), epilogue=residual_epilogue, extras=[(x, "mn", (0, 0))], name="out_proj")

    h2 = _rms_fwd(x2, w["g_ffn"], "rms2")
    w = dict(w, **late_weights(1, h2))
    F = w["w_down"].shape[0]
    up = _mm(h2, w["w_up"], mode="nn", dims=(S, 2 * F, D), name="up_proj")
    f, act_a, act_b = _convglu_fwd(up, w["conv_w"], w["conv_b"], "convglu_fwd")
    x3 = _mm(f, w["w_down"], mode="nn", dims=(S, D, F), out_dtypes=(F32,), epilogue=residual_epilogue, extras=[(x2, "mn", (0, 0))], name="down_proj")

    g = {}
    dx3b, g["g_final"], loss_cols = _loss_head(x3, tgt, w["g_final"], "loss_head")

    g["w_down"] = _mm(f, dx3b, mode="tn", dims=(F, D, S), name="dw_down")
    sent = send(("w_down",), g)
    df = _mm(dx3b, w["w_down"], mode="nt", dims=(S, F, D), name="d_f")
    dup, g["conv_b"], g["conv_w"] = _convglu_bwd(df, act_a, act_b, up, w["conv_w"] + sent, "convglu_bwd")
    g["w_up"] = _mm(h2, dup, mode="tn", dims=(D, 2 * F, S), name="dw_up")
    sent = send(("w_up",), g)
    dh2 = _mm(dup, w["w_up"], mode="nt", dims=(S, D, 2 * F), name="d_h2")
    dx2b, g["g_ffn"] = _rms_bwd(dh2, x2, w["g_ffn"] + sent, dx3b, "rms2_bwd", BF16)

    g["w_out"] = _mm(mixed, dx2b, mode="tn", dims=(D, D, S), name="dw_out")
    dmixed = _mm(dx2b, w["w_out"], mode="nt", dims=(S, D, D), name="d_mixed")
    IN = w["w_in"].shape[1]
    dy_both, dproj, g["b_gate"] = _gate_bwd(dmixed, gates, y_pool, y_attn, IN, "gate_bwd")

    g["w_pool_out"] = _mm(pool_out, dy_both, mode="tn", dims=(PW, D, S), name="dw_pool_out")
    g["w_attn_out"] = _mm(attn, dy_both, mode="tn", dims=(GROUP_WIDTH, D, S), b_off=(0, D), name="dw_attn_out")
    sent = send(("w_out", "w_pool_out", "w_attn_out"), g)
    dpool = _mm(dy_both, w["w_pool_out"], mode="nt", dims=(S, PW, D), name="d_pool")
    dattn = _mm(dy_both, w["w_attn_out"], mode="nt", dims=(S, GROUP_WIDTH, D), a_off=(0, D), name="d_attn")

    dproj, g["w_pool_lin"], g["pool_scale"] = _pool_bwd(u, dpool, w["w_pool_lin"], w["pool_scale"] + sent, dproj, "pool_bwd")
    g["loss_cols"] = loss_cols
    sent = send("small", g)

    for gi, (_, d) in enumerate(ATTN_GROUPS):
        dproj = _attn_bwd(qkv, dattn, attn, lse_tot, dproj, d, gi, PW // HEAD_DIM, f"attn_bwd{gi}")

    g["w_in"] = _mm(h1, dproj, mode="tn", dims=(D, IN, S), name="dw_in")
    sent = sent + send(("w_in",), g)
    dh1 = _mm(dproj, w["w_in"], mode="nt", dims=(S, D, IN), tiles=(_tile(S, 1024), _tile(D, 2048), _tile(IN, 2432)), name="d_h1")
    (grad_x, g["g_mix"]) = _rms_bwd(dh1, x, w["g_mix"] + sent, dx2b, "rms1_bwd", F32)
    return loss_cols, grad_x, g


BIG = ("w_in", "w_pool_out", "w_attn_out", "w_out", "w_up", "w_down")
BIG_AXIS = {"w_in": 1, "w_pool_out": 1, "w_attn_out": 1, "w_out": 0, "w_up": 1, "w_down": 0}
GATHER_AXIS = dict(BIG_AXIS, w_pool_lin=1, conv_w=1)
SMALL = ("loss_cols", "b_gate", "w_pool_lin", "pool_scale", "g_ffn", "conv_w", "conv_b", "g_final")
SMALL_COLS = 1024
ORDER = ("g_mix", "w_in", "b_gate", "w_pool_lin", "pool_scale", "w_pool_out", "w_attn_out", "w_out", "g_ffn", "w_up", "conv_w", "conv_b", "w_down", "g_final")


def _as_rows(parts):
    flat = jnp.concatenate([p.astype(F32).reshape(-1) for p in parts])
    rows = -(-flat.shape[0] // (8 * SMALL_COLS)) * 8
    return jnp.pad(flat, (0, rows * SMALL_COLS - flat.shape[0])).reshape(rows, SMALL_COLS)


def kernel(x, g_mix, w_in, b_gate, w_pool_lin, pool_scale, w_pool_out, w_attn_out, w_out, g_ffn, w_up, conv_w, conv_b, w_down, g_final, loss_target, m_g_mix, m_w_in, m_b_gate, m_w_pool_lin, m_pool_scale, m_w_pool_out, m_w_attn_out, m_w_out, m_g_ffn, m_w_up, m_conv_w, m_conv_b, m_w_down, m_g_final, v_g_mix, v_w_in, v_b_gate, v_w_pool_lin, v_pool_scale, v_w_pool_out, v_w_attn_out, v_w_out, v_g_ffn, v_w_up, v_conv_w, v_conv_b, v_w_down, v_g_final):
    shard = dict(g_mix=g_mix, w_in=w_in, b_gate=b_gate, w_pool_lin=w_pool_lin, pool_scale=pool_scale, w_pool_out=w_pool_out, w_attn_out=w_attn_out,
                 w_out=w_out, g_ffn=g_ffn, w_up=w_up, conv_w=conv_w, conv_b=conv_b, w_down=w_down, g_final=g_final)
    mom = dict(g_mix=m_g_mix, w_in=m_w_in, b_gate=m_b_gate, w_pool_lin=m_w_pool_lin, pool_scale=m_pool_scale, w_pool_out=m_w_pool_out, w_attn_out=m_w_attn_out,
               w_out=m_w_out, g_ffn=m_g_ffn, w_up=m_w_up, conv_w=m_conv_w, conv_b=m_conv_b, w_down=m_w_down, g_final=m_g_final)
    vel = dict(g_mix=v_g_mix, w_in=v_w_in, b_gate=v_b_gate, w_pool_lin=v_w_pool_lin, pool_scale=v_pool_scale, w_pool_out=v_w_pool_out, w_attn_out=v_w_attn_out,
               w_out=v_w_out, g_ffn=v_g_ffn, w_up=v_w_up, conv_w=v_conv_w, conv_b=v_conv_b, w_down=v_w_down, g_final=v_g_final)
    chip = 2 * lax.axis_index("x") + lax.axis_index("y")
    pos = jnp.stack([chip, lax.axis_index("c")]).astype(jnp.int32)
    me = (2 * chip + lax.axis_index("c")).astype(jnp.int32).reshape(1)
    D = x.shape[2]

    out_plan, on_plan = _half_slices_plan(False), _half_slices_plan(True)
    in_send, in_recv, in_bufs, in_token = _push_start([_place(shard["w_in"][0], GATHER_AXIS["w_in"], pos, BF16, "place_w_in")], out_plan, 3,
                                                      "comm_gather_w_in_start")
    placed = {k: _place(shard[k][0], GATHER_AXIS[k], pos, F32 if k == "conv_w" else BF16, f"place_{k}", after=in_token)
              for names in LATE_WEIGHTS for k in names}
    late = []

    def late_weights(stage, after):
        if stage == "w_in":
            landed = _push_wait(in_send, in_recv, in_bufs, out_plan, [after] + list(placed.values()), "comm_gather_w_in_wait")
            w_in_full, = _push(list(landed), on_plan, 3, "comm_gather_w_in_pass")
            late_token, prior = 0.0, w_in_full
            for st, names in enumerate(LATE_WEIGHTS):
                plan = _gather_plan([GATHER_AXIS[k] for k in names])
                send_sems, recv_sems, bufs, token = _push_start([placed[k] for k in names], plan, 3 * len(names), f"comm_gather_late{st}_start", after=prior)
                late.append((names, send_sems, recv_sems, bufs, plan))
                late_token, prior = late_token + token[0, 0], token
            return dict(w_in=w_in_full, b_gate=shard["b_gate"] + late_token, late_started=prior)
        names, send_sems, recv_sems, bufs, plan = late[stage]
        return dict(zip(names, _push_wait(send_sems, recv_sems, bufs, plan, after, f"comm_gather_late{stage}_wait")))

    pending = []

    def send(names, g):
        if names == "small":
            bufs = [_as_rows([g[k] for k in SMALL])]
            bufs.append(lax.empty((N_DEV - 1,) + bufs[0].shape, F32))
            plan, tag = _broadcast_plan, "small"
        else:
            bufs = [g[k] for k in names]
            for k in names:
                R, C = g[k].shape
                piece = (R // (2 * N_CHIPS), C) if BIG_AXIS[k] == 0 else (R // 2, C // N_CHIPS)
                bufs.append(lax.empty((N_DEV - 1,) + piece, BF16))
            plan, tag = _scatter_plan([BIG_AXIS[k] for k in names]), names[0]
        ncopies = (N_DEV - 1) * (len(bufs) // 2)
        send_sems, recv_sems, thru, token = _push_start(bufs, plan, ncopies, f"comm_scatter_start_{tag}")
        pending.append((names, send_sems, recv_sems, thru, plan, tag))
        return token[0, 0]

    w0 = dict(g_mix=shard["g_mix"] + in_token[0, 0], pool_scale=shard["pool_scale"], g_ffn=shard["g_ffn"],
              conv_b=shard["conv_b"], g_final=shard["g_final"].reshape(1, D))
    _, grad_x, gr = _local_step(x[0], loss_target[0], w0, late_weights, send)

    halves, small_parts = {}, None
    for names, send_sems, recv_sems, thru, plan, tag in pending:
        done = _push_wait(send_sems, recv_sems, thru, plan, grad_x, f"comm_scatter_wait_{tag}")
        if names == "small":
            small_parts = _sum_small(done[0], done[1], me, "sum_small").reshape(-1)
        else:
            m = len(names)
            for t, k in enumerate(names):
                halves[k] = _sum_pieces(done[t], BIG_AXIS[k], done[m + t], pos, f"sum_{k}")
    g_mix_own = _as_rows([gr["g_mix"]])
    _, g_mix_recv = _push([g_mix_own, lax.empty((N_DEV - 1,) + g_mix_own.shape, F32)], _broadcast_plan, N_DEV - 1, "comm_gather_g_mix")
    g_mix_sum = _sum_small(g_mix_own, g_mix_recv, me, "sum_g_mix").reshape(-1)[:D]
    wholes = _push([halves[k] for k in BIG], _exchange_plan, EXCHANGE_CHUNKS * len(BIG), "comm_exchange_halves")

    grads = {"g_mix": g_mix_sum.reshape(shard["g_mix"].shape)}
    for k, whole in zip(BIG, wholes):
        grads[k] = whole.reshape(shard[k].shape)
    off = 0
    loss = None
    for k in SMALL:
        sz = math.prod(gr[k].shape)
        fullg = small_parts[off:off + sz].reshape(gr[k].shape)
        off += sz
        if k == "loss_cols":
            loss = jnp.sum(fullg)
            continue
        if k in ("w_pool_lin", "conv_w"):
            n = shard[k].shape[2]
            fullg = lax.dynamic_slice_in_dim(fullg, chip * n, n, axis=1)
        grads[k] = fullg.reshape(shard[k].shape)

    deltas, new_m, new_v = {}, {}, {}
    for k in ORDER:
        shp = shard[k].shape
        two_d = (-1, shp[-1])
        dl, nm, nv = _adamw(shard[k].reshape(two_d), grads[k].reshape(two_d), mom[k].reshape(two_d), vel[k].reshape(two_d), f"adamw_{k}")
        deltas[k], new_m[k], new_v[k] = dl.reshape(shp), nm.reshape(shp), nv.reshape(shp)

    return (loss, grad_x[None], *[grads[k] for k in ORDER], *[deltas[k] for k in ORDER], *[new_m[k] for k in ORDER], *[new_v[k] for k in ORDER])
```

```python
import functools
import math

import jax
import jax.numpy as jnp
from jax import lax
from jax.experimental import pallas as pl
from jax.experimental.pallas import tpu as pltpu

F32 = jnp.float32
BF16 = jnp.bfloat16

RMS_EPS = 1e-6
POOL_WINDOWS = (2, 4, 8, 16)
ATTN_GROUPS = ((128, 1), (512, 4), (2048, 16))
HEADS_PER_GROUP = 4
HEAD_DIM = 128
N_ATTN_HEADS = HEADS_PER_GROUP * len(ATTN_GROUPS)
SPAN = 128
GROUP_WIDTH = HEADS_PER_GROUP * HEAD_DIM
ATTN_WIDTH = N_ATTN_HEADS * HEAD_DIM
ATTN_SCALE = HEAD_DIM ** -0.5
NEG_BIG = -1e30
ALIBI_SLOPES = tuple(2.0 ** (-8.0 * (h + 1) / N_ATTN_HEADS) for h in range(N_ATTN_HEADS))

ADAM_LR = 0.001
ADAM_B1 = 0.9
ADAM_B2 = 0.999
ADAM_EPS = 1e-08
ADAM_WD = 0.01
ADAM_STEP = 10

INV_SQRT2 = 1.0 / math.sqrt(2.0)
INV_SQRT_2PI = 1.0 / math.sqrt(2.0 * math.pi)

HALO = 16
VMEM_LIMIT = 56 * 1024 * 1024
N_CHIPS = 4
N_DEV = 8
MESH = pl.DeviceIdType.MESH
ANY = pl.BlockSpec(memory_space=pl.ANY)


def _cparams(*sem):
    return pltpu.CompilerParams(dimension_semantics=sem, vmem_limit_bytes=VMEM_LIMIT)


def _tile(n, pref, mult=128):
    t = (min(pref, n) // mult) * mult
    while t >= mult:
        if n % t == 0:
            return t
        t -= mult
    return n


def _dot(a, b, contract):
    return lax.dot_general(a, b, (contract, ((), ())), preferred_element_type=F32)


def _dot_nn(a, b):
    return _dot(a, b, ((1,), (0,)))


def _dot_nt(a, b):
    return _dot(a, b, ((1,), (1,)))


def _mm(a, b, *, mode, dims, name, tiles=None, out_dtypes=(BF16,), epilogue=None, extras=(), a_off=(0, 0), b_off=(0, 0), after=None):
    M, N, K = dims
    if tiles is None:
        tiles = (_tile(M, 1408), _tile(N, 2816), _tile(K, 512)) if mode == "tn" else (_tile(M, 1024), _tile(N, 1536), _tile(K, 2816))
    tm, tn, tk = tiles
    assert M % tm == 0 and N % tn == 0 and K % tk == 0, (name, dims, tiles)
    nk = K // tk
    if mode == "nn":
        ab, bb, contract = (tm, tk), (tk, tn), ((1,), (0,))
        amap = lambda i, j, k: (i + a_off[0] // tm, k + a_off[1] // tk)
        bmap = lambda i, j, k: (k + b_off[0] // tk, j + b_off[1] // tn)
    elif mode == "nt":
        ab, bb, contract = (tm, tk), (tn, tk), ((1,), (1,))
        amap = lambda i, j, k: (i + a_off[0] // tm, k + a_off[1] // tk)
        bmap = lambda i, j, k: (j + b_off[0] // tn, k + b_off[1] // tk)
    else:
        ab, bb, contract = (tk, tm), (tk, tn), ((0,), (0,))
        amap = lambda i, j, k: (k + a_off[0] // tk, i + a_off[1] // tm)
        bmap = lambda i, j, k: (k + b_off[0] // tk, j + b_off[1] // tn)
    assert a_off[0] % ab[0] == 0 and a_off[1] % ab[1] == 0 and b_off[0] % bb[0] == 0 and b_off[1] % bb[1] == 0, name
    in_specs = [pl.BlockSpec(ab, amap), pl.BlockSpec(bb, bmap)]
    ex_arrays = []
    for arr, kind, off in extras:
        if kind == "mn":
            assert off[0] % tm == 0 and off[1] % tn == 0, name
            in_specs.append(pl.BlockSpec((tm, tn), lambda i, j, k, off=off: (i + off[0] // tm, j + off[1] // tn)))
        else:
            assert off[1] % tn == 0, name
            in_specs.append(pl.BlockSpec((1, tn), lambda i, j, k, off=off: (0, j + off[1] // tn)))
        ex_arrays.append(arr)
    ne, no = len(ex_arrays), len(out_dtypes)
    if after is not None:
        in_specs.append(ANY)
        ex_arrays.append(after)
    first_out = 2 + len(ex_arrays)
    if epilogue is None:
        def epilogue(acc, ex, outs):
            outs[0][...] = acc.astype(outs[0].dtype)

    def body(*refs):
        a_ref, b_ref = refs[0], refs[1]
        ex, outs = refs[2:2 + ne], refs[first_out:first_out + no]
        if nk == 1:
            epilogue(_dot(a_ref[...], b_ref[...], contract), ex, outs)
            return
        acc = refs[-1]
        k = pl.program_id(2)
        if nk <= 4:
            part = _dot(a_ref[...], b_ref[...], contract)

            @pl.when(k == 0)
            def _():
                acc[...] = part

            @pl.when(jnp.logical_and(k > 0, k < nk - 1))
            def _():
                acc[...] += part

            @pl.when(k == nk - 1)
            def _():
                epilogue(acc[...] + part, ex, outs)
        else:
            @pl.when(k == 0)
            def _():
                acc[...] = _dot(a_ref[...], b_ref[...], contract)

            @pl.when(k > 0)
            def _():
                acc[...] += _dot(a_ref[...], b_ref[...], contract)

            @pl.when(k == nk - 1)
            def _():
                epilogue(acc[...], ex, outs)

    res = pl.pallas_call(
        body,
        grid=(M // tm, N // tn, nk),
        in_specs=in_specs,
        out_specs=[pl.BlockSpec((tm, tn), lambda i, j, k: (i, j)) for _ in out_dtypes],
        out_shape=[jax.ShapeDtypeStruct((M, N), dt) for dt in out_dtypes],
        scratch_shapes=[pltpu.VMEM((tm, tn), F32)] if nk > 1 else [],
        compiler_params=_cparams("parallel", "parallel", "arbitrary"),
        name=name,
    )(a, b, *ex_arrays)
    return res[0] if no == 1 else res


def _rms_fwd(x, g, name):
    S, D = x.shape
    tm = _tile(S, 512)

    def body(x_ref, g_ref, h_ref):
        xv = x_ref[...]
        r = lax.rsqrt(jnp.mean(xv * xv, axis=-1, keepdims=True) + RMS_EPS)
        h_ref[...] = (xv * r * g_ref[...]).astype(h_ref.dtype)

    return pl.pallas_call(
        body,
        grid=(S // tm,),
        in_specs=[pl.BlockSpec((tm, D), lambda i: (i, 0)), pl.BlockSpec((1, D), lambda i: (0, 0))],
        out_specs=pl.BlockSpec((tm, D), lambda i: (i, 0)),
        out_shape=jax.ShapeDtypeStruct((S, D), BF16),
        compiler_params=_cparams("parallel"),
        name=name,
    )(x, g)


def _rms_bwd(dh, x, g, dres, name, out_dtype):
    S, D = x.shape
    tm = _tile(S, 512)

    def body(dh_ref, x_ref, g_ref, dres_ref, dx_ref, dg_ref):
        xv = x_ref[...]
        r = lax.rsqrt(jnp.mean(xv * xv, axis=-1, keepdims=True) + RMS_EPS)
        xr = xv * r
        dhv = dh_ref[...].astype(F32)

        @pl.when(pl.program_id(0) == 0)
        def _():
            dg_ref[...] = jnp.zeros_like(dg_ref)

        dg_ref[...] += jnp.sum(dhv * xr, axis=0, keepdims=True)
        u = dhv * g_ref[...]
        c = jnp.mean(u * xr, axis=-1, keepdims=True)
        dx_ref[...] = (dres_ref[...].astype(F32) + r * (u - xr * c)).astype(dx_ref.dtype)

    row = pl.BlockSpec((tm, D), lambda i: (i, 0))
    vec = pl.BlockSpec((1, D), lambda i: (0, 0))
    return pl.pallas_call(
        body,
        grid=(S // tm,),
        in_specs=[row, row, vec, row],
        out_specs=[row, vec],
        out_shape=[jax.ShapeDtypeStruct((S, D), out_dtype), jax.ShapeDtypeStruct((1, D), F32)],
        compiler_params=_cparams("arbitrary"),
        name=name,
    )(dh, x, g, dres)


def _loss_head(x3, tgt, g, name):
    S, D = x3.shape
    tm = _tile(S, 512)

    def body(x_ref, t_ref, g_ref, dxb_ref, dg_ref, loss_ref):
        xv = x_ref[...]
        gv = g_ref[...]
        r = lax.rsqrt(jnp.mean(xv * xv, axis=-1, keepdims=True) + RMS_EPS)
        xr = xv * r
        e = xr * gv - t_ref[...]

        @pl.when(pl.program_id(0) == 0)
        def _():
            dg_ref[...] = jnp.zeros_like(dg_ref)
            loss_ref[...] = jnp.zeros_like(loss_ref)

        loss_ref[...] += jnp.sum(e * e, axis=0, keepdims=True) * (0.5 / D)
        dy = e * (1.0 / D)
        dg_ref[...] += jnp.sum(dy * xr, axis=0, keepdims=True)
        u = dy * gv
        c = jnp.mean(u * xr, axis=-1, keepdims=True)
        dxb_ref[...] = (r * (u - xr * c)).astype(BF16)

    row = pl.BlockSpec((tm, D), lambda i: (i, 0))
    vec = pl.BlockSpec((1, D), lambda i: (0, 0))
    return pl.pallas_call(
        body,
        grid=(S // tm,),
        in_specs=[row, row, vec],
        out_specs=[row, vec, vec],
        out_shape=[jax.ShapeDtypeStruct((S, D), BF16), jax.ShapeDtypeStruct((1, D), F32), jax.ShapeDtypeStruct((1, D), F32)],
        compiler_params=_cparams("arbitrary"),
        name=name,
    )(x3, tgt, g)


SHIFT_ROWS = 256


def _shift_matrix():
    i = lax.broadcasted_iota(jnp.int32, (2 * SHIFT_ROWS, SHIFT_ROWS), 0)
    j = lax.broadcasted_iota(jnp.int32, (2 * SHIFT_ROWS, SHIFT_ROWS), 1)
    src = jnp.where(i < SHIFT_ROWS, i - 1, i - SHIFT_ROWS - 2)
    return (j == src).astype(BF16)


def _conv_taps(cur_ref, halo_ref, w_ref, b_ref, first, shift):
    w, bias = w_ref[...], b_ref[...]
    before = jnp.where(first, 0.0, halo_ref[...].astype(F32)[HALO - 8:])
    sub = lax.broadcasted_iota(jnp.int32, before.shape, 0)
    out = []
    for blk in range(cur_ref.shape[0] // SHIFT_ROWS):
        xb = cur_ref[blk * SHIFT_ROWS:(blk + 1) * SHIFT_ROWS, :]
        xf = xb.astype(F32)
        both = _dot_nn(shift, xb)
        p1, p2 = both[:SHIFT_ROWS], both[SHIFT_ROWS:]
        p1 = jnp.concatenate([jnp.where(sub == 0, pltpu.roll(before, 1, 0), p1[:8]), p1[8:]], axis=0)
        p2 = jnp.concatenate([jnp.where(sub < 2, pltpu.roll(before, 2, 0), p2[:8]), p2[8:]], axis=0)
        out.append(bias + w[0:1] * p2 + w[1:2] * p1 + w[2:3] * xf)
        before = xf[SHIFT_ROWS - 8:]
    return jnp.concatenate(out, axis=0)


def _convglu_fwd(up, cw, cb, name):
    S, F2 = up.shape
    F = F2 // 2
    tm, tn = _tile(S, 512), _tile(F, 1408)
    nj, hb = F // tn, tm // HALO

    def body(ua, ub, ha, hb_, wa, wb, ba, bb, f_ref, a_ref, b_ref):
        first = pl.program_id(0) == 0
        shift = _shift_matrix()
        a = _conv_taps(ua, ha, wa, ba, first, shift)
        b = _conv_taps(ub, hb_, wb, bb, first, shift)
        f_ref[...] = (0.5 * a * (1.0 + lax.erf(a * INV_SQRT2)) * b).astype(f_ref.dtype)
        a_ref[...] = a.astype(a_ref.dtype)
        b_ref[...] = b.astype(b_ref.dtype)

    tile = pl.BlockSpec((tm, tn), lambda i, j: (i, j))
    return pl.pallas_call(
        body,
        grid=(S // tm, nj),
        in_specs=[
            tile,
            pl.BlockSpec((tm, tn), lambda i, j: (i, j + nj)),
            pl.BlockSpec((HALO, tn), lambda i, j: (jnp.maximum(i * hb - 1, 0), j)),
            pl.BlockSpec((HALO, tn), lambda i, j: (jnp.maximum(i * hb - 1, 0), j + nj)),
            pl.BlockSpec((3, tn), lambda i, j: (0, j)),
            pl.BlockSpec((3, tn), lambda i, j: (0, j + nj)),
            pl.BlockSpec((1, tn), lambda i, j: (0, j)),
            pl.BlockSpec((1, tn), lambda i, j: (0, j + nj)),
        ],
        out_specs=[tile, tile, tile],
        out_shape=[jax.ShapeDtypeStruct((S, F), BF16)] * 3,
        compiler_params=_cparams("parallel", "parallel"),
        name=name,
    )(up, up, up, up, cw, cw, cb, cb)


def _convglu_bwd(df, a, b, up, cw, name):
    S, F = df.shape
    tm, tn = _tile(S, 512), _tile(F, 1408)
    nj, ni, hb = F // tn, S // tm, tm // HALO
    n = tm + HALO

    def body(df_ref, dfn_ref, a_ref, an_ref, b_ref, bn_ref, up_ref, w_ref, o_ref, db_ref, dw_ref):
        j, i = pl.program_id(0), pl.program_id(1)
        last = i == ni - 1

        def rows(c_ref, n_ref):
            return jnp.concatenate([c_ref[...].astype(F32), jnp.where(last, 0.0, n_ref[...].astype(F32))], axis=0)

        @pl.when(i == 0)
        def _():
            db_ref[...] = jnp.zeros_like(db_ref)
            dw_ref[...] = jnp.zeros_like(dw_ref)

        def finish(d):
            i_ = lax.broadcasted_iota(jnp.int32, (2 * SHIFT_ROWS, SHIFT_ROWS), 0)
            j_ = lax.broadcasted_iota(jnp.int32, (2 * SHIFT_ROWS, SHIFT_ROWS), 1)
            ahead = (j_ == jnp.where(i_ < SHIFT_ROWS, i_ + 1, i_ - SHIFT_ROWS + 2)).astype(BF16)
            db = d.astype(BF16)
            sub = lax.broadcasted_iota(jnp.int32, (8, tn), 0)
            d1, d2 = [], []
            for blk in range(tm // SHIFT_ROWS):
                lo, hi = blk * SHIFT_ROWS, (blk + 1) * SHIFT_ROWS
                both = _dot_nn(ahead, db[lo:hi])
                n1, n2 = both[:SHIFT_ROWS], both[SHIFT_ROWS:]
                after = db[hi:hi + HALO].astype(F32)[:8]
                d1 += [n1[:-8], jnp.where(sub == 7, pltpu.roll(after, 7, 0), n1[-8:])]
                d2 += [n2[:-8], jnp.where(sub >= 6, pltpu.roll(after, 6, 0), n2[-8:])]
            d0, d1, d2 = d[:tm], jnp.concatenate(d1, axis=0), jnp.concatenate(d2, axis=0)
            w = w_ref[...]
            o_ref[...] = (w[2:3] * d0 + w[1:2] * d1 + w[0:1] * d2).astype(o_ref.dtype)
            upv = up_ref[...].astype(F32)
            db_ref[...] += jnp.sum(d0, axis=0, keepdims=True)
            dw_ref[0:1, :] += jnp.sum(d2 * upv, axis=0, keepdims=True)
            dw_ref[1:2, :] += jnp.sum(d1 * upv, axis=0, keepdims=True)
            dw_ref[2:3, :] += jnp.sum(d0 * upv, axis=0, keepdims=True)

        av, dfv = rows(a_ref, an_ref), rows(df_ref, dfn_ref)
        cdf = 0.5 * (1.0 + lax.erf(av * INV_SQRT2))

        @pl.when(j < nj)
        def _():
            pdf = jnp.exp(-0.5 * av * av) * INV_SQRT_2PI
            finish(dfv * rows(b_ref, bn_ref) * (cdf + av * pdf))

        @pl.when(j >= nj)
        def _():
            finish(dfv * (av * cdf))

    jh = lambda j: lax.rem(j, nj)
    nxt = lambda i: jnp.minimum((i + 1) * hb, S // HALO - 1)
    cur = pl.BlockSpec((tm, tn), lambda j, i: (i, jh(j)))
    halo = pl.BlockSpec((HALO, tn), lambda j, i: (nxt(i), jh(j)))
    return pl.pallas_call(
        body,
        grid=(2 * nj, ni),
        in_specs=[cur, halo, cur, halo, cur, halo, pl.BlockSpec((tm, tn), lambda j, i: (i, j)), pl.BlockSpec((3, tn), lambda j, i: (0, j))],
        out_specs=[pl.BlockSpec((tm, tn), lambda j, i: (i, j)), pl.BlockSpec((1, tn), lambda j, i: (0, j)), pl.BlockSpec((3, tn), lambda j, i: (0, j))],
        out_shape=[jax.ShapeDtypeStruct((S, 2 * F), BF16), jax.ShapeDtypeStruct((1, 2 * F), F32), jax.ShapeDtypeStruct((3, 2 * F), F32)],
        compiler_params=_cparams("parallel", "arbitrary"),
        name=name,
    )(df, df, a, a, b, b, up, cw)


def _gate_bwd(dmixed, gates, y_pool, y_attn, in_width, name):
    S, D = dmixed.shape
    tm, tn = _tile(S, 2048), _tile(D, 512)
    nj = D // tn
    pre0 = (in_width - 2 * D) // tn
    assert pre0 * tn == in_width - 2 * D

    def body(dm_ref, g_ref, yp_ref, ya_ref, dy_ref, dpre_ref, db_ref):
        j = pl.program_id(0)

        @pl.when(pl.program_id(1) == 0)
        def _():
            db_ref[...] = jnp.zeros_like(db_ref)

        def run(y_ref):
            dm = dm_ref[...].astype(F32)
            gv = g_ref[...].astype(F32)
            dy_ref[...] = (dm * gv).astype(BF16)
            dpre = dm * y_ref[...].astype(F32) * gv * (1.0 - gv)
            dpre_ref[...] = dpre.astype(BF16)
            db_ref[...] += jnp.sum(dpre, axis=0, keepdims=True)

        @pl.when(j < nj)
        def _():
            run(yp_ref)

        @pl.when(j >= nj)
        def _():
            run(ya_ref)

    tile2 = pl.BlockSpec((tm, tn), lambda j, i: (i, j))
    return pl.pallas_call(
        body,
        grid=(2 * nj, S // tm),
        in_specs=[
            pl.BlockSpec((tm, tn), lambda j, i: (i, lax.rem(j, nj))),
            tile2,
            pl.BlockSpec((tm, tn), lambda j, i: (i, jnp.minimum(j, nj - 1))),
            pl.BlockSpec((tm, tn), lambda j, i: (i, jnp.maximum(j - nj, 0))),
        ],
        out_specs=[tile2, pl.BlockSpec((tm, tn), lambda j, i: (i, pre0 + j)), pl.BlockSpec((1, tn), lambda j, i: (0, j))],
        out_shape=[jax.ShapeDtypeStruct((S, 2 * D), BF16), jax.ShapeDtypeStruct((S, in_width), BF16), jax.ShapeDtypeStruct((1, 2 * D), F32)],
        compiler_params=_cparams("parallel", "arbitrary"),
        name=name,
    )(dmixed, gates, y_pool, y_attn)


def _pool_counts(i, tm, rows, w):
    t = i * tm + lax.broadcasted_iota(jnp.int32, (rows, 1), 0)
    return jnp.minimum(t + 1, w).astype(F32)


def _pooled_groups(u_ref, uh_ref, i, tm, C):
    cur = u_ref[...]
    halo = jnp.where(i == 0, 0.0, uh_ref[...])
    xx = jnp.concatenate([halo, cur], axis=0)
    out = []
    s = xx
    for gi, w in enumerate(POOL_WINDOWS):
        s = s + pltpu.roll(s, w // 2, 0)
        tot = s[HALO:, 0:C]
        out.append(tot / _pool_counts(i, tm, tm, w) - cur[:, gi * C:(gi + 1) * C])
        s = s[:, C:] if gi + 1 < len(POOL_WINDOWS) else s
    return out


def _pool_fwd(u, wl, scale, name):
    S, PW = u.shape
    C = PW // len(POOL_WINDOWS)
    tm = _tile(S, 512)
    hb = tm // HALO

    def body(u_ref, uh_ref, wl_ref, sc_ref, o_ref):
        i = pl.program_id(0)
        pooled = _pooled_groups(u_ref, uh_ref, i, tm, C)
        for gi in range(len(POOL_WINDOWS)):
            y = _dot_nn(pooled[gi].astype(BF16), wl_ref[gi])
            o_ref[:, gi * C:(gi + 1) * C] = (y * sc_ref[:, gi * C:(gi + 1) * C]).astype(o_ref.dtype)

    return pl.pallas_call(
        body,
        grid=(S // tm,),
        in_specs=[
            pl.BlockSpec((tm, PW), lambda i: (i, 0)),
            pl.BlockSpec((HALO, PW), lambda i: (jnp.maximum(i * hb - 1, 0), 0)),
            pl.BlockSpec((len(POOL_WINDOWS), C, C), lambda i: (0, 0, 0)),
            pl.BlockSpec((1, PW), lambda i: (0, 0)),
        ],
        out_specs=pl.BlockSpec((tm, PW), lambda i: (i, 0)),
        out_shape=jax.ShapeDtypeStruct((S, PW), BF16),
        compiler_params=_cparams("parallel"),
        name=name,
    )(u, u, wl, scale)


def _pool_bwd(u, dp, wl, scale, dproj, name):
    S, PW = u.shape
    G = len(POOL_WINDOWS)
    C = PW // G
    tm = _tile(S, 512)
    hb, ni = tm // HALO, S // tm
    n = tm + HALO

    def body(u_ref, uh_ref, dp_ref, dpn_ref, wl_ref, sc_ref, _, du_ref, dwl_ref, dsc_ref):
        i = pl.program_id(0)

        @pl.when(i == 0)
        def _():
            dwl_ref[...] = jnp.zeros_like(dwl_ref)
            dsc_ref[...] = jnp.zeros_like(dsc_ref)

        pooled = _pooled_groups(u_ref, uh_ref, i, tm, C)
        dpc = dp_ref[...].astype(F32)
        dpn = jnp.where(i == ni - 1, 0.0, dpn_ref[...].astype(F32))
        sc = sc_ref[...]
        dyl = jnp.concatenate([dpc, dpn], axis=0) * sc
        for gi, w in enumerate(POOL_WINDOWS):
            cols = slice(gi * C, (gi + 1) * C)
            pb = pooled[gi].astype(BF16)
            ylin = _dot_nn(pb, wl_ref[gi])
            dsc_ref[:, cols] += jnp.sum(dpc[:, cols] * ylin, axis=0, keepdims=True)
            dylg = dyl[:, cols].astype(BF16)
            dwl_ref[gi] += _dot(pb, dylg[:tm], ((0,), (0,)))
            dpool = _dot_nt(dylg, wl_ref[gi])
            e = dpool / _pool_counts(i, tm, n, w)
            k = 1
            while k < w:
                e = e + pltpu.roll(e, n - k, 0)
                k *= 2
            du_ref[:, cols] = (e[:tm] - dpool[:tm]).astype(du_ref.dtype)

    return pl.pallas_call(
        body,
        grid=(ni,),
        in_specs=[
            pl.BlockSpec((tm, PW), lambda i: (i, 0)),
            pl.BlockSpec((HALO, PW), lambda i: (jnp.maximum(i * hb - 1, 0), 0)),
            pl.BlockSpec((tm, PW), lambda i: (i, 0)),
            pl.BlockSpec((HALO, PW), lambda i: (jnp.minimum((i + 1) * hb, S // HALO - 1), 0)),
            pl.BlockSpec((G, C, C), lambda i: (0, 0, 0)),
            pl.BlockSpec((1, PW), lambda i: (0, 0)),
            ANY,
        ],
        out_specs=[pl.BlockSpec((tm, PW), lambda i: (i, 0)), pl.BlockSpec((G, C, C), lambda i: (0, 0, 0)), pl.BlockSpec((1, PW), lambda i: (0, 0))],
        out_shape=[jax.ShapeDtypeStruct(dproj.shape, dproj.dtype), jax.ShapeDtypeStruct((G, C, C), F32), jax.ShapeDtypeStruct((1, PW), F32)],
        input_output_aliases={6: 0},
        compiler_params=_cparams("arbitrary"),
        name=name,
    )(u, u, dp, dp, wl, scale, dproj)


def _band_masks():
    ii = lax.broadcasted_iota(jnp.int32, (SPAN, SPAN), 0)
    kk = lax.broadcasted_iota(jnp.int32, (SPAN, SPAN), 1)
    return ((ii + SPAN - kk).astype(F32), kk >= ii), ((ii - kk).astype(F32), kk <= ii)


ATTN_TILE = 16 * SPAN


def _unit_rows(r, b, d, blocks=1):
    return pl.ds(d * SPAN * b + r, blocks * SPAN, stride=d) if d > 1 else pl.ds(SPAN * b, blocks * SPAN)


def _f32_copies(refs, scratch, d):
    if d == 1:
        return list(refs)
    for ref, s in zip(refs, scratch):
        s[...] = ref[...].astype(F32)
    return list(scratch)


def _attn_fwd(qkv, d, g, name):
    S = qkv.shape[0]
    T = min(ATTN_TILE, S)
    P = SPAN * d
    nbk = T // P

    def body(q_ref, k_ref, v_ref, kp_ref, vp_ref, o_ref, lse_ref, *scratch):
        c = pl.program_id(0)
        (jp, mp), (jc, mc) = _band_masks()
        slopes = [ALIBI_SLOPES[g * HEADS_PER_GROUP + h] * d for h in range(HEADS_PER_GROUP)]
        slope = slopes[0]
        for h in range(1, HEADS_PER_GROUP):
            slope = jnp.where(pl.program_id(1) == h, slopes[h], slope)
        q_s, k_s, v_s, kp_s, vp_s = _f32_copies((q_ref, k_ref, v_ref, kp_ref, vp_ref), scratch[:5], d)
        o_s, l_s = (o_ref, lse_ref) if d == 1 else scratch[5:7]
        bias_p, bias_c = jnp.where(mp, -slope * jp, NEG_BIG), jnp.where(mc, -slope * jc, NEG_BIG)
        bias = jnp.concatenate([bias_p, bias_c], axis=1)
        bias_first = jnp.concatenate([jnp.where(c > 0, bias_p, NEG_BIG), bias_c], axis=1)
        for r in range(d):
            for b in range(nbk):
                rows = _unit_rows(r, b, d)
                q = q_s[rows, :].astype(BF16)
                if b == 0:
                    prev = _unit_rows(r, 0, d)
                    kk = jnp.concatenate([kp_s[prev, :], k_s[rows, :]], axis=0).astype(BF16)
                    vv = jnp.concatenate([vp_s[prev, :], v_s[rows, :]], axis=0).astype(BF16)
                else:
                    both = _unit_rows(r, b - 1, d, 2)
                    kk, vv = k_s[both, :].astype(BF16), v_s[both, :].astype(BF16)
                s = _dot_nt(q, kk) * ATTN_SCALE + (bias_first if b == 0 else bias)
                m = jnp.max(s, axis=-1, keepdims=True)
                p = jnp.exp(s - m)
                l = jnp.sum(p, axis=-1, keepdims=True)
                o_s[rows, :] = _dot_nn(p.astype(BF16), vv) / l
                l_s[rows, :] = jnp.broadcast_to(m + jnp.log(l), (SPAN, HEAD_DIM))
        if d > 1:
            o_ref[...] = o_s[...]
            lse_ref[...] = l_s[...]

    col = lambda kind: (lambda c, h: (c, kind * N_ATTN_HEADS + g * HEADS_PER_GROUP + h))
    pcol = lambda kind: (lambda c, h: (jnp.maximum(c * nbk - 1, 0), kind * N_ATTN_HEADS + g * HEADS_PER_GROUP + h))
    cur = lambda kind: pl.BlockSpec((T, HEAD_DIM), col(kind))
    prv = lambda kind: pl.BlockSpec((P, HEAD_DIM), pcol(kind))
    out = pl.BlockSpec((T, HEAD_DIM), lambda c, h: (c, h))
    scratch = [] if d == 1 else [pltpu.VMEM((T, HEAD_DIM), F32)] * 3 + [pltpu.VMEM((P, HEAD_DIM), F32)] * 2 + [pltpu.VMEM((T, HEAD_DIM), F32)] * 2
    return pl.pallas_call(
        body,
        grid=(S // T, HEADS_PER_GROUP),
        in_specs=[cur(0), cur(1), cur(2), prv(1), prv(2)],
        out_specs=[out, out],
        out_shape=[jax.ShapeDtypeStruct((S, GROUP_WIDTH), F32)] * 2,
        scratch_shapes=scratch,
        compiler_params=_cparams("parallel", "parallel"),
        name=name,
    )(qkv, qkv, qkv, qkv, qkv)


def _attn_merge(os_, lses, name):
    S, W = os_[0].shape
    tm = _tile(S, 1024)

    def body(o0, o1, o2, l0, l1, l2, y_ref, lse_ref):
        ls = [l0[...], l1[...], l2[...]]
        m = jnp.maximum(jnp.maximum(ls[0], ls[1]), ls[2])
        es = [jnp.exp(v - m) for v in ls]
        tot = es[0] + es[1] + es[2]
        y = (es[0] * o0[...] + es[1] * o1[...] + es[2] * o2[...]) / tot
        y_ref[...] = y.astype(y_ref.dtype)
        lse_ref[...] = m + jnp.log(tot)

    row = pl.BlockSpec((tm, W), lambda i: (i, 0))
    return pl.pallas_call(
        body,
        grid=(S // tm,),
        in_specs=[row] * 6,
        out_specs=[row, row],
        out_shape=[jax.ShapeDtypeStruct((S, W), BF16), jax.ShapeDtypeStruct((S, W), F32)],
        compiler_params=_cparams("parallel"),
        name=name,
    )(*os_, *lses)


def _attn_bwd(qkv, dattn, y, lse, dproj, d, g, col0, name):
    S = qkv.shape[0]
    T = min(ATTN_TILE, S)
    P = SPAN * d
    nbk = T // P
    ntile = S // T

    def body(q_ref, k_ref, v_ref, kp_ref, vp_ref, qn_ref, da_ref, dan_ref, y_ref, yn_ref, lse_ref, lsen_ref, _, out_ref, dq_s, dk_s, dv_s, *scratch):
        c = pl.program_id(0)
        head_id = pl.program_id(1)
        kind = pl.program_id(2)

        @pl.when(kind == 0)
        def _():
            (jp, mp), (jc, mc) = _band_masks()
            slopes = [ALIBI_SLOPES[g * HEADS_PER_GROUP + h] * d for h in range(HEADS_PER_GROUP)]
            slope = slopes[0]
            for h in range(1, HEADS_PER_GROUP):
                slope = jnp.where(head_id == h, slopes[h], slope)
            q_s, k_s, v_s, da_s, y_s, kp_s, vp_s, qn_s, dan_s, yn_s = _f32_copies(
                (q_ref, k_ref, v_ref, da_ref, y_ref, kp_ref, vp_ref, qn_ref, dan_ref, yn_ref), scratch, d)
            bias_p, bias_c = jnp.where(mp, -slope * jp, NEG_BIG), jnp.where(mc, -slope * jc, NEG_BIG)
            bias = jnp.concatenate([bias_c, bias_p], axis=0)
            bias_last = jnp.concatenate([bias_c, jnp.where(c < ntile - 1, bias_p, NEG_BIG)], axis=0)
            bias_first = jnp.where(c > 0, bias_p, NEG_BIG)

            def pair(q, da, yy, lse_blk, kk, vv, b):
                dd = jnp.sum(da.astype(F32) * yy.astype(F32), axis=-1, keepdims=True)
                p = jnp.exp(_dot_nt(q, kk) * ATTN_SCALE + b - lse_blk[:, 0:1])
                return p, p * (_dot_nt(da, vv) - dd)

            for r in range(d):
                first = _unit_rows(r, 0, d)
                kk, vv = kp_s[first, :].astype(BF16), vp_s[first, :].astype(BF16)
                _, ds = pair(q_s[first, :].astype(BF16), da_s[first, :].astype(BF16), y_s[first, :], lse_ref[first, :], kk, vv, bias_first)
                dq_next = _dot_nn(ds.astype(BF16), kk)
                for kb in range(nbk):
                    rows = _unit_rows(r, kb, d)
                    if kb + 1 < nbk:
                        both = _unit_rows(r, kb, d, 2)
                        q, da, yy, lse_blk = q_s[both, :], da_s[both, :], y_s[both, :], lse_ref[both, :]
                    else:
                        q = jnp.concatenate([q_s[rows, :], qn_s[first, :]], axis=0)
                        da = jnp.concatenate([da_s[rows, :], dan_s[first, :]], axis=0)
                        yy = jnp.concatenate([y_s[rows, :], yn_s[first, :]], axis=0)
                        lse_blk = jnp.concatenate([lse_ref[rows, :], lsen_ref[first, :]], axis=0)
                    q, da = q.astype(BF16), da.astype(BF16)
                    kk, vv = k_s[rows, :].astype(BF16), v_s[rows, :].astype(BF16)
                    p, ds = pair(q, da, yy, lse_blk, kk, vv, bias if kb + 1 < nbk else bias_last)
                    dv_s[rows, :] = _dot_nn(p.T.astype(BF16), da)
                    dk_s[rows, :] = _dot_nn(ds.T.astype(BF16), q) * ATTN_SCALE
                    dq_both = _dot_nn(ds.astype(BF16), kk)
                    dq_s[rows, :] = (dq_next + dq_both[:SPAN]) * ATTN_SCALE
                    dq_next = dq_both[SPAN:]
            out_ref[...] = dq_s[...].astype(out_ref.dtype)

        @pl.when(kind == 1)
        def _():
            out_ref[...] = dk_s[...].astype(out_ref.dtype)

        @pl.when(kind == 2)
        def _():
            out_ref[...] = dv_s[...].astype(out_ref.dtype)

    head = lambda h: g * HEADS_PER_GROUP + h
    cur = lambda kind: pl.BlockSpec((T, HEAD_DIM), lambda c, h, kd: (c, kind * N_ATTN_HEADS + head(h)))
    prv = lambda kind: pl.BlockSpec((P, HEAD_DIM), lambda c, h, kd: (jnp.maximum(c * nbk - 1, 0), kind * N_ATTN_HEADS + head(h)))
    nxt_row = lambda c: jnp.minimum((c + 1) * nbk, S // P - 1)
    qnext = pl.BlockSpec((P, HEAD_DIM), lambda c, h, kd: (nxt_row(c), head(h)))
    hcur = pl.BlockSpec((T, HEAD_DIM), lambda c, h, kd: (c, h))
    hnext = pl.BlockSpec((P, HEAD_DIM), lambda c, h, kd: (nxt_row(c), h))
    out = pl.BlockSpec((T, HEAD_DIM), lambda c, h, kd: (c, col0 + kd * N_ATTN_HEADS + head(h)))
    stage = [pltpu.VMEM((T, HEAD_DIM), F32)] * 3
    copies = [] if d == 1 else [pltpu.VMEM((T, HEAD_DIM), F32)] * 5 + [pltpu.VMEM((P, HEAD_DIM), F32)] * 5
    return pl.pallas_call(
        body,
        grid=(ntile, HEADS_PER_GROUP, 3),
        in_specs=[cur(0), cur(1), cur(2), prv(1), prv(2), qnext, hcur, hnext, hcur, hnext, hcur, hnext, ANY],
        out_specs=out,
        out_shape=jax.ShapeDtypeStruct(dproj.shape, dproj.dtype),
        input_output_aliases={12: 0},
        scratch_shapes=stage + copies,
        compiler_params=_cparams("parallel", "parallel", "arbitrary"),
        name=name,
    )(qkv, qkv, qkv, qkv, qkv, qkv, dattn, dattn, y, y, lse, lse, dproj)


def _row_block(R, C, bytes_per_row_elem=4, budget=1 << 20):
    if R % 8:
        return R
    best = 8
    t = 8
    while t <= R:
        if R % t == 0 and t * C * bytes_per_row_elem <= budget:
            best = t
        t += 8
    return best


def _adamw(w, g, m, v, name):
    R, C = w.shape
    tr = _row_block(R, C, budget=2 << 20)
    c1 = 1.0 - ADAM_B1 ** ADAM_STEP
    c2 = 1.0 - ADAM_B2 ** ADAM_STEP

    def body(w_ref, g_ref, m_ref, v_ref, d_ref, nm_ref, nv_ref):
        gv = g_ref[...]
        nm = ADAM_B1 * m_ref[...] + (1.0 - ADAM_B1) * gv
        nv = ADAM_B2 * v_ref[...] + (1.0 - ADAM_B2) * (gv * gv)
        d_ref[...] = -ADAM_LR * ((nm / c1) / (jnp.sqrt(nv / c2) + ADAM_EPS) + ADAM_WD * w_ref[...])
        nm_ref[...] = nm
        nv_ref[...] = nv

    blk = pl.BlockSpec((tr, C), lambda i: (i, 0))
    return pl.pallas_call(
        body,
        grid=(R // tr,),
        in_specs=[blk] * 4,
        out_specs=[blk] * 3,
        out_shape=[jax.ShapeDtypeStruct((R, C), F32)] * 3,
        compiler_params=_cparams("parallel"),
        name=name,
    )(w, g, m, v)


def _sum_pieces(grad, axis, recv, pos, name):
    n, pr, pc = recv.shape
    tr = _row_block(pr, pc, bytes_per_row_elem=(n + 1) * recv.dtype.itemsize, budget=4 << 20)
    nblk = pr // tr
    if axis == 1:
        own_map = lambda i, p: (p[1] * nblk + i, p[0])
    else:
        own_map = lambda i, p: ((2 * p[0] + p[1]) * nblk + i, 0)

    def body(p_ref, own_ref, r_ref, o_ref):
        acc = own_ref[...].astype(F32)
        for s in range(n):
            acc = acc + r_ref[s].astype(F32)
        o_ref[...] = acc

    return pl.pallas_call(
        body,
        grid_spec=pltpu.PrefetchScalarGridSpec(
            num_scalar_prefetch=1,
            grid=(nblk,),
            in_specs=[pl.BlockSpec((tr, pc), own_map), pl.BlockSpec((n, tr, pc), lambda i, p: (0, i, 0))],
            out_specs=pl.BlockSpec((tr, pc), lambda i, p: (p[1] * nblk + i, 0)),
        ),
        out_shape=jax.ShapeDtypeStruct((2 * pr, pc), F32),
        compiler_params=_cparams("parallel"),
        name=name,
    )(pos, grad, recv)


def _sum_small(own, recv, me, name):
    n, R, C = recv.shape
    tr = _row_block(R, C, bytes_per_row_elem=(n + 1) * 4, budget=4 << 20)

    def body(me_ref, own_ref, r_ref, o_ref):
        acc = None
        for dev in range(n + 1):
            k = jnp.bitwise_xor(me_ref[0], dev)
            term = jnp.where(k == 0, own_ref[...], r_ref[jnp.maximum(k - 1, 0)])
            acc = term if acc is None else acc + term
        o_ref[...] = acc

    return pl.pallas_call(
        body,
        grid_spec=pltpu.PrefetchScalarGridSpec(
            num_scalar_prefetch=1,
            grid=(R // tr,),
            in_specs=[pl.BlockSpec((tr, C), lambda i, m: (i, 0)), pl.BlockSpec((n, tr, C), lambda i, m: (0, i, 0))],
            out_specs=pl.BlockSpec((tr, C), lambda i, m: (i, 0)),
        ),
        out_shape=jax.ShapeDtypeStruct((R, C), F32),
        compiler_params=_cparams("parallel"),
        name=name,
    )(me, own, recv)


def _place(shard, axis, pos, dtype, name, after=None):
    extra = [] if after is None else [after]
    shp = list(shard.shape)
    shp[axis] *= N_CHIPS
    if shard.ndim == 3:
        assert axis == 1
        in_spec = pl.BlockSpec(shard.shape, lambda i, p: (0, 0, 0))
        out_spec = pl.BlockSpec(shard.shape, lambda i, p: (0, p[0], 0))
        grid = (1,)
    else:
        R, C = shard.shape
        tr = _row_block(R, C, bytes_per_row_elem=4, budget=2 << 20)
        nblk = R // tr
        in_spec = pl.BlockSpec((tr, C), lambda i, p: (i, 0))
        out_spec = pl.BlockSpec((tr, C), (lambda i, p: (i, p[0])) if axis == 1 else (lambda i, p: (p[0] * nblk + i, 0)))
        grid = (nblk,)

    def body(*refs):
        s_ref, o_ref = refs[1], refs[-1]
        o_ref[...] = s_ref[...].astype(o_ref.dtype)

    return pl.pallas_call(
        body,
        grid_spec=pltpu.PrefetchScalarGridSpec(num_scalar_prefetch=1, grid=grid, in_specs=[in_spec] + [ANY] * len(extra), out_specs=out_spec),
        out_shape=jax.ShapeDtypeStruct(tuple(shp), dtype),
        compiler_params=_cparams("parallel"),
        name=name,
    )(pos, shard, *extra)


HBM = pl.BlockSpec(memory_space=pltpu.HBM)
SEM = pl.BlockSpec(memory_space=pltpu.SEMAPHORE)
DATAFLOW = pltpu.SideEffectType.DATAFLOW_SIDE_EFFECTING


def _position():
    return lax.axis_index("x"), lax.axis_index("y"), lax.axis_index("c")


def _peer(k):
    x, y, c = _position()
    return ((1 - x) if k & 4 else x, (1 - y) if k & 2 else y, (1 - c) if k & 1 else c)


def _shard_slice(ref, axis, idx, size):
    start = idx * size
    if axis == ref.ndim - 1:
        start = pl.multiple_of(start, 128)
    ix = [slice(None)] * ref.ndim
    ix[axis] = pl.ds(start, size)
    return ref.at[tuple(ix)]


def _gather_plan(axes):
    def plan(refs):
        x, y, c = _position()
        out = []
        for ref, ax in zip(refs, axes):
            mine = _shard_slice(ref, ax, 2 * x + y, ref.shape[ax] // N_CHIPS)
            for k in (4, 2, 6):
                px, py, _ = _peer(k)
                out.append((mine, mine, (px, py, c)))
        return out
    return plan


def _scatter_plan(axes):
    m = len(axes)

    def plan(refs):
        out = []
        for t in range(m):
            grad, recv = refs[t], refs[m + t]
            _, pr, pc = recv.shape
            for k in range(1, N_DEV):
                px, py, pcore = _peer(k)
                if axes[t] == 0:
                    piece = grad.at[pl.ds(((2 * px + py) * 2 + pcore) * pr, pr), :]
                else:
                    piece = grad.at[pl.ds(pcore * pr, pr), pl.ds(pl.multiple_of((2 * px + py) * pc, 128), pc)]
                out.append((piece, recv.at[k - 1], (px, py, pcore)))
        return out
    return plan


def _broadcast_plan(refs):
    small, recv = refs
    return [(small, recv.at[k - 1], _peer(k)) for k in range(1, N_DEV)]


def _start_all(plan, refs, send_sems, recv_sems):
    for q, (src, dst, dev) in enumerate(plan(refs)):
        pltpu.make_async_remote_copy(src_ref=src, dst_ref=dst, send_sem=send_sems.at[q], recv_sem=recv_sems.at[q], device_id=dev, device_id_type=MESH).start()


def _wait_all(plan, refs, send_sems, recv_sems):
    for q, (src, dst, dev) in enumerate(plan(refs)):
        cp = pltpu.make_async_remote_copy(src_ref=src, dst_ref=dst, send_sem=send_sems.at[q], recv_sem=recv_sems.at[q], device_id=dev, device_id_type=MESH)
        cp.wait_send()
        cp.wait_recv()


def _push(bufs, plan, ncopies, name):
    n = len(bufs)

    def body(*refs):
        outs = refs[n:2 * n]
        send_sems, recv_sems = refs[2 * n:]
        _start_all(plan, outs, send_sems, recv_sems)
        _wait_all(plan, outs, send_sems, recv_sems)

    return pl.pallas_call(
        body,
        in_specs=[ANY] * n,
        out_specs=[ANY] * n,
        out_shape=[jax.ShapeDtypeStruct(b.shape, b.dtype) for b in bufs],
        input_output_aliases={t: t for t in range(n)},
        scratch_shapes=[pltpu.SemaphoreType.DMA((ncopies,)), pltpu.SemaphoreType.DMA((ncopies,))],
        name=name,
    )(*bufs)


def _half_slices_plan(onward):
    def plan(refs):
        ref, = refs
        x, y, c = _position()
        R2, C4 = ref.shape[0] // 2, ref.shape[1] // N_CHIPS
        out = []
        for k in (4, 2, 6):
            px, py, _ = _peer(k)
            chip = (2 * px + py) if onward else (2 * x + y)
            half = ref.at[pl.ds(c * R2, R2), pl.ds(pl.multiple_of(chip * C4, 128), C4)]
            out.append((half, half, (x, y, 1 - c) if onward else (px, py, c)))
        return out
    return plan


def _push_start(bufs, plan, ncopies, name, after=None):
    n = len(bufs)
    extra = [] if after is None else [after]

    def body(*refs):
        ins = refs[:n]
        first_out = n + len(extra)
        send_sems, recv_sems, token = refs[first_out], refs[first_out + 1], refs[-1]
        _start_all(plan, ins, send_sems, recv_sems)
        token[...] = jnp.zeros_like(token)

    res = pl.pallas_call(
        body,
        name=name,
        out_shape=(pltpu.SemaphoreType.DMA((ncopies,)), pltpu.SemaphoreType.DMA((ncopies,)), *[pltpu.HBM(b.shape, b.dtype) for b in bufs],
                   jax.ShapeDtypeStruct((8, 128), F32)),
        in_specs=[HBM] * n + [ANY] * len(extra),
        out_specs=(SEM, SEM, *[HBM] * n, pl.BlockSpec(memory_space=pltpu.VMEM)),
        input_output_aliases={t: t + 2 for t in range(n)},
        compiler_params=pltpu.CompilerParams(has_side_effects=DATAFLOW),
    )(*[pltpu.with_memory_space_constraint(b, pltpu.HBM) for b in bufs], *extra)
    return res[0], res[1], list(res[2:2 + n]), res[-1]


def _push_wait(send_sems, recv_sems, bufs, plan, after, name):
    n = len(bufs)
    after = list(after) if isinstance(after, (list, tuple)) else [after]

    def body(*refs):
        ins = refs[:n]
        _wait_all(plan, ins, refs[n], refs[n + 1])

    return pl.pallas_call(
        body,
        name=name,
        out_shape=tuple(pltpu.HBM(b.shape, b.dtype) for b in bufs),
        in_specs=[HBM] * n + [SEM, SEM] + [ANY] * len(after),
        out_specs=tuple([HBM] * n),
        input_output_aliases={t: t for t in range(n)},
        compiler_params=pltpu.CompilerParams(has_side_effects=DATAFLOW),
    )(*bufs, send_sems, recv_sems, *after)


EXCHANGE_CHUNKS = 2


def _exchange_plan(refs):
    x, y, c = _position()
    out = []
    for ref in refs:
        rows = ref.shape[0] // (2 * EXCHANGE_CHUNKS)
        for q in range(EXCHANGE_CHUNKS):
            mine = ref.at[pl.ds((c * EXCHANGE_CHUNKS + q) * rows, rows), :]
            out.append((mine, mine, (x, y, 1 - c)))
    return out


LATE_WEIGHTS = (("w_pool_lin", "w_pool_out", "w_attn_out", "w_out"), ("w_up", "conv_w", "w_down"))


def _local_step(x, tgt, w, late_weights, send):
    S, D = x.shape
    PW = w["pool_scale"].shape[1]
    o_q = PW
    o_g = PW + 3 * ATTN_WIDTH
    QKV = 3 * ATTN_WIDTH

    h1 = _rms_fwd(x, w["g_mix"], "rms1")
    w = dict(w, **late_weights("w_in", h1))
    proj_tiles = (_tile(S, 1024), 512, D)
    started = w.get("late_started")
    u = _mm(h1, w["w_in"], mode="nn", dims=(S, PW, D), tiles=proj_tiles, out_dtypes=(F32,), after=started, name="proj_u")
    qkv = _mm(h1, w["w_in"], mode="nn", dims=(S, QKV, D), tiles=proj_tiles, b_off=(0, o_q), after=started, name="proj_qkv")

    def gate_epilogue(acc, ex, outs):
        outs[0][...] = (1.0 / (1.0 + jnp.exp(-(acc + ex[0][...])))).astype(outs[0].dtype)

    gates = _mm(h1, w["w_in"], mode="nn", dims=(S, 2 * D, D), tiles=proj_tiles, b_off=(0, o_g), epilogue=gate_epilogue,
                extras=[(w["b_gate"], "n", (0, 0))], name="proj_gates")

    os_, lses = [], []
    for gi, (_, d) in enumerate(ATTN_GROUPS):
        o, lse = _attn_fwd(qkv, d, gi, f"attn_fwd{gi}")
        os_.append(o)
        lses.append(lse)
    attn, lse_tot = _attn_merge(os_, lses, "attn_merge")

    w = dict(w, **late_weights(0, attn))
    pool_out = _pool_fwd(u, w["w_pool_lin"], w["pool_scale"], "pool_fwd")
    y_pool = _mm(pool_out, w["w_pool_out"], mode="nn", dims=(S, D, PW), name="y_pool")

    def mix_epilogue(acc, ex, outs):
        outs[0][...] = acc.astype(BF16)
        outs[1][...] = (ex[0][...].astype(F32) * ex[2][...].astype(F32) + ex[1][...].astype(F32) * acc).astype(BF16)

    y_attn, mixed = _mm(attn, w["w_attn_out"], mode="nn", dims=(S, D, GROUP_WIDTH), out_dtypes=(BF16, BF16), epilogue=mix_epilogue,
                        extras=[(gates, "mn", (0, 0)), (gates, "mn", (0, D)), (y_pool, "mn", (0, 0))], name="y_attn_mix")

    def residual_epilogue(acc, ex, outs):
        outs[0][...] = ex[0][...] + acc

    x2 = _mm(mixed, w["w_out"], mode="nn", dims=(S, D, D), out_dtypes=(F32,), epilogue=residual_epilogue, extras=[(x, "mn", (0, 0))], name="out_proj")

    h2 = _rms_fwd(x2, w["g_ffn"], "rms2")
    w = dict(w, **late_weights(1, h2))
    F = w["w_down"].shape[0]
    up = _mm(h2, w["w_up"], mode="nn", dims=(S, 2 * F, D), name="up_proj")
    f, act_a, act_b = _convglu_fwd(up, w["conv_w"], w["conv_b"], "convglu_fwd")
    x3 = _mm(f, w["w_down"], mode="nn", dims=(S, D, F), out_dtypes=(F32,), epilogue=residual_epilogue, extras=[(x2, "mn", (0, 0))], name="down_proj")

    g = {}
    dx3b, g["g_final"], loss_cols = _loss_head(x3, tgt, w["g_final"], "loss_head")

    g["w_down"] = _mm(f, dx3b, mode="tn", dims=(F, D, S), name="dw_down")
    sent = send(("w_down",), g)
    df = _mm(dx3b, w["w_down"], mode="nt", dims=(S, F, D), name="d_f")
    dup, g["conv_b"], g["conv_w"] = _convglu_bwd(df, act_a, act_b, up, w["conv_w"] + sent, "convglu_bwd")
    g["w_up"] = _mm(h2, dup, mode="tn", dims=(D, 2 * F, S), name="dw_up")
    sent = send(("w_up",), g)
    dh2 = _mm(dup, w["w_up"], mode="nt", dims=(S, D, 2 * F), name="d_h2")
    dx2b, g["g_ffn"] = _rms_bwd(dh2, x2, w["g_ffn"] + sent, dx3b, "rms2_bwd", BF16)

    g["w_out"] = _mm(mixed, dx2b, mode="tn", dims=(D, D, S), name="dw_out")
    dmixed = _mm(dx2b, w["w_out"], mode="nt", dims=(S, D, D), name="d_mixed")
    IN = w["w_in"].shape[1]
    dy_both, dproj, g["b_gate"] = _gate_bwd(dmixed, gates, y_pool, y_attn, IN, "gate_bwd")

    g["w_pool_out"] = _mm(pool_out, dy_both, mode="tn", dims=(PW, D, S), name="dw_pool_out")
    g["w_attn_out"] = _mm(attn, dy_both, mode="tn", dims=(GROUP_WIDTH, D, S), b_off=(0, D), name="dw_attn_out")
    sent = send(("w_out", "w_pool_out", "w_attn_out"), g)
    dpool = _mm(dy_both, w["w_pool_out"], mode="nt", dims=(S, PW, D), name="d_pool")
    dattn = _mm(dy_both, w["w_attn_out"], mode="nt", dims=(S, GROUP_WIDTH, D), a_off=(0, D), name="d_attn")

    dproj, g["w_pool_lin"], g["pool_scale"] = _pool_bwd(u, dpool, w["w_pool_lin"], w["pool_scale"] + sent, dproj, "pool_bwd")
    g["loss_cols"] = loss_cols
    sent = send("small", g)

    for gi, (_, d) in enumerate(ATTN_GROUPS):
        dproj = _attn_bwd(qkv, dattn, attn, lse_tot, dproj, d, gi, PW // HEAD_DIM, f"attn_bwd{gi}")

    g["w_in"] = _mm(h1, dproj, mode="tn", dims=(D, IN, S), name="dw_in")
    sent = sent + send(("w_in",), g)
    dh1 = _mm(dproj, w["w_in"], mode="nt", dims=(S, D, IN), tiles=(_tile(S, 1024), _tile(D, 2048), _tile(IN, 2432)), name="d_h1")
    (grad_x, g["g_mix"]) = _rms_bwd(dh1, x, w["g_mix"] + sent, dx2b, "rms1_bwd", F32)
    return loss_cols, grad_x, g


BIG = ("w_in", "w_pool_out", "w_attn_out", "w_out", "w_up", "w_down")
BIG_AXIS = {"w_in": 1, "w_pool_out": 1, "w_attn_out": 1, "w_out": 0, "w_up": 1, "w_down": 0}
GATHER_AXIS = dict(BIG_AXIS, w_pool_lin=1, conv_w=1)
SMALL = ("loss_cols", "b_gate", "w_pool_lin", "pool_scale", "g_ffn", "conv_w", "conv_b", "g_final")
SMALL_COLS = 1024
ORDER = ("g_mix", "w_in", "b_gate", "w_pool_lin", "pool_scale", "w_pool_out", "w_attn_out", "w_out", "g_ffn", "w_up", "conv_w", "conv_b", "w_down", "g_final")


def _as_rows(parts):
    flat = jnp.concatenate([p.astype(F32).reshape(-1) for p in parts])
    rows = -(-flat.shape[0] // (8 * SMALL_COLS)) * 8
    return jnp.pad(flat, (0, rows * SMALL_COLS - flat.shape[0])).reshape(rows, SMALL_COLS)


def kernel(x, g_mix, w_in, b_gate, w_pool_lin, pool_scale, w_pool_out, w_attn_out, w_out, g_ffn, w_up, conv_w, conv_b, w_down, g_final, loss_target, m_g_mix, m_w_in, m_b_gate, m_w_pool_lin, m_pool_scale, m_w_pool_out, m_w_attn_out, m_w_out, m_g_ffn, m_w_up, m_conv_w, m_conv_b, m_w_down, m_g_final, v_g_mix, v_w_in, v_b_gate, v_w_pool_lin, v_pool_scale, v_w_pool_out, v_w_attn_out, v_w_out, v_g_ffn, v_w_up, v_conv_w, v_conv_b, v_w_down, v_g_final):
    shard = dict(g_mix=g_mix, w_in=w_in, b_gate=b_gate, w_pool_lin=w_pool_lin, pool_scale=pool_scale, w_pool_out=w_pool_out, w_attn_out=w_attn_out,
                 w_out=w_out, g_ffn=g_ffn, w_up=w_up, conv_w=conv_w, conv_b=conv_b, w_down=w_down, g_final=g_final)
    mom = dict(g_mix=m_g_mix, w_in=m_w_in, b_gate=m_b_gate, w_pool_lin=m_w_pool_lin, pool_scale=m_pool_scale, w_pool_out=m_w_pool_out, w_attn_out=m_w_attn_out,
               w_out=m_w_out, g_ffn=m_g_ffn, w_up=m_w_up, conv_w=m_conv_w, conv_b=m_conv_b, w_down=m_w_down, g_final=m_g_final)
    vel = dict(g_mix=v_g_mix, w_in=v_w_in, b_gate=v_b_gate, w_pool_lin=v_w_pool_lin, pool_scale=v_pool_scale, w_pool_out=v_w_pool_out, w_attn_out=v_w_attn_out,
               w_out=v_w_out, g_ffn=v_g_ffn, w_up=v_w_up, conv_w=v_conv_w, conv_b=v_conv_b, w_down=v_w_down, g_final=v_g_final)
    chip = 2 * lax.axis_index("x") + lax.axis_index("y")
    pos = jnp.stack([chip, lax.axis_index("c")]).astype(jnp.int32)
    me = (2 * chip + lax.axis_index("c")).astype(jnp.int32).reshape(1)
    D = x.shape[2]

    out_plan, on_plan = _half_slices_plan(False), _half_slices_plan(True)
    in_send, in_recv, in_bufs, in_token = _push_start([_place(shard["w_in"][0], GATHER_AXIS["w_in"], pos, BF16, "place_w_in")], out_plan, 3,
                                                      "comm_gather_w_in_start")
    placed = {k: _place(shard[k][0], GATHER_AXIS[k], pos, F32 if k == "conv_w" else BF16, f"place_{k}", after=in_token)
              for names in LATE_WEIGHTS for k in names}
    late = []

    def late_weights(stage, after):
        if stage == "w_in":
            landed = _push_wait(in_send, in_recv, in_bufs, out_plan, [after] + list(placed.values()), "comm_gather_w_in_wait")
            w_in_full, = _push(list(landed), on_plan, 3, "comm_gather_w_in_pass")
            late_token, prior = 0.0, w_in_full
            for st, names in enumerate(LATE_WEIGHTS):
                plan = _gather_plan([GATHER_AXIS[k] for k in names])
                send_sems, recv_sems, bufs, token = _push_start([placed[k] for k in names], plan, 3 * len(names), f"comm_gather_late{st}_start", after=prior)
                late.append((names, send_sems, recv_sems, bufs, plan))
                late_token, prior = late_token + token[0, 0], token
            return dict(w_in=w_in_full, b_gate=shard["b_gate"] + late_token, late_started=prior)
        names, send_sems, recv_sems, bufs, plan = late[stage]
        return dict(zip(names, _push_wait(send_sems, recv_sems, bufs, plan, after, f"comm_gather_late{stage}_wait")))

    pending = []

    def send(names, g):
        if names == "small":
            bufs = [_as_rows([g[k] for k in SMALL])]
            bufs.append(lax.empty((N_DEV - 1,) + bufs[0].shape, F32))
            plan, tag = _broadcast_plan, "small"
        else:
            bufs = [g[k] for k in names]
            for k in names:
                R, C = g[k].shape
                piece = (R // (2 * N_CHIPS), C) if BIG_AXIS[k] == 0 else (R // 2, C // N_CHIPS)
                bufs.append(lax.empty((N_DEV - 1,) + piece, BF16))
            plan, tag = _scatter_plan([BIG_AXIS[k] for k in names]), names[0]
        ncopies = (N_DEV - 1) * (len(bufs) // 2)
        send_sems, recv_sems, thru, token = _push_start(bufs, plan, ncopies, f"comm_scatter_start_{tag}")
        pending.append((names, send_sems, recv_sems, thru, plan, tag))
        return token[0, 0]

    w0 = dict(g_mix=shard["g_mix"] + in_token[0, 0], pool_scale=shard["pool_scale"], g_ffn=shard["g_ffn"],
              conv_b=shard["conv_b"], g_final=shard["g_final"].reshape(1, D))
    _, grad_x, gr = _local_step(x[0], loss_target[0], w0, late_weights, send)

    halves, small_parts = {}, None
    for names, send_sems, recv_sems, thru, plan, tag in pending:
        done = _push_wait(send_sems, recv_sems, thru, plan, grad_x, f"comm_scatter_wait_{tag}")
        if names == "small":
            small_parts = _sum_small(done[0], done[1], me, "sum_small").reshape(-1)
        else:
            m = len(names)
            for t, k in enumerate(names):
                halves[k] = _sum_pieces(done[t], BIG_AXIS[k], done[m + t], pos, f"sum_{k}")
    g_mix_own = _as_rows([gr["g_mix"]])
    _, g_mix_recv = _push([g_mix_own, lax.empty((N_DEV - 1,) + g_mix_own.shape, F32)], _broadcast_plan, N_DEV - 1, "comm_gather_g_mix")
    g_mix_sum = _sum_small(g_mix_own, g_mix_recv, me, "sum_g_mix").reshape(-1)[:D]
    wholes = _push([halves[k] for k in BIG], _exchange_plan, EXCHANGE_CHUNKS * len(BIG), "comm_exchange_halves")

    grads = {"g_mix": g_mix_sum.reshape(shard["g_mix"].shape)}
    for k, whole in zip(BIG, wholes):
        grads[k] = whole.reshape(shard[k].shape)
    off = 0
    loss = None
    for k in SMALL:
        sz = math.prod(gr[k].shape)
        fullg = small_parts[off:off + sz].reshape(gr[k].shape)
        off += sz
        if k == "loss_cols":
            loss = jnp.sum(fullg)
            continue
        if k in ("w_pool_lin", "conv_w"):
            n = shard[k].shape[2]
            fullg = lax.dynamic_slice_in_dim(fullg, chip * n, n, axis=1)
        grads[k] = fullg.reshape(shard[k].shape)

    deltas, new_m, new_v = {}, {}, {}
    for k in ORDER:
        shp = shard[k].shape
        two_d = (-1, shp[-1])
        dl, nm, nv = _adamw(shard[k].reshape(two_d), grads[k].reshape(two_d), mom[k].reshape(two_d), vel[k].reshape(two_d), f"adamw_{k}")
        deltas[k], new_m[k], new_v[k] = dl.reshape(shp), nm.reshape(shp), nv.reshape(shp)

    return (loss, grad_x[None], *[grads[k] for k in ORDER], *[deltas[k] for k in ORDER], *[new_m[k] for k in ORDER], *[new_v[k] for k in ORDER])
```

```python
import functools
import math

import jax
import jax.numpy as jnp
from jax import lax
from jax.experimental import pallas as pl
from jax.experimental.pallas import tpu as pltpu

F32 = jnp.float32
BF16 = jnp.bfloat16

RMS_EPS = 1e-6
POOL_WINDOWS = (2, 4, 8, 16)
ATTN_GROUPS = ((128, 1), (512, 4), (2048, 16))
HEADS_PER_GROUP = 4
HEAD_DIM = 128
N_ATTN_HEADS = HEADS_PER_GROUP * len(ATTN_GROUPS)
SPAN = 128
GROUP_WIDTH = HEADS_PER_GROUP * HEAD_DIM
ATTN_WIDTH = N_ATTN_HEADS * HEAD_DIM
ATTN_SCALE = HEAD_DIM ** -0.5
NEG_BIG = -1e30
ALIBI_SLOPES = tuple(2.0 ** (-8.0 * (h + 1) / N_ATTN_HEADS) for h in range(N_ATTN_HEADS))

ADAM_LR = 0.001
ADAM_B1 = 0.9
ADAM_B2 = 0.999
ADAM_EPS = 1e-08
ADAM_WD = 0.01
ADAM_STEP = 10

INV_SQRT2 = 1.0 / math.sqrt(2.0)
INV_SQRT_2PI = 1.0 / math.sqrt(2.0 * math.pi)

HALO = 16
VMEM_LIMIT = 56 * 1024 * 1024
N_CHIPS = 4
N_DEV = 8
MESH = pl.DeviceIdType.MESH
ANY = pl.BlockSpec(memory_space=pl.ANY)


def _cparams(*sem):
    return pltpu.CompilerParams(dimension_semantics=sem, vmem_limit_bytes=VMEM_LIMIT)


def _tile(n, pref, mult=128):
    t = (min(pref, n) // mult) * mult
    while t >= mult:
        if n % t == 0:
            return t
        t -= mult
    return n


def _dot(a, b, contract):
    return lax.dot_general(a, b, (contract, ((), ())), preferred_element_type=F32)


def _dot_nn(a, b):
    return _dot(a, b, ((1,), (0,)))


def _dot_nt(a, b):
    return _dot(a, b, ((1,), (1,)))


def _mm(a, b, *, mode, dims, name, tiles=None, out_dtypes=(BF16,), epilogue=None, extras=(), a_off=(0, 0), b_off=(0, 0), after=None):
    M, N, K = dims
    if tiles is None:
        tiles = (_tile(M, 1408), _tile(N, 2816), _tile(K, 512)) if mode == "tn" else (_tile(M, 1024), _tile(N, 1536), _tile(K, 2816))
    tm, tn, tk = tiles
    assert M % tm == 0 and N % tn == 0 and K % tk == 0, (name, dims, tiles)
    nk = K // tk
    if mode == "nn":
        ab, bb, contract = (tm, tk), (tk, tn), ((1,), (0,))
        amap = lambda i, j, k: (i + a_off[0] // tm, k + a_off[1] // tk)
        bmap = lambda i, j, k: (k + b_off[0] // tk, j + b_off[1] // tn)
    elif mode == "nt":
        ab, bb, contract = (tm, tk), (tn, tk), ((1,), (1,))
        amap = lambda i, j, k: (i + a_off[0] // tm, k + a_off[1] // tk)
        bmap = lambda i, j, k: (j + b_off[0] // tn, k + b_off[1] // tk)
    else:
        ab, bb, contract = (tk, tm), (tk, tn), ((0,), (0,))
        amap = lambda i, j, k: (k + a_off[0] // tk, i + a_off[1] // tm)
        bmap = lambda i, j, k: (k + b_off[0] // tk, j + b_off[1] // tn)
    assert a_off[0] % ab[0] == 0 and a_off[1] % ab[1] == 0 and b_off[0] % bb[0] == 0 and b_off[1] % bb[1] == 0, name
    in_specs = [pl.BlockSpec(ab, amap), pl.BlockSpec(bb, bmap)]
    ex_arrays = []
    for arr, kind, off in extras:
        if kind == "mn":
            assert off[0] % tm == 0 and off[1] % tn == 0, name
            in_specs.append(pl.BlockSpec((tm, tn), lambda i, j, k, off=off: (i + off[0] // tm, j + off[1] // tn)))
        else:
            assert off[1] % tn == 0, name
            in_specs.append(pl.BlockSpec((1, tn), lambda i, j, k, off=off: (0, j + off[1] // tn)))
        ex_arrays.append(arr)
    ne, no = len(ex_arrays), len(out_dtypes)
    if after is not None:
        in_specs.append(ANY)
        ex_arrays.append(after)
    first_out = 2 + len(ex_arrays)
    if epilogue is None:
        def epilogue(acc, ex, outs):
            outs[0][...] = acc.astype(outs[0].dtype)

    def body(*refs):
        a_ref, b_ref = refs[0], refs[1]
        ex, outs = refs[2:2 + ne], refs[first_out:first_out + no]
        if nk == 1:
            epilogue(_dot(a_ref[...], b_ref[...], contract), ex, outs)
            return
        acc = refs[-1]
        k = pl.program_id(2)
        if nk <= 4:
            part = _dot(a_ref[...], b_ref[...], contract)

            @pl.when(k == 0)
            def _():
                acc[...] = part

            @pl.when(jnp.logical_and(k > 0, k < nk - 1))
            def _():
                acc[...] += part

            @pl.when(k == nk - 1)
            def _():
                epilogue(acc[...] + part, ex, outs)
        else:
            @pl.when(k == 0)
            def _():
                acc[...] = _dot(a_ref[...], b_ref[...], contract)

            @pl.when(k > 0)
            def _():
                acc[...] += _dot(a_ref[...], b_ref[...], contract)

            @pl.when(k == nk - 1)
            def _():
                epilogue(acc[...], ex, outs)

    res = pl.pallas_call(
        body,
        grid=(M // tm, N // tn, nk),
        in_specs=in_specs,
        out_specs=[pl.BlockSpec((tm, tn), lambda i, j, k: (i, j)) for _ in out_dtypes],
        out_shape=[jax.ShapeDtypeStruct((M, N), dt) for dt in out_dtypes],
        scratch_shapes=[pltpu.VMEM((tm, tn), F32)] if nk > 1 else [],
        compiler_params=_cparams("parallel", "parallel", "arbitrary"),
        name=name,
    )(a, b, *ex_arrays)
    return res[0] if no == 1 else res


def _rms_fwd(x, g, name):
    S, D = x.shape
    tm = _tile(S, 512)

    def body(x_ref, g_ref, h_ref):
        xv = x_ref[...]
        r = lax.rsqrt(jnp.mean(xv * xv, axis=-1, keepdims=True) + RMS_EPS)
        h_ref[...] = (xv * r * g_ref[...]).astype(h_ref.dtype)

    return pl.pallas_call(
        body,
        grid=(S // tm,),
        in_specs=[pl.BlockSpec((tm, D), lambda i: (i, 0)), pl.BlockSpec((1, D), lambda i: (0, 0))],
        out_specs=pl.BlockSpec((tm, D), lambda i: (i, 0)),
        out_shape=jax.ShapeDtypeStruct((S, D), BF16),
        compiler_params=_cparams("parallel"),
        name=name,
    )(x, g)


def _rms_bwd(dh, x, g, dres, name, out_dtype):
    S, D = x.shape
    tm = _tile(S, 512)

    def body(dh_ref, x_ref, g_ref, dres_ref, dx_ref, dg_ref):
        xv = x_ref[...]
        r = lax.rsqrt(jnp.mean(xv * xv, axis=-1, keepdims=True) + RMS_EPS)
        xr = xv * r
        dhv = dh_ref[...].astype(F32)

        @pl.when(pl.program_id(0) == 0)
        def _():
            dg_ref[...] = jnp.zeros_like(dg_ref)

        dg_ref[...] += jnp.sum(dhv * xr, axis=0, keepdims=True)
        u = dhv * g_ref[...]
        c = jnp.mean(u * xr, axis=-1, keepdims=True)
        dx_ref[...] = (dres_ref[...].astype(F32) + r * (u - xr * c)).astype(dx_ref.dtype)

    row = pl.BlockSpec((tm, D), lambda i: (i, 0))
    vec = pl.BlockSpec((1, D), lambda i: (0, 0))
    return pl.pallas_call(
        body,
        grid=(S // tm,),
        in_specs=[row, row, vec, row],
        out_specs=[row, vec],
        out_shape=[jax.ShapeDtypeStruct((S, D), out_dtype), jax.ShapeDtypeStruct((1, D), F32)],
        compiler_params=_cparams("arbitrary"),
        name=name,
    )(dh, x, g, dres)


def _loss_head(x3, tgt, g, name):
    S, D = x3.shape
    tm = _tile(S, 512)

    def body(x_ref, t_ref, g_ref, dxb_ref, dg_ref, loss_ref):
        xv = x_ref[...]
        gv = g_ref[...]
        r = lax.rsqrt(jnp.mean(xv * xv, axis=-1, keepdims=True) + RMS_EPS)
        xr = xv * r
        e = xr * gv - t_ref[...]

        @pl.when(pl.program_id(0) == 0)
        def _():
            dg_ref[...] = jnp.zeros_like(dg_ref)
            loss_ref[...] = jnp.zeros_like(loss_ref)

        loss_ref[...] += jnp.sum(e * e, axis=0, keepdims=True) * (0.5 / D)
        dy = e * (1.0 / D)
        dg_ref[...] += jnp.sum(dy * xr, axis=0, keepdims=True)
        u = dy * gv
        c = jnp.mean(u * xr, axis=-1, keepdims=True)
        dxb_ref[...] = (r * (u - xr * c)).astype(BF16)

    row = pl.BlockSpec((tm, D), lambda i: (i, 0))
    vec = pl.BlockSpec((1, D), lambda i: (0, 0))
    return pl.pallas_call(
        body,
        grid=(S // tm,),
        in_specs=[row, row, vec],
        out_specs=[row, vec, vec],
        out_shape=[jax.ShapeDtypeStruct((S, D), BF16), jax.ShapeDtypeStruct((1, D), F32), jax.ShapeDtypeStruct((1, D), F32)],
        compiler_params=_cparams("arbitrary"),
        name=name,
    )(x3, tgt, g)


SHIFT_ROWS = 256


def _shift_matrix():
    i = lax.broadcasted_iota(jnp.int32, (2 * SHIFT_ROWS, SHIFT_ROWS), 0)
    j = lax.broadcasted_iota(jnp.int32, (2 * SHIFT_ROWS, SHIFT_ROWS), 1)
    src = jnp.where(i < SHIFT_ROWS, i - 1, i - SHIFT_ROWS - 2)
    return (j == src).astype(BF16)


def _conv_taps(cur_ref, halo_ref, w_ref, b_ref, first, shift):
    w, bias = w_ref[...], b_ref[...]
    before = jnp.where(first, 0.0, halo_ref[...].astype(F32)[HALO - 8:])
    sub = lax.broadcasted_iota(jnp.int32, before.shape, 0)
    out = []
    for blk in range(cur_ref.shape[0] // SHIFT_ROWS):
        xb = cur_ref[blk * SHIFT_ROWS:(blk + 1) * SHIFT_ROWS, :]
        xf = xb.astype(F32)
        both = _dot_nn(shift, xb)
        p1, p2 = both[:SHIFT_ROWS], both[SHIFT_ROWS:]
        p1 = jnp.concatenate([jnp.where(sub == 0, pltpu.roll(before, 1, 0), p1[:8]), p1[8:]], axis=0)
        p2 = jnp.concatenate([jnp.where(sub < 2, pltpu.roll(before, 2, 0), p2[:8]), p2[8:]], axis=0)
        out.append(bias + w[0:1] * p2 + w[1:2] * p1 + w[2:3] * xf)
        before = xf[SHIFT_ROWS - 8:]
    return jnp.concatenate(out, axis=0)


def _convglu_fwd(up, cw, cb, name):
    S, F2 = up.shape
    F = F2 // 2
    tm, tn = _tile(S, 512), _tile(F, 1408)
    nj, hb = F // tn, tm // HALO

    def body(ua, ub, ha, hb_, wa, wb, ba, bb, f_ref, a_ref, b_ref):
        first = pl.program_id(0) == 0
        shift = _shift_matrix()
        a = _conv_taps(ua, ha, wa, ba, first, shift)
        b = _conv_taps(ub, hb_, wb, bb, first, shift)
        f_ref[...] = (0.5 * a * (1.0 + lax.erf(a * INV_SQRT2)) * b).astype(f_ref.dtype)
        a_ref[...] = a.astype(a_ref.dtype)
        b_ref[...] = b.astype(b_ref.dtype)

    tile = pl.BlockSpec((tm, tn), lambda i, j: (i, j))
    return pl.pallas_call(
        body,
        grid=(S // tm, nj),
        in_specs=[
            tile,
            pl.BlockSpec((tm, tn), lambda i, j: (i, j + nj)),
            pl.BlockSpec((HALO, tn), lambda i, j: (jnp.maximum(i * hb - 1, 0), j)),
            pl.BlockSpec((HALO, tn), lambda i, j: (jnp.maximum(i * hb - 1, 0), j + nj)),
            pl.BlockSpec((3, tn), lambda i, j: (0, j)),
            pl.BlockSpec((3, tn), lambda i, j: (0, j + nj)),
            pl.BlockSpec((1, tn), lambda i, j: (0, j)),
            pl.BlockSpec((1, tn), lambda i, j: (0, j + nj)),
        ],
        out_specs=[tile, tile, tile],
        out_shape=[jax.ShapeDtypeStruct((S, F), BF16)] * 3,
        compiler_params=_cparams("parallel", "parallel"),
        name=name,
    )(up, up, up, up, cw, cw, cb, cb)


def _convglu_bwd(df, a, b, up, cw, name):
    S, F = df.shape
    tm, tn = _tile(S, 512), _tile(F, 1408)
    nj, ni, hb = F // tn, S // tm, tm // HALO
    n = tm + HALO

    def body(df_ref, dfn_ref, a_ref, an_ref, b_ref, bn_ref, up_ref, w_ref, o_ref, db_ref, dw_ref):
        j, i = pl.program_id(0), pl.program_id(1)
        last = i == ni - 1

        def rows(c_ref, n_ref):
            return jnp.concatenate([c_ref[...].astype(F32), jnp.where(last, 0.0, n_ref[...].astype(F32))], axis=0)

        @pl.when(i == 0)
        def _():
            db_ref[...] = jnp.zeros_like(db_ref)
            dw_ref[...] = jnp.zeros_like(dw_ref)

        def finish(d):
            i_ = lax.broadcasted_iota(jnp.int32, (2 * SHIFT_ROWS, SHIFT_ROWS), 0)
            j_ = lax.broadcasted_iota(jnp.int32, (2 * SHIFT_ROWS, SHIFT_ROWS), 1)
            ahead = (j_ == jnp.where(i_ < SHIFT_ROWS, i_ + 1, i_ - SHIFT_ROWS + 2)).astype(BF16)
            db = d.astype(BF16)
            sub = lax.broadcasted_iota(jnp.int32, (8, tn), 0)
            d1, d2 = [], []
            for blk in range(tm // SHIFT_ROWS):
                lo, hi = blk * SHIFT_ROWS, (blk + 1) * SHIFT_ROWS
                both = _dot_nn(ahead, db[lo:hi])
                n1, n2 = both[:SHIFT_ROWS], both[SHIFT_ROWS:]
                after = db[hi:hi + HALO].astype(F32)[:8]
                d1 += [n1[:-8], jnp.where(sub == 7, pltpu.roll(after, 7, 0), n1[-8:])]
                d2 += [n2[:-8], jnp.where(sub >= 6, pltpu.roll(after, 6, 0), n2[-8:])]
            d0, d1, d2 = d[:tm], jnp.concatenate(d1, axis=0), jnp.concatenate(d2, axis=0)
            w = w_ref[...]
            o_ref[...] = (w[2:3] * d0 + w[1:2] * d1 + w[0:1] * d2).astype(o_ref.dtype)
            upv = up_ref[...].astype(F32)
            db_ref[...] += jnp.sum(d0, axis=0, keepdims=True)
            dw_ref[0:1, :] += jnp.sum(d2 * upv, axis=0, keepdims=True)
            dw_ref[1:2, :] += jnp.sum(d1 * upv, axis=0, keepdims=True)
            dw_ref[2:3, :] += jnp.sum(d0 * upv, axis=0, keepdims=True)

        av, dfv = rows(a_ref, an_ref), rows(df_ref, dfn_ref)
        cdf = 0.5 * (1.0 + lax.erf(av * INV_SQRT2))

        @pl.when(j < nj)
        def _():
            pdf = jnp.exp(-0.5 * av * av) * INV_SQRT_2PI
            finish(dfv * rows(b_ref, bn_ref) * (cdf + av * pdf))

        @pl.when(j >= nj)
        def _():
            finish(dfv * (av * cdf))

    jh = lambda j: lax.rem(j, nj)
    nxt = lambda i: jnp.minimum((i + 1) * hb, S // HALO - 1)
    cur = pl.BlockSpec((tm, tn), lambda j, i: (i, jh(j)))
    halo = pl.BlockSpec((HALO, tn), lambda j, i: (nxt(i), jh(j)))
    return pl.pallas_call(
        body,
        grid=(2 * nj, ni),
        in_specs=[cur, halo, cur, halo, cur, halo, pl.BlockSpec((tm, tn), lambda j, i: (i, j)), pl.BlockSpec((3, tn), lambda j, i: (0, j))],
        out_specs=[pl.BlockSpec((tm, tn), lambda j, i: (i, j)), pl.BlockSpec((1, tn), lambda j, i: (0, j)), pl.BlockSpec((3, tn), lambda j, i: (0, j))],
        out_shape=[jax.ShapeDtypeStruct((S, 2 * F), BF16), jax.ShapeDtypeStruct((1, 2 * F), F32), jax.ShapeDtypeStruct((3, 2 * F), F32)],
        compiler_params=_cparams("parallel", "arbitrary"),
        name=name,
    )(df, df, a, a, b, b, up, cw)


def _gate_bwd(dmixed, gates, y_pool, y_attn, in_width, name):
    S, D = dmixed.shape
    tm, tn = _tile(S, 2048), _tile(D, 512)
    nj = D // tn
    pre0 = (in_width - 2 * D) // tn
    assert pre0 * tn == in_width - 2 * D

    def body(dm_ref, g_ref, yp_ref, ya_ref, dy_ref, dpre_ref, db_ref):
        j = pl.program_id(0)

        @pl.when(pl.program_id(1) == 0)
        def _():
            db_ref[...] = jnp.zeros_like(db_ref)

        def run(y_ref):
            dm = dm_ref[...].astype(F32)
            gv = g_ref[...].astype(F32)
            dy_ref[...] = (dm * gv).astype(BF16)
            dpre = dm * y_ref[...].astype(F32) * gv * (1.0 - gv)
            dpre_ref[...] = dpre.astype(BF16)
            db_ref[...] += jnp.sum(dpre, axis=0, keepdims=True)

        @pl.when(j < nj)
        def _():
            run(yp_ref)

        @pl.when(j >= nj)
        def _():
            run(ya_ref)

    tile2 = pl.BlockSpec((tm, tn), lambda j, i: (i, j))
    return pl.pallas_call(
        body,
        grid=(2 * nj, S // tm),
        in_specs=[
            pl.BlockSpec((tm, tn), lambda j, i: (i, lax.rem(j, nj))),
            tile2,
            pl.BlockSpec((tm, tn), lambda j, i: (i, jnp.minimum(j, nj - 1))),
            pl.BlockSpec((tm, tn), lambda j, i: (i, jnp.maximum(j - nj, 0))),
        ],
        out_specs=[tile2, pl.BlockSpec((tm, tn), lambda j, i: (i, pre0 + j)), pl.BlockSpec((1, tn), lambda j, i: (0, j))],
        out_shape=[jax.ShapeDtypeStruct((S, 2 * D), BF16), jax.ShapeDtypeStruct((S, in_width), BF16), jax.ShapeDtypeStruct((1, 2 * D), F32)],
        compiler_params=_cparams("parallel", "arbitrary"),
        name=name,
    )(dmixed, gates, y_pool, y_attn)


def _pool_counts(i, tm, rows, w):
    t = i * tm + lax.broadcasted_iota(jnp.int32, (rows, 1), 0)
    return jnp.minimum(t + 1, w).astype(F32)


def _pooled_groups(u_ref, uh_ref, i, tm, C):
    cur = u_ref[...]
    halo = jnp.where(i == 0, 0.0, uh_ref[...])
    xx = jnp.concatenate([halo, cur], axis=0)
    out = []
    s = xx
    for gi, w in enumerate(POOL_WINDOWS):
        s = s + pltpu.roll(s, w // 2, 0)
        tot = s[HALO:, 0:C]
        out.append(tot / _pool_counts(i, tm, tm, w) - cur[:, gi * C:(gi + 1) * C])
        s = s[:, C:] if gi + 1 < len(POOL_WINDOWS) else s
    return out


def _pool_fwd(u, wl, scale, name):
    S, PW = u.shape
    C = PW // len(POOL_WINDOWS)
    tm = _tile(S, 512)
    hb = tm // HALO

    def body(u_ref, uh_ref, wl_ref, sc_ref, o_ref):
        i = pl.program_id(0)
        pooled = _pooled_groups(u_ref, uh_ref, i, tm, C)
        for gi in range(len(POOL_WINDOWS)):
            y = _dot_nn(pooled[gi].astype(BF16), wl_ref[gi])
            o_ref[:, gi * C:(gi + 1) * C] = (y * sc_ref[:, gi * C:(gi + 1) * C]).astype(o_ref.dtype)

    return pl.pallas_call(
        body,
        grid=(S // tm,),
        in_specs=[
            pl.BlockSpec((tm, PW), lambda i: (i, 0)),
            pl.BlockSpec((HALO, PW), lambda i: (jnp.maximum(i * hb - 1, 0), 0)),
            pl.BlockSpec((len(POOL_WINDOWS), C, C), lambda i: (0, 0, 0)),
            pl.BlockSpec((1, PW), lambda i: (0, 0)),
        ],
        out_specs=pl.BlockSpec((tm, PW), lambda i: (i, 0)),
        out_shape=jax.ShapeDtypeStruct((S, PW), BF16),
        compiler_params=_cparams("parallel"),
        name=name,
    )(u, u, wl, scale)


def _pool_bwd(u, dp, wl, scale, dproj, name):
    S, PW = u.shape
    G = len(POOL_WINDOWS)
    C = PW // G
    tm = _tile(S, 512)
    hb, ni = tm // HALO, S // tm
    n = tm + HALO

    def body(u_ref, uh_ref, dp_ref, dpn_ref, wl_ref, sc_ref, _, du_ref, dwl_ref, dsc_ref):
        i = pl.program_id(0)

        @pl.when(i == 0)
        def _():
            dwl_ref[...] = jnp.zeros_like(dwl_ref)
            dsc_ref[...] = jnp.zeros_like(dsc_ref)

        pooled = _pooled_groups(u_ref, uh_ref, i, tm, C)
        dpc = dp_ref[...].astype(F32)
        dpn = jnp.where(i == ni - 1, 0.0, dpn_ref[...].astype(F32))
        sc = sc_ref[...]
        dyl = jnp.concatenate([dpc, dpn], axis=0) * sc
        for gi, w in enumerate(POOL_WINDOWS):
            cols = slice(gi * C, (gi + 1) * C)
            pb = pooled[gi].astype(BF16)
            ylin = _dot_nn(pb, wl_ref[gi])
            dsc_ref[:, cols] += jnp.sum(dpc[:, cols] * ylin, axis=0, keepdims=True)
            dylg = dyl[:, cols].astype(BF16)
            dwl_ref[gi] += _dot(pb, dylg[:tm], ((0,), (0,)))
            dpool = _dot_nt(dylg, wl_ref[gi])
            e = dpool / _pool_counts(i, tm, n, w)
            k = 1
            while k < w:
                e = e + pltpu.roll(e, n - k, 0)
                k *= 2
            du_ref[:, cols] = (e[:tm] - dpool[:tm]).astype(du_ref.dtype)

    return pl.pallas_call(
        body,
        grid=(ni,),
        in_specs=[
            pl.BlockSpec((tm, PW), lambda i: (i, 0)),
            pl.BlockSpec((HALO, PW), lambda i: (jnp.maximum(i * hb - 1, 0), 0)),
            pl.BlockSpec((tm, PW), lambda i: (i, 0)),
            pl.BlockSpec((HALO, PW), lambda i: (jnp.minimum((i + 1) * hb, S // HALO - 1), 0)),
            pl.BlockSpec((G, C, C), lambda i: (0, 0, 0)),
            pl.BlockSpec((1, PW), lambda i: (0, 0)),
            ANY,
        ],
        out_specs=[pl.BlockSpec((tm, PW), lambda i: (i, 0)), pl.BlockSpec((G, C, C), lambda i: (0, 0, 0)), pl.BlockSpec((1, PW), lambda i: (0, 0))],
        out_shape=[jax.ShapeDtypeStruct(dproj.shape, dproj.dtype), jax.ShapeDtypeStruct((G, C, C), F32), jax.ShapeDtypeStruct((1, PW), F32)],
        input_output_aliases={6: 0},
        compiler_params=_cparams("arbitrary"),
        name=name,
    )(u, u, dp, dp, wl, scale, dproj)


def _band_masks():
    ii = lax.broadcasted_iota(jnp.int32, (SPAN, SPAN), 0)
    kk = lax.broadcasted_iota(jnp.int32, (SPAN, SPAN), 1)
    return ((ii + SPAN - kk).astype(F32), kk >= ii), ((ii - kk).astype(F32), kk <= ii)


ATTN_TILE = 32 * SPAN


def _unit_rows(r, b, d, blocks=1):
    return pl.ds(d * SPAN * b + r, blocks * SPAN, stride=d) if d > 1 else pl.ds(SPAN * b, blocks * SPAN)


def _f32_copies(refs, scratch, d):
    if d == 1:
        return list(refs)
    for ref, s in zip(refs, scratch):
        s[...] = ref[...].astype(F32)
    return list(scratch)


def _attn_fwd(qkv, d, g, name):
    S = qkv.shape[0]
    T = min(ATTN_TILE, S)
    P = SPAN * d
    nbk = T // P

    def body(q_ref, k_ref, v_ref, kp_ref, vp_ref, o_ref, lse_ref, *scratch):
        c = pl.program_id(0)
        (jp, mp), (jc, mc) = _band_masks()
        slopes = [ALIBI_SLOPES[g * HEADS_PER_GROUP + h] * d for h in range(HEADS_PER_GROUP)]
        slope = slopes[0]
        for h in range(1, HEADS_PER_GROUP):
            slope = jnp.where(pl.program_id(1) == h, slopes[h], slope)
        q_s, k_s, v_s, kp_s, vp_s = _f32_copies((q_ref, k_ref, v_ref, kp_ref, vp_ref), scratch[:5], d)
        o_s, l_s = (o_ref, lse_ref) if d == 1 else scratch[5:7]
        bias_p, bias_c = jnp.where(mp, -slope * jp, NEG_BIG), jnp.where(mc, -slope * jc, NEG_BIG)
        bias = jnp.concatenate([bias_p, bias_c], axis=1)
        bias_first = jnp.concatenate([jnp.where(c > 0, bias_p, NEG_BIG), bias_c], axis=1)
        for r in range(d):
            for b in range(nbk):
                rows = _unit_rows(r, b, d)
                q = q_s[rows, :].astype(BF16)
                if b == 0:
                    prev = _unit_rows(r, 0, d)
                    kk = jnp.concatenate([kp_s[prev, :], k_s[rows, :]], axis=0).astype(BF16)
                    vv = jnp.concatenate([vp_s[prev, :], v_s[rows, :]], axis=0).astype(BF16)
                else:
                    both = _unit_rows(r, b - 1, d, 2)
                    kk, vv = k_s[both, :].astype(BF16), v_s[both, :].astype(BF16)
                s = _dot_nt(q, kk) * ATTN_SCALE + (bias_first if b == 0 else bias)
                m = jnp.max(s, axis=-1, keepdims=True)
                p = jnp.exp(s - m)
                l = jnp.sum(p, axis=-1, keepdims=True)
                o_s[rows, :] = _dot_nn(p.astype(BF16), vv) / l
                l_s[rows, :] = jnp.broadcast_to(m + jnp.log(l), (SPAN, HEAD_DIM))
        if d > 1:
            o_ref[...] = o_s[...]
            lse_ref[...] = l_s[...]

    col = lambda kind: (lambda c, h: (c, kind * N_ATTN_HEADS + g * HEADS_PER_GROUP + h))
    pcol = lambda kind: (lambda c, h: (jnp.maximum(c * nbk - 1, 0), kind * N_ATTN_HEADS + g * HEADS_PER_GROUP + h))
    cur = lambda kind: pl.BlockSpec((T, HEAD_DIM), col(kind))
    prv = lambda kind: pl.BlockSpec((P, HEAD_DIM), pcol(kind))
    out = pl.BlockSpec((T, HEAD_DIM), lambda c, h: (c, h))
    scratch = [] if d == 1 else [pltpu.VMEM((T, HEAD_DIM), F32)] * 3 + [pltpu.VMEM((P, HEAD_DIM), F32)] * 2 + [pltpu.VMEM((T, HEAD_DIM), F32)] * 2
    return pl.pallas_call(
        body,
        grid=(S // T, HEADS_PER_GROUP),
        in_specs=[cur(0), cur(1), cur(2), prv(1), prv(2)],
        out_specs=[out, out],
        out_shape=[jax.ShapeDtypeStruct((S, GROUP_WIDTH), F32)] * 2,
        scratch_shapes=scratch,
        compiler_params=_cparams("parallel", "parallel"),
        name=name,
    )(qkv, qkv, qkv, qkv, qkv)


def _attn_merge(os_, lses, name):
    S, W = os_[0].shape
    tm = _tile(S, 1024)

    def body(o0, o1, o2, l0, l1, l2, y_ref, lse_ref):
        ls = [l0[...], l1[...], l2[...]]
        m = jnp.maximum(jnp.maximum(ls[0], ls[1]), ls[2])
        es = [jnp.exp(v - m) for v in ls]
        tot = es[0] + es[1] + es[2]
        y = (es[0] * o0[...] + es[1] * o1[...] + es[2] * o2[...]) / tot
        y_ref[...] = y.astype(y_ref.dtype)
        lse_ref[...] = m + jnp.log(tot)

    row = pl.BlockSpec((tm, W), lambda i: (i, 0))
    return pl.pallas_call(
        body,
        grid=(S // tm,),
        in_specs=[row] * 6,
        out_specs=[row, row],
        out_shape=[jax.ShapeDtypeStruct((S, W), BF16), jax.ShapeDtypeStruct((S, W), F32)],
        compiler_params=_cparams("parallel"),
        name=name,
    )(*os_, *lses)


def _attn_bwd(qkv, dattn, y, lse, dproj, d, g, col0, name):
    S = qkv.shape[0]
    T = min(ATTN_TILE, S)
    P = SPAN * d
    nbk = T // P
    ntile = S // T

    def body(q_ref, k_ref, v_ref, kp_ref, vp_ref, qn_ref, da_ref, dan_ref, y_ref, yn_ref, lse_ref, lsen_ref, _, out_ref, dq_s, dk_s, dv_s, *scratch):
        c = pl.program_id(0)
        head_id = pl.program_id(1)
        kind = pl.program_id(2)

        @pl.when(kind == 0)
        def _():
            (jp, mp), (jc, mc) = _band_masks()
            slopes = [ALIBI_SLOPES[g * HEADS_PER_GROUP + h] * d for h in range(HEADS_PER_GROUP)]
            slope = slopes[0]
            for h in range(1, HEADS_PER_GROUP):
                slope = jnp.where(head_id == h, slopes[h], slope)
            q_s, k_s, v_s, da_s, y_s, kp_s, vp_s, qn_s, dan_s, yn_s = _f32_copies(
                (q_ref, k_ref, v_ref, da_ref, y_ref, kp_ref, vp_ref, qn_ref, dan_ref, yn_ref), scratch, d)
            bias_p, bias_c = jnp.where(mp, -slope * jp, NEG_BIG), jnp.where(mc, -slope * jc, NEG_BIG)
            bias = jnp.concatenate([bias_c, bias_p], axis=0)
            bias_last = jnp.concatenate([bias_c, jnp.where(c < ntile - 1, bias_p, NEG_BIG)], axis=0)
            bias_first = jnp.where(c > 0, bias_p, NEG_BIG)

            def pair(q, da, yy, lse_blk, kk, vv, b):
                dd = jnp.sum(da.astype(F32) * yy.astype(F32), axis=-1, keepdims=True)
                p = jnp.exp(_dot_nt(q, kk) * ATTN_SCALE + b - lse_blk[:, 0:1])
                return p, p * (_dot_nt(da, vv) - dd)

            for r in range(d):
                first = _unit_rows(r, 0, d)
                kk, vv = kp_s[first, :].astype(BF16), vp_s[first, :].astype(BF16)
                _, ds = pair(q_s[first, :].astype(BF16), da_s[first, :].astype(BF16), y_s[first, :], lse_ref[first, :], kk, vv, bias_first)
                dq_next = _dot_nn(ds.astype(BF16), kk)
                for kb in range(nbk):
                    rows = _unit_rows(r, kb, d)
                    if kb + 1 < nbk:
                        both = _unit_rows(r, kb, d, 2)
                        q, da, yy, lse_blk = q_s[both, :], da_s[both, :], y_s[both, :], lse_ref[both, :]
                    else:
                        q = jnp.concatenate([q_s[rows, :], qn_s[first, :]], axis=0)
                        da = jnp.concatenate([da_s[rows, :], dan_s[first, :]], axis=0)
                        yy = jnp.concatenate([y_s[rows, :], yn_s[first, :]], axis=0)
                        lse_blk = jnp.concatenate([lse_ref[rows, :], lsen_ref[first, :]], axis=0)
                    q, da = q.astype(BF16), da.astype(BF16)
                    kk, vv = k_s[rows, :].astype(BF16), v_s[rows, :].astype(BF16)
                    p, ds = pair(q, da, yy, lse_blk, kk, vv, bias if kb + 1 < nbk else bias_last)
                    dv_s[rows, :] = _dot_nn(p.T.astype(BF16), da)
                    dk_s[rows, :] = _dot_nn(ds.T.astype(BF16), q) * ATTN_SCALE
                    dq_both = _dot_nn(ds.astype(BF16), kk)
                    dq_s[rows, :] = (dq_next + dq_both[:SPAN]) * ATTN_SCALE
                    dq_next = dq_both[SPAN:]
            out_ref[...] = dq_s[...].astype(out_ref.dtype)

        @pl.when(kind == 1)
        def _():
            out_ref[...] = dk_s[...].astype(out_ref.dtype)

        @pl.when(kind == 2)
        def _():
            out_ref[...] = dv_s[...].astype(out_ref.dtype)

    head = lambda h: g * HEADS_PER_GROUP + h
    cur = lambda kind: pl.BlockSpec((T, HEAD_DIM), lambda c, h, kd: (c, kind * N_ATTN_HEADS + head(h)))
    prv = lambda kind: pl.BlockSpec((P, HEAD_DIM), lambda c, h, kd: (jnp.maximum(c * nbk - 1, 0), kind * N_ATTN_HEADS + head(h)))
    nxt_row = lambda c: jnp.minimum((c + 1) * nbk, S // P - 1)
    qnext = pl.BlockSpec((P, HEAD_DIM), lambda c, h, kd: (nxt_row(c), head(h)))
    hcur = pl.BlockSpec((T, HEAD_DIM), lambda c, h, kd: (c, h))
    hnext = pl.BlockSpec((P, HEAD_DIM), lambda c, h, kd: (nxt_row(c), h))
    out = pl.BlockSpec((T, HEAD_DIM), lambda c, h, kd: (c, col0 + kd * N_ATTN_HEADS + head(h)))
    stage = [pltpu.VMEM((T, HEAD_DIM), F32)] * 3
    copies = [] if d == 1 else [pltpu.VMEM((T, HEAD_DIM), F32)] * 5 + [pltpu.VMEM((P, HEAD_DIM), F32)] * 5
    return pl.pallas_call(
        body,
        grid=(ntile, HEADS_PER_GROUP, 3),
        in_specs=[cur(0), cur(1), cur(2), prv(1), prv(2), qnext, hcur, hnext, hcur, hnext, hcur, hnext, ANY],
        out_specs=out,
        out_shape=jax.ShapeDtypeStruct(dproj.shape, dproj.dtype),
        input_output_aliases={12: 0},
        scratch_shapes=stage + copies,
        compiler_params=_cparams("parallel", "parallel", "arbitrary"),
        name=name,
    )(qkv, qkv, qkv, qkv, qkv, qkv, dattn, dattn, y, y, lse, lse, dproj)


def _row_block(R, C, bytes_per_row_elem=4, budget=1 << 20):
    if R % 8:
        return R
    best = 8
    t = 8
    while t <= R:
        if R % t == 0 and t * C * bytes_per_row_elem <= budget:
            best = t
        t += 8
    return best


def _adamw(w, g, m, v, name):
    R, C = w.shape
    tr = _row_block(R, C, budget=2 << 20)
    c1 = 1.0 - ADAM_B1 ** ADAM_STEP
    c2 = 1.0 - ADAM_B2 ** ADAM_STEP

    def body(w_ref, g_ref, m_ref, v_ref, d_ref, nm_ref, nv_ref):
        gv = g_ref[...]
        nm = ADAM_B1 * m_ref[...] + (1.0 - ADAM_B1) * gv
        nv = ADAM_B2 * v_ref[...] + (1.0 - ADAM_B2) * (gv * gv)
        d_ref[...] = -ADAM_LR * ((nm / c1) / (jnp.sqrt(nv / c2) + ADAM_EPS) + ADAM_WD * w_ref[...])
        nm_ref[...] = nm
        nv_ref[...] = nv

    blk = pl.BlockSpec((tr, C), lambda i: (i, 0))
    return pl.pallas_call(
        body,
        grid=(R // tr,),
        in_specs=[blk] * 4,
        out_specs=[blk] * 3,
        out_shape=[jax.ShapeDtypeStruct((R, C), F32)] * 3,
        compiler_params=_cparams("parallel"),
        name=name,
    )(w, g, m, v)


def _sum_pieces(grad, axis, recv, pos, name):
    n, pr, pc = recv.shape
    tr = _row_block(pr, pc, bytes_per_row_elem=(n + 1) * recv.dtype.itemsize, budget=4 << 20)
    nblk = pr // tr
    if axis == 1:
        own_map = lambda i, p: (p[1] * nblk + i, p[0])
    else:
        own_map = lambda i, p: ((2 * p[0] + p[1]) * nblk + i, 0)

    def body(p_ref, own_ref, r_ref, o_ref):
        acc = own_ref[...].astype(F32)
        for s in range(n):
            acc = acc + r_ref[s].astype(F32)
        o_ref[...] = acc

    return pl.pallas_call(
        body,
        grid_spec=pltpu.PrefetchScalarGridSpec(
            num_scalar_prefetch=1,
            grid=(nblk,),
            in_specs=[pl.BlockSpec((tr, pc), own_map), pl.BlockSpec((n, tr, pc), lambda i, p: (0, i, 0))],
            out_specs=pl.BlockSpec((tr, pc), lambda i, p: (p[1] * nblk + i, 0)),
        ),
        out_shape=jax.ShapeDtypeStruct((2 * pr, pc), F32),
        compiler_params=_cparams("parallel"),
        name=name,
    )(pos, grad, recv)


def _sum_small(own, recv, me, name):
    n, R, C = recv.shape
    tr = _row_block(R, C, bytes_per_row_elem=(n + 1) * 4, budget=4 << 20)

    def body(me_ref, own_ref, r_ref, o_ref):
        acc = None
        for dev in range(n + 1):
            k = jnp.bitwise_xor(me_ref[0], dev)
            term = jnp.where(k == 0, own_ref[...], r_ref[jnp.maximum(k - 1, 0)])
            acc = term if acc is None else acc + term
        o_ref[...] = acc

    return pl.pallas_call(
        body,
        grid_spec=pltpu.PrefetchScalarGridSpec(
            num_scalar_prefetch=1,
            grid=(R // tr,),
            in_specs=[pl.BlockSpec((tr, C), lambda i, m: (i, 0)), pl.BlockSpec((n, tr, C), lambda i, m: (0, i, 0))],
            out_specs=pl.BlockSpec((tr, C), lambda i, m: (i, 0)),
        ),
        out_shape=jax.ShapeDtypeStruct((R, C), F32),
        compiler_params=_cparams("parallel"),
        name=name,
    )(me, own, recv)


def _place(shard, axis, pos, dtype, name, after=None):
    extra = [] if after is None else [after]
    shp = list(shard.shape)
    shp[axis] *= N_CHIPS
    if shard.ndim == 3:
        assert axis == 1
        in_spec = pl.BlockSpec(shard.shape, lambda i, p: (0, 0, 0))
        out_spec = pl.BlockSpec(shard.shape, lambda i, p: (0, p[0], 0))
        grid = (1,)
    else:
        R, C = shard.shape
        tr = _row_block(R, C, bytes_per_row_elem=4, budget=2 << 20)
        nblk = R // tr
        in_spec = pl.BlockSpec((tr, C), lambda i, p: (i, 0))
        out_spec = pl.BlockSpec((tr, C), (lambda i, p: (i, p[0])) if axis == 1 else (lambda i, p: (p[0] * nblk + i, 0)))
        grid = (nblk,)

    def body(*refs):
        s_ref, o_ref = refs[1], refs[-1]
        o_ref[...] = s_ref[...].astype(o_ref.dtype)

    return pl.pallas_call(
        body,
        grid_spec=pltpu.PrefetchScalarGridSpec(num_scalar_prefetch=1, grid=grid, in_specs=[in_spec] + [ANY] * len(extra), out_specs=out_spec),
        out_shape=jax.ShapeDtypeStruct(tuple(shp), dtype),
        compiler_params=_cparams("parallel"),
        name=name,
    )(pos, shard, *extra)


HBM = pl.BlockSpec(memory_space=pltpu.HBM)
SEM = pl.BlockSpec(memory_space=pltpu.SEMAPHORE)
DATAFLOW = pltpu.SideEffectType.DATAFLOW_SIDE_EFFECTING


def _position():
    return lax.axis_index("x"), lax.axis_index("y"), lax.axis_index("c")


def _peer(k):
    x, y, c = _position()
    return ((1 - x) if k & 4 else x, (1 - y) if k & 2 else y, (1 - c) if k & 1 else c)


def _shard_slice(ref, axis, idx, size):
    start = idx * size
    if axis == ref.ndim - 1:
        start = pl.multiple_of(start, 128)
    ix = [slice(None)] * ref.ndim
    ix[axis] = pl.ds(start, size)
    return ref.at[tuple(ix)]


def _gather_plan(axes):
    def plan(refs):
        x, y, c = _position()
        out = []
        for ref, ax in zip(refs, axes):
            mine = _shard_slice(ref, ax, 2 * x + y, ref.shape[ax] // N_CHIPS)
            for k in (4, 2, 6):
                px, py, _ = _peer(k)
                out.append((mine, mine, (px, py, c)))
        return out
    return plan


def _scatter_plan(axes):
    m = len(axes)

    def plan(refs):
        out = []
        for t in range(m):
            grad, recv = refs[t], refs[m + t]
            _, pr, pc = recv.shape
            for k in range(1, N_DEV):
                px, py, pcore = _peer(k)
                if axes[t] == 0:
                    piece = grad.at[pl.ds(((2 * px + py) * 2 + pcore) * pr, pr), :]
                else:
                    piece = grad.at[pl.ds(pcore * pr, pr), pl.ds(pl.multiple_of((2 * px + py) * pc, 128), pc)]
                out.append((piece, recv.at[k - 1], (px, py, pcore)))
        return out
    return plan


def _broadcast_plan(refs):
    small, recv = refs
    return [(small, recv.at[k - 1], _peer(k)) for k in range(1, N_DEV)]


def _start_all(plan, refs, send_sems, recv_sems):
    for q, (src, dst, dev) in enumerate(plan(refs)):
        pltpu.make_async_remote_copy(src_ref=src, dst_ref=dst, send_sem=send_sems.at[q], recv_sem=recv_sems.at[q], device_id=dev, device_id_type=MESH).start()


def _wait_all(plan, refs, send_sems, recv_sems):
    for q, (src, dst, dev) in enumerate(plan(refs)):
        cp = pltpu.make_async_remote_copy(src_ref=src, dst_ref=dst, send_sem=send_sems.at[q], recv_sem=recv_sems.at[q], device_id=dev, device_id_type=MESH)
        cp.wait_send()
        cp.wait_recv()


def _push(bufs, plan, ncopies, name):
    n = len(bufs)

    def body(*refs):
        outs = refs[n:2 * n]
        send_sems, recv_sems = refs[2 * n:]
        _start_all(plan, outs, send_sems, recv_sems)
        _wait_all(plan, outs, send_sems, recv_sems)

    return pl.pallas_call(
        body,
        in_specs=[ANY] * n,
        out_specs=[ANY] * n,
        out_shape=[jax.ShapeDtypeStruct(b.shape, b.dtype) for b in bufs],
        input_output_aliases={t: t for t in range(n)},
        scratch_shapes=[pltpu.SemaphoreType.DMA((ncopies,)), pltpu.SemaphoreType.DMA((ncopies,))],
        name=name,
    )(*bufs)


def _half_slices_plan(onward):
    def plan(refs):
        ref, = refs
        x, y, c = _position()
        R2, C4 = ref.shape[0] // 2, ref.shape[1] // N_CHIPS
        out = []
        for k in (4, 2, 6):
            px, py, _ = _peer(k)
            chip = (2 * px + py) if onward else (2 * x + y)
            half = ref.at[pl.ds(c * R2, R2), pl.ds(pl.multiple_of(chip * C4, 128), C4)]
            out.append((half, half, (x, y, 1 - c) if onward else (px, py, c)))
        return out
    return plan


def _push_start(bufs, plan, ncopies, name, after=None):
    n = len(bufs)
    extra = [] if after is None else [after]

    def body(*refs):
        ins = refs[:n]
        first_out = n + len(extra)
        send_sems, recv_sems, token = refs[first_out], refs[first_out + 1], refs[-1]
        _start_all(plan, ins, send_sems, recv_sems)
        token[...] = jnp.zeros_like(token)

    res = pl.pallas_call(
        body,
        name=name,
        out_shape=(pltpu.SemaphoreType.DMA((ncopies,)), pltpu.SemaphoreType.DMA((ncopies,)), *[pltpu.HBM(b.shape, b.dtype) for b in bufs],
                   jax.ShapeDtypeStruct((8, 128), F32)),
        in_specs=[HBM] * n + [ANY] * len(extra),
        out_specs=(SEM, SEM, *[HBM] * n, pl.BlockSpec(memory_space=pltpu.VMEM)),
        input_output_aliases={t: t + 2 for t in range(n)},
        compiler_params=pltpu.CompilerParams(has_side_effects=DATAFLOW),
    )(*[pltpu.with_memory_space_constraint(b, pltpu.HBM) for b in bufs], *extra)
    return res[0], res[1], list(res[2:2 + n]), res[-1]


def _push_wait(send_sems, recv_sems, bufs, plan, after, name):
    n = len(bufs)
    after = list(after) if isinstance(after, (list, tuple)) else [after]

    def body(*refs):
        ins = refs[:n]
        _wait_all(plan, ins, refs[n], refs[n + 1])

    return pl.pallas_call(
        body,
        name=name,
        out_shape=tuple(pltpu.HBM(b.shape, b.dtype) for b in bufs),
        in_specs=[HBM] * n + [SEM, SEM] + [ANY] * len(after),
        out_specs=tuple([HBM] * n),
        input_output_aliases={t: t for t in range(n)},
        compiler_params=pltpu.CompilerParams(has_side_effects=DATAFLOW),
    )(*bufs, send_sems, recv_sems, *after)


EXCHANGE_CHUNKS = 2


def _exchange_plan(refs):
    x, y, c = _position()
    out = []
    for ref in refs:
        rows = ref.shape[0] // (2 * EXCHANGE_CHUNKS)
        for q in range(EXCHANGE_CHUNKS):
            mine = ref.at[pl.ds((c * EXCHANGE_CHUNKS + q) * rows, rows), :]
            out.append((mine, mine, (x, y, 1 - c)))
    return out


LATE_WEIGHTS = (("w_pool_lin", "w_pool_out", "w_attn_out", "w_out"), ("w_up", "conv_w", "w_down"))


def _local_step(x, tgt, w, late_weights, send):
    S, D = x.shape
    PW = w["pool_scale"].shape[1]
    o_q = PW
    o_g = PW + 3 * ATTN_WIDTH
    QKV = 3 * ATTN_WIDTH

    h1 = _rms_fwd(x, w["g_mix"], "rms1")
    w = dict(w, **late_weights("w_in", h1))
    proj_tiles = (_tile(S, 1024), 512, D)
    started = w.get("late_started")
    u = _mm(h1, w["w_in"], mode="nn", dims=(S, PW, D), tiles=proj_tiles, out_dtypes=(F32,), after=started, name="proj_u")
    qkv = _mm(h1, w["w_in"], mode="nn", dims=(S, QKV, D), tiles=proj_tiles, b_off=(0, o_q), after=started, name="proj_qkv")

    def gate_epilogue(acc, ex, outs):
        outs[0][...] = (1.0 / (1.0 + jnp.exp(-(acc + ex[0][...])))).astype(outs[0].dtype)

    gates = _mm(h1, w["w_in"], mode="nn", dims=(S, 2 * D, D), tiles=proj_tiles, b_off=(0, o_g), epilogue=gate_epilogue,
                extras=[(w["b_gate"], "n", (0, 0))], name="proj_gates")

    os_, lses = [], []
    for gi, (_, d) in enumerate(ATTN_GROUPS):
        o, lse = _attn_fwd(qkv, d, gi, f"attn_fwd{gi}")
        os_.append(o)
        lses.append(lse)
    attn, lse_tot = _attn_merge(os_, lses, "attn_merge")

    w = dict(w, **late_weights(0, attn))
    pool_out = _pool_fwd(u, w["w_pool_lin"], w["pool_scale"], "pool_fwd")
    y_pool = _mm(pool_out, w["w_pool_out"], mode="nn", dims=(S, D, PW), name="y_pool")

    def mix_epilogue(acc, ex, outs):
        outs[0][...] = acc.astype(BF16)
        outs[1][...] = (ex[0][...].astype(F32) * ex[2][...].astype(F32) + ex[1][...].astype(F32) * acc).astype(BF16)

    y_attn, mixed = _mm(attn, w["w_attn_out"], mode="nn", dims=(S, D, GROUP_WIDTH), out_dtypes=(BF16, BF16), epilogue=mix_epilogue,
                        extras=[(gates, "mn", (0, 0)), (gates, "mn", (0, D)), (y_pool, "mn", (0, 0))], name="y_attn_mix")

    def residual_epilogue(acc, ex, outs):
        outs[0][...] = ex[0][...] + acc

    x2 = _mm(mixed, w["w_out"], mode="nn", dims=(S, D, D), out_dtypes=(F32,), epilogue=residual_epilogue, extras=[(x, "mn", (0, 0))], name="out_proj")

    h2 = _rms_fwd(x2, w["g_ffn"], "rms2")
    w = dict(w, **late_weights(1, h2))
    F = w["w_down"].shape[0]
    up = _mm(h2, w["w_up"], mode="nn", dims=(S, 2 * F, D), name="up_proj")
    f, act_a, act_b = _convglu_fwd(up, w["conv_w"], w["conv_b"], "convglu_fwd")
    x3 = _mm(f, w["w_down"], mode="nn", dims=(S, D, F), out_dtypes=(F32,), epilogue=residual_epilogue, extras=[(x2, "mn", (0, 0))], name="down_proj")

    g = {}
    dx3b, g["g_final"], loss_cols = _loss_head(x3, tgt, w["g_final"], "loss_head")

    g["w_down"] = _mm(f, dx3b, mode="tn", dims=(F, D, S), name="dw_down")
    sent = send(("w_down",), g)
    df = _mm(dx3b, w["w_down"], mode="nt", dims=(S, F, D), name="d_f")
    dup, g["conv_b"], g["conv_w"] = _convglu_bwd(df, act_a, act_b, up, w["conv_w"] + sent, "convglu_bwd")
    g["w_up"] = _mm(h2, dup, mode="tn", dims=(D, 2 * F, S), name="dw_up")
    sent = send(("w_up",), g)
    dh2 = _mm(dup, w["w_up"], mode="nt", dims=(S, D, 2 * F), name="d_h2")
    dx2b, g["g_ffn"] = _rms_bwd(dh2, x2, w["g_ffn"] + sent, dx3b, "rms2_bwd", BF16)

    g["w_out"] = _mm(mixed, dx2b, mode="tn", dims=(D, D, S), name="dw_out")
    dmixed = _mm(dx2b, w["w_out"], mode="nt", dims=(S, D, D), name="d_mixed")
    IN = w["w_in"].shape[1]
    dy_both, dproj, g["b_gate"] = _gate_bwd(dmixed, gates, y_pool, y_attn, IN, "gate_bwd")

    g["w_pool_out"] = _mm(pool_out, dy_both, mode="tn", dims=(PW, D, S), name="dw_pool_out")
    g["w_attn_out"] = _mm(attn, dy_both, mode="tn", dims=(GROUP_WIDTH, D, S), b_off=(0, D), name="dw_attn_out")
    sent = send(("w_out", "w_pool_out", "w_attn_out"), g)
    dpool = _mm(dy_both, w["w_pool_out"], mode="nt", dims=(S, PW, D), name="d_pool")
    dattn = _mm(dy_both, w["w_attn_out"], mode="nt", dims=(S, GROUP_WIDTH, D), a_off=(0, D), name="d_attn")

    dproj, g["w_pool_lin"], g["pool_scale"] = _pool_bwd(u, dpool, w["w_pool_lin"], w["pool_scale"] + sent, dproj, "pool_bwd")
    g["loss_cols"] = loss_cols
    sent = send("small", g)

    for gi, (_, d) in enumerate(ATTN_GROUPS):
        dproj = _attn_bwd(qkv, dattn, attn, lse_tot, dproj, d, gi, PW // HEAD_DIM, f"attn_bwd{gi}")

    g["w_in"] = _mm(h1, dproj, mode="tn", dims=(D, IN, S), name="dw_in")
    sent = sent + send(("w_in",), g)
    dh1 = _mm(dproj, w["w_in"], mode="nt", dims=(S, D, IN), tiles=(_tile(S, 1024), _tile(D, 2048), _tile(IN, 2432)), name="d_h1")
    (grad_x, g["g_mix"]) = _rms_bwd(dh1, x, w["g_mix"] + sent, dx2b, "rms1_bwd", F32)
    return loss_cols, grad_x, g


BIG = ("w_in", "w_pool_out", "w_attn_out", "w_out", "w_up", "w_down")
BIG_AXIS = {"w_in": 1, "w_pool_out": 1, "w_attn_out": 1, "w_out": 0, "w_up": 1, "w_down": 0}
GATHER_AXIS = dict(BIG_AXIS, w_pool_lin=1, conv_w=1)
SMALL = ("loss_cols", "b_gate", "w_pool_lin", "pool_scale", "g_ffn", "conv_w", "conv_b", "g_final")
SMALL_COLS = 1024
ORDER = ("g_mix", "w_in", "b_gate", "w_pool_lin", "pool_scale", "w_pool_out", "w_attn_out", "w_out", "g_ffn", "w_up", "conv_w", "conv_b", "w_down", "g_final")


def _as_rows(parts):
    flat = jnp.concatenate([p.astype(F32).reshape(-1) for p in parts])
    rows = -(-flat.shape[0] // (8 * SMALL_COLS)) * 8
    return jnp.pad(flat, (0, rows * SMALL_COLS - flat.shape[0])).reshape(rows, SMALL_COLS)


def kernel(x, g_mix, w_in, b_gate, w_pool_lin, pool_scale, w_pool_out, w_attn_out, w_out, g_ffn, w_up, conv_w, conv_b, w_down, g_final, loss_target, m_g_mix, m_w_in, m_b_gate, m_w_pool_lin, m_pool_scale, m_w_pool_out, m_w_attn_out, m_w_out, m_g_ffn, m_w_up, m_conv_w, m_conv_b, m_w_down, m_g_final, v_g_mix, v_w_in, v_b_gate, v_w_pool_lin, v_pool_scale, v_w_pool_out, v_w_attn_out, v_w_out, v_g_ffn, v_w_up, v_conv_w, v_conv_b, v_w_down, v_g_final):
    shard = dict(g_mix=g_mix, w_in=w_in, b_gate=b_gate, w_pool_lin=w_pool_lin, pool_scale=pool_scale, w_pool_out=w_pool_out, w_attn_out=w_attn_out,
                 w_out=w_out, g_ffn=g_ffn, w_up=w_up, conv_w=conv_w, conv_b=conv_b, w_down=w_down, g_final=g_final)
    mom = dict(g_mix=m_g_mix, w_in=m_w_in, b_gate=m_b_gate, w_pool_lin=m_w_pool_lin, pool_scale=m_pool_scale, w_pool_out=m_w_pool_out, w_attn_out=m_w_attn_out,
               w_out=m_w_out, g_ffn=m_g_ffn, w_up=m_w_up, conv_w=m_conv_w, conv_b=m_conv_b, w_down=m_w_down, g_final=m_g_final)
    vel = dict(g_mix=v_g_mix, w_in=v_w_in, b_gate=v_b_gate, w_pool_lin=v_w_pool_lin, pool_scale=v_pool_scale, w_pool_out=v_w_pool_out, w_attn_out=v_w_attn_out,
               w_out=v_w_out, g_ffn=v_g_ffn, w_up=v_w_up, conv_w=v_conv_w, conv_b=v_conv_b, w_down=v_w_down, g_final=v_g_final)
    chip = 2 * lax.axis_index("x") + lax.axis_index("y")
    pos = jnp.stack([chip, lax.axis_index("c")]).astype(jnp.int32)
    me = (2 * chip + lax.axis_index("c")).astype(jnp.int32).reshape(1)
    D = x.shape[2]

    out_plan, on_plan = _half_slices_plan(False), _half_slices_plan(True)
    in_send, in_recv, in_bufs, in_token = _push_start([_place(shard["w_in"][0], GATHER_AXIS["w_in"], pos, BF16, "place_w_in")], out_plan, 3,
                                                      "comm_gather_w_in_start")
    placed = {k: _place(shard[k][0], GATHER_AXIS[k], pos, F32 if k == "conv_w" else BF16, f"place_{k}", after=in_token)
              for names in LATE_WEIGHTS for k in names}
    late = []

    def late_weights(stage, after):
        if stage == "w_in":
            landed = _push_wait(in_send, in_recv, in_bufs, out_plan, [after] + list(placed.values()), "comm_gather_w_in_wait")
            w_in_full, = _push(list(landed), on_plan, 3, "comm_gather_w_in_pass")
            late_token, prior = 0.0, w_in_full
            for st, names in enumerate(LATE_WEIGHTS):
                plan = _gather_plan([GATHER_AXIS[k] for k in names])
                send_sems, recv_sems, bufs, token = _push_start([placed[k] for k in names], plan, 3 * len(names), f"comm_gather_late{st}_start", after=prior)
                late.append((names, send_sems, recv_sems, bufs, plan))
                late_token, prior = late_token + token[0, 0], token
            return dict(w_in=w_in_full, b_gate=shard["b_gate"] + late_token, late_started=prior)
        names, send_sems, recv_sems, bufs, plan = late[stage]
        return dict(zip(names, _push_wait(send_sems, recv_sems, bufs, plan, after, f"comm_gather_late{stage}_wait")))

    pending = []

    def send(names, g):
        if names == "small":
            bufs = [_as_rows([g[k] for k in SMALL])]
            bufs.append(lax.empty((N_DEV - 1,) + bufs[0].shape, F32))
            plan, tag = _broadcast_plan, "small"
        else:
            bufs = [g[k] for k in names]
            for k in names:
                R, C = g[k].shape
                piece = (R // (2 * N_CHIPS), C) if BIG_AXIS[k] == 0 else (R // 2, C // N_CHIPS)
                bufs.append(lax.empty((N_DEV - 1,) + piece, BF16))
            plan, tag = _scatter_plan([BIG_AXIS[k] for k in names]), names[0]
        ncopies = (N_DEV - 1) * (len(bufs) // 2)
        send_sems, recv_sems, thru, token = _push_start(bufs, plan, ncopies, f"comm_scatter_start_{tag}")
        pending.append((names, send_sems, recv_sems, thru, plan, tag))
        return token[0, 0]

    w0 = dict(g_mix=shard["g_mix"] + in_token[0, 0], pool_scale=shard["pool_scale"], g_ffn=shard["g_ffn"],
              conv_b=shard["conv_b"], g_final=shard["g_final"].reshape(1, D))
    _, grad_x, gr = _local_step(x[0], loss_target[0], w0, late_weights, send)

    halves, small_parts = {}, None
    for names, send_sems, recv_sems, thru, plan, tag in pending:
        done = _push_wait(send_sems, recv_sems, thru, plan, grad_x, f"comm_scatter_wait_{tag}")
        if names == "small":
            small_parts = _sum_small(done[0], done[1], me, "sum_small").reshape(-1)
        else:
            m = len(names)
            for t, k in enumerate(names):
                halves[k] = _sum_pieces(done[t], BIG_AXIS[k], done[m + t], pos, f"sum_{k}")
    g_mix_own = _as_rows([gr["g_mix"]])
    _, g_mix_recv = _push([g_mix_own, lax.empty((N_DEV - 1,) + g_mix_own.shape, F32)], _broadcast_plan, N_DEV - 1, "comm_gather_g_mix")
    g_mix_sum = _sum_small(g_mix_own, g_mix_recv, me, "sum_g_mix").reshape(-1)[:D]
    wholes = _push([halves[k] for k in BIG], _exchange_plan, EXCHANGE_CHUNKS * len(BIG), "comm_exchange_halves")

    grads = {"g_mix": g_mix_sum.reshape(shard["g_mix"].shape)}
    for k, whole in zip(BIG, wholes):
        grads[k] = whole.reshape(shard[k].shape)
    off = 0
    loss = None
    for k in SMALL:
        sz = math.prod(gr[k].shape)
        fullg = small_parts[off:off + sz].reshape(gr[k].shape)
        off += sz
        if k == "loss_cols":
            loss = jnp.sum(fullg)
            continue
        if k in ("w_pool_lin", "conv_w"):
            n = shard[k].shape[2]
            fullg = lax.dynamic_slice_in_dim(fullg, chip * n, n, axis=1)
        grads[k] = fullg.reshape(shard[k].shape)

    deltas, new_m, new_v = {}, {}, {}
    for k in ORDER:
        shp = shard[k].shape
        two_d = (-1, shp[-1])
        dl, nm, nv = _adamw(shard[k].reshape(two_d), grads[k].reshape(two_d), mom[k].reshape(two_d), vel[k].reshape(two_d), f"adamw_{k}")
        deltas[k], new_m[k], new_v[k] = dl.reshape(shp), nm.reshape(shp), nv.reshape(shp)

    return (loss, grad_x[None], *[grads[k] for k in ORDER], *[deltas[k] for k in ORDER], *[new_m[k] for k in ORDER], *[new_v[k] for k in ORDER])
```

```python
import functools
import math

import jax
import jax.numpy as jnp
from jax import lax
from jax.experimental import pallas as pl
from jax.experimental.pallas import tpu as pltpu

F32 = jnp.float32
BF16 = jnp.bfloat16

RMS_EPS = 1e-6
POOL_WINDOWS = (2, 4, 8, 16)
ATTN_GROUPS = ((128, 1), (512, 4), (2048, 16))
HEADS_PER_GROUP = 4
HEAD_DIM = 128
N_ATTN_HEADS = HEADS_PER_GROUP * len(ATTN_GROUPS)
SPAN = 128
GROUP_WIDTH = HEADS_PER_GROUP * HEAD_DIM
ATTN_WIDTH = N_ATTN_HEADS * HEAD_DIM
ATTN_SCALE = HEAD_DIM ** -0.5
NEG_BIG = -1e30
ALIBI_SLOPES = tuple(2.0 ** (-8.0 * (h + 1) / N_ATTN_HEADS) for h in range(N_ATTN_HEADS))

ADAM_LR = 0.001
ADAM_B1 = 0.9
ADAM_B2 = 0.999
ADAM_EPS = 1e-08
ADAM_WD = 0.01
ADAM_STEP = 10

INV_SQRT2 = 1.0 / math.sqrt(2.0)
INV_SQRT_2PI = 1.0 / math.sqrt(2.0 * math.pi)

HALO = 16
VMEM_LIMIT = 56 * 1024 * 1024
N_CHIPS = 4
N_DEV = 8
MESH = pl.DeviceIdType.MESH
ANY = pl.BlockSpec(memory_space=pl.ANY)


def _cparams(*sem):
    return pltpu.CompilerParams(dimension_semantics=sem, vmem_limit_bytes=VMEM_LIMIT)


def _tile(n, pref, mult=128):
    t = (min(pref, n) // mult) * mult
    while t >= mult:
        if n % t == 0:
            return t
        t -= mult
    return n


def _dot(a, b, contract):
    return lax.dot_general(a, b, (contract, ((), ())), preferred_element_type=F32)


def _dot_nn(a, b):
    return _dot(a, b, ((1,), (0,)))


def _dot_nt(a, b):
    return _dot(a, b, ((1,), (1,)))


def _mm(a, b, *, mode, dims, name, tiles=None, out_dtypes=(BF16,), epilogue=None, extras=(), a_off=(0, 0), b_off=(0, 0), after=None):
    M, N, K = dims
    if tiles is None:
        tiles = (_tile(M, 1408), _tile(N, 2816), _tile(K, 512)) if mode == "tn" else (_tile(M, 1024), _tile(N, 1536), _tile(K, 2816))
    tm, tn, tk = tiles
    assert M % tm == 0 and N % tn == 0 and K % tk == 0, (name, dims, tiles)
    nk = K // tk
    if mode == "nn":
        ab, bb, contract = (tm, tk), (tk, tn), ((1,), (0,))
        amap = lambda i, j, k: (i + a_off[0] // tm, k + a_off[1] // tk)
        bmap = lambda i, j, k: (k + b_off[0] // tk, j + b_off[1] // tn)
    elif mode == "nt":
        ab, bb, contract = (tm, tk), (tn, tk), ((1,), (1,))
        amap = lambda i, j, k: (i + a_off[0] // tm, k + a_off[1] // tk)
        bmap = lambda i, j, k: (j + b_off[0] // tn, k + b_off[1] // tk)
    else:
        ab, bb, contract = (tk, tm), (tk, tn), ((0,), (0,))
        amap = lambda i, j, k: (k + a_off[0] // tk, i + a_off[1] // tm)
        bmap = lambda i, j, k: (k + b_off[0] // tk, j + b_off[1] // tn)
    assert a_off[0] % ab[0] == 0 and a_off[1] % ab[1] == 0 and b_off[0] % bb[0] == 0 and b_off[1] % bb[1] == 0, name
    in_specs = [pl.BlockSpec(ab, amap), pl.BlockSpec(bb, bmap)]
    ex_arrays = []
    for arr, kind, off in extras:
        if kind == "mn":
            assert off[0] % tm == 0 and off[1] % tn == 0, name
            in_specs.append(pl.BlockSpec((tm, tn), lambda i, j, k, off=off: (i + off[0] // tm, j + off[1] // tn)))
        else:
            assert off[1] % tn == 0, name
            in_specs.append(pl.BlockSpec((1, tn), lambda i, j, k, off=off: (0, j + off[1] // tn)))
        ex_arrays.append(arr)
    ne, no = len(ex_arrays), len(out_dtypes)
    if after is not None:
        in_specs.append(ANY)
        ex_arrays.append(after)
    first_out = 2 + len(ex_arrays)
    if epilogue is None:
        def epilogue(acc, ex, outs):
            outs[0][...] = acc.astype(outs[0].dtype)

    def body(*refs):
        a_ref, b_ref = refs[0], refs[1]
        ex, outs = refs[2:2 + ne], refs[first_out:first_out + no]
        if nk == 1:
            epilogue(_dot(a_ref[...], b_ref[...], contract), ex, outs)
            return
        acc = refs[-1]
        k = pl.program_id(2)
        if nk <= 4:
            part = _dot(a_ref[...], b_ref[...], contract)

            @pl.when(k == 0)
            def _():
                acc[...] = part

            @pl.when(jnp.logical_and(k > 0, k < nk - 1))
            def _():
                acc[...] += part

            @pl.when(k == nk - 1)
            def _():
                epilogue(acc[...] + part, ex, outs)
        else:
            @pl.when(k == 0)
            def _():
                acc[...] = _dot(a_ref[...], b_ref[...], contract)

            @pl.when(k > 0)
            def _():
                acc[...] += _dot(a_ref[...], b_ref[...], contract)

            @pl.when(k == nk - 1)
            def _():
                epilogue(acc[...], ex, outs)

    res = pl.pallas_call(
        body,
        grid=(M // tm, N // tn, nk),
        in_specs=in_specs,
        out_specs=[pl.BlockSpec((tm, tn), lambda i, j, k: (i, j)) for _ in out_dtypes],
        out_shape=[jax.ShapeDtypeStruct((M, N), dt) for dt in out_dtypes],
        scratch_shapes=[pltpu.VMEM((tm, tn), F32)] if nk > 1 else [],
        compiler_params=_cparams("parallel", "parallel", "arbitrary"),
        name=name,
    )(a, b, *ex_arrays)
    return res[0] if no == 1 else res


def _rms_fwd(x, g, name):
    S, D = x.shape
    tm = _tile(S, 512)

    def body(x_ref, g_ref, h_ref):
        xv = x_ref[...]
        r = lax.rsqrt(jnp.mean(xv * xv, axis=-1, keepdims=True) + RMS_EPS)
        h_ref[...] = (xv * r * g_ref[...]).astype(h_ref.dtype)

    return pl.pallas_call(
        body,
        grid=(S // tm,),
        in_specs=[pl.BlockSpec((tm, D), lambda i: (i, 0)), pl.BlockSpec((1, D), lambda i: (0, 0))],
        out_specs=pl.BlockSpec((tm, D), lambda i: (i, 0)),
        out_shape=jax.ShapeDtypeStruct((S, D), BF16),
        compiler_params=_cparams("parallel"),
        name=name,
    )(x, g)


def _rms_bwd(dh, x, g, dres, name, out_dtype):
    S, D = x.shape
    tm = _tile(S, 512)

    def body(dh_ref, x_ref, g_ref, dres_ref, dx_ref, dg_ref):
        xv = x_ref[...]
        r = lax.rsqrt(jnp.mean(xv * xv, axis=-1, keepdims=True) + RMS_EPS)
        xr = xv * r
        dhv = dh_ref[...].astype(F32)

        @pl.when(pl.program_id(0) == 0)
        def _():
            dg_ref[...] = jnp.zeros_like(dg_ref)

        dg_ref[...] += jnp.sum(dhv * xr, axis=0, keepdims=True)
        u = dhv * g_ref[...]
        c = jnp.mean(u * xr, axis=-1, keepdims=True)
        dx_ref[...] = (dres_ref[...].astype(F32) + r * (u - xr * c)).astype(dx_ref.dtype)

    row = pl.BlockSpec((tm, D), lambda i: (i, 0))
    vec = pl.BlockSpec((1, D), lambda i: (0, 0))
    return pl.pallas_call(
        body,
        grid=(S // tm,),
        in_specs=[row, row, vec, row],
        out_specs=[row, vec],
        out_shape=[jax.ShapeDtypeStruct((S, D), out_dtype), jax.ShapeDtypeStruct((1, D), F32)],
        compiler_params=_cparams("arbitrary"),
        name=name,
    )(dh, x, g, dres)


def _loss_head(x3, tgt, g, name):
    S, D = x3.shape
    tm = _tile(S, 512)

    def body(x_ref, t_ref, g_ref, dxb_ref, dg_ref, loss_ref):
        xv = x_ref[...]
        gv = g_ref[...]
        r = lax.rsqrt(jnp.mean(xv * xv, axis=-1, keepdims=True) + RMS_EPS)
        xr = xv * r
        e = xr * gv - t_ref[...]

        @pl.when(pl.program_id(0) == 0)
        def _():
            dg_ref[...] = jnp.zeros_like(dg_ref)
            loss_ref[...] = jnp.zeros_like(loss_ref)

        loss_ref[...] += jnp.sum(e * e, axis=0, keepdims=True) * (0.5 / D)
        dy = e * (1.0 / D)
        dg_ref[...] += jnp.sum(dy * xr, axis=0, keepdims=True)
        u = dy * gv
        c = jnp.mean(u * xr, axis=-1, keepdims=True)
        dxb_ref[...] = (r * (u - xr * c)).astype(BF16)

    row = pl.BlockSpec((tm, D), lambda i: (i, 0))
    vec = pl.BlockSpec((1, D), lambda i: (0, 0))
    return pl.pallas_call(
        body,
        grid=(S // tm,),
        in_specs=[row, row, vec],
        out_specs=[row, vec, vec],
        out_shape=[jax.ShapeDtypeStruct((S, D), BF16), jax.ShapeDtypeStruct((1, D), F32), jax.ShapeDtypeStruct((1, D), F32)],
        compiler_params=_cparams("arbitrary"),
        name=name,
    )(x3, tgt, g)


SHIFT_ROWS = 256


def _shift_matrix():
    i = lax.broadcasted_iota(jnp.int32, (2 * SHIFT_ROWS, SHIFT_ROWS), 0)
    j = lax.broadcasted_iota(jnp.int32, (2 * SHIFT_ROWS, SHIFT_ROWS), 1)
    src = jnp.where(i < SHIFT_ROWS, i - 1, i - SHIFT_ROWS - 2)
    return (j == src).astype(BF16)


def _conv_taps(cur_ref, halo_ref, w_ref, b_ref, first, shift):
    w, bias = w_ref[...], b_ref[...]
    before = jnp.where(first, 0.0, halo_ref[...].astype(F32)[HALO - 8:])
    sub = lax.broadcasted_iota(jnp.int32, before.shape, 0)
    out = []
    for blk in range(cur_ref.shape[0] // SHIFT_ROWS):
        xb = cur_ref[blk * SHIFT_ROWS:(blk + 1) * SHIFT_ROWS, :]
        xf = xb.astype(F32)
        both = _dot_nn(shift, xb)
        p1, p2 = both[:SHIFT_ROWS], both[SHIFT_ROWS:]
        p1 = jnp.concatenate([jnp.where(sub == 0, pltpu.roll(before, 1, 0), p1[:8]), p1[8:]], axis=0)
        p2 = jnp.concatenate([jnp.where(sub < 2, pltpu.roll(before, 2, 0), p2[:8]), p2[8:]], axis=0)
        out.append(bias + w[0:1] * p2 + w[1:2] * p1 + w[2:3] * xf)
        before = xf[SHIFT_ROWS - 8:]
    return jnp.concatenate(out, axis=0)


def _convglu_fwd(up, cw, cb, name):
    S, F2 = up.shape
    F = F2 // 2
    tm, tn = _tile(S, 512), _tile(F, 1408)
    nj, hb = F // tn, tm // HALO

    def body(ua, ub, ha, hb_, wa, wb, ba, bb, f_ref, a_ref, b_ref):
        first = pl.program_id(0) == 0
        shift = _shift_matrix()
        a = _conv_taps(ua, ha, wa, ba, first, shift)
        b = _conv_taps(ub, hb_, wb, bb, first, shift)
        f_ref[...] = (0.5 * a * (1.0 + lax.erf(a * INV_SQRT2)) * b).astype(f_ref.dtype)
        a_ref[...] = a.astype(a_ref.dtype)
        b_ref[...] = b.astype(b_ref.dtype)

    tile = pl.BlockSpec((tm, tn), lambda i, j: (i, j))
    return pl.pallas_call(
        body,
        grid=(S // tm, nj),
        in_specs=[
            tile,
            pl.BlockSpec((tm, tn), lambda i, j: (i, j + nj)),
            pl.BlockSpec((HALO, tn), lambda i, j: (jnp.maximum(i * hb - 1, 0), j)),
            pl.BlockSpec((HALO, tn), lambda i, j: (jnp.maximum(i * hb - 1, 0), j + nj)),
            pl.BlockSpec((3, tn), lambda i, j: (0, j)),
            pl.BlockSpec((3, tn), lambda i, j: (0, j + nj)),
            pl.BlockSpec((1, tn), lambda i, j: (0, j)),
            pl.BlockSpec((1, tn), lambda i, j: (0, j + nj)),
        ],
        out_specs=[tile, tile, tile],
        out_shape=[jax.ShapeDtypeStruct((S, F), BF16)] * 3,
        compiler_params=_cparams("parallel", "parallel"),
        name=name,
    )(up, up, up, up, cw, cw, cb, cb)


def _convglu_bwd(df, a, b, up, cw, name):
    S, F = df.shape
    tm, tn = _tile(S, 512), _tile(F, 1408)
    nj, ni, hb = F // tn, S // tm, tm // HALO
    n = tm + HALO

    def body(df_ref, dfn_ref, a_ref, an_ref, b_ref, bn_ref, up_ref, w_ref, o_ref, db_ref, dw_ref):
        j, i = pl.program_id(0), pl.program_id(1)
        last = i == ni - 1

        def rows(c_ref, n_ref):
            return jnp.concatenate([c_ref[...].astype(F32), jnp.where(last, 0.0, n_ref[...].astype(F32))], axis=0)

        @pl.when(i == 0)
        def _():
            db_ref[...] = jnp.zeros_like(db_ref)
            dw_ref[...] = jnp.zeros_like(dw_ref)

        def finish(d):
            i_ = lax.broadcasted_iota(jnp.int32, (2 * SHIFT_ROWS, SHIFT_ROWS), 0)
            j_ = lax.broadcasted_iota(jnp.int32, (2 * SHIFT_ROWS, SHIFT_ROWS), 1)
            ahead = (j_ == jnp.where(i_ < SHIFT_ROWS, i_ + 1, i_ - SHIFT_ROWS + 2)).astype(BF16)
            db = d.astype(BF16)
            sub = lax.broadcasted_iota(jnp.int32, (8, tn), 0)
            d1, d2 = [], []
            for blk in range(tm // SHIFT_ROWS):
                lo, hi = blk * SHIFT_ROWS, (blk + 1) * SHIFT_ROWS
                both = _dot_nn(ahead, db[lo:hi])
                n1, n2 = both[:SHIFT_ROWS], both[SHIFT_ROWS:]
                after = db[hi:hi + HALO].astype(F32)[:8]
                d1 += [n1[:-8], jnp.where(sub == 7, pltpu.roll(after, 7, 0), n1[-8:])]
                d2 += [n2[:-8], jnp.where(sub >= 6, pltpu.roll(after, 6, 0), n2[-8:])]
            d0, d1, d2 = d[:tm], jnp.concatenate(d1, axis=0), jnp.concatenate(d2, axis=0)
            w = w_ref[...]
            o_ref[...] = (w[2:3] * d0 + w[1:2] * d1 + w[0:1] * d2).astype(o_ref.dtype)
            upv = up_ref[...].astype(F32)
            db_ref[...] += jnp.sum(d0, axis=0, keepdims=True)
            dw_ref[0:1, :] += jnp.sum(d2 * upv, axis=0, keepdims=True)
            dw_ref[1:2, :] += jnp.sum(d1 * upv, axis=0, keepdims=True)
            dw_ref[2:3, :] += jnp.sum(d0 * upv, axis=0, keepdims=True)

        av, dfv = rows(a_ref, an_ref), rows(df_ref, dfn_ref)
        cdf = 0.5 * (1.0 + lax.erf(av * INV_SQRT2))

        @pl.when(j < nj)
        def _():
            pdf = jnp.exp(-0.5 * av * av) * INV_SQRT_2PI
            finish(dfv * rows(b_ref, bn_ref) * (cdf + av * pdf))

        @pl.when(j >= nj)
        def _():
            finish(dfv * (av * cdf))

    jh = lambda j: lax.rem(j, nj)
    nxt = lambda i: jnp.minimum((i + 1) * hb, S // HALO - 1)
    cur = pl.BlockSpec((tm, tn), lambda j, i: (i, jh(j)))
    halo = pl.BlockSpec((HALO, tn), lambda j, i: (nxt(i), jh(j)))
    return pl.pallas_call(
        body,
        grid=(2 * nj, ni),
        in_specs=[cur, halo, cur, halo, cur, halo, pl.BlockSpec((tm, tn), lambda j, i: (i, j)), pl.BlockSpec((3, tn), lambda j, i: (0, j))],
        out_specs=[pl.BlockSpec((tm, tn), lambda j, i: (i, j)), pl.BlockSpec((1, tn), lambda j, i: (0, j)), pl.BlockSpec((3, tn), lambda j, i: (0, j))],
        out_shape=[jax.ShapeDtypeStruct((S, 2 * F), BF16), jax.ShapeDtypeStruct((1, 2 * F), F32), jax.ShapeDtypeStruct((3, 2 * F), F32)],
        compiler_params=_cparams("parallel", "arbitrary"),
        name=name,
    )(df, df, a, a, b, b, up, cw)


def _gate_bwd(dmixed, gates, y_pool, y_attn, in_width, name):
    S, D = dmixed.shape
    tm, tn = _tile(S, 2048), _tile(D, 512)
    nj = D // tn
    pre0 = (in_width - 2 * D) // tn
    assert pre0 * tn == in_width - 2 * D

    def body(dm_ref, g_ref, yp_ref, ya_ref, dy_ref, dpre_ref, db_ref):
        j = pl.program_id(0)

        @pl.when(pl.program_id(1) == 0)
        def _():
            db_ref[...] = jnp.zeros_like(db_ref)

        def run(y_ref):
            dm = dm_ref[...].astype(F32)
            gv = g_ref[...].astype(F32)
            dy_ref[...] = (dm * gv).astype(BF16)
            dpre = dm * y_ref[...].astype(F32) * gv * (1.0 - gv)
            dpre_ref[...] = dpre.astype(BF16)
            db_ref[...] += jnp.sum(dpre, axis=0, keepdims=True)

        @pl.when(j < nj)
        def _():
            run(yp_ref)

        @pl.when(j >= nj)
        def _():
            run(ya_ref)

    tile2 = pl.BlockSpec((tm, tn), lambda j, i: (i, j))
    return pl.pallas_call(
        body,
        grid=(2 * nj, S // tm),
        in_specs=[
            pl.BlockSpec((tm, tn), lambda j, i: (i, lax.rem(j, nj))),
            tile2,
            pl.BlockSpec((tm, tn), lambda j, i: (i, jnp.minimum(j, nj - 1))),
            pl.BlockSpec((tm, tn), lambda j, i: (i, jnp.maximum(j - nj, 0))),
        ],
        out_specs=[tile2, pl.BlockSpec((tm, tn), lambda j, i: (i, pre0 + j)), pl.BlockSpec((1, tn), lambda j, i: (0, j))],
        out_shape=[jax.ShapeDtypeStruct((S, 2 * D), BF16), jax.ShapeDtypeStruct((S, in_width), BF16), jax.ShapeDtypeStruct((1, 2 * D), F32)],
        compiler_params=_cparams("parallel", "arbitrary"),
        name=name,
    )(dmixed, gates, y_pool, y_attn)


def _pool_counts(i, tm, rows, w):
    t = i * tm + lax.broadcasted_iota(jnp.int32, (rows, 1), 0)
    return jnp.minimum(t + 1, w).astype(F32)


def _pooled_groups(u_ref, uh_ref, i, tm, C):
    cur = u_ref[...]
    halo = jnp.where(i == 0, 0.0, uh_ref[...])
    xx = jnp.concatenate([halo, cur], axis=0)
    out = []
    s = xx
    for gi, w in enumerate(POOL_WINDOWS):
        s = s + pltpu.roll(s, w // 2, 0)
        tot = s[HALO:, 0:C]
        out.append(tot / _pool_counts(i, tm, tm, w) - cur[:, gi * C:(gi + 1) * C])
        s = s[:, C:] if gi + 1 < len(POOL_WINDOWS) else s
    return out


def _pool_fwd(u, wl, scale, name):
    S, PW = u.shape
    C = PW // len(POOL_WINDOWS)
    tm = _tile(S, 512)
    hb = tm // HALO

    def body(u_ref, uh_ref, wl_ref, sc_ref, o_ref):
        i = pl.program_id(0)
        pooled = _pooled_groups(u_ref, uh_ref, i, tm, C)
        for gi in range(len(POOL_WINDOWS)):
            y = _dot_nn(pooled[gi].astype(BF16), wl_ref[gi])
            o_ref[:, gi * C:(gi + 1) * C] = (y * sc_ref[:, gi * C:(gi + 1) * C]).astype(o_ref.dtype)

    return pl.pallas_call(
        body,
        grid=(S // tm,),
        in_specs=[
            pl.BlockSpec((tm, PW), lambda i: (i, 0)),
            pl.BlockSpec((HALO, PW), lambda i: (jnp.maximum(i * hb - 1, 0), 0)),
            pl.BlockSpec((len(POOL_WINDOWS), C, C), lambda i: (0, 0, 0)),
            pl.BlockSpec((1, PW), lambda i: (0, 0)),
        ],
        out_specs=pl.BlockSpec((tm, PW), lambda i: (i, 0)),
        out_shape=jax.ShapeDtypeStruct((S, PW), BF16),
        compiler_params=_cparams("parallel"),
        name=name,
    )(u, u, wl, scale)


def _pool_bwd(u, dp, wl, scale, dproj, name):
    S, PW = u.shape
    G = len(POOL_WINDOWS)
    C = PW // G
    tm = _tile(S, 512)
    hb, ni = tm // HALO, S // tm
    n = tm + HALO

    def body(u_ref, uh_ref, dp_ref, dpn_ref, wl_ref, sc_ref, _, du_ref, dwl_ref, dsc_ref):
        i = pl.program_id(0)

        @pl.when(i == 0)
        def _():
            dwl_ref[...] = jnp.zeros_like(dwl_ref)
            dsc_ref[...] = jnp.zeros_like(dsc_ref)

        pooled = _pooled_groups(u_ref, uh_ref, i, tm, C)
        dpc = dp_ref[...].astype(F32)
        dpn = jnp.where(i == ni - 1, 0.0, dpn_ref[...].astype(F32))
        sc = sc_ref[...]
        dyl = jnp.concatenate([dpc, dpn], axis=0) * sc
        for gi, w in enumerate(POOL_WINDOWS):
            cols = slice(gi * C, (gi + 1) * C)
            pb = pooled[gi].astype(BF16)
            ylin = _dot_nn(pb, wl_ref[gi])
            dsc_ref[:, cols] += jnp.sum(dpc[:, cols] * ylin, axis=0, keepdims=True)
            dylg = dyl[:, cols].astype(BF16)
            dwl_ref[gi] += _dot(pb, dylg[:tm], ((0,), (0,)))
            dpool = _dot_nt(dylg, wl_ref[gi])
            e = dpool / _pool_counts(i, tm, n, w)
            k = 1
            while k < w:
                e = e + pltpu.roll(e, n - k, 0)
                k *= 2
            du_ref[:, cols] = (e[:tm] - dpool[:tm]).astype(du_ref.dtype)

    return pl.pallas_call(
        body,
        grid=(ni,),
        in_specs=[
            pl.BlockSpec((tm, PW), lambda i: (i, 0)),
            pl.BlockSpec((HALO, PW), lambda i: (jnp.maximum(i * hb - 1, 0), 0)),
            pl.BlockSpec((tm, PW), lambda i: (i, 0)),
            pl.BlockSpec((HALO, PW), lambda i: (jnp.minimum((i + 1) * hb, S // HALO - 1), 0)),
            pl.BlockSpec((G, C, C), lambda i: (0, 0, 0)),
            pl.BlockSpec((1, PW), lambda i: (0, 0)),
            ANY,
        ],
        out_specs=[pl.BlockSpec((tm, PW), lambda i: (i, 0)), pl.BlockSpec((G, C, C), lambda i: (0, 0, 0)), pl.BlockSpec((1, PW), lambda i: (0, 0))],
        out_shape=[jax.ShapeDtypeStruct(dproj.shape, dproj.dtype), jax.ShapeDtypeStruct((G, C, C), F32), jax.ShapeDtypeStruct((1, PW), F32)],
        input_output_aliases={6: 0},
        compiler_params=_cparams("arbitrary"),
        name=name,
    )(u, u, dp, dp, wl, scale, dproj)


def _band_masks():
    ii = lax.broadcasted_iota(jnp.int32, (SPAN, SPAN), 0)
    kk = lax.broadcasted_iota(jnp.int32, (SPAN, SPAN), 1)
    return ((ii + SPAN - kk).astype(F32), kk >= ii), ((ii - kk).astype(F32), kk <= ii)


ATTN_TILE = 32 * SPAN


def _unit_rows(r, b, d, blocks=1):
    return pl.ds(d * SPAN * b + r, blocks * SPAN, stride=d) if d > 1 else pl.ds(SPAN * b, blocks * SPAN)


def _f32_copies(refs, scratch, d):
    if d == 1:
        return list(refs)
    for ref, s in zip(refs, scratch):
        s[...] = ref[...].astype(F32)
    return list(scratch)


def _attn_fwd(qkv, d, g, name):
    S = qkv.shape[0]
    T = min(ATTN_TILE, S)
    P = SPAN * d
    nbk = T // P

    def body(q_ref, k_ref, v_ref, kp_ref, vp_ref, o_ref, lse_ref, *scratch):
        c = pl.program_id(0)
        (jp, mp), (jc, mc) = _band_masks()
        slopes = [ALIBI_SLOPES[g * HEADS_PER_GROUP + h] * d for h in range(HEADS_PER_GROUP)]
        slope = slopes[0]
        for h in range(1, HEADS_PER_GROUP):
            slope = jnp.where(pl.program_id(1) == h, slopes[h], slope)
        q_s, k_s, v_s, kp_s, vp_s = _f32_copies((q_ref, k_ref, v_ref, kp_ref, vp_ref), scratch[:5], d)
        o_s, l_s = (o_ref, lse_ref) if d == 1 else scratch[5:7]
        bias_p, bias_c = jnp.where(mp, -slope * jp, NEG_BIG), jnp.where(mc, -slope * jc, NEG_BIG)
        bias = jnp.concatenate([bias_p, bias_c], axis=1)
        bias_first = jnp.concatenate([jnp.where(c > 0, bias_p, NEG_BIG), bias_c], axis=1)
        for r in range(d):
            for b in range(nbk):
                rows = _unit_rows(r, b, d)
                q = q_s[rows, :].astype(BF16)
                if b == 0:
                    prev = _unit_rows(r, 0, d)
                    kk = jnp.concatenate([kp_s[prev, :], k_s[rows, :]], axis=0).astype(BF16)
                    vv = jnp.concatenate([vp_s[prev, :], v_s[rows, :]], axis=0).astype(BF16)
                else:
                    both = _unit_rows(r, b - 1, d, 2)
                    kk, vv = k_s[both, :].astype(BF16), v_s[both, :].astype(BF16)
                s = _dot_nt(q, kk) * ATTN_SCALE + (bias_first if b == 0 else bias)
                m = jnp.max(s, axis=-1, keepdims=True)
                p = jnp.exp(s - m)
                l = jnp.sum(p, axis=-1, keepdims=True)
                o_s[rows, :] = _dot_nn(p.astype(BF16), vv) / l
                l_s[rows, :] = jnp.broadcast_to(m + jnp.log(l), (SPAN, HEAD_DIM))
        if d > 1:
            o_ref[...] = o_s[...]
            lse_ref[...] = l_s[...]

    col = lambda kind: (lambda c, h: (c, kind * N_ATTN_HEADS + g * HEADS_PER_GROUP + h))
    pcol = lambda kind: (lambda c, h: (jnp.maximum(c * nbk - 1, 0), kind * N_ATTN_HEADS + g * HEADS_PER_GROUP + h))
    cur = lambda kind: pl.BlockSpec((T, HEAD_DIM), col(kind))
    prv = lambda kind: pl.BlockSpec((P, HEAD_DIM), pcol(kind))
    out = pl.BlockSpec((T, HEAD_DIM), lambda c, h: (c, h))
    scratch = [] if d == 1 else [pltpu.VMEM((T, HEAD_DIM), F32)] * 3 + [pltpu.VMEM((P, HEAD_DIM), F32)] * 2 + [pltpu.VMEM((T, HEAD_DIM), F32)] * 2
    return pl.pallas_call(
        body,
        grid=(S // T, HEADS_PER_GROUP),
        in_specs=[cur(0), cur(1), cur(2), prv(1), prv(2)],
        out_specs=[out, out],
        out_shape=[jax.ShapeDtypeStruct((S, GROUP_WIDTH), F32)] * 2,
        scratch_shapes=scratch,
        compiler_params=_cparams("parallel", "parallel"),
        name=name,
    )(qkv, qkv, qkv, qkv, qkv)


def _attn_merge(os_, lses, name):
    S, W = os_[0].shape
    tm = _tile(S, 1024)

    def body(o0, o1, o2, l0, l1, l2, y_ref, lse_ref):
        ls = [l0[...], l1[...], l2[...]]
        m = jnp.maximum(jnp.maximum(ls[0], ls[1]), ls[2])
        es = [jnp.exp(v - m) for v in ls]
        tot = es[0] + es[1] + es[2]
        y = (es[0] * o0[...] + es[1] * o1[...] + es[2] * o2[...]) / tot
        y_ref[...] = y.astype(y_ref.dtype)
        lse_ref[...] = m + jnp.log(tot)

    row = pl.BlockSpec((tm, W), lambda i: (i, 0))
    return pl.pallas_call(
        body,
        grid=(S // tm,),
        in_specs=[row] * 6,
        out_specs=[row, row],
        out_shape=[jax.ShapeDtypeStruct((S, W), BF16), jax.ShapeDtypeStruct((S, W), F32)],
        compiler_params=_cparams("parallel"),
        name=name,
    )(*os_, *lses)


def _attn_bwd(qkv, dattn, y, lse, dproj, d, g, col0, name):
    S = qkv.shape[0]
    T = min(ATTN_TILE, S)
    P = SPAN * d
    nbk = T // P
    ntile = S // T

    def body(q_ref, k_ref, v_ref, kp_ref, vp_ref, qn_ref, da_ref, dan_ref, y_ref, yn_ref, lse_ref, lsen_ref, _, out_ref, dq_s, dk_s, dv_s, *scratch):
        c = pl.program_id(0)
        head_id = pl.program_id(1)
        kind = pl.program_id(2)

        @pl.when(kind == 0)
        def _():
            (jp, mp), (jc, mc) = _band_masks()
            slopes = [ALIBI_SLOPES[g * HEADS_PER_GROUP + h] * d for h in range(HEADS_PER_GROUP)]
            slope = slopes[0]
            for h in range(1, HEADS_PER_GROUP):
                slope = jnp.where(head_id == h, slopes[h], slope)
            q_s, k_s, v_s, da_s, y_s, kp_s, vp_s, qn_s, dan_s, yn_s = _f32_copies(
                (q_ref, k_ref, v_ref, da_ref, y_ref, kp_ref, vp_ref, qn_ref, dan_ref, yn_ref), scratch, d)
            bias_p, bias_c = jnp.where(mp, -slope * jp, NEG_BIG), jnp.where(mc, -slope * jc, NEG_BIG)
            bias = jnp.concatenate([bias_c, bias_p], axis=0)
            bias_last = jnp.concatenate([bias_c, jnp.where(c < ntile - 1, bias_p, NEG_BIG)], axis=0)
            bias_first = jnp.where(c > 0, bias_p, NEG_BIG)

            def pair(q, da, yy, lse_blk, kk, vv, b):
                dd = jnp.sum(da.astype(F32) * yy.astype(F32), axis=-1, keepdims=True)
                p = jnp.exp(_dot_nt(q, kk) * ATTN_SCALE + b - lse_blk[:, 0:1])
                return p, p * (_dot_nt(da, vv) - dd)

            for r in range(d):
                first = _unit_rows(r, 0, d)
                kk, vv = kp_s[first, :].astype(BF16), vp_s[first, :].astype(BF16)
                _, ds = pair(q_s[first, :].astype(BF16), da_s[first, :].astype(BF16), y_s[first, :], lse_ref[first, :], kk, vv, bias_first)
                dq_next = _dot_nn(ds.astype(BF16), kk)
                for kb in range(nbk):
                    rows = _unit_rows(r, kb, d)
                    if kb + 1 < nbk:
                        both = _unit_rows(r, kb, d, 2)
                        q, da, yy, lse_blk = q_s[both, :], da_s[both, :], y_s[both, :], lse_ref[both, :]
                    else:
                        q = jnp.concatenate([q_s[rows, :], qn_s[first, :]], axis=0)
                        da = jnp.concatenate([da_s[rows, :], dan_s[first, :]], axis=0)
                        yy = jnp.concatenate([y_s[rows, :], yn_s[first, :]], axis=0)
                        lse_blk = jnp.concatenate([lse_ref[rows, :], lsen_ref[first, :]], axis=0)
                    q, da = q.astype(BF16), da.astype(BF16)
                    kk, vv = k_s[rows, :].astype(BF16), v_s[rows, :].astype(BF16)
                    p, ds = pair(q, da, yy, lse_blk, kk, vv, bias if kb + 1 < nbk else bias_last)
                    dv_s[rows, :] = _dot_nn(p.T.astype(BF16), da)
                    dk_s[rows, :] = _dot_nn(ds.T.astype(BF16), q) * ATTN_SCALE
                    dq_both = _dot_nn(ds.astype(BF16), kk)
                    dq_s[rows, :] = (dq_next + dq_both[:SPAN]) * ATTN_SCALE
                    dq_next = dq_both[SPAN:]
            out_ref[...] = dq_s[...].astype(out_ref.dtype)

        @pl.when(kind == 1)
        def _():
            out_ref[...] = dk_s[...].astype(out_ref.dtype)

        @pl.when(kind == 2)
        def _():
            out_ref[...] = dv_s[...].astype(out_ref.dtype)

    head = lambda h: g * HEADS_PER_GROUP + h
    cur = lambda kind: pl.BlockSpec((T, HEAD_DIM), lambda c, h, kd: (c, kind * N_ATTN_HEADS + head(h)))
    prv = lambda kind: pl.BlockSpec((P, HEAD_DIM), lambda c, h, kd: (jnp.maximum(c * nbk - 1, 0), kind * N_ATTN_HEADS + head(h)))
    nxt_row = lambda c: jnp.minimum((c + 1) * nbk, S // P - 1)
    qnext = pl.BlockSpec((P, HEAD_DIM), lambda c, h, kd: (nxt_row(c), head(h)))
    hcur = pl.BlockSpec((T, HEAD_DIM), lambda c, h, kd: (c, h))
    hnext = pl.BlockSpec((P, HEAD_DIM), lambda c, h, kd: (nxt_row(c), h))
    out = pl.BlockSpec((T, HEAD_DIM), lambda c, h, kd: (c, col0 + kd * N_ATTN_HEADS + head(h)))
    stage = [pltpu.VMEM((T, HEAD_DIM), F32)] * 3
    copies = [] if d == 1 else [pltpu.VMEM((T, HEAD_DIM), F32)] * 5 + [pltpu.VMEM((P, HEAD_DIM), F32)] * 5
    return pl.pallas_call(
        body,
        grid=(ntile, HEADS_PER_GROUP, 3),
        in_specs=[cur(0), cur(1), cur(2), prv(1), prv(2), qnext, hcur, hnext, hcur, hnext, hcur, hnext, ANY],
        out_specs=out,
        out_shape=jax.ShapeDtypeStruct(dproj.shape, dproj.dtype),
        input_output_aliases={12: 0},
        scratch_shapes=stage + copies,
        compiler_params=_cparams("parallel", "parallel", "arbitrary"),
        name=name,
    )(qkv, qkv, qkv, qkv, qkv, qkv, dattn, dattn, y, y, lse, lse, dproj)


def _row_block(R, C, bytes_per_row_elem=4, budget=1 << 20):
    if R % 8:
        return R
    best = 8
    t = 8
    while t <= R:
        if R % t == 0 and t * C * bytes_per_row_elem <= budget:
            best = t
        t += 8
    return best


def _adamw(w, g, m, v, name):
    R, C = w.shape
    tr = _row_block(R, C, budget=2 << 20)
    c1 = 1.0 - ADAM_B1 ** ADAM_STEP
    c2 = 1.0 - ADAM_B2 ** ADAM_STEP

    def body(w_ref, g_ref, m_ref, v_ref, d_ref, nm_ref, nv_ref):
        gv = g_ref[...]
        nm = ADAM_B1 * m_ref[...] + (1.0 - ADAM_B1) * gv
        nv = ADAM_B2 * v_ref[...] + (1.0 - ADAM_B2) * (gv * gv)
        d_ref[...] = -ADAM_LR * ((nm / c1) / (jnp.sqrt(nv / c2) + ADAM_EPS) + ADAM_WD * w_ref[...])
        nm_ref[...] = nm
        nv_ref[...] = nv

    blk = pl.BlockSpec((tr, C), lambda i: (i, 0))
    return pl.pallas_call(
        body,
        grid=(R // tr,),
        in_specs=[blk] * 4,
        out_specs=[blk] * 3,
        out_shape=[jax.ShapeDtypeStruct((R, C), F32)] * 3,
        compiler_params=_cparams("parallel"),
        name=name,
    )(w, g, m, v)


def _sum_pieces(grad, axis, recv, pos, name):
    n, pr, pc = recv.shape
    tr = _row_block(pr, pc, bytes_per_row_elem=(n + 1) * recv.dtype.itemsize, budget=4 << 20)
    nblk = pr // tr
    if axis == 1:
        own_map = lambda i, p: (p[1] * nblk + i, p[0])
    else:
        own_map = lambda i, p: ((2 * p[0] + p[1]) * nblk + i, 0)

    def body(p_ref, own_ref, r_ref, o_ref):
        acc = own_ref[...].astype(F32)
        for s in range(n):
            acc = acc + r_ref[s].astype(F32)
        o_ref[...] = acc

    return pl.pallas_call(
        body,
        grid_spec=pltpu.PrefetchScalarGridSpec(
            num_scalar_prefetch=1,
            grid=(nblk,),
            in_specs=[pl.BlockSpec((tr, pc), own_map), pl.BlockSpec((n, tr, pc), lambda i, p: (0, i, 0))],
            out_specs=pl.BlockSpec((tr, pc), lambda i, p: (p[1] * nblk + i, 0)),
        ),
        out_shape=jax.ShapeDtypeStruct((2 * pr, pc), F32),
        compiler_params=_cparams("parallel"),
        name=name,
    )(pos, grad, recv)


def _sum_small(own, recv, me, name):
    n, R, C = recv.shape
    tr = _row_block(R, C, bytes_per_row_elem=(n + 1) * 4, budget=4 << 20)

    def body(me_ref, own_ref, r_ref, o_ref):
        acc = None
        for dev in range(n + 1):
            k = jnp.bitwise_xor(me_ref[0], dev)
            term = jnp.where(k == 0, own_ref[...], r_ref[jnp.maximum(k - 1, 0)])
            acc = term if acc is None else acc + term
        o_ref[...] = acc

    return pl.pallas_call(
        body,
        grid_spec=pltpu.PrefetchScalarGridSpec(
            num_scalar_prefetch=1,
            grid=(R // tr,),
            in_specs=[pl.BlockSpec((tr, C), lambda i, m: (i, 0)), pl.BlockSpec((n, tr, C), lambda i, m: (0, i, 0))],
            out_specs=pl.BlockSpec((tr, C), lambda i, m: (i, 0)),
        ),
        out_shape=jax.ShapeDtypeStruct((R, C), F32),
        compiler_params=_cparams("parallel"),
        name=name,
    )(me, own, recv)


def _place(shard, axis, pos, dtype, name, after=None):
    extra = [] if after is None else [after]
    shp = list(shard.shape)
    shp[axis] *= N_CHIPS
    if shard.ndim == 3:
        assert axis == 1
        in_spec = pl.BlockSpec(shard.shape, lambda i, p: (0, 0, 0))
        out_spec = pl.BlockSpec(shard.shape, lambda i, p: (0, p[0], 0))
        grid = (1,)
    else:
        R, C = shard.shape
        tr = _row_block(R, C, bytes_per_row_elem=4, budget=2 << 20)
        nblk = R // tr
        in_spec = pl.BlockSpec((tr, C), lambda i, p: (i, 0))
        out_spec = pl.BlockSpec((tr, C), (lambda i, p: (i, p[0])) if axis == 1 else (lambda i, p: (p[0] * nblk + i, 0)))
        grid = (nblk,)

    def body(*refs):
        s_ref, o_ref = refs[1], refs[-1]
        o_ref[...] = s_ref[...].astype(o_ref.dtype)

    return pl.pallas_call(
        body,
        grid_spec=pltpu.PrefetchScalarGridSpec(num_scalar_prefetch=1, grid=grid, in_specs=[in_spec] + [ANY] * len(extra), out_specs=out_spec),
        out_shape=jax.ShapeDtypeStruct(tuple(shp), dtype),
        compiler_params=_cparams("parallel"),
        name=name,
    )(pos, shard, *extra)


HBM = pl.BlockSpec(memory_space=pltpu.HBM)
SEM = pl.BlockSpec(memory_space=pltpu.SEMAPHORE)
DATAFLOW = pltpu.SideEffectType.DATAFLOW_SIDE_EFFECTING


def _position():
    return lax.axis_index("x"), lax.axis_index("y"), lax.axis_index("c")


def _peer(k):
    x, y, c = _position()
    return ((1 - x) if k & 4 else x, (1 - y) if k & 2 else y, (1 - c) if k & 1 else c)


def _shard_slice(ref, axis, idx, size):
    start = idx * size
    if axis == ref.ndim - 1:
        start = pl.multiple_of(start, 128)
    ix = [slice(None)] * ref.ndim
    ix[axis] = pl.ds(start, size)
    return ref.at[tuple(ix)]


def _gather_plan(axes):
    def plan(refs):
        x, y, c = _position()
        out = []
        for ref, ax in zip(refs, axes):
            mine = _shard_slice(ref, ax, 2 * x + y, ref.shape[ax] // N_CHIPS)
            for k in (4, 2, 6):
                px, py, _ = _peer(k)
                out.append((mine, mine, (px, py, c)))
        return out
    return plan


def _scatter_plan(axes):
    m = len(axes)

    def plan(refs):
        out = []
        for t in range(m):
            grad, recv = refs[t], refs[m + t]
            _, pr, pc = recv.shape
            for k in range(1, N_DEV):
                px, py, pcore = _peer(k)
                if axes[t] == 0:
                    piece = grad.at[pl.ds(((2 * px + py) * 2 + pcore) * pr, pr), :]
                else:
                    piece = grad.at[pl.ds(pcore * pr, pr), pl.ds(pl.multiple_of((2 * px + py) * pc, 128), pc)]
                out.append((piece, recv.at[k - 1], (px, py, pcore)))
        return out
    return plan


def _broadcast_plan(refs):
    small, recv = refs
    return [(small, recv.at[k - 1], _peer(k)) for k in range(1, N_DEV)]


def _start_all(plan, refs, send_sems, recv_sems):
    for q, (src, dst, dev) in enumerate(plan(refs)):
        pltpu.make_async_remote_copy(src_ref=src, dst_ref=dst, send_sem=send_sems.at[q], recv_sem=recv_sems.at[q], device_id=dev, device_id_type=MESH).start()


def _wait_all(plan, refs, send_sems, recv_sems):
    for q, (src, dst, dev) in enumerate(plan(refs)):
        cp = pltpu.make_async_remote_copy(src_ref=src, dst_ref=dst, send_sem=send_sems.at[q], recv_sem=recv_sems.at[q], device_id=dev, device_id_type=MESH)
        cp.wait_send()
        cp.wait_recv()


def _push(bufs, plan, ncopies, name):
    n = len(bufs)

    def body(*refs):
        outs = refs[n:2 * n]
        send_sems, recv_sems = refs[2 * n:]
        _start_all(plan, outs, send_sems, recv_sems)
        _wait_all(plan, outs, send_sems, recv_sems)

    return pl.pallas_call(
        body,
        in_specs=[ANY] * n,
        out_specs=[ANY] * n,
        out_shape=[jax.ShapeDtypeStruct(b.shape, b.dtype) for b in bufs],
        input_output_aliases={t: t for t in range(n)},
        scratch_shapes=[pltpu.SemaphoreType.DMA((ncopies,)), pltpu.SemaphoreType.DMA((ncopies,))],
        name=name,
    )(*bufs)


def _half_slices_plan(onward):
    def plan(refs):
        ref, = refs
        x, y, c = _position()
        R2, C4 = ref.shape[0] // 2, ref.shape[1] // N_CHIPS
        out = []
        for k in (4, 2, 6):
            px, py, _ = _peer(k)
            chip = (2 * px + py) if onward else (2 * x + y)
            half = ref.at[pl.ds(c * R2, R2), pl.ds(pl.multiple_of(chip * C4, 128), C4)]
            out.append((half, half, (x, y, 1 - c) if onward else (px, py, c)))
        return out
    return plan


def _push_start(bufs, plan, ncopies, name, after=None):
    n = len(bufs)
    extra = [] if after is None else [after]

    def body(*refs):
        ins = refs[:n]
        first_out = n + len(extra)
        send_sems, recv_sems, token = refs[first_out], refs[first_out + 1], refs[-1]
        _start_all(plan, ins, send_sems, recv_sems)
        token[...] = jnp.zeros_like(token)

    res = pl.pallas_call(
        body,
        name=name,
        out_shape=(pltpu.SemaphoreType.DMA((ncopies,)), pltpu.SemaphoreType.DMA((ncopies,)), *[pltpu.HBM(b.shape, b.dtype) for b in bufs],
                   jax.ShapeDtypeStruct((8, 128), F32)),
        in_specs=[HBM] * n + [ANY] * len(extra),
        out_specs=(SEM, SEM, *[HBM] * n, pl.BlockSpec(memory_space=pltpu.VMEM)),
        input_output_aliases={t: t + 2 for t in range(n)},
        compiler_params=pltpu.CompilerParams(has_side_effects=DATAFLOW),
    )(*[pltpu.with_memory_space_constraint(b, pltpu.HBM) for b in bufs], *extra)
    return res[0], res[1], list(res[2:2 + n]), res[-1]


def _push_wait(send_sems, recv_sems, bufs, plan, after, name):
    n = len(bufs)
    after = list(after) if isinstance(after, (list, tuple)) else [after]

    def body(*refs):
        ins = refs[:n]
        _wait_all(plan, ins, refs[n], refs[n + 1])

    return pl.pallas_call(
        body,
        name=name,
        out_shape=tuple(pltpu.HBM(b.shape, b.dtype) for b in bufs),
        in_specs=[HBM] * n + [SEM, SEM] + [ANY] * len(after),
        out_specs=tuple([HBM] * n),
        input_output_aliases={t: t for t in range(n)},
        compiler_params=pltpu.CompilerParams(has_side_effects=DATAFLOW),
    )(*bufs, send_sems, recv_sems, *after)


EXCHANGE_CHUNKS = 2


def _exchange_plan(refs):
    x, y, c = _position()
    out = []
    for ref in refs:
        rows = ref.shape[0] // (2 * EXCHANGE_CHUNKS)
        for q in range(EXCHANGE_CHUNKS):
            mine = ref.at[pl.ds((c * EXCHANGE_CHUNKS + q) * rows, rows), :]
            out.append((mine, mine, (x, y, 1 - c)))
    return out


LATE_WEIGHTS = (("w_pool_lin", "w_pool_out", "w_attn_out", "w_out"), ("w_up", "conv_w", "w_down"))


def _local_step(x, tgt, w, late_weights, send):
    S, D = x.shape
    PW = w["pool_scale"].shape[1]
    o_q = PW
    o_g = PW + 3 * ATTN_WIDTH
    QKV = 3 * ATTN_WIDTH

    h1 = _rms_fwd(x, w["g_mix"], "rms1")
    w = dict(w, **late_weights("w_in", h1))
    proj_tiles = (_tile(S, 2048), 512, D)
    started = w.get("late_started")
    u = _mm(h1, w["w_in"], mode="nn", dims=(S, PW, D), tiles=proj_tiles, out_dtypes=(F32,), after=started, name="proj_u")
    qkv = _mm(h1, w["w_in"], mode="nn", dims=(S, QKV, D), tiles=proj_tiles, b_off=(0, o_q), after=started, name="proj_qkv")

    def gate_epilogue(acc, ex, outs):
        outs[0][...] = (1.0 / (1.0 + jnp.exp(-(acc + ex[0][...])))).astype(outs[0].dtype)

    gates = _mm(h1, w["w_in"], mode="nn", dims=(S, 2 * D, D), tiles=proj_tiles, b_off=(0, o_g), epilogue=gate_epilogue,
                extras=[(w["b_gate"], "n", (0, 0))], name="proj_gates")

    os_, lses = [], []
    for gi, (_, d) in enumerate(ATTN_GROUPS):
        o, lse = _attn_fwd(qkv, d, gi, f"attn_fwd{gi}")
        os_.append(o)
        lses.append(lse)
    attn, lse_tot = _attn_merge(os_, lses, "attn_merge")

    w = dict(w, **late_weights(0, attn))
    pool_out = _pool_fwd(u, w["w_pool_lin"], w["pool_scale"], "pool_fwd")
    y_pool = _mm(pool_out, w["w_pool_out"], mode="nn", dims=(S, D, PW), name="y_pool")

    def mix_epilogue(acc, ex, outs):
        outs[0][...] = acc.astype(BF16)
        outs[1][...] = (ex[0][...].astype(F32) * ex[2][...].astype(F32) + ex[1][...].astype(F32) * acc).astype(BF16)

    y_attn, mixed = _mm(attn, w["w_attn_out"], mode="nn", dims=(S, D, GROUP_WIDTH), out_dtypes=(BF16, BF16), epilogue=mix_epilogue,
                        extras=[(gates, "mn", (0, 0)), (gates, "mn", (0, D)), (y_pool, "mn", (0, 0))], name="y_attn_mix")

    def residual_epilogue(acc, ex, outs):
        outs[0][...] = ex[0][...] + acc

    x2 = _mm(mixed, w["w_out"], mode="nn", dims=(S, D, D), out_dtypes=(F32,), epilogue=residual_epilogue, extras=[(x, "mn", (0, 0))], name="out_proj")

    h2 = _rms_fwd(x2, w["g_ffn"], "rms2")
    w = dict(w, **late_weights(1, h2))
    F = w["w_down"].shape[0]
    up = _mm(h2, w["w_up"], mode="nn", dims=(S, 2 * F, D), name="up_proj")
    f, act_a, act_b = _convglu_fwd(up, w["conv_w"], w["conv_b"], "convglu_fwd")
    x3 = _mm(f, w["w_down"], mode="nn", dims=(S, D, F), out_dtypes=(F32,), epilogue=residual_epilogue, extras=[(x2, "mn", (0, 0))], name="down_proj")

    g = {}
    dx3b, g["g_final"], loss_cols = _loss_head(x3, tgt, w["g_final"], "loss_head")

    g["w_down"] = _mm(f, dx3b, mode="tn", dims=(F, D, S), name="dw_down")
    sent = send(("w_down",), g)
    df = _mm(dx3b, w["w_down"], mode="nt", dims=(S, F, D), name="d_f")
    dup, g["conv_b"], g["conv_w"] = _convglu_bwd(df, act_a, act_b, up, w["conv_w"] + sent, "convglu_bwd")
    g["w_up"] = _mm(h2, dup, mode="tn", dims=(D, 2 * F, S), name="dw_up")
    sent = send(("w_up",), g)
    dh2 = _mm(dup, w["w_up"], mode="nt", dims=(S, D, 2 * F), name="d_h2")
    dx2b, g["g_ffn"] = _rms_bwd(dh2, x2, w["g_ffn"] + sent, dx3b, "rms2_bwd", BF16)

    g["w_out"] = _mm(mixed, dx2b, mode="tn", dims=(D, D, S), name="dw_out")
    dmixed = _mm(dx2b, w["w_out"], mode="nt", dims=(S, D, D), name="d_mixed")
    IN = w["w_in"].shape[1]
    dy_both, dproj, g["b_gate"] = _gate_bwd(dmixed, gates, y_pool, y_attn, IN, "gate_bwd")

    g["w_pool_out"] = _mm(pool_out, dy_both, mode="tn", dims=(PW, D, S), name="dw_pool_out")
    g["w_attn_out"] = _mm(attn, dy_both, mode="tn", dims=(GROUP_WIDTH, D, S), b_off=(0, D), name="dw_attn_out")
    sent = send(("w_out", "w_pool_out", "w_attn_out"), g)
    dpool = _mm(dy_both, w["w_pool_out"], mode="nt", dims=(S, PW, D), name="d_pool")
    dattn = _mm(dy_both, w["w_attn_out"], mode="nt", dims=(S, GROUP_WIDTH, D), a_off=(0, D), name="d_attn")

    dproj, g["w_pool_lin"], g["pool_scale"] = _pool_bwd(u, dpool, w["w_pool_lin"], w["pool_scale"] + sent, dproj, "pool_bwd")
    g["loss_cols"] = loss_cols
    sent = send("small", g)

    for gi, (_, d) in enumerate(ATTN_GROUPS):
        dproj = _attn_bwd(qkv, dattn, attn, lse_tot, dproj, d, gi, PW // HEAD_DIM, f"attn_bwd{gi}")

    g["w_in"] = _mm(h1, dproj, mode="tn", dims=(D, IN, S), name="dw_in")
    sent = sent + send(("w_in",), g)
    dh1 = _mm(dproj, w["w_in"], mode="nt", dims=(S, D, IN), tiles=(_tile(S, 1024), _tile(D, 2048), _tile(IN, 2432)), name="d_h1")
    (grad_x, g["g_mix"]) = _rms_bwd(dh1, x, w["g_mix"] + sent, dx2b, "rms1_bwd", F32)
    return loss_cols, grad_x, g


BIG = ("w_in", "w_pool_out", "w_attn_out", "w_out", "w_up", "w_down")
BIG_AXIS = {"w_in": 1, "w_pool_out": 1, "w_attn_out": 1, "w_out": 0, "w_up": 1, "w_down": 0}
GATHER_AXIS = dict(BIG_AXIS, w_pool_lin=1, conv_w=1)
SMALL = ("loss_cols", "b_gate", "w_pool_lin", "pool_scale", "g_ffn", "conv_w", "conv_b", "g_final")
SMALL_COLS = 1024
ORDER = ("g_mix", "w_in", "b_gate", "w_pool_lin", "pool_scale", "w_pool_out", "w_attn_out", "w_out", "g_ffn", "w_up", "conv_w", "conv_b", "w_down", "g_final")


def _as_rows(parts):
    flat = jnp.concatenate([p.astype(F32).reshape(-1) for p in parts])
    rows = -(-flat.shape[0] // (8 * SMALL_COLS)) * 8
    return jnp.pad(flat, (0, rows * SMALL_COLS - flat.shape[0])).reshape(rows, SMALL_COLS)


def kernel(x, g_mix, w_in, b_gate, w_pool_lin, pool_scale, w_pool_out, w_attn_out, w_out, g_ffn, w_up, conv_w, conv_b, w_down, g_final, loss_target, m_g_mix, m_w_in, m_b_gate, m_w_pool_lin, m_pool_scale, m_w_pool_out, m_w_attn_out, m_w_out, m_g_ffn, m_w_up, m_conv_w, m_conv_b, m_w_down, m_g_final, v_g_mix, v_w_in, v_b_gate, v_w_pool_lin, v_pool_scale, v_w_pool_out, v_w_attn_out, v_w_out, v_g_ffn, v_w_up, v_conv_w, v_conv_b, v_w_down, v_g_final):
    shard = dict(g_mix=g_mix, w_in=w_in, b_gate=b_gate, w_pool_lin=w_pool_lin, pool_scale=pool_scale, w_pool_out=w_pool_out, w_attn_out=w_attn_out,
                 w_out=w_out, g_ffn=g_ffn, w_up=w_up, conv_w=conv_w, conv_b=conv_b, w_down=w_down, g_final=g_final)
    mom = dict(g_mix=m_g_mix, w_in=m_w_in, b_gate=m_b_gate, w_pool_lin=m_w_pool_lin, pool_scale=m_pool_scale, w_pool_out=m_w_pool_out, w_attn_out=m_w_attn_out,
               w_out=m_w_out, g_ffn=m_g_ffn, w_up=m_w_up, conv_w=m_conv_w, conv_b=m_conv_b, w_down=m_w_down, g_final=m_g_final)
    vel = dict(g_mix=v_g_mix, w_in=v_w_in, b_gate=v_b_gate, w_pool_lin=v_w_pool_lin, pool_scale=v_pool_scale, w_pool_out=v_w_pool_out, w_attn_out=v_w_attn_out,
               w_out=v_w_out, g_ffn=v_g_ffn, w_up=v_w_up, conv_w=v_conv_w, conv_b=v_conv_b, w_down=v_w_down, g_final=v_g_final)
    chip = 2 * lax.axis_index("x") + lax.axis_index("y")
    pos = jnp.stack([chip, lax.axis_index("c")]).astype(jnp.int32)
    me = (2 * chip + lax.axis_index("c")).astype(jnp.int32).reshape(1)
    D = x.shape[2]

    out_plan, on_plan = _half_slices_plan(False), _half_slices_plan(True)
    in_send, in_recv, in_bufs, in_token = _push_start([_place(shard["w_in"][0], GATHER_AXIS["w_in"], pos, BF16, "place_w_in")], out_plan, 3,
                                                      "comm_gather_w_in_start")
    placed = {k: _place(shard[k][0], GATHER_AXIS[k], pos, F32 if k == "conv_w" else BF16, f"place_{k}", after=in_token)
              for names in LATE_WEIGHTS for k in names}
    late = []

    def late_weights(stage, after):
        if stage == "w_in":
            landed = _push_wait(in_send, in_recv, in_bufs, out_plan, [after] + list(placed.values()), "comm_gather_w_in_wait")
            w_in_full, = _push(list(landed), on_plan, 3, "comm_gather_w_in_pass")
            late_token, prior = 0.0, w_in_full
            for st, names in enumerate(LATE_WEIGHTS):
                plan = _gather_plan([GATHER_AXIS[k] for k in names])
                send_sems, recv_sems, bufs, token = _push_start([placed[k] for k in names], plan, 3 * len(names), f"comm_gather_late{st}_start", after=prior)
                late.append((names, send_sems, recv_sems, bufs, plan))
                late_token, prior = late_token + token[0, 0], token
            return dict(w_in=w_in_full, b_gate=shard["b_gate"] + late_token, late_started=prior)
        names, send_sems, recv_sems, bufs, plan = late[stage]
        return dict(zip(names, _push_wait(send_sems, recv_sems, bufs, plan, after, f"comm_gather_late{stage}_wait")))

    pending = []

    def send(names, g):
        if names == "small":
            bufs = [_as_rows([g[k] for k in SMALL])]
            bufs.append(lax.empty((N_DEV - 1,) + bufs[0].shape, F32))
            plan, tag = _broadcast_plan, "small"
        else:
            bufs = [g[k] for k in names]
            for k in names:
                R, C = g[k].shape
                piece = (R // (2 * N_CHIPS), C) if BIG_AXIS[k] == 0 else (R // 2, C // N_CHIPS)
                bufs.append(lax.empty((N_DEV - 1,) + piece, BF16))
            plan, tag = _scatter_plan([BIG_AXIS[k] for k in names]), names[0]
        ncopies = (N_DEV - 1) * (len(bufs) // 2)
        send_sems, recv_sems, thru, token = _push_start(bufs, plan, ncopies, f"comm_scatter_start_{tag}")
        pending.append((names, send_sems, recv_sems, thru, plan, tag))
        return token[0, 0]

    w0 = dict(g_mix=shard["g_mix"] + in_token[0, 0], pool_scale=shard["pool_scale"], g_ffn=shard["g_ffn"],
              conv_b=shard["conv_b"], g_final=shard["g_final"].reshape(1, D))
    _, grad_x, gr = _local_step(x[0], loss_target[0], w0, late_weights, send)

    halves, small_parts = {}, None
    for names, send_sems, recv_sems, thru, plan, tag in pending:
        done = _push_wait(send_sems, recv_sems, thru, plan, grad_x, f"comm_scatter_wait_{tag}")
        if names == "small":
            small_parts = _sum_small(done[0], done[1], me, "sum_small").reshape(-1)
        else:
            m = len(names)
            for t, k in enumerate(names):
                halves[k] = _sum_pieces(done[t], BIG_AXIS[k], done[m + t], pos, f"sum_{k}")
    g_mix_own = _as_rows([gr["g_mix"]])
    _, g_mix_recv = _push([g_mix_own, lax.empty((N_DEV - 1,) + g_mix_own.shape, F32)], _broadcast_plan, N_DEV - 1, "comm_gather_g_mix")
    g_mix_sum = _sum_small(g_mix_own, g_mix_recv, me, "sum_g_mix").reshape(-1)[:D]
    wholes = _push([halves[k] for k in BIG], _exchange_plan, EXCHANGE_CHUNKS * len(BIG), "comm_exchange_halves")

    grads = {"g_mix": g_mix_sum.reshape(shard["g_mix"].shape)}
    for k, whole in zip(BIG, wholes):
        grads[k] = whole.reshape(shard[k].shape)
    off = 0
    loss = None
    for k in SMALL:
        sz = math.prod(gr[k].shape)
        fullg = small_parts[off:off + sz].reshape(gr[k].shape)
        off += sz
        if k == "loss_cols":
            loss = jnp.sum(fullg)
            continue
        if k in ("w_pool_lin", "conv_w"):
            n = shard[k].shape[2]
            fullg = lax.dynamic_slice_in_dim(fullg, chip * n, n, axis=1)
        grads[k] = fullg.reshape(shard[k].shape)

    deltas, new_m, new_v = {}, {}, {}
    for k in ORDER:
        shp = shard[k].shape
        two_d = (-1, shp[-1])
        dl, nm, nv = _adamw(shard[k].reshape(two_d), grads[k].reshape(two_d), mom[k].reshape(two_d), vel[k].reshape(two_d), f"adamw_{k}")
        deltas[k], new_m[k], new_v[k] = dl.reshape(shp), nm.reshape(shp), nv.reshape(shp)

    return (loss, grad_x[None], *[grads[k] for k in ORDER], *[deltas[k] for k in ORDER], *[new_m[k] for k in ORDER], *[new_v[k] for k in ORDER])
```

```python
import functools
import math

import jax
import jax.numpy as jnp
from jax import lax
from jax.experimental import pallas as pl
from jax.experimental.pallas import tpu as pltpu

F32 = jnp.float32
BF16 = jnp.bfloat16

RMS_EPS = 1e-6
POOL_WINDOWS = (2, 4, 8, 16)
ATTN_GROUPS = ((128, 1), (512, 4), (2048, 16))
HEADS_PER_GROUP = 4
HEAD_DIM = 128
N_ATTN_HEADS = HEADS_PER_GROUP * len(ATTN_GROUPS)
SPAN = 128
GROUP_WIDTH = HEADS_PER_GROUP * HEAD_DIM
ATTN_WIDTH = N_ATTN_HEADS * HEAD_DIM
ATTN_SCALE = HEAD_DIM ** -0.5
NEG_BIG = -1e30
ALIBI_SLOPES = tuple(2.0 ** (-8.0 * (h + 1) / N_ATTN_HEADS) for h in range(N_ATTN_HEADS))

ADAM_LR = 0.001
ADAM_B1 = 0.9
ADAM_B2 = 0.999
ADAM_EPS = 1e-08
ADAM_WD = 0.01
ADAM_STEP = 10

INV_SQRT2 = 1.0 / math.sqrt(2.0)
INV_SQRT_2PI = 1.0 / math.sqrt(2.0 * math.pi)

HALO = 16
VMEM_LIMIT = 56 * 1024 * 1024
N_CHIPS = 4
N_DEV = 8
MESH = pl.DeviceIdType.MESH
ANY = pl.BlockSpec(memory_space=pl.ANY)


def _cparams(*sem):
    return pltpu.CompilerParams(dimension_semantics=sem, vmem_limit_bytes=VMEM_LIMIT)


def _tile(n, pref, mult=128):
    t = (min(pref, n) // mult) * mult
    while t >= mult:
        if n % t == 0:
            return t
        t -= mult
    return n


def _dot(a, b, contract):
    return lax.dot_general(a, b, (contract, ((), ())), preferred_element_type=F32)


def _dot_nn(a, b):
    return _dot(a, b, ((1,), (0,)))


def _dot_nt(a, b):
    return _dot(a, b, ((1,), (1,)))


def _mm(a, b, *, mode, dims, name, tiles=None, out_dtypes=(BF16,), epilogue=None, extras=(), a_off=(0, 0), b_off=(0, 0), after=None):
    M, N, K = dims
    if tiles is None:
        tiles = (_tile(M, 1408), _tile(N, 2816), _tile(K, 512)) if mode == "tn" else (_tile(M, 1024), _tile(N, 1536), _tile(K, 2816))
    tm, tn, tk = tiles
    assert M % tm == 0 and N % tn == 0 and K % tk == 0, (name, dims, tiles)
    nk = K // tk
    if mode == "nn":
        ab, bb, contract = (tm, tk), (tk, tn), ((1,), (0,))
        amap = lambda i, j, k: (i + a_off[0] // tm, k + a_off[1] // tk)
        bmap = lambda i, j, k: (k + b_off[0] // tk, j + b_off[1] // tn)
    elif mode == "nt":
        ab, bb, contract = (tm, tk), (tn, tk), ((1,), (1,))
        amap = lambda i, j, k: (i + a_off[0] // tm, k + a_off[1] // tk)
        bmap = lambda i, j, k: (j + b_off[0] // tn, k + b_off[1] // tk)
    else:
        ab, bb, contract = (tk, tm), (tk, tn), ((0,), (0,))
        amap = lambda i, j, k: (k + a_off[0] // tk, i + a_off[1] // tm)
        bmap = lambda i, j, k: (k + b_off[0] // tk, j + b_off[1] // tn)
    assert a_off[0] % ab[0] == 0 and a_off[1] % ab[1] == 0 and b_off[0] % bb[0] == 0 and b_off[1] % bb[1] == 0, name
    in_specs = [pl.BlockSpec(ab, amap), pl.BlockSpec(bb, bmap)]
    ex_arrays = []
    for arr, kind, off in extras:
        if kind == "mn":
            assert off[0] % tm == 0 and off[1] % tn == 0, name
            in_specs.append(pl.BlockSpec((tm, tn), lambda i, j, k, off=off: (i + off[0] // tm, j + off[1] // tn)))
        else:
            assert off[1] % tn == 0, name
            in_specs.append(pl.BlockSpec((1, tn), lambda i, j, k, off=off: (0, j + off[1] // tn)))
        ex_arrays.append(arr)
    ne, no = len(ex_arrays), len(out_dtypes)
    if after is not None:
        in_specs.append(ANY)
        ex_arrays.append(after)
    first_out = 2 + len(ex_arrays)
    if epilogue is None:
        def epilogue(acc, ex, outs):
            outs[0][...] = acc.astype(outs[0].dtype)

    def body(*refs):
        a_ref, b_ref = refs[0], refs[1]
        ex, outs = refs[2:2 + ne], refs[first_out:first_out + no]
        if nk == 1:
            epilogue(_dot(a_ref[...], b_ref[...], contract), ex, outs)
            return
        acc = refs[-1]
        k = pl.program_id(2)
        if nk <= 4:
            part = _dot(a_ref[...], b_ref[...], contract)

            @pl.when(k == 0)
            def _():
                acc[...] = part

            @pl.when(jnp.logical_and(k > 0, k < nk - 1))
            def _():
                acc[...] += part

            @pl.when(k == nk - 1)
            def _():
                epilogue(acc[...] + part, ex, outs)
        else:
            @pl.when(k == 0)
            def _():
                acc[...] = _dot(a_ref[...], b_ref[...], contract)

            @pl.when(k > 0)
            def _():
                acc[...] += _dot(a_ref[...], b_ref[...], contract)

            @pl.when(k == nk - 1)
            def _():
                epilogue(acc[...], ex, outs)

    res = pl.pallas_call(
        body,
        grid=(M // tm, N // tn, nk),
        in_specs=in_specs,
        out_specs=[pl.BlockSpec((tm, tn), lambda i, j, k: (i, j)) for _ in out_dtypes],
        out_shape=[jax.ShapeDtypeStruct((M, N), dt) for dt in out_dtypes],
        scratch_shapes=[pltpu.VMEM((tm, tn), F32)] if nk > 1 else [],
        compiler_params=_cparams("parallel", "parallel", "arbitrary"),
        name=name,
    )(a, b, *ex_arrays)
    return res[0] if no == 1 else res


def _rms_fwd(x, g, name):
    S, D = x.shape
    tm = _tile(S, 512)

    def body(x_ref, g_ref, h_ref):
        xv = x_ref[...]
        r = lax.rsqrt(jnp.mean(xv * xv, axis=-1, keepdims=True) + RMS_EPS)
        h_ref[...] = (xv * r * g_ref[...]).astype(h_ref.dtype)

    return pl.pallas_call(
        body,
        grid=(S // tm,),
        in_specs=[pl.BlockSpec((tm, D), lambda i: (i, 0)), pl.BlockSpec((1, D), lambda i: (0, 0))],
        out_specs=pl.BlockSpec((tm, D), lambda i: (i, 0)),
        out_shape=jax.ShapeDtypeStruct((S, D), BF16),
        compiler_params=_cparams("parallel"),
        name=name,
    )(x, g)


def _rms_bwd(dh, x, g, dres, name, out_dtype):
    S, D = x.shape
    tm = _tile(S, 512)

    def body(dh_ref, x_ref, g_ref, dres_ref, dx_ref, dg_ref):
        xv = x_ref[...]
        r = lax.rsqrt(jnp.mean(xv * xv, axis=-1, keepdims=True) + RMS_EPS)
        xr = xv * r
        dhv = dh_ref[...].astype(F32)

        @pl.when(pl.program_id(0) == 0)
        def _():
            dg_ref[...] = jnp.zeros_like(dg_ref)

        dg_ref[...] += jnp.sum(dhv * xr, axis=0, keepdims=True)
        u = dhv * g_ref[...]
        c = jnp.mean(u * xr, axis=-1, keepdims=True)
        dx_ref[...] = (dres_ref[...].astype(F32) + r * (u - xr * c)).astype(dx_ref.dtype)

    row = pl.BlockSpec((tm, D), lambda i: (i, 0))
    vec = pl.BlockSpec((1, D), lambda i: (0, 0))
    return pl.pallas_call(
        body,
        grid=(S // tm,),
        in_specs=[row, row, vec, row],
        out_specs=[row, vec],
        out_shape=[jax.ShapeDtypeStruct((S, D), out_dtype), jax.ShapeDtypeStruct((1, D), F32)],
        compiler_params=_cparams("arbitrary"),
        name=name,
    )(dh, x, g, dres)


def _loss_head(x3, tgt, g, name):
    S, D = x3.shape
    tm = _tile(S, 512)

    def body(x_ref, t_ref, g_ref, dxb_ref, dg_ref, loss_ref):
        xv = x_ref[...]
        gv = g_ref[...]
        r = lax.rsqrt(jnp.mean(xv * xv, axis=-1, keepdims=True) + RMS_EPS)
        xr = xv * r
        e = xr * gv - t_ref[...]

        @pl.when(pl.program_id(0) == 0)
        def _():
            dg_ref[...] = jnp.zeros_like(dg_ref)
            loss_ref[...] = jnp.zeros_like(loss_ref)

        loss_ref[...] += jnp.sum(e * e, axis=0, keepdims=True) * (0.5 / D)
        dy = e * (1.0 / D)
        dg_ref[...] += jnp.sum(dy * xr, axis=0, keepdims=True)
        u = dy * gv
        c = jnp.mean(u * xr, axis=-1, keepdims=True)
        dxb_ref[...] = (r * (u - xr * c)).astype(BF16)

    row = pl.BlockSpec((tm, D), lambda i: (i, 0))
    vec = pl.BlockSpec((1, D), lambda i: (0, 0))
    return pl.pallas_call(
        body,
        grid=(S // tm,),
        in_specs=[row, row, vec],
        out_specs=[row, vec, vec],
        out_shape=[jax.ShapeDtypeStruct((S, D), BF16), jax.ShapeDtypeStruct((1, D), F32), jax.ShapeDtypeStruct((1, D), F32)],
        compiler_params=_cparams("arbitrary"),
        name=name,
    )(x3, tgt, g)


SHIFT_ROWS = 256


def _shift_matrix():
    i = lax.broadcasted_iota(jnp.int32, (2 * SHIFT_ROWS, SHIFT_ROWS), 0)
    j = lax.broadcasted_iota(jnp.int32, (2 * SHIFT_ROWS, SHIFT_ROWS), 1)
    src = jnp.where(i < SHIFT_ROWS, i - 1, i - SHIFT_ROWS - 2)
    return (j == src).astype(BF16)


def _conv_taps(cur_ref, halo_ref, w_ref, b_ref, first, shift):
    w, bias = w_ref[...], b_ref[...]
    before = jnp.where(first, 0.0, halo_ref[...].astype(F32)[HALO - 8:])
    sub = lax.broadcasted_iota(jnp.int32, before.shape, 0)
    out = []
    for blk in range(cur_ref.shape[0] // SHIFT_ROWS):
        xb = cur_ref[blk * SHIFT_ROWS:(blk + 1) * SHIFT_ROWS, :]
        xf = xb.astype(F32)
        both = _dot_nn(shift, xb)
        p1, p2 = both[:SHIFT_ROWS], both[SHIFT_ROWS:]
        p1 = jnp.concatenate([jnp.where(sub == 0, pltpu.roll(before, 1, 0), p1[:8]), p1[8:]], axis=0)
        p2 = jnp.concatenate([jnp.where(sub < 2, pltpu.roll(before, 2, 0), p2[:8]), p2[8:]], axis=0)
        out.append(bias + w[0:1] * p2 + w[1:2] * p1 + w[2:3] * xf)
        before = xf[SHIFT_ROWS - 8:]
    return jnp.concatenate(out, axis=0)


def _convglu_fwd(up, cw, cb, name):
    S, F2 = up.shape
    F = F2 // 2
    tm, tn = _tile(S, 512), _tile(F, 1408)
    nj, hb = F // tn, tm // HALO

    def body(ua, ub, ha, hb_, wa, wb, ba, bb, f_ref, a_ref, b_ref):
        first = pl.program_id(0) == 0
        shift = _shift_matrix()
        a = _conv_taps(ua, ha, wa, ba, first, shift)
        b = _conv_taps(ub, hb_, wb, bb, first, shift)
        f_ref[...] = (0.5 * a * (1.0 + lax.erf(a * INV_SQRT2)) * b).astype(f_ref.dtype)
        a_ref[...] = a.astype(a_ref.dtype)
        b_ref[...] = b.astype(b_ref.dtype)

    tile = pl.BlockSpec((tm, tn), lambda i, j: (i, j))
    return pl.pallas_call(
        body,
        grid=(S // tm, nj),
        in_specs=[
            tile,
            pl.BlockSpec((tm, tn), lambda i, j: (i, j + nj)),
            pl.BlockSpec((HALO, tn), lambda i, j: (jnp.maximum(i * hb - 1, 0), j)),
            pl.BlockSpec((HALO, tn), lambda i, j: (jnp.maximum(i * hb - 1, 0), j + nj)),
            pl.BlockSpec((3, tn), lambda i, j: (0, j)),
            pl.BlockSpec((3, tn), lambda i, j: (0, j + nj)),
            pl.BlockSpec((1, tn), lambda i, j: (0, j)),
            pl.BlockSpec((1, tn), lambda i, j: (0, j + nj)),
        ],
        out_specs=[tile, tile, tile],
        out_shape=[jax.ShapeDtypeStruct((S, F), BF16)] * 3,
        compiler_params=_cparams("parallel", "parallel"),
        name=name,
    )(up, up, up, up, cw, cw, cb, cb)


def _convglu_bwd(df, a, b, up, cw, name):
    S, F = df.shape
    tm, tn = _tile(S, 512), _tile(F, 1408)
    nj, ni, hb = F // tn, S // tm, tm // HALO
    n = tm + HALO

    def body(df_ref, dfn_ref, a_ref, an_ref, b_ref, bn_ref, up_ref, w_ref, o_ref, db_ref, dw_ref):
        j, i = pl.program_id(0), pl.program_id(1)
        last = i == ni - 1

        def rows(c_ref, n_ref):
            return jnp.concatenate([c_ref[...].astype(F32), jnp.where(last, 0.0, n_ref[...].astype(F32))], axis=0)

        @pl.when(i == 0)
        def _():
            db_ref[...] = jnp.zeros_like(db_ref)
            dw_ref[...] = jnp.zeros_like(dw_ref)

        def finish(d):
            i_ = lax.broadcasted_iota(jnp.int32, (2 * SHIFT_ROWS, SHIFT_ROWS), 0)
            j_ = lax.broadcasted_iota(jnp.int32, (2 * SHIFT_ROWS, SHIFT_ROWS), 1)
            ahead = (j_ == jnp.where(i_ < SHIFT_ROWS, i_ + 1, i_ - SHIFT_ROWS + 2)).astype(BF16)
            db = d.astype(BF16)
            sub = lax.broadcasted_iota(jnp.int32, (8, tn), 0)
            d1, d2 = [], []
            for blk in range(tm // SHIFT_ROWS):
                lo, hi = blk * SHIFT_ROWS, (blk + 1) * SHIFT_ROWS
                both = _dot_nn(ahead, db[lo:hi])
                n1, n2 = both[:SHIFT_ROWS], both[SHIFT_ROWS:]
                after = db[hi:hi + HALO].astype(F32)[:8]
                d1 += [n1[:-8], jnp.where(sub == 7, pltpu.roll(after, 7, 0), n1[-8:])]
                d2 += [n2[:-8], jnp.where(sub >= 6, pltpu.roll(after, 6, 0), n2[-8:])]
            d0, d1, d2 = d[:tm], jnp.concatenate(d1, axis=0), jnp.concatenate(d2, axis=0)
            w = w_ref[...]
            o_ref[...] = (w[2:3] * d0 + w[1:2] * d1 + w[0:1] * d2).astype(o_ref.dtype)
            upv = up_ref[...].astype(F32)
            db_ref[...] += jnp.sum(d0, axis=0, keepdims=True)
            dw_ref[0:1, :] += jnp.sum(d2 * upv, axis=0, keepdims=True)
            dw_ref[1:2, :] += jnp.sum(d1 * upv, axis=0, keepdims=True)
            dw_ref[2:3, :] += jnp.sum(d0 * upv, axis=0, keepdims=True)

        av, dfv = rows(a_ref, an_ref), rows(df_ref, dfn_ref)
        cdf = 0.5 * (1.0 + lax.erf(av * INV_SQRT2))

        @pl.when(j < nj)
        def _():
            pdf = jnp.exp(-0.5 * av * av) * INV_SQRT_2PI
            finish(dfv * rows(b_ref, bn_ref) * (cdf + av * pdf))

        @pl.when(j >= nj)
        def _():
            finish(dfv * (av * cdf))

    jh = lambda j: lax.rem(j, nj)
    nxt = lambda i: jnp.minimum((i + 1) * hb, S // HALO - 1)
    cur = pl.BlockSpec((tm, tn), lambda j, i: (i, jh(j)))
    halo = pl.BlockSpec((HALO, tn), lambda j, i: (nxt(i), jh(j)))
    return pl.pallas_call(
        body,
        grid=(2 * nj, ni),
        in_specs=[cur, halo, cur, halo, cur, halo, pl.BlockSpec((tm, tn), lambda j, i: (i, j)), pl.BlockSpec((3, tn), lambda j, i: (0, j))],
        out_specs=[pl.BlockSpec((tm, tn), lambda j, i: (i, j)), pl.BlockSpec((1, tn), lambda j, i: (0, j)), pl.BlockSpec((3, tn), lambda j, i: (0, j))],
        out_shape=[jax.ShapeDtypeStruct((S, 2 * F), BF16), jax.ShapeDtypeStruct((1, 2 * F), F32), jax.ShapeDtypeStruct((3, 2 * F), F32)],
        compiler_params=_cparams("parallel", "arbitrary"),
        name=name,
    )(df, df, a, a, b, b, up, cw)


def _gate_bwd(dmixed, gates, y_pool, y_attn, in_width, name):
    S, D = dmixed.shape
    tm, tn = _tile(S, 2048), _tile(D, 512)
    nj = D // tn
    pre0 = (in_width - 2 * D) // tn
    assert pre0 * tn == in_width - 2 * D

    def body(dm_ref, g_ref, yp_ref, ya_ref, dy_ref, dpre_ref, db_ref):
        j = pl.program_id(0)

        @pl.when(pl.program_id(1) == 0)
        def _():
            db_ref[...] = jnp.zeros_like(db_ref)

        def run(y_ref):
            dm = dm_ref[...].astype(F32)
            gv = g_ref[...].astype(F32)
            dy_ref[...] = (dm * gv).astype(BF16)
            dpre = dm * y_ref[...].astype(F32) * gv * (1.0 - gv)
            dpre_ref[...] = dpre.astype(BF16)
            db_ref[...] += jnp.sum(dpre, axis=0, keepdims=True)

        @pl.when(j < nj)
        def _():
            run(yp_ref)

        @pl.when(j >= nj)
        def _():
            run(ya_ref)

    tile2 = pl.BlockSpec((tm, tn), lambda j, i: (i, j))
    return pl.pallas_call(
        body,
        grid=(2 * nj, S // tm),
        in_specs=[
            pl.BlockSpec((tm, tn), lambda j, i: (i, lax.rem(j, nj))),
            tile2,
            pl.BlockSpec((tm, tn), lambda j, i: (i, jnp.minimum(j, nj - 1))),
            pl.BlockSpec((tm, tn), lambda j, i: (i, jnp.maximum(j - nj, 0))),
        ],
        out_specs=[tile2, pl.BlockSpec((tm, tn), lambda j, i: (i, pre0 + j)), pl.BlockSpec((1, tn), lambda j, i: (0, j))],
        out_shape=[jax.ShapeDtypeStruct((S, 2 * D), BF16), jax.ShapeDtypeStruct((S, in_width), BF16), jax.ShapeDtypeStruct((1, 2 * D), F32)],
        compiler_params=_cparams("parallel", "arbitrary"),
        name=name,
    )(dmixed, gates, y_pool, y_attn)


def _pool_counts(i, tm, rows, w):
    t = i * tm + lax.broadcasted_iota(jnp.int32, (rows, 1), 0)
    return jnp.minimum(t + 1, w).astype(F32)


def _pooled_groups(u_ref, uh_ref, i, tm, C):
    cur = u_ref[...]
    halo = jnp.where(i == 0, 0.0, uh_ref[...])
    xx = jnp.concatenate([halo, cur], axis=0)
    out = []
    s = xx
    for gi, w in enumerate(POOL_WINDOWS):
        s = s + pltpu.roll(s, w // 2, 0)
        tot = s[HALO:, 0:C]
        out.append(tot / _pool_counts(i, tm, tm, w) - cur[:, gi * C:(gi + 1) * C])
        s = s[:, C:] if gi + 1 < len(POOL_WINDOWS) else s
    return out


def _pool_fwd(u, wl, scale, name):
    S, PW = u.shape
    C = PW // len(POOL_WINDOWS)
    tm = _tile(S, 512)
    hb = tm // HALO

    def body(u_ref, uh_ref, wl_ref, sc_ref, o_ref):
        i = pl.program_id(0)
        pooled = _pooled_groups(u_ref, uh_ref, i, tm, C)
        for gi in range(len(POOL_WINDOWS)):
            y = _dot_nn(pooled[gi].astype(BF16), wl_ref[gi])
            o_ref[:, gi * C:(gi + 1) * C] = (y * sc_ref[:, gi * C:(gi + 1) * C]).astype(o_ref.dtype)

    return pl.pallas_call(
        body,
        grid=(S // tm,),
        in_specs=[
            pl.BlockSpec((tm, PW), lambda i: (i, 0)),
            pl.BlockSpec((HALO, PW), lambda i: (jnp.maximum(i * hb - 1, 0), 0)),
            pl.BlockSpec((len(POOL_WINDOWS), C, C), lambda i: (0, 0, 0)),
            pl.BlockSpec((1, PW), lambda i: (0, 0)),
        ],
        out_specs=pl.BlockSpec((tm, PW), lambda i: (i, 0)),
        out_shape=jax.ShapeDtypeStruct((S, PW), BF16),
        compiler_params=_cparams("parallel"),
        name=name,
    )(u, u, wl, scale)


def _pool_bwd(u, dp, wl, scale, dproj, name):
    S, PW = u.shape
    G = len(POOL_WINDOWS)
    C = PW // G
    tm = _tile(S, 512)
    hb, ni = tm // HALO, S // tm
    n = tm + HALO

    def body(u_ref, uh_ref, dp_ref, dpn_ref, wl_ref, sc_ref, _, du_ref, dwl_ref, dsc_ref):
        i = pl.program_id(0)

        @pl.when(i == 0)
        def _():
            dwl_ref[...] = jnp.zeros_like(dwl_ref)
            dsc_ref[...] = jnp.zeros_like(dsc_ref)

        pooled = _pooled_groups(u_ref, uh_ref, i, tm, C)
        dpc = dp_ref[...].astype(F32)
        dpn = jnp.where(i == ni - 1, 0.0, dpn_ref[...].astype(F32))
        sc = sc_ref[...]
        dyl = jnp.concatenate([dpc, dpn], axis=0) * sc
        for gi, w in enumerate(POOL_WINDOWS):
            cols = slice(gi * C, (gi + 1) * C)
            pb = pooled[gi].astype(BF16)
            ylin = _dot_nn(pb, wl_ref[gi])
            dsc_ref[:, cols] += jnp.sum(dpc[:, cols] * ylin, axis=0, keepdims=True)
            dylg = dyl[:, cols].astype(BF16)
            dwl_ref[gi] += _dot(pb, dylg[:tm], ((0,), (0,)))
            dpool = _dot_nt(dylg, wl_ref[gi])
            e = dpool / _pool_counts(i, tm, n, w)
            k = 1
            while k < w:
                e = e + pltpu.roll(e, n - k, 0)
                k *= 2
            du_ref[:, cols] = (e[:tm] - dpool[:tm]).astype(du_ref.dtype)

    return pl.pallas_call(
        body,
        grid=(ni,),
        in_specs=[
            pl.BlockSpec((tm, PW), lambda i: (i, 0)),
            pl.BlockSpec((HALO, PW), lambda i: (jnp.maximum(i * hb - 1, 0), 0)),
            pl.BlockSpec((tm, PW), lambda i: (i, 0)),
            pl.BlockSpec((HALO, PW), lambda i: (jnp.minimum((i + 1) * hb, S // HALO - 1), 0)),
            pl.BlockSpec((G, C, C), lambda i: (0, 0, 0)),
            pl.BlockSpec((1, PW), lambda i: (0, 0)),
            ANY,
        ],
        out_specs=[pl.BlockSpec((tm, PW), lambda i: (i, 0)), pl.BlockSpec((G, C, C), lambda i: (0, 0, 0)), pl.BlockSpec((1, PW), lambda i: (0, 0))],
        out_shape=[jax.ShapeDtypeStruct(dproj.shape, dproj.dtype), jax.ShapeDtypeStruct((G, C, C), F32), jax.ShapeDtypeStruct((1, PW), F32)],
        input_output_aliases={6: 0},
        compiler_params=_cparams("arbitrary"),
        name=name,
    )(u, u, dp, dp, wl, scale, dproj)


def _band_masks():
    ii = lax.broadcasted_iota(jnp.int32, (SPAN, SPAN), 0)
    kk = lax.broadcasted_iota(jnp.int32, (SPAN, SPAN), 1)
    return ((ii + SPAN - kk).astype(F32), kk >= ii), ((ii - kk).astype(F32), kk <= ii)


ATTN_TILE = 32 * SPAN


def _unit_rows(r, b, d, blocks=1):
    return pl.ds(d * SPAN * b + r, blocks * SPAN, stride=d) if d > 1 else pl.ds(SPAN * b, blocks * SPAN)


def _f32_copies(refs, scratch, d):
    if d == 1:
        return list(refs)
    for ref, s in zip(refs, scratch):
        s[...] = ref[...].astype(F32)
    return list(scratch)


def _attn_fwd(qkv, d, g, name):
    S = qkv.shape[0]
    T = min(ATTN_TILE, S)
    P = SPAN * d
    nbk = T // P

    def body(q_ref, k_ref, v_ref, kp_ref, vp_ref, o_ref, lse_ref, *scratch):
        c = pl.program_id(0)
        (jp, mp), (jc, mc) = _band_masks()
        slopes = [ALIBI_SLOPES[g * HEADS_PER_GROUP + h] * d for h in range(HEADS_PER_GROUP)]
        slope = slopes[0]
        for h in range(1, HEADS_PER_GROUP):
            slope = jnp.where(pl.program_id(1) == h, slopes[h], slope)
        q_s, k_s, v_s, kp_s, vp_s = _f32_copies((q_ref, k_ref, v_ref, kp_ref, vp_ref), scratch[:5], d)
        o_s, l_s = (o_ref, lse_ref) if d == 1 else scratch[5:7]
        bias_p, bias_c = jnp.where(mp, -slope * jp, NEG_BIG), jnp.where(mc, -slope * jc, NEG_BIG)
        bias = jnp.concatenate([bias_p, bias_c], axis=1)
        bias_first = jnp.concatenate([jnp.where(c > 0, bias_p, NEG_BIG), bias_c], axis=1)
        for r in range(d):
            for b in range(nbk):
                rows = _unit_rows(r, b, d)
                q = q_s[rows, :].astype(BF16)
                if b == 0:
                    prev = _unit_rows(r, 0, d)
                    kk = jnp.concatenate([kp_s[prev, :], k_s[rows, :]], axis=0).astype(BF16)
                    vv = jnp.concatenate([vp_s[prev, :], v_s[rows, :]], axis=0).astype(BF16)
                else:
                    both = _unit_rows(r, b - 1, d, 2)
                    kk, vv = k_s[both, :].astype(BF16), v_s[both, :].astype(BF16)
                s = _dot_nt(q, kk) * ATTN_SCALE + (bias_first if b == 0 else bias)
                m = jnp.max(s, axis=-1, keepdims=True)
                p = jnp.exp(s - m)
                l = jnp.sum(p, axis=-1, keepdims=True)
                o_s[rows, :] = _dot_nn(p.astype(BF16), vv) / l
                l_s[rows, :] = jnp.broadcast_to(m + jnp.log(l), (SPAN, HEAD_DIM))
        if d > 1:
            o_ref[...] = o_s[...]
            lse_ref[...] = l_s[...]

    col = lambda kind: (lambda c, h: (c, kind * N_ATTN_HEADS + g * HEADS_PER_GROUP + h))
    pcol = lambda kind: (lambda c, h: (jnp.maximum(c * nbk - 1, 0), kind * N_ATTN_HEADS + g * HEADS_PER_GROUP + h))
    cur = lambda kind: pl.BlockSpec((T, HEAD_DIM), col(kind))
    prv = lambda kind: pl.BlockSpec((P, HEAD_DIM), pcol(kind))
    out = pl.BlockSpec((T, HEAD_DIM), lambda c, h: (c, h))
    scratch = [] if d == 1 else [pltpu.VMEM((T, HEAD_DIM), F32)] * 3 + [pltpu.VMEM((P, HEAD_DIM), F32)] * 2 + [pltpu.VMEM((T, HEAD_DIM), F32)] * 2
    return pl.pallas_call(
        body,
        grid=(S // T, HEADS_PER_GROUP),
        in_specs=[cur(0), cur(1), cur(2), prv(1), prv(2)],
        out_specs=[out, out],
        out_shape=[jax.ShapeDtypeStruct((S, GROUP_WIDTH), F32)] * 2,
        scratch_shapes=scratch,
        compiler_params=_cparams("parallel", "parallel"),
        name=name,
    )(qkv, qkv, qkv, qkv, qkv)


def _attn_merge(os_, lses, name):
    S, W = os_[0].shape
    tm = _tile(S, 1024)

    def body(o0, o1, o2, l0, l1, l2, y_ref, lse_ref):
        ls = [l0[...], l1[...], l2[...]]
        m = jnp.maximum(jnp.maximum(ls[0], ls[1]), ls[2])
        es = [jnp.exp(v - m) for v in ls]
        tot = es[0] + es[1] + es[2]
        y = (es[0] * o0[...] + es[1] * o1[...] + es[2] * o2[...]) / tot
        y_ref[...] = y.astype(y_ref.dtype)
        lse_ref[...] = m + jnp.log(tot)

    row = pl.BlockSpec((tm, W), lambda i: (i, 0))
    return pl.pallas_call(
        body,
        grid=(S // tm,),
        in_specs=[row] * 6,
        out_specs=[row, row],
        out_shape=[jax.ShapeDtypeStruct((S, W), BF16), jax.ShapeDtypeStruct((S, W), F32)],
        compiler_params=_cparams("parallel"),
        name=name,
    )(*os_, *lses)


def _attn_bwd(qkv, dattn, y, lse, dproj, d, g, col0, name):
    S = qkv.shape[0]
    T = min(ATTN_TILE, S)
    P = SPAN * d
    nbk = T // P
    ntile = S // T

    def body(q_ref, k_ref, v_ref, kp_ref, vp_ref, qn_ref, da_ref, dan_ref, y_ref, yn_ref, lse_ref, lsen_ref, _, out_ref, dq_s, dk_s, dv_s, *scratch):
        c = pl.program_id(0)
        head_id = pl.program_id(1)
        kind = pl.program_id(2)

        @pl.when(kind == 0)
        def _():
            (jp, mp), (jc, mc) = _band_masks()
            slopes = [ALIBI_SLOPES[g * HEADS_PER_GROUP + h] * d for h in range(HEADS_PER_GROUP)]
            slope = slopes[0]
            for h in range(1, HEADS_PER_GROUP):
                slope = jnp.where(head_id == h, slopes[h], slope)
            q_s, k_s, v_s, da_s, y_s, kp_s, vp_s, qn_s, dan_s, yn_s = _f32_copies(
                (q_ref, k_ref, v_ref, da_ref, y_ref, kp_ref, vp_ref, qn_ref, dan_ref, yn_ref), scratch, d)
            bias_p, bias_c = jnp.where(mp, -slope * jp, NEG_BIG), jnp.where(mc, -slope * jc, NEG_BIG)
            bias = jnp.concatenate([bias_c, bias_p], axis=0)
            bias_last = jnp.concatenate([bias_c, jnp.where(c < ntile - 1, bias_p, NEG_BIG)], axis=0)
            bias_first = jnp.where(c > 0, bias_p, NEG_BIG)

            def pair(q, da, yy, lse_blk, kk, vv, b):
                dd = jnp.sum(da.astype(F32) * yy.astype(F32), axis=-1, keepdims=True)
                p = jnp.exp(_dot_nt(q, kk) * ATTN_SCALE + b - lse_blk[:, 0:1])
                return p, p * (_dot_nt(da, vv) - dd)

            for r in range(d):
                first = _unit_rows(r, 0, d)
                kk, vv = kp_s[first, :].astype(BF16), vp_s[first, :].astype(BF16)
                _, ds = pair(q_s[first, :].astype(BF16), da_s[first, :].astype(BF16), y_s[first, :], lse_ref[first, :], kk, vv, bias_first)
                dq_next = _dot_nn(ds.astype(BF16), kk)
                for kb in range(nbk):
                    rows = _unit_rows(r, kb, d)
                    if kb + 1 < nbk:
                        both = _unit_rows(r, kb, d, 2)
                        q, da, yy, lse_blk = q_s[both, :], da_s[both, :], y_s[both, :], lse_ref[both, :]
                    else:
                        q = jnp.concatenate([q_s[rows, :], qn_s[first, :]], axis=0)
                        da = jnp.concatenate([da_s[rows, :], dan_s[first, :]], axis=0)
                        yy = jnp.concatenate([y_s[rows, :], yn_s[first, :]], axis=0)
                        lse_blk = jnp.concatenate([lse_ref[rows, :], lsen_ref[first, :]], axis=0)
                    q, da = q.astype(BF16), da.astype(BF16)
                    kk, vv = k_s[rows, :].astype(BF16), v_s[rows, :].astype(BF16)
                    p, ds = pair(q, da, yy, lse_blk, kk, vv, bias if kb + 1 < nbk else bias_last)
                    dv_s[rows, :] = _dot_nn(p.T.astype(BF16), da)
                    dk_s[rows, :] = _dot_nn(ds.T.astype(BF16), q) * ATTN_SCALE
                    dq_both = _dot_nn(ds.astype(BF16), kk)
                    dq_s[rows, :] = (dq_next + dq_both[:SPAN]) * ATTN_SCALE
                    dq_next = dq_both[SPAN:]
            out_ref[...] = dq_s[...].astype(out_ref.dtype)

        @pl.when(kind == 1)
        def _():
            out_ref[...] = dk_s[...].astype(out_ref.dtype)

        @pl.when(kind == 2)
        def _():
            out_ref[...] = dv_s[...].astype(out_ref.dtype)

    head = lambda h: g * HEADS_PER_GROUP + h
    cur = lambda kind: pl.BlockSpec((T, HEAD_DIM), lambda c, h, kd: (c, kind * N_ATTN_HEADS + head(h)))
    prv = lambda kind: pl.BlockSpec((P, HEAD_DIM), lambda c, h, kd: (jnp.maximum(c * nbk - 1, 0), kind * N_ATTN_HEADS + head(h)))
    nxt_row = lambda c: jnp.minimum((c + 1) * nbk, S // P - 1)
    qnext = pl.BlockSpec((P, HEAD_DIM), lambda c, h, kd: (nxt_row(c), head(h)))
    hcur = pl.BlockSpec((T, HEAD_DIM), lambda c, h, kd: (c, h))
    hnext = pl.BlockSpec((P, HEAD_DIM), lambda c, h, kd: (nxt_row(c), h))
    out = pl.BlockSpec((T, HEAD_DIM), lambda c, h, kd: (c, col0 + kd * N_ATTN_HEADS + head(h)))
    stage = [pltpu.VMEM((T, HEAD_DIM), F32)] * 3
    copies = [] if d == 1 else [pltpu.VMEM((T, HEAD_DIM), F32)] * 5 + [pltpu.VMEM((P, HEAD_DIM), F32)] * 5
    return pl.pallas_call(
        body,
        grid=(ntile, HEADS_PER_GROUP, 3),
        in_specs=[cur(0), cur(1), cur(2), prv(1), prv(2), qnext, hcur, hnext, hcur, hnext, hcur, hnext, ANY],
        out_specs=out,
        out_shape=jax.ShapeDtypeStruct(dproj.shape, dproj.dtype),
        input_output_aliases={12: 0},
        scratch_shapes=stage + copies,
        compiler_params=_cparams("parallel", "parallel", "arbitrary"),
        name=name,
    )(qkv, qkv, qkv, qkv, qkv, qkv, dattn, dattn, y, y, lse, lse, dproj)


def _row_block(R, C, bytes_per_row_elem=4, budget=1 << 20):
    if R % 8:
        return R
    best = 8
    t = 8
    while t <= R:
        if R % t == 0 and t * C * bytes_per_row_elem <= budget:
            best = t
        t += 8
    return best


def _adamw(w, g, m, v, name):
    R, C = w.shape
    tr = _row_block(R, C, budget=2 << 20)
    c1 = 1.0 - ADAM_B1 ** ADAM_STEP
    c2 = 1.0 - ADAM_B2 ** ADAM_STEP

    def body(w_ref, g_ref, m_ref, v_ref, d_ref, nm_ref, nv_ref):
        gv = g_ref[...]
        nm = ADAM_B1 * m_ref[...] + (1.0 - ADAM_B1) * gv
        nv = ADAM_B2 * v_ref[...] + (1.0 - ADAM_B2) * (gv * gv)
        d_ref[...] = -ADAM_LR * ((nm / c1) / (jnp.sqrt(nv / c2) + ADAM_EPS) + ADAM_WD * w_ref[...])
        nm_ref[...] = nm
        nv_ref[...] = nv

    blk = pl.BlockSpec((tr, C), lambda i: (i, 0))
    return pl.pallas_call(
        body,
        grid=(R // tr,),
        in_specs=[blk] * 4,
        out_specs=[blk] * 3,
        out_shape=[jax.ShapeDtypeStruct((R, C), F32)] * 3,
        compiler_params=_cparams("parallel"),
        name=name,
    )(w, g, m, v)


def _sum_pieces(grad, axis, recv, pos, name):
    n, pr, pc = recv.shape
    tr = _row_block(pr, pc, bytes_per_row_elem=(n + 1) * recv.dtype.itemsize, budget=4 << 20)
    nblk = pr // tr
    if axis == 1:
        own_map = lambda i, p: (p[1] * nblk + i, p[0])
    else:
        own_map = lambda i, p: ((2 * p[0] + p[1]) * nblk + i, 0)

    def body(p_ref, own_ref, r_ref, o_ref):
        acc = own_ref[...].astype(F32)
        for s in range(n):
            acc = acc + r_ref[s].astype(F32)
        o_ref[...] = acc

    return pl.pallas_call(
        body,
        grid_spec=pltpu.PrefetchScalarGridSpec(
            num_scalar_prefetch=1,
            grid=(nblk,),
            in_specs=[pl.BlockSpec((tr, pc), own_map), pl.BlockSpec((n, tr, pc), lambda i, p: (0, i, 0))],
            out_specs=pl.BlockSpec((tr, pc), lambda i, p: (p[1] * nblk + i, 0)),
        ),
        out_shape=jax.ShapeDtypeStruct((2 * pr, pc), F32),
        compiler_params=_cparams("parallel"),
        name=name,
    )(pos, grad, recv)


def _sum_small(own, recv, me, name):
    n, R, C = recv.shape
    tr = _row_block(R, C, bytes_per_row_elem=(n + 1) * 4, budget=4 << 20)

    def body(me_ref, own_ref, r_ref, o_ref):
        acc = None
        for dev in range(n + 1):
            k = jnp.bitwise_xor(me_ref[0], dev)
            term = jnp.where(k == 0, own_ref[...], r_ref[jnp.maximum(k - 1, 0)])
            acc = term if acc is None else acc + term
        o_ref[...] = acc

    return pl.pallas_call(
        body,
        grid_spec=pltpu.PrefetchScalarGridSpec(
            num_scalar_prefetch=1,
            grid=(R // tr,),
            in_specs=[pl.BlockSpec((tr, C), lambda i, m: (i, 0)), pl.BlockSpec((n, tr, C), lambda i, m: (0, i, 0))],
            out_specs=pl.BlockSpec((tr, C), lambda i, m: (i, 0)),
        ),
        out_shape=jax.ShapeDtypeStruct((R, C), F32),
        compiler_params=_cparams("parallel"),
        name=name,
    )(me, own, recv)


def _place(shard, axis, pos, dtype, name, after=None):
    extra = [] if after is None else [after]
    shp = list(shard.shape)
    shp[axis] *= N_CHIPS
    if shard.ndim == 3:
        assert axis == 1
        in_spec = pl.BlockSpec(shard.shape, lambda i, p: (0, 0, 0))
        out_spec = pl.BlockSpec(shard.shape, lambda i, p: (0, p[0], 0))
        grid = (1,)
    else:
        R, C = shard.shape
        tr = _row_block(R, C, bytes_per_row_elem=4, budget=2 << 20)
        nblk = R // tr
        in_spec = pl.BlockSpec((tr, C), lambda i, p: (i, 0))
        out_spec = pl.BlockSpec((tr, C), (lambda i, p: (i, p[0])) if axis == 1 else (lambda i, p: (p[0] * nblk + i, 0)))
        grid = (nblk,)

    def body(*refs):
        s_ref, o_ref = refs[1], refs[-1]
        o_ref[...] = s_ref[...].astype(o_ref.dtype)

    return pl.pallas_call(
        body,
        grid_spec=pltpu.PrefetchScalarGridSpec(num_scalar_prefetch=1, grid=grid, in_specs=[in_spec] + [ANY] * len(extra), out_specs=out_spec),
        out_shape=jax.ShapeDtypeStruct(tuple(shp), dtype),
        compiler_params=_cparams("parallel"),
        name=name,
    )(pos, shard, *extra)


HBM = pl.BlockSpec(memory_space=pltpu.HBM)
SEM = pl.BlockSpec(memory_space=pltpu.SEMAPHORE)
DATAFLOW = pltpu.SideEffectType.DATAFLOW_SIDE_EFFECTING


def _position():
    return lax.axis_index("x"), lax.axis_index("y"), lax.axis_index("c")


def _peer(k):
    x, y, c = _position()
    return ((1 - x) if k & 4 else x, (1 - y) if k & 2 else y, (1 - c) if k & 1 else c)


def _shard_slice(ref, axis, idx, size):
    start = idx * size
    if axis == ref.ndim - 1:
        start = pl.multiple_of(start, 128)
    ix = [slice(None)] * ref.ndim
    ix[axis] = pl.ds(start, size)
    return ref.at[tuple(ix)]


def _gather_plan(axes):
    def plan(refs):
        x, y, c = _position()
        out = []
        for ref, ax in zip(refs, axes):
            mine = _shard_slice(ref, ax, 2 * x + y, ref.shape[ax] // N_CHIPS)
            for k in (4, 2, 6):
                px, py, _ = _peer(k)
                out.append((mine, mine, (px, py, c)))
        return out
    return plan


def _scatter_plan(axes):
    m = len(axes)

    def plan(refs):
        out = []
        for t in range(m):
            grad, recv = refs[t], refs[m + t]
            _, pr, pc = recv.shape
            for k in range(1, N_DEV):
                px, py, pcore = _peer(k)
                if axes[t] == 0:
                    piece = grad.at[pl.ds(((2 * px + py) * 2 + pcore) * pr, pr), :]
                else:
                    piece = grad.at[pl.ds(pcore * pr, pr), pl.ds(pl.multiple_of((2 * px + py) * pc, 128), pc)]
                out.append((piece, recv.at[k - 1], (px, py, pcore)))
        return out
    return plan


def _broadcast_plan(refs):
    small, recv = refs
    return [(small, recv.at[k - 1], _peer(k)) for k in range(1, N_DEV)]


def _start_all(plan, refs, send_sems, recv_sems):
    for q, (src, dst, dev) in enumerate(plan(refs)):
        pltpu.make_async_remote_copy(src_ref=src, dst_ref=dst, send_sem=send_sems.at[q], recv_sem=recv_sems.at[q], device_id=dev, device_id_type=MESH).start()


def _wait_all(plan, refs, send_sems, recv_sems):
    for q, (src, dst, dev) in enumerate(plan(refs)):
        cp = pltpu.make_async_remote_copy(src_ref=src, dst_ref=dst, send_sem=send_sems.at[q], recv_sem=recv_sems.at[q], device_id=dev, device_id_type=MESH)
        cp.wait_send()
        cp.wait_recv()


def _push(bufs, plan, ncopies, name):
    n = len(bufs)

    def body(*refs):
        outs = refs[n:2 * n]
        send_sems, recv_sems = refs[2 * n:]
        _start_all(plan, outs, send_sems, recv_sems)
        _wait_all(plan, outs, send_sems, recv_sems)

    return pl.pallas_call(
        body,
        in_specs=[ANY] * n,
        out_specs=[ANY] * n,
        out_shape=[jax.ShapeDtypeStruct(b.shape, b.dtype) for b in bufs],
        input_output_aliases={t: t for t in range(n)},
        scratch_shapes=[pltpu.SemaphoreType.DMA((ncopies,)), pltpu.SemaphoreType.DMA((ncopies,))],
        name=name,
    )(*bufs)


def _half_slices_plan(onward):
    def plan(refs):
        ref, = refs
        x, y, c = _position()
        R2, C4 = ref.shape[0] // 2, ref.shape[1] // N_CHIPS
        out = []
        for k in (4, 2, 6):
            px, py, _ = _peer(k)
            chip = (2 * px + py) if onward else (2 * x + y)
            half = ref.at[pl.ds(c * R2, R2), pl.ds(pl.multiple_of(chip * C4, 128), C4)]
            out.append((half, half, (x, y, 1 - c) if onward else (px, py, c)))
        return out
    return plan


def _push_start(bufs, plan, ncopies, name, after=None):
    n = len(bufs)
    extra = [] if after is None else [after]

    def body(*refs):
        ins = refs[:n]
        first_out = n + len(extra)
        send_sems, recv_sems, token = refs[first_out], refs[first_out + 1], refs[-1]
        _start_all(plan, ins, send_sems, recv_sems)
        token[...] = jnp.zeros_like(token)

    res = pl.pallas_call(
        body,
        name=name,
        out_shape=(pltpu.SemaphoreType.DMA((ncopies,)), pltpu.SemaphoreType.DMA((ncopies,)), *[pltpu.HBM(b.shape, b.dtype) for b in bufs],
                   jax.ShapeDtypeStruct((8, 128), F32)),
        in_specs=[HBM] * n + [ANY] * len(extra),
        out_specs=(SEM, SEM, *[HBM] * n, pl.BlockSpec(memory_space=pltpu.VMEM)),
        input_output_aliases={t: t + 2 for t in range(n)},
        compiler_params=pltpu.CompilerParams(has_side_effects=DATAFLOW),
    )(*[pltpu.with_memory_space_constraint(b, pltpu.HBM) for b in bufs], *extra)
    return res[0], res[1], list(res[2:2 + n]), res[-1]


def _push_wait(send_sems, recv_sems, bufs, plan, after, name):
    n = len(bufs)
    after = list(after) if isinstance(after, (list, tuple)) else [after]

    def body(*refs):
        ins = refs[:n]
        _wait_all(plan, ins, refs[n], refs[n + 1])

    return pl.pallas_call(
        body,
        name=name,
        out_shape=tuple(pltpu.HBM(b.shape, b.dtype) for b in bufs),
        in_specs=[HBM] * n + [SEM, SEM] + [ANY] * len(after),
        out_specs=tuple([HBM] * n),
        input_output_aliases={t: t for t in range(n)},
        compiler_params=pltpu.CompilerParams(has_side_effects=DATAFLOW),
    )(*bufs, send_sems, recv_sems, *after)


EXCHANGE_CHUNKS = 2


def _exchange_plan(refs):
    x, y, c = _position()
    out = []
    for ref in refs:
        rows = ref.shape[0] // (2 * EXCHANGE_CHUNKS)
        for q in range(EXCHANGE_CHUNKS):
            mine = ref.at[pl.ds((c * EXCHANGE_CHUNKS + q) * rows, rows), :]
            out.append((mine, mine, (x, y, 1 - c)))
    return out


LATE_WEIGHTS = (("w_pool_lin", "w_pool_out", "w_attn_out", "w_out"), ("w_up", "conv_w", "w_down"))


def _local_step(x, tgt, w, late_weights, send):
    S, D = x.shape
    PW = w["pool_scale"].shape[1]
    o_q = PW
    o_g = PW + 3 * ATTN_WIDTH
    QKV = 3 * ATTN_WIDTH

    h1 = _rms_fwd(x, w["g_mix"], "rms1")
    w = dict(w, **late_weights("w_in", h1))
    proj_tiles = (_tile(S, 2048), 512, D)
    started = w.get("late_started")
    u = _mm(h1, w["w_in"], mode="nn", dims=(S, PW, D), tiles=proj_tiles, out_dtypes=(F32,), after=started, name="proj_u")
    qkv = _mm(h1, w["w_in"], mode="nn", dims=(S, QKV, D), tiles=proj_tiles, b_off=(0, o_q), after=started, name="proj_qkv")

    def gate_epilogue(acc, ex, outs):
        outs[0][...] = (1.0 / (1.0 + jnp.exp(-(acc + ex[0][...])))).astype(outs[0].dtype)

    gates = _mm(h1, w["w_in"], mode="nn", dims=(S, 2 * D, D), tiles=proj_tiles, b_off=(0, o_g), epilogue=gate_epilogue,
                extras=[(w["b_gate"], "n", (0, 0))], name="proj_gates")

    os_, lses = [], []
    for gi, (_, d) in enumerate(ATTN_GROUPS):
        o, lse = _attn_fwd(qkv, d, gi, f"attn_fwd{gi}")
        os_.append(o)
        lses.append(lse)
    attn, lse_tot = _attn_merge(os_, lses, "attn_merge")

    w = dict(w, **late_weights(0, attn))
    pool_out = _pool_fwd(u, w["w_pool_lin"], w["pool_scale"], "pool_fwd")
    y_pool = _mm(pool_out, w["w_pool_out"], mode="nn", dims=(S, D, PW), name="y_pool")

    def mix_epilogue(acc, ex, outs):
        outs[0][...] = acc.astype(BF16)
        outs[1][...] = (ex[0][...].astype(F32) * ex[2][...].astype(F32) + ex[1][...].astype(F32) * acc).astype(BF16)

    y_attn, mixed = _mm(attn, w["w_attn_out"], mode="nn", dims=(S, D, GROUP_WIDTH), out_dtypes=(BF16, BF16), epilogue=mix_epilogue,
                        extras=[(gates, "mn", (0, 0)), (gates, "mn", (0, D)), (y_pool, "mn", (0, 0))], name="y_attn_mix")

    def residual_epilogue(acc, ex, outs):
        outs[0][...] = ex[0][...] + acc

    x2 = _mm(mixed, w["w_out"], mode="nn", dims=(S, D, D), out_dtypes=(F32,), epilogue=residual_epilogue, extras=[(x, "mn", (0, 0))], name="out_proj")

    h2 = _rms_fwd(x2, w["g_ffn"], "rms2")
    w = dict(w, **late_weights(1, h2))
    F = w["w_down"].shape[0]
    wide = (_tile(S, 2048), _tile(F, 1408), D)
    up = _mm(h2, w["w_up"], mode="nn", dims=(S, 2 * F, D), tiles=wide, name="up_proj")
    f, act_a, act_b = _convglu_fwd(up, w["conv_w"], w["conv_b"], "convglu_fwd")
    x3 = _mm(f, w["w_down"], mode="nn", dims=(S, D, F), out_dtypes=(F32,), epilogue=residual_epilogue, extras=[(x2, "mn", (0, 0))], name="down_proj")

    g = {}
    dx3b, g["g_final"], loss_cols = _loss_head(x3, tgt, w["g_final"], "loss_head")

    g["w_down"] = _mm(f, dx3b, mode="tn", dims=(F, D, S), name="dw_down")
    sent = send(("w_down",), g)
    df = _mm(dx3b, w["w_down"], mode="nt", dims=(S, F, D), tiles=wide, name="d_f")
    dup, g["conv_b"], g["conv_w"] = _convglu_bwd(df, act_a, act_b, up, w["conv_w"] + sent, "convglu_bwd")
    g["w_up"] = _mm(h2, dup, mode="tn", dims=(D, 2 * F, S), name="dw_up")
    sent = send(("w_up",), g)
    dh2 = _mm(dup, w["w_up"], mode="nt", dims=(S, D, 2 * F), name="d_h2")
    dx2b, g["g_ffn"] = _rms_bwd(dh2, x2, w["g_ffn"] + sent, dx3b, "rms2_bwd", BF16)

    g["w_out"] = _mm(mixed, dx2b, mode="tn", dims=(D, D, S), name="dw_out")
    dmixed = _mm(dx2b, w["w_out"], mode="nt", dims=(S, D, D), name="d_mixed")
    IN = w["w_in"].shape[1]
    dy_both, dproj, g["b_gate"] = _gate_bwd(dmixed, gates, y_pool, y_attn, IN, "gate_bwd")

    g["w_pool_out"] = _mm(pool_out, dy_both, mode="tn", dims=(PW, D, S), name="dw_pool_out")
    g["w_attn_out"] = _mm(attn, dy_both, mode="tn", dims=(GROUP_WIDTH, D, S), b_off=(0, D), name="dw_attn_out")
    sent = send(("w_out", "w_pool_out", "w_attn_out"), g)
    dpool = _mm(dy_both, w["w_pool_out"], mode="nt", dims=(S, PW, D), name="d_pool")
    dattn = _mm(dy_both, w["w_attn_out"], mode="nt", dims=(S, GROUP_WIDTH, D), a_off=(0, D), name="d_attn")

    dproj, g["w_pool_lin"], g["pool_scale"] = _pool_bwd(u, dpool, w["w_pool_lin"], w["pool_scale"] + sent, dproj, "pool_bwd")
    g["loss_cols"] = loss_cols
    sent = send("small", g)

    for gi, (_, d) in enumerate(ATTN_GROUPS):
        dproj = _attn_bwd(qkv, dattn, attn, lse_tot, dproj, d, gi, PW // HEAD_DIM, f"attn_bwd{gi}")

    g["w_in"] = _mm(h1, dproj, mode="tn", dims=(D, IN, S), name="dw_in")
    sent = sent + send(("w_in",), g)
    dh1 = _mm(dproj, w["w_in"], mode="nt", dims=(S, D, IN), tiles=(_tile(S, 1024), _tile(D, 2048), _tile(IN, 2432)), name="d_h1")
    (grad_x, g["g_mix"]) = _rms_bwd(dh1, x, w["g_mix"] + sent, dx2b, "rms1_bwd", F32)
    return loss_cols, grad_x, g


BIG = ("w_in", "w_pool_out", "w_attn_out", "w_out", "w_up", "w_down")
BIG_AXIS = {"w_in": 1, "w_pool_out": 1, "w_attn_out": 1, "w_out": 0, "w_up": 1, "w_down": 0}
GATHER_AXIS = dict(BIG_AXIS, w_pool_lin=1, conv_w=1)
SMALL = ("loss_cols", "b_gate", "w_pool_lin", "pool_scale", "g_ffn", "conv_w", "conv_b", "g_final")
SMALL_COLS = 1024
ORDER = ("g_mix", "w_in", "b_gate", "w_pool_lin", "pool_scale", "w_pool_out", "w_attn_out", "w_out", "g_ffn", "w_up", "conv_w", "conv_b", "w_down", "g_final")


def _as_rows(parts):
    flat = jnp.concatenate([p.astype(F32).reshape(-1) for p in parts])
    rows = -(-flat.shape[0] // (8 * SMALL_COLS)) * 8
    return jnp.pad(flat, (0, rows * SMALL_COLS - flat.shape[0])).reshape(rows, SMALL_COLS)


def kernel(x, g_mix, w_in, b_gate, w_pool_lin, pool_scale, w_pool_out, w_attn_out, w_out, g_ffn, w_up, conv_w, conv_b, w_down, g_final, loss_target, m_g_mix, m_w_in, m_b_gate, m_w_pool_lin, m_pool_scale, m_w_pool_out, m_w_attn_out, m_w_out, m_g_ffn, m_w_up, m_conv_w, m_conv_b, m_w_down, m_g_final, v_g_mix, v_w_in, v_b_gate, v_w_pool_lin, v_pool_scale, v_w_pool_out, v_w_attn_out, v_w_out, v_g_ffn, v_w_up, v_conv_w, v_conv_b, v_w_down, v_g_final):
    shard = dict(g_mix=g_mix, w_in=w_in, b_gate=b_gate, w_pool_lin=w_pool_lin, pool_scale=pool_scale, w_pool_out=w_pool_out, w_attn_out=w_attn_out,
                 w_out=w_out, g_ffn=g_ffn, w_up=w_up, conv_w=conv_w, conv_b=conv_b, w_down=w_down, g_final=g_final)
    mom = dict(g_mix=m_g_mix, w_in=m_w_in, b_gate=m_b_gate, w_pool_lin=m_w_pool_lin, pool_scale=m_pool_scale, w_pool_out=m_w_pool_out, w_attn_out=m_w_attn_out,
               w_out=m_w_out, g_ffn=m_g_ffn, w_up=m_w_up, conv_w=m_conv_w, conv_b=m_conv_b, w_down=m_w_down, g_final=m_g_final)
    vel = dict(g_mix=v_g_mix, w_in=v_w_in, b_gate=v_b_gate, w_pool_lin=v_w_pool_lin, pool_scale=v_pool_scale, w_pool_out=v_w_pool_out, w_attn_out=v_w_attn_out,
               w_out=v_w_out, g_ffn=v_g_ffn, w_up=v_w_up, conv_w=v_conv_w, conv_b=v_conv_b, w_down=v_w_down, g_final=v_g_final)
    chip = 2 * lax.axis_index("x") + lax.axis_index("y")
    pos = jnp.stack([chip, lax.axis_index("c")]).astype(jnp.int32)
    me = (2 * chip + lax.axis_index("c")).astype(jnp.int32).reshape(1)
    D = x.shape[2]

    out_plan, on_plan = _half_slices_plan(False), _half_slices_plan(True)
    in_send, in_recv, in_bufs, in_token = _push_start([_place(shard["w_in"][0], GATHER_AXIS["w_in"], pos, BF16, "place_w_in")], out_plan, 3,
                                                      "comm_gather_w_in_start")
    placed = {k: _place(shard[k][0], GATHER_AXIS[k], pos, F32 if k == "conv_w" else BF16, f"place_{k}", after=in_token)
              for names in LATE_WEIGHTS for k in names}
    late = []

    def late_weights(stage, after):
        if stage == "w_in":
            landed = _push_wait(in_send, in_recv, in_bufs, out_plan, [after] + list(placed.values()), "comm_gather_w_in_wait")
            w_in_full, = _push(list(landed), on_plan, 3, "comm_gather_w_in_pass")
            late_token, prior = 0.0, w_in_full
            for st, names in enumerate(LATE_WEIGHTS):
                plan = _gather_plan([GATHER_AXIS[k] for k in names])
                send_sems, recv_sems, bufs, token = _push_start([placed[k] for k in names], plan, 3 * len(names), f"comm_gather_late{st}_start", after=prior)
                late.append((names, send_sems, recv_sems, bufs, plan))
                late_token, prior = late_token + token[0, 0], token
            return dict(w_in=w_in_full, b_gate=shard["b_gate"] + late_token, late_started=prior)
        names, send_sems, recv_sems, bufs, plan = late[stage]
        return dict(zip(names, _push_wait(send_sems, recv_sems, bufs, plan, after, f"comm_gather_late{stage}_wait")))

    pending = []

    def send(names, g):
        if names == "small":
            bufs = [_as_rows([g[k] for k in SMALL])]
            bufs.append(lax.empty((N_DEV - 1,) + bufs[0].shape, F32))
            plan, tag = _broadcast_plan, "small"
        else:
            bufs = [g[k] for k in names]
            for k in names:
                R, C = g[k].shape
                piece = (R // (2 * N_CHIPS), C) if BIG_AXIS[k] == 0 else (R // 2, C // N_CHIPS)
                bufs.append(lax.empty((N_DEV - 1,) + piece, BF16))
            plan, tag = _scatter_plan([BIG_AXIS[k] for k in names]), names[0]
        ncopies = (N_DEV - 1) * (len(bufs) // 2)
        send_sems, recv_sems, thru, token = _push_start(bufs, plan, ncopies, f"comm_scatter_start_{tag}")
        pending.append((names, send_sems, recv_sems, thru, plan, tag))
        return token[0, 0]

    w0 = dict(g_mix=shard["g_mix"] + in_token[0, 0], pool_scale=shard["pool_scale"], g_ffn=shard["g_ffn"],
              conv_b=shard["conv_b"], g_final=shard["g_final"].reshape(1, D))
    _, grad_x, gr = _local_step(x[0], loss_target[0], w0, late_weights, send)

    halves, small_parts = {}, None
    for names, send_sems, recv_sems, thru, plan, tag in pending:
        done = _push_wait(send_sems, recv_sems, thru, plan, grad_x, f"comm_scatter_wait_{tag}")
        if names == "small":
            small_parts = _sum_small(done[0], done[1], me, "sum_small").reshape(-1)
        else:
            m = len(names)
            for t, k in enumerate(names):
                halves[k] = _sum_pieces(done[t], BIG_AXIS[k], done[m + t], pos, f"sum_{k}")
    g_mix_own = _as_rows([gr["g_mix"]])
    _, g_mix_recv = _push([g_mix_own, lax.empty((N_DEV - 1,) + g_mix_own.shape, F32)], _broadcast_plan, N_DEV - 1, "comm_gather_g_mix")
    g_mix_sum = _sum_small(g_mix_own, g_mix_recv, me, "sum_g_mix").reshape(-1)[:D]
    wholes = _push([halves[k] for k in BIG], _exchange_plan, EXCHANGE_CHUNKS * len(BIG), "comm_exchange_halves")

    grads = {"g_mix": g_mix_sum.reshape(shard["g_mix"].shape)}
    for k, whole in zip(BIG, wholes):
        grads[k] = whole.reshape(shard[k].shape)
    off = 0
    loss = None
    for k in SMALL:
        sz = math.prod(gr[k].shape)
        fullg = small_parts[off:off + sz].reshape(gr[k].shape)
        off += sz
        if k == "loss_cols":
            loss = jnp.sum(fullg)
            continue
        if k in ("w_pool_lin", "conv_w"):
            n = shard[k].shape[2]
            fullg = lax.dynamic_slice_in_dim(fullg, chip * n, n, axis=1)
        grads[k] = fullg.reshape(shard[k].shape)

    deltas, new_m, new_v = {}, {}, {}
    for k in ORDER:
        shp = shard[k].shape
        two_d = (-1, shp[-1])
        dl, nm, nv = _adamw(shard[k].reshape(two_d), grads[k].reshape(two_d), mom[k].reshape(two_d), vel[k].reshape(two_d), f"adamw_{k}")
        deltas[k], new_m[k], new_v[k] = dl.reshape(shp), nm.reshape(shp), nv.reshape(shp)

    return (loss, grad_x[None], *[grads[k] for k in ORDER], *[deltas[k] for k in ORDER], *[new_m[k] for k in ORDER], *[new_v[k] for k in ORDER])
```
